```python
import jax
import jax.numpy as jnp
from jax import lax
import numpy as np

D_MODEL = 1024
BATCH = 8
SEQ = 2048
DEPTH = 2

HEAD_DIM = 64
MIX_WIDTH = D_MODEL
GROUP_WIDTH = MIX_WIDTH // 2
N_HEADS = GROUP_WIDTH // HEAD_DIM
CONV_WIDTH = 31
MOBA_BLOCK = 256
MOBA_TOPK = 3
MOBA_Q_CHUNK = 32
NSA_KV_HEADS = 2
NSA_KV_WIDTH = NSA_KV_HEADS * HEAD_DIM
NSA_CMP_LEN = 32
NSA_CMP_STRIDE = 16
NSA_CMP_HIDDEN = 256
NSA_SEL_BLOCK = 64
NSA_TOPN = 16
NSA_WINDOW = 512
NSA_Q_CHUNK = 64
NSA_FORCE = 1e4
SWA_KV_HEADS = 2
SWA_KV_WIDTH = SWA_KV_HEADS * HEAD_DIM
SWA_WINDOW = 128
BAND_Q_BLOCK = 128
EPS = 1e-6
NEG = -1e30
TINY = 1e-30

EVEN_SIZES = (GROUP_WIDTH, GROUP_WIDTH, GROUP_WIDTH, GROUP_WIDTH, GROUP_WIDTH, GROUP_WIDTH, GROUP_WIDTH)
ODD_SIZES = (GROUP_WIDTH, NSA_KV_WIDTH, NSA_KV_WIDTH, NSA_KV_WIDTH, NSA_KV_WIDTH, NSA_KV_WIDTH,
             NSA_KV_WIDTH, 3 * N_HEADS, GROUP_WIDTH, GROUP_WIDTH, SWA_KV_WIDTH, SWA_KV_WIDTH, GROUP_WIDTH)
EVEN_IN = sum(EVEN_SIZES)
ODD_IN = sum(ODD_SIZES)

kernel_name = 'hybrid_conv_moba_nsa_swa_sink_block'


def _split(t, sizes):
    points = np.cumsum(sizes)[:-1].tolist()
    return jnp.split(t, points, axis=-1)


def _rmsnorm(x, g):
    xf = x.astype(jnp.float32)
    y = xf * lax.rsqrt(jnp.mean(xf * xf, axis=-1, keepdims=True) + EPS)
    return (y * g.astype(jnp.float32)).astype(x.dtype)


def _layernorm(x, g, b):
    xf = x.astype(jnp.float32)
    mu = jnp.mean(xf, axis=-1, keepdims=True)
    xc = xf - mu
    y = xc * lax.rsqrt(jnp.mean(xc * xc, axis=-1, keepdims=True) + EPS)
    return (y * g.astype(jnp.float32) + b.astype(jnp.float32)).astype(x.dtype)


def _split_heads(t, n):
    b, s, _ = t.shape
    return t.reshape(b, s, n, HEAD_DIM).transpose(0, 2, 1, 3)


def _merge_heads(t):
    b, n, s, d = t.shape
    return t.transpose(0, 2, 1, 3).reshape(b, s, n * d)


def _alibi_slopes(n):
    return jnp.asarray(np.power(2.0, -8.0 * np.arange(1, n + 1) / n).astype(np.float32))


def _masked_softmax(s, mask, sink=None):
    s = jnp.where(mask, s, NEG)
    m = jnp.max(s, axis=-1, keepdims=True)
    if sink is not None:
        m = jnp.maximum(m, sink)
    e = jnp.where(mask, jnp.exp(s - m), 0.0)
    den = jnp.sum(e, axis=-1, keepdims=True)
    if sink is not None:
        den = den + jnp.exp(sink - m)
    return e / jnp.maximum(den, TINY)


def _conformer_conv(u_val, u_gate, w_dw, b_dw, ln_g, ln_b):
    c = u_val.shape[-1]
    h = u_val * jax.nn.sigmoid(u_gate)
    h = jnp.pad(h, ((0, 0), (CONV_WIDTH - 1, 0), (0, 0)))
    y = lax.conv_general_dilated(h, w_dw[:, None, :].astype(h.dtype), (1,), 'VALID',
                                 dimension_numbers=('NWC', 'WIO', 'NWC'), feature_group_count=c)
    y = _layernorm(y + b_dw.astype(y.dtype), ln_g, ln_b)
    return jax.nn.silu(y)


def _moba_attention(q, k, v, slopes):
    b, h, s, d = q.shape
    nblk = -(-s // MOBA_BLOCK)
    sp = nblk * MOBA_BLOCK
    padw = ((0, 0), (0, 0), (0, sp - s), (0, 0))
    q, k, v = jnp.pad(q, padw), jnp.pad(k, padw), jnp.pad(v, padw)
    kb = k.reshape(b, h, nblk, MOBA_BLOCK, d)
    vb = v.reshape(b, h, nblk, MOBA_BLOCK, d)
    kmean = jnp.mean(kb.astype(jnp.float32), axis=3)
    n_sel = min(MOBA_TOPK, nblk)
    n_g = n_sel * MOBA_BLOCK
    scale = d ** -0.5
    bi = jnp.arange(b)[:, None, None, None]
    hi = jnp.arange(h)[None, :, None, None]
    offs = jnp.arange(MOBA_BLOCK)
    blk_ids = jnp.arange(nblk)
    sl = slopes[:, None, None]

    def chunk(c):
        t0 = c * MOBA_Q_CHUNK
        own = t0 // MOBA_BLOCK
        qc = lax.dynamic_slice_in_dim(q, t0, MOBA_Q_CHUNK, axis=2)
        tp = t0 + jnp.arange(MOBA_Q_CHUNK)
        gate = jnp.einsum('bhtd,bhnd->bhtn', qc.astype(jnp.float32), kmean)
        gate = jnp.where(blk_ids < own, gate, NEG)
        _, idx = lax.top_k(gate, n_sel)
        ks = kb[bi, hi, idx]
        vs = vb[bi, hi, idx]
        kpos = idx[..., None] * MOBA_BLOCK + offs
        dist = (tp[:, None, None] - kpos).astype(jnp.float32)
        s_sel = jnp.einsum('bhtd,bhtnld->bhtnl', qc, ks).astype(jnp.float32) * scale - sl[..., None] * dist
        m_sel = jnp.broadcast_to((idx < own)[..., None], s_sel.shape)
        k_own = lax.dynamic_index_in_dim(kb, own, axis=2, keepdims=False)
        v_own = lax.dynamic_index_in_dim(vb, own, axis=2, keepdims=False)
        kpos_own = own * MOBA_BLOCK + offs
        dist_own = (tp[:, None] - kpos_own[None, :]).astype(jnp.float32)
        s_own = jnp.einsum('bhtd,bhld->bhtl', qc, k_own).astype(jnp.float32) * scale - sl * dist_own
        m_own = jnp.broadcast_to(dist_own >= 0, s_own.shape)
        scores = jnp.concatenate([s_sel.reshape(b, h, MOBA_Q_CHUNK, n_g), s_own], axis=-1)
        mask = jnp.concatenate([m_sel.reshape(b, h, MOBA_Q_CHUNK, n_g), m_own], axis=-1)
        p = _masked_softmax(scores, mask).astype(v.dtype)
        out = jnp.einsum('bhtm,bhtmd->bhtd', p[..., :n_g], vs.reshape(b, h, MOBA_Q_CHUNK, n_g, d))
        return out + jnp.einsum('bhtl,bhld->bhtd', p[..., n_g:], v_own)

    outs = lax.map(chunk, jnp.arange(sp // MOBA_Q_CHUNK))
    outs = jnp.moveaxis(outs, 0, 2).reshape(b, h, sp, d)
    return outs[:, :, :s]


def _banded_attention(q, k, v, sl, window, sink=None):
    b, g, r, s, d = q.shape
    nq = s // BAND_Q_BLOCK
    span = window + BAND_Q_BLOCK
    padw = ((0, 0), (0, 0), (window, 0), (0, 0))
    kp, vp = jnp.pad(k, padw), jnp.pad(v, padw)
    kidx = jnp.arange(nq)[:, None] * BAND_Q_BLOCK + jnp.arange(span)[None, :]
    kband = kp[:, :, kidx]
    vband = vp[:, :, kidx]
    qb = q.reshape(b, g, r, nq, BAND_Q_BLOCK, d)
    tpos = jnp.arange(s).reshape(nq, BAND_Q_BLOCK)
    kpos = kidx - window
    dist = tpos[:, :, None] - kpos[:, None, :]
    mask = (dist >= 0) & (dist < window) & (kpos[:, None, :] >= 0)
    scores = (jnp.einsum('bgrnqd,bgnkd->bgrnqk', qb, kband).astype(jnp.float32) * d ** -0.5
              - sl[:, :, None, None, None] * dist.astype(jnp.float32))
    sk = None if sink is None else sink.astype(jnp.float32)[:, :, None, None, None]
    p = _masked_softmax(scores, mask, sk)
    out = jnp.einsum('bgrnqk,bgnkd->bgrnqd', p.astype(v.dtype), vband)
    return out.reshape(b, g, r, s, d)


def _compress(x, pos, w1, b1, w2, b2):
    s = x.shape[2]
    n_cmp = (s - NSA_CMP_LEN) // NSA_CMP_STRIDE + 1
    idx = jnp.arange(n_cmp)[:, None] * NSA_CMP_STRIDE + jnp.arange(NSA_CMP_LEN)[None, :]
    blocks = x[:, :, idx] + pos.astype(x.dtype)
    flat = blocks.reshape(blocks.shape[0], blocks.shape[1], n_cmp, NSA_CMP_LEN * x.shape[-1])
    hid = jax.nn.silu(flat @ w1 + b1)
    return hid @ w2 + b2


def _overlap_matrix(n_cmp, nsb):
    start = np.arange(n_cmp)[:, None] * NSA_CMP_STRIDE
    bs = np.arange(nsb)[None, :] * NSA_SEL_BLOCK
    return jnp.asarray(((start < bs + NSA_SEL_BLOCK) & (start + NSA_CMP_LEN > bs)).astype(np.float32))


def _nsa_attention(q, k_cmp, v_cmp, k_slc, v_slc, k_win, v_win, gates, kn_cmp,
                   pos_k, pos_v, kw1, kb1, kw2, kb2, vw1, vb1, vw2, vb2, slopes):
    b, h, s, d = q.shape
    g = k_slc.shape[1]
    r = h // g
    scale = d ** -0.5
    qg = q.reshape(b, g, r, s, d)
    sl = slopes.reshape(g, r)
    tpos = jnp.arange(s)
    kc = _rmsnorm(_compress(k_cmp, pos_k, kw1, kb1, kw2, kb2), kn_cmp)
    vc = _compress(v_cmp, pos_v, vw1, vb1, vw2, vb2)
    n_cmp = kc.shape[2]
    cend = jnp.arange(n_cmp) * NSA_CMP_STRIDE + NSA_CMP_LEN - 1
    dist_c = tpos[:, None] - cend[None, :]
    s_c = (jnp.einsum('bgrtd,bgcd->bgrtc', qg, kc).astype(jnp.float32) * scale
           - sl[:, :, None, None] * dist_c.astype(jnp.float32))
    p_c = _masked_softmax(s_c, dist_c >= 0)
    o_cmp = jnp.einsum('bgrtc,bgcd->bgrtd', p_c.astype(vc.dtype), vc)
    nsb = s // NSA_SEL_BLOCK
    imp = jnp.einsum('bgrtc,cn->bgtn', p_c, _overlap_matrix(n_cmp, nsb))
    blk = jnp.arange(nsb)[None, :]
    cur = (tpos // NSA_SEL_BLOCK)[:, None]
    forced = (blk == 0) | (blk == cur) | (blk == cur - 1)
    imp = jnp.where(forced, NSA_FORCE, imp)
    imp = jnp.where(blk <= cur, imp, NEG)
    n_top = min(NSA_TOPN, nsb)
    _, sel = lax.top_k(imp, n_top)
    kbk = k_slc.reshape(b, g, nsb, NSA_SEL_BLOCK, d)
    vbk = v_slc.reshape(b, g, nsb, NSA_SEL_BLOCK, d)
    bi = jnp.arange(b)[:, None, None, None]
    gi = jnp.arange(g)[None, :, None, None]
    offs = jnp.arange(NSA_SEL_BLOCK)
    n_g = n_top * NSA_SEL_BLOCK

    def chunk(c):
        t0 = c * NSA_Q_CHUNK
        qc = lax.dynamic_slice_in_dim(qg, t0, NSA_Q_CHUNK, axis=3)
        idx = lax.dynamic_slice_in_dim(sel, t0, NSA_Q_CHUNK, axis=2)
        tp = t0 + jnp.arange(NSA_Q_CHUNK)
        ks = kbk[bi, gi, idx]
        vs = vbk[bi, gi, idx]
        kpos = idx[..., None] * NSA_SEL_BLOCK + offs
        dist = (tp[:, None, None] - kpos).astype(jnp.float32)
        sc = (jnp.einsum('bgrtd,bgtnld->bgrtnl', qc, ks).astype(jnp.float32) * scale
              - sl[:, :, None, None, None] * dist[:, :, None])
        mask = jnp.broadcast_to((dist >= 0)[:, :, None], sc.shape)
        p = _masked_softmax(sc.reshape(b, g, r, NSA_Q_CHUNK, n_g), mask.reshape(b, g, r, NSA_Q_CHUNK, n_g))
        return jnp.einsum('bgrtm,bgtmd->bgrtd', p.astype(vs.dtype), vs.reshape(b, g, NSA_Q_CHUNK, n_g, d))

    o_slc = lax.map(chunk, jnp.arange(s // NSA_Q_CHUNK))
    o_slc = jnp.moveaxis(o_slc, 0, 3).reshape(b, g, r, s, d)
    o_win = _banded_attention(qg, k_win, v_win, sl, NSA_WINDOW)
    gt = [jnp.transpose(gates[:, :, i], (0, 2, 1)).reshape(b, g, r, s, 1) for i in range(3)]
    o = gt[0] * o_cmp + gt[1] * o_slc + gt[2] * o_win
    return o.astype(q.dtype).reshape(b, h, s, d)


def setup_inputs(seed: int = 0) -> dict:
    key = jax.random.key(seed)
    keys = iter(jax.random.split(key, 40))
    ne = (DEPTH + 1) // 2
    no = DEPTH // 2

    def nrm(shape, scale):
        return jax.random.normal(next(keys), shape, jnp.float32) * scale

    def gain(shape):
        return 1.0 + nrm(shape, 0.02)

    hd = HEAD_DIM
    flat = NSA_CMP_LEN * hd
    return {
        'x': nrm((BATCH, SEQ, D_MODEL), 1.0),
        'norm_g': gain((DEPTH, D_MODEL)),
        'w_out': nrm((DEPTH, MIX_WIDTH, D_MODEL), MIX_WIDTH ** -0.5),
        'e_w_in': nrm((ne, D_MODEL, EVEN_IN), D_MODEL ** -0.5),
        'a_conv_w': nrm((ne, CONV_WIDTH, GROUP_WIDTH), CONV_WIDTH ** -0.5),
        'a_conv_b': nrm((ne, GROUP_WIDTH), 0.02),
        'a_ln_g': gain((ne, GROUP_WIDTH)),
        'a_ln_b': nrm((ne, GROUP_WIDTH), 0.02),
        'b_qnorm_g': gain((ne, hd)),
        'b_knorm_g': gain((ne, hd)),
        'o_w_in': nrm((no, D_MODEL, ODD_IN), D_MODEL ** -0.5),
        'c_qnorm_g': gain((no, hd)),
        'c_knorm_cmp_g': gain((no, hd)),
        'c_knorm_slc_g': gain((no, hd)),
        'c_knorm_win_g': gain((no, hd)),
        'c_pos_k': nrm((no, NSA_CMP_LEN, hd), 0.02),
        'c_pos_v': nrm((no, NSA_CMP_LEN, hd), 0.02),
        'c_k_w1': nrm((no, flat, NSA_CMP_HIDDEN), flat ** -0.5),
        'c_k_b1': nrm((no, NSA_CMP_HIDDEN), 0.02),
        'c_k_w2': nrm((no, NSA_CMP_HIDDEN, hd), NSA_CMP_HIDDEN ** -0.5),
        'c_k_b2': nrm((no, hd), 0.02),
        'c_v_w1': nrm((no, flat, NSA_CMP_HIDDEN), flat ** -0.5),
        'c_v_b1': nrm((no, NSA_CMP_HIDDEN), 0.02),
        'c_v_w2': nrm((no, NSA_CMP_HIDDEN, hd), NSA_CMP_HIDDEN ** -0.5),
        'c_v_b2': nrm((no, hd), 0.02),
        'd_qnorm_g': gain((no, hd)),
        'd_knorm_g': gain((no, hd)),
        'd_sinks': nrm((no, N_HEADS), 0.5),
    }


def reference(x, norm_g, w_out, e_w_in, a_conv_w, a_conv_b, a_ln_g, a_ln_b, b_qnorm_g, b_knorm_g,
              o_w_in, c_qnorm_g, c_knorm_cmp_g, c_knorm_slc_g, c_knorm_win_g, c_pos_k, c_pos_v,
              c_k_w1, c_k_b1, c_k_w2, c_k_b2, c_v_w1, c_v_b1, c_v_w2, c_v_b2,
              d_qnorm_g, d_knorm_g, d_sinks):
    slopes = _alibi_slopes(N_HEADS)
    bsz, slen, _ = x.shape
    for layer in range(DEPTH):
        h = _rmsnorm(x, norm_g[layer])
        if layer % 2 == 0:
            i = layer // 2
            proj = jnp.einsum('bsd,de->bse', h, e_w_in[i])
            u_val, u_gate, z_a, q_b, k_b, v_b, z_b = _split(proj, EVEN_SIZES)
            y_a = _conformer_conv(u_val, u_gate, a_conv_w[i], a_conv_b[i], a_ln_g[i], a_ln_b[i]) * jax.nn.silu(z_a)
            q = _rmsnorm(_split_heads(q_b, N_HEADS), b_qnorm_g[i])
            k = _rmsnorm(_split_heads(k_b, N_HEADS), b_knorm_g[i])
            v = _split_heads(v_b, N_HEADS)
            y_b = _merge_heads(_moba_attention(q, k, v, slopes)) * jax.nn.silu(z_b)
            y = jnp.concatenate([y_a, y_b], axis=-1)
        else:
            i = layer // 2
            proj = jnp.einsum('bsd,de->bse', h, o_w_in[i])
            (q_c, kc_, vc_, ks_, vs_, kw_, vw_, g_c, z_c, q_d, k_d, v_d, z_d) = _split(proj, ODD_SIZES)
            qc = _rmsnorm(_split_heads(q_c, N_HEADS), c_qnorm_g[i])
            gates = jax.nn.sigmoid(g_c.astype(jnp.float32)).reshape(bsz, slen, 3, N_HEADS)
            o_c = _nsa_attention(
                qc, _split_heads(kc_, NSA_KV_HEADS), _split_heads(vc_, NSA_KV_HEADS),
                _rmsnorm(_split_heads(ks_, NSA_KV_HEADS), c_knorm_slc_g[i]), _split_heads(vs_, NSA_KV_HEADS),
                _rmsnorm(_split_heads(kw_, NSA_KV_HEADS), c_knorm_win_g[i]), _split_heads(vw_, NSA_KV_HEADS),
                gates, c_knorm_cmp_g[i], c_pos_k[i], c_pos_v[i],
                c_k_w1[i], c_k_b1[i], c_k_w2[i], c_k_b2[i], c_v_w1[i], c_v_b1[i], c_v_w2[i], c_v_b2[i], slopes)
            y_c = _merge_heads(o_c) * jax.nn.silu(z_c)
            r = N_HEADS // SWA_KV_HEADS
            qd = _rmsnorm(_split_heads(q_d, N_HEADS), d_qnorm_g[i]).reshape(bsz, SWA_KV_HEADS, r, slen, HEAD_DIM)
            kd = _rmsnorm(_split_heads(k_d, SWA_KV_HEADS), d_knorm_g[i])
            vd = _split_heads(v_d, SWA_KV_HEADS)
            o_d = _banded_attention(qd, kd, vd, slopes.reshape(SWA_KV_HEADS, r), SWA_WINDOW,
                                    d_sinks[i].reshape(SWA_KV_HEADS, r))
            y_d = _merge_heads(o_d.reshape(bsz, N_HEADS, slen, HEAD_DIM)) * jax.nn.silu(z_d)
            y = jnp.concatenate([y_c, y_d], axis=-1)
        x = x + jnp.einsum('bse,ed->bsd', y, w_out[layer]).astype(x.dtype)
    return x
```

```python
import functools

import numpy as np
import jax
import jax.numpy as jnp
from jax import lax
from jax.experimental import pallas as pl
from jax.experimental.pallas import tpu as pltpu

HEAD_DIM = 64
N_HEADS = 8
GROUP_WIDTH = N_HEADS * HEAD_DIM
CONV_WIDTH = 31
MOBA_BLOCK = 256
MOBA_TOPK = 3
KV_HEADS = 2
Q_PER_KV = N_HEADS // KV_HEADS
NSA_CMP_LEN = 32
NSA_CMP_STRIDE = 16
NSA_CMP_HIDDEN = 256
NSA_SEL_BLOCK = 64
SEL_SHIFT = NSA_SEL_BLOCK.bit_length() - 1
NSA_TOPN = 16
NSA_WINDOW = 512
NSA_FORCE = 1e4
SWA_WINDOW = 128
EPS = 1e-6
NEG = -1e30
TINY = 1e-30
LANES = 128
CONV_HALO = 32

F32 = jnp.float32
BF16 = jnp.bfloat16

ODD_QC, ODD_ZC, ODD_QD, ODD_ZD = 0, 4, 8, 12
ODD_KC, ODD_VC, ODD_KS, ODD_VS, ODD_KW, ODD_VW, ODD_KD, ODD_VD, ODD_GC = 16, 17, 18, 19, 20, 21, 22, 23, 24
ODD_COLS = 25 * LANES


def _alibi_slopes(n):
    return [float(2.0 ** (-8.0 * (i + 1) / n)) for i in range(n)]


SLOPES = _alibi_slopes(N_HEADS)


def _dot_nt(a, b):
    return lax.dot_general(a, b, (((1,), (1,)), ((), ())), preferred_element_type=F32)


def _dot(a, b):
    return jnp.dot(a, b, preferred_element_type=F32)


def _dot_exact(a, b):
    return jnp.dot(a, b, preferred_element_type=F32, precision=lax.Precision.HIGHEST)


def _sigmoid(x):
    return 1.0 / (1.0 + jnp.exp(-x))


def _silu(x):
    return x * _sigmoid(x)


def _rms(x, g):
    return x * lax.rsqrt(jnp.mean(x * x, axis=-1, keepdims=True) + EPS) * g


def _iota(shape, dim):
    return lax.broadcasted_iota(jnp.int32, shape, dim)


def _inproj_kernel(x_ref, g_ref, w_ref, o_ref, *, chunk):
    x = x_ref[...]
    h = _rms(x, g_ref[...]).astype(BF16)
    for c in range(o_ref.shape[1] // chunk):
        o_ref[:, c * chunk:(c + 1) * chunk] = _dot(h, w_ref[:, c * chunk:(c + 1) * chunk])


def _inproj(x2d, g, w, *, tm=256, chunk):
    m, d = x2d.shape
    e = w.shape[1]
    return pl.pallas_call(
        functools.partial(_inproj_kernel, chunk=chunk),
        grid=(m // tm,),
        in_specs=[pl.BlockSpec((tm, d), lambda i: (i, 0)),
                  pl.BlockSpec((1, d), lambda i: (0, 0)),
                  pl.BlockSpec((d, e), lambda i: (0, 0))],
        out_specs=pl.BlockSpec((tm, e), lambda i: (i, 0)),
        out_shape=jax.ShapeDtypeStruct((m, e), F32),
        name="inproj",
    )(x2d, g.reshape(1, d), w)


def _outproj_kernel(x_ref, ya_ref, yb_ref, wa_ref, wb_ref, o_ref):
    o_ref[...] = x_ref[...] + _dot(ya_ref[...], wa_ref[...]) + _dot(yb_ref[...], wb_ref[...])


def _outproj(x2d, ya, yb, w, *, tm=512):
    m, d = x2d.shape
    gw = ya.shape[1]
    return pl.pallas_call(
        _outproj_kernel,
        grid=(m // tm,),
        in_specs=[pl.BlockSpec((tm, d), lambda i: (i, 0)),
                  pl.BlockSpec((tm, gw), lambda i: (i, 0)),
                  pl.BlockSpec((tm, gw), lambda i: (i, 0)),
                  pl.BlockSpec((gw, d), lambda i: (0, 0)),
                  pl.BlockSpec((gw, d), lambda i: (0, 0))],
        out_specs=pl.BlockSpec((tm, d), lambda i: (i, 0)),
        out_shape=jax.ShapeDtypeStruct((m, d), F32),
        name="outproj",
    )(x2d, ya, yb, w[:gw], w[gw:])


def _conv_kernel(uv_ref, ug_ref, uvh_ref, ugh_ref, z_ref, w_ref, b_ref, lg_ref, lb_ref, o_ref, h_s, *, rows):
    i = pl.program_id(1)
    ts = uv_ref.shape[1]
    halo = uvh_ref[0] * _sigmoid(ugh_ref[0])
    h_s[0:CONV_HALO, :] = jnp.where(i > 0, halo, 0.0)
    h_s[CONV_HALO:, :] = uv_ref[0] * _sigmoid(ug_ref[0])
    base = CONV_HALO - (CONV_WIDTH - 1)
    for c in range(ts // rows):
        acc = jnp.zeros((rows, GROUP_WIDTH), F32)
        for j in range(CONV_WIDTH):
            acc = acc + w_ref[j:j + 1, :] * h_s[pl.ds(c * rows + base + j, rows), :]
        y = acc + b_ref[...]
        mu = jnp.mean(y, axis=-1, keepdims=True)
        yc = y - mu
        y = yc * lax.rsqrt(jnp.mean(yc * yc, axis=-1, keepdims=True) + EPS) * lg_ref[...] + lb_ref[...]
        out = _silu(y) * _silu(z_ref[0, c * rows:(c + 1) * rows, :])
        o_ref[0, c * rows:(c + 1) * rows, :] = out.astype(o_ref.dtype)


def _conv_module(proj, w, b, lg, lb, *, ts=256, rows=64):
    bsz, s, _ = proj.shape
    gw = GROUP_WIDTH
    hb = ts // CONV_HALO
    cur = lambda col: pl.BlockSpec((1, ts, gw), lambda bi, i: (bi, i, col))
    halo = lambda col: pl.BlockSpec((1, CONV_HALO, gw), lambda bi, i: (bi, jnp.maximum(i * hb - 1, 0), col))
    vec = pl.BlockSpec((1, gw), lambda bi, i: (0, 0))
    return pl.pallas_call(
        functools.partial(_conv_kernel, rows=rows),
        grid=(bsz, s // ts),
        in_specs=[cur(0), cur(1), halo(0), halo(1), cur(2),
                  pl.BlockSpec((CONV_WIDTH, gw), lambda bi, i: (0, 0)), vec, vec, vec],
        out_specs=pl.BlockSpec((1, ts, gw), lambda bi, i: (bi, i, 0)),
        out_shape=jax.ShapeDtypeStruct((bsz, s, gw), BF16),
        scratch_shapes=[pltpu.VMEM((CONV_HALO + ts, gw), F32)],
        name="conv_module",
    )(proj, proj, proj, proj, proj, w, b.reshape(1, gw), lg.reshape(1, gw), lb.reshape(1, gw))


def _moba_kernel(slopes_ref, q_ref, k_ref, v_ref, z_ref, gq_ref, gk_ref, o_ref, kn_s, km_s):
    p = pl.program_id(1)
    i = pl.program_id(2)
    s = k_ref.shape[1]
    nblk = s // MOBA_BLOCK
    tq = MOBA_BLOCK
    hd = HEAD_DIM

    @pl.when(i == 0)
    def _():
        for j in range(2):
            kn = _rms(k_ref[0, :, j * hd:(j + 1) * hd], gk_ref[...])
            kn_s[j] = kn.astype(BF16)
            km_s[j] = jnp.mean(kn.reshape(nblk, MOBA_BLOCK, hd), axis=1)

    t0 = i * tq
    tpos = t0 + _iota((tq, tq), 0)
    lane = _iota((tq, tq), 1)
    blk = _iota((tq, nblk), 1)
    outs = []
    for j in range(2):
        slope = slopes_ref[2 * p + j]
        qn = _rms(q_ref[0, :, j * hd:(j + 1) * hd], gq_ref[...])
        gate = lax.dot_general(qn, km_s[j], (((1,), (1,)), ((), ())), preferred_element_type=F32,
                               precision=lax.Precision.HIGHEST)
        gate = jnp.where(blk < i, gate, NEG)
        rank = jnp.zeros((tq, nblk), jnp.int32)
        for mm in range(nblk):
            gm = gate[:, mm:mm + 1]
            beats = (gm > gate) | ((gm == gate) & (mm < blk))
            rank = rank + beats.astype(jnp.int32)
        sel = jnp.where((rank < MOBA_TOPK) & (blk < i), 1.0, 0.0)
        qb = (qn * hd ** -0.5).astype(BF16)

        def body(n, carry, j=j, slope=slope, qb=qb, sel=sel):
            m, l, acc = carry
            off = pl.multiple_of(n * MOBA_BLOCK, MOBA_BLOCK)
            kb = kn_s[j, pl.ds(off, MOBA_BLOCK), :]
            vb = v_ref[0, pl.ds(off, MOBA_BLOCK), j * hd:(j + 1) * hd].astype(BF16)
            dist = tpos - (off + lane)
            sc = _dot_nt(qb, kb) - slope * dist.astype(F32)
            chosen = jnp.sum(jnp.where(blk == n, sel, 0.0), axis=1, keepdims=True) > 0.5
            mask = jnp.where(n == i, dist, jnp.where(chosen, 0, -1)) >= 0
            sc = jnp.where(mask, sc, NEG)
            m_new = jnp.maximum(m, jnp.max(sc, axis=1, keepdims=True))
            alpha = jnp.exp(m - m_new)
            e = jnp.where(mask, jnp.exp(sc - m_new), 0.0)
            l = alpha * l + jnp.sum(e, axis=1, keepdims=True)
            acc = alpha * acc + _dot(e.astype(BF16), vb)
            return m_new, l, acc

        init = (jnp.full((tq, 1), NEG, F32), jnp.zeros((tq, 1), F32), jnp.zeros((tq, hd), F32))
        m, l, acc = lax.fori_loop(0, i + 1, body, init)
        outs.append(acc / jnp.maximum(l, TINY))
    y = jnp.concatenate(outs, axis=1) * _silu(z_ref[0])
    o_ref[0] = y.astype(o_ref.dtype)


def _moba(proj, gq, gk):
    bsz, s, _ = proj.shape
    tq = MOBA_BLOCK
    hd = HEAD_DIM
    q0, k0, v0, z0 = 12, 16, 20, 24
    tile = lambda c0: pl.BlockSpec((1, tq, LANES), lambda b, p, i: (b, i, c0 + p))
    full = lambda c0: pl.BlockSpec((1, s, LANES), lambda b, p, i: (b, 0, c0 + p))
    vec = pl.BlockSpec((1, hd), lambda b, p, i: (0, 0))
    return pl.pallas_call(
        _moba_kernel,
        grid=(bsz, N_HEADS // 2, s // tq),
        in_specs=[pl.BlockSpec(memory_space=pltpu.SMEM), tile(q0), full(k0), full(v0), tile(z0), vec, vec],
        out_specs=pl.BlockSpec((1, tq, LANES), lambda b, p, i: (b, i, p)),
        out_shape=jax.ShapeDtypeStruct((bsz, s, GROUP_WIDTH), BF16),
        scratch_shapes=[pltpu.VMEM((2, s, hd), BF16), pltpu.VMEM((2, s // MOBA_BLOCK, hd), F32)],
        name="moba",
    )(jnp.asarray(SLOPES, F32), proj, proj, proj, proj, gq.reshape(1, hd), gk.reshape(1, hd))


def _cmp_kernel(xk_ref, xv_ref, wka_ref, wkb_ref, wva_ref, wvb_ref, pk_ref, pv_ref, bk1_ref, bv1_ref,
                wk2_ref, wv2_ref, bk2_ref, bv2_ref, gn_ref, ko_ref, vo_ref):
    half = xk_ref.shape[2]
    n = xk_ref.shape[1]

    def hidden(x_ref, wa_ref, wb_ref, pos_ref, b1_ref):
        x = x_ref[0].astype(BF16)
        first = _dot(x, wa_ref[...])
        second = _dot(x, wb_ref[...])
        pos = jnp.broadcast_to(pos_ref[...], (8, 2 * half)).astype(BF16)
        ph = _dot(pos[:, :half], wa_ref[...]) + _dot(pos[:, half:], wb_ref[...])
        return _silu(first + pltpu.roll(second, n - 1, 0) + ph[0:1, :] + b1_ref[...])

    hk = hidden(xk_ref, wka_ref, wkb_ref, pk_ref, bk1_ref).astype(BF16)
    hv = hidden(xv_ref, wva_ref, wvb_ref, pv_ref, bv1_ref).astype(BF16)
    nh = NSA_CMP_HIDDEN
    for g in range(KV_HEADS):
        kc = _dot(hk[:, g * nh:(g + 1) * nh], wk2_ref[...]) + bk2_ref[...]
        ko_ref[0, g] = _rms(kc, gn_ref[...])
        vo_ref[0, g] = _dot(hv[:, g * nh:(g + 1) * nh], wv2_ref[...]) + bv2_ref[...]


def _expand_w1(w1):
    hid = w1.shape[1]
    w = w1.reshape(NSA_CMP_LEN, HEAD_DIM, hid)
    eye = jnp.eye(KV_HEADS, dtype=w1.dtype)
    w = jnp.einsum('ldn,gh->lgdhn', w, eye).reshape(NSA_CMP_LEN * KV_HEADS * HEAD_DIM, KV_HEADS * hid)
    half = w.shape[0] // 2
    return w[:half].astype(BF16), w[half:].astype(BF16)


def _compress(xk, xv, pos_k, pos_v, kw1, kb1, kw2, kb2, vw1, vb1, vw2, vb2, gn):
    bsz, n, half = xk.shape
    hd = HEAD_DIM
    wka, wkb = _expand_w1(kw1)
    wva, wvb = _expand_w1(vw1)
    tile_pos = lambda p: jnp.tile(p[:, None, :], (1, KV_HEADS, 1)).reshape(1, 2 * half)
    tile_b = lambda b: jnp.tile(b.reshape(1, -1), (1, KV_HEADS))
    const = lambda shape: pl.BlockSpec(shape, lambda b: (0,) * len(shape))
    xs = pl.BlockSpec((1, n, half), lambda b: (b, 0, 0))
    hw = KV_HEADS * NSA_CMP_HIDDEN
    out = pl.BlockSpec((1, KV_HEADS, n, hd), lambda b: (b, 0, 0, 0))
    return pl.pallas_call(
        _cmp_kernel,
        grid=(bsz,),
        in_specs=[xs, xs, const((half, hw)), const((half, hw)), const((half, hw)), const((half, hw)),
                  const((1, 2 * half)), const((1, 2 * half)), const((1, hw)), const((1, hw)),
                  const((NSA_CMP_HIDDEN, hd)), const((NSA_CMP_HIDDEN, hd)), const((1, hd)), const((1, hd)),
                  const((1, hd))],
        out_specs=[out, out],
        out_shape=[jax.ShapeDtypeStruct((bsz, KV_HEADS, n, hd), F32)] * 2,
        name="nsa_compress",
    )(xk, xv, wka, wkb, wva, wvb, tile_pos(pos_k), tile_pos(pos_v), tile_b(kb1), tile_b(vb1),
      kw2.astype(BF16), vw2.astype(BF16), kb2.reshape(1, hd), vb2.reshape(1, hd), gn.reshape(1, hd))


def _group_rows(tq, t0):
    rows = Q_PER_KV * tq
    r = _iota((rows, 1), 0)
    tpos = t0 + (r & (tq - 1))
    return rows, r, tpos


def _slope_col(r, tq, g):
    col = jnp.full(r.shape, SLOPES[Q_PER_KV * g], F32)
    for k in range(1, Q_PER_KV):
        col = jnp.where(r >= k * tq, SLOPES[Q_PER_KV * g + k], col)
    return col


def _online_step(carry, sc, mask, vb):
    m, l, acc = carry
    sc = jnp.where(mask, sc, NEG)
    m_new = jnp.maximum(m, jnp.max(sc, axis=1, keepdims=True))
    alpha = jnp.exp(m - m_new)
    e = jnp.where(mask, jnp.exp(sc - m_new), 0.0)
    l = alpha * l + jnp.sum(e, axis=1, keepdims=True)
    acc = alpha * acc + _dot(e.astype(BF16), vb)
    return m_new, l, acc


def _online_init(rows):
    return (jnp.full((rows, 1), NEG, F32), jnp.zeros((rows, 1), F32), jnp.zeros((rows, HEAD_DIM), F32))


def _banded(q2, kn_ref, v_ref, g, i, tq, tpos, slope, window, sink=None):
    rows = q2.shape[0]
    lane = _iota((rows, tq), 1)
    carry = _online_init(rows)
    nback = window // tq
    for jj in range(nback + 1):
        cidx = i - nback + jj
        off = pl.multiple_of(jnp.maximum(cidx, 0) * tq, tq)
        kb = kn_ref[g, pl.ds(off, tq), :]
        vb = v_ref[g, pl.ds(off, tq), :]
        kpos = cidx * tq + lane
        dist = tpos - kpos
        sc = _dot_nt(q2, kb) - slope * dist.astype(F32)
        mask = (dist >= 0) & (dist < window) & (kpos >= 0)
        carry = _online_step(carry, sc, mask, vb)
    m, l, acc = carry
    if sink is None:
        return acc / jnp.maximum(l, TINY)
    mf = jnp.maximum(m, sink)
    alpha = jnp.exp(m - mf)
    return acc * alpha / jnp.maximum(l * alpha + jnp.exp(sink - mf), TINY)


def _stack_q(q_ref, g, gq_ref):
    hd = HEAD_DIM
    heads = [_rms(q_ref[0, :, (Q_PER_KV * g + k) * hd:(Q_PER_KV * g + k + 1) * hd], gq_ref[...])
             for k in range(Q_PER_KV)]
    return (jnp.concatenate(heads, axis=0) * hd ** -0.5).astype(BF16)


def _normalize_kv(k_ref, v_ref, gk_ref, kn_s, vb_s):
    hd = HEAD_DIM
    for g in range(KV_HEADS):
        kn_s[g] = _rms(k_ref[0, :, g * hd:(g + 1) * hd], gk_ref[...]).astype(BF16)
        vb_s[g] = v_ref[0, :, g * hd:(g + 1) * hd].astype(BF16)


def _nsa_kernel(q_ref, kc_ref, vc_ref, ks_ref, vs_ref, kw_ref, vw_ref, gate_ref, z_ref,
                gq_ref, gks_ref, gkw_ref, o_ref, ksn_s, vsb_s, kwn_s, vwb_s):
    i = pl.program_id(1)
    tq = q_ref.shape[1]
    s = ks_ref.shape[1]
    hd = HEAD_DIM
    ncmp = kc_ref.shape[2]
    nsb = s // NSA_SEL_BLOCK

    @pl.when(i == 0)
    def _():
        _normalize_kv(ks_ref, vs_ref, gks_ref, ksn_s, vsb_s)
        _normalize_kv(kw_ref, vw_ref, gkw_ref, kwn_s, vwb_s)

    t0 = i * tq
    rows, r, tpos = _group_rows(tq, t0)
    gates = _sigmoid(gate_ref[0])
    outs = []
    for g in range(KV_HEADS):
        slope = _slope_col(r, tq, g)
        q2 = _stack_q(q_ref, g, gq_ref)

        cend = _iota((rows, ncmp), 1) * NSA_CMP_STRIDE + (NSA_CMP_LEN - 1)
        dist_c = tpos - cend
        sc = _dot_nt(q2, kc_ref[0, g].astype(BF16)) - slope * dist_c.astype(F32)
        mask_c = dist_c >= 0
        sc = jnp.where(mask_c, sc, NEG)
        e = jnp.where(mask_c, jnp.exp(sc - jnp.max(sc, axis=1, keepdims=True)), 0.0)
        p_c = e / jnp.maximum(jnp.sum(e, axis=1, keepdims=True), TINY)
        o_cmp = _dot(p_c.astype(BF16), vc_ref[0, g].astype(BF16))

        psum = p_c[0:tq]
        for k in range(1, Q_PER_KV):
            psum = psum + p_c[k * tq:(k + 1) * tq]
        cstart = _iota((ncmp, nsb), 0) * NSA_CMP_STRIDE
        bstart = _iota((ncmp, nsb), 1) * NSA_SEL_BLOCK
        overlap = ((cstart < bstart + NSA_SEL_BLOCK) & (cstart + NSA_CMP_LEN > bstart)).astype(F32)
        imp = _dot_exact(psum, overlap)
        blk = _iota((tq, nsb), 1)
        cur = (t0 + _iota((tq, nsb), 0)) >> SEL_SHIFT
        forced = (blk == 0) | (blk == cur) | (blk == cur - 1)
        imp = jnp.where(forced, NSA_FORCE, imp)
        imp = jnp.where(blk <= cur, imp, NEG)
        rank = jnp.zeros((tq, nsb), jnp.int32)
        for mm in range(nsb):
            gm = imp[:, mm:mm + 1]
            beats = (gm > imp) | ((gm == imp) & (mm < blk))
            rank = rank + beats.astype(jnp.int32)
        sel = jnp.where(rank < min(NSA_TOPN, nsb), 1.0, 0.0).astype(BF16)
        sel4 = jnp.concatenate([sel] * Q_PER_KV, axis=0)

        lane = _iota((rows, tq), 1)

        def body(n, carry, g=g, q2=q2, slope=slope, sel4=sel4):
            off = pl.multiple_of(n * tq, tq)
            kpos = off + lane
            dist = tpos - kpos
            sc = _dot_nt(q2, ksn_s[g, pl.ds(off, tq), :]) - slope * dist.astype(F32)
            member = jnp.where(_iota((nsb, tq), 0) == ((off + _iota((nsb, tq), 1)) >> SEL_SHIFT), 1.0, 0.0)
            member = member.astype(BF16)
            chosen = _dot(sel4, member) > 0.5
            return _online_step(carry, sc, chosen & (dist >= 0), vsb_s[g, pl.ds(off, tq), :])

        m, l, acc = lax.fori_loop(0, i + 1, body, _online_init(rows))
        o_slc = acc / jnp.maximum(l, TINY)

        o_win = _banded(q2, kwn_s, vwb_s, g, i, tq, tpos, slope, NSA_WINDOW)

        for k in range(Q_PER_KV):
            h = Q_PER_KV * g + k
            rs = slice(k * tq, (k + 1) * tq)
            o = (gates[:, h:h + 1] * o_cmp[rs] + gates[:, N_HEADS + h:N_HEADS + h + 1] * o_slc[rs]
                 + gates[:, 2 * N_HEADS + h:2 * N_HEADS + h + 1] * o_win[rs])
            outs.append(o)
    y = jnp.concatenate(outs, axis=1) * _silu(z_ref[0])
    o_ref[0] = y.astype(o_ref.dtype)


def _nsa(proj, kcn, vc, gq, gks, gkw, *, tq=128):
    bsz, s, _ = proj.shape
    hd = HEAD_DIM
    gw = GROUP_WIDTH
    ncmp = kcn.shape[2]
    wide = lambda c0: pl.BlockSpec((1, tq, gw), lambda b, i: (b, i, c0 // 4))
    full = lambda c0: pl.BlockSpec((1, s, LANES), lambda b, i: (b, 0, c0))
    cmp = pl.BlockSpec((1, KV_HEADS, ncmp, hd), lambda b, i: (b, 0, 0, 0))
    vec = pl.BlockSpec((1, hd), lambda b, i: (0, 0))
    kv_scratch = pltpu.VMEM((KV_HEADS, s, hd), BF16)
    return pl.pallas_call(
        _nsa_kernel,
        grid=(bsz, s // tq),
        in_specs=[wide(ODD_QC), cmp, cmp, full(ODD_KS), full(ODD_VS), full(ODD_KW), full(ODD_VW),
                  pl.BlockSpec((1, tq, LANES), lambda b, i: (b, i, ODD_GC)), wide(ODD_ZC), vec, vec, vec],
        out_specs=pl.BlockSpec((1, tq, gw), lambda b, i: (b, i, 0)),
        out_shape=jax.ShapeDtypeStruct((bsz, s, gw), BF16),
        scratch_shapes=[kv_scratch] * 4,
        name="nsa",
    )(proj, kcn, vc, proj, proj, proj, proj, proj, proj, gq.reshape(1, hd), gks.reshape(1, hd),
      gkw.reshape(1, hd))


def _swa_kernel(sinks_ref, q_ref, k_ref, v_ref, z_ref, gq_ref, gk_ref, o_ref, kn_s, vb_s):
    i = pl.program_id(1)
    tq = q_ref.shape[1]

    @pl.when(i == 0)
    def _():
        _normalize_kv(k_ref, v_ref, gk_ref, kn_s, vb_s)

    rows, r, tpos = _group_rows(tq, i * tq)
    outs = []
    for g in range(KV_HEADS):
        slope = _slope_col(r, tq, g)
        sink = jnp.full((rows, 1), sinks_ref[Q_PER_KV * g], F32)
        for k in range(1, Q_PER_KV):
            sink = jnp.where(r >= k * tq, sinks_ref[Q_PER_KV * g + k], sink)
        q2 = _stack_q(q_ref, g, gq_ref)
        o = _banded(q2, kn_s, vb_s, g, i, tq, tpos, slope, SWA_WINDOW, sink)
        outs.extend(o[k * tq:(k + 1) * tq] for k in range(Q_PER_KV))
    y = jnp.concatenate(outs, axis=1) * _silu(z_ref[0])
    o_ref[0] = y.astype(o_ref.dtype)


def _swa(proj, sinks, gq, gk, *, tq=128):
    bsz, s, _ = proj.shape
    hd = HEAD_DIM
    gw = GROUP_WIDTH
    wide = lambda c0: pl.BlockSpec((1, tq, gw), lambda b, i: (b, i, c0 // 4))
    full = lambda c0: pl.BlockSpec((1, s, LANES), lambda b, i: (b, 0, c0))
    vec = pl.BlockSpec((1, hd), lambda b, i: (0, 0))
    kv_scratch = pltpu.VMEM((KV_HEADS, s, hd), BF16)
    return pl.pallas_call(
        _swa_kernel,
        grid=(bsz, s // tq),
        in_specs=[pl.BlockSpec(memory_space=pltpu.SMEM), wide(ODD_QD), full(ODD_KD), full(ODD_VD),
                  wide(ODD_ZD), vec, vec],
        out_specs=pl.BlockSpec((1, tq, gw), lambda b, i: (b, i, 0)),
        out_shape=jax.ShapeDtypeStruct((bsz, s, gw), BF16),
        scratch_shapes=[kv_scratch] * 2,
        name="swa",
    )(sinks.astype(F32), proj, proj, proj, proj, gq.reshape(1, hd), gk.reshape(1, hd))


def _pack_odd_weight(w):
    gw, kw, ng = GROUP_WIDTH, KV_HEADS * HEAD_DIM, 3 * N_HEADS
    sizes = [gw, kw, kw, kw, kw, kw, kw, ng, gw, gw, kw, kw, gw]
    starts = np.concatenate([[0], np.cumsum(sizes)]).tolist()
    part = lambda k: w[:, starts[k]:starts[k + 1]]
    pad = jnp.zeros((w.shape[0], LANES - ng), w.dtype)
    order = [0, 8, 9, 12, 1, 2, 3, 4, 5, 6, 10, 11, 7]
    return jnp.concatenate([part(k) for k in order] + [pad], axis=1).astype(BF16)


def kernel(x, norm_g, w_out, e_w_in, a_conv_w, a_conv_b, a_ln_g, a_ln_b, b_qnorm_g, b_knorm_g, o_w_in, c_qnorm_g, c_knorm_cmp_g, c_knorm_slc_g, c_knorm_win_g, c_pos_k, c_pos_v, c_k_w1, c_k_b1, c_k_w2, c_k_b2, c_v_w1, c_v_b1, c_v_w2, c_v_b2, d_qnorm_g, d_knorm_g, d_sinks):
    bsz, s, d = x.shape
    m = bsz * s
    assert s % MOBA_BLOCK == 0 and d == 2 * GROUP_WIDTH
    x2 = x.reshape(m, d)

    proj = _inproj(x2, norm_g[0], e_w_in[0].astype(BF16), chunk=GROUP_WIDTH).reshape(bsz, s, -1)
    y_a = _conv_module(proj, a_conv_w[0], a_conv_b[0], a_ln_g[0], a_ln_b[0])
    y_b = _moba(proj, b_qnorm_g[0], b_knorm_g[0])
    x2 = _outproj(x2, y_a.reshape(m, -1), y_b.reshape(m, -1), w_out[0].astype(BF16))

    proj = _inproj(x2, norm_g[1], _pack_odd_weight(o_w_in[0]), chunk=5 * LANES).reshape(bsz, s, ODD_COLS)
    rows16 = lambda c0: proj[:, :, c0 * LANES:(c0 + 1) * LANES].reshape(bsz, s // NSA_CMP_STRIDE,
                                                                       NSA_CMP_STRIDE * LANES)
    kcn, vc = _compress(rows16(ODD_KC), rows16(ODD_VC), c_pos_k[0], c_pos_v[0], c_k_w1[0], c_k_b1[0],
                        c_k_w2[0], c_k_b2[0], c_v_w1[0], c_v_b1[0], c_v_w2[0], c_v_b2[0], c_knorm_cmp_g[0])
    y_c = _nsa(proj, kcn, vc, c_qnorm_g[0], c_knorm_slc_g[0], c_knorm_win_g[0])
    y_d = _swa(proj, d_sinks[0], d_qnorm_g[0], d_knorm_g[0])
    x2 = _outproj(x2, y_c.reshape(m, -1), y_d.reshape(m, -1), w_out[1].astype(BF16))
    return x2.reshape(bsz, s, d)
```

```python
import functools

import ml_dtypes
import numpy as np
import jax
import jax.numpy as jnp
from jax import lax
from jax.experimental import pallas as pl
from jax.experimental.pallas import tpu as pltpu

HEAD_DIM = 64
N_HEADS = 8
GROUP_WIDTH = N_HEADS * HEAD_DIM
CONV_WIDTH = 31
MOBA_BLOCK = 256
MOBA_TOPK = 3
KV_HEADS = 2
Q_PER_KV = N_HEADS // KV_HEADS
NSA_CMP_LEN = 32
NSA_CMP_STRIDE = 16
NSA_CMP_HIDDEN = 256
NSA_SEL_BLOCK = 64
NSA_TOPN = 16
NSA_WINDOW = 512
NSA_FORCE = 1e4
SWA_WINDOW = 128
EPS = 1e-6
NEG = -1e30
TINY = 1e-30
LANES = 128
CONV_HALO = 32

LOG2E = float(np.log2(np.e))
BIG = 2.0 ** 99
M_INIT = -1e38
POS_SPLIT_SHIFT = 8
SLOPE_PIECES = 4
SEL_FEAT0 = 2 * SLOPE_PIECES

F32 = jnp.float32
BF16 = jnp.bfloat16

ODD_QC, ODD_ZC, ODD_QD, ODD_ZD = 0, 4, 8, 12
ODD_KC, ODD_VC, ODD_KS, ODD_VS, ODD_KW, ODD_VW, ODD_KD, ODD_VD, ODD_GC = 16, 17, 18, 19, 20, 21, 22, 23, 24
ODD_COLS = 25 * LANES


def _alibi_slopes(n):
    return [float(2.0 ** (-8.0 * (i + 1) / n)) for i in range(n)]


SLOPES = _alibi_slopes(N_HEADS)


def _alibi_query_features(width):
    table = np.zeros((N_HEADS, 2 * SLOPE_PIECES, width), np.float32)
    for h, slope in enumerate(SLOPES):
        rest = np.float64(slope) * LOG2E
        for k in range(SLOPE_PIECES):
            piece = float(np.float32(rest).astype(ml_dtypes.bfloat16).astype(np.float32))
            table[h, k] = piece
            table[h, SLOPE_PIECES + k] = piece
            rest -= piece
    return table


def _dot(a, b):
    return jnp.dot(a, b, preferred_element_type=F32)


def _dot_exact(a, b):
    return jnp.dot(a, b, preferred_element_type=F32, precision=lax.Precision.HIGHEST)


def _sigmoid(x):
    return 1.0 / (1.0 + jnp.exp(-x))


def _silu(x):
    return x * _sigmoid(x)


def _rms(x, g):
    return x * lax.rsqrt(jnp.mean(x * x, axis=-1, keepdims=True) + EPS) * g


def _iota(shape, dim):
    return lax.broadcasted_iota(jnp.int32, shape, dim)


def _inproj_kernel(x_ref, g_ref, w_ref, o_ref, *, chunk):
    x = x_ref[...]
    h = _rms(x, g_ref[...]).astype(BF16)
    for c in range(o_ref.shape[1] // chunk):
        o_ref[:, c * chunk:(c + 1) * chunk] = _dot(h, w_ref[:, c * chunk:(c + 1) * chunk])


def _inproj(x2d, g, w, *, tm=256, chunk):
    m, d = x2d.shape
    e = w.shape[1]
    return pl.pallas_call(
        functools.partial(_inproj_kernel, chunk=chunk),
        grid=(m // tm,),
        in_specs=[pl.BlockSpec((tm, d), lambda i: (i, 0)),
                  pl.BlockSpec((1, d), lambda i: (0, 0)),
                  pl.BlockSpec((d, e), lambda i: (0, 0))],
        out_specs=pl.BlockSpec((tm, e), lambda i: (i, 0)),
        out_shape=jax.ShapeDtypeStruct((m, e), F32),
        name="inproj",
    )(x2d, g.reshape(1, d), w)


def _outproj_kernel(x_ref, ya_ref, yb_ref, wa_ref, wb_ref, o_ref):
    o_ref[...] = x_ref[...] + _dot(ya_ref[...], wa_ref[...]) + _dot(yb_ref[...], wb_ref[...])


def _outproj(x2d, ya, yb, w, *, tm=512):
    m, d = x2d.shape
    gw = ya.shape[1]
    return pl.pallas_call(
        _outproj_kernel,
        grid=(m // tm,),
        in_specs=[pl.BlockSpec((tm, d), lambda i: (i, 0)),
                  pl.BlockSpec((tm, gw), lambda i: (i, 0)),
                  pl.BlockSpec((tm, gw), lambda i: (i, 0)),
                  pl.BlockSpec((gw, d), lambda i: (0, 0)),
                  pl.BlockSpec((gw, d), lambda i: (0, 0))],
        out_specs=pl.BlockSpec((tm, d), lambda i: (i, 0)),
        out_shape=jax.ShapeDtypeStruct((m, d), F32),
        name="outproj",
    )(x2d, ya, yb, w[:gw], w[gw:])


def _conv_kernel(uv_ref, ug_ref, uvh_ref, ugh_ref, z_ref, w_ref, b_ref, lg_ref, lb_ref, o_ref, h_s, *, rows):
    i = pl.program_id(1)
    ts = uv_ref.shape[1]
    halo = uvh_ref[0] * _sigmoid(ugh_ref[0])
    h_s[0:CONV_HALO, :] = jnp.where(i > 0, halo, 0.0)
    h_s[CONV_HALO:, :] = uv_ref[0] * _sigmoid(ug_ref[0])
    base = CONV_HALO - (CONV_WIDTH - 1)
    for c in range(ts // rows):
        acc = jnp.zeros((rows, GROUP_WIDTH), F32)
        for j in range(CONV_WIDTH):
            acc = acc + w_ref[j:j + 1, :] * h_s[pl.ds(c * rows + base + j, rows), :]
        y = acc + b_ref[...]
        mu = jnp.mean(y, axis=-1, keepdims=True)
        yc = y - mu
        y = yc * lax.rsqrt(jnp.mean(yc * yc, axis=-1, keepdims=True) + EPS) * lg_ref[...] + lb_ref[...]
        out = _silu(y) * _silu(z_ref[0, c * rows:(c + 1) * rows, :])
        o_ref[0, c * rows:(c + 1) * rows, :] = out.astype(o_ref.dtype)


def _conv_module(proj, w, b, lg, lb, *, ts=256, rows=64):
    bsz, s, _ = proj.shape
    gw = GROUP_WIDTH
    hb = ts // CONV_HALO
    cur = lambda col: pl.BlockSpec((1, ts, gw), lambda bi, i: (bi, i, col))
    halo = lambda col: pl.BlockSpec((1, CONV_HALO, gw), lambda bi, i: (bi, jnp.maximum(i * hb - 1, 0), col))
    vec = pl.BlockSpec((1, gw), lambda bi, i: (0, 0))
    return pl.pallas_call(
        functools.partial(_conv_kernel, rows=rows),
        grid=(bsz, s // ts),
        in_specs=[cur(0), cur(1), halo(0), halo(1), cur(2),
                  pl.BlockSpec((CONV_WIDTH, gw), lambda bi, i: (0, 0)), vec, vec, vec],
        out_specs=pl.BlockSpec((1, ts, gw), lambda bi, i: (bi, i, 0)),
        out_shape=jax.ShapeDtypeStruct((bsz, s, gw), BF16),
        scratch_shapes=[pltpu.VMEM((CONV_HALO + ts, gw), F32)],
        name="conv_module",
    )(proj, proj, proj, proj, proj, w, b.reshape(1, gw), lg.reshape(1, gw), lb.reshape(1, gw))


def _key_features(pos, sel_shift=None):
    col = _iota(pos.shape, 1)
    hi = (pos >> POS_SPLIT_SHIFT) << POS_SPLIT_SHIFT
    lo = pos & ((1 << POS_SPLIT_SHIFT) - 1)
    feat = jnp.where(col < SLOPE_PIECES, hi, jnp.where(col < 2 * SLOPE_PIECES, lo, 0))
    if sel_shift is not None:
        feat = jnp.where(col - SEL_FEAT0 == (pos >> sel_shift), 1, feat)
    return feat.astype(F32).astype(BF16)


def _norm_query_t(x, g_col):
    ss = jnp.mean(x * x, axis=0, keepdims=True)
    return x * lax.rsqrt(ss + EPS) * g_col * (HEAD_DIM ** -0.5 * LOG2E)


def _augment_query(q_t, alibi, sel_bias=None):
    n = q_t.shape[1]
    parts = [q_t, alibi]
    used = alibi.shape[0]
    if sel_bias is not None:
        parts.append(sel_bias)
        used += sel_bias.shape[0]
    parts.append(jnp.zeros((HEAD_DIM - used, n), F32))
    return jnp.concatenate(parts, axis=0).astype(BF16)


def _flash_init(n):
    return (jnp.full((1, n), M_INIT, F32), jnp.zeros((1, n), F32), jnp.zeros((HEAD_DIM, n), F32))


def _flash_step(carry, sc, v_t):
    m, l, acc = carry
    m_new = jnp.maximum(m, jnp.max(sc, axis=0, keepdims=True))
    p = jnp.exp2(sc - m_new)
    alpha = jnp.exp2(m - m_new)
    l = alpha * l + jnp.sum(p, axis=0, keepdims=True)
    acc = alpha * acc + _dot(v_t, p.astype(BF16))
    return m_new, l, acc


def _rank_select(score, blk, limit, count):
    rank = jnp.zeros(score.shape, jnp.int32)
    for mm in range(limit):
        gm = score[mm:mm + 1, :]
        beats = (gm > score) | ((gm == score) & (mm < blk))
        rank = rank + jnp.where(beats, 1, 0)
    return rank < count


def _tile_lanes(x, reps):
    return jnp.concatenate([x] * reps, axis=1) if reps > 1 else x


def _causal_bias(tk, tq, reps):
    keep = _iota((tk, tq), 0) <= _iota((tk, tq), 1)
    return _tile_lanes(jnp.where(keep, 0.0, -BIG), reps)


def _window_edge_bias(tk, tq, reps):
    keep = _iota((tk, tq), 0) > _iota((tk, tq), 1)
    return _tile_lanes(jnp.where(keep, 0.0, -BIG), reps)


def _store_chunked_t(x, dst_ref, tk):
    x_t = x.T.astype(BF16)
    for c in range(x.shape[0] // tk):
        dst_ref[c] = x_t[:, c * tk:(c + 1) * tk]


def _moba_kernel(alibi_ref, q_ref, k_ref, v_ref, z_ref, gq_ref, gk_ref, o_ref, ka_s, vt_s, km_s):
    i = pl.program_id(2)
    s = k_ref.shape[1]
    nblk = s // MOBA_BLOCK
    tq = MOBA_BLOCK
    hd = HEAD_DIM

    @pl.when(i == 0)
    def _():
        feat = _key_features(_iota((s, hd), 0), sel_shift=MOBA_BLOCK.bit_length() - 1)
        for j in range(2):
            kn = _rms(k_ref[0, :, j * hd:(j + 1) * hd], gk_ref[...])
            ka_s[j] = jnp.concatenate([kn.astype(BF16), feat], axis=1)
            km_s[j] = jnp.mean(kn.reshape(nblk, MOBA_BLOCK, hd), axis=1)
        _store_chunked_t(v_ref[0], vt_s, tq)

    q_t = q_ref[0].T
    blk = _iota((nblk, tq), 0)
    diag_bias = _causal_bias(tq, tq, 1)
    outs = []
    for j in range(2):
        qn = _norm_query_t(q_t[j * hd:(j + 1) * hd], gq_ref[...])
        gate = jnp.where(blk < i, _dot_exact(km_s[j], qn), NEG)
        chosen = _rank_select(gate, blk, nblk, MOBA_TOPK) & (blk < i)
        sel_bias = jnp.where(chosen | (blk == i), 0.0, -BIG)
        qa = _augment_query(qn, alibi_ref[j], sel_bias)

        def body(n, carry, j=j, qa=qa):
            off = pl.multiple_of(n * tq, tq)
            return _flash_step(carry, _dot(ka_s[j, pl.ds(off, tq), :], qa), vt_s[n, j * hd:(j + 1) * hd, :])

        carry = lax.fori_loop(0, i, body, _flash_init(tq))
        own = _dot(ka_s[j, pl.ds(pl.multiple_of(i * tq, tq), tq), :], qa) + diag_bias
        m, l, acc = _flash_step(carry, own, vt_s[i, j * hd:(j + 1) * hd, :])
        outs.append(acc / jnp.maximum(l, TINY))
    y = jnp.concatenate(outs, axis=0).T * _silu(z_ref[0])
    o_ref[0] = y.astype(o_ref.dtype)


def _moba(proj, gq, gk):
    bsz, s, _ = proj.shape
    tq = MOBA_BLOCK
    hd = HEAD_DIM
    nblk = s // MOBA_BLOCK
    assert SEL_FEAT0 + nblk <= hd
    q0, k0, v0, z0 = 12, 16, 20, 24
    tile = lambda c0: pl.BlockSpec((1, tq, LANES), lambda b, p, i: (b, i, c0 + p))
    full = lambda c0: pl.BlockSpec((1, s, LANES), lambda b, p, i: (b, 0, c0 + p))
    return pl.pallas_call(
        _moba_kernel,
        grid=(bsz, N_HEADS // 2, s // tq),
        in_specs=[pl.BlockSpec((2, 2 * SLOPE_PIECES, tq), lambda b, p, i: (p, 0, 0)),
                  tile(q0), full(k0), full(v0), tile(z0),
                  pl.BlockSpec((hd, 1), lambda b, p, i: (0, 0)), pl.BlockSpec((1, hd), lambda b, p, i: (0, 0))],
        out_specs=pl.BlockSpec((1, tq, LANES), lambda b, p, i: (b, i, p)),
        out_shape=jax.ShapeDtypeStruct((bsz, s, GROUP_WIDTH), BF16),
        scratch_shapes=[pltpu.VMEM((2, s, LANES), BF16), pltpu.VMEM((nblk, LANES, tq), BF16),
                        pltpu.VMEM((2, nblk, hd), F32)],
        name="moba",
    )(jnp.asarray(_alibi_query_features(tq)), proj, proj, proj, proj, gq.reshape(hd, 1), gk.reshape(1, hd))


def _cmp_kernel(xk_ref, xv_ref, wka_ref, wkb_ref, wva_ref, wvb_ref, pk_ref, pv_ref, bk1_ref, bv1_ref,
                wk2_ref, wv2_ref, bk2_ref, bv2_ref, gn_ref, ko_ref, vo_ref):
    half = xk_ref.shape[2]
    n = xk_ref.shape[1]

    def hidden(x_ref, wa_ref, wb_ref, pos_ref, b1_ref):
        x = x_ref[0].astype(BF16)
        first = _dot(x, wa_ref[...])
        second = _dot(x, wb_ref[...])
        pos = jnp.broadcast_to(pos_ref[...], (8, 2 * half)).astype(BF16)
        ph = _dot(pos[:, :half], wa_ref[...]) + _dot(pos[:, half:], wb_ref[...])
        return _silu(first + pltpu.roll(second, n - 1, 0) + ph[0:1, :] + b1_ref[...])

    hk = hidden(xk_ref, wka_ref, wkb_ref, pk_ref, bk1_ref).astype(BF16)
    hv = hidden(xv_ref, wva_ref, wvb_ref, pv_ref, bv1_ref).astype(BF16)
    nh = NSA_CMP_HIDDEN
    ks, vs = [], []
    for g in range(KV_HEADS):
        ks.append(_rms(_dot(hk[:, g * nh:(g + 1) * nh], wk2_ref[...]) + bk2_ref[...], gn_ref[...]))
        vs.append(_dot(hv[:, g * nh:(g + 1) * nh], wv2_ref[...]) + bv2_ref[...])
    ko_ref[0] = jnp.concatenate(ks, axis=1)
    vo_ref[0] = jnp.concatenate(vs, axis=1)


def _expand_w1(w1):
    hid = w1.shape[1]
    w = w1.reshape(NSA_CMP_LEN, HEAD_DIM, hid)
    eye = jnp.eye(KV_HEADS, dtype=w1.dtype)
    w = jnp.einsum('ldn,gh->lgdhn', w, eye).reshape(NSA_CMP_LEN * KV_HEADS * HEAD_DIM, KV_HEADS * hid)
    half = w.shape[0] // 2
    return w[:half].astype(BF16), w[half:].astype(BF16)


def _compress(xk, xv, pos_k, pos_v, kw1, kb1, kw2, kb2, vw1, vb1, vw2, vb2, gn):
    bsz, n, half = xk.shape
    hd = HEAD_DIM
    wka, wkb = _expand_w1(kw1)
    wva, wvb = _expand_w1(vw1)
    tile_pos = lambda p: jnp.tile(p[:, None, :], (1, KV_HEADS, 1)).reshape(1, 2 * half)
    tile_b = lambda b: jnp.tile(b.reshape(1, -1), (1, KV_HEADS))
    const = lambda shape: pl.BlockSpec(shape, lambda b: (0,) * len(shape))
    xs = pl.BlockSpec((1, n, half), lambda b: (b, 0, 0))
    hw = KV_HEADS * NSA_CMP_HIDDEN
    out = pl.BlockSpec((1, n, KV_HEADS * hd), lambda b: (b, 0, 0))
    return pl.pallas_call(
        _cmp_kernel,
        grid=(bsz,),
        in_specs=[xs, xs, const((half, hw)), const((half, hw)), const((half, hw)), const((half, hw)),
                  const((1, 2 * half)), const((1, 2 * half)), const((1, hw)), const((1, hw)),
                  const((NSA_CMP_HIDDEN, hd)), const((NSA_CMP_HIDDEN, hd)), const((1, hd)), const((1, hd)),
                  const((1, hd))],
        out_specs=[out, out],
        out_shape=[jax.ShapeDtypeStruct((bsz, n, KV_HEADS * hd), F32)] * 2,
        name="nsa_compress",
    )(xk, xv, wka, wkb, wva, wvb, tile_pos(pos_k), tile_pos(pos_v), tile_b(kb1), tile_b(vb1),
      kw2.astype(BF16), vw2.astype(BF16), kb2.reshape(1, hd), vb2.reshape(1, hd), gn.reshape(1, hd))


def _group_query(q_t, alibi_ref, g, gq_col):
    hd = HEAD_DIM
    heads = range(Q_PER_KV * g, Q_PER_KV * (g + 1))
    qn = jnp.concatenate([_norm_query_t(q_t[h * hd:(h + 1) * hd], gq_col) for h in heads], axis=1)
    alibi = jnp.concatenate([alibi_ref[h] for h in heads], axis=1)
    return qn, alibi


def _augment_keys(k_ref, gk_ref, feat, dst_ref):
    hd = HEAD_DIM
    for g in range(KV_HEADS):
        kn = _rms(k_ref[0, :, g * hd:(g + 1) * hd], gk_ref[...])
        dst_ref[g] = jnp.concatenate([kn.astype(BF16), feat], axis=1)


def _banded(qa, ka_ref, vt_ref, g, i, tq, window, diag_bias, edge_bias):
    hd = HEAD_DIM
    nback = window // tq

    def chunk(c):
        off = pl.multiple_of(c * tq, tq)
        return _dot(ka_ref[g, pl.ds(off, tq), :], qa), vt_ref[c, g * hd:(g + 1) * hd, :]

    far = i - nback
    sc, v_t = chunk(jnp.maximum(far, 0))
    carry = _flash_step(_flash_init(qa.shape[1]), sc + (edge_bias + jnp.where(far >= 0, 0.0, -BIG)), v_t)
    carry = lax.fori_loop(jnp.maximum(far + 1, 0), i, lambda c, cr: _flash_step(cr, *chunk(c)), carry)
    sc, v_t = chunk(i)
    return _flash_step(carry, sc + diag_bias, v_t)


def _regroup_heads(o_t, tq):
    return jnp.concatenate([o_t[:, k * tq:(k + 1) * tq] for k in range(Q_PER_KV)], axis=0)


def _nsa_kernel(alibi_ref, q_ref, kc_ref, vc_ref, ks_ref, vs_ref, kw_ref, vw_ref, gate_ref, z_ref,
                gq_ref, gks_ref, gkw_ref, o_ref, kca_s, vct_s, ksa_s, vst_s, kwa_s, vwt_s):
    i = pl.program_id(1)
    tq = q_ref.shape[1]
    s = ks_ref.shape[1]
    hd = HEAD_DIM
    ncmp = kc_ref.shape[1]
    nsb = s // NSA_SEL_BLOCK
    n4 = Q_PER_KV * tq

    @pl.when(i == 0)
    def _():
        pos = _iota((s, hd), 0)
        _augment_keys(ks_ref, gks_ref, _key_features(pos, sel_shift=NSA_SEL_BLOCK.bit_length() - 1), ksa_s)
        _augment_keys(kw_ref, gkw_ref, _key_features(pos), kwa_s)
        _store_chunked_t(vs_ref[0], vst_s, tq)
        _store_chunked_t(vw_ref[0], vwt_s, tq)
        cfeat = _key_features(_iota((ncmp, hd), 0) * NSA_CMP_STRIDE + (NSA_CMP_LEN - 1))
        for g in range(KV_HEADS):
            kca_s[g] = jnp.concatenate([kc_ref[0, :, g * hd:(g + 1) * hd].astype(BF16), cfeat], axis=1)
        vct_s[...] = vc_ref[0].T.astype(BF16)

    t0 = i * tq
    q_t = q_ref[0].T
    gates_t = _sigmoid(gate_ref[0]).T
    diag_bias = _causal_bias(tq, tq, Q_PER_KV)
    edge_bias = _window_edge_bias(tq, tq, Q_PER_KV)
    cend = _iota((ncmp, tq), 0) * NSA_CMP_STRIDE + (NSA_CMP_LEN - 1)
    seen = _tile_lanes(cend <= t0 + _iota((ncmp, tq), 1), Q_PER_KV)
    cstart = _iota((nsb, ncmp), 1) * NSA_CMP_STRIDE
    bstart = _iota((nsb, ncmp), 0) * NSA_SEL_BLOCK
    overlap_t = jnp.where((cstart < bstart + NSA_SEL_BLOCK) & (cstart + NSA_CMP_LEN > bstart), 1.0, 0.0)
    blk = _iota((nsb, tq), 0)
    cur = (t0 + _iota((nsb, tq), 1)) >> (NSA_SEL_BLOCK.bit_length() - 1)
    forced = (blk == 0) | (blk == cur) | (blk == cur - 1)
    outs = []
    for g in range(KV_HEADS):
        qn, alibi = _group_query(q_t, alibi_ref, g, gq_ref[...])

        sc = jnp.where(seen, _dot(kca_s[g], _augment_query(qn, alibi)), -BIG)
        e = jnp.where(seen, jnp.exp2(sc - jnp.max(sc, axis=0, keepdims=True)), 0.0)
        p_c = e / jnp.maximum(jnp.sum(e, axis=0, keepdims=True), TINY)
        o_cmp = _dot(vct_s[g * hd:(g + 1) * hd, :], p_c.astype(BF16))

        psum = p_c[:, 0:tq]
        for k in range(1, Q_PER_KV):
            psum = psum + p_c[:, k * tq:(k + 1) * tq]
        imp = _dot_exact(overlap_t, psum)
        imp = jnp.where(blk <= cur, jnp.where(forced, NSA_FORCE, imp), NEG)
        chosen = _rank_select(imp, blk, nsb, min(NSA_TOPN, nsb))
        sel_bias = _tile_lanes(jnp.where(chosen, 0.0, -BIG), Q_PER_KV)
        qa = _augment_query(qn, alibi, sel_bias)

        def body(n, carry, g=g, qa=qa):
            off = pl.multiple_of(n * tq, tq)
            return _flash_step(carry, _dot(ksa_s[g, pl.ds(off, tq), :], qa), vst_s[n, g * hd:(g + 1) * hd, :])

        carry = lax.fori_loop(0, i, body, _flash_init(n4))
        own = _dot(ksa_s[g, pl.ds(pl.multiple_of(i * tq, tq), tq), :], qa) + diag_bias
        m, l, acc = _flash_step(carry, own, vst_s[i, g * hd:(g + 1) * hd, :])
        o_slc = acc / jnp.maximum(l, TINY)

        m, l, acc = _banded(qa, kwa_s, vwt_s, g, i, tq, NSA_WINDOW, diag_bias, edge_bias)
        o_win = acc / jnp.maximum(l, TINY)

        def gate_row(branch):
            rows = [gates_t[branch * N_HEADS + h:branch * N_HEADS + h + 1, :]
                    for h in range(Q_PER_KV * g, Q_PER_KV * (g + 1))]
            return jnp.concatenate(rows, axis=1)

        o = gate_row(0) * o_cmp + gate_row(1) * o_slc + gate_row(2) * o_win
        outs.append(_regroup_heads(o, tq))
    y = jnp.concatenate(outs, axis=0).T * _silu(z_ref[0])
    o_ref[0] = y.astype(o_ref.dtype)


def _nsa(proj, kcn, vc, gq, gks, gkw, *, tq=128):
    bsz, s, _ = proj.shape
    hd = HEAD_DIM
    gw = GROUP_WIDTH
    ncmp = kcn.shape[1]
    assert SEL_FEAT0 + s // NSA_SEL_BLOCK <= hd and ncmp == LANES
    wide = lambda c0: pl.BlockSpec((1, tq, gw), lambda b, i: (b, i, c0 // 4))
    full = lambda c0: pl.BlockSpec((1, s, LANES), lambda b, i: (b, 0, c0))
    cmp = pl.BlockSpec((1, ncmp, LANES), lambda b, i: (b, 0, 0))
    row = pl.BlockSpec((1, hd), lambda b, i: (0, 0))
    ka = pltpu.VMEM((KV_HEADS, s, LANES), BF16)
    vt = pltpu.VMEM((s // tq, LANES, tq), BF16)
    return pl.pallas_call(
        _nsa_kernel,
        grid=(bsz, s // tq),
        in_specs=[pl.BlockSpec((N_HEADS, 2 * SLOPE_PIECES, tq), lambda b, i: (0, 0, 0)),
                  wide(ODD_QC), cmp, cmp, full(ODD_KS), full(ODD_VS), full(ODD_KW), full(ODD_VW),
                  pl.BlockSpec((1, tq, LANES), lambda b, i: (b, i, ODD_GC)), wide(ODD_ZC),
                  pl.BlockSpec((hd, 1), lambda b, i: (0, 0)), row, row],
        out_specs=pl.BlockSpec((1, tq, gw), lambda b, i: (b, i, 0)),
        out_shape=jax.ShapeDtypeStruct((bsz, s, gw), BF16),
        scratch_shapes=[pltpu.VMEM((KV_HEADS, ncmp, LANES), BF16), pltpu.VMEM((LANES, ncmp), BF16),
                        ka, vt, ka, vt],
        name="nsa",
    )(jnp.asarray(_alibi_query_features(tq)), proj, kcn, vc, proj, proj, proj, proj, proj, proj,
      gq.reshape(hd, 1), gks.reshape(1, hd), gkw.reshape(1, hd))


def _swa_kernel(sinks_ref, alibi_ref, q_ref, k_ref, v_ref, z_ref, gq_ref, gk_ref, o_ref, ka_s, vt_s):
    i = pl.program_id(1)
    tq = q_ref.shape[1]
    s = k_ref.shape[1]

    @pl.when(i == 0)
    def _():
        _augment_keys(k_ref, gk_ref, _key_features(_iota((s, HEAD_DIM), 0)), ka_s)
        _store_chunked_t(v_ref[0], vt_s, tq)

    q_t = q_ref[0].T
    diag_bias = _causal_bias(tq, tq, Q_PER_KV)
    edge_bias = _window_edge_bias(tq, tq, Q_PER_KV)
    tpos = (i * tq + _iota((1, tq), 1)).astype(F32)
    outs = []
    for g in range(KV_HEADS):
        qn, alibi = _group_query(q_t, alibi_ref, g, gq_ref[...])
        m, l, acc = _banded(_augment_query(qn, alibi), ka_s, vt_s, g, i, tq, SWA_WINDOW, diag_bias, edge_bias)
        sink = jnp.concatenate([sinks_ref[h] * LOG2E + (SLOPES[h] * LOG2E) * tpos
                                for h in range(Q_PER_KV * g, Q_PER_KV * (g + 1))], axis=1)
        mf = jnp.maximum(m, sink)
        alpha = jnp.exp2(m - mf)
        o = acc * alpha / jnp.maximum(l * alpha + jnp.exp2(sink - mf), TINY)
        outs.append(_regroup_heads(o, tq))
    y = jnp.concatenate(outs, axis=0).T * _silu(z_ref[0])
    o_ref[0] = y.astype(o_ref.dtype)


def _swa(proj, sinks, gq, gk, *, tq=128):
    bsz, s, _ = proj.shape
    hd = HEAD_DIM
    gw = GROUP_WIDTH
    wide = lambda c0: pl.BlockSpec((1, tq, gw), lambda b, i: (b, i, c0 // 4))
    full = lambda c0: pl.BlockSpec((1, s, LANES), lambda b, i: (b, 0, c0))
    return pl.pallas_call(
        _swa_kernel,
        grid=(bsz, s // tq),
        in_specs=[pl.BlockSpec(memory_space=pltpu.SMEM),
                  pl.BlockSpec((N_HEADS, 2 * SLOPE_PIECES, tq), lambda b, i: (0, 0, 0)),
                  wide(ODD_QD), full(ODD_KD), full(ODD_VD), wide(ODD_ZD),
                  pl.BlockSpec((hd, 1), lambda b, i: (0, 0)), pl.BlockSpec((1, hd), lambda b, i: (0, 0))],
        out_specs=pl.BlockSpec((1, tq, gw), lambda b, i: (b, i, 0)),
        out_shape=jax.ShapeDtypeStruct((bsz, s, gw), BF16),
        scratch_shapes=[pltpu.VMEM((KV_HEADS, s, LANES), BF16), pltpu.VMEM((s // tq, LANES, tq), BF16)],
        name="swa",
    )(sinks.astype(F32), jnp.asarray(_alibi_query_features(tq)), proj, proj, proj, proj,
      gq.reshape(hd, 1), gk.reshape(1, hd))


def _pack_odd_weight(w):
    gw, kw, ng = GROUP_WIDTH, KV_HEADS * HEAD_DIM, 3 * N_HEADS
    sizes = [gw, kw, kw, kw, kw, kw, kw, ng, gw, gw, kw, kw, gw]
    starts = np.concatenate([[0], np.cumsum(sizes)]).tolist()
    part = lambda k: w[:, starts[k]:starts[k + 1]]
    pad = jnp.zeros((w.shape[0], LANES - ng), w.dtype)
    order = [0, 8, 9, 12, 1, 2, 3, 4, 5, 6, 10, 11, 7]
    return jnp.concatenate([part(k) for k in order] + [pad], axis=1).astype(BF16)


def kernel(x, norm_g, w_out, e_w_in, a_conv_w, a_conv_b, a_ln_g, a_ln_b, b_qnorm_g, b_knorm_g, o_w_in, c_qnorm_g, c_knorm_cmp_g, c_knorm_slc_g, c_knorm_win_g, c_pos_k, c_pos_v, c_k_w1, c_k_b1, c_k_w2, c_k_b2, c_v_w1, c_v_b1, c_v_w2, c_v_b2, d_qnorm_g, d_knorm_g, d_sinks):
    bsz, s, d = x.shape
    m = bsz * s
    assert s % MOBA_BLOCK == 0 and d == 2 * GROUP_WIDTH
    x2 = x.reshape(m, d)

    proj = _inproj(x2, norm_g[0], e_w_in[0].astype(BF16), chunk=GROUP_WIDTH).reshape(bsz, s, -1)
    y_a = _conv_module(proj, a_conv_w[0], a_conv_b[0], a_ln_g[0], a_ln_b[0])
    y_b = _moba(proj, b_qnorm_g[0], b_knorm_g[0])
    x2 = _outproj(x2, y_a.reshape(m, -1), y_b.reshape(m, -1), w_out[0].astype(BF16))

    proj = _inproj(x2, norm_g[1], _pack_odd_weight(o_w_in[0]), chunk=5 * LANES).reshape(bsz, s, ODD_COLS)
    rows16 = lambda c0: proj[:, :, c0 * LANES:(c0 + 1) * LANES].reshape(bsz, s // NSA_CMP_STRIDE,
                                                                       NSA_CMP_STRIDE * LANES)
    kcn, vc = _compress(rows16(ODD_KC), rows16(ODD_VC), c_pos_k[0], c_pos_v[0], c_k_w1[0], c_k_b1[0],
                        c_k_w2[0], c_k_b2[0], c_v_w1[0], c_v_b1[0], c_v_w2[0], c_v_b2[0], c_knorm_cmp_g[0])
    y_c = _nsa(proj, kcn, vc, c_qnorm_g[0], c_knorm_slc_g[0], c_knorm_win_g[0])
    y_d = _swa(proj, d_sinks[0], d_qnorm_g[0], d_knorm_g[0])
    x2 = _outproj(x2, y_c.reshape(m, -1), y_d.reshape(m, -1), w_out[1].astype(BF16))
    return x2.reshape(bsz, s, d)
```

```python
import functools

import ml_dtypes
import numpy as np
import jax
import jax.numpy as jnp
from jax import lax
from jax.experimental import pallas as pl
from jax.experimental.pallas import tpu as pltpu

HEAD_DIM = 64
N_HEADS = 8
GROUP_WIDTH = N_HEADS * HEAD_DIM
CONV_WIDTH = 31
MOBA_BLOCK = 256
MOBA_TOPK = 3
KV_HEADS = 2
Q_PER_KV = N_HEADS // KV_HEADS
NSA_CMP_LEN = 32
NSA_CMP_STRIDE = 16
NSA_CMP_HIDDEN = 256
NSA_SEL_BLOCK = 64
NSA_TOPN = 16
NSA_WINDOW = 512
NSA_FORCE = 1e4
SWA_WINDOW = 128
EPS = 1e-6
NEG = -1e30
TINY = 1e-30
LANES = 128
CONV_HALO = 32

LOG2E = float(np.log2(np.e))
BIG = 2.0 ** 99
M_INIT = -1e38
POS_SPLIT_SHIFT = 8
SLOPE_PIECES = 4
SEL_FEAT0 = 2 * SLOPE_PIECES

F32 = jnp.float32
BF16 = jnp.bfloat16

ODD_QC, ODD_ZC, ODD_QD, ODD_ZD = 0, 4, 8, 12
ODD_KC, ODD_VC, ODD_KS, ODD_VS, ODD_KW, ODD_VW, ODD_KD, ODD_VD, ODD_GC = 16, 17, 18, 19, 20, 21, 22, 23, 24
ODD_COLS = 25 * LANES


def _alibi_slopes(n):
    return [float(2.0 ** (-8.0 * (i + 1) / n)) for i in range(n)]


SLOPES = _alibi_slopes(N_HEADS)


def _alibi_query_features(width):
    table = np.zeros((N_HEADS, 2 * SLOPE_PIECES, width), np.float32)
    for h, slope in enumerate(SLOPES):
        rest = np.float64(slope) * LOG2E
        for k in range(SLOPE_PIECES):
            piece = float(np.float32(rest).astype(ml_dtypes.bfloat16).astype(np.float32))
            table[h, k] = piece
            table[h, SLOPE_PIECES + k] = piece
            rest -= piece
    return table


def _dot(a, b):
    return jnp.dot(a, b, preferred_element_type=F32)


def _dot_exact(a, b):
    return jnp.dot(a, b, preferred_element_type=F32, precision=lax.Precision.HIGHEST)


def _sigmoid(x):
    return 1.0 / (1.0 + jnp.exp(-x))


def _silu(x):
    return x * _sigmoid(x)


def _rms(x, g):
    return x * lax.rsqrt(jnp.mean(x * x, axis=-1, keepdims=True) + EPS) * g


def _iota(shape, dim):
    return lax.broadcasted_iota(jnp.int32, shape, dim)


def _inproj_kernel(x_ref, g_ref, w_ref, o_ref, *, chunk):
    x = x_ref[...]
    h = _rms(x, g_ref[...]).astype(BF16)
    for c in range(o_ref.shape[1] // chunk):
        o_ref[:, c * chunk:(c + 1) * chunk] = _dot(h, w_ref[:, c * chunk:(c + 1) * chunk])


def _inproj(x2d, g, w, *, tm=256, chunk):
    m, d = x2d.shape
    e = w.shape[1]
    return pl.pallas_call(
        functools.partial(_inproj_kernel, chunk=chunk),
        grid=(m // tm,),
        in_specs=[pl.BlockSpec((tm, d), lambda i: (i, 0)),
                  pl.BlockSpec((1, d), lambda i: (0, 0)),
                  pl.BlockSpec((d, e), lambda i: (0, 0))],
        out_specs=pl.BlockSpec((tm, e), lambda i: (i, 0)),
        out_shape=jax.ShapeDtypeStruct((m, e), F32),
        name="inproj",
    )(x2d, g.reshape(1, d), w)


def _outproj_kernel(x_ref, ya_ref, yb_ref, wa_ref, wb_ref, o_ref):
    o_ref[...] = x_ref[...] + _dot(ya_ref[...], wa_ref[...]) + _dot(yb_ref[...], wb_ref[...])


def _outproj(x2d, ya, yb, w, *, tm=512):
    m, d = x2d.shape
    gw = ya.shape[1]
    return pl.pallas_call(
        _outproj_kernel,
        grid=(m // tm,),
        in_specs=[pl.BlockSpec((tm, d), lambda i: (i, 0)),
                  pl.BlockSpec((tm, gw), lambda i: (i, 0)),
                  pl.BlockSpec((tm, gw), lambda i: (i, 0)),
                  pl.BlockSpec((gw, d), lambda i: (0, 0)),
                  pl.BlockSpec((gw, d), lambda i: (0, 0))],
        out_specs=pl.BlockSpec((tm, d), lambda i: (i, 0)),
        out_shape=jax.ShapeDtypeStruct((m, d), F32),
        name="outproj",
    )(x2d, ya, yb, w[:gw], w[gw:])


def _conv_kernel(uv_ref, ug_ref, uvh_ref, ugh_ref, z_ref, w_ref, b_ref, lg_ref, lb_ref, o_ref, h_s, *, rows):
    i = pl.program_id(1)
    ts = uv_ref.shape[1]
    halo = uvh_ref[0] * _sigmoid(ugh_ref[0])
    h_s[0:CONV_HALO, :] = jnp.where(i > 0, halo, 0.0)
    h_s[CONV_HALO:, :] = uv_ref[0] * _sigmoid(ug_ref[0])
    base = CONV_HALO - (CONV_WIDTH - 1)
    for c in range(ts // rows):
        acc = jnp.zeros((rows, GROUP_WIDTH), F32)
        for j in range(CONV_WIDTH):
            acc = acc + w_ref[j:j + 1, :] * h_s[pl.ds(c * rows + base + j, rows), :]
        y = acc + b_ref[...]
        mu = jnp.mean(y, axis=-1, keepdims=True)
        yc = y - mu
        y = yc * lax.rsqrt(jnp.mean(yc * yc, axis=-1, keepdims=True) + EPS) * lg_ref[...] + lb_ref[...]
        out = _silu(y) * _silu(z_ref[0, c * rows:(c + 1) * rows, :])
        o_ref[0, c * rows:(c + 1) * rows, :] = out.astype(o_ref.dtype)


def _conv_module(proj, w, b, lg, lb, *, ts=256, rows=64):
    bsz, s, _ = proj.shape
    gw = GROUP_WIDTH
    hb = ts // CONV_HALO
    cur = lambda col: pl.BlockSpec((1, ts, gw), lambda bi, i: (bi, i, col))
    halo = lambda col: pl.BlockSpec((1, CONV_HALO, gw), lambda bi, i: (bi, jnp.maximum(i * hb - 1, 0), col))
    vec = pl.BlockSpec((1, gw), lambda bi, i: (0, 0))
    return pl.pallas_call(
        functools.partial(_conv_kernel, rows=rows),
        grid=(bsz, s // ts),
        in_specs=[cur(0), cur(1), halo(0), halo(1), cur(2),
                  pl.BlockSpec((CONV_WIDTH, gw), lambda bi, i: (0, 0)), vec, vec, vec],
        out_specs=pl.BlockSpec((1, ts, gw), lambda bi, i: (bi, i, 0)),
        out_shape=jax.ShapeDtypeStruct((bsz, s, gw), BF16),
        scratch_shapes=[pltpu.VMEM((CONV_HALO + ts, gw), F32)],
        name="conv_module",
    )(proj, proj, proj, proj, proj, w, b.reshape(1, gw), lg.reshape(1, gw), lb.reshape(1, gw))


def _key_features(pos, sel_shift=None):
    col = _iota(pos.shape, 1)
    hi = (pos >> POS_SPLIT_SHIFT) << POS_SPLIT_SHIFT
    lo = pos & ((1 << POS_SPLIT_SHIFT) - 1)
    feat = jnp.where(col < SLOPE_PIECES, hi, jnp.where(col < 2 * SLOPE_PIECES, lo, 0))
    if sel_shift is not None:
        feat = jnp.where(col - SEL_FEAT0 == (pos >> sel_shift), 1, feat)
    return feat.astype(F32).astype(BF16)


def _norm_query_t(x, g_col):
    ss = jnp.mean(x * x, axis=0, keepdims=True)
    return x * lax.rsqrt(ss + EPS) * g_col * (HEAD_DIM ** -0.5 * LOG2E)


def _augment_query(q_t, alibi, sel_bias=None):
    n = q_t.shape[1]
    parts = [q_t, alibi]
    used = alibi.shape[0]
    if sel_bias is not None:
        parts.append(sel_bias)
        used += sel_bias.shape[0]
    parts.append(jnp.zeros((HEAD_DIM - used, n), F32))
    return jnp.concatenate(parts, axis=0).astype(BF16)


def _flash_init(n):
    return (jnp.full((1, n), M_INIT, F32), jnp.zeros((1, n), F32), jnp.zeros((HEAD_DIM, n), F32))


def _flash_step(carry, sc, v_t):
    m, l, acc = carry
    m_new = jnp.maximum(m, jnp.max(sc, axis=0, keepdims=True))
    p = jnp.exp2(sc - m_new)
    alpha = jnp.exp2(m - m_new)
    l = alpha * l + jnp.sum(p, axis=0, keepdims=True)
    acc = alpha * acc + _dot(v_t, p.astype(BF16))
    return m_new, l, acc


def _flash_steps(carries, chunks, cap=None):
    return tuple(_flash_step(carry, sc if cap is None else jnp.minimum(sc, cap), v_t)
                 for carry, (sc, v_t) in zip(carries, chunks))


def _rank_select(score, blk, limit, count):
    rank = jnp.zeros(score.shape, jnp.int32)
    for mm in range(limit):
        gm = score[mm:mm + 1, :]
        beats = (gm > score) | ((gm == score) & (mm < blk))
        rank = rank + jnp.where(beats, 1, 0)
    return rank < count


def _tile_lanes(x, reps):
    return jnp.concatenate([x] * reps, axis=1) if reps > 1 else x


def _causal_cap(tk, tq, reps):
    keep = _iota((tk, tq), 0) <= _iota((tk, tq), 1)
    return _tile_lanes(jnp.where(keep, BIG, -BIG), reps)


def _window_edge_cap(tk, tq, reps):
    keep = _iota((tk, tq), 0) > _iota((tk, tq), 1)
    return _tile_lanes(jnp.where(keep, BIG, -BIG), reps)


def _store_chunked_t(x, dst_ref, tk):
    x_t = x.T.astype(BF16)
    for c in range(x.shape[0] // tk):
        dst_ref[c] = x_t[:, c * tk:(c + 1) * tk]


def _moba_kernel(alibi_ref, q_ref, k_ref, v_ref, z_ref, gq_ref, gk_ref, o_ref, ka_s, vt_s, km_s):
    i = pl.program_id(2)
    s = k_ref.shape[1]
    nblk = s // MOBA_BLOCK
    tq = MOBA_BLOCK
    hd = HEAD_DIM
    heads = range(ka_s.shape[0])

    @pl.when(i == 0)
    def _():
        feat = _key_features(_iota((s, hd), 0), sel_shift=MOBA_BLOCK.bit_length() - 1)
        for j in heads:
            kn = _rms(k_ref[0, :, j * hd:(j + 1) * hd], gk_ref[...])
            ka_s[j] = jnp.concatenate([kn.astype(BF16), feat], axis=1)
            km_s[j] = jnp.mean(kn.reshape(nblk, MOBA_BLOCK, hd), axis=1)
        _store_chunked_t(v_ref[0], vt_s, tq)

    q_t = q_ref[0].T
    blk = _iota((nblk, tq), 0)
    qas = []
    for j in heads:
        qn = _norm_query_t(q_t[j * hd:(j + 1) * hd], gq_ref[...])
        gate = jnp.where(blk < i, _dot_exact(km_s[j], qn), NEG)
        chosen = _rank_select(gate, blk, nblk, MOBA_TOPK) & (blk < i)
        sel_bias = jnp.where(chosen | (blk == i), 0.0, -BIG)
        qas.append(_augment_query(qn, alibi_ref[j], sel_bias))

    def chunk(j, n):
        off = pl.multiple_of(n * tq, tq)
        return _dot(ka_s[j, pl.ds(off, tq), :], qas[j]), vt_s[n, j * hd:(j + 1) * hd, :]

    def body(n, carries):
        return _flash_steps(carries, [chunk(j, n) for j in heads])

    carries = lax.fori_loop(0, i, body, tuple(_flash_init(tq) for _ in heads))
    diag_cap = _causal_cap(tq, tq, 1)
    carries = lax.fori_loop(i, i + 1, lambda n, cr: _flash_steps(cr, [chunk(j, n) for j in heads], diag_cap),
                            carries)
    outs = [acc / jnp.maximum(l, TINY) for m, l, acc in carries]
    y = jnp.concatenate(outs, axis=0).T * _silu(z_ref[0])
    o_ref[0] = y.astype(o_ref.dtype)


def _moba(proj, gq, gk, *, heads_per_step=4):
    bsz, s, _ = proj.shape
    tq = MOBA_BLOCK
    hd = HEAD_DIM
    nblk = s // MOBA_BLOCK
    hps = heads_per_step
    wd = hps * hd
    assert SEL_FEAT0 + nblk <= hd and wd % LANES == 0
    q0, k0, v0, z0 = (c * GROUP_WIDTH // wd for c in (3, 4, 5, 6))
    tile = lambda c0: pl.BlockSpec((1, tq, wd), lambda b, p, i: (b, i, c0 + p))
    full = lambda c0: pl.BlockSpec((1, s, wd), lambda b, p, i: (b, 0, c0 + p))
    return pl.pallas_call(
        _moba_kernel,
        grid=(bsz, N_HEADS // hps, s // tq),
        in_specs=[pl.BlockSpec((hps, 2 * SLOPE_PIECES, tq), lambda b, p, i: (p, 0, 0)),
                  tile(q0), full(k0), full(v0), tile(z0),
                  pl.BlockSpec((hd, 1), lambda b, p, i: (0, 0)), pl.BlockSpec((1, hd), lambda b, p, i: (0, 0))],
        out_specs=pl.BlockSpec((1, tq, wd), lambda b, p, i: (b, i, p)),
        out_shape=jax.ShapeDtypeStruct((bsz, s, GROUP_WIDTH), BF16),
        scratch_shapes=[pltpu.VMEM((hps, s, LANES), BF16), pltpu.VMEM((nblk, wd, tq), BF16),
                        pltpu.VMEM((hps, nblk, hd), F32)],
        name="moba",
    )(jnp.asarray(_alibi_query_features(tq)), proj, proj, proj, proj, gq.reshape(hd, 1), gk.reshape(1, hd))


def _cmp_kernel(xk_ref, xv_ref, wka_ref, wkb_ref, wva_ref, wvb_ref, pk_ref, pv_ref, bk1_ref, bv1_ref,
                wk2_ref, wv2_ref, bk2_ref, bv2_ref, gn_ref, ko_ref, vo_ref):
    half = xk_ref.shape[2]
    n = xk_ref.shape[1]

    def hidden(x_ref, wa_ref, wb_ref, pos_ref, b1_ref):
        x = x_ref[0].astype(BF16)
        first = _dot(x, wa_ref[...])
        second = _dot(x, wb_ref[...])
        pos = jnp.broadcast_to(pos_ref[...], (8, 2 * half)).astype(BF16)
        ph = _dot(pos[:, :half], wa_ref[...]) + _dot(pos[:, half:], wb_ref[...])
        return _silu(first + pltpu.roll(second, n - 1, 0) + ph[0:1, :] + b1_ref[...])

    hk = hidden(xk_ref, wka_ref, wkb_ref, pk_ref, bk1_ref).astype(BF16)
    hv = hidden(xv_ref, wva_ref, wvb_ref, pv_ref, bv1_ref).astype(BF16)
    nh = NSA_CMP_HIDDEN
    ks, vs = [], []
    for g in range(KV_HEADS):
        ks.append(_rms(_dot(hk[:, g * nh:(g + 1) * nh], wk2_ref[...]) + bk2_ref[...], gn_ref[...]))
        vs.append(_dot(hv[:, g * nh:(g + 1) * nh], wv2_ref[...]) + bv2_ref[...])
    ko_ref[0] = jnp.concatenate(ks, axis=1)
    vo_ref[0] = jnp.concatenate(vs, axis=1)


def _expand_w1(w1):
    hid = w1.shape[1]
    w = w1.reshape(NSA_CMP_LEN, HEAD_DIM, hid)
    eye = jnp.eye(KV_HEADS, dtype=w1.dtype)
    w = jnp.einsum('ldn,gh->lgdhn', w, eye).reshape(NSA_CMP_LEN * KV_HEADS * HEAD_DIM, KV_HEADS * hid)
    half = w.shape[0] // 2
    return w[:half].astype(BF16), w[half:].astype(BF16)


def _compress(xk, xv, pos_k, pos_v, kw1, kb1, kw2, kb2, vw1, vb1, vw2, vb2, gn):
    bsz, n, half = xk.shape
    hd = HEAD_DIM
    wka, wkb = _expand_w1(kw1)
    wva, wvb = _expand_w1(vw1)
    tile_pos = lambda p: jnp.tile(p[:, None, :], (1, KV_HEADS, 1)).reshape(1, 2 * half)
    tile_b = lambda b: jnp.tile(b.reshape(1, -1), (1, KV_HEADS))
    const = lambda shape: pl.BlockSpec(shape, lambda b: (0,) * len(shape))
    xs = pl.BlockSpec((1, n, half), lambda b: (b, 0, 0))
    hw = KV_HEADS * NSA_CMP_HIDDEN
    out = pl.BlockSpec((1, n, KV_HEADS * hd), lambda b: (b, 0, 0))
    return pl.pallas_call(
        _cmp_kernel,
        grid=(bsz,),
        in_specs=[xs, xs, const((half, hw)), const((half, hw)), const((half, hw)), const((half, hw)),
                  const((1, 2 * half)), const((1, 2 * half)), const((1, hw)), const((1, hw)),
                  const((NSA_CMP_HIDDEN, hd)), const((NSA_CMP_HIDDEN, hd)), const((1, hd)), const((1, hd)),
                  const((1, hd))],
        out_specs=[out, out],
        out_shape=[jax.ShapeDtypeStruct((bsz, n, KV_HEADS * hd), F32)] * 2,
        name="nsa_compress",
    )(xk, xv, wka, wkb, wva, wvb, tile_pos(pos_k), tile_pos(pos_v), tile_b(kb1), tile_b(vb1),
      kw2.astype(BF16), vw2.astype(BF16), kb2.reshape(1, hd), vb2.reshape(1, hd), gn.reshape(1, hd))


def _group_query(q_t, alibi_ref, g, gq_col):
    hd = HEAD_DIM
    heads = range(Q_PER_KV * g, Q_PER_KV * (g + 1))
    qn = jnp.concatenate([_norm_query_t(q_t[h * hd:(h + 1) * hd], gq_col) for h in heads], axis=1)
    alibi = jnp.concatenate([alibi_ref[h] for h in heads], axis=1)
    return qn, alibi


def _augment_keys(k_ref, gk_ref, feat, dst_ref):
    hd = HEAD_DIM
    for g in range(KV_HEADS):
        kn = _rms(k_ref[0, :, g * hd:(g + 1) * hd], gk_ref[...])
        dst_ref[g] = jnp.concatenate([kn.astype(BF16), feat], axis=1)


def _chunk_scores(ka_ref, vt_ref, g, c, qa, tq):
    off = pl.multiple_of(c * tq, tq)
    return _dot(ka_ref[g, pl.ds(off, tq), :], qa), vt_ref[c, g * HEAD_DIM:(g + 1) * HEAD_DIM, :]


def _window_first_chunk(ka_ref, vt_ref, g, far, qa, tq, edge_cap):
    sc, v_t = _chunk_scores(ka_ref, vt_ref, g, jnp.maximum(far, 0), qa, tq)
    return jnp.minimum(sc, jnp.minimum(edge_cap, jnp.where(far >= 0, BIG, -BIG))), v_t


def _regroup_heads(o_t, tq):
    return jnp.concatenate([o_t[:, k * tq:(k + 1) * tq] for k in range(Q_PER_KV)], axis=0)


def _nsa_kernel(alibi_ref, q_ref, kc_ref, vc_ref, ks_ref, vs_ref, kw_ref, vw_ref, gate_ref, z_ref,
                gq_ref, gks_ref, gkw_ref, o_ref, kca_s, vct_s, ksa_s, vst_s, kwa_s, vwt_s):
    i = pl.program_id(1)
    tq = q_ref.shape[1]
    s = ks_ref.shape[1]
    hd = HEAD_DIM
    ncmp = kc_ref.shape[1]
    nsb = s // NSA_SEL_BLOCK
    n4 = Q_PER_KV * tq

    @pl.when(i == 0)
    def _():
        pos = _iota((s, hd), 0)
        _augment_keys(ks_ref, gks_ref, _key_features(pos, sel_shift=NSA_SEL_BLOCK.bit_length() - 1), ksa_s)
        _augment_keys(kw_ref, gkw_ref, _key_features(pos), kwa_s)
        _store_chunked_t(vs_ref[0], vst_s, tq)
        _store_chunked_t(vw_ref[0], vwt_s, tq)
        cfeat = _key_features(_iota((ncmp, hd), 0) * NSA_CMP_STRIDE + (NSA_CMP_LEN - 1))
        for g in range(KV_HEADS):
            kca_s[g] = jnp.concatenate([kc_ref[0, :, g * hd:(g + 1) * hd].astype(BF16), cfeat], axis=1)
        vct_s[...] = vc_ref[0].T.astype(BF16)

    t0 = i * tq
    q_t = q_ref[0].T
    gates_t = _sigmoid(gate_ref[0]).T
    diag_cap = _causal_cap(tq, tq, Q_PER_KV)
    edge_cap = _window_edge_cap(tq, tq, Q_PER_KV)
    cend =_iota((ncmp, tq), 0) * NSA_CMP_STRIDE + (NSA_CMP_LEN - 1)
    seen = _tile_lanes(cend <= t0 + _iota((ncmp, tq), 1), Q_PER_KV)
    cstart = _iota((nsb, ncmp), 1) * NSA_CMP_STRIDE
    bstart = _iota((nsb, ncmp), 0) * NSA_SEL_BLOCK
    overlap_t = jnp.where((cstart < bstart + NSA_SEL_BLOCK) & (cstart + NSA_CMP_LEN > bstart), 1.0, 0.0)
    blk = _iota((nsb, tq), 0)
    cur = (t0 + _iota((nsb, tq), 1)) >> (NSA_SEL_BLOCK.bit_length() - 1)
    forced = (blk == 0) | (blk == cur) | (blk == cur - 1)
    groups = range(KV_HEADS)
    far = i - NSA_WINDOW // tq
    lo = jnp.maximum(far + 1, 0)

    queries = [_group_query(q_t, alibi_ref, g, gq_ref[...]) for g in groups]
    qa_win = [_augment_query(qn, alibi) for qn, alibi in queries]
    first = [_window_first_chunk(kwa_s, vwt_s, g, far, qa_win[g], tq, edge_cap) for g in groups]
    cmp_sc = [_dot(kca_s[g], qa_win[g]) for g in groups]
    win = _flash_steps([_flash_init(n4) for _ in groups], first)
    qa_sel, o_cmp = [], []
    for g in groups:
        qn, alibi = queries[g]
        sc = jnp.where(seen, cmp_sc[g], -BIG)
        e = jnp.where(seen, jnp.exp2(sc - jnp.max(sc, axis=0, keepdims=True)), 0.0)
        p_c = e / jnp.maximum(jnp.sum(e, axis=0, keepdims=True), TINY)
        o_cmp.append(_dot(vct_s[g * hd:(g + 1) * hd, :], p_c.astype(BF16)))

        psum = p_c[:, 0:tq]
        for k in range(1, Q_PER_KV):
            psum = psum + p_c[:, k * tq:(k + 1) * tq]
        imp = _dot_exact(overlap_t, psum)
        imp = jnp.where(blk <= cur, jnp.where(forced, NSA_FORCE, imp), NEG)
        chosen = _rank_select(imp, blk, nsb, min(NSA_TOPN, nsb))
        sel_bias = _tile_lanes(jnp.where(chosen, 0.0, -BIG), Q_PER_KV)
        qa_sel.append(_augment_query(qn, alibi, sel_bias))

    def sel_chunk(g, c):
        return _chunk_scores(ksa_s, vst_s, g, c, qa_sel[g], tq)

    def win_chunk(g, c):
        return _chunk_scores(kwa_s, vwt_s, g, c, qa_win[g], tq)

    def both_steps(c, carries, cap=None):
        sel, win = carries
        chunks = [sel_chunk(g, c) for g in groups] + [win_chunk(g, c) for g in groups]
        return _flash_steps(tuple(sel) + tuple(win), chunks, cap)

    sel = lax.fori_loop(0, lo, lambda c, cr: _flash_steps(cr, [sel_chunk(g, c) for g in groups]),
                        tuple(_flash_init(n4) for _ in groups))
    split = lambda cr: (cr[:KV_HEADS], cr[KV_HEADS:])
    sel, win = split(lax.fori_loop(lo, i, lambda c, cr: both_steps(c, split(cr)), tuple(sel) + tuple(win)))
    sel, win = split(lax.fori_loop(i, i + 1, lambda c, cr: both_steps(c, split(cr), diag_cap),
                                   tuple(sel) + tuple(win)))

    outs = []
    for g in groups:
        def gate_row(branch):
            rows = [gates_t[branch * N_HEADS + h:branch * N_HEADS + h + 1, :]
                    for h in range(Q_PER_KV * g, Q_PER_KV * (g + 1))]
            return jnp.concatenate(rows, axis=1)

        o_slc = sel[g][2] / jnp.maximum(sel[g][1], TINY)
        o_win = win[g][2] / jnp.maximum(win[g][1], TINY)
        o = gate_row(0) * o_cmp[g] + gate_row(1) * o_slc + gate_row(2) * o_win
        outs.append(_regroup_heads(o, tq))
    y = jnp.concatenate(outs, axis=0).T * _silu(z_ref[0])
    o_ref[0] = y.astype(o_ref.dtype)


def _nsa(proj, kcn, vc, gq, gks, gkw, *, tq=128):
    bsz, s, _ = proj.shape
    hd = HEAD_DIM
    gw = GROUP_WIDTH
    ncmp = kcn.shape[1]
    assert SEL_FEAT0 + s // NSA_SEL_BLOCK <= hd and ncmp == LANES
    wide = lambda c0: pl.BlockSpec((1, tq, gw), lambda b, i: (b, i, c0 // 4))
    full = lambda c0: pl.BlockSpec((1, s, LANES), lambda b, i: (b, 0, c0))
    cmp = pl.BlockSpec((1, ncmp, LANES), lambda b, i: (b, 0, 0))
    row = pl.BlockSpec((1, hd), lambda b, i: (0, 0))
    ka = pltpu.VMEM((KV_HEADS, s, LANES), BF16)
    vt = pltpu.VMEM((s // tq, LANES, tq), BF16)
    return pl.pallas_call(
        _nsa_kernel,
        grid=(bsz, s // tq),
        in_specs=[pl.BlockSpec((N_HEADS, 2 * SLOPE_PIECES, tq), lambda b, i: (0, 0, 0)),
                  wide(ODD_QC), cmp, cmp, full(ODD_KS), full(ODD_VS), full(ODD_KW), full(ODD_VW),
                  pl.BlockSpec((1, tq, LANES), lambda b, i: (b, i, ODD_GC)), wide(ODD_ZC),
                  pl.BlockSpec((hd, 1), lambda b, i: (0, 0)), row, row],
        out_specs=pl.BlockSpec((1, tq, gw), lambda b, i: (b, i, 0)),
        out_shape=jax.ShapeDtypeStruct((bsz, s, gw), BF16),
        scratch_shapes=[pltpu.VMEM((KV_HEADS, ncmp, LANES), BF16), pltpu.VMEM((LANES, ncmp), BF16),
                        ka, vt, ka, vt],
        name="nsa",
    )(jnp.asarray(_alibi_query_features(tq)), proj, kcn, vc, proj, proj, proj, proj, proj, proj,
      gq.reshape(hd, 1), gks.reshape(1, hd), gkw.reshape(1, hd))


def _swa_kernel(sinks_ref, alibi_ref, q_ref, k_ref, v_ref, z_ref, gq_ref, gk_ref, o_ref, ka_s, vt_s):
    i = pl.program_id(1)
    tq = q_ref.shape[1]
    s = k_ref.shape[1]

    @pl.when(i == 0)
    def _():
        _augment_keys(k_ref, gk_ref, _key_features(_iota((s, HEAD_DIM), 0)), ka_s)
        _store_chunked_t(v_ref[0], vt_s, tq)

    q_t = q_ref[0].T
    diag_cap = _causal_cap(tq, tq, Q_PER_KV)
    edge_cap = _window_edge_cap(tq, tq, Q_PER_KV)
    tpos =(i * tq + _iota((1, tq), 1)).astype(F32)
    outs = []
    assert SWA_WINDOW == tq
    groups = range(KV_HEADS)
    qas = [_augment_query(*_group_query(q_t, alibi_ref, g, gq_ref[...])) for g in groups]
    first = [_window_first_chunk(ka_s, vt_s, g, i - 1, qas[g], tq, edge_cap) for g in groups]
    own = [_chunk_scores(ka_s, vt_s, g, i, qas[g], tq) for g in groups]
    carries = _flash_steps([_flash_init(Q_PER_KV * tq) for _ in groups], first)
    carries = _flash_steps(carries, own, diag_cap)
    for g in groups:
        m, l, acc = carries[g]
        sink = jnp.concatenate([sinks_ref[h] * LOG2E + (SLOPES[h] * LOG2E) * tpos
                                for h in range(Q_PER_KV * g, Q_PER_KV * (g + 1))], axis=1)
        mf = jnp.maximum(m, sink)
        alpha = jnp.exp2(m - mf)
        o = acc * alpha / jnp.maximum(l * alpha + jnp.exp2(sink - mf), TINY)
        outs.append(_regroup_heads(o, tq))
    y = jnp.concatenate(outs, axis=0).T * _silu(z_ref[0])
    o_ref[0] = y.astype(o_ref.dtype)


def _swa(proj, sinks, gq, gk, *, tq=128):
    bsz, s, _ = proj.shape
    hd = HEAD_DIM
    gw = GROUP_WIDTH
    wide = lambda c0: pl.BlockSpec((1, tq, gw), lambda b, i: (b, i, c0 // 4))
    full = lambda c0: pl.BlockSpec((1, s, LANES), lambda b, i: (b, 0, c0))
    return pl.pallas_call(
        _swa_kernel,
        grid=(bsz, s // tq),
        in_specs=[pl.BlockSpec(memory_space=pltpu.SMEM),
                  pl.BlockSpec((N_HEADS, 2 * SLOPE_PIECES, tq), lambda b, i: (0, 0, 0)),
                  wide(ODD_QD), full(ODD_KD), full(ODD_VD), wide(ODD_ZD),
                  pl.BlockSpec((hd, 1), lambda b, i: (0, 0)), pl.BlockSpec((1, hd), lambda b, i: (0, 0))],
        out_specs=pl.BlockSpec((1, tq, gw), lambda b, i: (b, i, 0)),
        out_shape=jax.ShapeDtypeStruct((bsz, s, gw), BF16),
        scratch_shapes=[pltpu.VMEM((KV_HEADS, s, LANES), BF16), pltpu.VMEM((s // tq, LANES, tq), BF16)],
        name="swa",
    )(sinks.astype(F32), jnp.asarray(_alibi_query_features(tq)), proj, proj, proj, proj,
      gq.reshape(hd, 1), gk.reshape(1, hd))


def _pack_odd_weight(w):
    gw, kw, ng = GROUP_WIDTH, KV_HEADS * HEAD_DIM, 3 * N_HEADS
    sizes = [gw, kw, kw, kw, kw, kw, kw, ng, gw, gw, kw, kw, gw]
    starts = np.concatenate([[0], np.cumsum(sizes)]).tolist()
    part = lambda k: w[:, starts[k]:starts[k + 1]]
    pad = jnp.zeros((w.shape[0], LANES - ng), w.dtype)
    order = [0, 8, 9, 12, 1, 2, 3, 4, 5, 6, 10, 11, 7]
    return jnp.concatenate([part(k) for k in order] + [pad], axis=1).astype(BF16)


def kernel(x, norm_g, w_out, e_w_in, a_conv_w, a_conv_b, a_ln_g, a_ln_b, b_qnorm_g, b_knorm_g, o_w_in, c_qnorm_g, c_knorm_cmp_g, c_knorm_slc_g, c_knorm_win_g, c_pos_k, c_pos_v, c_k_w1, c_k_b1, c_k_w2, c_k_b2, c_v_w1, c_v_b1, c_v_w2, c_v_b2, d_qnorm_g, d_knorm_g, d_sinks):
    bsz, s, d = x.shape
    m = bsz * s
    assert s % MOBA_BLOCK == 0 and d == 2 * GROUP_WIDTH
    x2 = x.reshape(m, d)

    proj = _inproj(x2, norm_g[0], e_w_in[0].astype(BF16), chunk=GROUP_WIDTH).reshape(bsz, s, -1)
    y_a = _conv_module(proj, a_conv_w[0], a_conv_b[0], a_ln_g[0], a_ln_b[0])
    y_b = _moba(proj, b_qnorm_g[0], b_knorm_g[0])
    x2 = _outproj(x2, y_a.reshape(m, -1), y_b.reshape(m, -1), w_out[0].astype(BF16))

    proj = _inproj(x2, norm_g[1], _pack_odd_weight(o_w_in[0]), chunk=5 * LANES).reshape(bsz, s, ODD_COLS)
    rows16 = lambda c0: proj[:, :, c0 * LANES:(c0 + 1) * LANES].reshape(bsz, s // NSA_CMP_STRIDE,
                                                                       NSA_CMP_STRIDE * LANES)
    kcn, vc = _compress(rows16(ODD_KC), rows16(ODD_VC), c_pos_k[0], c_pos_v[0], c_k_w1[0], c_k_b1[0],
                        c_k_w2[0], c_k_b2[0], c_v_w1[0], c_v_b1[0], c_v_w2[0], c_v_b2[0], c_knorm_cmp_g[0])
    y_c = _nsa(proj, kcn, vc, c_qnorm_g[0], c_knorm_slc_g[0], c_knorm_win_g[0])
    y_d = _swa(proj, d_sinks[0], d_qnorm_g[0], d_knorm_g[0])
    x2 = _outproj(x2, y_c.reshape(m, -1), y_d.reshape(m, -1), w_out[1].astype(BF16))
    return x2.reshape(bsz, s, d)
```

```python
import functools

import ml_dtypes
import numpy as np
import jax
import jax.numpy as jnp
from jax import lax
from jax.experimental import pallas as pl
from jax.experimental.pallas import tpu as pltpu

HEAD_DIM = 64
N_HEADS = 8
GROUP_WIDTH = N_HEADS * HEAD_DIM
CONV_WIDTH = 31
MOBA_BLOCK = 256
MOBA_TOPK = 3
KV_HEADS = 2
Q_PER_KV = N_HEADS // KV_HEADS
NSA_CMP_LEN = 32
NSA_CMP_STRIDE = 16
NSA_CMP_HIDDEN = 256
NSA_SEL_BLOCK = 64
NSA_TOPN = 16
NSA_WINDOW = 512
NSA_FORCE = 1e4
SWA_WINDOW = 128
EPS = 1e-6
NEG = -1e30
TINY = 1e-30
LANES = 128
CONV_HALO = 32

LOG2E = float(np.log2(np.e))
BIG = 2.0 ** 99
M_INIT = -1e38
POS_SPLIT_SHIFT = 8
SLOPE_PIECES = 4
SEL_FEAT0 = 2 * SLOPE_PIECES

F32 = jnp.float32
BF16 = jnp.bfloat16

ODD_QC, ODD_ZC, ODD_QD, ODD_ZD = 0, 4, 8, 12
ODD_KC, ODD_VC, ODD_KS, ODD_VS, ODD_KW, ODD_VW, ODD_KD, ODD_VD, ODD_GC = 16, 17, 18, 19, 20, 21, 22, 23, 24
ODD_COLS = 25 * LANES


def _alibi_slopes(n):
    return [float(2.0 ** (-8.0 * (i + 1) / n)) for i in range(n)]


SLOPES = _alibi_slopes(N_HEADS)


def _alibi_query_features(width):
    table = np.zeros((N_HEADS, 2 * SLOPE_PIECES, width), np.float32)
    for h, slope in enumerate(SLOPES):
        rest = np.float64(slope) * LOG2E
        for k in range(SLOPE_PIECES):
            piece = float(np.float32(rest).astype(ml_dtypes.bfloat16).astype(np.float32))
            table[h, k] = piece
            table[h, SLOPE_PIECES + k] = piece
            rest -= piece
    return table


def _dot(a, b):
    return jnp.dot(a, b, preferred_element_type=F32)


def _dot_exact(a, b):
    return jnp.dot(a, b, preferred_element_type=F32, precision=lax.Precision.HIGHEST)


def _sigmoid(x):
    return 1.0 / (1.0 + jnp.exp(-x))


def _silu(x):
    return x * _sigmoid(x)


def _rms(x, g):
    return x * lax.rsqrt(jnp.mean(x * x, axis=-1, keepdims=True) + EPS) * g


def _iota(shape, dim):
    return lax.broadcasted_iota(jnp.int32, shape, dim)


def _inproj_kernel(x_ref, g_ref, w_ref, o_ref, *, chunk):
    x = x_ref[...]
    h = _rms(x, g_ref[...]).astype(BF16)
    for c in range(o_ref.shape[1] // chunk):
        o_ref[:, c * chunk:(c + 1) * chunk] = _dot(h, w_ref[:, c * chunk:(c + 1) * chunk])


def _inproj(x2d, g, w, *, tm=256, chunk):
    m, d = x2d.shape
    e = w.shape[1]
    return pl.pallas_call(
        functools.partial(_inproj_kernel, chunk=chunk),
        grid=(m // tm,),
        in_specs=[pl.BlockSpec((tm, d), lambda i: (i, 0)),
                  pl.BlockSpec((1, d), lambda i: (0, 0)),
                  pl.BlockSpec((d, e), lambda i: (0, 0))],
        out_specs=pl.BlockSpec((tm, e), lambda i: (i, 0)),
        out_shape=jax.ShapeDtypeStruct((m, e), F32),
        name="inproj",
    )(x2d, g.reshape(1, d), w)


def _outproj_kernel(x_ref, ya_ref, yb_ref, wa_ref, wb_ref, o_ref):
    o_ref[...] = x_ref[...] + _dot(ya_ref[...], wa_ref[...]) + _dot(yb_ref[...], wb_ref[...])


def _outproj(x2d, ya, yb, w, *, tm=512):
    m, d = x2d.shape
    gw = ya.shape[1]
    return pl.pallas_call(
        _outproj_kernel,
        grid=(m // tm,),
        in_specs=[pl.BlockSpec((tm, d), lambda i: (i, 0)),
                  pl.BlockSpec((tm, gw), lambda i: (i, 0)),
                  pl.BlockSpec((tm, gw), lambda i: (i, 0)),
                  pl.BlockSpec((gw, d), lambda i: (0, 0)),
                  pl.BlockSpec((gw, d), lambda i: (0, 0))],
        out_specs=pl.BlockSpec((tm, d), lambda i: (i, 0)),
        out_shape=jax.ShapeDtypeStruct((m, d), F32),
        name="outproj",
    )(x2d, ya, yb, w[:gw], w[gw:])


def _conv_kernel(uv_ref, ug_ref, uvh_ref, ugh_ref, z_ref, w_ref, b_ref, lg_ref, lb_ref, o_ref, h_s, *, rows):
    i = pl.program_id(1)
    ts = uv_ref.shape[1]
    halo = uvh_ref[0] * _sigmoid(ugh_ref[0])
    h_s[0:CONV_HALO, :] = jnp.where(i > 0, halo, 0.0)
    h_s[CONV_HALO:, :] = uv_ref[0] * _sigmoid(ug_ref[0])
    base = CONV_HALO - (CONV_WIDTH - 1)
    for c in range(ts // rows):
        acc = jnp.zeros((rows, GROUP_WIDTH), F32)
        for j in range(CONV_WIDTH):
            acc = acc + w_ref[j:j + 1, :] * h_s[pl.ds(c * rows + base + j, rows), :]
        y = acc + b_ref[...]
        mu = jnp.mean(y, axis=-1, keepdims=True)
        yc = y - mu
        y = yc * lax.rsqrt(jnp.mean(yc * yc, axis=-1, keepdims=True) + EPS) * lg_ref[...] + lb_ref[...]
        out = _silu(y) * _silu(z_ref[0, c * rows:(c + 1) * rows, :])
        o_ref[0, c * rows:(c + 1) * rows, :] = out.astype(o_ref.dtype)


def _conv_module(proj, w, b, lg, lb, *, ts=256, rows=64):
    bsz, s, _ = proj.shape
    gw = GROUP_WIDTH
    hb = ts // CONV_HALO
    cur = lambda col: pl.BlockSpec((1, ts, gw), lambda bi, i: (bi, i, col))
    halo = lambda col: pl.BlockSpec((1, CONV_HALO, gw), lambda bi, i: (bi, jnp.maximum(i * hb - 1, 0), col))
    vec = pl.BlockSpec((1, gw), lambda bi, i: (0, 0))
    return pl.pallas_call(
        functools.partial(_conv_kernel, rows=rows),
        grid=(bsz, s // ts),
        in_specs=[cur(0), cur(1), halo(0), halo(1), cur(2),
                  pl.BlockSpec((CONV_WIDTH, gw), lambda bi, i: (0, 0)), vec, vec, vec],
        out_specs=pl.BlockSpec((1, ts, gw), lambda bi, i: (bi, i, 0)),
        out_shape=jax.ShapeDtypeStruct((bsz, s, gw), BF16),
        scratch_shapes=[pltpu.VMEM((CONV_HALO + ts, gw), F32)],
        name="conv_module",
    )(proj, proj, proj, proj, proj, w, b.reshape(1, gw), lg.reshape(1, gw), lb.reshape(1, gw))


def _key_features(pos, sel_shift=None):
    col = _iota(pos.shape, 1)
    hi = (pos >> POS_SPLIT_SHIFT) << POS_SPLIT_SHIFT
    lo = pos & ((1 << POS_SPLIT_SHIFT) - 1)
    feat = jnp.where(col < SLOPE_PIECES, hi, jnp.where(col < 2 * SLOPE_PIECES, lo, 0))
    if sel_shift is not None:
        feat = jnp.where(col - SEL_FEAT0 == (pos >> sel_shift), 1, feat)
    return feat.astype(F32).astype(BF16)


def _norm_query_t(x, g_col):
    ss = jnp.mean(x * x, axis=0, keepdims=True)
    return x * lax.rsqrt(ss + EPS) * g_col * (HEAD_DIM ** -0.5 * LOG2E)


def _augment_query(q_t, alibi, sel_bias=None):
    n = q_t.shape[1]
    parts = [q_t, alibi]
    used = alibi.shape[0]
    if sel_bias is not None:
        parts.append(sel_bias)
        used += sel_bias.shape[0]
    parts.append(jnp.zeros((HEAD_DIM - used, n), F32))
    return jnp.concatenate(parts, axis=0).astype(BF16)


def _flash_init(n):
    return (jnp.full((1, n), M_INIT, F32), jnp.zeros((1, n), F32), jnp.zeros((HEAD_DIM, n), F32))


def _flash_step(carry, sc, v_t):
    m, l, acc = carry
    m_new = jnp.maximum(m, jnp.max(sc, axis=0, keepdims=True))
    p = jnp.exp2(sc - m_new)
    alpha = jnp.exp2(m - m_new)
    l = alpha * l + jnp.sum(p, axis=0, keepdims=True)
    acc = alpha * acc + _dot(v_t, p.astype(BF16))
    return m_new, l, acc


def _flash_steps(carries, chunks, cap=None):
    return tuple(_flash_step(carry, sc if cap is None else jnp.minimum(sc, cap), v_t)
                 for carry, (sc, v_t) in zip(carries, chunks))


def _flash_reset(state):
    m_ref, l_ref, acc_ref = state
    m_ref[...] = jnp.full(m_ref.shape, M_INIT, F32)
    l_ref[...] = jnp.zeros(l_ref.shape, F32)
    acc_ref[...] = jnp.zeros(acc_ref.shape, F32)


def _flash_update_all(state, slots, chunks, cap=None):
    m_ref, l_ref, acc_ref = state
    for slot, (sc, v_t) in zip(slots, chunks):
        carry = _flash_step((m_ref[slot], l_ref[slot], acc_ref[slot]), sc if cap is None else jnp.minimum(sc, cap),
                            v_t)
        m_ref[slot], l_ref[slot], acc_ref[slot] = carry


def _flash_update_pipelined(state, bufs, slots, c, next_scores, values, cap=None):
    m_ref, l_ref, acc_ref = state

    def run(src_ref, dst_ref):
        fresh = None if next_scores is None else next_scores()
        v_t = values()
        for k, slot in enumerate(slots):
            sc = src_ref[slot]
            carry = _flash_step((m_ref[slot], l_ref[slot], acc_ref[slot]),
                                sc if cap is None else jnp.minimum(sc, cap), v_t[k])
            m_ref[slot], l_ref[slot], acc_ref[slot] = carry
        if fresh is not None:
            for k, slot in enumerate(slots):
                dst_ref[slot] = fresh[k]

    pl.when((c & 1) == 0)(lambda: run(bufs[0], bufs[1]))
    pl.when((c & 1) == 1)(lambda: run(bufs[1], bufs[0]))


def _flash_result(state, slot):
    m_ref, l_ref, acc_ref = state
    return acc_ref[slot] / jnp.maximum(l_ref[slot], TINY)


def _flash_scratch(chains, n):
    return [pltpu.VMEM((chains, 1, n), F32), pltpu.VMEM((chains, 1, n), F32),
            pltpu.VMEM((chains, HEAD_DIM, n), F32)]


def _rank_select(score, blk, limit, count):
    rank = jnp.zeros(score.shape, jnp.int32)
    for mm in range(limit):
        gm = score[mm:mm + 1, :]
        beats = (gm > score) | ((gm == score) & (mm < blk))
        rank = rank + jnp.where(beats, 1, 0)
    return rank < count


def _tile_lanes(x, reps):
    return jnp.concatenate([x] * reps, axis=1) if reps > 1 else x


def _causal_cap(tk, tq, reps):
    keep = _iota((tk, tq), 0) <= _iota((tk, tq), 1)
    return _tile_lanes(jnp.where(keep, BIG, -BIG), reps)


def _window_edge_cap(tk, tq, reps):
    keep = _iota((tk, tq), 0) > _iota((tk, tq), 1)
    return _tile_lanes(jnp.where(keep, BIG, -BIG), reps)


def _store_chunked_t(x, dst_ref, tk):
    x_t = x.T.astype(BF16)
    for c in range(x.shape[0] // tk):
        dst_ref[c] = x_t[:, c * tk:(c + 1) * tk]


def _moba_kernel(alibi_ref, q_ref, k_ref, v_ref, z_ref, gq_ref, gk_ref, o_ref, ka_s, vt_s, km_s,
                 m_s, l_s, acc_s, sc0_s, sc1_s):
    i = pl.program_id(2)
    s = k_ref.shape[1]
    nblk = s // MOBA_BLOCK
    tq = MOBA_BLOCK
    hd = HEAD_DIM
    heads = range(ka_s.shape[0])

    @pl.when(i == 0)
    def _():
        feat = _key_features(_iota((s, hd), 0), sel_shift=MOBA_BLOCK.bit_length() - 1)
        for j in heads:
            kn = _rms(k_ref[0, :, j * hd:(j + 1) * hd], gk_ref[...])
            ka_s[j] = jnp.concatenate([kn.astype(BF16), feat], axis=1)
            km_s[j] = jnp.mean(kn.reshape(nblk, MOBA_BLOCK, hd), axis=1)
        _store_chunked_t(v_ref[0], vt_s, tq)

    q_t = q_ref[0].T
    blk = _iota((nblk, tq), 0)
    qas = []
    for j in heads:
        qn = _norm_query_t(q_t[j * hd:(j + 1) * hd], gq_ref[...])
        gate = jnp.where(blk < i, _dot_exact(km_s[j], qn), NEG)
        chosen = _rank_select(gate, blk, nblk, MOBA_TOPK) & (blk < i)
        sel_bias = jnp.where(chosen | (blk == i), 0.0, -BIG)
        qas.append(_augment_query(qn, alibi_ref[j], sel_bias))

    def scores(n):
        off = pl.multiple_of(n * tq, tq)
        return [_dot(ka_s[j, pl.ds(off, tq), :], qas[j]) for j in heads]

    def values(n):
        return [vt_s[n, j * hd:(j + 1) * hd, :] for j in heads]

    state = (m_s, l_s, acc_s)
    _flash_reset(state)
    bufs = (sc0_s, sc1_s)
    for j, sc in zip(heads, scores(0)):
        sc0_s[j] = sc

    def body(n, carry):
        _flash_update_pipelined(state, bufs, heads, n, lambda: scores(n + 1), lambda: values(n))
        return carry

    lax.fori_loop(0, i, body, 0)
    diag_cap = _causal_cap(tq, tq, 1)
    _flash_update_pipelined(state, bufs, heads, i, None, lambda: values(i), diag_cap)
    outs = [_flash_result(state, j) for j in heads]
    y = jnp.concatenate(outs, axis=0).T * _silu(z_ref[0])
    o_ref[0] = y.astype(o_ref.dtype)


def _moba(proj, gq, gk, *, heads_per_step=4):
    bsz, s, _ = proj.shape
    tq = MOBA_BLOCK
    hd = HEAD_DIM
    nblk = s // MOBA_BLOCK
    hps = heads_per_step
    wd = hps * hd
    assert SEL_FEAT0 + nblk <= hd and wd % LANES == 0
    q0, k0, v0, z0 = (c * GROUP_WIDTH // wd for c in (3, 4, 5, 6))
    tile = lambda c0: pl.BlockSpec((1, tq, wd), lambda b, p, i: (b, i, c0 + p))
    full = lambda c0: pl.BlockSpec((1, s, wd), lambda b, p, i: (b, 0, c0 + p))
    return pl.pallas_call(
        _moba_kernel,
        grid=(bsz, N_HEADS // hps, s // tq),
        in_specs=[pl.BlockSpec((hps, 2 * SLOPE_PIECES, tq), lambda b, p, i: (p, 0, 0)),
                  tile(q0), full(k0), full(v0), tile(z0),
                  pl.BlockSpec((hd, 1), lambda b, p, i: (0, 0)), pl.BlockSpec((1, hd), lambda b, p, i: (0, 0))],
        out_specs=pl.BlockSpec((1, tq, wd), lambda b, p, i: (b, i, p)),
        out_shape=jax.ShapeDtypeStruct((bsz, s, GROUP_WIDTH), BF16),
        scratch_shapes=[pltpu.VMEM((hps, s, LANES), BF16), pltpu.VMEM((nblk, wd, tq), BF16),
                        pltpu.VMEM((hps, nblk, hd), F32)] + _flash_scratch(hps, tq)
                       + [pltpu.VMEM((hps, tq, tq), F32)] * 2,
        name="moba",
    )(jnp.asarray(_alibi_query_features(tq)), proj, proj, proj, proj, gq.reshape(hd, 1), gk.reshape(1, hd))


def _cmp_kernel(xk_ref, xv_ref, wka_ref, wkb_ref, wva_ref, wvb_ref, pk_ref, pv_ref, bk1_ref, bv1_ref,
                wk2_ref, wv2_ref, bk2_ref, bv2_ref, gn_ref, ko_ref, vo_ref):
    half = xk_ref.shape[2]
    n = xk_ref.shape[1]

    def hidden(x_ref, wa_ref, wb_ref, pos_ref, b1_ref):
        x = x_ref[0].astype(BF16)
        first = _dot(x, wa_ref[...])
        second = _dot(x, wb_ref[...])
        pos = jnp.broadcast_to(pos_ref[...], (8, 2 * half)).astype(BF16)
        ph = _dot(pos[:, :half], wa_ref[...]) + _dot(pos[:, half:], wb_ref[...])
        return _silu(first + pltpu.roll(second, n - 1, 0) + ph[0:1, :] + b1_ref[...])

    hk = hidden(xk_ref, wka_ref, wkb_ref, pk_ref, bk1_ref).astype(BF16)
    hv = hidden(xv_ref, wva_ref, wvb_ref, pv_ref, bv1_ref).astype(BF16)
    nh = NSA_CMP_HIDDEN
    ks, vs = [], []
    for g in range(KV_HEADS):
        ks.append(_rms(_dot(hk[:, g * nh:(g + 1) * nh], wk2_ref[...]) + bk2_ref[...], gn_ref[...]))
        vs.append(_dot(hv[:, g * nh:(g + 1) * nh], wv2_ref[...]) + bv2_ref[...])
    ko_ref[0] = jnp.concatenate(ks, axis=1)
    vo_ref[0] = jnp.concatenate(vs, axis=1)


def _expand_w1(w1):
    hid = w1.shape[1]
    w = w1.reshape(NSA_CMP_LEN, HEAD_DIM, hid)
    eye = jnp.eye(KV_HEADS, dtype=w1.dtype)
    w = jnp.einsum('ldn,gh->lgdhn', w, eye).reshape(NSA_CMP_LEN * KV_HEADS * HEAD_DIM, KV_HEADS * hid)
    half = w.shape[0] // 2
    return w[:half].astype(BF16), w[half:].astype(BF16)


def _compress(xk, xv, pos_k, pos_v, kw1, kb1, kw2, kb2, vw1, vb1, vw2, vb2, gn):
    bsz, n, half = xk.shape
    hd = HEAD_DIM
    wka, wkb = _expand_w1(kw1)
    wva, wvb = _expand_w1(vw1)
    tile_pos = lambda p: jnp.tile(p[:, None, :], (1, KV_HEADS, 1)).reshape(1, 2 * half)
    tile_b = lambda b: jnp.tile(b.reshape(1, -1), (1, KV_HEADS))
    const = lambda shape: pl.BlockSpec(shape, lambda b: (0,) * len(shape))
    xs = pl.BlockSpec((1, n, half), lambda b: (b, 0, 0))
    hw = KV_HEADS * NSA_CMP_HIDDEN
    out = pl.BlockSpec((1, n, KV_HEADS * hd), lambda b: (b, 0, 0))
    return pl.pallas_call(
        _cmp_kernel,
        grid=(bsz,),
        in_specs=[xs, xs, const((half, hw)), const((half, hw)), const((half, hw)), const((half, hw)),
                  const((1, 2 * half)), const((1, 2 * half)), const((1, hw)), const((1, hw)),
                  const((NSA_CMP_HIDDEN, hd)), const((NSA_CMP_HIDDEN, hd)), const((1, hd)), const((1, hd)),
                  const((1, hd))],
        out_specs=[out, out],
        out_shape=[jax.ShapeDtypeStruct((bsz, n, KV_HEADS * hd), F32)] * 2,
        name="nsa_compress",
    )(xk, xv, wka, wkb, wva, wvb, tile_pos(pos_k), tile_pos(pos_v), tile_b(kb1), tile_b(vb1),
      kw2.astype(BF16), vw2.astype(BF16), kb2.reshape(1, hd), vb2.reshape(1, hd), gn.reshape(1, hd))


def _group_query(q_t, alibi_ref, g, gq_col):
    hd = HEAD_DIM
    heads = range(Q_PER_KV * g, Q_PER_KV * (g + 1))
    qn = jnp.concatenate([_norm_query_t(q_t[h * hd:(h + 1) * hd], gq_col) for h in heads], axis=1)
    alibi = jnp.concatenate([alibi_ref[h] for h in heads], axis=1)
    return qn, alibi


def _augment_keys(k_ref, gk_ref, feat, dst_ref):
    hd = HEAD_DIM
    for g in range(KV_HEADS):
        kn = _rms(k_ref[0, :, g * hd:(g + 1) * hd], gk_ref[...])
        dst_ref[g] = jnp.concatenate([kn.astype(BF16), feat], axis=1)


def _chunk_scores(ka_ref, vt_ref, g, c, qa, tq):
    off = pl.multiple_of(c * tq, tq)
    return _dot(ka_ref[g, pl.ds(off, tq), :], qa), vt_ref[c, g * HEAD_DIM:(g + 1) * HEAD_DIM, :]


def _window_first_chunk(ka_ref, vt_ref, g, far, qa, tq, edge_cap):
    sc, v_t = _chunk_scores(ka_ref, vt_ref, g, jnp.maximum(far, 0), qa, tq)
    return jnp.minimum(sc, jnp.minimum(edge_cap, jnp.where(far >= 0, BIG, -BIG))), v_t


def _regroup_heads(o_t, tq):
    return jnp.concatenate([o_t[:, k * tq:(k + 1) * tq] for k in range(Q_PER_KV)], axis=0)


def _nsa_kernel(alibi_ref, q_ref, kc_ref, vc_ref, ks_ref, vs_ref, kw_ref, vw_ref, gate_ref, z_ref,
                gq_ref, gks_ref, gkw_ref, o_ref, kca_s, vct_s, ksa_s, vst_s, kwa_s, vwt_s, m_s, l_s, acc_s,
                sc0_s, sc1_s):
    i = pl.program_id(1)
    tq = q_ref.shape[1]
    s = ks_ref.shape[1]
    hd = HEAD_DIM
    ncmp = kc_ref.shape[1]
    nsb = s // NSA_SEL_BLOCK
    n4 = Q_PER_KV * tq

    @pl.when(i == 0)
    def _():
        pos = _iota((s, hd), 0)
        _augment_keys(ks_ref, gks_ref, _key_features(pos, sel_shift=NSA_SEL_BLOCK.bit_length() - 1), ksa_s)
        _augment_keys(kw_ref, gkw_ref, _key_features(pos), kwa_s)
        _store_chunked_t(vs_ref[0], vst_s, tq)
        _store_chunked_t(vw_ref[0], vwt_s, tq)
        cfeat = _key_features(_iota((ncmp, hd), 0) * NSA_CMP_STRIDE + (NSA_CMP_LEN - 1))
        for g in range(KV_HEADS):
            kca_s[g] = jnp.concatenate([kc_ref[0, :, g * hd:(g + 1) * hd].astype(BF16), cfeat], axis=1)
        vct_s[...] = vc_ref[0].T.astype(BF16)

    t0 = i * tq
    q_t = q_ref[0].T
    gates_t = _sigmoid(gate_ref[0]).T
    diag_cap = _causal_cap(tq, tq, Q_PER_KV)
    edge_cap = _window_edge_cap(tq, tq, Q_PER_KV)
    cend =_iota((ncmp, tq), 0) * NSA_CMP_STRIDE + (NSA_CMP_LEN - 1)
    seen = _tile_lanes(cend <= t0 + _iota((ncmp, tq), 1), Q_PER_KV)
    cstart = _iota((nsb, ncmp), 1) * NSA_CMP_STRIDE
    bstart = _iota((nsb, ncmp), 0) * NSA_SEL_BLOCK
    overlap_t = jnp.where((cstart < bstart + NSA_SEL_BLOCK) & (cstart + NSA_CMP_LEN > bstart), 1.0, 0.0)
    blk = _iota((nsb, tq), 0)
    cur = (t0 + _iota((nsb, tq), 1)) >> (NSA_SEL_BLOCK.bit_length() - 1)
    forced = (blk == 0) | (blk == cur) | (blk == cur - 1)
    groups = range(KV_HEADS)
    far = i - NSA_WINDOW // tq
    lo = jnp.maximum(far + 1, 0)

    state = (m_s, l_s, acc_s)
    sel_slots = tuple(groups)
    win_slots = tuple(KV_HEADS + g for g in groups)
    _flash_reset(state)
    queries = [_group_query(q_t, alibi_ref, g, gq_ref[...]) for g in groups]
    qa_win = [_augment_query(qn, alibi) for qn, alibi in queries]
    def win_scores(c):
        return [_chunk_scores(kwa_s, vwt_s, g, c, qa_win[g], tq)[0] for g in groups]

    first = [_window_first_chunk(kwa_s, vwt_s, g, far, qa_win[g], tq, edge_cap) for g in groups]
    cmp_sc = [_dot(kca_s[g], qa_win[g]) for g in groups]
    sc_win_lo = win_scores(lo)
    _flash_update_all(state, win_slots, first)
    qa_sel, o_cmp = [], []
    for g in groups:
        qn, alibi = queries[g]
        sc = jnp.where(seen, cmp_sc[g], -BIG)
        e = jnp.where(seen, jnp.exp2(sc - jnp.max(sc, axis=0, keepdims=True)), 0.0)
        p_c = e / jnp.maximum(jnp.sum(e, axis=0, keepdims=True), TINY)
        o_cmp.append(_dot(vct_s[g * hd:(g + 1) * hd, :], p_c.astype(BF16)))

        psum = p_c[:, 0:tq]
        for k in range(1, Q_PER_KV):
            psum = psum + p_c[:, k * tq:(k + 1) * tq]
        imp = _dot_exact(overlap_t, psum)
        imp = jnp.where(blk <= cur, jnp.where(forced, NSA_FORCE, imp), NEG)
        chosen = _rank_select(imp, blk, nsb, min(NSA_TOPN, nsb))
        sel_bias = _tile_lanes(jnp.where(chosen, 0.0, -BIG), Q_PER_KV)
        qa_sel.append(_augment_query(qn, alibi, sel_bias))

    def sel_scores(c):
        return [_chunk_scores(ksa_s, vst_s, g, c, qa_sel[g], tq)[0] for g in groups]

    def values(c, with_win):
        refs = (vst_s, vwt_s) if with_win else (vst_s,)
        return [ref[c, g * hd:(g + 1) * hd, :] for ref in refs for g in groups]

    bufs = (sc0_s, sc1_s)
    for slot, sc in zip(sel_slots, sel_scores(0)):
        sc0_s[slot] = sc
    for parity, buf in enumerate(bufs):
        @pl.when((lo & 1) == parity)
        def _(buf=buf):
            for slot, sc in zip(win_slots, sc_win_lo):
                buf[slot] = sc

    def sel_body(c, carry):
        _flash_update_pipelined(state, bufs, sel_slots, c, lambda: sel_scores(c + 1), lambda: values(c, False))
        return carry

    def both_body(c, carry):
        _flash_update_pipelined(state, bufs, sel_slots + win_slots, c,
                                lambda: sel_scores(c + 1) + win_scores(c + 1), lambda: values(c, True))
        return carry

    lax.fori_loop(0, lo, sel_body, 0)
    lax.fori_loop(lo, i, both_body, 0)
    _flash_update_pipelined(state, bufs, sel_slots + win_slots, i, None, lambda: values(i, True), diag_cap)

    outs = []
    for g in groups:
        def gate_row(branch):
            rows = [gates_t[branch * N_HEADS + h:branch * N_HEADS + h + 1, :]
                    for h in range(Q_PER_KV * g, Q_PER_KV * (g + 1))]
            return jnp.concatenate(rows, axis=1)

        o_slc = _flash_result(state, sel_slots[g])
        o_win = _flash_result(state, win_slots[g])
        o = gate_row(0) * o_cmp[g] + gate_row(1) * o_slc + gate_row(2) * o_win
        outs.append(_regroup_heads(o, tq))
    y = jnp.concatenate(outs, axis=0).T * _silu(z_ref[0])
    o_ref[0] = y.astype(o_ref.dtype)


def _nsa(proj, kcn, vc, gq, gks, gkw, *, tq=128):
    bsz, s, _ = proj.shape
    hd = HEAD_DIM
    gw = GROUP_WIDTH
    ncmp = kcn.shape[1]
    assert SEL_FEAT0 + s // NSA_SEL_BLOCK <= hd and ncmp == LANES
    wide = lambda c0: pl.BlockSpec((1, tq, gw), lambda b, i: (b, i, c0 // 4))
    full = lambda c0: pl.BlockSpec((1, s, LANES), lambda b, i: (b, 0, c0))
    cmp = pl.BlockSpec((1, ncmp, LANES), lambda b, i: (b, 0, 0))
    row = pl.BlockSpec((1, hd), lambda b, i: (0, 0))
    ka = pltpu.VMEM((KV_HEADS, s, LANES), BF16)
    vt = pltpu.VMEM((s // tq, LANES, tq), BF16)
    return pl.pallas_call(
        _nsa_kernel,
        grid=(bsz, s // tq),
        in_specs=[pl.BlockSpec((N_HEADS, 2 * SLOPE_PIECES, tq), lambda b, i: (0, 0, 0)),
                  wide(ODD_QC), cmp, cmp, full(ODD_KS), full(ODD_VS), full(ODD_KW), full(ODD_VW),
                  pl.BlockSpec((1, tq, LANES), lambda b, i: (b, i, ODD_GC)), wide(ODD_ZC),
                  pl.BlockSpec((hd, 1), lambda b, i: (0, 0)), row, row],
        out_specs=pl.BlockSpec((1, tq, gw), lambda b, i: (b, i, 0)),
        out_shape=jax.ShapeDtypeStruct((bsz, s, gw), BF16),
        scratch_shapes=[pltpu.VMEM((KV_HEADS, ncmp, LANES), BF16), pltpu.VMEM((LANES, ncmp), BF16),
                        ka, vt, ka, vt] + _flash_scratch(2 * KV_HEADS, Q_PER_KV * tq)
                       + [pltpu.VMEM((2 * KV_HEADS, tq, Q_PER_KV * tq), F32)] * 2,
        name="nsa",
    )(jnp.asarray(_alibi_query_features(tq)), proj, kcn, vc, proj, proj, proj, proj, proj, proj,
      gq.reshape(hd, 1), gks.reshape(1, hd), gkw.reshape(1, hd))


def _swa_kernel(sinks_ref, alibi_ref, q_ref, k_ref, v_ref, z_ref, gq_ref, gk_ref, o_ref, ka_s, vt_s):
    i = pl.program_id(1)
    tq = q_ref.shape[1]
    s = k_ref.shape[1]

    @pl.when(i == 0)
    def _():
        _augment_keys(k_ref, gk_ref, _key_features(_iota((s, HEAD_DIM), 0)), ka_s)
        _store_chunked_t(v_ref[0], vt_s, tq)

    q_t = q_ref[0].T
    diag_cap = _causal_cap(tq, tq, Q_PER_KV)
    edge_cap = _window_edge_cap(tq, tq, Q_PER_KV)
    tpos =(i * tq + _iota((1, tq), 1)).astype(F32)
    outs = []
    assert SWA_WINDOW == tq
    groups = range(KV_HEADS)
    qas = [_augment_query(*_group_query(q_t, alibi_ref, g, gq_ref[...])) for g in groups]
    first = [_window_first_chunk(ka_s, vt_s, g, i - 1, qas[g], tq, edge_cap) for g in groups]
    own = [_chunk_scores(ka_s, vt_s, g, i, qas[g], tq) for g in groups]
    carries = _flash_steps([_flash_init(Q_PER_KV * tq) for _ in groups], first)
    carries = _flash_steps(carries, own, diag_cap)
    for g in groups:
        m, l, acc = carries[g]
        sink = jnp.concatenate([sinks_ref[h] * LOG2E + (SLOPES[h] * LOG2E) * tpos
                                for h in range(Q_PER_KV * g, Q_PER_KV * (g + 1))], axis=1)
        mf = jnp.maximum(m, sink)
        alpha = jnp.exp2(m - mf)
        o = acc * alpha / jnp.maximum(l * alpha + jnp.exp2(sink - mf), TINY)
        outs.append(_regroup_heads(o, tq))
    y = jnp.concatenate(outs, axis=0).T * _silu(z_ref[0])
    o_ref[0] = y.astype(o_ref.dtype)


def _swa(proj, sinks, gq, gk, *, tq=128):
    bsz, s, _ = proj.shape
    hd = HEAD_DIM
    gw = GROUP_WIDTH
    wide = lambda c0: pl.BlockSpec((1, tq, gw), lambda b, i: (b, i, c0 // 4))
    full = lambda c0: pl.BlockSpec((1, s, LANES), lambda b, i: (b, 0, c0))
    return pl.pallas_call(
        _swa_kernel,
        grid=(bsz, s // tq),
        in_specs=[pl.BlockSpec(memory_space=pltpu.SMEM),
                  pl.BlockSpec((N_HEADS, 2 * SLOPE_PIECES, tq), lambda b, i: (0, 0, 0)),
                  wide(ODD_QD), full(ODD_KD), full(ODD_VD), wide(ODD_ZD),
                  pl.BlockSpec((hd, 1), lambda b, i: (0, 0)), pl.BlockSpec((1, hd), lambda b, i: (0, 0))],
        out_specs=pl.BlockSpec((1, tq, gw), lambda b, i: (b, i, 0)),
        out_shape=jax.ShapeDtypeStruct((bsz, s, gw), BF16),
        scratch_shapes=[pltpu.VMEM((KV_HEADS, s, LANES), BF16), pltpu.VMEM((s // tq, LANES, tq), BF16)],
        name="swa",
    )(sinks.astype(F32), jnp.asarray(_alibi_query_features(tq)), proj, proj, proj, proj,
      gq.reshape(hd, 1), gk.reshape(1, hd))


def _pack_odd_weight(w):
    gw, kw, ng = GROUP_WIDTH, KV_HEADS * HEAD_DIM, 3 * N_HEADS
    sizes = [gw, kw, kw, kw, kw, kw, kw, ng, gw, gw, kw, kw, gw]
    starts = np.concatenate([[0], np.cumsum(sizes)]).tolist()
    part = lambda k: w[:, starts[k]:starts[k + 1]]
    pad = jnp.zeros((w.shape[0], LANES - ng), w.dtype)
    order = [0, 8, 9, 12, 1, 2, 3, 4, 5, 6, 10, 11, 7]
    return jnp.concatenate([part(k) for k in order] + [pad], axis=1).astype(BF16)


def kernel(x, norm_g, w_out, e_w_in, a_conv_w, a_conv_b, a_ln_g, a_ln_b, b_qnorm_g, b_knorm_g, o_w_in, c_qnorm_g, c_knorm_cmp_g, c_knorm_slc_g, c_knorm_win_g, c_pos_k, c_pos_v, c_k_w1, c_k_b1, c_k_w2, c_k_b2, c_v_w1, c_v_b1, c_v_w2, c_v_b2, d_qnorm_g, d_knorm_g, d_sinks):
    bsz, s, d = x.shape
    m = bsz * s
    assert s % MOBA_BLOCK == 0 and d == 2 * GROUP_WIDTH
    x2 = x.reshape(m, d)

    proj = _inproj(x2, norm_g[0], e_w_in[0].astype(BF16), chunk=GROUP_WIDTH).reshape(bsz, s, -1)
    y_a = _conv_module(proj, a_conv_w[0], a_conv_b[0], a_ln_g[0], a_ln_b[0])
    y_b = _moba(proj, b_qnorm_g[0], b_knorm_g[0])
    x2 = _outproj(x2, y_a.reshape(m, -1), y_b.reshape(m, -1), w_out[0].astype(BF16))

    proj = _inproj(x2, norm_g[1], _pack_odd_weight(o_w_in[0]), chunk=5 * LANES).reshape(bsz, s, ODD_COLS)
    rows16 = lambda c0: proj[:, :, c0 * LANES:(c0 + 1) * LANES].reshape(bsz, s // NSA_CMP_STRIDE,
                                                                       NSA_CMP_STRIDE * LANES)
    kcn, vc = _compress(rows16(ODD_KC), rows16(ODD_VC), c_pos_k[0], c_pos_v[0], c_k_w1[0], c_k_b1[0],
                        c_k_w2[0], c_k_b2[0], c_v_w1[0], c_v_b1[0], c_v_w2[0], c_v_b2[0], c_knorm_cmp_g[0])
    y_c = _nsa(proj, kcn, vc, c_qnorm_g[0], c_knorm_slc_g[0], c_knorm_win_g[0])
    y_d = _swa(proj, d_sinks[0], d_qnorm_g[0], d_knorm_g[0])
    x2 = _outproj(x2, y_c.reshape(m, -1), y_d.reshape(m, -1), w_out[1].astype(BF16))
    return x2.reshape(bsz, s, d)
```

```python
import functools

import ml_dtypes
import numpy as np
import jax
import jax.numpy as jnp
from jax import lax
from jax.experimental import pallas as pl
from jax.experimental.pallas import tpu as pltpu

HEAD_DIM = 64
N_HEADS = 8
GROUP_WIDTH = N_HEADS * HEAD_DIM
CONV_WIDTH = 31
MOBA_BLOCK = 256
MOBA_TOPK = 3
KV_HEADS = 2
Q_PER_KV = N_HEADS // KV_HEADS
NSA_CMP_LEN = 32
NSA_CMP_STRIDE = 16
NSA_CMP_HIDDEN = 256
NSA_SEL_BLOCK = 64
NSA_TOPN = 16
NSA_WINDOW = 512
NSA_FORCE = 1e4
SWA_WINDOW = 128
EPS = 1e-6
NEG = -1e30
TINY = 1e-30
LANES = 128
CONV_HALO = 32

LOG2E = float(np.log2(np.e))
BIG = 2.0 ** 99
M_INIT = -1e38
POS_SPLIT_SHIFT = 8
SLOPE_PIECES = 4
SEL_FEAT0 = 2 * SLOPE_PIECES

F32 = jnp.float32
BF16 = jnp.bfloat16

ODD_QC, ODD_ZC, ODD_QD, ODD_ZD = 0, 4, 8, 12
ODD_KC, ODD_VC, ODD_KS, ODD_VS, ODD_KW, ODD_VW, ODD_KD, ODD_VD, ODD_GC = 16, 17, 18, 19, 20, 21, 22, 23, 24
ODD_COLS = 25 * LANES


def _alibi_slopes(n):
    return [float(2.0 ** (-8.0 * (i + 1) / n)) for i in range(n)]


SLOPES = _alibi_slopes(N_HEADS)


def _alibi_query_features(width):
    table = np.zeros((N_HEADS, 2 * SLOPE_PIECES, width), np.float32)
    for h, slope in enumerate(SLOPES):
        rest = np.float64(slope) * LOG2E
        for k in range(SLOPE_PIECES):
            piece = float(np.float32(rest).astype(ml_dtypes.bfloat16).astype(np.float32))
            table[h, k] = piece
            table[h, SLOPE_PIECES + k] = piece
            rest -= piece
    return table


def _dot(a, b):
    return jnp.dot(a, b, preferred_element_type=F32)


def _dot_exact(a, b):
    return jnp.dot(a, b, preferred_element_type=F32, precision=lax.Precision.HIGHEST)


def _sigmoid(x):
    return 1.0 / (1.0 + jnp.exp(-x))


def _silu(x):
    return x * _sigmoid(x)


def _rms(x, g):
    return x * lax.rsqrt(jnp.mean(x * x, axis=-1, keepdims=True) + EPS) * g


def _iota(shape, dim):
    return lax.broadcasted_iota(jnp.int32, shape, dim)


def _inproj_kernel(x_ref, g_ref, w_ref, o_ref, *, chunk):
    x = x_ref[...]
    h = _rms(x, g_ref[...]).astype(BF16)
    for c in range(o_ref.shape[1] // chunk):
        o_ref[:, c * chunk:(c + 1) * chunk] = _dot(h, w_ref[:, c * chunk:(c + 1) * chunk])


def _inproj(x2d, g, w, *, tm=256, chunk):
    m, d = x2d.shape
    e = w.shape[1]
    return pl.pallas_call(
        functools.partial(_inproj_kernel, chunk=chunk),
        grid=(m // tm,),
        in_specs=[pl.BlockSpec((tm, d), lambda i: (i, 0)),
                  pl.BlockSpec((1, d), lambda i: (0, 0)),
                  pl.BlockSpec((d, e), lambda i: (0, 0))],
        out_specs=pl.BlockSpec((tm, e), lambda i: (i, 0)),
        out_shape=jax.ShapeDtypeStruct((m, e), F32),
        name="inproj",
    )(x2d, g.reshape(1, d), w)


def _outproj_kernel(x_ref, ya_ref, yb_ref, wa_ref, wb_ref, o_ref):
    o_ref[...] = x_ref[...] + _dot(ya_ref[...], wa_ref[...]) + _dot(yb_ref[...], wb_ref[...])


def _outproj(x2d, ya, yb, w, *, tm=512):
    m, d = x2d.shape
    gw = ya.shape[1]
    return pl.pallas_call(
        _outproj_kernel,
        grid=(m // tm,),
        in_specs=[pl.BlockSpec((tm, d), lambda i: (i, 0)),
                  pl.BlockSpec((tm, gw), lambda i: (i, 0)),
                  pl.BlockSpec((tm, gw), lambda i: (i, 0)),
                  pl.BlockSpec((gw, d), lambda i: (0, 0)),
                  pl.BlockSpec((gw, d), lambda i: (0, 0))],
        out_specs=pl.BlockSpec((tm, d), lambda i: (i, 0)),
        out_shape=jax.ShapeDtypeStruct((m, d), F32),
        name="outproj",
    )(x2d, ya, yb, w[:gw], w[gw:])


def _conv_kernel(uv_ref, ug_ref, uvh_ref, ugh_ref, z_ref, w_ref, b_ref, lg_ref, lb_ref, o_ref, h_s, *, rows):
    i = pl.program_id(1)
    ts = uv_ref.shape[1]
    halo = uvh_ref[0] * _sigmoid(ugh_ref[0])
    h_s[0:CONV_HALO, :] = jnp.where(i > 0, halo, 0.0)
    h_s[CONV_HALO:, :] = uv_ref[0] * _sigmoid(ug_ref[0])
    base = CONV_HALO - (CONV_WIDTH - 1)
    for c in range(ts // rows):
        acc = jnp.zeros((rows, GROUP_WIDTH), F32)
        for j in range(CONV_WIDTH):
            acc = acc + w_ref[j:j + 1, :] * h_s[pl.ds(c * rows + base + j, rows), :]
        y = acc + b_ref[...]
        mu = jnp.mean(y, axis=-1, keepdims=True)
        yc = y - mu
        y = yc * lax.rsqrt(jnp.mean(yc * yc, axis=-1, keepdims=True) + EPS) * lg_ref[...] + lb_ref[...]
        out = _silu(y) * _silu(z_ref[0, c * rows:(c + 1) * rows, :])
        o_ref[0, c * rows:(c + 1) * rows, :] = out.astype(o_ref.dtype)


def _conv_module(proj, w, b, lg, lb, *, ts=256, rows=64):
    bsz, s, _ = proj.shape
    gw = GROUP_WIDTH
    hb = ts // CONV_HALO
    cur = lambda col: pl.BlockSpec((1, ts, gw), lambda bi, i: (bi, i, col))
    halo = lambda col: pl.BlockSpec((1, CONV_HALO, gw), lambda bi, i: (bi, jnp.maximum(i * hb - 1, 0), col))
    vec = pl.BlockSpec((1, gw), lambda bi, i: (0, 0))
    return pl.pallas_call(
        functools.partial(_conv_kernel, rows=rows),
        grid=(bsz, s // ts),
        in_specs=[cur(0), cur(1), halo(0), halo(1), cur(2),
                  pl.BlockSpec((CONV_WIDTH, gw), lambda bi, i: (0, 0)), vec, vec, vec],
        out_specs=pl.BlockSpec((1, ts, gw), lambda bi, i: (bi, i, 0)),
        out_shape=jax.ShapeDtypeStruct((bsz, s, gw), BF16),
        scratch_shapes=[pltpu.VMEM((CONV_HALO + ts, gw), F32)],
        name="conv_module",
    )(proj, proj, proj, proj, proj, w, b.reshape(1, gw), lg.reshape(1, gw), lb.reshape(1, gw))


def _key_features(pos, sel_shift=None):
    col = _iota(pos.shape, 1)
    hi = (pos >> POS_SPLIT_SHIFT) << POS_SPLIT_SHIFT
    lo = pos & ((1 << POS_SPLIT_SHIFT) - 1)
    feat = jnp.where(col < SLOPE_PIECES, hi, jnp.where(col < 2 * SLOPE_PIECES, lo, 0))
    if sel_shift is not None:
        feat = jnp.where(col - SEL_FEAT0 == (pos >> sel_shift), 1, feat)
    return feat.astype(F32).astype(BF16)


def _norm_query_t(x, g_col):
    ss = jnp.mean(x * x, axis=0, keepdims=True)
    return x * lax.rsqrt(ss + EPS) * g_col * (HEAD_DIM ** -0.5 * LOG2E)


def _augment_query(q_t, alibi, sel_bias=None):
    n = q_t.shape[1]
    parts = [q_t, alibi]
    used = alibi.shape[0]
    if sel_bias is not None:
        parts.append(sel_bias)
        used += sel_bias.shape[0]
    parts.append(jnp.zeros((HEAD_DIM - used, n), F32))
    return jnp.concatenate(parts, axis=0).astype(BF16)


ONES_ROWS = 16
ACC_ROWS = HEAD_DIM + ONES_ROWS


def _flash_init(n):
    return jnp.full((1, n), M_INIT, F32), jnp.zeros((ACC_ROWS, n), F32)


def _flash_step(carry, sc, v_t):
    m, acc = carry
    m_new = jnp.maximum(m, jnp.max(sc, axis=0, keepdims=True))
    p = jnp.exp2(sc - m_new).astype(BF16)
    v_ones = jnp.concatenate([v_t, jnp.ones((ONES_ROWS, v_t.shape[1]), BF16)], axis=0)
    return m_new, jnp.exp2(m - m_new) * acc + _dot(v_ones, p)


def _flash_split(carry):
    m, acc = carry
    return m, acc[:HEAD_DIM], acc[HEAD_DIM:HEAD_DIM + 1]


def _flash_steps(carries, chunks, cap=None):
    return tuple(_flash_step(carry, sc if cap is None else jnp.minimum(sc, cap), v_t)
                 for carry, (sc, v_t) in zip(carries, chunks))


def _flash_reset(state):
    m_ref, acc_ref = state
    m_ref[...] = jnp.full(m_ref.shape, M_INIT, F32)
    acc_ref[...] = jnp.zeros(acc_ref.shape, F32)


def _flash_update(state, slot, sc, v_t, cap=None):
    m_ref, acc_ref = state
    m_ref[slot], acc_ref[slot] = _flash_step((m_ref[slot], acc_ref[slot]),
                                             sc if cap is None else jnp.minimum(sc, cap), v_t)


def _flash_update_all(state, slots, chunks, cap=None):
    for slot, (sc, v_t) in zip(slots, chunks):
        _flash_update(state, slot, sc, v_t, cap)


def _flash_update_pipelined(state, bufs, slots, c, next_scores, values, cap=None):
    def run(src_ref, dst_ref):
        for slot in slots:
            fresh = None if next_scores is None else next_scores(slot)
            _flash_update(state, slot, src_ref[slot], values(slot), cap)
            if fresh is not None:
                dst_ref[slot] = fresh

    pl.when((c & 1) == 0)(lambda: run(bufs[0], bufs[1]))
    pl.when((c & 1) == 1)(lambda: run(bufs[1], bufs[0]))


def _flash_result(state, slot):
    m_ref, acc_ref = state
    m, out, den = _flash_split((m_ref[slot], acc_ref[slot]))
    return out / jnp.maximum(den, TINY)


def _flash_scratch(chains, n):
    return [pltpu.VMEM((chains, 1, n), F32), pltpu.VMEM((chains, ACC_ROWS, n), F32)]


def _rank_select(score, blk, limit, count):
    rank = jnp.zeros(score.shape, jnp.int32)
    for mm in range(limit):
        gm = score[mm:mm + 1, :]
        beats = (gm > score) | ((gm == score) & (mm < blk))
        rank = rank + jnp.where(beats, 1, 0)
    return rank < count


def _tile_lanes(x, reps):
    return jnp.concatenate([x] * reps, axis=1) if reps > 1 else x


def _causal_cap(tk, tq, reps):
    keep = _iota((tk, tq), 0) <= _iota((tk, tq), 1)
    return _tile_lanes(jnp.where(keep, BIG, -BIG), reps)


def _window_edge_cap(tk, tq, reps):
    keep = _iota((tk, tq), 0) > _iota((tk, tq), 1)
    return _tile_lanes(jnp.where(keep, BIG, -BIG), reps)


def _store_chunked_t(x, dst_ref, tk):
    x_t = x.T.astype(BF16)
    for c in range(x.shape[0] // tk):
        dst_ref[c] = x_t[:, c * tk:(c + 1) * tk]


def _moba_kernel(alibi_ref, q_ref, k_ref, v_ref, z_ref, gq_ref, gk_ref, o_ref, ka_s, vt_s, km_s,
                 m_s, acc_s, sc0_s, sc1_s):
    i = pl.program_id(2)
    s = k_ref.shape[1]
    nblk = s // MOBA_BLOCK
    tq = MOBA_BLOCK
    hd = HEAD_DIM
    heads = range(ka_s.shape[0])

    @pl.when(i == 0)
    def _():
        feat = _key_features(_iota((s, hd), 0), sel_shift=MOBA_BLOCK.bit_length() - 1)
        for j in heads:
            kn = _rms(k_ref[0, :, j * hd:(j + 1) * hd], gk_ref[...])
            ka_s[j] = jnp.concatenate([kn.astype(BF16), feat], axis=1)
            km_s[j] = jnp.mean(kn.reshape(nblk, MOBA_BLOCK, hd), axis=1)
        _store_chunked_t(v_ref[0], vt_s, tq)

    q_t = q_ref[0].T
    blk = _iota((nblk, tq), 0)
    qas = []
    for j in heads:
        qn = _norm_query_t(q_t[j * hd:(j + 1) * hd], gq_ref[...])
        gate = jnp.where(blk < i, _dot_exact(km_s[j], qn), NEG)
        chosen = _rank_select(gate, blk, nblk, MOBA_TOPK) & (blk < i)
        sel_bias = jnp.where(chosen | (blk == i), 0.0, -BIG)
        qas.append(_augment_query(qn, alibi_ref[j], sel_bias))

    def scores(j, n):
        return _dot(ka_s[j, pl.ds(pl.multiple_of(n * tq, tq), tq), :], qas[j])

    def values(j, n):
        return vt_s[n, j * hd:(j + 1) * hd, :]

    state = (m_s, acc_s)
    _flash_reset(state)
    bufs = (sc0_s, sc1_s)
    for j in heads:
        sc0_s[j] = scores(j, 0)

    def body(n, carry):
        _flash_update_pipelined(state, bufs, heads, n, lambda j: scores(j, n + 1), lambda j: values(j, n))
        return carry

    lax.fori_loop(0, i, body, 0)
    diag_cap = _causal_cap(tq, tq, 1)
    _flash_update_pipelined(state, bufs, heads, i, None, lambda j: values(j, i), diag_cap)
    outs = [_flash_result(state, j) for j in heads]
    y = jnp.concatenate(outs, axis=0).T * _silu(z_ref[0])
    o_ref[0] = y.astype(o_ref.dtype)


def _moba(proj, gq, gk, *, heads_per_step=8):
    bsz, s, _ = proj.shape
    tq = MOBA_BLOCK
    hd = HEAD_DIM
    nblk = s // MOBA_BLOCK
    hps = heads_per_step
    wd = hps * hd
    assert SEL_FEAT0 + nblk <= hd and wd % LANES == 0
    q0, k0, v0, z0 = (c * GROUP_WIDTH // wd for c in (3, 4, 5, 6))
    tile = lambda c0: pl.BlockSpec((1, tq, wd), lambda b, p, i: (b, i, c0 + p))
    full = lambda c0: pl.BlockSpec((1, s, wd), lambda b, p, i: (b, 0, c0 + p))
    return pl.pallas_call(
        _moba_kernel,
        grid=(bsz, N_HEADS // hps, s // tq),
        in_specs=[pl.BlockSpec((hps, 2 * SLOPE_PIECES, tq), lambda b, p, i: (p, 0, 0)),
                  tile(q0), full(k0), full(v0), tile(z0),
                  pl.BlockSpec((hd, 1), lambda b, p, i: (0, 0)), pl.BlockSpec((1, hd), lambda b, p, i: (0, 0))],
        out_specs=pl.BlockSpec((1, tq, wd), lambda b, p, i: (b, i, p)),
        out_shape=jax.ShapeDtypeStruct((bsz, s, GROUP_WIDTH), BF16),
        scratch_shapes=[pltpu.VMEM((hps, s, LANES), BF16), pltpu.VMEM((nblk, wd, tq), BF16),
                        pltpu.VMEM((hps, nblk, hd), F32)] + _flash_scratch(hps, tq)
                       + [pltpu.VMEM((hps, tq, tq), F32)] * 2,
        name="moba",
    )(jnp.asarray(_alibi_query_features(tq)), proj, proj, proj, proj, gq.reshape(hd, 1), gk.reshape(1, hd))


def _cmp_kernel(xk_ref, xv_ref, wka_ref, wkb_ref, wva_ref, wvb_ref, pk_ref, pv_ref, bk1_ref, bv1_ref,
                wk2_ref, wv2_ref, bk2_ref, bv2_ref, gn_ref, ko_ref, vo_ref):
    half = xk_ref.shape[2]
    n = xk_ref.shape[1]

    def hidden(x_ref, wa_ref, wb_ref, pos_ref, b1_ref):
        x = x_ref[0].astype(BF16)
        first = _dot(x, wa_ref[...])
        second = _dot(x, wb_ref[...])
        pos = jnp.broadcast_to(pos_ref[...], (8, 2 * half)).astype(BF16)
        ph = _dot(pos[:, :half], wa_ref[...]) + _dot(pos[:, half:], wb_ref[...])
        return _silu(first + pltpu.roll(second, n - 1, 0) + ph[0:1, :] + b1_ref[...])

    hk = hidden(xk_ref, wka_ref, wkb_ref, pk_ref, bk1_ref).astype(BF16)
    hv = hidden(xv_ref, wva_ref, wvb_ref, pv_ref, bv1_ref).astype(BF16)
    nh = NSA_CMP_HIDDEN
    ks, vs = [], []
    for g in range(KV_HEADS):
        ks.append(_rms(_dot(hk[:, g * nh:(g + 1) * nh], wk2_ref[...]) + bk2_ref[...], gn_ref[...]))
        vs.append(_dot(hv[:, g * nh:(g + 1) * nh], wv2_ref[...]) + bv2_ref[...])
    ko_ref[0] = jnp.concatenate(ks, axis=1)
    vo_ref[0] = jnp.concatenate(vs, axis=1)


def _expand_w1(w1):
    hid = w1.shape[1]
    w = w1.reshape(NSA_CMP_LEN, HEAD_DIM, hid)
    eye = jnp.eye(KV_HEADS, dtype=w1.dtype)
    w = jnp.einsum('ldn,gh->lgdhn', w, eye).reshape(NSA_CMP_LEN * KV_HEADS * HEAD_DIM, KV_HEADS * hid)
    half = w.shape[0] // 2
    return w[:half].astype(BF16), w[half:].astype(BF16)


def _compress(xk, xv, pos_k, pos_v, kw1, kb1, kw2, kb2, vw1, vb1, vw2, vb2, gn):
    bsz, n, half = xk.shape
    hd = HEAD_DIM
    wka, wkb = _expand_w1(kw1)
    wva, wvb = _expand_w1(vw1)
    tile_pos = lambda p: jnp.tile(p[:, None, :], (1, KV_HEADS, 1)).reshape(1, 2 * half)
    tile_b = lambda b: jnp.tile(b.reshape(1, -1), (1, KV_HEADS))
    const = lambda shape: pl.BlockSpec(shape, lambda b: (0,) * len(shape))
    xs = pl.BlockSpec((1, n, half), lambda b: (b, 0, 0))
    hw = KV_HEADS * NSA_CMP_HIDDEN
    out = pl.BlockSpec((1, n, KV_HEADS * hd), lambda b: (b, 0, 0))
    return pl.pallas_call(
        _cmp_kernel,
        grid=(bsz,),
        in_specs=[xs, xs, const((half, hw)), const((half, hw)), const((half, hw)), const((half, hw)),
                  const((1, 2 * half)), const((1, 2 * half)), const((1, hw)), const((1, hw)),
                  const((NSA_CMP_HIDDEN, hd)), const((NSA_CMP_HIDDEN, hd)), const((1, hd)), const((1, hd)),
                  const((1, hd))],
        out_specs=[out, out],
        out_shape=[jax.ShapeDtypeStruct((bsz, n, KV_HEADS * hd), F32)] * 2,
        name="nsa_compress",
    )(xk, xv, wka, wkb, wva, wvb, tile_pos(pos_k), tile_pos(pos_v), tile_b(kb1), tile_b(vb1),
      kw2.astype(BF16), vw2.astype(BF16), kb2.reshape(1, hd), vb2.reshape(1, hd), gn.reshape(1, hd))


def _group_query(q_t, alibi_ref, g, gq_col):
    hd = HEAD_DIM
    heads = range(Q_PER_KV * g, Q_PER_KV * (g + 1))
    qn = jnp.concatenate([_norm_query_t(q_t[h * hd:(h + 1) * hd], gq_col) for h in heads], axis=1)
    alibi = jnp.concatenate([alibi_ref[h] for h in heads], axis=1)
    return qn, alibi


def _augment_keys(k_ref, gk_ref, feat, dst_ref):
    hd = HEAD_DIM
    for g in range(KV_HEADS):
        kn = _rms(k_ref[0, :, g * hd:(g + 1) * hd], gk_ref[...])
        dst_ref[g] = jnp.concatenate([kn.astype(BF16), feat], axis=1)


def _chunk_scores(ka_ref, vt_ref, g, c, qa, tq):
    off = pl.multiple_of(c * tq, tq)
    return _dot(ka_ref[g, pl.ds(off, tq), :], qa), vt_ref[c, g * HEAD_DIM:(g + 1) * HEAD_DIM, :]


def _window_first_chunk(ka_ref, vt_ref, g, far, qa, tq, edge_cap):
    sc, v_t = _chunk_scores(ka_ref, vt_ref, g, jnp.maximum(far, 0), qa, tq)
    return jnp.minimum(sc, jnp.minimum(edge_cap, jnp.where(far >= 0, BIG, -BIG))), v_t


def _regroup_heads(o_t, tq):
    return jnp.concatenate([o_t[:, k * tq:(k + 1) * tq] for k in range(Q_PER_KV)], axis=0)


def _nsa_kernel(alibi_ref, q_ref, kc_ref, vc_ref, ks_ref, vs_ref, kw_ref, vw_ref, gate_ref, z_ref,
                gq_ref, gks_ref, gkw_ref, o_ref, kca_s, vct_s, ksa_s, vst_s, kwa_s, vwt_s, m_s, acc_s,
                sc0_s, sc1_s):
    i = pl.program_id(1)
    tq = q_ref.shape[1]
    s = ks_ref.shape[1]
    hd = HEAD_DIM
    ncmp = kc_ref.shape[1]
    nsb = s // NSA_SEL_BLOCK
    n4 = Q_PER_KV * tq

    @pl.when(i == 0)
    def _():
        pos = _iota((s, hd), 0)
        _augment_keys(ks_ref, gks_ref, _key_features(pos, sel_shift=NSA_SEL_BLOCK.bit_length() - 1), ksa_s)
        _augment_keys(kw_ref, gkw_ref, _key_features(pos), kwa_s)
        _store_chunked_t(vs_ref[0], vst_s, tq)
        _store_chunked_t(vw_ref[0], vwt_s, tq)
        cfeat = _key_features(_iota((ncmp, hd), 0) * NSA_CMP_STRIDE + (NSA_CMP_LEN - 1))
        for g in range(KV_HEADS):
            kca_s[g] = jnp.concatenate([kc_ref[0, :, g * hd:(g + 1) * hd].astype(BF16), cfeat], axis=1)
        vct_s[...] = vc_ref[0].T.astype(BF16)

    t0 = i * tq
    q_t = q_ref[0].T
    gates_t = _sigmoid(gate_ref[0]).T
    diag_cap = _causal_cap(tq, tq, Q_PER_KV)
    edge_cap = _window_edge_cap(tq, tq, Q_PER_KV)
    cend =_iota((ncmp, tq), 0) * NSA_CMP_STRIDE + (NSA_CMP_LEN - 1)
    seen = _tile_lanes(cend <= t0 + _iota((ncmp, tq), 1), Q_PER_KV)
    cstart = _iota((nsb, ncmp), 1) * NSA_CMP_STRIDE
    bstart = _iota((nsb, ncmp), 0) * NSA_SEL_BLOCK
    overlap_t = jnp.where((cstart < bstart + NSA_SEL_BLOCK) & (cstart + NSA_CMP_LEN > bstart), 1.0, 0.0)
    blk = _iota((nsb, tq), 0)
    cur = (t0 + _iota((nsb, tq), 1)) >> (NSA_SEL_BLOCK.bit_length() - 1)
    forced = (blk == 0) | (blk == cur) | (blk == cur - 1)
    groups = range(KV_HEADS)
    far = i - NSA_WINDOW // tq
    lo = jnp.maximum(far + 1, 0)

    state = (m_s, acc_s)
    sel_slots = tuple(groups)
    win_slots = tuple(KV_HEADS + g for g in groups)
    _flash_reset(state)
    queries = [_group_query(q_t, alibi_ref, g, gq_ref[...]) for g in groups]
    qa_win = [_augment_query(qn, alibi) for qn, alibi in queries]
    qa_sel = []

    def scores(slot, c):
        g = slot % KV_HEADS
        if slot in sel_slots:
            return _chunk_scores(ksa_s, vst_s, g, c, qa_sel[g], tq)[0]
        return _chunk_scores(kwa_s, vwt_s, g, c, qa_win[g], tq)[0]

    def values(slot, c):
        g = slot % KV_HEADS
        return (vst_s if slot in sel_slots else vwt_s)[c, g * hd:(g + 1) * hd, :]

    first = [_window_first_chunk(kwa_s, vwt_s, g, far, qa_win[g], tq, edge_cap) for g in groups]
    cmp_sc = [_dot(kca_s[g], qa_win[g]) for g in groups]
    sc_win_lo = [scores(slot, lo) for slot in win_slots]
    _flash_update_all(state, win_slots, first)
    o_cmp = []
    for g in groups:
        qn, alibi = queries[g]
        sc = jnp.where(seen, cmp_sc[g], -BIG)
        e = jnp.where(seen, jnp.exp2(sc - jnp.max(sc, axis=0, keepdims=True)), 0.0)
        p_c = e / jnp.maximum(jnp.sum(e, axis=0, keepdims=True), TINY)
        o_cmp.append(_dot(vct_s[g * hd:(g + 1) * hd, :], p_c.astype(BF16)))

        psum = p_c[:, 0:tq]
        for k in range(1, Q_PER_KV):
            psum = psum + p_c[:, k * tq:(k + 1) * tq]
        imp = _dot_exact(overlap_t, psum)
        imp = jnp.where(blk <= cur, jnp.where(forced, NSA_FORCE, imp), NEG)
        chosen = _rank_select(imp, blk, nsb, min(NSA_TOPN, nsb))
        sel_bias = _tile_lanes(jnp.where(chosen, 0.0, -BIG), Q_PER_KV)
        qa_sel.append(_augment_query(qn, alibi, sel_bias))

    bufs = (sc0_s, sc1_s)
    for slot in sel_slots:
        sc0_s[slot] = scores(slot, 0)
    for parity, buf in enumerate(bufs):
        @pl.when((lo & 1) == parity)
        def _(buf=buf):
            for slot, sc in zip(win_slots, sc_win_lo):
                buf[slot] = sc

    def body(slots, c, carry):
        _flash_update_pipelined(state, bufs, slots, c, lambda slot: scores(slot, c + 1),
                                lambda slot: values(slot, c))
        return carry

    both_slots = sel_slots + win_slots
    lax.fori_loop(0, lo, functools.partial(body, sel_slots), 0)
    lax.fori_loop(lo, i, functools.partial(body, both_slots), 0)
    _flash_update_pipelined(state, bufs, both_slots, i, None, lambda slot: values(slot, i), diag_cap)

    outs = []
    for g in groups:
        def gate_row(branch):
            rows = [gates_t[branch * N_HEADS + h:branch * N_HEADS + h + 1, :]
                    for h in range(Q_PER_KV * g, Q_PER_KV * (g + 1))]
            return jnp.concatenate(rows, axis=1)

        o_slc = _flash_result(state, sel_slots[g])
        o_win = _flash_result(state, win_slots[g])
        o = gate_row(0) * o_cmp[g] + gate_row(1) * o_slc + gate_row(2) * o_win
        outs.append(_regroup_heads(o, tq))
    y = jnp.concatenate(outs, axis=0).T * _silu(z_ref[0])
    o_ref[0] = y.astype(o_ref.dtype)


def _nsa(proj, kcn, vc, gq, gks, gkw, *, tq=256):
    bsz, s, _ = proj.shape
    hd = HEAD_DIM
    gw = GROUP_WIDTH
    ncmp = kcn.shape[1]
    assert SEL_FEAT0 + s // NSA_SEL_BLOCK <= hd and ncmp == LANES
    wide = lambda c0: pl.BlockSpec((1, tq, gw), lambda b, i: (b, i, c0 // 4))
    full = lambda c0: pl.BlockSpec((1, s, LANES), lambda b, i: (b, 0, c0))
    cmp = pl.BlockSpec((1, ncmp, LANES), lambda b, i: (b, 0, 0))
    row = pl.BlockSpec((1, hd), lambda b, i: (0, 0))
    ka = pltpu.VMEM((KV_HEADS, s, LANES), BF16)
    vt = pltpu.VMEM((s // tq, LANES, tq), BF16)
    return pl.pallas_call(
        _nsa_kernel,
        grid=(bsz, s // tq),
        in_specs=[pl.BlockSpec((N_HEADS, 2 * SLOPE_PIECES, tq), lambda b, i: (0, 0, 0)),
                  wide(ODD_QC), cmp, cmp, full(ODD_KS), full(ODD_VS), full(ODD_KW), full(ODD_VW),
                  pl.BlockSpec((1, tq, LANES), lambda b, i: (b, i, ODD_GC)), wide(ODD_ZC),
                  pl.BlockSpec((hd, 1), lambda b, i: (0, 0)), row, row],
        out_specs=pl.BlockSpec((1, tq, gw), lambda b, i: (b, i, 0)),
        out_shape=jax.ShapeDtypeStruct((bsz, s, gw), BF16),
        scratch_shapes=[pltpu.VMEM((KV_HEADS, ncmp, LANES), BF16), pltpu.VMEM((LANES, ncmp), BF16),
                        ka, vt, ka, vt] + _flash_scratch(2 * KV_HEADS, Q_PER_KV * tq)
                       + [pltpu.VMEM((2 * KV_HEADS, tq, Q_PER_KV * tq), F32)] * 2,
        name="nsa",
    )(jnp.asarray(_alibi_query_features(tq)), proj, kcn, vc, proj, proj, proj, proj, proj, proj,
      gq.reshape(hd, 1), gks.reshape(1, hd), gkw.reshape(1, hd))


def _swa_kernel(sinks_ref, alibi_ref, q_ref, k_ref, v_ref, z_ref, gq_ref, gk_ref, o_ref, ka_s, vt_s):
    i = pl.program_id(1)
    tq = q_ref.shape[1]
    s = k_ref.shape[1]

    @pl.when(i == 0)
    def _():
        _augment_keys(k_ref, gk_ref, _key_features(_iota((s, HEAD_DIM), 0)), ka_s)
        _store_chunked_t(v_ref[0], vt_s, tq)

    q_t = q_ref[0].T
    diag_cap = _causal_cap(tq, tq, Q_PER_KV)
    edge_cap = _window_edge_cap(tq, tq, Q_PER_KV)
    tpos =(i * tq + _iota((1, tq), 1)).astype(F32)
    outs = []
    assert SWA_WINDOW == tq
    groups = range(KV_HEADS)
    qas = [_augment_query(*_group_query(q_t, alibi_ref, g, gq_ref[...])) for g in groups]
    first = [_window_first_chunk(ka_s, vt_s, g, i - 1, qas[g], tq, edge_cap) for g in groups]
    own = [_chunk_scores(ka_s, vt_s, g, i, qas[g], tq) for g in groups]
    carries = _flash_steps([_flash_init(Q_PER_KV * tq) for _ in groups], first)
    carries = _flash_steps(carries, own, diag_cap)
    for g in groups:
        m, acc, l = _flash_split(carries[g])
        sink = jnp.concatenate([sinks_ref[h] * LOG2E + (SLOPES[h] * LOG2E) * tpos
                                for h in range(Q_PER_KV * g, Q_PER_KV * (g + 1))], axis=1)
        mf = jnp.maximum(m, sink)
        alpha = jnp.exp2(m - mf)
        o = acc * alpha / jnp.maximum(l * alpha + jnp.exp2(sink - mf), TINY)
        outs.append(_regroup_heads(o, tq))
    y = jnp.concatenate(outs, axis=0).T * _silu(z_ref[0])
    o_ref[0] = y.astype(o_ref.dtype)


def _swa(proj, sinks, gq, gk, *, tq=128):
    bsz, s, _ = proj.shape
    hd = HEAD_DIM
    gw = GROUP_WIDTH
    wide = lambda c0: pl.BlockSpec((1, tq, gw), lambda b, i: (b, i, c0 // 4))
    full = lambda c0: pl.BlockSpec((1, s, LANES), lambda b, i: (b, 0, c0))
    return pl.pallas_call(
        _swa_kernel,
        grid=(bsz, s // tq),
        in_specs=[pl.BlockSpec(memory_space=pltpu.SMEM),
                  pl.BlockSpec((N_HEADS, 2 * SLOPE_PIECES, tq), lambda b, i: (0, 0, 0)),
                  wide(ODD_QD), full(ODD_KD), full(ODD_VD), wide(ODD_ZD),
                  pl.BlockSpec((hd, 1), lambda b, i: (0, 0)), pl.BlockSpec((1, hd), lambda b, i: (0, 0))],
        out_specs=pl.BlockSpec((1, tq, gw), lambda b, i: (b, i, 0)),
        out_shape=jax.ShapeDtypeStruct((bsz, s, gw), BF16),
        scratch_shapes=[pltpu.VMEM((KV_HEADS, s, LANES), BF16), pltpu.VMEM((s // tq, LANES, tq), BF16)],
        name="swa",
    )(sinks.astype(F32), jnp.asarray(_alibi_query_features(tq)), proj, proj, proj, proj,
      gq.reshape(hd, 1), gk.reshape(1, hd))


def _pack_odd_weight(w):
    gw, kw, ng = GROUP_WIDTH, KV_HEADS * HEAD_DIM, 3 * N_HEADS
    sizes = [gw, kw, kw, kw, kw, kw, kw, ng, gw, gw, kw, kw, gw]
    starts = np.concatenate([[0], np.cumsum(sizes)]).tolist()
    part = lambda k: w[:, starts[k]:starts[k + 1]]
    pad = jnp.zeros((w.shape[0], LANES - ng), w.dtype)
    order = [0, 8, 9, 12, 1, 2, 3, 4, 5, 6, 10, 11, 7]
    return jnp.concatenate([part(k) for k in order] + [pad], axis=1).astype(BF16)


def kernel(x, norm_g, w_out, e_w_in, a_conv_w, a_conv_b, a_ln_g, a_ln_b, b_qnorm_g, b_knorm_g, o_w_in, c_qnorm_g, c_knorm_cmp_g, c_knorm_slc_g, c_knorm_win_g, c_pos_k, c_pos_v, c_k_w1, c_k_b1, c_k_w2, c_k_b2, c_v_w1, c_v_b1, c_v_w2, c_v_b2, d_qnorm_g, d_knorm_g, d_sinks):
    bsz, s, d = x.shape
    m = bsz * s
    assert s % MOBA_BLOCK == 0 and d == 2 * GROUP_WIDTH
    x2 = x.reshape(m, d)

    proj = _inproj(x2, norm_g[0], e_w_in[0].astype(BF16), chunk=GROUP_WIDTH).reshape(bsz, s, -1)
    y_a = _conv_module(proj, a_conv_w[0], a_conv_b[0], a_ln_g[0], a_ln_b[0])
    y_b = _moba(proj, b_qnorm_g[0], b_knorm_g[0])
    x2 = _outproj(x2, y_a.reshape(m, -1), y_b.reshape(m, -1), w_out[0].astype(BF16))

    proj = _inproj(x2, norm_g[1], _pack_odd_weight(o_w_in[0]), chunk=5 * LANES).reshape(bsz, s, ODD_COLS)
    rows16 = lambda c0: proj[:, :, c0 * LANES:(c0 + 1) * LANES].reshape(bsz, s // NSA_CMP_STRIDE,
                                                                       NSA_CMP_STRIDE * LANES)
    kcn, vc = _compress(rows16(ODD_KC), rows16(ODD_VC), c_pos_k[0], c_pos_v[0], c_k_w1[0], c_k_b1[0],
                        c_k_w2[0], c_k_b2[0], c_v_w1[0], c_v_b1[0], c_v_w2[0], c_v_b2[0], c_knorm_cmp_g[0])
    y_c = _nsa(proj, kcn, vc, c_qnorm_g[0], c_knorm_slc_g[0], c_knorm_win_g[0])
    y_d = _swa(proj, d_sinks[0], d_qnorm_g[0], d_knorm_g[0])
    x2 = _outproj(x2, y_c.reshape(m, -1), y_d.reshape(m, -1), w_out[1].astype(BF16))
    return x2.reshape(bsz, s, d)
```

```python
import functools

import ml_dtypes
import numpy as np
import jax
import jax.numpy as jnp
from jax import lax
from jax.experimental import pallas as pl
from jax.experimental.pallas import tpu as pltpu

HEAD_DIM = 64
N_HEADS = 8
GROUP_WIDTH = N_HEADS * HEAD_DIM
CONV_WIDTH = 31
MOBA_BLOCK = 256
MOBA_TOPK = 3
KV_HEADS = 2
Q_PER_KV = N_HEADS // KV_HEADS
NSA_CMP_LEN = 32
NSA_CMP_STRIDE = 16
NSA_CMP_HIDDEN = 256
NSA_SEL_BLOCK = 64
NSA_TOPN = 16
NSA_WINDOW = 512
NSA_FORCE = 1e4
SWA_WINDOW = 128
EPS = 1e-6
NEG = -1e30
TINY = 1e-30
LANES = 128
SUBLANES = 8
CONV_HALO = 32

LOG2E = float(np.log2(np.e))
BIG = 2.0 ** 99
M_INIT = -1e38
POS_SPLIT_SHIFT = 8
SLOPE_PIECES = 4
SEL_FEAT0 = 2 * SLOPE_PIECES

F32 = jnp.float32
BF16 = jnp.bfloat16

ODD_QC, ODD_ZC, ODD_QD, ODD_ZD = 0, 4, 8, 12
ODD_KC, ODD_VC, ODD_KS, ODD_VS, ODD_KW, ODD_VW, ODD_KD, ODD_VD, ODD_GC = 16, 17, 18, 19, 20, 21, 22, 23, 24
ODD_COLS = 25 * LANES


def _alibi_slopes(n):
    return [float(2.0 ** (-8.0 * (i + 1) / n)) for i in range(n)]


SLOPES = _alibi_slopes(N_HEADS)


def _alibi_query_features(width):
    table = np.zeros((N_HEADS, 2 * SLOPE_PIECES, width), np.float32)
    for h, slope in enumerate(SLOPES):
        rest = np.float64(slope) * LOG2E
        for k in range(SLOPE_PIECES):
            piece = float(np.float32(rest).astype(ml_dtypes.bfloat16).astype(np.float32))
            table[h, k] = piece
            table[h, SLOPE_PIECES + k] = piece
            rest -= piece
    return table


def _dot(a, b):
    return jnp.dot(a, b, preferred_element_type=F32)


def _dot_exact(a, b):
    return jnp.dot(a, b, preferred_element_type=F32, precision=lax.Precision.HIGHEST)


def _sigmoid(x):
    return 1.0 / (1.0 + jnp.exp(-x))


def _silu(x):
    return x * _sigmoid(x)


def _rms(x, g):
    return x * lax.rsqrt(jnp.mean(x * x, axis=-1, keepdims=True) + EPS) * g


def _iota(shape, dim):
    return lax.broadcasted_iota(jnp.int32, shape, dim)


def _inproj_kernel(x_ref, g_ref, w_ref, o_ref, *, chunk):
    x = x_ref[...]
    h = _rms(x, g_ref[...]).astype(BF16)
    for c in range(o_ref.shape[1] // chunk):
        o_ref[:, c * chunk:(c + 1) * chunk] = _dot(h, w_ref[:, c * chunk:(c + 1) * chunk])


def _inproj(x2d, g, w, *, tm=512, chunk):
    m, d = x2d.shape
    e = w.shape[1]
    return pl.pallas_call(
        functools.partial(_inproj_kernel, chunk=chunk),
        grid=(m // tm,),
        in_specs=[pl.BlockSpec((tm, d), lambda i: (i, 0)),
                  pl.BlockSpec((1, d), lambda i: (0, 0)),
                  pl.BlockSpec((d, e), lambda i: (0, 0))],
        out_specs=pl.BlockSpec((tm, e), lambda i: (i, 0)),
        out_shape=jax.ShapeDtypeStruct((m, e), F32),
        name="inproj",
    )(x2d, g.reshape(1, d), w)


def _outproj_kernel(x_ref, ya_ref, yb_ref, wa_ref, wb_ref, o_ref):
    o_ref[...] = x_ref[...] + _dot(ya_ref[...], wa_ref[...]) + _dot(yb_ref[...], wb_ref[...])


def _outproj(x2d, ya, yb, w, *, tm=512):
    m, d = x2d.shape
    gw = ya.shape[1]
    return pl.pallas_call(
        _outproj_kernel,
        grid=(m // tm,),
        in_specs=[pl.BlockSpec((tm, d), lambda i: (i, 0)),
                  pl.BlockSpec((tm, gw), lambda i: (i, 0)),
                  pl.BlockSpec((tm, gw), lambda i: (i, 0)),
                  pl.BlockSpec((gw, d), lambda i: (0, 0)),
                  pl.BlockSpec((gw, d), lambda i: (0, 0))],
        out_specs=pl.BlockSpec((tm, d), lambda i: (i, 0)),
        out_shape=jax.ShapeDtypeStruct((m, d), F32),
        name="outproj",
    )(x2d, ya, yb, w[:gw], w[gw:])


def _conv_kernel(uv_ref, ug_ref, uvh_ref, ugh_ref, z_ref, w_ref, b_ref, lg_ref, lb_ref, o_ref, h_s, *, rows):
    i = pl.program_id(1)
    ts = uv_ref.shape[1]
    halo = uvh_ref[0] * _sigmoid(ugh_ref[0])
    h_s[0:CONV_HALO, :] = jnp.where(i > 0, halo, 0.0)
    h_s[CONV_HALO:, :] = uv_ref[0] * _sigmoid(ug_ref[0])
    base = CONV_HALO - (CONV_WIDTH - 1)
    for c in range(ts // rows):
        acc = None
        for b in range(SUBLANES):
            n = rows if b == 0 else rows + SUBLANES
            part = None
            for a in range((base + CONV_WIDTH - 1) // SUBLANES + 1):
                j = SUBLANES * a + b - base
                if 0 <= j < CONV_WIDTH:
                    term = w_ref[j:j + 1, :] * h_s[c * rows + SUBLANES * a:c * rows + SUBLANES * a + n, :]
                    part = term if part is None else part + term
            part = part[b:b + rows]
            acc = part if acc is None else acc + part
        y = acc + b_ref[...]
        mu = jnp.mean(y, axis=-1, keepdims=True)
        yc = y - mu
        y = yc * lax.rsqrt(jnp.mean(yc * yc, axis=-1, keepdims=True) + EPS) * lg_ref[...] + lb_ref[...]
        out = _silu(y) * _silu(z_ref[0, c * rows:(c + 1) * rows, :])
        o_ref[0, c * rows:(c + 1) * rows, :] = out.astype(o_ref.dtype)


def _conv_module(proj, w, b, lg, lb, *, ts=256, rows=64):
    bsz, s, _ = proj.shape
    gw = GROUP_WIDTH
    hb = ts // CONV_HALO
    cur = lambda col: pl.BlockSpec((1, ts, gw), lambda bi, i: (bi, i, col))
    halo = lambda col: pl.BlockSpec((1, CONV_HALO, gw), lambda bi, i: (bi, jnp.maximum(i * hb - 1, 0), col))
    vec = pl.BlockSpec((1, gw), lambda bi, i: (0, 0))
    return pl.pallas_call(
        functools.partial(_conv_kernel, rows=rows),
        grid=(bsz, s // ts),
        in_specs=[cur(0), cur(1), halo(0), halo(1), cur(2),
                  pl.BlockSpec((CONV_WIDTH, gw), lambda bi, i: (0, 0)), vec, vec, vec],
        out_specs=pl.BlockSpec((1, ts, gw), lambda bi, i: (bi, i, 0)),
        out_shape=jax.ShapeDtypeStruct((bsz, s, gw), BF16),
        scratch_shapes=[pltpu.VMEM((CONV_HALO + ts, gw), F32)],
        name="conv_module",
    )(proj, proj, proj, proj, proj, w, b.reshape(1, gw), lg.reshape(1, gw), lb.reshape(1, gw))


def _key_features(pos, sel_shift=None):
    col = _iota(pos.shape, 1)
    hi = (pos >> POS_SPLIT_SHIFT) << POS_SPLIT_SHIFT
    lo = pos & ((1 << POS_SPLIT_SHIFT) - 1)
    feat = jnp.where(col < SLOPE_PIECES, hi, jnp.where(col < 2 * SLOPE_PIECES, lo, 0))
    if sel_shift is not None:
        feat = jnp.where(col - SEL_FEAT0 == (pos >> sel_shift), 1, feat)
    return feat.astype(F32).astype(BF16)


def _norm_query_t(x, g_col):
    ss = jnp.mean(x * x, axis=0, keepdims=True)
    return x * lax.rsqrt(ss + EPS) * g_col * (HEAD_DIM ** -0.5 * LOG2E)


def _augment_query(q_t, alibi, sel_bias=None):
    n = q_t.shape[1]
    parts = [q_t, alibi]
    used = alibi.shape[0]
    if sel_bias is not None:
        parts.append(sel_bias)
        used += sel_bias.shape[0]
    parts.append(jnp.zeros((HEAD_DIM - used, n), F32))
    return jnp.concatenate(parts, axis=0).astype(BF16)


ONES_ROWS = 16
ACC_ROWS = HEAD_DIM + ONES_ROWS


def _flash_init(n):
    return jnp.full((1, n), M_INIT, F32), jnp.zeros((ACC_ROWS, n), F32)


def _flash_step(carry, sc, v_t):
    m, acc = carry
    m_new = jnp.maximum(m, jnp.max(sc, axis=0, keepdims=True))
    p = jnp.exp2(sc - m_new).astype(BF16)
    v_ones = jnp.concatenate([v_t, jnp.ones((ONES_ROWS, v_t.shape[1]), BF16)], axis=0)
    return m_new, jnp.exp2(m - m_new) * acc + _dot(v_ones, p)


def _flash_split(carry):
    m, acc = carry
    return m, acc[:HEAD_DIM], acc[HEAD_DIM:HEAD_DIM + 1]


def _flash_steps(carries, chunks, cap=None):
    return tuple(_flash_step(carry, sc if cap is None else jnp.minimum(sc, cap), v_t)
                 for carry, (sc, v_t) in zip(carries, chunks))


def _flash_reset(state):
    m_ref, acc_ref = state
    m_ref[...] = jnp.full(m_ref.shape, M_INIT, F32)
    acc_ref[...] = jnp.zeros(acc_ref.shape, F32)


def _flash_update(state, slot, sc, v_t, cap=None):
    m_ref, acc_ref = state
    m_ref[slot], acc_ref[slot] = _flash_step((m_ref[slot], acc_ref[slot]),
                                             sc if cap is None else jnp.minimum(sc, cap), v_t)


def _flash_update_all(state, slots, chunks, cap=None):
    for slot, (sc, v_t) in zip(slots, chunks):
        _flash_update(state, slot, sc, v_t, cap)


def _flash_update_pipelined(state, bufs, slots, c, next_scores, values, cap=None):
    def run(src_ref, dst_ref):
        for slot in slots:
            fresh = None if next_scores is None else next_scores(slot)
            _flash_update(state, slot, src_ref[slot], values(slot), cap)
            if fresh is not None:
                dst_ref[slot] = fresh

    pl.when((c & 1) == 0)(lambda: run(bufs[0], bufs[1]))
    pl.when((c & 1) == 1)(lambda: run(bufs[1], bufs[0]))


def _flash_result(state, slot):
    m_ref, acc_ref = state
    m, out, den = _flash_split((m_ref[slot], acc_ref[slot]))
    return out / jnp.maximum(den, TINY)


def _flash_scratch(chains, n):
    return [pltpu.VMEM((chains, 1, n), F32), pltpu.VMEM((chains, ACC_ROWS, n), F32)]


def _rank_select(score, blk, limit, count):
    rank = jnp.zeros(score.shape, jnp.int32)
    for mm in range(limit):
        gm = score[mm:mm + 1, :]
        beats = (gm > score) | ((gm == score) & (mm < blk))
        rank = rank + jnp.where(beats, 1, 0)
    return rank < count


def _tile_lanes(x, reps):
    return jnp.concatenate([x] * reps, axis=1) if reps > 1 else x


def _causal_cap(tk, tq, reps):
    keep = _iota((tk, tq), 0) <= _iota((tk, tq), 1)
    return _tile_lanes(jnp.where(keep, BIG, -BIG), reps)


def _window_edge_cap(tk, tq, reps):
    keep = _iota((tk, tq), 0) > _iota((tk, tq), 1)
    return _tile_lanes(jnp.where(keep, BIG, -BIG), reps)


def _store_chunked_t(x, dst_ref, tk):
    x_t = x.T.astype(BF16)
    for c in range(x.shape[0] // tk):
        dst_ref[c] = x_t[:, c * tk:(c + 1) * tk]


def _moba_kernel(alibi_ref, q_ref, k_ref, v_ref, z_ref, gq_ref, gk_ref, o_ref, ka_s, vt_s, km_s,
                 m_s, acc_s, sc0_s, sc1_s):
    i = pl.program_id(2)
    s = k_ref.shape[1]
    nblk = s // MOBA_BLOCK
    tq = MOBA_BLOCK
    hd = HEAD_DIM
    heads = range(ka_s.shape[0])

    @pl.when(i == 0)
    def _():
        feat = _key_features(_iota((s, hd), 0), sel_shift=MOBA_BLOCK.bit_length() - 1)
        for j in heads:
            kn = _rms(k_ref[0, :, j * hd:(j + 1) * hd], gk_ref[...])
            ka_s[j] = jnp.concatenate([kn.astype(BF16), feat], axis=1)
            km_s[j] = jnp.mean(kn.reshape(nblk, MOBA_BLOCK, hd), axis=1)
        _store_chunked_t(v_ref[0], vt_s, tq)

    q_t = q_ref[0].T
    blk = _iota((nblk, tq), 0)
    qas = []
    for j in heads:
        qn = _norm_query_t(q_t[j * hd:(j + 1) * hd], gq_ref[...])
        gate = jnp.where(blk < i, _dot_exact(km_s[j], qn), NEG)
        chosen = _rank_select(gate, blk, nblk, MOBA_TOPK) & (blk < i)
        sel_bias = jnp.where(chosen | (blk == i), 0.0, -BIG)
        qas.append(_augment_query(qn, alibi_ref[j], sel_bias))

    def scores(j, n):
        return _dot(ka_s[j, pl.ds(pl.multiple_of(n * tq, tq), tq), :], qas[j])

    def values(j, n):
        return vt_s[n, j * hd:(j + 1) * hd, :]

    state = (m_s, acc_s)
    _flash_reset(state)
    bufs = (sc0_s, sc1_s)
    for j in heads:
        sc0_s[j] = scores(j, 0)

    def body(n, carry):
        _flash_update_pipelined(state, bufs, heads, n, lambda j: scores(j, n + 1), lambda j: values(j, n))
        return carry

    lax.fori_loop(0, i, body, 0)
    diag_cap = _causal_cap(tq, tq, 1)
    _flash_update_pipelined(state, bufs, heads, i, None, lambda j: values(j, i), diag_cap)
    outs = [_flash_result(state, j) for j in heads]
    y = jnp.concatenate(outs, axis=0).T * _silu(z_ref[0])
    o_ref[0] = y.astype(o_ref.dtype)


def _moba(proj, gq, gk, *, heads_per_step=8):
    bsz, s, _ = proj.shape
    tq = MOBA_BLOCK
    hd = HEAD_DIM
    nblk = s // MOBA_BLOCK
    hps = heads_per_step
    wd = hps * hd
    assert SEL_FEAT0 + nblk <= hd and wd % LANES == 0
    q0, k0, v0, z0 = (c * GROUP_WIDTH // wd for c in (3, 4, 5, 6))
    tile = lambda c0: pl.BlockSpec((1, tq, wd), lambda b, p, i: (b, i, c0 + p))
    full = lambda c0: pl.BlockSpec((1, s, wd), lambda b, p, i: (b, 0, c0 + p))
    return pl.pallas_call(
        _moba_kernel,
        grid=(bsz, N_HEADS // hps, s // tq),
        in_specs=[pl.BlockSpec((hps, 2 * SLOPE_PIECES, tq), lambda b, p, i: (p, 0, 0)),
                  tile(q0), full(k0), full(v0), tile(z0),
                  pl.BlockSpec((hd, 1), lambda b, p, i: (0, 0)), pl.BlockSpec((1, hd), lambda b, p, i: (0, 0))],
        out_specs=pl.BlockSpec((1, tq, wd), lambda b, p, i: (b, i, p)),
        out_shape=jax.ShapeDtypeStruct((bsz, s, GROUP_WIDTH), BF16),
        scratch_shapes=[pltpu.VMEM((hps, s, LANES), BF16), pltpu.VMEM((nblk, wd, tq), BF16),
                        pltpu.VMEM((hps, nblk, hd), F32)] + _flash_scratch(hps, tq)
                       + [pltpu.VMEM((hps, tq, tq), F32)] * 2,
        name="moba",
    )(jnp.asarray(_alibi_query_features(tq)), proj, proj, proj, proj, gq.reshape(hd, 1), gk.reshape(1, hd))


def _cmp_kernel(xk_ref, xv_ref, wka_ref, wkb_ref, wva_ref, wvb_ref, pk_ref, pv_ref, bk1_ref, bv1_ref,
                wk2_ref, wv2_ref, bk2_ref, bv2_ref, gn_ref, ko_ref, vo_ref):
    half = xk_ref.shape[2]
    n = xk_ref.shape[1]

    def hidden(x_ref, wa_ref, wb_ref, pos_ref, b1_ref):
        x = x_ref[0].astype(BF16)
        first = _dot(x, wa_ref[...])
        second = _dot(x, wb_ref[...])
        pos = jnp.broadcast_to(pos_ref[...], (8, 2 * half)).astype(BF16)
        ph = _dot(pos[:, :half], wa_ref[...]) + _dot(pos[:, half:], wb_ref[...])
        return _silu(first + pltpu.roll(second, n - 1, 0) + ph[0:1, :] + b1_ref[...])

    hk = hidden(xk_ref, wka_ref, wkb_ref, pk_ref, bk1_ref).astype(BF16)
    hv = hidden(xv_ref, wva_ref, wvb_ref, pv_ref, bv1_ref).astype(BF16)
    nh = NSA_CMP_HIDDEN
    ks, vs = [], []
    for g in range(KV_HEADS):
        ks.append(_rms(_dot(hk[:, g * nh:(g + 1) * nh], wk2_ref[...]) + bk2_ref[...], gn_ref[...]))
        vs.append(_dot(hv[:, g * nh:(g + 1) * nh], wv2_ref[...]) + bv2_ref[...])
    ko_ref[0] = jnp.concatenate(ks, axis=1)
    vo_ref[0] = jnp.concatenate(vs, axis=1)


def _expand_w1(w1):
    hid = w1.shape[1]
    w = w1.reshape(NSA_CMP_LEN, HEAD_DIM, hid)
    eye = jnp.eye(KV_HEADS, dtype=w1.dtype)
    w = jnp.einsum('ldn,gh->lgdhn', w, eye).reshape(NSA_CMP_LEN * KV_HEADS * HEAD_DIM, KV_HEADS * hid)
    half = w.shape[0] // 2
    return w[:half].astype(BF16), w[half:].astype(BF16)


def _compress(xk, xv, pos_k, pos_v, kw1, kb1, kw2, kb2, vw1, vb1, vw2, vb2, gn):
    bsz, n, half = xk.shape
    hd = HEAD_DIM
    wka, wkb = _expand_w1(kw1)
    wva, wvb = _expand_w1(vw1)
    tile_pos = lambda p: jnp.tile(p[:, None, :], (1, KV_HEADS, 1)).reshape(1, 2 * half)
    tile_b = lambda b: jnp.tile(b.reshape(1, -1), (1, KV_HEADS))
    const = lambda shape: pl.BlockSpec(shape, lambda b: (0,) * len(shape))
    xs = pl.BlockSpec((1, n, half), lambda b: (b, 0, 0))
    hw = KV_HEADS * NSA_CMP_HIDDEN
    out = pl.BlockSpec((1, n, KV_HEADS * hd), lambda b: (b, 0, 0))
    return pl.pallas_call(
        _cmp_kernel,
        grid=(bsz,),
        in_specs=[xs, xs, const((half, hw)), const((half, hw)), const((half, hw)), const((half, hw)),
                  const((1, 2 * half)), const((1, 2 * half)), const((1, hw)), const((1, hw)),
                  const((NSA_CMP_HIDDEN, hd)), const((NSA_CMP_HIDDEN, hd)), const((1, hd)), const((1, hd)),
                  const((1, hd))],
        out_specs=[out, out],
        out_shape=[jax.ShapeDtypeStruct((bsz, n, KV_HEADS * hd), F32)] * 2,
        name="nsa_compress",
    )(xk, xv, wka, wkb, wva, wvb, tile_pos(pos_k), tile_pos(pos_v), tile_b(kb1), tile_b(vb1),
      kw2.astype(BF16), vw2.astype(BF16), kb2.reshape(1, hd), vb2.reshape(1, hd), gn.reshape(1, hd))


def _group_query(q_t, alibi_ref, g, gq_col):
    hd = HEAD_DIM
    heads = range(Q_PER_KV * g, Q_PER_KV * (g + 1))
    qn = jnp.concatenate([_norm_query_t(q_t[h * hd:(h + 1) * hd], gq_col) for h in heads], axis=1)
    alibi = jnp.concatenate([alibi_ref[h] for h in heads], axis=1)
    return qn, alibi


def _augment_keys(k_ref, gk_ref, feat, dst_ref):
    hd = HEAD_DIM
    for g in range(KV_HEADS):
        kn = _rms(k_ref[0, :, g * hd:(g + 1) * hd], gk_ref[...])
        dst_ref[g] = jnp.concatenate([kn.astype(BF16), feat], axis=1)


def _chunk_scores(ka_ref, vt_ref, g, c, qa, tq):
    off = pl.multiple_of(c * tq, tq)
    return _dot(ka_ref[g, pl.ds(off, tq), :], qa), vt_ref[c, g * HEAD_DIM:(g + 1) * HEAD_DIM, :]


def _window_first_chunk(ka_ref, vt_ref, g, far, qa, tq, edge_cap):
    sc, v_t = _chunk_scores(ka_ref, vt_ref, g, jnp.maximum(far, 0), qa, tq)
    return jnp.minimum(sc, jnp.minimum(edge_cap, jnp.where(far >= 0, BIG, -BIG))), v_t


def _regroup_heads(o_t, tq):
    return jnp.concatenate([o_t[:, k * tq:(k + 1) * tq] for k in range(Q_PER_KV)], axis=0)


def _nsa_kernel(alibi_ref, q_ref, kc_ref, vc_ref, ks_ref, vs_ref, kw_ref, vw_ref, gate_ref, z_ref,
                gq_ref, gks_ref, gkw_ref, o_ref, kca_s, vct_s, ksa_s, vst_s, kwa_s, vwt_s, m_s, acc_s,
                sc0_s, sc1_s):
    i = pl.program_id(1)
    tq = q_ref.shape[1]
    s = ks_ref.shape[1]
    hd = HEAD_DIM
    ncmp = kc_ref.shape[1]
    nsb = s // NSA_SEL_BLOCK
    n4 = Q_PER_KV * tq

    @pl.when(i == 0)
    def _():
        pos = _iota((s, hd), 0)
        _augment_keys(ks_ref, gks_ref, _key_features(pos, sel_shift=NSA_SEL_BLOCK.bit_length() - 1), ksa_s)
        _augment_keys(kw_ref, gkw_ref, _key_features(pos), kwa_s)
        _store_chunked_t(vs_ref[0], vst_s, tq)
        _store_chunked_t(vw_ref[0], vwt_s, tq)
        cfeat = _key_features(_iota((ncmp, hd), 0) * NSA_CMP_STRIDE + (NSA_CMP_LEN - 1))
        for g in range(KV_HEADS):
            kca_s[g] = jnp.concatenate([kc_ref[0, :, g * hd:(g + 1) * hd].astype(BF16), cfeat], axis=1)
        vct_s[...] = vc_ref[0].T.astype(BF16)

    t0 = i * tq
    q_t = q_ref[0].T
    gates_t = _sigmoid(gate_ref[0]).T
    diag_cap = _causal_cap(tq, tq, Q_PER_KV)
    edge_cap = _window_edge_cap(tq, tq, Q_PER_KV)
    cend =_iota((ncmp, tq), 0) * NSA_CMP_STRIDE + (NSA_CMP_LEN - 1)
    seen = _tile_lanes(cend <= t0 + _iota((ncmp, tq), 1), Q_PER_KV)
    cstart = _iota((nsb, ncmp), 1) * NSA_CMP_STRIDE
    bstart = _iota((nsb, ncmp), 0) * NSA_SEL_BLOCK
    overlap_t = jnp.where((cstart < bstart + NSA_SEL_BLOCK) & (cstart + NSA_CMP_LEN > bstart), 1.0, 0.0)
    blk = _iota((nsb, tq), 0)
    cur = (t0 + _iota((nsb, tq), 1)) >> (NSA_SEL_BLOCK.bit_length() - 1)
    forced = (blk == 0) | (blk == cur) | (blk == cur - 1)
    groups = range(KV_HEADS)
    far = i - NSA_WINDOW // tq
    lo = jnp.maximum(far + 1, 0)

    state = (m_s, acc_s)
    sel_slots = tuple(groups)
    win_slots = tuple(KV_HEADS + g for g in groups)
    _flash_reset(state)
    queries = [_group_query(q_t, alibi_ref, g, gq_ref[...]) for g in groups]
    qa_win = [_augment_query(qn, alibi) for qn, alibi in queries]
    qa_sel = []

    def scores(slot, c):
        g = slot % KV_HEADS
        if slot in sel_slots:
            return _chunk_scores(ksa_s, vst_s, g, c, qa_sel[g], tq)[0]
        return _chunk_scores(kwa_s, vwt_s, g, c, qa_win[g], tq)[0]

    def values(slot, c):
        g = slot % KV_HEADS
        return (vst_s if slot in sel_slots else vwt_s)[c, g * hd:(g + 1) * hd, :]

    first = [_window_first_chunk(kwa_s, vwt_s, g, far, qa_win[g], tq, edge_cap) for g in groups]
    cmp_sc = [_dot(kca_s[g], qa_win[g]) for g in groups]
    sc_win_lo = [scores(slot, lo) for slot in win_slots]
    _flash_update_all(state, win_slots, first)
    o_cmp = []
    for g in groups:
        qn, alibi = queries[g]
        sc = jnp.where(seen, cmp_sc[g], -BIG)
        e = jnp.where(seen, jnp.exp2(sc - jnp.max(sc, axis=0, keepdims=True)), 0.0)
        p_c = e / jnp.maximum(jnp.sum(e, axis=0, keepdims=True), TINY)
        o_cmp.append(_dot(vct_s[g * hd:(g + 1) * hd, :], p_c.astype(BF16)))

        psum = p_c[:, 0:tq]
        for k in range(1, Q_PER_KV):
            psum = psum + p_c[:, k * tq:(k + 1) * tq]
        imp = _dot_exact(overlap_t, psum)
        imp = jnp.where(blk <= cur, jnp.where(forced, NSA_FORCE, imp), NEG)
        chosen = _rank_select(imp, blk, nsb, min(NSA_TOPN, nsb))
        sel_bias = _tile_lanes(jnp.where(chosen, 0.0, -BIG), Q_PER_KV)
        qa_sel.append(_augment_query(qn, alibi, sel_bias))

    bufs = (sc0_s, sc1_s)
    for slot in sel_slots:
        sc0_s[slot] = scores(slot, 0)
    for slot, sc in zip(win_slots, sc_win_lo):
        sc0_s[slot] = sc
        sc1_s[slot] = sc

    def body(slots, c, carry):
        _flash_update_pipelined(state, bufs, slots, c, lambda slot: scores(slot, c + 1),
                                lambda slot: values(slot, c))
        return carry

    both_slots = sel_slots + win_slots
    lax.fori_loop(0, lo, functools.partial(body, sel_slots), 0)
    lax.fori_loop(lo, i, functools.partial(body, both_slots), 0)
    _flash_update_pipelined(state, bufs, both_slots, i, None, lambda slot: values(slot, i), diag_cap)

    outs = []
    for g in groups:
        def gate_row(branch):
            rows = [gates_t[branch * N_HEADS + h:branch * N_HEADS + h + 1, :]
                    for h in range(Q_PER_KV * g, Q_PER_KV * (g + 1))]
            return jnp.concatenate(rows, axis=1)

        o_slc = _flash_result(state, sel_slots[g])
        o_win = _flash_result(state, win_slots[g])
        o = gate_row(0) * o_cmp[g] + gate_row(1) * o_slc + gate_row(2) * o_win
        outs.append(_regroup_heads(o, tq))
    y = jnp.concatenate(outs, axis=0).T * _silu(z_ref[0])
    o_ref[0] = y.astype(o_ref.dtype)


def _nsa(proj, kcn, vc, gq, gks, gkw, *, tq=256):
    bsz, s, _ = proj.shape
    hd = HEAD_DIM
    gw = GROUP_WIDTH
    ncmp = kcn.shape[1]
    assert SEL_FEAT0 + s // NSA_SEL_BLOCK <= hd and ncmp == LANES
    wide = lambda c0: pl.BlockSpec((1, tq, gw), lambda b, i: (b, i, c0 // 4))
    full = lambda c0: pl.BlockSpec((1, s, LANES), lambda b, i: (b, 0, c0))
    cmp = pl.BlockSpec((1, ncmp, LANES), lambda b, i: (b, 0, 0))
    row = pl.BlockSpec((1, hd), lambda b, i: (0, 0))
    ka = pltpu.VMEM((KV_HEADS, s, LANES), BF16)
    vt = pltpu.VMEM((s // tq, LANES, tq), BF16)
    return pl.pallas_call(
        _nsa_kernel,
        grid=(bsz, s // tq),
        in_specs=[pl.BlockSpec((N_HEADS, 2 * SLOPE_PIECES, tq), lambda b, i: (0, 0, 0)),
                  wide(ODD_QC), cmp, cmp, full(ODD_KS), full(ODD_VS), full(ODD_KW), full(ODD_VW),
                  pl.BlockSpec((1, tq, LANES), lambda b, i: (b, i, ODD_GC)), wide(ODD_ZC),
                  pl.BlockSpec((hd, 1), lambda b, i: (0, 0)), row, row],
        out_specs=pl.BlockSpec((1, tq, gw), lambda b, i: (b, i, 0)),
        out_shape=jax.ShapeDtypeStruct((bsz, s, gw), BF16),
        scratch_shapes=[pltpu.VMEM((KV_HEADS, ncmp, LANES), BF16), pltpu.VMEM((LANES, ncmp), BF16),
                        ka, vt, ka, vt] + _flash_scratch(2 * KV_HEADS, Q_PER_KV * tq)
                       + [pltpu.VMEM((2 * KV_HEADS, tq, Q_PER_KV * tq), F32)] * 2,
        name="nsa",
    )(jnp.asarray(_alibi_query_features(tq)), proj, kcn, vc, proj, proj, proj, proj, proj, proj,
      gq.reshape(hd, 1), gks.reshape(1, hd), gkw.reshape(1, hd))


def _swa_kernel(sinks_ref, alibi_ref, q_ref, k_ref, v_ref, z_ref, gq_ref, gk_ref, o_ref, ka_s, vt_s):
    step = pl.program_id(1)
    tq = SWA_WINDOW
    tiles = q_ref.shape[1] // tq
    s = k_ref.shape[1]

    @pl.when(step == 0)
    def _():
        _augment_keys(k_ref, gk_ref, _key_features(_iota((s, HEAD_DIM), 0)), ka_s)
        _store_chunked_t(v_ref[0], vt_s, tq)

    q_t = q_ref[0].T
    diag_cap = _causal_cap(tq, tq, Q_PER_KV)
    edge_cap = _window_edge_cap(tq, tq, Q_PER_KV)
    chains = [(u, g) for u in range(tiles) for g in range(KV_HEADS)]
    qas = [_augment_query(*_group_query(q_t[:, u * tq:(u + 1) * tq], alibi_ref, g, gq_ref[...])) for u, g in chains]
    first = [_window_first_chunk(ka_s, vt_s, g, step * tiles + u - 1, qa, tq, edge_cap)
             for (u, g), qa in zip(chains, qas)]
    own = [_chunk_scores(ka_s, vt_s, g, step * tiles + u, qa, tq) for (u, g), qa in zip(chains, qas)]
    carries = _flash_steps([_flash_init(Q_PER_KV * tq) for _ in chains], first)
    carries = _flash_steps(carries, own, diag_cap)
    outs = []
    for (u, g), carry in zip(chains, carries):
        m, acc, l = _flash_split(carry)
        tpos = ((step * tiles + u) * tq + _iota((1, tq), 1)).astype(F32)
        sink = jnp.concatenate([sinks_ref[h] * LOG2E + (SLOPES[h] * LOG2E) * tpos
                                for h in range(Q_PER_KV * g, Q_PER_KV * (g + 1))], axis=1)
        mf = jnp.maximum(m, sink)
        alpha = jnp.exp2(m - mf)
        o = acc * alpha / jnp.maximum(l * alpha + jnp.exp2(sink - mf), TINY)
        outs.append(_regroup_heads(o, tq))
    o_t = jnp.concatenate([jnp.concatenate(outs[u * KV_HEADS:(u + 1) * KV_HEADS], axis=0) for u in range(tiles)],
                          axis=1)
    y = o_t.T * _silu(z_ref[0])
    o_ref[0] = y.astype(o_ref.dtype)


def _swa(proj, sinks, gq, gk, *, tiles_per_step=4):
    bsz, s, _ = proj.shape
    hd = HEAD_DIM
    gw = GROUP_WIDTH
    tq = SWA_WINDOW
    rows = tiles_per_step * tq
    wide = lambda c0: pl.BlockSpec((1, rows, gw), lambda b, i: (b, i, c0 // 4))
    full = lambda c0: pl.BlockSpec((1, s, LANES), lambda b, i: (b, 0, c0))
    return pl.pallas_call(
        _swa_kernel,
        grid=(bsz, s // rows),
        in_specs=[pl.BlockSpec(memory_space=pltpu.SMEM),
                  pl.BlockSpec((N_HEADS, 2 * SLOPE_PIECES, tq), lambda b, i: (0, 0, 0)),
                  wide(ODD_QD), full(ODD_KD), full(ODD_VD), wide(ODD_ZD),
                  pl.BlockSpec((hd, 1), lambda b, i: (0, 0)), pl.BlockSpec((1, hd), lambda b, i: (0, 0))],
        out_specs=pl.BlockSpec((1, rows, gw), lambda b, i: (b, i, 0)),
        out_shape=jax.ShapeDtypeStruct((bsz, s, gw), BF16),
        scratch_shapes=[pltpu.VMEM((KV_HEADS, s, LANES), BF16), pltpu.VMEM((s // tq, LANES, tq), BF16)],
        name="swa",
    )(sinks.astype(F32), jnp.asarray(_alibi_query_features(tq)), proj, proj, proj, proj,
      gq.reshape(hd, 1), gk.reshape(1, hd))


def _pack_odd_weight(w):
    gw, kw, ng = GROUP_WIDTH, KV_HEADS * HEAD_DIM, 3 * N_HEADS
    sizes = [gw, kw, kw, kw, kw, kw, kw, ng, gw, gw, kw, kw, gw]
    starts = np.concatenate([[0], np.cumsum(sizes)]).tolist()
    part = lambda k: w[:, starts[k]:starts[k + 1]]
    pad = jnp.zeros((w.shape[0], LANES - ng), w.dtype)
    order = [0, 8, 9, 12, 1, 2, 3, 4, 5, 6, 10, 11, 7]
    return jnp.concatenate([part(k) for k in order] + [pad], axis=1).astype(BF16)


def kernel(x, norm_g, w_out, e_w_in, a_conv_w, a_conv_b, a_ln_g, a_ln_b, b_qnorm_g, b_knorm_g, o_w_in, c_qnorm_g, c_knorm_cmp_g, c_knorm_slc_g, c_knorm_win_g, c_pos_k, c_pos_v, c_k_w1, c_k_b1, c_k_w2, c_k_b2, c_v_w1, c_v_b1, c_v_w2, c_v_b2, d_qnorm_g, d_knorm_g, d_sinks):
    bsz, s, d = x.shape
    m = bsz * s
    assert s % MOBA_BLOCK == 0 and d == 2 * GROUP_WIDTH
    x2 = x.reshape(m, d)

    proj = _inproj(x2, norm_g[0], e_w_in[0].astype(BF16), chunk=GROUP_WIDTH).reshape(bsz, s, -1)
    y_a = _conv_module(proj, a_conv_w[0], a_conv_b[0], a_ln_g[0], a_ln_b[0])
    y_b = _moba(proj, b_qnorm_g[0], b_knorm_g[0])
    x2 = _outproj(x2, y_a.reshape(m, -1), y_b.reshape(m, -1), w_out[0].astype(BF16))

    proj = _inproj(x2, norm_g[1], _pack_odd_weight(o_w_in[0]), chunk=5 * LANES).reshape(bsz, s, ODD_COLS)
    rows16 = lambda c0: proj[:, :, c0 * LANES:(c0 + 1) * LANES].reshape(bsz, s // NSA_CMP_STRIDE,
                                                                       NSA_CMP_STRIDE * LANES)
    kcn, vc = _compress(rows16(ODD_KC), rows16(ODD_VC), c_pos_k[0], c_pos_v[0], c_k_w1[0], c_k_b1[0],
                        c_k_w2[0], c_k_b2[0], c_v_w1[0], c_v_b1[0], c_v_w2[0], c_v_b2[0], c_knorm_cmp_g[0])
    y_c = _nsa(proj, kcn, vc, c_qnorm_g[0], c_knorm_slc_g[0], c_knorm_win_g[0])
    y_d = _swa(proj, d_sinks[0], d_qnorm_g[0], d_knorm_g[0])
    x2 = _outproj(x2, y_c.reshape(m, -1), y_d.reshape(m, -1), w_out[1].astype(BF16))
    return x2.reshape(bsz, s, d)
```

```python
import functools

import ml_dtypes
import numpy as np
import jax
import jax.numpy as jnp
from jax import lax
from jax.experimental import pallas as pl
from jax.experimental.pallas import tpu as pltpu

HEAD_DIM = 64
N_HEADS = 8
GROUP_WIDTH = N_HEADS * HEAD_DIM
CONV_WIDTH = 31
MOBA_BLOCK = 256
MOBA_TOPK = 3
KV_HEADS = 2
Q_PER_KV = N_HEADS // KV_HEADS
NSA_CMP_LEN = 32
NSA_CMP_STRIDE = 16
NSA_CMP_HIDDEN = 256
NSA_SEL_BLOCK = 64
NSA_TOPN = 16
NSA_WINDOW = 512
NSA_FORCE = 1e4
SWA_WINDOW = 128
EPS = 1e-6
NEG = -1e30
TINY = 1e-30
LANES = 128
SUBLANES = 8
CONV_HALO = 32

LOG2E = float(np.log2(np.e))
BIG = 2.0 ** 99
M_INIT = -1e38
POS_SPLIT_SHIFT = 8
SLOPE_PIECES = 4
SEL_FEAT0 = 2 * SLOPE_PIECES

F32 = jnp.float32
BF16 = jnp.bfloat16

ODD_QC, ODD_ZC, ODD_QD, ODD_ZD = 0, 4, 8, 12
ODD_KC, ODD_VC, ODD_KS, ODD_VS, ODD_KW, ODD_VW, ODD_KD, ODD_VD, ODD_GC = 16, 17, 18, 19, 20, 21, 22, 23, 24
ODD_COLS = 25 * LANES


def _alibi_slopes(n):
    return [float(2.0 ** (-8.0 * (i + 1) / n)) for i in range(n)]


SLOPES = _alibi_slopes(N_HEADS)


def _alibi_query_features(width):
    table = np.zeros((N_HEADS, 2 * SLOPE_PIECES, width), np.float32)
    for h, slope in enumerate(SLOPES):
        rest = np.float64(slope) * LOG2E
        for k in range(SLOPE_PIECES):
            piece = float(np.float32(rest).astype(ml_dtypes.bfloat16).astype(np.float32))
            table[h, k] = piece
            table[h, SLOPE_PIECES + k] = piece
            rest -= piece
    return table


def _dot(a, b):
    return jnp.dot(a, b, preferred_element_type=F32)


def _dot_exact(a, b):
    return jnp.dot(a, b, preferred_element_type=F32, precision=lax.Precision.HIGHEST)


def _sigmoid(x):
    return 1.0 / (1.0 + jnp.exp(-x))


def _silu(x):
    return x * _sigmoid(x)


def _rms(x, g):
    return x * lax.rsqrt(jnp.mean(x * x, axis=-1, keepdims=True) + EPS) * g


def _iota(shape, dim):
    return lax.broadcasted_iota(jnp.int32, shape, dim)


def _inproj_kernel(x_ref, g_ref, w_ref, o_ref, *copy_refs, chunk, copies):
    h = _rms(x_ref[...], g_ref[...]).astype(BF16)
    for c in range(o_ref.shape[1] // chunk):
        o_ref[:, c * chunk:(c + 1) * chunk] = _dot(h, w_ref[:, c * chunk:(c + 1) * chunk])
    for ref, (start, width) in zip(copy_refs, copies):
        ref[...] = o_ref[:, start:start + width]


def _inproj(x2d, g, w, *, tm=512, chunk, copies=()):
    m, d = x2d.shape
    e = w.shape[1]
    rows = lambda width: pl.BlockSpec((tm, width), lambda i: (i, 0))
    const = lambda shape: pl.BlockSpec(shape, lambda i: (0, 0))
    return pl.pallas_call(
        functools.partial(_inproj_kernel, chunk=chunk, copies=tuple(copies)),
        grid=(m // tm,),
        in_specs=[rows(d), const((1, d)), const((d, e))],
        out_specs=[rows(e)] + [rows(width) for _, width in copies],
        out_shape=[jax.ShapeDtypeStruct((m, e), F32)]
                  + [jax.ShapeDtypeStruct((m, width), F32) for _, width in copies],
        name="inproj",
    )(x2d, g.reshape(1, d), w)


def _outproj_kernel(x_ref, ya_ref, yb_ref, wa_ref, wb_ref, o_ref):
    o_ref[...] = x_ref[...] + _dot(ya_ref[...], wa_ref[...]) + _dot(yb_ref[...], wb_ref[...])


def _outproj(x2d, ya, yb, w, *, tm=512):
    m, d = x2d.shape
    gw = ya.shape[1]
    return pl.pallas_call(
        _outproj_kernel,
        grid=(m // tm,),
        in_specs=[pl.BlockSpec((tm, d), lambda i: (i, 0)),
                  pl.BlockSpec((tm, gw), lambda i: (i, 0)),
                  pl.BlockSpec((tm, gw), lambda i: (i, 0)),
                  pl.BlockSpec((gw, d), lambda i: (0, 0)),
                  pl.BlockSpec((gw, d), lambda i: (0, 0))],
        out_specs=pl.BlockSpec((tm, d), lambda i: (i, 0)),
        out_shape=jax.ShapeDtypeStruct((m, d), F32),
        name="outproj",
    )(x2d, ya, yb, w[:gw], w[gw:])


def _conv_kernel(uv_ref, ug_ref, uvh_ref, ugh_ref, z_ref, w_ref, b_ref, lg_ref, lb_ref, o_ref, h_s, *, rows):
    i = pl.program_id(1)
    ts = uv_ref.shape[1]
    halo = uvh_ref[0] * _sigmoid(ugh_ref[0])
    h_s[0:CONV_HALO, :] = jnp.where(i > 0, halo, 0.0)
    h_s[CONV_HALO:, :] = uv_ref[0] * _sigmoid(ug_ref[0])
    base = CONV_HALO - (CONV_WIDTH - 1)
    for c in range(ts // rows):
        acc = None
        for b in range(SUBLANES):
            n = rows if b == 0 else rows + SUBLANES
            part = None
            for a in range((base + CONV_WIDTH - 1) // SUBLANES + 1):
                j = SUBLANES * a + b - base
                if 0 <= j < CONV_WIDTH:
                    term = w_ref[j:j + 1, :] * h_s[c * rows + SUBLANES * a:c * rows + SUBLANES * a + n, :]
                    part = term if part is None else part + term
            part = part[b:b + rows]
            acc = part if acc is None else acc + part
        y = acc + b_ref[...]
        mu = jnp.mean(y, axis=-1, keepdims=True)
        yc = y - mu
        y = yc * lax.rsqrt(jnp.mean(yc * yc, axis=-1, keepdims=True) + EPS) * lg_ref[...] + lb_ref[...]
        out = _silu(y) * _silu(z_ref[0, c * rows:(c + 1) * rows, :])
        o_ref[0, c * rows:(c + 1) * rows, :] = out.astype(o_ref.dtype)


def _conv_module(proj, w, b, lg, lb, *, ts=256, rows=64):
    bsz, s, _ = proj.shape
    gw = GROUP_WIDTH
    hb = ts // CONV_HALO
    cur = lambda col: pl.BlockSpec((1, ts, gw), lambda bi, i: (bi, i, col))
    halo = lambda col: pl.BlockSpec((1, CONV_HALO, gw), lambda bi, i: (bi, jnp.maximum(i * hb - 1, 0), col))
    vec = pl.BlockSpec((1, gw), lambda bi, i: (0, 0))
    return pl.pallas_call(
        functools.partial(_conv_kernel, rows=rows),
        grid=(bsz, s // ts),
        in_specs=[cur(0), cur(1), halo(0), halo(1), cur(2),
                  pl.BlockSpec((CONV_WIDTH, gw), lambda bi, i: (0, 0)), vec, vec, vec],
        out_specs=pl.BlockSpec((1, ts, gw), lambda bi, i: (bi, i, 0)),
        out_shape=jax.ShapeDtypeStruct((bsz, s, gw), BF16),
        scratch_shapes=[pltpu.VMEM((CONV_HALO + ts, gw), F32)],
        name="conv_module",
    )(proj, proj, proj, proj, proj, w, b.reshape(1, gw), lg.reshape(1, gw), lb.reshape(1, gw))


def _key_features(pos, flipped, sel_shift=None):
    col = _iota(pos.shape, 1) - (0 if flipped else HEAD_DIM)
    hi = (pos >> POS_SPLIT_SHIFT) << POS_SPLIT_SHIFT
    lo = pos & ((1 << POS_SPLIT_SHIFT) - 1)
    feat = jnp.where(col < SLOPE_PIECES, hi, jnp.where(col < 2 * SLOPE_PIECES, lo, 0))
    if sel_shift is not None:
        feat = jnp.where(col - SEL_FEAT0 == (pos >> sel_shift), 1, feat)
    return jnp.where((col >= 0) & (col < HEAD_DIM), feat, 0).astype(F32)


def _pair_rms(x, g):
    sq = x * x
    hi = sq.astype(BF16)
    lo = (sq - hi.astype(F32)).astype(BF16)
    same_head = (_iota((LANES, LANES), 0) >> 6) == (_iota((LANES, LANES), 1) >> 6)
    ones = jnp.where(same_head, 1.0, 0.0).astype(BF16)
    ss = _dot(hi, ones) + _dot(lo, ones)
    return x * lax.rsqrt(ss * (1.0 / HEAD_DIM) + EPS) * jnp.concatenate([g, g], axis=1)


def _augment_pair_keys(kn, pos, sel_shift=None):
    lane = _iota(kn.shape, 1)
    return [jnp.where(lane >= HEAD_DIM if flipped else lane < HEAD_DIM, kn,
                      _key_features(pos, flipped, sel_shift)).astype(BF16) for flipped in (False, True)]


def _norm_query_t(x, g_col):
    ss = jnp.mean(x * x, axis=0, keepdims=True)
    return x * lax.rsqrt(ss + EPS) * g_col * (HEAD_DIM ** -0.5 * LOG2E)


def _augment_query(q_t, alibi, sel_bias=None, flipped=False):
    n = q_t.shape[1]
    feats = [alibi]
    used = alibi.shape[0]
    if sel_bias is not None:
        feats.append(sel_bias)
        used += sel_bias.shape[0]
    feats.append(jnp.zeros((HEAD_DIM - used, n), F32))
    return jnp.concatenate(feats + [q_t] if flipped else [q_t] + feats, axis=0).astype(BF16)


ONES_ROWS = 16
ACC_ROWS = HEAD_DIM + ONES_ROWS


def _flash_init(n):
    return jnp.full((1, n), M_INIT, F32), jnp.zeros((ACC_ROWS, n), F32)


def _flash_step(carry, sc, v_t):
    m, acc = carry
    m_new = jnp.maximum(m, jnp.max(sc, axis=0, keepdims=True))
    p = jnp.exp2(sc - m_new).astype(BF16)
    v_ones = jnp.concatenate([v_t, jnp.ones((ONES_ROWS, v_t.shape[1]), BF16)], axis=0)
    return m_new, jnp.exp2(m - m_new) * acc + _dot(v_ones, p)


def _flash_split(carry):
    m, acc = carry
    return m, acc[:HEAD_DIM], acc[HEAD_DIM:HEAD_DIM + 1]


def _flash_steps(carries, chunks, cap=None):
    return tuple(_flash_step(carry, sc if cap is None else jnp.minimum(sc, cap), v_t)
                 for carry, (sc, v_t) in zip(carries, chunks))


def _flash_reset(state):
    m_ref, acc_ref = state
    m_ref[...] = jnp.full(m_ref.shape, M_INIT, F32)
    acc_ref[...] = jnp.zeros(acc_ref.shape, F32)


def _flash_update(state, slot, sc, v_t, cap=None):
    m_ref, acc_ref = state
    m_ref[slot], acc_ref[slot] = _flash_step((m_ref[slot], acc_ref[slot]),
                                             sc if cap is None else jnp.minimum(sc, cap), v_t)


def _flash_update_all(state, slots, chunks, cap=None):
    for slot, (sc, v_t) in zip(slots, chunks):
        _flash_update(state, slot, sc, v_t, cap)


def _flash_update_pipelined(state, bufs, slots, c, next_scores, values, cap=None):
    def run(src_ref, dst_ref):
        for slot in slots:
            fresh = None if next_scores is None else next_scores(slot)
            _flash_update(state, slot, src_ref[slot], values(slot), cap)
            if fresh is not None:
                dst_ref[slot] = fresh

    pl.when((c & 1) == 0)(lambda: run(bufs[0], bufs[1]))
    pl.when((c & 1) == 1)(lambda: run(bufs[1], bufs[0]))


def _flash_result(state, slot):
    m_ref, acc_ref = state
    m, out, den = _flash_split((m_ref[slot], acc_ref[slot]))
    return out / jnp.maximum(den, TINY)


def _flash_scratch(chains, n):
    return [pltpu.VMEM((chains, 1, n), F32), pltpu.VMEM((chains, ACC_ROWS, n), F32)]


def _rank_select(score, blk, limit, count):
    rank = jnp.zeros(score.shape, jnp.int32)
    for mm in range(limit):
        gm = score[mm:mm + 1, :]
        beats = (gm > score) | ((gm == score) & (mm < blk))
        rank = rank + jnp.where(beats, 1, 0)
    return rank < count


def _tile_lanes(x, reps):
    return jnp.concatenate([x] * reps, axis=1) if reps > 1 else x


def _causal_cap(tk, tq, reps):
    keep = _iota((tk, tq), 0) <= _iota((tk, tq), 1)
    return _tile_lanes(jnp.where(keep, BIG, -BIG), reps)


def _window_edge_cap(tk, tq, reps):
    keep = _iota((tk, tq), 0) > _iota((tk, tq), 1)
    return _tile_lanes(jnp.where(keep, BIG, -BIG), reps)


def _store_chunked_t(x, dst_ref, tk):
    x_t = x.T.astype(BF16)
    for c in range(x.shape[0] // tk):
        dst_ref[c] = x_t[:, c * tk:(c + 1) * tk]


def _moba_kernel(alibi_ref, q_ref, k_ref, v_ref, z_ref, gq_ref, gk_ref, o_ref, ka_s, vt_s, km_s,
                 m_s, acc_s, sc0_s, sc1_s):
    i = pl.program_id(2)
    s = k_ref.shape[1]
    nblk = s // MOBA_BLOCK
    tq = MOBA_BLOCK
    hd = HEAD_DIM
    heads = range(ka_s.shape[0])

    @pl.when(i == 0)
    def _():
        pos = _iota((s, LANES), 0)
        for pair in range(len(heads) // 2):
            kn = _pair_rms(k_ref[0, :, pair * LANES:(pair + 1) * LANES], gk_ref[...])
            ka_s[2 * pair], ka_s[2 * pair + 1] = _augment_pair_keys(kn, pos, MOBA_BLOCK.bit_length() - 1)
            km = jnp.mean(kn.reshape(nblk, MOBA_BLOCK, LANES), axis=1)
            km_s[2 * pair], km_s[2 * pair + 1] = km[:, :hd], km[:, hd:]
        _store_chunked_t(v_ref[0], vt_s, tq)

    q_t = q_ref[0].T
    blk = _iota((nblk, tq), 0)
    qas = []
    for j in heads:
        qn = _norm_query_t(q_t[j * hd:(j + 1) * hd], gq_ref[...])
        gate = jnp.where(blk < i, _dot_exact(km_s[j], qn), NEG)
        chosen = _rank_select(gate, blk, nblk, MOBA_TOPK) & (blk < i)
        sel_bias = jnp.where(chosen | (blk == i), 0.0, -BIG)
        qas.append(_augment_query(qn, alibi_ref[j], sel_bias, flipped=j % 2 == 1))

    def scores(j, n):
        return _dot(ka_s[j, pl.ds(pl.multiple_of(n * tq, tq), tq), :], qas[j])

    def values(j, n):
        return vt_s[n, j * hd:(j + 1) * hd, :]

    state = (m_s, acc_s)
    _flash_reset(state)
    bufs = (sc0_s, sc1_s)
    for j in heads:
        sc0_s[j] = scores(j, 0)

    def body(n, carry):
        _flash_update_pipelined(state, bufs, heads, n, lambda j: scores(j, n + 1), lambda j: values(j, n))
        return carry

    lax.fori_loop(0, i, body, 0)
    diag_cap = _causal_cap(tq, tq, 1)
    _flash_update_pipelined(state, bufs, heads, i, None, lambda j: values(j, i), diag_cap)
    outs = [_flash_result(state, j) for j in heads]
    y = jnp.concatenate(outs, axis=0).T * _silu(z_ref[0])
    o_ref[0] = y.astype(o_ref.dtype)


def _moba(proj, gq, gk, *, heads_per_step=8):
    bsz, s, _ = proj.shape
    tq = MOBA_BLOCK
    hd = HEAD_DIM
    nblk = s // MOBA_BLOCK
    hps = heads_per_step
    wd = hps * hd
    assert SEL_FEAT0 + nblk <= hd and wd % LANES == 0
    q0, k0, v0, z0 = (c * GROUP_WIDTH // wd for c in (3, 4, 5, 6))
    tile = lambda c0: pl.BlockSpec((1, tq, wd), lambda b, p, i: (b, i, c0 + p))
    full = lambda c0: pl.BlockSpec((1, s, wd), lambda b, p, i: (b, 0, c0 + p))
    return pl.pallas_call(
        _moba_kernel,
        grid=(bsz, N_HEADS // hps, s // tq),
        in_specs=[pl.BlockSpec((hps, 2 * SLOPE_PIECES, tq), lambda b, p, i: (p, 0, 0)),
                  tile(q0), full(k0), full(v0), tile(z0),
                  pl.BlockSpec((hd, 1), lambda b, p, i: (0, 0)), pl.BlockSpec((1, hd), lambda b, p, i: (0, 0))],
        out_specs=pl.BlockSpec((1, tq, wd), lambda b, p, i: (b, i, p)),
        out_shape=jax.ShapeDtypeStruct((bsz, s, GROUP_WIDTH), BF16),
        scratch_shapes=[pltpu.VMEM((hps, s, LANES), BF16), pltpu.VMEM((nblk, wd, tq), BF16),
                        pltpu.VMEM((hps, nblk, hd), F32)] + _flash_scratch(hps, tq)
                       + [pltpu.VMEM((hps, tq, tq), F32)] * 2,
        name="moba",
    )(jnp.asarray(_alibi_query_features(tq)), proj, proj, proj, proj, gq.reshape(hd, 1), gk.reshape(1, hd))


def _cmp_kernel(xk_ref, xv_ref, wk1_ref, wv1_ref, pk_ref, pv_ref, bk1_ref, bv1_ref,
                wk2_ref, wv2_ref, bk2_ref, bv2_ref, gn_ref, ko_ref, vo_ref, wke_s, wve_s):
    half = xk_ref.shape[2]
    n = xk_ref.shape[1]
    hd = HEAD_DIM
    nh = NSA_CMP_HIDDEN

    @pl.when(pl.program_id(0) == 0)
    def _():
        for w_ref, we_ref in ((wk1_ref, wke_s), (wv1_ref, wve_s)):
            we_ref[...] = jnp.zeros(we_ref.shape, BF16)
            for tok in range(NSA_CMP_LEN):
                for g in range(KV_HEADS):
                    row = (tok * KV_HEADS + g) * hd
                    we_ref[row:row + hd, g * nh:(g + 1) * nh] = w_ref[tok * hd:(tok + 1) * hd, :]

    def hidden(x_ref, we_ref, pos_ref, b1_ref):
        x = x_ref[0].astype(BF16)
        first = _dot(x, we_ref[:half, :])
        second = _dot(x, we_ref[half:, :])
        pos = jnp.broadcast_to(pos_ref[...], (8, 2 * half)).astype(BF16)
        ph = _dot(pos[:, :half], we_ref[:half, :]) + _dot(pos[:, half:], we_ref[half:, :])
        return _silu(first + pltpu.roll(second, n - 1, 0) + ph[0:1, :] + b1_ref[...])

    hk = hidden(xk_ref, wke_s, pk_ref, bk1_ref).astype(BF16)
    hv = hidden(xv_ref, wve_s, pv_ref, bv1_ref).astype(BF16)
    nh = NSA_CMP_HIDDEN
    ks, vs = [], []
    for g in range(KV_HEADS):
        ks.append(_rms(_dot(hk[:, g * nh:(g + 1) * nh], wk2_ref[...]) + bk2_ref[...], gn_ref[...]))
        vs.append(_dot(hv[:, g * nh:(g + 1) * nh], wv2_ref[...]) + bv2_ref[...])
    ko_ref[0] = jnp.concatenate(ks, axis=1)
    vo_ref[0] = jnp.concatenate(vs, axis=1)


def _compress(xk, xv, pos_k, pos_v, kw1, kb1, kw2, kb2, vw1, vb1, vw2, vb2, gn):
    bsz, n, half = xk.shape
    hd = HEAD_DIM
    tile_pos = lambda p: jnp.tile(p[:, None, :], (1, KV_HEADS, 1)).reshape(1, 2 * half)
    tile_b = lambda b: jnp.tile(b.reshape(1, -1), (1, KV_HEADS))
    const = lambda shape: pl.BlockSpec(shape, lambda b: (0,) * len(shape))
    xs = pl.BlockSpec((1, n, half), lambda b: (b, 0, 0))
    hw = KV_HEADS * NSA_CMP_HIDDEN
    out = pl.BlockSpec((1, n, KV_HEADS * hd), lambda b: (b, 0, 0))
    expanded = pltpu.VMEM((2 * half, hw), BF16)
    return pl.pallas_call(
        _cmp_kernel,
        grid=(bsz,),
        in_specs=[xs, xs, const(kw1.shape), const(vw1.shape),
                  const((1, 2 * half)), const((1, 2 * half)), const((1, hw)), const((1, hw)),
                  const((NSA_CMP_HIDDEN, hd)), const((NSA_CMP_HIDDEN, hd)), const((1, hd)), const((1, hd)),
                  const((1, hd))],
        out_specs=[out, out],
        out_shape=[jax.ShapeDtypeStruct((bsz, n, KV_HEADS * hd), F32)] * 2,
        scratch_shapes=[expanded, expanded],
        name="nsa_compress",
    )(xk, xv, kw1.astype(BF16), vw1.astype(BF16), tile_pos(pos_k), tile_pos(pos_v), tile_b(kb1), tile_b(vb1),
      kw2.astype(BF16), vw2.astype(BF16), kb2.reshape(1, hd), vb2.reshape(1, hd), gn.reshape(1, hd))


def _group_query(q_t, alibi_ref, g, gq_col):
    hd = HEAD_DIM
    heads = range(Q_PER_KV * g, Q_PER_KV * (g + 1))
    qn = jnp.concatenate([_norm_query_t(q_t[h * hd:(h + 1) * hd], gq_col) for h in heads], axis=1)
    alibi = jnp.concatenate([alibi_ref[h] for h in heads], axis=1)
    return qn, alibi


def _augment_keys(k_ref, gk_ref, dst_ref, sel_shift=None):
    assert KV_HEADS == 2
    pos = _iota(k_ref.shape[1:], 0)
    dst_ref[0], dst_ref[1] = _augment_pair_keys(_pair_rms(k_ref[0], gk_ref[...]), pos, sel_shift)


def _chunk_scores(ka_ref, vt_ref, g, c, qa, tq):
    off = pl.multiple_of(c * tq, tq)
    return _dot(ka_ref[g, pl.ds(off, tq), :], qa), vt_ref[c, g * HEAD_DIM:(g + 1) * HEAD_DIM, :]


def _window_first_chunk(ka_ref, vt_ref, g, far, qa, tq, edge_cap):
    sc, v_t = _chunk_scores(ka_ref, vt_ref, g, jnp.maximum(far, 0), qa, tq)
    return jnp.minimum(sc, jnp.minimum(edge_cap, jnp.where(far >= 0, BIG, -BIG))), v_t


def _regroup_heads(o_t, tq):
    return jnp.concatenate([o_t[:, k * tq:(k + 1) * tq] for k in range(Q_PER_KV)], axis=0)


def _nsa_kernel(alibi_ref, q_ref, kc_ref, vc_ref, ks_ref, vs_ref, kw_ref, vw_ref, gate_ref, z_ref,
                gq_ref, gks_ref, gkw_ref, o_ref, kca_s, vct_s, ksa_s, vst_s, kwa_s, vwt_s, m_s, acc_s,
                sc0_s, sc1_s):
    i = pl.program_id(1)
    tq = q_ref.shape[1]
    s = ks_ref.shape[1]
    hd = HEAD_DIM
    ncmp = kc_ref.shape[1]
    nsb = s // NSA_SEL_BLOCK
    n4 = Q_PER_KV * tq

    @pl.when(i == 0)
    def _():
        _augment_keys(ks_ref, gks_ref, ksa_s, NSA_SEL_BLOCK.bit_length() - 1)
        _augment_keys(kw_ref, gkw_ref, kwa_s)
        _store_chunked_t(vs_ref[0], vst_s, tq)
        _store_chunked_t(vw_ref[0], vwt_s, tq)
        cend = _iota((ncmp, LANES), 0) * NSA_CMP_STRIDE + (NSA_CMP_LEN - 1)
        kca_s[0], kca_s[1] = _augment_pair_keys(kc_ref[0], cend)
        vct_s[...] = vc_ref[0].T.astype(BF16)

    t0 = i * tq
    q_t = q_ref[0].T
    gates_t = _sigmoid(gate_ref[0]).T
    diag_cap = _causal_cap(tq, tq, Q_PER_KV)
    edge_cap = _window_edge_cap(tq, tq, Q_PER_KV)
    cend =_iota((ncmp, tq), 0) * NSA_CMP_STRIDE + (NSA_CMP_LEN - 1)
    seen = _tile_lanes(cend <= t0 + _iota((ncmp, tq), 1), Q_PER_KV)
    cstart = _iota((nsb, ncmp), 1) * NSA_CMP_STRIDE
    bstart = _iota((nsb, ncmp), 0) * NSA_SEL_BLOCK
    overlap_t = jnp.where((cstart < bstart + NSA_SEL_BLOCK) & (cstart + NSA_CMP_LEN > bstart), 1.0, 0.0)
    blk = _iota((nsb, tq), 0)
    cur = (t0 + _iota((nsb, tq), 1)) >> (NSA_SEL_BLOCK.bit_length() - 1)
    forced = (blk == 0) | (blk == cur) | (blk == cur - 1)
    groups = range(KV_HEADS)
    far = i - NSA_WINDOW // tq
    lo = jnp.maximum(far + 1, 0)

    state = (m_s, acc_s)
    sel_slots = tuple(groups)
    win_slots = tuple(KV_HEADS + g for g in groups)
    _flash_reset(state)
    queries = [_group_query(q_t, alibi_ref, g, gq_ref[...]) for g in groups]
    qa_win = [_augment_query(qn, alibi, flipped=g == 1) for g, (qn, alibi) in enumerate(queries)]
    qa_sel = []

    def scores(slot, c):
        g = slot % KV_HEADS
        if slot in sel_slots:
            return _chunk_scores(ksa_s, vst_s, g, c, qa_sel[g], tq)[0]
        return _chunk_scores(kwa_s, vwt_s, g, c, qa_win[g], tq)[0]

    def values(slot, c):
        g = slot % KV_HEADS
        return (vst_s if slot in sel_slots else vwt_s)[c, g * hd:(g + 1) * hd, :]

    first = [_window_first_chunk(kwa_s, vwt_s, g, far, qa_win[g], tq, edge_cap) for g in groups]
    cmp_sc = [_dot(kca_s[g], qa_win[g]) for g in groups]
    sc_win_lo = [scores(slot, lo) for slot in win_slots]
    _flash_update_all(state, win_slots, first)
    o_cmp = []
    for g in groups:
        qn, alibi = queries[g]
        sc = jnp.where(seen, cmp_sc[g], -BIG)
        e = jnp.where(seen, jnp.exp2(sc - jnp.max(sc, axis=0, keepdims=True)), 0.0)
        p_c = e / jnp.maximum(jnp.sum(e, axis=0, keepdims=True), TINY)
        o_cmp.append(_dot(vct_s[g * hd:(g + 1) * hd, :], p_c.astype(BF16)))

        psum = p_c[:, 0:tq]
        for k in range(1, Q_PER_KV):
            psum = psum + p_c[:, k * tq:(k + 1) * tq]
        imp = _dot_exact(overlap_t, psum)
        imp = jnp.where(blk <= cur, jnp.where(forced, NSA_FORCE, imp), NEG)
        chosen = _rank_select(imp, blk, nsb, min(NSA_TOPN, nsb))
        sel_bias = _tile_lanes(jnp.where(chosen, 0.0, -BIG), Q_PER_KV)
        qa_sel.append(_augment_query(qn, alibi, sel_bias, flipped=g == 1))

    bufs = (sc0_s, sc1_s)
    for slot in sel_slots:
        sc0_s[slot] = scores(slot, 0)
    for slot, sc in zip(win_slots, sc_win_lo):
        sc0_s[slot] = sc
        sc1_s[slot] = sc

    def body(slots, c, carry):
        _flash_update_pipelined(state, bufs, slots, c, lambda slot: scores(slot, c + 1),
                                lambda slot: values(slot, c))
        return carry

    both_slots = sel_slots + win_slots
    lax.fori_loop(0, lo, functools.partial(body, sel_slots), 0)
    lax.fori_loop(lo, i, functools.partial(body, both_slots), 0)
    _flash_update_pipelined(state, bufs, both_slots, i, None, lambda slot: values(slot, i), diag_cap)

    outs = []
    for g in groups:
        def gate_row(branch):
            rows = [gates_t[branch * N_HEADS + h:branch * N_HEADS + h + 1, :]
                    for h in range(Q_PER_KV * g, Q_PER_KV * (g + 1))]
            return jnp.concatenate(rows, axis=1)

        o_slc = _flash_result(state, sel_slots[g])
        o_win = _flash_result(state, win_slots[g])
        o = gate_row(0) * o_cmp[g] + gate_row(1) * o_slc + gate_row(2) * o_win
        outs.append(_regroup_heads(o, tq))
    y = jnp.concatenate(outs, axis=0).T * _silu(z_ref[0])
    o_ref[0] = y.astype(o_ref.dtype)


def _nsa(proj, kcn, vc, gq, gks, gkw, *, tq=256):
    bsz, s, _ = proj.shape
    hd = HEAD_DIM
    gw = GROUP_WIDTH
    ncmp = kcn.shape[1]
    assert SEL_FEAT0 + s // NSA_SEL_BLOCK <= hd and ncmp == LANES
    wide = lambda c0: pl.BlockSpec((1, tq, gw), lambda b, i: (b, i, c0 // 4))
    full = lambda c0: pl.BlockSpec((1, s, LANES), lambda b, i: (b, 0, c0))
    cmp = pl.BlockSpec((1, ncmp, LANES), lambda b, i: (b, 0, 0))
    row = pl.BlockSpec((1, hd), lambda b, i: (0, 0))
    ka = pltpu.VMEM((KV_HEADS, s, LANES), BF16)
    vt = pltpu.VMEM((s // tq, LANES, tq), BF16)
    return pl.pallas_call(
        _nsa_kernel,
        grid=(bsz, s // tq),
        in_specs=[pl.BlockSpec((N_HEADS, 2 * SLOPE_PIECES, tq), lambda b, i: (0, 0, 0)),
                  wide(ODD_QC), cmp, cmp, full(ODD_KS), full(ODD_VS), full(ODD_KW), full(ODD_VW),
                  pl.BlockSpec((1, tq, LANES), lambda b, i: (b, i, ODD_GC)), wide(ODD_ZC),
                  pl.BlockSpec((hd, 1), lambda b, i: (0, 0)), row, row],
        out_specs=pl.BlockSpec((1, tq, gw), lambda b, i: (b, i, 0)),
        out_shape=jax.ShapeDtypeStruct((bsz, s, gw), BF16),
        scratch_shapes=[pltpu.VMEM((KV_HEADS, ncmp, LANES), BF16), pltpu.VMEM((LANES, ncmp), BF16),
                        ka, vt, ka, vt] + _flash_scratch(2 * KV_HEADS, Q_PER_KV * tq)
                       + [pltpu.VMEM((2 * KV_HEADS, tq, Q_PER_KV * tq), F32)] * 2,
        name="nsa",
    )(jnp.asarray(_alibi_query_features(tq)), proj, kcn, vc, proj, proj, proj, proj, proj, proj,
      gq.reshape(hd, 1), gks.reshape(1, hd), gkw.reshape(1, hd))


def _swa_kernel(sinks_ref, alibi_ref, q_ref, k_ref, v_ref, z_ref, gq_ref, gk_ref, o_ref, ka_s, vt_s):
    step = pl.program_id(1)
    tq = SWA_WINDOW
    tiles = q_ref.shape[1] // tq
    s = k_ref.shape[1]

    @pl.when(step == 0)
    def _():
        _augment_keys(k_ref, gk_ref, ka_s)
        _store_chunked_t(v_ref[0], vt_s, tq)

    q_t = q_ref[0].T
    diag_cap = _causal_cap(tq, tq, Q_PER_KV)
    edge_cap = _window_edge_cap(tq, tq, Q_PER_KV)
    chains = [(u, g) for u in range(tiles) for g in range(KV_HEADS)]
    qas = [_augment_query(*_group_query(q_t[:, u * tq:(u + 1) * tq], alibi_ref, g, gq_ref[...]), flipped=g == 1)
           for u, g in chains]
    first = [_window_first_chunk(ka_s, vt_s, g, step * tiles + u - 1, qa, tq, edge_cap)
             for (u, g), qa in zip(chains, qas)]
    own = [_chunk_scores(ka_s, vt_s, g, step * tiles + u, qa, tq) for (u, g), qa in zip(chains, qas)]
    carries = _flash_steps([_flash_init(Q_PER_KV * tq) for _ in chains], first)
    carries = _flash_steps(carries, own, diag_cap)
    outs = []
    for (u, g), carry in zip(chains, carries):
        m, acc, l = _flash_split(carry)
        tpos = ((step * tiles + u) * tq + _iota((1, tq), 1)).astype(F32)
        sink = jnp.concatenate([sinks_ref[h] * LOG2E + (SLOPES[h] * LOG2E) * tpos
                                for h in range(Q_PER_KV * g, Q_PER_KV * (g + 1))], axis=1)
        mf = jnp.maximum(m, sink)
        alpha = jnp.exp2(m - mf)
        o = acc * alpha / jnp.maximum(l * alpha + jnp.exp2(sink - mf), TINY)
        outs.append(_regroup_heads(o, tq))
    o_t = jnp.concatenate([jnp.concatenate(outs[u * KV_HEADS:(u + 1) * KV_HEADS], axis=0) for u in range(tiles)],
                          axis=1)
    y = o_t.T * _silu(z_ref[0])
    o_ref[0] = y.astype(o_ref.dtype)


def _swa(proj, sinks, gq, gk, *, tiles_per_step=4):
    bsz, s, _ = proj.shape
    hd = HEAD_DIM
    gw = GROUP_WIDTH
    tq = SWA_WINDOW
    rows = tiles_per_step * tq
    wide = lambda c0: pl.BlockSpec((1, rows, gw), lambda b, i: (b, i, c0 // 4))
    full = lambda c0: pl.BlockSpec((1, s, LANES), lambda b, i: (b, 0, c0))
    return pl.pallas_call(
        _swa_kernel,
        grid=(bsz, s // rows),
        in_specs=[pl.BlockSpec(memory_space=pltpu.SMEM),
                  pl.BlockSpec((N_HEADS, 2 * SLOPE_PIECES, tq), lambda b, i: (0, 0, 0)),
                  wide(ODD_QD), full(ODD_KD), full(ODD_VD), wide(ODD_ZD),
                  pl.BlockSpec((hd, 1), lambda b, i: (0, 0)), pl.BlockSpec((1, hd), lambda b, i: (0, 0))],
        out_specs=pl.BlockSpec((1, rows, gw), lambda b, i: (b, i, 0)),
        out_shape=jax.ShapeDtypeStruct((bsz, s, gw), BF16),
        scratch_shapes=[pltpu.VMEM((KV_HEADS, s, LANES), BF16), pltpu.VMEM((s // tq, LANES, tq), BF16)],
        name="swa",
    )(sinks.astype(F32), jnp.asarray(_alibi_query_features(tq)), proj, proj, proj, proj,
      gq.reshape(hd, 1), gk.reshape(1, hd))


def _pack_odd_weight(w):
    gw, kw, ng = GROUP_WIDTH, KV_HEADS * HEAD_DIM, 3 * N_HEADS
    sizes = [gw, kw, kw, kw, kw, kw, kw, ng, gw, gw, kw, kw, gw]
    starts = np.concatenate([[0], np.cumsum(sizes)]).tolist()
    part = lambda k: w[:, starts[k]:starts[k + 1]]
    pad = jnp.zeros((w.shape[0], LANES - ng), w.dtype)
    order = [0, 8, 9, 12, 1, 2, 3, 4, 5, 6, 10, 11, 7]
    return jnp.concatenate([part(k) for k in order] + [pad], axis=1).astype(BF16)


def kernel(x, norm_g, w_out, e_w_in, a_conv_w, a_conv_b, a_ln_g, a_ln_b, b_qnorm_g, b_knorm_g, o_w_in, c_qnorm_g, c_knorm_cmp_g, c_knorm_slc_g, c_knorm_win_g, c_pos_k, c_pos_v, c_k_w1, c_k_b1, c_k_w2, c_k_b2, c_v_w1, c_v_b1, c_v_w2, c_v_b2, d_qnorm_g, d_knorm_g, d_sinks):
    bsz, s, d = x.shape
    m = bsz * s
    assert s % MOBA_BLOCK == 0 and d == 2 * GROUP_WIDTH
    x2 = x.reshape(m, d)

    (proj,) = _inproj(x2, norm_g[0], e_w_in[0].astype(BF16), chunk=GROUP_WIDTH)
    proj = proj.reshape(bsz, s, -1)
    y_a = _conv_module(proj, a_conv_w[0], a_conv_b[0], a_ln_g[0], a_ln_b[0])
    y_b = _moba(proj, b_qnorm_g[0], b_knorm_g[0])
    x2 = _outproj(x2, y_a.reshape(m, -1), y_b.reshape(m, -1), w_out[0].astype(BF16))

    proj, k_cmp, v_cmp = _inproj(x2, norm_g[1], _pack_odd_weight(o_w_in[0]), chunk=5 * LANES,
                                 copies=((ODD_KC * LANES, LANES), (ODD_VC * LANES, LANES)))
    proj = proj.reshape(bsz, s, ODD_COLS)
    rows16 = lambda t: t.reshape(bsz, s // NSA_CMP_STRIDE, NSA_CMP_STRIDE * LANES)
    kcn, vc = _compress(rows16(k_cmp), rows16(v_cmp), c_pos_k[0], c_pos_v[0], c_k_w1[0], c_k_b1[0],
                        c_k_w2[0], c_k_b2[0], c_v_w1[0], c_v_b1[0], c_v_w2[0], c_v_b2[0], c_knorm_cmp_g[0])
    y_c = _nsa(proj, kcn, vc, c_qnorm_g[0], c_knorm_slc_g[0], c_knorm_win_g[0])
    y_d = _swa(proj, d_sinks[0], d_qnorm_g[0], d_knorm_g[0])
    x2 = _outproj(x2, y_c.reshape(m, -1), y_d.reshape(m, -1), w_out[1].astype(BF16))
    return x2.reshape(bsz, s, d)
```

```python
import functools

import ml_dtypes
import numpy as np
import jax
import jax.numpy as jnp
from jax import lax
from jax.experimental import pallas as pl
from jax.experimental.pallas import tpu as pltpu

HEAD_DIM = 64
N_HEADS = 8
GROUP_WIDTH = N_HEADS * HEAD_DIM
CONV_WIDTH = 31
MOBA_BLOCK = 256
MOBA_TOPK = 3
KV_HEADS = 2
Q_PER_KV = N_HEADS // KV_HEADS
NSA_CMP_LEN = 32
NSA_CMP_STRIDE = 16
NSA_CMP_HIDDEN = 256
NSA_SEL_BLOCK = 64
NSA_TOPN = 16
NSA_WINDOW = 512
NSA_FORCE = 1e4
SWA_WINDOW = 128
EPS = 1e-6
NEG = -1e30
TINY = 1e-30
LANES = 128
SUBLANES = 8
CONV_HALO = 32

LOG2E = float(np.log2(np.e))
BIG = 2.0 ** 99
M_INIT = -1e38
POS_SPLIT_SHIFT = 8
SLOPE_PIECES = 4
SEL_FEAT0 = 2 * SLOPE_PIECES

F32 = jnp.float32
BF16 = jnp.bfloat16

ODD_QC, ODD_ZC, ODD_QD, ODD_ZD = 0, 4, 8, 12
ODD_KC, ODD_VC, ODD_KS, ODD_VS, ODD_KW, ODD_VW, ODD_KD, ODD_VD, ODD_GC = 16, 17, 18, 19, 20, 21, 22, 23, 24
ODD_COLS = 25 * LANES


def _alibi_slopes(n):
    return [float(2.0 ** (-8.0 * (i + 1) / n)) for i in range(n)]


SLOPES = _alibi_slopes(N_HEADS)


def _alibi_query_features(width):
    table = np.zeros((N_HEADS, 2 * SLOPE_PIECES, width), np.float32)
    for h, slope in enumerate(SLOPES):
        rest = np.float64(slope) * LOG2E
        for k in range(SLOPE_PIECES):
            piece = float(np.float32(rest).astype(ml_dtypes.bfloat16).astype(np.float32))
            table[h, k] = piece
            table[h, SLOPE_PIECES + k] = piece
            rest -= piece
    return table


def _dot(a, b):
    return jnp.dot(a, b, preferred_element_type=F32)


def _dot_exact(a, b):
    return jnp.dot(a, b, preferred_element_type=F32, precision=lax.Precision.HIGHEST)


def _sigmoid(x):
    return 1.0 / (1.0 + jnp.exp(-x))


def _silu(x):
    return x * _sigmoid(x)


def _rms(x, g):
    return x * lax.rsqrt(jnp.mean(x * x, axis=-1, keepdims=True) + EPS) * g


def _iota(shape, dim):
    return lax.broadcasted_iota(jnp.int32, shape, dim)


def _inproj_kernel(x_ref, g_ref, w_ref, o_ref, *rest, chunk, copies, group):
    copy_refs, stage_refs = rest[:len(copies)], rest[len(copies):]
    h = _rms(x_ref[...], g_ref[...]).astype(BF16)
    for c in range(o_ref.shape[1] // chunk):
        o_ref[:, c * chunk:(c + 1) * chunk] = _dot(h, w_ref[:, c * chunk:(c + 1) * chunk])
    for ref, stage, (start, width) in zip(copy_refs, stage_refs, copies):
        stage[...] = o_ref[:, start:start + width]
        for t in range(group):
            ref[:, t * width:(t + 1) * width] = stage[pl.ds(t, ref.shape[0], stride=group), :]


def _inproj(x2d, g, w, *, tm=512, chunk, copies=(), group=1):
    m, d = x2d.shape
    e = w.shape[1]
    rows = lambda n, width: pl.BlockSpec((n, width), lambda i: (i, 0))
    const = lambda shape: pl.BlockSpec(shape, lambda i: (0, 0))
    return pl.pallas_call(
        functools.partial(_inproj_kernel, chunk=chunk, copies=tuple(copies), group=group),
        grid=(m // tm,),
        in_specs=[rows(tm, d), const((1, d)), const((d, e))],
        out_specs=[rows(tm, e)] + [rows(tm // group, group * width) for _, width in copies],
        out_shape=[jax.ShapeDtypeStruct((m, e), F32)]
                  + [jax.ShapeDtypeStruct((m // group, group * width), F32) for _, width in copies],
        scratch_shapes=[pltpu.VMEM((tm, width), F32) for _, width in copies],
        name="inproj",
    )(x2d, g.reshape(1, d), w)


def _outproj_kernel(x_ref, ya_ref, yb_ref, wa_ref, wb_ref, o_ref):
    o_ref[...] = x_ref[...] + _dot(ya_ref[...], wa_ref[...]) + _dot(yb_ref[...], wb_ref[...])


def _outproj(x2d, ya, yb, w, *, tm=512):
    m, d = x2d.shape
    gw = ya.shape[1]
    return pl.pallas_call(
        _outproj_kernel,
        grid=(m // tm,),
        in_specs=[pl.BlockSpec((tm, d), lambda i: (i, 0)),
                  pl.BlockSpec((tm, gw), lambda i: (i, 0)),
                  pl.BlockSpec((tm, gw), lambda i: (i, 0)),
                  pl.BlockSpec((gw, d), lambda i: (0, 0)),
                  pl.BlockSpec((gw, d), lambda i: (0, 0))],
        out_specs=pl.BlockSpec((tm, d), lambda i: (i, 0)),
        out_shape=jax.ShapeDtypeStruct((m, d), F32),
        name="outproj",
    )(x2d, ya, yb, w[:gw], w[gw:])


def _conv_kernel(uv_ref, ug_ref, uvh_ref, ugh_ref, z_ref, w_ref, b_ref, lg_ref, lb_ref, o_ref, h_s, *, rows):
    i = pl.program_id(1)
    ts = uv_ref.shape[1]
    halo = uvh_ref[0] * _sigmoid(ugh_ref[0])
    h_s[0:CONV_HALO, :] = jnp.where(i > 0, halo, 0.0)
    h_s[CONV_HALO:, :] = uv_ref[0] * _sigmoid(ug_ref[0])
    base = CONV_HALO - (CONV_WIDTH - 1)
    for c in range(ts // rows):
        acc = None
        for b in range(SUBLANES):
            n = rows if b == 0 else rows + SUBLANES
            part = None
            for a in range((base + CONV_WIDTH - 1) // SUBLANES + 1):
                j = SUBLANES * a + b - base
                if 0 <= j < CONV_WIDTH:
                    term = w_ref[j:j + 1, :] * h_s[c * rows + SUBLANES * a:c * rows + SUBLANES * a + n, :]
                    part = term if part is None else part + term
            part = part[b:b + rows]
            acc = part if acc is None else acc + part
        y = acc + b_ref[...]
        mu = jnp.mean(y, axis=-1, keepdims=True)
        yc = y - mu
        y = yc * lax.rsqrt(jnp.mean(yc * yc, axis=-1, keepdims=True) + EPS) * lg_ref[...] + lb_ref[...]
        out = _silu(y) * _silu(z_ref[0, c * rows:(c + 1) * rows, :])
        o_ref[0, c * rows:(c + 1) * rows, :] = out.astype(o_ref.dtype)


def _conv_module(proj, w, b, lg, lb, *, ts=256, rows=64):
    bsz, s, _ = proj.shape
    gw = GROUP_WIDTH
    hb = ts // CONV_HALO
    cur = lambda col: pl.BlockSpec((1, ts, gw), lambda bi, i: (bi, i, col))
    halo = lambda col: pl.BlockSpec((1, CONV_HALO, gw), lambda bi, i: (bi, jnp.maximum(i * hb - 1, 0), col))
    vec = pl.BlockSpec((1, gw), lambda bi, i: (0, 0))
    return pl.pallas_call(
        functools.partial(_conv_kernel, rows=rows),
        grid=(bsz, s // ts),
        in_specs=[cur(0), cur(1), halo(0), halo(1), cur(2),
                  pl.BlockSpec((CONV_WIDTH, gw), lambda bi, i: (0, 0)), vec, vec, vec],
        out_specs=pl.BlockSpec((1, ts, gw), lambda bi, i: (bi, i, 0)),
        out_shape=jax.ShapeDtypeStruct((bsz, s, gw), BF16),
        scratch_shapes=[pltpu.VMEM((CONV_HALO + ts, gw), F32)],
        name="conv_module",
    )(proj, proj, proj, proj, proj, w, b.reshape(1, gw), lg.reshape(1, gw), lb.reshape(1, gw))


def _key_features(pos, flipped, sel_shift=None):
    col = _iota(pos.shape, 1) - (0 if flipped else HEAD_DIM)
    hi = (pos >> POS_SPLIT_SHIFT) << POS_SPLIT_SHIFT
    lo = pos & ((1 << POS_SPLIT_SHIFT) - 1)
    feat = jnp.where(col < SLOPE_PIECES, hi, jnp.where(col < 2 * SLOPE_PIECES, lo, 0))
    if sel_shift is not None:
        feat = jnp.where(col - SEL_FEAT0 == (pos >> sel_shift), 1, feat)
    return jnp.where((col >= 0) & (col < HEAD_DIM), feat, 0).astype(F32)


def _pair_rms(x, g):
    sq = x * x
    hi = sq.astype(BF16)
    lo = (sq - hi.astype(F32)).astype(BF16)
    same_head = (_iota((LANES, LANES), 0) >> 6) == (_iota((LANES, LANES), 1) >> 6)
    ones = jnp.where(same_head, 1.0, 0.0).astype(BF16)
    ss = _dot(hi, ones) + _dot(lo, ones)
    return x * lax.rsqrt(ss * (1.0 / HEAD_DIM) + EPS) * jnp.concatenate([g, g], axis=1)


def _augment_pair_keys(kn, pos, sel_shift=None):
    lane = _iota(kn.shape, 1)
    return [jnp.where(lane >= HEAD_DIM if flipped else lane < HEAD_DIM, kn,
                      _key_features(pos, flipped, sel_shift)).astype(BF16) for flipped in (False, True)]


def _norm_query_t(x, g_col):
    ss = jnp.mean(x * x, axis=0, keepdims=True)
    return x * lax.rsqrt(ss + EPS) * g_col * (HEAD_DIM ** -0.5 * LOG2E)


def _augment_query(q_t, alibi, sel_bias=None, flipped=False):
    n = q_t.shape[1]
    feats = [alibi]
    used = alibi.shape[0]
    if sel_bias is not None:
        feats.append(sel_bias)
        used += sel_bias.shape[0]
    feats.append(jnp.zeros((HEAD_DIM - used, n), F32))
    return jnp.concatenate(feats + [q_t] if flipped else [q_t] + feats, axis=0).astype(BF16)


ONES_ROWS = 16
ACC_ROWS = HEAD_DIM + ONES_ROWS


def _flash_init(n):
    return jnp.full((1, n), M_INIT, F32), jnp.zeros((ACC_ROWS, n), F32)


def _flash_step(carry, sc, v_t):
    m, acc = carry
    m_new = jnp.maximum(m, jnp.max(sc, axis=0, keepdims=True))
    p = jnp.exp2(sc - m_new).astype(BF16)
    v_ones = jnp.concatenate([v_t, jnp.ones((ONES_ROWS, v_t.shape[1]), BF16)], axis=0)
    return m_new, jnp.exp2(m - m_new) * acc + _dot(v_ones, p)


def _flash_split(carry):
    m, acc = carry
    return m, acc[:HEAD_DIM], acc[HEAD_DIM:HEAD_DIM + 1]


def _flash_steps(carries, chunks, cap=None):
    return tuple(_flash_step(carry, sc if cap is None else jnp.minimum(sc, cap), v_t)
                 for carry, (sc, v_t) in zip(carries, chunks))


def _flash_reset(state):
    m_ref, acc_ref = state
    m_ref[...] = jnp.full(m_ref.shape, M_INIT, F32)
    acc_ref[...] = jnp.zeros(acc_ref.shape, F32)


def _flash_update(state, slot, sc, v_t, cap=None):
    m_ref, acc_ref = state
    m_ref[slot], acc_ref[slot] = _flash_step((m_ref[slot], acc_ref[slot]),
                                             sc if cap is None else jnp.minimum(sc, cap), v_t)


def _flash_update_all(state, slots, chunks, cap=None):
    for slot, (sc, v_t) in zip(slots, chunks):
        _flash_update(state, slot, sc, v_t, cap)


def _flash_update_pipelined(state, bufs, slots, c, next_scores, values, cap=None):
    def run(src_ref, dst_ref):
        for slot in slots:
            fresh = None if next_scores is None else next_scores(slot)
            _flash_update(state, slot, src_ref[slot], values(slot), cap)
            if fresh is not None:
                dst_ref[slot] = fresh

    pl.when((c & 1) == 0)(lambda: run(bufs[0], bufs[1]))
    pl.when((c & 1) == 1)(lambda: run(bufs[1], bufs[0]))


def _flash_result(state, slot):
    m_ref, acc_ref = state
    m, out, den = _flash_split((m_ref[slot], acc_ref[slot]))
    return out / jnp.maximum(den, TINY)


def _flash_scratch(chains, n):
    return [pltpu.VMEM((chains, 1, n), F32), pltpu.VMEM((chains, ACC_ROWS, n), F32)]


def _rank_select(score, blk, limit, count):
    ranks = []
    for r0 in range(0, score.shape[0], SUBLANES):
        tile = score[r0:r0 + SUBLANES]
        tile_blk = r0 + _iota(tile.shape, 0)
        rank = jnp.zeros(tile.shape, jnp.int32)
        for mm in range(limit):
            gm = score[mm:mm + 1, :]
            if mm < r0:
                beats = gm >= tile
            elif mm >= r0 + SUBLANES:
                beats = gm > tile
            else:
                beats = (gm > tile) | ((gm == tile) & (mm < tile_blk))
            rank = rank + jnp.where(beats, 1, 0)
        ranks.append(rank)
    return jnp.concatenate(ranks, axis=0) < count


def _tile_lanes(x, reps):
    return jnp.concatenate([x] * reps, axis=1) if reps > 1 else x


def _causal_cap(tk, tq, reps):
    keep = _iota((tk, tq), 0) <= _iota((tk, tq), 1)
    return _tile_lanes(jnp.where(keep, BIG, -BIG), reps)


def _window_edge_cap(tk, tq, reps):
    keep = _iota((tk, tq), 0) > _iota((tk, tq), 1)
    return _tile_lanes(jnp.where(keep, BIG, -BIG), reps)


def _store_chunked_t(x, dst_ref, tk):
    x_t = x.T.astype(BF16)
    for c in range(x.shape[0] // tk):
        dst_ref[c] = x_t[:, c * tk:(c + 1) * tk]


def _moba_kernel(alibi_ref, q_ref, k_ref, v_ref, z_ref, gq_ref, gk_ref, o_ref, ka_s, vt_s, km_s,
                 m_s, acc_s, sc0_s, sc1_s):
    i = pl.program_id(2)
    s = k_ref.shape[1]
    nblk = s // MOBA_BLOCK
    tq = MOBA_BLOCK
    hd = HEAD_DIM
    heads = range(ka_s.shape[0])

    @pl.when(i == 0)
    def _():
        pos = _iota((s, LANES), 0)
        for pair in range(len(heads) // 2):
            kn = _pair_rms(k_ref[0, :, pair * LANES:(pair + 1) * LANES], gk_ref[...])
            ka_s[2 * pair], ka_s[2 * pair + 1] = _augment_pair_keys(kn, pos, MOBA_BLOCK.bit_length() - 1)
            km = jnp.mean(kn.reshape(nblk, MOBA_BLOCK, LANES), axis=1)
            km_s[2 * pair], km_s[2 * pair + 1] = km[:, :hd], km[:, hd:]
        _store_chunked_t(v_ref[0], vt_s, tq)

    q_t = q_ref[0].T
    blk = _iota((nblk, tq), 0)
    qas = []
    for j in heads:
        qn = _norm_query_t(q_t[j * hd:(j + 1) * hd], gq_ref[...])
        gate = jnp.where(blk < i, _dot_exact(km_s[j], qn), NEG)
        chosen = _rank_select(gate, blk, nblk, MOBA_TOPK) & (blk < i)
        sel_bias = jnp.where(chosen | (blk == i), 0.0, -BIG)
        qas.append(_augment_query(qn, alibi_ref[j], sel_bias, flipped=j % 2 == 1))

    def scores(j, n):
        return _dot(ka_s[j, pl.ds(pl.multiple_of(n * tq, tq), tq), :], qas[j])

    def values(j, n):
        return vt_s[n, j * hd:(j + 1) * hd, :]

    state = (m_s, acc_s)
    _flash_reset(state)
    bufs = (sc0_s, sc1_s)
    for j in heads:
        sc0_s[j] = scores(j, 0)

    def body(n, carry):
        _flash_update_pipelined(state, bufs, heads, n, lambda j: scores(j, n + 1), lambda j: values(j, n))
        return carry

    lax.fori_loop(0, i, body, 0)
    diag_cap = _causal_cap(tq, tq, 1)
    _flash_update_pipelined(state, bufs, heads, i, None, lambda j: values(j, i), diag_cap)
    outs = [_flash_result(state, j) for j in heads]
    y = jnp.concatenate(outs, axis=0).T * _silu(z_ref[0])
    o_ref[0] = y.astype(o_ref.dtype)


def _moba(proj, gq, gk, *, heads_per_step=8):
    bsz, s, _ = proj.shape
    tq = MOBA_BLOCK
    hd = HEAD_DIM
    nblk = s // MOBA_BLOCK
    hps = heads_per_step
    wd = hps * hd
    assert SEL_FEAT0 + nblk <= hd and wd % LANES == 0
    q0, k0, v0, z0 = (c * GROUP_WIDTH // wd for c in (3, 4, 5, 6))
    tile = lambda c0: pl.BlockSpec((1, tq, wd), lambda b, p, i: (b, i, c0 + p))
    full = lambda c0: pl.BlockSpec((1, s, wd), lambda b, p, i: (b, 0, c0 + p))
    return pl.pallas_call(
        _moba_kernel,
        grid=(bsz, N_HEADS // hps, s // tq),
        in_specs=[pl.BlockSpec((hps, 2 * SLOPE_PIECES, tq), lambda b, p, i: (p, 0, 0)),
                  tile(q0), full(k0), full(v0), tile(z0),
                  pl.BlockSpec((hd, 1), lambda b, p, i: (0, 0)), pl.BlockSpec((1, hd), lambda b, p, i: (0, 0))],
        out_specs=pl.BlockSpec((1, tq, wd), lambda b, p, i: (b, i, p)),
        out_shape=jax.ShapeDtypeStruct((bsz, s, GROUP_WIDTH), BF16),
        scratch_shapes=[pltpu.VMEM((hps, s, LANES), BF16), pltpu.VMEM((nblk, wd, tq), BF16),
                        pltpu.VMEM((hps, nblk, hd), F32)] + _flash_scratch(hps, tq)
                       + [pltpu.VMEM((hps, tq, tq), F32)] * 2,
        name="moba",
    )(jnp.asarray(_alibi_query_features(tq)), proj, proj, proj, proj, gq.reshape(hd, 1), gk.reshape(1, hd))


def _cmp_kernel(xk_ref, xv_ref, wk1_ref, wv1_ref, pk_ref, pv_ref, bk1_ref, bv1_ref,
                wk2_ref, wv2_ref, bk2_ref, bv2_ref, gn_ref, ko_ref, vo_ref, wke_s, wve_s):
    half = xk_ref.shape[2]
    n = xk_ref.shape[1]
    hd = HEAD_DIM
    nh = NSA_CMP_HIDDEN

    @pl.when(pl.program_id(0) == 0)
    def _():
        for w_ref, we_ref in ((wk1_ref, wke_s), (wv1_ref, wve_s)):
            we_ref[...] = jnp.zeros(we_ref.shape, BF16)
            for tok in range(NSA_CMP_LEN):
                for g in range(KV_HEADS):
                    row = (tok * KV_HEADS + g) * hd
                    we_ref[row:row + hd, g * nh:(g + 1) * nh] = w_ref[tok * hd:(tok + 1) * hd, :]

    def hidden(x_ref, we_ref, pos_ref, b1_ref):
        x = x_ref[0].astype(BF16)
        first = _dot(x, we_ref[:half, :])
        second = _dot(x, we_ref[half:, :])
        pos = jnp.broadcast_to(pos_ref[...], (8, 2 * half)).astype(BF16)
        ph = _dot(pos[:, :half], we_ref[:half, :]) + _dot(pos[:, half:], we_ref[half:, :])
        return _silu(first + pltpu.roll(second, n - 1, 0) + ph[0:1, :] + b1_ref[...])

    hk = hidden(xk_ref, wke_s, pk_ref, bk1_ref).astype(BF16)
    hv = hidden(xv_ref, wve_s, pv_ref, bv1_ref).astype(BF16)
    nh = NSA_CMP_HIDDEN
    ks, vs = [], []
    for g in range(KV_HEADS):
        ks.append(_rms(_dot(hk[:, g * nh:(g + 1) * nh], wk2_ref[...]) + bk2_ref[...], gn_ref[...]))
        vs.append(_dot(hv[:, g * nh:(g + 1) * nh], wv2_ref[...]) + bv2_ref[...])
    ko_ref[0] = jnp.concatenate(ks, axis=1)
    vo_ref[0] = jnp.concatenate(vs, axis=1)


def _compress(xk, xv, pos_k, pos_v, kw1, kb1, kw2, kb2, vw1, vb1, vw2, vb2, gn):
    bsz, n, half = xk.shape
    hd = HEAD_DIM
    tile_pos = lambda p: jnp.tile(p[:, None, :], (1, KV_HEADS, 1)).reshape(1, 2 * half)
    tile_b = lambda b: jnp.tile(b.reshape(1, -1), (1, KV_HEADS))
    const = lambda shape: pl.BlockSpec(shape, lambda b: (0,) * len(shape))
    xs = pl.BlockSpec((1, n, half), lambda b: (b, 0, 0))
    hw = KV_HEADS * NSA_CMP_HIDDEN
    out = pl.BlockSpec((1, n, KV_HEADS * hd), lambda b: (b, 0, 0))
    expanded = pltpu.VMEM((2 * half, hw), BF16)
    return pl.pallas_call(
        _cmp_kernel,
        grid=(bsz,),
        in_specs=[xs, xs, const(kw1.shape), const(vw1.shape),
                  const((1, 2 * half)), const((1, 2 * half)), const((1, hw)), const((1, hw)),
                  const((NSA_CMP_HIDDEN, hd)), const((NSA_CMP_HIDDEN, hd)), const((1, hd)), const((1, hd)),
                  const((1, hd))],
        out_specs=[out, out],
        out_shape=[jax.ShapeDtypeStruct((bsz, n, KV_HEADS * hd), F32)] * 2,
        scratch_shapes=[expanded, expanded],
        name="nsa_compress",
    )(xk, xv, kw1.astype(BF16), vw1.astype(BF16), tile_pos(pos_k), tile_pos(pos_v), tile_b(kb1), tile_b(vb1),
      kw2.astype(BF16), vw2.astype(BF16), kb2.reshape(1, hd), vb2.reshape(1, hd), gn.reshape(1, hd))


def _group_query(q_t, alibi_ref, g, gq_col):
    hd = HEAD_DIM
    heads = range(Q_PER_KV * g, Q_PER_KV * (g + 1))
    qn = jnp.concatenate([_norm_query_t(q_t[h * hd:(h + 1) * hd], gq_col) for h in heads], axis=1)
    alibi = jnp.concatenate([alibi_ref[h] for h in heads], axis=1)
    return qn, alibi


def _augment_keys(k_ref, gk_ref, dst_ref, sel_shift=None):
    assert KV_HEADS == 2
    pos = _iota(k_ref.shape[1:], 0)
    dst_ref[0], dst_ref[1] = _augment_pair_keys(_pair_rms(k_ref[0], gk_ref[...]), pos, sel_shift)


def _chunk_scores(ka_ref, vt_ref, g, c, qa, tq):
    off = pl.multiple_of(c * tq, tq)
    return _dot(ka_ref[g, pl.ds(off, tq), :], qa), vt_ref[c, g * HEAD_DIM:(g + 1) * HEAD_DIM, :]


def _window_first_chunk(ka_ref, vt_ref, g, far, qa, tq, edge_cap):
    sc, v_t = _chunk_scores(ka_ref, vt_ref, g, jnp.maximum(far, 0), qa, tq)
    return jnp.minimum(sc, jnp.minimum(edge_cap, jnp.where(far >= 0, BIG, -BIG))), v_t


def _regroup_heads(o_t, tq):
    return jnp.concatenate([o_t[:, k * tq:(k + 1) * tq] for k in range(Q_PER_KV)], axis=0)


def _nsa_kernel(alibi_ref, q_ref, kc_ref, vc_ref, ks_ref, vs_ref, kw_ref, vw_ref, gate_ref, z_ref,
                gq_ref, gks_ref, gkw_ref, o_ref, kca_s, vct_s, ksa_s, vst_s, kwa_s, vwt_s, m_s, acc_s,
                sc0_s, sc1_s):
    i = pl.program_id(1)
    tq = q_ref.shape[1]
    s = ks_ref.shape[1]
    hd = HEAD_DIM
    ncmp = kc_ref.shape[1]
    nsb = s // NSA_SEL_BLOCK
    n4 = Q_PER_KV * tq

    @pl.when(i == 0)
    def _():
        _augment_keys(ks_ref, gks_ref, ksa_s, NSA_SEL_BLOCK.bit_length() - 1)
        _augment_keys(kw_ref, gkw_ref, kwa_s)
        _store_chunked_t(vs_ref[0], vst_s, tq)
        _store_chunked_t(vw_ref[0], vwt_s, tq)
        cend = _iota((ncmp, LANES), 0) * NSA_CMP_STRIDE + (NSA_CMP_LEN - 1)
        kca_s[0], kca_s[1] = _augment_pair_keys(kc_ref[0], cend)
        vct_s[...] = vc_ref[0].T.astype(BF16)

    t0 = i * tq
    q_t = q_ref[0].T
    gates_t = _sigmoid(gate_ref[0]).T
    diag_cap = _causal_cap(tq, tq, Q_PER_KV)
    edge_cap = _window_edge_cap(tq, tq, Q_PER_KV)
    cend =_iota((ncmp, tq), 0) * NSA_CMP_STRIDE + (NSA_CMP_LEN - 1)
    seen = _tile_lanes(cend <= t0 + _iota((ncmp, tq), 1), Q_PER_KV)
    cstart = _iota((nsb, ncmp), 1) * NSA_CMP_STRIDE
    bstart = _iota((nsb, ncmp), 0) * NSA_SEL_BLOCK
    overlap_t = jnp.where((cstart < bstart + NSA_SEL_BLOCK) & (cstart + NSA_CMP_LEN > bstart), 1.0, 0.0)
    blk = _iota((nsb, tq), 0)
    cur = (t0 + _iota((nsb, tq), 1)) >> (NSA_SEL_BLOCK.bit_length() - 1)
    forced = (blk == 0) | (blk == cur) | (blk == cur - 1)
    groups = range(KV_HEADS)
    far = i - NSA_WINDOW // tq
    lo = jnp.maximum(far + 1, 0)

    state = (m_s, acc_s)
    sel_slots = tuple(groups)
    win_slots = tuple(KV_HEADS + g for g in groups)
    _flash_reset(state)
    queries = [_group_query(q_t, alibi_ref, g, gq_ref[...]) for g in groups]
    qa_win = [_augment_query(qn, alibi, flipped=g == 1) for g, (qn, alibi) in enumerate(queries)]
    qa_sel = []

    def scores(slot, c):
        g = slot % KV_HEADS
        if slot in sel_slots:
            return _chunk_scores(ksa_s, vst_s, g, c, qa_sel[g], tq)[0]
        return _chunk_scores(kwa_s, vwt_s, g, c, qa_win[g], tq)[0]

    def values(slot, c):
        g = slot % KV_HEADS
        return (vst_s if slot in sel_slots else vwt_s)[c, g * hd:(g + 1) * hd, :]

    first = [_window_first_chunk(kwa_s, vwt_s, g, far, qa_win[g], tq, edge_cap) for g in groups]
    cmp_sc = [_dot(kca_s[g], qa_win[g]) for g in groups]
    sc_win_lo = [scores(slot, lo) for slot in win_slots]
    _flash_update_all(state, win_slots, first)
    o_cmp = []
    for g in groups:
        qn, alibi = queries[g]
        sc = jnp.where(seen, cmp_sc[g], -BIG)
        e = jnp.where(seen, jnp.exp2(sc - jnp.max(sc, axis=0, keepdims=True)), 0.0)
        p_c = e / jnp.maximum(jnp.sum(e, axis=0, keepdims=True), TINY)
        o_cmp.append(_dot(vct_s[g * hd:(g + 1) * hd, :], p_c.astype(BF16)))

        psum = p_c[:, 0:tq]
        for k in range(1, Q_PER_KV):
            psum = psum + p_c[:, k * tq:(k + 1) * tq]
        imp = _dot_exact(overlap_t, psum)
        imp = jnp.where(blk <= cur, jnp.where(forced, NSA_FORCE, imp), NEG)
        chosen = _rank_select(imp, blk, nsb, min(NSA_TOPN, nsb))
        sel_bias = _tile_lanes(jnp.where(chosen, 0.0, -BIG), Q_PER_KV)
        qa_sel.append(_augment_query(qn, alibi, sel_bias, flipped=g == 1))

    bufs = (sc0_s, sc1_s)
    for slot in sel_slots:
        sc0_s[slot] = scores(slot, 0)
    for slot, sc in zip(win_slots, sc_win_lo):
        sc0_s[slot] = sc
        sc1_s[slot] = sc

    def body(slots, c, carry):
        _flash_update_pipelined(state, bufs, slots, c, lambda slot: scores(slot, c + 1),
                                lambda slot: values(slot, c))
        return carry

    both_slots = sel_slots + win_slots
    lax.fori_loop(0, lo, functools.partial(body, sel_slots), 0)
    lax.fori_loop(lo, i, functools.partial(body, both_slots), 0)
    _flash_update_pipelined(state, bufs, both_slots, i, None, lambda slot: values(slot, i), diag_cap)

    outs = []
    for g in groups:
        def gate_row(branch):
            rows = [gates_t[branch * N_HEADS + h:branch * N_HEADS + h + 1, :]
                    for h in range(Q_PER_KV * g, Q_PER_KV * (g + 1))]
            return jnp.concatenate(rows, axis=1)

        o_slc = _flash_result(state, sel_slots[g])
        o_win = _flash_result(state, win_slots[g])
        o = gate_row(0) * o_cmp[g] + gate_row(1) * o_slc + gate_row(2) * o_win
        outs.append(_regroup_heads(o, tq))
    y = jnp.concatenate(outs, axis=0).T * _silu(z_ref[0])
    o_ref[0] = y.astype(o_ref.dtype)


def _nsa(proj, kcn, vc, gq, gks, gkw, *, tq=256):
    bsz, s, _ = proj.shape
    hd = HEAD_DIM
    gw = GROUP_WIDTH
    ncmp = kcn.shape[1]
    assert SEL_FEAT0 + s // NSA_SEL_BLOCK <= hd and ncmp == LANES
    wide = lambda c0: pl.BlockSpec((1, tq, gw), lambda b, i: (b, i, c0 // 4))
    full = lambda c0: pl.BlockSpec((1, s, LANES), lambda b, i: (b, 0, c0))
    cmp = pl.BlockSpec((1, ncmp, LANES), lambda b, i: (b, 0, 0))
    row = pl.BlockSpec((1, hd), lambda b, i: (0, 0))
    ka = pltpu.VMEM((KV_HEADS, s, LANES), BF16)
    vt = pltpu.VMEM((s // tq, LANES, tq), BF16)
    return pl.pallas_call(
        _nsa_kernel,
        grid=(bsz, s // tq),
        in_specs=[pl.BlockSpec((N_HEADS, 2 * SLOPE_PIECES, tq), lambda b, i: (0, 0, 0)),
                  wide(ODD_QC), cmp, cmp, full(ODD_KS), full(ODD_VS), full(ODD_KW), full(ODD_VW),
                  pl.BlockSpec((1, tq, LANES), lambda b, i: (b, i, ODD_GC)), wide(ODD_ZC),
                  pl.BlockSpec((hd, 1), lambda b, i: (0, 0)), row, row],
        out_specs=pl.BlockSpec((1, tq, gw), lambda b, i: (b, i, 0)),
        out_shape=jax.ShapeDtypeStruct((bsz, s, gw), BF16),
        scratch_shapes=[pltpu.VMEM((KV_HEADS, ncmp, LANES), BF16), pltpu.VMEM((LANES, ncmp), BF16),
                        ka, vt, ka, vt] + _flash_scratch(2 * KV_HEADS, Q_PER_KV * tq)
                       + [pltpu.VMEM((2 * KV_HEADS, tq, Q_PER_KV * tq), F32)] * 2,
        name="nsa",
    )(jnp.asarray(_alibi_query_features(tq)), proj, kcn, vc, proj, proj, proj, proj, proj, proj,
      gq.reshape(hd, 1), gks.reshape(1, hd), gkw.reshape(1, hd))


def _swa_kernel(sinks_ref, alibi_ref, q_ref, k_ref, v_ref, z_ref, gq_ref, gk_ref, o_ref, ka_s, vt_s):
    step = pl.program_id(1)
    tq = SWA_WINDOW
    tiles = q_ref.shape[1] // tq
    s = k_ref.shape[1]

    @pl.when(step == 0)
    def _():
        _augment_keys(k_ref, gk_ref, ka_s)
        _store_chunked_t(v_ref[0], vt_s, tq)

    q_t = q_ref[0].T
    diag_cap = _causal_cap(tq, tq, Q_PER_KV)
    edge_cap = _window_edge_cap(tq, tq, Q_PER_KV)
    chains = [(u, g) for u in range(tiles) for g in range(KV_HEADS)]
    qas = [_augment_query(*_group_query(q_t[:, u * tq:(u + 1) * tq], alibi_ref, g, gq_ref[...]), flipped=g == 1)
           for u, g in chains]
    first = [_window_first_chunk(ka_s, vt_s, g, step * tiles + u - 1, qa, tq, edge_cap)
             for (u, g), qa in zip(chains, qas)]
    own = [_chunk_scores(ka_s, vt_s, g, step * tiles + u, qa, tq) for (u, g), qa in zip(chains, qas)]
    carries = _flash_steps([_flash_init(Q_PER_KV * tq) for _ in chains], first)
    carries = _flash_steps(carries, own, diag_cap)
    outs = []
    for (u, g), carry in zip(chains, carries):
        m, acc, l = _flash_split(carry)
        tpos = ((step * tiles + u) * tq + _iota((1, tq), 1)).astype(F32)
        sink = jnp.concatenate([sinks_ref[h] * LOG2E + (SLOPES[h] * LOG2E) * tpos
                                for h in range(Q_PER_KV * g, Q_PER_KV * (g + 1))], axis=1)
        mf = jnp.maximum(m, sink)
        alpha = jnp.exp2(m - mf)
        o = acc * alpha / jnp.maximum(l * alpha + jnp.exp2(sink - mf), TINY)
        outs.append(_regroup_heads(o, tq))
    o_t = jnp.concatenate([jnp.concatenate(outs[u * KV_HEADS:(u + 1) * KV_HEADS], axis=0) for u in range(tiles)],
                          axis=1)
    y = o_t.T * _silu(z_ref[0])
    o_ref[0] = y.astype(o_ref.dtype)


def _swa(proj, sinks, gq, gk, *, tiles_per_step=4):
    bsz, s, _ = proj.shape
    hd = HEAD_DIM
    gw = GROUP_WIDTH
    tq = SWA_WINDOW
    rows = tiles_per_step * tq
    wide = lambda c0: pl.BlockSpec((1, rows, gw), lambda b, i: (b, i, c0 // 4))
    full = lambda c0: pl.BlockSpec((1, s, LANES), lambda b, i: (b, 0, c0))
    return pl.pallas_call(
        _swa_kernel,
        grid=(bsz, s // rows),
        in_specs=[pl.BlockSpec(memory_space=pltpu.SMEM),
                  pl.BlockSpec((N_HEADS, 2 * SLOPE_PIECES, tq), lambda b, i: (0, 0, 0)),
                  wide(ODD_QD), full(ODD_KD), full(ODD_VD), wide(ODD_ZD),
                  pl.BlockSpec((hd, 1), lambda b, i: (0, 0)), pl.BlockSpec((1, hd), lambda b, i: (0, 0))],
        out_specs=pl.BlockSpec((1, rows, gw), lambda b, i: (b, i, 0)),
        out_shape=jax.ShapeDtypeStruct((bsz, s, gw), BF16),
        scratch_shapes=[pltpu.VMEM((KV_HEADS, s, LANES), BF16), pltpu.VMEM((s // tq, LANES, tq), BF16)],
        name="swa",
    )(sinks.astype(F32), jnp.asarray(_alibi_query_features(tq)), proj, proj, proj, proj,
      gq.reshape(hd, 1), gk.reshape(1, hd))


def _pack_odd_weight(w):
    gw, kw, ng = GROUP_WIDTH, KV_HEADS * HEAD_DIM, 3 * N_HEADS
    sizes = [gw, kw, kw, kw, kw, kw, kw, ng, gw, gw, kw, kw, gw]
    starts = np.concatenate([[0], np.cumsum(sizes)]).tolist()
    part = lambda k: w[:, starts[k]:starts[k + 1]]
    pad = jnp.zeros((w.shape[0], LANES - ng), w.dtype)
    order = [0, 8, 9, 12, 1, 2, 3, 4, 5, 6, 10, 11, 7]
    return jnp.concatenate([part(k) for k in order] + [pad], axis=1).astype(BF16)


def kernel(x, norm_g, w_out, e_w_in, a_conv_w, a_conv_b, a_ln_g, a_ln_b, b_qnorm_g, b_knorm_g, o_w_in, c_qnorm_g, c_knorm_cmp_g, c_knorm_slc_g, c_knorm_win_g, c_pos_k, c_pos_v, c_k_w1, c_k_b1, c_k_w2, c_k_b2, c_v_w1, c_v_b1, c_v_w2, c_v_b2, d_qnorm_g, d_knorm_g, d_sinks):
    bsz, s, d = x.shape
    m = bsz * s
    assert s % MOBA_BLOCK == 0 and d == 2 * GROUP_WIDTH
    x2 = x.reshape(m, d)

    (proj,) = _inproj(x2, norm_g[0], e_w_in[0].astype(BF16), chunk=GROUP_WIDTH)
    proj = proj.reshape(bsz, s, -1)
    y_a = _conv_module(proj, a_conv_w[0], a_conv_b[0], a_ln_g[0], a_ln_b[0])
    y_b = _moba(proj, b_qnorm_g[0], b_knorm_g[0])
    x2 = _outproj(x2, y_a.reshape(m, -1), y_b.reshape(m, -1), w_out[0].astype(BF16))

    proj, k_cmp, v_cmp = _inproj(x2, norm_g[1], _pack_odd_weight(o_w_in[0]), chunk=5 * LANES,
                                 copies=((ODD_KC * LANES, LANES), (ODD_VC * LANES, LANES)), group=NSA_CMP_STRIDE)
    proj = proj.reshape(bsz, s, ODD_COLS)
    rows16 = lambda t: t.reshape(bsz, s // NSA_CMP_STRIDE, NSA_CMP_STRIDE * LANES)
    kcn, vc = _compress(rows16(k_cmp), rows16(v_cmp), c_pos_k[0], c_pos_v[0], c_k_w1[0], c_k_b1[0],
                        c_k_w2[0], c_k_b2[0], c_v_w1[0], c_v_b1[0], c_v_w2[0], c_v_b2[0], c_knorm_cmp_g[0])
    y_c = _nsa(proj, kcn, vc, c_qnorm_g[0], c_knorm_slc_g[0], c_knorm_win_g[0])
    y_d = _swa(proj, d_sinks[0], d_qnorm_g[0], d_knorm_g[0])
    x2 = _outproj(x2, y_c.reshape(m, -1), y_d.reshape(m, -1), w_out[1].astype(BF16))
    return x2.reshape(bsz, s, d)
```

```python
import functools

import ml_dtypes
import numpy as np
import jax
import jax.numpy as jnp
from jax import lax
from jax.experimental import pallas as pl
from jax.experimental.pallas import tpu as pltpu

HEAD_DIM = 64
N_HEADS = 8
GROUP_WIDTH = N_HEADS * HEAD_DIM
CONV_WIDTH = 31
MOBA_BLOCK = 256
MOBA_TOPK = 3
KV_HEADS = 2
Q_PER_KV = N_HEADS // KV_HEADS
NSA_CMP_LEN = 32
NSA_CMP_STRIDE = 16
NSA_CMP_HIDDEN = 256
NSA_SEL_BLOCK = 64
NSA_TOPN = 16
NSA_WINDOW = 512
NSA_FORCE = 1e4
SWA_WINDOW = 128
EPS = 1e-6
NEG = -1e30
TINY = 1e-30
LANES = 128
SUBLANES = 8
CONV_HALO = 32

LOG2E = float(np.log2(np.e))
BIG = 2.0 ** 99
M_INIT = -1e38
POS_SPLIT_SHIFT = 8
SLOPE_PIECES = 4
SEL_FEAT0 = 2 * SLOPE_PIECES

F32 = jnp.float32
BF16 = jnp.bfloat16

ODD_QC, ODD_ZC, ODD_QD, ODD_ZD = 0, 4, 8, 12
ODD_KC, ODD_VC, ODD_KS, ODD_VS, ODD_KW, ODD_VW, ODD_KD, ODD_VD, ODD_GC = 16, 17, 18, 19, 20, 21, 22, 23, 24
ODD_COLS = 25 * LANES


def _alibi_slopes(n):
    return [float(2.0 ** (-8.0 * (i + 1) / n)) for i in range(n)]


SLOPES = _alibi_slopes(N_HEADS)


def _alibi_query_features(width):
    table = np.zeros((N_HEADS, 2 * SLOPE_PIECES, width), np.float32)
    for h, slope in enumerate(SLOPES):
        rest = np.float64(slope) * LOG2E
        for k in range(SLOPE_PIECES):
            piece = float(np.float32(rest).astype(ml_dtypes.bfloat16).astype(np.float32))
            table[h, k] = piece
            table[h, SLOPE_PIECES + k] = piece
            rest -= piece
    return table


def _dot(a, b):
    return jnp.dot(a, b, preferred_element_type=F32)


def _split_bf16(x, pieces):
    out = []
    for _ in range(pieces):
        out.append(x.astype(BF16))
        x = x - out[-1].astype(F32)
    return out


def _dot_fine(a, b, a_is_bf16_exact=False):
    if a_is_bf16_exact:
        return sum(_dot(a.astype(BF16), piece) for piece in _split_bf16(b, 3))
    (a_hi, a_lo), (b_hi, b_lo) = _split_bf16(a, 2), _split_bf16(b, 2)
    return _dot(a_hi, b_hi) + (_dot(a_hi, b_lo) + _dot(a_lo, b_hi))


def _sigmoid(x):
    return 1.0 / (1.0 + jnp.exp(-x))


def _silu(x):
    return x * _sigmoid(x)


def _rms(x, g):
    return x * lax.rsqrt(jnp.mean(x * x, axis=-1, keepdims=True) + EPS) * g


def _iota(shape, dim):
    return lax.broadcasted_iota(jnp.int32, shape, dim)


def _inproj_kernel(x_ref, g_ref, w_ref, o_ref, *rest, chunk, copies, group):
    copy_refs, stage_refs = rest[:len(copies)], rest[len(copies):]
    h = _rms(x_ref[...], g_ref[...]).astype(BF16)
    for c in range(o_ref.shape[1] // chunk):
        o_ref[:, c * chunk:(c + 1) * chunk] = _dot(h, w_ref[:, c * chunk:(c + 1) * chunk])
    for ref, stage, (start, width) in zip(copy_refs, stage_refs, copies):
        stage[...] = o_ref[:, start:start + width]
        for t in range(group):
            ref[:, t * width:(t + 1) * width] = stage[pl.ds(t, ref.shape[0], stride=group), :]


def _inproj(x2d, g, w, *, tm=512, chunk, copies=(), group=1):
    m, d = x2d.shape
    e = w.shape[1]
    rows = lambda n, width: pl.BlockSpec((n, width), lambda i: (i, 0))
    const = lambda shape: pl.BlockSpec(shape, lambda i: (0, 0))
    return pl.pallas_call(
        functools.partial(_inproj_kernel, chunk=chunk, copies=tuple(copies), group=group),
        grid=(m // tm,),
        in_specs=[rows(tm, d), const((1, d)), const((d, e))],
        out_specs=[rows(tm, e)] + [rows(tm // group, group * width) for _, width in copies],
        out_shape=[jax.ShapeDtypeStruct((m, e), F32)]
                  + [jax.ShapeDtypeStruct((m // group, group * width), F32) for _, width in copies],
        scratch_shapes=[pltpu.VMEM((tm, width), F32) for _, width in copies],
        name="inproj",
    )(x2d, g.reshape(1, d), w)


def _outproj_kernel(x_ref, ya_ref, yb_ref, wa_ref, wb_ref, o_ref):
    o_ref[...] = x_ref[...] + _dot(ya_ref[...], wa_ref[...]) + _dot(yb_ref[...], wb_ref[...])


def _outproj(x2d, ya, yb, w, *, tm=1024):
    m, d = x2d.shape
    gw = ya.shape[1]
    return pl.pallas_call(
        _outproj_kernel,
        grid=(m // tm,),
        in_specs=[pl.BlockSpec((tm, d), lambda i: (i, 0)),
                  pl.BlockSpec((tm, gw), lambda i: (i, 0)),
                  pl.BlockSpec((tm, gw), lambda i: (i, 0)),
                  pl.BlockSpec((gw, d), lambda i: (0, 0)),
                  pl.BlockSpec((gw, d), lambda i: (0, 0))],
        out_specs=pl.BlockSpec((tm, d), lambda i: (i, 0)),
        out_shape=jax.ShapeDtypeStruct((m, d), F32),
        name="outproj",
    )(x2d, ya, yb, w[:gw], w[gw:])


def _conv_kernel(uv_ref, ug_ref, uvh_ref, ugh_ref, z_ref, w_ref, b_ref, lg_ref, lb_ref, o_ref, h_s, *, rows):
    i = pl.program_id(1)
    ts = uv_ref.shape[1]
    halo = uvh_ref[0] * _sigmoid(ugh_ref[0])
    h_s[0:CONV_HALO, :] = jnp.where(i > 0, halo, 0.0)
    h_s[CONV_HALO:, :] = uv_ref[0] * _sigmoid(ug_ref[0])
    base = CONV_HALO - (CONV_WIDTH - 1)
    for c in range(ts // rows):
        acc = None
        for b in range(SUBLANES):
            n = rows if b == 0 else rows + SUBLANES
            part = None
            for a in range((base + CONV_WIDTH - 1) // SUBLANES + 1):
                j = SUBLANES * a + b - base
                if 0 <= j < CONV_WIDTH:
                    term = w_ref[j:j + 1, :] * h_s[c * rows + SUBLANES * a:c * rows + SUBLANES * a + n, :]
                    part = term if part is None else part + term
            part = part[b:b + rows]
            acc = part if acc is None else acc + part
        y = acc + b_ref[...]
        mu = jnp.mean(y, axis=-1, keepdims=True)
        yc = y - mu
        y = yc * lax.rsqrt(jnp.mean(yc * yc, axis=-1, keepdims=True) + EPS) * lg_ref[...] + lb_ref[...]
        out = _silu(y) * _silu(z_ref[0, c * rows:(c + 1) * rows, :])
        o_ref[0, c * rows:(c + 1) * rows, :] = out.astype(o_ref.dtype)


def _conv_module(proj, w, b, lg, lb, *, ts=256, rows=128):
    bsz, s, _ = proj.shape
    gw = GROUP_WIDTH
    hb = ts // CONV_HALO
    cur = lambda col: pl.BlockSpec((1, ts, gw), lambda bi, i: (bi, i, col))
    halo = lambda col: pl.BlockSpec((1, CONV_HALO, gw), lambda bi, i: (bi, jnp.maximum(i * hb - 1, 0), col))
    vec = pl.BlockSpec((1, gw), lambda bi, i: (0, 0))
    return pl.pallas_call(
        functools.partial(_conv_kernel, rows=rows),
        grid=(bsz, s // ts),
        in_specs=[cur(0), cur(1), halo(0), halo(1), cur(2),
                  pl.BlockSpec((CONV_WIDTH, gw), lambda bi, i: (0, 0)), vec, vec, vec],
        out_specs=pl.BlockSpec((1, ts, gw), lambda bi, i: (bi, i, 0)),
        out_shape=jax.ShapeDtypeStruct((bsz, s, gw), BF16),
        scratch_shapes=[pltpu.VMEM((CONV_HALO + ts, gw), F32)],
        name="conv_module",
    )(proj, proj, proj, proj, proj, w, b.reshape(1, gw), lg.reshape(1, gw), lb.reshape(1, gw))


def _key_features(pos, flipped, sel_shift=None):
    col = _iota(pos.shape, 1) - (0 if flipped else HEAD_DIM)
    hi = (pos >> POS_SPLIT_SHIFT) << POS_SPLIT_SHIFT
    lo = pos & ((1 << POS_SPLIT_SHIFT) - 1)
    feat = jnp.where(col < SLOPE_PIECES, hi, jnp.where(col < 2 * SLOPE_PIECES, lo, 0))
    if sel_shift is not None:
        feat = jnp.where(col - SEL_FEAT0 == (pos >> sel_shift), 1, feat)
    return jnp.where((col >= 0) & (col < HEAD_DIM), feat, 0).astype(F32)


def _pair_rms(x, g):
    sq = x * x
    hi = sq.astype(BF16)
    lo = (sq - hi.astype(F32)).astype(BF16)
    same_head = (_iota((LANES, LANES), 0) >> 6) == (_iota((LANES, LANES), 1) >> 6)
    ones = jnp.where(same_head, 1.0, 0.0).astype(BF16)
    ss = _dot(hi, ones) + _dot(lo, ones)
    return x * lax.rsqrt(ss * (1.0 / HEAD_DIM) + EPS) * jnp.concatenate([g, g], axis=1)


def _augment_pair_keys(kn, pos, sel_shift=None):
    lane = _iota(kn.shape, 1)
    return [jnp.where(lane >= HEAD_DIM if flipped else lane < HEAD_DIM, kn,
                      _key_features(pos, flipped, sel_shift)).astype(BF16) for flipped in (False, True)]


def _norm_query_t(x, g_col):
    ss = jnp.mean(x * x, axis=0, keepdims=True)
    return x * lax.rsqrt(ss + EPS) * g_col * (HEAD_DIM ** -0.5 * LOG2E)


def _augment_query(q_t, alibi, sel_bias=None, flipped=False):
    n = q_t.shape[1]
    feats = [alibi]
    used = alibi.shape[0]
    if sel_bias is not None:
        feats.append(sel_bias)
        used += sel_bias.shape[0]
    feats.append(jnp.zeros((HEAD_DIM - used, n), F32))
    return jnp.concatenate(feats + [q_t] if flipped else [q_t] + feats, axis=0).astype(BF16)


ONES_ROWS = 16
ACC_ROWS = HEAD_DIM + ONES_ROWS


def _flash_init(n):
    return jnp.full((1, n), M_INIT, F32), jnp.zeros((ACC_ROWS, n), F32)


def _flash_step(carry, sc, v_t):
    m, acc = carry
    m_new = jnp.maximum(m, jnp.max(sc, axis=0, keepdims=True))
    p = jnp.exp2(sc - m_new).astype(BF16)
    v_ones = jnp.concatenate([v_t, jnp.ones((ONES_ROWS, v_t.shape[1]), BF16)], axis=0)
    return m_new, jnp.exp2(m - m_new) * acc + _dot(v_ones, p)


def _flash_split(carry):
    m, acc = carry
    return m, acc[:HEAD_DIM], acc[HEAD_DIM:HEAD_DIM + 1]


def _flash_steps(carries, chunks, cap=None):
    return tuple(_flash_step(carry, sc if cap is None else jnp.minimum(sc, cap), v_t)
                 for carry, (sc, v_t) in zip(carries, chunks))


def _flash_reset(state):
    m_ref, acc_ref = state
    m_ref[...] = jnp.full(m_ref.shape, M_INIT, F32)
    acc_ref[...] = jnp.zeros(acc_ref.shape, F32)


def _flash_update(state, slot, sc, v_t, cap=None):
    m_ref, acc_ref = state
    m_ref[slot], acc_ref[slot] = _flash_step((m_ref[slot], acc_ref[slot]),
                                             sc if cap is None else jnp.minimum(sc, cap), v_t)


def _flash_update_all(state, slots, chunks, cap=None):
    for slot, (sc, v_t) in zip(slots, chunks):
        _flash_update(state, slot, sc, v_t, cap)


def _flash_update_pipelined(state, bufs, slots, c, next_scores, values, cap=None):
    def run(src_ref, dst_ref):
        for slot in slots:
            fresh = None if next_scores is None else next_scores(slot)
            _flash_update(state, slot, src_ref[slot], values(slot), cap)
            if fresh is not None:
                dst_ref[slot] = fresh

    pl.when((c & 1) == 0)(lambda: run(bufs[0], bufs[1]))
    pl.when((c & 1) == 1)(lambda: run(bufs[1], bufs[0]))


def _flash_result(state, slot):
    m_ref, acc_ref = state
    m, out, den = _flash_split((m_ref[slot], acc_ref[slot]))
    return out / jnp.maximum(den, TINY)


def _flash_scratch(chains, n):
    return [pltpu.VMEM((chains, 1, n), F32), pltpu.VMEM((chains, ACC_ROWS, n), F32)]


def _rank_select(score, blk, limit, count):
    ranks = []
    for r0 in range(0, score.shape[0], SUBLANES):
        tile = score[r0:r0 + SUBLANES]
        tile_blk = r0 + _iota(tile.shape, 0)
        rank = jnp.zeros(tile.shape, jnp.int32)
        for mm in range(limit):
            gm = score[mm:mm + 1, :]
            if mm < r0:
                beats = gm >= tile
            elif mm >= r0 + SUBLANES:
                beats = gm > tile
            else:
                beats = (gm > tile) | ((gm == tile) & (mm < tile_blk))
            rank = rank + jnp.where(beats, 1, 0)
        ranks.append(rank)
    return jnp.concatenate(ranks, axis=0) < count


def _tile_lanes(x, reps):
    return jnp.concatenate([x] * reps, axis=1) if reps > 1 else x


def _causal_cap(tk, tq, reps):
    keep = _iota((tk, tq), 0) <= _iota((tk, tq), 1)
    return _tile_lanes(jnp.where(keep, BIG, -BIG), reps)


def _window_edge_cap(tk, tq, reps):
    keep = _iota((tk, tq), 0) > _iota((tk, tq), 1)
    return _tile_lanes(jnp.where(keep, BIG, -BIG), reps)


def _store_chunked_t(x, dst_ref, tk):
    x_t = x.T.astype(BF16)
    for c in range(x.shape[0] // tk):
        dst_ref[c] = x_t[:, c * tk:(c + 1) * tk]


def _moba_kernel(alibi_ref, q_ref, k_ref, v_ref, z_ref, gq_ref, gk_ref, o_ref, ka_s, vt_s, km_s,
                 m_s, acc_s, sc0_s, sc1_s):
    i = pl.program_id(2)
    s = k_ref.shape[1]
    nblk = s // MOBA_BLOCK
    tq = MOBA_BLOCK
    hd = HEAD_DIM
    heads = range(ka_s.shape[0])

    @pl.when(i == 0)
    def _():
        pos = _iota((s, LANES), 0)
        for pair in range(len(heads) // 2):
            kn = _pair_rms(k_ref[0, :, pair * LANES:(pair + 1) * LANES], gk_ref[...])
            ka_s[2 * pair], ka_s[2 * pair + 1] = _augment_pair_keys(kn, pos, MOBA_BLOCK.bit_length() - 1)
            km = jnp.mean(kn.reshape(nblk, MOBA_BLOCK, LANES), axis=1)
            km_s[2 * pair], km_s[2 * pair + 1] = km[:, :hd], km[:, hd:]
        _store_chunked_t(v_ref[0], vt_s, tq)

    q_t = q_ref[0].T
    blk = _iota((nblk, tq), 0)
    qas = []
    for j in heads:
        qn = _norm_query_t(q_t[j * hd:(j + 1) * hd], gq_ref[...])
        gate = jnp.where(blk < i, _dot_fine(km_s[j], qn), NEG)
        chosen = _rank_select(gate, blk, nblk, MOBA_TOPK) & (blk < i)
        sel_bias = jnp.where(chosen | (blk == i), 0.0, -BIG)
        qas.append(_augment_query(qn, alibi_ref[j], sel_bias, flipped=j % 2 == 1))

    def scores(j, n):
        return _dot(ka_s[j, pl.ds(pl.multiple_of(n * tq, tq), tq), :], qas[j])

    def values(j, n):
        return vt_s[n, j * hd:(j + 1) * hd, :]

    state = (m_s, acc_s)
    _flash_reset(state)
    bufs = (sc0_s, sc1_s)
    for j in heads:
        sc0_s[j] = scores(j, 0)

    def body(n, carry):
        _flash_update_pipelined(state, bufs, heads, n, lambda j: scores(j, n + 1), lambda j: values(j, n))
        return carry

    lax.fori_loop(0, i, body, 0)
    diag_cap = _causal_cap(tq, tq, 1)
    _flash_update_pipelined(state, bufs, heads, i, None, lambda j: values(j, i), diag_cap)
    outs = [_flash_result(state, j) for j in heads]
    y = jnp.concatenate(outs, axis=0).T * _silu(z_ref[0])
    o_ref[0] = y.astype(o_ref.dtype)


def _moba(proj, gq, gk, *, heads_per_step=8):
    bsz, s, _ = proj.shape
    tq = MOBA_BLOCK
    hd = HEAD_DIM
    nblk = s // MOBA_BLOCK
    hps = heads_per_step
    wd = hps * hd
    assert SEL_FEAT0 + nblk <= hd and wd % LANES == 0
    q0, k0, v0, z0 = (c * GROUP_WIDTH // wd for c in (3, 4, 5, 6))
    tile = lambda c0: pl.BlockSpec((1, tq, wd), lambda b, p, i: (b, i, c0 + p))
    full = lambda c0: pl.BlockSpec((1, s, wd), lambda b, p, i: (b, 0, c0 + p))
    return pl.pallas_call(
        _moba_kernel,
        grid=(bsz, N_HEADS // hps, s // tq),
        in_specs=[pl.BlockSpec((hps, 2 * SLOPE_PIECES, tq), lambda b, p, i: (p, 0, 0)),
                  tile(q0), full(k0), full(v0), tile(z0),
                  pl.BlockSpec((hd, 1), lambda b, p, i: (0, 0)), pl.BlockSpec((1, hd), lambda b, p, i: (0, 0))],
        out_specs=pl.BlockSpec((1, tq, wd), lambda b, p, i: (b, i, p)),
        out_shape=jax.ShapeDtypeStruct((bsz, s, GROUP_WIDTH), BF16),
        scratch_shapes=[pltpu.VMEM((hps, s, LANES), BF16), pltpu.VMEM((nblk, wd, tq), BF16),
                        pltpu.VMEM((hps, nblk, hd), F32)] + _flash_scratch(hps, tq)
                       + [pltpu.VMEM((hps, tq, tq), F32)] * 2,
        name="moba",
    )(jnp.asarray(_alibi_query_features(tq)), proj, proj, proj, proj, gq.reshape(hd, 1), gk.reshape(1, hd))


def _cmp_kernel(xk_ref, xv_ref, wk1_ref, wv1_ref, pk_ref, pv_ref, bk1_ref, bv1_ref,
                wk2_ref, wv2_ref, bk2_ref, bv2_ref, gn_ref, ko_ref, vo_ref, wke_s, wve_s):
    half = xk_ref.shape[2]
    n = xk_ref.shape[1]
    hd = HEAD_DIM
    nh = NSA_CMP_HIDDEN

    @pl.when(pl.program_id(0) == 0)
    def _():
        for w_ref, we_ref in ((wk1_ref, wke_s), (wv1_ref, wve_s)):
            we_ref[...] = jnp.zeros(we_ref.shape, BF16)
            for tok in range(NSA_CMP_LEN):
                for g in range(KV_HEADS):
                    row = (tok * KV_HEADS + g) * hd
                    we_ref[row:row + hd, g * nh:(g + 1) * nh] = w_ref[tok * hd:(tok + 1) * hd, :]

    def hidden(x_ref, we_ref, pos_ref, b1_ref):
        x = x_ref[0].astype(BF16)
        first = _dot(x, we_ref[:half, :])
        second = _dot(x, we_ref[half:, :])
        pos = jnp.broadcast_to(pos_ref[...], (8, 2 * half)).astype(BF16)
        ph = _dot(pos[:, :half], we_ref[:half, :]) + _dot(pos[:, half:], we_ref[half:, :])
        return _silu(first + pltpu.roll(second, n - 1, 0) + ph[0:1, :] + b1_ref[...])

    hk = hidden(xk_ref, wke_s, pk_ref, bk1_ref).astype(BF16)
    hv = hidden(xv_ref, wve_s, pv_ref, bv1_ref).astype(BF16)
    nh = NSA_CMP_HIDDEN
    ks, vs = [], []
    for g in range(KV_HEADS):
        ks.append(_rms(_dot(hk[:, g * nh:(g + 1) * nh], wk2_ref[...]) + bk2_ref[...], gn_ref[...]))
        vs.append(_dot(hv[:, g * nh:(g + 1) * nh], wv2_ref[...]) + bv2_ref[...])
    ko_ref[0] = jnp.concatenate(ks, axis=1)
    vo_ref[0] = jnp.concatenate(vs, axis=1)


def _compress(xk, xv, pos_k, pos_v, kw1, kb1, kw2, kb2, vw1, vb1, vw2, vb2, gn):
    bsz, n, half = xk.shape
    hd = HEAD_DIM
    tile_pos = lambda p: jnp.tile(p[:, None, :], (1, KV_HEADS, 1)).reshape(1, 2 * half)
    tile_b = lambda b: jnp.tile(b.reshape(1, -1), (1, KV_HEADS))
    const = lambda shape: pl.BlockSpec(shape, lambda b: (0,) * len(shape))
    xs = pl.BlockSpec((1, n, half), lambda b: (b, 0, 0))
    hw = KV_HEADS * NSA_CMP_HIDDEN
    out = pl.BlockSpec((1, n, KV_HEADS * hd), lambda b: (b, 0, 0))
    expanded = pltpu.VMEM((2 * half, hw), BF16)
    return pl.pallas_call(
        _cmp_kernel,
        grid=(bsz,),
        in_specs=[xs, xs, const(kw1.shape), const(vw1.shape),
                  const((1, 2 * half)), const((1, 2 * half)), const((1, hw)), const((1, hw)),
                  const((NSA_CMP_HIDDEN, hd)), const((NSA_CMP_HIDDEN, hd)), const((1, hd)), const((1, hd)),
                  const((1, hd))],
        out_specs=[out, out],
        out_shape=[jax.ShapeDtypeStruct((bsz, n, KV_HEADS * hd), F32)] * 2,
        scratch_shapes=[expanded, expanded],
        name="nsa_compress",
    )(xk, xv, kw1.astype(BF16), vw1.astype(BF16), tile_pos(pos_k), tile_pos(pos_v), tile_b(kb1), tile_b(vb1),
      kw2.astype(BF16), vw2.astype(BF16), kb2.reshape(1, hd), vb2.reshape(1, hd), gn.reshape(1, hd))


def _group_query(q_t, alibi_ref, g, gq_col):
    hd = HEAD_DIM
    heads = range(Q_PER_KV * g, Q_PER_KV * (g + 1))
    qn = jnp.concatenate([_norm_query_t(q_t[h * hd:(h + 1) * hd], gq_col) for h in heads], axis=1)
    alibi = jnp.concatenate([alibi_ref[h] for h in heads], axis=1)
    return qn, alibi


def _augment_keys(k_ref, gk_ref, dst_ref, sel_shift=None):
    assert KV_HEADS == 2
    pos = _iota(k_ref.shape[1:], 0)
    dst_ref[0], dst_ref[1] = _augment_pair_keys(_pair_rms(k_ref[0], gk_ref[...]), pos, sel_shift)


def _chunk_scores(ka_ref, vt_ref, g, c, qa, tq):
    off = pl.multiple_of(c * tq, tq)
    return _dot(ka_ref[g, pl.ds(off, tq), :], qa), vt_ref[c, g * HEAD_DIM:(g + 1) * HEAD_DIM, :]


def _window_first_chunk(ka_ref, vt_ref, g, far, qa, tq, edge_cap):
    sc, v_t = _chunk_scores(ka_ref, vt_ref, g, jnp.maximum(far, 0), qa, tq)
    return jnp.minimum(sc, jnp.minimum(edge_cap, jnp.where(far >= 0, BIG, -BIG))), v_t


def _regroup_heads(o_t, tq):
    return jnp.concatenate([o_t[:, k * tq:(k + 1) * tq] for k in range(Q_PER_KV)], axis=0)


def _nsa_kernel(alibi_ref, q_ref, kc_ref, vc_ref, ks_ref, vs_ref, kw_ref, vw_ref, gate_ref, z_ref,
                gq_ref, gks_ref, gkw_ref, o_ref, kca_s, vct_s, ksa_s, vst_s, kwa_s, vwt_s, m_s, acc_s,
                sc0_s, sc1_s):
    i = pl.program_id(1)
    tq = q_ref.shape[1]
    s = ks_ref.shape[1]
    hd = HEAD_DIM
    ncmp = kc_ref.shape[1]
    nsb = s // NSA_SEL_BLOCK
    n4 = Q_PER_KV * tq

    @pl.when(i == 0)
    def _():
        _augment_keys(ks_ref, gks_ref, ksa_s, NSA_SEL_BLOCK.bit_length() - 1)
        _augment_keys(kw_ref, gkw_ref, kwa_s)
        _store_chunked_t(vs_ref[0], vst_s, tq)
        _store_chunked_t(vw_ref[0], vwt_s, tq)
        cend = _iota((ncmp, LANES), 0) * NSA_CMP_STRIDE + (NSA_CMP_LEN - 1)
        kca_s[0], kca_s[1] = _augment_pair_keys(kc_ref[0], cend)
        vct_s[...] = vc_ref[0].T.astype(BF16)

    t0 = i * tq
    q_t = q_ref[0].T
    gates_t = _sigmoid(gate_ref[0]).T
    diag_cap = _causal_cap(tq, tq, Q_PER_KV)
    edge_cap = _window_edge_cap(tq, tq, Q_PER_KV)
    cend =_iota((ncmp, tq), 0) * NSA_CMP_STRIDE + (NSA_CMP_LEN - 1)
    seen = _tile_lanes(cend <= t0 + _iota((ncmp, tq), 1), Q_PER_KV)
    cstart = _iota((nsb, ncmp), 1) * NSA_CMP_STRIDE
    bstart = _iota((nsb, ncmp), 0) * NSA_SEL_BLOCK
    overlap_t = jnp.where((cstart < bstart + NSA_SEL_BLOCK) & (cstart + NSA_CMP_LEN > bstart), 1.0, 0.0)
    blk = _iota((nsb, tq), 0)
    cur = (t0 + _iota((nsb, tq), 1)) >> (NSA_SEL_BLOCK.bit_length() - 1)
    forced = (blk == 0) | (blk == cur) | (blk == cur - 1)
    groups = range(KV_HEADS)
    far = i - NSA_WINDOW // tq
    lo = jnp.maximum(far + 1, 0)

    state = (m_s, acc_s)
    sel_slots = tuple(groups)
    win_slots = tuple(KV_HEADS + g for g in groups)
    _flash_reset(state)
    queries = [_group_query(q_t, alibi_ref, g, gq_ref[...]) for g in groups]
    qa_win = [_augment_query(qn, alibi, flipped=g == 1) for g, (qn, alibi) in enumerate(queries)]
    qa_sel = []

    def scores(slot, c):
        g = slot % KV_HEADS
        if slot in sel_slots:
            return _chunk_scores(ksa_s, vst_s, g, c, qa_sel[g], tq)[0]
        return _chunk_scores(kwa_s, vwt_s, g, c, qa_win[g], tq)[0]

    def values(slot, c):
        g = slot % KV_HEADS
        return (vst_s if slot in sel_slots else vwt_s)[c, g * hd:(g + 1) * hd, :]

    first = [_window_first_chunk(kwa_s, vwt_s, g, far, qa_win[g], tq, edge_cap) for g in groups]
    cmp_sc = [_dot(kca_s[g], qa_win[g]) for g in groups]
    sc_win_lo = [scores(slot, lo) for slot in win_slots]
    _flash_update_all(state, win_slots, first)
    o_cmp = []
    for g in groups:
        qn, alibi = queries[g]
        sc = jnp.where(seen, cmp_sc[g], -BIG)
        e = jnp.where(seen, jnp.exp2(sc - jnp.max(sc, axis=0, keepdims=True)), 0.0)
        p_c = e / jnp.maximum(jnp.sum(e, axis=0, keepdims=True), TINY)
        o_cmp.append(_dot(vct_s[g * hd:(g + 1) * hd, :], p_c.astype(BF16)))

        psum = p_c[:, 0:tq]
        for k in range(1, Q_PER_KV):
            psum = psum + p_c[:, k * tq:(k + 1) * tq]
        imp = _dot_fine(overlap_t, psum, a_is_bf16_exact=True)
        imp = jnp.where(blk <= cur, jnp.where(forced, NSA_FORCE, imp), NEG)
        chosen = _rank_select(imp, blk, nsb, min(NSA_TOPN, nsb))
        sel_bias = _tile_lanes(jnp.where(chosen, 0.0, -BIG), Q_PER_KV)
        qa_sel.append(_augment_query(qn, alibi, sel_bias, flipped=g == 1))

    bufs = (sc0_s, sc1_s)
    for slot in sel_slots:
        sc0_s[slot] = scores(slot, 0)
    for slot, sc in zip(win_slots, sc_win_lo):
        sc0_s[slot] = sc
        sc1_s[slot] = sc

    def body(slots, c, carry):
        _flash_update_pipelined(state, bufs, slots, c, lambda slot: scores(slot, c + 1),
                                lambda slot: values(slot, c))
        return carry

    both_slots = sel_slots + win_slots
    lax.fori_loop(0, lo, functools.partial(body, sel_slots), 0)
    lax.fori_loop(lo, i, functools.partial(body, both_slots), 0)
    _flash_update_pipelined(state, bufs, both_slots, i, None, lambda slot: values(slot, i), diag_cap)

    outs = []
    for g in groups:
        def gate_row(branch):
            rows = [gates_t[branch * N_HEADS + h:branch * N_HEADS + h + 1, :]
                    for h in range(Q_PER_KV * g, Q_PER_KV * (g + 1))]
            return jnp.concatenate(rows, axis=1)

        o_slc = _flash_result(state, sel_slots[g])
        o_win = _flash_result(state, win_slots[g])
        o = gate_row(0) * o_cmp[g] + gate_row(1) * o_slc + gate_row(2) * o_win
        outs.append(_regroup_heads(o, tq))
    y = jnp.concatenate(outs, axis=0).T * _silu(z_ref[0])
    o_ref[0] = y.astype(o_ref.dtype)


def _nsa(proj, kcn, vc, gq, gks, gkw, *, tq=256):
    bsz, s, _ = proj.shape
    hd = HEAD_DIM
    gw = GROUP_WIDTH
    ncmp = kcn.shape[1]
    assert SEL_FEAT0 + s // NSA_SEL_BLOCK <= hd and ncmp == LANES
    wide = lambda c0: pl.BlockSpec((1, tq, gw), lambda b, i: (b, i, c0 // 4))
    full = lambda c0: pl.BlockSpec((1, s, LANES), lambda b, i: (b, 0, c0))
    cmp = pl.BlockSpec((1, ncmp, LANES), lambda b, i: (b, 0, 0))
    row = pl.BlockSpec((1, hd), lambda b, i: (0, 0))
    ka = pltpu.VMEM((KV_HEADS, s, LANES), BF16)
    vt = pltpu.VMEM((s // tq, LANES, tq), BF16)
    return pl.pallas_call(
        _nsa_kernel,
        grid=(bsz, s // tq),
        in_specs=[pl.BlockSpec((N_HEADS, 2 * SLOPE_PIECES, tq), lambda b, i: (0, 0, 0)),
                  wide(ODD_QC), cmp, cmp, full(ODD_KS), full(ODD_VS), full(ODD_KW), full(ODD_VW),
                  pl.BlockSpec((1, tq, LANES), lambda b, i: (b, i, ODD_GC)), wide(ODD_ZC),
                  pl.BlockSpec((hd, 1), lambda b, i: (0, 0)), row, row],
        out_specs=pl.BlockSpec((1, tq, gw), lambda b, i: (b, i, 0)),
        out_shape=jax.ShapeDtypeStruct((bsz, s, gw), BF16),
        scratch_shapes=[pltpu.VMEM((KV_HEADS, ncmp, LANES), BF16), pltpu.VMEM((LANES, ncmp), BF16),
                        ka, vt, ka, vt] + _flash_scratch(2 * KV_HEADS, Q_PER_KV * tq)
                       + [pltpu.VMEM((2 * KV_HEADS, tq, Q_PER_KV * tq), F32)] * 2,
        name="nsa",
    )(jnp.asarray(_alibi_query_features(tq)), proj, kcn, vc, proj, proj, proj, proj, proj, proj,
      gq.reshape(hd, 1), gks.reshape(1, hd), gkw.reshape(1, hd))


def _swa_kernel(sinks_ref, alibi_ref, q_ref, k_ref, v_ref, z_ref, gq_ref, gk_ref, o_ref, ka_s, vt_s):
    step = pl.program_id(1)
    tq = SWA_WINDOW
    tiles = q_ref.shape[1] // tq
    s = k_ref.shape[1]

    @pl.when(step == 0)
    def _():
        _augment_keys(k_ref, gk_ref, ka_s)
        _store_chunked_t(v_ref[0], vt_s, tq)

    q_t = q_ref[0].T
    diag_cap = _causal_cap(tq, tq, Q_PER_KV)
    edge_cap = _window_edge_cap(tq, tq, Q_PER_KV)
    chains = [(u, g) for u in range(tiles) for g in range(KV_HEADS)]
    qas = [_augment_query(*_group_query(q_t[:, u * tq:(u + 1) * tq], alibi_ref, g, gq_ref[...]), flipped=g == 1)
           for u, g in chains]
    first = [_window_first_chunk(ka_s, vt_s, g, step * tiles + u - 1, qa, tq, edge_cap)
             for (u, g), qa in zip(chains, qas)]
    own = [_chunk_scores(ka_s, vt_s, g, step * tiles + u, qa, tq) for (u, g), qa in zip(chains, qas)]
    carries = _flash_steps([_flash_init(Q_PER_KV * tq) for _ in chains], first)
    carries = _flash_steps(carries, own, diag_cap)
    outs = []
    for (u, g), carry in zip(chains, carries):
        m, acc, l = _flash_split(carry)
        tpos = ((step * tiles + u) * tq + _iota((1, tq), 1)).astype(F32)
        sink = jnp.concatenate([sinks_ref[h] * LOG2E + (SLOPES[h] * LOG2E) * tpos
                                for h in range(Q_PER_KV * g, Q_PER_KV * (g + 1))], axis=1)
        mf = jnp.maximum(m, sink)
        alpha = jnp.exp2(m - mf)
        o = acc * alpha / jnp.maximum(l * alpha + jnp.exp2(sink - mf), TINY)
        outs.append(_regroup_heads(o, tq))
    o_t = jnp.concatenate([jnp.concatenate(outs[u * KV_HEADS:(u + 1) * KV_HEADS], axis=0) for u in range(tiles)],
                          axis=1)
    y = o_t.T * _silu(z_ref[0])
    o_ref[0] = y.astype(o_ref.dtype)


def _swa(proj, sinks, gq, gk, *, tiles_per_step=4):
    bsz, s, _ = proj.shape
    hd = HEAD_DIM
    gw = GROUP_WIDTH
    tq = SWA_WINDOW
    rows = tiles_per_step * tq
    wide = lambda c0: pl.BlockSpec((1, rows, gw), lambda b, i: (b, i, c0 // 4))
    full = lambda c0: pl.BlockSpec((1, s, LANES), lambda b, i: (b, 0, c0))
    return pl.pallas_call(
        _swa_kernel,
        grid=(bsz, s // rows),
        in_specs=[pl.BlockSpec(memory_space=pltpu.SMEM),
                  pl.BlockSpec((N_HEADS, 2 * SLOPE_PIECES, tq), lambda b, i: (0, 0, 0)),
                  wide(ODD_QD), full(ODD_KD), full(ODD_VD), wide(ODD_ZD),
                  pl.BlockSpec((hd, 1), lambda b, i: (0, 0)), pl.BlockSpec((1, hd), lambda b, i: (0, 0))],
        out_specs=pl.BlockSpec((1, rows, gw), lambda b, i: (b, i, 0)),
        out_shape=jax.ShapeDtypeStruct((bsz, s, gw), BF16),
        scratch_shapes=[pltpu.VMEM((KV_HEADS, s, LANES), BF16), pltpu.VMEM((s // tq, LANES, tq), BF16)],
        name="swa",
    )(sinks.astype(F32), jnp.asarray(_alibi_query_features(tq)), proj, proj, proj, proj,
      gq.reshape(hd, 1), gk.reshape(1, hd))


def _pack_odd_weight(w):
    gw, kw, ng = GROUP_WIDTH, KV_HEADS * HEAD_DIM, 3 * N_HEADS
    sizes = [gw, kw, kw, kw, kw, kw, kw, ng, gw, gw, kw, kw, gw]
    starts = np.concatenate([[0], np.cumsum(sizes)]).tolist()
    part = lambda k: w[:, starts[k]:starts[k + 1]]
    pad = jnp.zeros((w.shape[0], LANES - ng), w.dtype)
    order = [0, 8, 9, 12, 1, 2, 3, 4, 5, 6, 10, 11, 7]
    return jnp.concatenate([part(k) for k in order] + [pad], axis=1).astype(BF16)


def kernel(x, norm_g, w_out, e_w_in, a_conv_w, a_conv_b, a_ln_g, a_ln_b, b_qnorm_g, b_knorm_g, o_w_in, c_qnorm_g, c_knorm_cmp_g, c_knorm_slc_g, c_knorm_win_g, c_pos_k, c_pos_v, c_k_w1, c_k_b1, c_k_w2, c_k_b2, c_v_w1, c_v_b1, c_v_w2, c_v_b2, d_qnorm_g, d_knorm_g, d_sinks):
    bsz, s, d = x.shape
    m = bsz * s
    assert s % MOBA_BLOCK == 0 and d == 2 * GROUP_WIDTH
    x2 = x.reshape(m, d)

    (proj,) = _inproj(x2, norm_g[0], e_w_in[0].astype(BF16), chunk=GROUP_WIDTH)
    proj = proj.reshape(bsz, s, -1)
    y_a = _conv_module(proj, a_conv_w[0], a_conv_b[0], a_ln_g[0], a_ln_b[0])
    y_b = _moba(proj, b_qnorm_g[0], b_knorm_g[0])
    x2 = _outproj(x2, y_a.reshape(m, -1), y_b.reshape(m, -1), w_out[0].astype(BF16))

    proj, k_cmp, v_cmp = _inproj(x2, norm_g[1], _pack_odd_weight(o_w_in[0]), chunk=5 * LANES,
                                 copies=((ODD_KC * LANES, LANES), (ODD_VC * LANES, LANES)), group=NSA_CMP_STRIDE)
    proj = proj.reshape(bsz, s, ODD_COLS)
    rows16 = lambda t: t.reshape(bsz, s // NSA_CMP_STRIDE, NSA_CMP_STRIDE * LANES)
    kcn, vc = _compress(rows16(k_cmp), rows16(v_cmp), c_pos_k[0], c_pos_v[0], c_k_w1[0], c_k_b1[0],
                        c_k_w2[0], c_k_b2[0], c_v_w1[0], c_v_b1[0], c_v_w2[0], c_v_b2[0], c_knorm_cmp_g[0])
    y_c = _nsa(proj, kcn, vc, c_qnorm_g[0], c_knorm_slc_g[0], c_knorm_win_g[0])
    y_d = _swa(proj, d_sinks[0], d_qnorm_g[0], d_knorm_g[0])
    x2 = _outproj(x2, y_c.reshape(m, -1), y_d.reshape(m, -1), w_out[1].astype(BF16))
    return x2.reshape(bsz, s, d)
```

```python
import functools

import ml_dtypes
import numpy as np
import jax
import jax.numpy as jnp
from jax import lax
from jax.experimental import pallas as pl
from jax.experimental.pallas import tpu as pltpu

HEAD_DIM = 64
N_HEADS = 8
GROUP_WIDTH = N_HEADS * HEAD_DIM
CONV_WIDTH = 31
MOBA_BLOCK = 256
MOBA_TOPK = 3
KV_HEADS = 2
Q_PER_KV = N_HEADS // KV_HEADS
NSA_CMP_LEN = 32
NSA_CMP_STRIDE = 16
NSA_CMP_HIDDEN = 256
NSA_SEL_BLOCK = 64
NSA_TOPN = 16
NSA_WINDOW = 512
NSA_FORCE = 1e4
SWA_WINDOW = 128
EPS = 1e-6
NEG = -1e30
TINY = 1e-30
LANES = 128
SUBLANES = 8
CONV_HALO = 32

LOG2E = float(np.log2(np.e))
BIG = 2.0 ** 99
M_INIT = -1e38
POS_SPLIT_SHIFT = 8
SLOPE_PIECES = 4
SEL_FEAT0 = 2 * SLOPE_PIECES

F32 = jnp.float32
BF16 = jnp.bfloat16

ODD_QC, ODD_ZC, ODD_QD, ODD_ZD = 0, 4, 8, 12
ODD_KC, ODD_VC, ODD_KS, ODD_VS, ODD_KW, ODD_VW, ODD_KD, ODD_VD, ODD_GC = 16, 17, 18, 19, 20, 21, 22, 23, 24
ODD_COLS = 25 * LANES


def _alibi_slopes(n):
    return [float(2.0 ** (-8.0 * (i + 1) / n)) for i in range(n)]


SLOPES = _alibi_slopes(N_HEADS)


def _alibi_query_features(width):
    table = np.zeros((N_HEADS, 2 * SLOPE_PIECES, width), np.float32)
    for h, slope in enumerate(SLOPES):
        rest = np.float64(slope) * LOG2E
        for k in range(SLOPE_PIECES):
            piece = float(np.float32(rest).astype(ml_dtypes.bfloat16).astype(np.float32))
            table[h, k] = piece
            table[h, SLOPE_PIECES + k] = piece
            rest -= piece
    return table


def _dot(a, b):
    return jnp.dot(a, b, preferred_element_type=F32)


def _split_bf16(x, pieces):
    out = []
    for _ in range(pieces):
        out.append(x.astype(BF16))
        x = x - out[-1].astype(F32)
    return out


def _dot_fine(a, b, a_is_bf16_exact=False):
    if a_is_bf16_exact:
        return sum(_dot(a.astype(BF16), piece) for piece in _split_bf16(b, 3))
    (a_hi, a_lo), (b_hi, b_lo) = _split_bf16(a, 2), _split_bf16(b, 2)
    return _dot(a_hi, b_hi) + (_dot(a_hi, b_lo) + _dot(a_lo, b_hi))


def _sigmoid(x):
    return 1.0 / (1.0 + jnp.exp(-x))


def _silu(x):
    return x * _sigmoid(x)


def _rms(x, g):
    return x * lax.rsqrt(jnp.mean(x * x, axis=-1, keepdims=True) + EPS) * g


def _iota(shape, dim):
    return lax.broadcasted_iota(jnp.int32, shape, dim)


def _inproj_kernel(x_ref, g_ref, w_ref, o_ref, *rest, chunk, copies, group):
    copy_refs, stage_refs = rest[:len(copies)], rest[len(copies):]
    h = _rms(x_ref[...], g_ref[...]).astype(BF16)
    for c in range(o_ref.shape[1] // chunk):
        o_ref[:, c * chunk:(c + 1) * chunk] = _dot(h, w_ref[:, c * chunk:(c + 1) * chunk])
    for ref, stage, (start, width) in zip(copy_refs, stage_refs, copies):
        stage[...] = o_ref[:, start:start + width]
        for t in range(group):
            ref[:, t * width:(t + 1) * width] = stage[pl.ds(t, ref.shape[0], stride=group), :]


def _inproj(x2d, g, w, *, tm=512, chunk, copies=(), group=1):
    m, d = x2d.shape
    e = w.shape[1]
    rows = lambda n, width: pl.BlockSpec((n, width), lambda i: (i, 0))
    const = lambda shape: pl.BlockSpec(shape, lambda i: (0, 0))
    return pl.pallas_call(
        functools.partial(_inproj_kernel, chunk=chunk, copies=tuple(copies), group=group),
        grid=(m // tm,),
        in_specs=[rows(tm, d), const((1, d)), const((d, e))],
        out_specs=[rows(tm, e)] + [rows(tm // group, group * width) for _, width in copies],
        out_shape=[jax.ShapeDtypeStruct((m, e), F32)]
                  + [jax.ShapeDtypeStruct((m // group, group * width), F32) for _, width in copies],
        scratch_shapes=[pltpu.VMEM((tm, width), F32) for _, width in copies],
        name="inproj",
    )(x2d, g.reshape(1, d), w)


def _inproj_moba_kernel(x_ref, g_ref, w_ref, gq_ref, gk_ref, o_ref, qt_ref, ka_ref, vt_ref, km_ref, *, chunk, seq):
    tm = x_ref.shape[0]
    hd, gw = HEAD_DIM, GROUP_WIDTH
    h = _rms(x_ref[...], g_ref[...]).astype(BF16)
    for c in range(o_ref.shape[1] // chunk):
        o_ref[:, c * chunk:(c + 1) * chunk] = _dot(h, w_ref[:, c * chunk:(c + 1) * chunk])
    q0, k0, v0 = 3 * gw, 4 * gw, 5 * gw
    q_t = o_ref[:, q0:q0 + gw].T
    for j in range(N_HEADS):
        qt_ref[0, j * hd:(j + 1) * hd, :] = _norm_query_t(q_t[j * hd:(j + 1) * hd], gq_ref[...])
    pos = (pl.program_id(0) % (seq // tm)) * tm + _iota((tm, LANES), 0)
    for pair in range(N_HEADS // 2):
        kn = _pair_rms(o_ref[:, k0 + pair * LANES:k0 + (pair + 1) * LANES], gk_ref[...])
        ka_ref[0, 2 * pair], ka_ref[0, 2 * pair + 1] = _augment_pair_keys(kn, pos, MOBA_BLOCK.bit_length() - 1)
        km_ref[0, 0, :, pair * LANES:(pair + 1) * LANES] = jnp.mean(
            kn.reshape(tm // MOBA_BLOCK, MOBA_BLOCK, LANES), axis=1)
    v_t = o_ref[:, v0:v0 + gw].T.astype(BF16)
    for c in range(tm // MOBA_BLOCK):
        vt_ref[0, c] = v_t[:, c * MOBA_BLOCK:(c + 1) * MOBA_BLOCK]


def _inproj_moba(x2d, g, w, gq, gk, *, bsz, tm=512, chunk):
    m, d = x2d.shape
    e = w.shape[1]
    s = m // bsz
    per_b = s // tm
    hd, gw, nb = HEAD_DIM, GROUP_WIDTH, tm // MOBA_BLOCK
    assert s % tm == 0 and tm % MOBA_BLOCK == 0
    const = lambda shape: pl.BlockSpec(shape, lambda i: (0, 0))
    return pl.pallas_call(
        functools.partial(_inproj_moba_kernel, chunk=chunk, seq=s),
        grid=(m // tm,),
        in_specs=[pl.BlockSpec((tm, d), lambda i: (i, 0)), const((1, d)), const((d, e)), const((hd, 1)),
                  const((1, hd))],
        out_specs=[pl.BlockSpec((tm, e), lambda i: (i, 0)),
                   pl.BlockSpec((1, gw, tm), lambda i: (i // per_b, 0, i % per_b)),
                   pl.BlockSpec((1, N_HEADS, tm, LANES), lambda i: (i // per_b, 0, i % per_b, 0)),
                   pl.BlockSpec((1, nb, gw, MOBA_BLOCK), lambda i: (i // per_b, i % per_b, 0, 0)),
                   pl.BlockSpec((1, 1, nb, gw), lambda i: (i // per_b, i % per_b, 0, 0))],
        out_shape=[jax.ShapeDtypeStruct((m, e), F32),
                   jax.ShapeDtypeStruct((bsz, gw, s), F32),
                   jax.ShapeDtypeStruct((bsz, N_HEADS, s, LANES), BF16),
                   jax.ShapeDtypeStruct((bsz, s // MOBA_BLOCK, gw, MOBA_BLOCK), BF16),
                   jax.ShapeDtypeStruct((bsz, per_b, nb, gw), F32)],
        name="inproj_moba",
    )(x2d, g.reshape(1, d), w, gq.reshape(hd, 1), gk.reshape(1, hd))


def _outproj_kernel(x_ref, ya_ref, yb_ref, wa_ref, wb_ref, o_ref):
    o_ref[...] = x_ref[...] + _dot(ya_ref[...], wa_ref[...]) + _dot(yb_ref[...], wb_ref[...])


def _outproj(x2d, ya, yb, w, *, tm=1024):
    m, d = x2d.shape
    gw = ya.shape[1]
    return pl.pallas_call(
        _outproj_kernel,
        grid=(m // tm,),
        in_specs=[pl.BlockSpec((tm, d), lambda i: (i, 0)),
                  pl.BlockSpec((tm, gw), lambda i: (i, 0)),
                  pl.BlockSpec((tm, gw), lambda i: (i, 0)),
                  pl.BlockSpec((gw, d), lambda i: (0, 0)),
                  pl.BlockSpec((gw, d), lambda i: (0, 0))],
        out_specs=pl.BlockSpec((tm, d), lambda i: (i, 0)),
        out_shape=jax.ShapeDtypeStruct((m, d), F32),
        name="outproj",
    )(x2d, ya, yb, w[:gw], w[gw:])


def _conv_kernel(uv_ref, ug_ref, uvh_ref, ugh_ref, z_ref, w_ref, b_ref, lg_ref, lb_ref, o_ref, h_s, *, rows):
    i = pl.program_id(1)
    ts = uv_ref.shape[1]
    halo = uvh_ref[0] * _sigmoid(ugh_ref[0])
    h_s[0:CONV_HALO, :] = jnp.where(i > 0, halo, 0.0)
    h_s[CONV_HALO:, :] = uv_ref[0] * _sigmoid(ug_ref[0])
    base = CONV_HALO - (CONV_WIDTH - 1)
    for c in range(ts // rows):
        acc = None
        for b in range(SUBLANES):
            n = rows if b == 0 else rows + SUBLANES
            part = None
            for a in range((base + CONV_WIDTH - 1) // SUBLANES + 1):
                j = SUBLANES * a + b - base
                if 0 <= j < CONV_WIDTH:
                    term = w_ref[j:j + 1, :] * h_s[c * rows + SUBLANES * a:c * rows + SUBLANES * a + n, :]
                    part = term if part is None else part + term
            part = part[b:b + rows]
            acc = part if acc is None else acc + part
        y = acc + b_ref[...]
        mu = jnp.mean(y, axis=-1, keepdims=True)
        yc = y - mu
        y = yc * lax.rsqrt(jnp.mean(yc * yc, axis=-1, keepdims=True) + EPS) * lg_ref[...] + lb_ref[...]
        out = _silu(y) * _silu(z_ref[0, c * rows:(c + 1) * rows, :])
        o_ref[0, c * rows:(c + 1) * rows, :] = out.astype(o_ref.dtype)


def _conv_module(proj, w, b, lg, lb, *, ts=256, rows=128):
    bsz, s, _ = proj.shape
    gw = GROUP_WIDTH
    hb = ts // CONV_HALO
    cur = lambda col: pl.BlockSpec((1, ts, gw), lambda bi, i: (bi, i, col))
    halo = lambda col: pl.BlockSpec((1, CONV_HALO, gw), lambda bi, i: (bi, jnp.maximum(i * hb - 1, 0), col))
    vec = pl.BlockSpec((1, gw), lambda bi, i: (0, 0))
    return pl.pallas_call(
        functools.partial(_conv_kernel, rows=rows),
        grid=(bsz, s // ts),
        in_specs=[cur(0), cur(1), halo(0), halo(1), cur(2),
                  pl.BlockSpec((CONV_WIDTH, gw), lambda bi, i: (0, 0)), vec, vec, vec],
        out_specs=pl.BlockSpec((1, ts, gw), lambda bi, i: (bi, i, 0)),
        out_shape=jax.ShapeDtypeStruct((bsz, s, gw), BF16),
        scratch_shapes=[pltpu.VMEM((CONV_HALO + ts, gw), F32)],
        name="conv_module",
    )(proj, proj, proj, proj, proj, w, b.reshape(1, gw), lg.reshape(1, gw), lb.reshape(1, gw))


def _key_features(pos, flipped, sel_shift=None):
    col = _iota(pos.shape, 1) - (0 if flipped else HEAD_DIM)
    hi = (pos >> POS_SPLIT_SHIFT) << POS_SPLIT_SHIFT
    lo = pos & ((1 << POS_SPLIT_SHIFT) - 1)
    feat = jnp.where(col < SLOPE_PIECES, hi, jnp.where(col < 2 * SLOPE_PIECES, lo, 0))
    if sel_shift is not None:
        feat = jnp.where(col - SEL_FEAT0 == (pos >> sel_shift), 1, feat)
    return jnp.where((col >= 0) & (col < HEAD_DIM), feat, 0).astype(F32)


def _pair_rms(x, g):
    sq = x * x
    hi = sq.astype(BF16)
    lo = (sq - hi.astype(F32)).astype(BF16)
    same_head = (_iota((LANES, LANES), 0) >> 6) == (_iota((LANES, LANES), 1) >> 6)
    ones = jnp.where(same_head, 1.0, 0.0).astype(BF16)
    ss = _dot(hi, ones) + _dot(lo, ones)
    return x * lax.rsqrt(ss * (1.0 / HEAD_DIM) + EPS) * jnp.concatenate([g, g], axis=1)


def _augment_pair_keys(kn, pos, sel_shift=None):
    lane = _iota(kn.shape, 1)
    return [jnp.where(lane >= HEAD_DIM if flipped else lane < HEAD_DIM, kn,
                      _key_features(pos, flipped, sel_shift)).astype(BF16) for flipped in (False, True)]


def _norm_query_t(x, g_col):
    ss = jnp.mean(x * x, axis=0, keepdims=True)
    return x * lax.rsqrt(ss + EPS) * g_col * (HEAD_DIM ** -0.5 * LOG2E)


def _augment_query(q_t, alibi, sel_bias=None, flipped=False):
    n = q_t.shape[1]
    feats = [alibi]
    used = alibi.shape[0]
    if sel_bias is not None:
        feats.append(sel_bias)
        used += sel_bias.shape[0]
    feats.append(jnp.zeros((HEAD_DIM - used, n), F32))
    return jnp.concatenate(feats + [q_t] if flipped else [q_t] + feats, axis=0).astype(BF16)


ONES_ROWS = 16
ACC_ROWS = HEAD_DIM + ONES_ROWS


def _flash_init(n):
    return jnp.full((1, n), M_INIT, F32), jnp.zeros((ACC_ROWS, n), F32)


def _flash_step(carry, sc, v_t):
    m, acc = carry
    m_new = jnp.maximum(m, jnp.max(sc, axis=0, keepdims=True))
    p = jnp.exp2(sc - m_new).astype(BF16)
    v_ones = jnp.concatenate([v_t, jnp.ones((ONES_ROWS, v_t.shape[1]), BF16)], axis=0)
    return m_new, jnp.exp2(m - m_new) * acc + _dot(v_ones, p)


def _flash_split(carry):
    m, acc = carry
    return m, acc[:HEAD_DIM], acc[HEAD_DIM:HEAD_DIM + 1]


def _flash_steps(carries, chunks, cap=None):
    return tuple(_flash_step(carry, sc if cap is None else jnp.minimum(sc, cap), v_t)
                 for carry, (sc, v_t) in zip(carries, chunks))


def _flash_reset(state):
    m_ref, acc_ref = state
    m_ref[...] = jnp.full(m_ref.shape, M_INIT, F32)
    acc_ref[...] = jnp.zeros(acc_ref.shape, F32)


def _flash_update(state, slot, sc, v_t, cap=None):
    m_ref, acc_ref = state
    m_ref[slot], acc_ref[slot] = _flash_step((m_ref[slot], acc_ref[slot]),
                                             sc if cap is None else jnp.minimum(sc, cap), v_t)


def _flash_update_all(state, slots, chunks, cap=None):
    for slot, (sc, v_t) in zip(slots, chunks):
        _flash_update(state, slot, sc, v_t, cap)


def _flash_update_pipelined(state, bufs, slots, c, next_scores, values, cap=None):
    def run(src_ref, dst_ref):
        for slot in slots:
            fresh = None if next_scores is None else next_scores(slot)
            _flash_update(state, slot, src_ref[slot], values(slot), cap)
            if fresh is not None:
                dst_ref[slot] = fresh

    pl.when((c & 1) == 0)(lambda: run(bufs[0], bufs[1]))
    pl.when((c & 1) == 1)(lambda: run(bufs[1], bufs[0]))


def _flash_result(state, slot):
    m_ref, acc_ref = state
    m, out, den = _flash_split((m_ref[slot], acc_ref[slot]))
    return out / jnp.maximum(den, TINY)


def _flash_scratch(chains, n):
    return [pltpu.VMEM((chains, 1, n), F32), pltpu.VMEM((chains, ACC_ROWS, n), F32)]


def _rank_select(score, blk, limit, count):
    ranks = []
    for r0 in range(0, score.shape[0], SUBLANES):
        tile = score[r0:r0 + SUBLANES]
        tile_blk = r0 + _iota(tile.shape, 0)
        rank = jnp.zeros(tile.shape, jnp.int32)
        for mm in range(limit):
            gm = score[mm:mm + 1, :]
            if mm < r0:
                beats = gm >= tile
            elif mm >= r0 + SUBLANES:
                beats = gm > tile
            else:
                beats = (gm > tile) | ((gm == tile) & (mm < tile_blk))
            rank = rank + jnp.where(beats, 1, 0)
        ranks.append(rank)
    return jnp.concatenate(ranks, axis=0) < count


def _tile_lanes(x, reps):
    return jnp.concatenate([x] * reps, axis=1) if reps > 1 else x


def _causal_cap(tk, tq, reps):
    keep = _iota((tk, tq), 0) <= _iota((tk, tq), 1)
    return _tile_lanes(jnp.where(keep, BIG, -BIG), reps)


def _window_edge_cap(tk, tq, reps):
    keep = _iota((tk, tq), 0) > _iota((tk, tq), 1)
    return _tile_lanes(jnp.where(keep, BIG, -BIG), reps)


def _store_chunked_t(x, dst_ref, tk):
    x_t = x.T.astype(BF16)
    for c in range(x.shape[0] // tk):
        dst_ref[c] = x_t[:, c * tk:(c + 1) * tk]


def _moba_kernel(alibi_ref, qt_ref, ka_ref, vt_ref, km_ref, z_ref, o_ref, m_s, acc_s, sc0_s, sc1_s):
    i = pl.program_id(1)
    nblk = vt_ref.shape[1]
    tq = MOBA_BLOCK
    hd = HEAD_DIM
    heads = range(N_HEADS)

    blk = _iota((nblk, tq), 0)
    qas = []
    for j in heads:
        qn = qt_ref[0, j * hd:(j + 1) * hd, :]
        km = jnp.concatenate([km_ref[0, r, :, j * hd:(j + 1) * hd] for r in range(km_ref.shape[1])], axis=0)
        gate = jnp.where(blk < i, _dot_fine(km, qn), NEG)
        chosen = _rank_select(gate, blk, nblk, MOBA_TOPK) & (blk < i)
        sel_bias = jnp.where(chosen | (blk == i), 0.0, -BIG)
        qas.append(_augment_query(qn, alibi_ref[j], sel_bias, flipped=j % 2 == 1))

    def scores(j, n):
        return _dot(ka_ref[0, j, pl.ds(pl.multiple_of(n * tq, tq), tq), :], qas[j])

    def values(j, n):
        return vt_ref[0, n, j * hd:(j + 1) * hd, :]

    state = (m_s, acc_s)
    _flash_reset(state)
    bufs = (sc0_s, sc1_s)
    for j in heads:
        sc0_s[j] = scores(j, 0)

    def body(n, carry):
        _flash_update_pipelined(state, bufs, heads, n, lambda j: scores(j, n + 1), lambda j: values(j, n))
        return carry

    lax.fori_loop(0, i, body, 0)
    diag_cap = _causal_cap(tq, tq, 1)
    _flash_update_pipelined(state, bufs, heads, i, None, lambda j: values(j, i), diag_cap)
    outs = [_flash_result(state, j) for j in heads]
    y = jnp.concatenate(outs, axis=0).T * _silu(z_ref[0])
    o_ref[0] = y.astype(o_ref.dtype)


def _moba(proj, q_t, ka, vt, km):
    bsz, s, _ = proj.shape
    tq = MOBA_BLOCK
    gw = GROUP_WIDTH
    nblk = s // MOBA_BLOCK
    assert SEL_FEAT0 + nblk <= HEAD_DIM
    whole = lambda a: pl.BlockSpec((1,) + a.shape[1:], lambda b, i: (b,) + (0,) * (a.ndim - 1))
    return pl.pallas_call(
        _moba_kernel,
        grid=(bsz, s // tq),
        in_specs=[pl.BlockSpec((N_HEADS, 2 * SLOPE_PIECES, tq), lambda b, i: (0, 0, 0)),
                  pl.BlockSpec((1, gw, tq), lambda b, i: (b, 0, i)), whole(ka), whole(vt), whole(km),
                  pl.BlockSpec((1, tq, gw), lambda b, i: (b, i, 6))],
        out_specs=pl.BlockSpec((1, tq, gw), lambda b, i: (b, i, 0)),
        out_shape=jax.ShapeDtypeStruct((bsz, s, gw), BF16),
        scratch_shapes=_flash_scratch(N_HEADS, tq) + [pltpu.VMEM((N_HEADS, tq, tq), F32)] * 2,
        name="moba",
    )(jnp.asarray(_alibi_query_features(tq)), q_t, ka, vt, km, proj)


def _cmp_kernel(xk_ref, xv_ref, wk1_ref, wv1_ref, pk_ref, pv_ref, bk1_ref, bv1_ref,
                wk2_ref, wv2_ref, bk2_ref, bv2_ref, gn_ref, ko_ref, vo_ref, wke_s, wve_s):
    half = xk_ref.shape[2]
    n = xk_ref.shape[1]
    hd = HEAD_DIM
    nh = NSA_CMP_HIDDEN

    @pl.when(pl.program_id(0) == 0)
    def _():
        for w_ref, we_ref in ((wk1_ref, wke_s), (wv1_ref, wve_s)):
            we_ref[...] = jnp.zeros(we_ref.shape, BF16)
            for tok in range(NSA_CMP_LEN):
                for g in range(KV_HEADS):
                    row = (tok * KV_HEADS + g) * hd
                    we_ref[row:row + hd, g * nh:(g + 1) * nh] = w_ref[tok * hd:(tok + 1) * hd, :]

    def hidden(x_ref, we_ref, pos_ref, b1_ref):
        x = x_ref[0].astype(BF16)
        first = _dot(x, we_ref[:half, :])
        second = _dot(x, we_ref[half:, :])
        pos = jnp.broadcast_to(pos_ref[...], (8, 2 * half)).astype(BF16)
        ph = _dot(pos[:, :half], we_ref[:half, :]) + _dot(pos[:, half:], we_ref[half:, :])
        return _silu(first + pltpu.roll(second, n - 1, 0) + ph[0:1, :] + b1_ref[...])

    hk = hidden(xk_ref, wke_s, pk_ref, bk1_ref).astype(BF16)
    hv = hidden(xv_ref, wve_s, pv_ref, bv1_ref).astype(BF16)
    nh = NSA_CMP_HIDDEN
    ks, vs = [], []
    for g in range(KV_HEADS):
        ks.append(_rms(_dot(hk[:, g * nh:(g + 1) * nh], wk2_ref[...]) + bk2_ref[...], gn_ref[...]))
        vs.append(_dot(hv[:, g * nh:(g + 1) * nh], wv2_ref[...]) + bv2_ref[...])
    ko_ref[0] = jnp.concatenate(ks, axis=1)
    vo_ref[0] = jnp.concatenate(vs, axis=1)


def _compress(xk, xv, pos_k, pos_v, kw1, kb1, kw2, kb2, vw1, vb1, vw2, vb2, gn):
    bsz, n, half = xk.shape
    hd = HEAD_DIM
    tile_pos = lambda p: jnp.tile(p[:, None, :], (1, KV_HEADS, 1)).reshape(1, 2 * half)
    tile_b = lambda b: jnp.tile(b.reshape(1, -1), (1, KV_HEADS))
    const = lambda shape: pl.BlockSpec(shape, lambda b: (0,) * len(shape))
    xs = pl.BlockSpec((1, n, half), lambda b: (b, 0, 0))
    hw = KV_HEADS * NSA_CMP_HIDDEN
    out = pl.BlockSpec((1, n, KV_HEADS * hd), lambda b: (b, 0, 0))
    expanded = pltpu.VMEM((2 * half, hw), BF16)
    return pl.pallas_call(
        _cmp_kernel,
        grid=(bsz,),
        in_specs=[xs, xs, const(kw1.shape), const(vw1.shape),
                  const((1, 2 * half)), const((1, 2 * half)), const((1, hw)), const((1, hw)),
                  const((NSA_CMP_HIDDEN, hd)), const((NSA_CMP_HIDDEN, hd)), const((1, hd)), const((1, hd)),
                  const((1, hd))],
        out_specs=[out, out],
        out_shape=[jax.ShapeDtypeStruct((bsz, n, KV_HEADS * hd), F32)] * 2,
        scratch_shapes=[expanded, expanded],
        name="nsa_compress",
    )(xk, xv, kw1.astype(BF16), vw1.astype(BF16), tile_pos(pos_k), tile_pos(pos_v), tile_b(kb1), tile_b(vb1),
      kw2.astype(BF16), vw2.astype(BF16), kb2.reshape(1, hd), vb2.reshape(1, hd), gn.reshape(1, hd))


def _group_query(q_t, alibi_ref, g, gq_col):
    hd = HEAD_DIM
    heads = range(Q_PER_KV * g, Q_PER_KV * (g + 1))
    qn = jnp.concatenate([_norm_query_t(q_t[h * hd:(h + 1) * hd], gq_col) for h in heads], axis=1)
    alibi = jnp.concatenate([alibi_ref[h] for h in heads], axis=1)
    return qn, alibi


def _augment_keys(k_ref, gk_ref, dst_ref, sel_shift=None):
    assert KV_HEADS == 2
    pos = _iota(k_ref.shape[1:], 0)
    dst_ref[0], dst_ref[1] = _augment_pair_keys(_pair_rms(k_ref[0], gk_ref[...]), pos, sel_shift)


def _chunk_scores(ka_ref, vt_ref, g, c, qa, tq):
    off = pl.multiple_of(c * tq, tq)
    return _dot(ka_ref[g, pl.ds(off, tq), :], qa), vt_ref[c, g * HEAD_DIM:(g + 1) * HEAD_DIM, :]


def _window_first_chunk(ka_ref, vt_ref, g, far, qa, tq, edge_cap):
    sc, v_t = _chunk_scores(ka_ref, vt_ref, g, jnp.maximum(far, 0), qa, tq)
    return jnp.minimum(sc, jnp.minimum(edge_cap, jnp.where(far >= 0, BIG, -BIG))), v_t


def _regroup_heads(o_t, tq):
    return jnp.concatenate([o_t[:, k * tq:(k + 1) * tq] for k in range(Q_PER_KV)], axis=0)


def _nsa_kernel(alibi_ref, q_ref, kc_ref, vc_ref, ks_ref, vs_ref, kw_ref, vw_ref, gate_ref, z_ref,
                gq_ref, gks_ref, gkw_ref, o_ref, kca_s, vct_s, ksa_s, vst_s, kwa_s, vwt_s, m_s, acc_s,
                sc0_s, sc1_s):
    i = pl.program_id(1)
    tq = q_ref.shape[1]
    s = ks_ref.shape[1]
    hd = HEAD_DIM
    ncmp = kc_ref.shape[1]
    nsb = s // NSA_SEL_BLOCK
    n4 = Q_PER_KV * tq

    @pl.when(i == 0)
    def _():
        _augment_keys(ks_ref, gks_ref, ksa_s, NSA_SEL_BLOCK.bit_length() - 1)
        _augment_keys(kw_ref, gkw_ref, kwa_s)
        _store_chunked_t(vs_ref[0], vst_s, tq)
        _store_chunked_t(vw_ref[0], vwt_s, tq)
        cend = _iota((ncmp, LANES), 0) * NSA_CMP_STRIDE + (NSA_CMP_LEN - 1)
        kca_s[0], kca_s[1] = _augment_pair_keys(kc_ref[0], cend)
        vct_s[...] = vc_ref[0].T.astype(BF16)

    t0 = i * tq
    q_t = q_ref[0].T
    gates_t = _sigmoid(gate_ref[0]).T
    diag_cap = _causal_cap(tq, tq, Q_PER_KV)
    edge_cap = _window_edge_cap(tq, tq, Q_PER_KV)
    cend =_iota((ncmp, tq), 0) * NSA_CMP_STRIDE + (NSA_CMP_LEN - 1)
    seen = _tile_lanes(cend <= t0 + _iota((ncmp, tq), 1), Q_PER_KV)
    cstart = _iota((nsb, ncmp), 1) * NSA_CMP_STRIDE
    bstart = _iota((nsb, ncmp), 0) * NSA_SEL_BLOCK
    overlap_t = jnp.where((cstart < bstart + NSA_SEL_BLOCK) & (cstart + NSA_CMP_LEN > bstart), 1.0, 0.0)
    blk = _iota((nsb, tq), 0)
    cur = (t0 + _iota((nsb, tq), 1)) >> (NSA_SEL_BLOCK.bit_length() - 1)
    forced = (blk == 0) | (blk == cur) | (blk == cur - 1)
    groups = range(KV_HEADS)
    far = i - NSA_WINDOW // tq
    lo = jnp.maximum(far + 1, 0)

    state = (m_s, acc_s)
    sel_slots = tuple(groups)
    win_slots = tuple(KV_HEADS + g for g in groups)
    _flash_reset(state)
    queries = [_group_query(q_t, alibi_ref, g, gq_ref[...]) for g in groups]
    qa_win = [_augment_query(qn, alibi, flipped=g == 1) for g, (qn, alibi) in enumerate(queries)]
    qa_sel = []

    def scores(slot, c):
        g = slot % KV_HEADS
        if slot in sel_slots:
            return _chunk_scores(ksa_s, vst_s, g, c, qa_sel[g], tq)[0]
        return _chunk_scores(kwa_s, vwt_s, g, c, qa_win[g], tq)[0]

    def values(slot, c):
        g = slot % KV_HEADS
        return (vst_s if slot in sel_slots else vwt_s)[c, g * hd:(g + 1) * hd, :]

    first = [_window_first_chunk(kwa_s, vwt_s, g, far, qa_win[g], tq, edge_cap) for g in groups]
    cmp_sc = [_dot(kca_s[g], qa_win[g]) for g in groups]
    sc_win_lo = [scores(slot, lo) for slot in win_slots]
    _flash_update_all(state, win_slots, first)
    o_cmp = []
    for g in groups:
        qn, alibi = queries[g]
        sc = jnp.where(seen, cmp_sc[g], -BIG)
        e = jnp.where(seen, jnp.exp2(sc - jnp.max(sc, axis=0, keepdims=True)), 0.0)
        p_c = e / jnp.maximum(jnp.sum(e, axis=0, keepdims=True), TINY)
        o_cmp.append(_dot(vct_s[g * hd:(g + 1) * hd, :], p_c.astype(BF16)))

        psum = p_c[:, 0:tq]
        for k in range(1, Q_PER_KV):
            psum = psum + p_c[:, k * tq:(k + 1) * tq]
        imp = _dot_fine(overlap_t, psum, a_is_bf16_exact=True)
        imp = jnp.where(blk <= cur, jnp.where(forced, NSA_FORCE, imp), NEG)
        chosen = _rank_select(imp, blk, nsb, min(NSA_TOPN, nsb))
        sel_bias = _tile_lanes(jnp.where(chosen, 0.0, -BIG), Q_PER_KV)
        qa_sel.append(_augment_query(qn, alibi, sel_bias, flipped=g == 1))

    bufs = (sc0_s, sc1_s)
    for slot in sel_slots:
        sc0_s[slot] = scores(slot, 0)
    for slot, sc in zip(win_slots, sc_win_lo):
        sc0_s[slot] = sc
        sc1_s[slot] = sc

    def body(slots, c, carry):
        _flash_update_pipelined(state, bufs, slots, c, lambda slot: scores(slot, c + 1),
                                lambda slot: values(slot, c))
        return carry

    both_slots = sel_slots + win_slots
    lax.fori_loop(0, lo, functools.partial(body, sel_slots), 0)
    lax.fori_loop(lo, i, functools.partial(body, both_slots), 0)
    _flash_update_pipelined(state, bufs, both_slots, i, None, lambda slot: values(slot, i), diag_cap)

    outs = []
    for g in groups:
        def gate_row(branch):
            rows = [gates_t[branch * N_HEADS + h:branch * N_HEADS + h + 1, :]
                    for h in range(Q_PER_KV * g, Q_PER_KV * (g + 1))]
            return jnp.concatenate(rows, axis=1)

        o_slc = _flash_result(state, sel_slots[g])
        o_win = _flash_result(state, win_slots[g])
        o = gate_row(0) * o_cmp[g] + gate_row(1) * o_slc + gate_row(2) * o_win
        outs.append(_regroup_heads(o, tq))
    y = jnp.concatenate(outs, axis=0).T * _silu(z_ref[0])
    o_ref[0] = y.astype(o_ref.dtype)


def _nsa(proj, kcn, vc, gq, gks, gkw, *, tq=256):
    bsz, s, _ = proj.shape
    hd = HEAD_DIM
    gw = GROUP_WIDTH
    ncmp = kcn.shape[1]
    assert SEL_FEAT0 + s // NSA_SEL_BLOCK <= hd and ncmp == LANES
    wide = lambda c0: pl.BlockSpec((1, tq, gw), lambda b, i: (b, i, c0 // 4))
    full = lambda c0: pl.BlockSpec((1, s, LANES), lambda b, i: (b, 0, c0))
    cmp = pl.BlockSpec((1, ncmp, LANES), lambda b, i: (b, 0, 0))
    row = pl.BlockSpec((1, hd), lambda b, i: (0, 0))
    ka = pltpu.VMEM((KV_HEADS, s, LANES), BF16)
    vt = pltpu.VMEM((s // tq, LANES, tq), BF16)
    return pl.pallas_call(
        _nsa_kernel,
        grid=(bsz, s // tq),
        in_specs=[pl.BlockSpec((N_HEADS, 2 * SLOPE_PIECES, tq), lambda b, i: (0, 0, 0)),
                  wide(ODD_QC), cmp, cmp, full(ODD_KS), full(ODD_VS), full(ODD_KW), full(ODD_VW),
                  pl.BlockSpec((1, tq, LANES), lambda b, i: (b, i, ODD_GC)), wide(ODD_ZC),
                  pl.BlockSpec((hd, 1), lambda b, i: (0, 0)), row, row],
        out_specs=pl.BlockSpec((1, tq, gw), lambda b, i: (b, i, 0)),
        out_shape=jax.ShapeDtypeStruct((bsz, s, gw), BF16),
        scratch_shapes=[pltpu.VMEM((KV_HEADS, ncmp, LANES), BF16), pltpu.VMEM((LANES, ncmp), BF16),
                        ka, vt, ka, vt] + _flash_scratch(2 * KV_HEADS, Q_PER_KV * tq)
                       + [pltpu.VMEM((2 * KV_HEADS, tq, Q_PER_KV * tq), F32)] * 2,
        name="nsa",
    )(jnp.asarray(_alibi_query_features(tq)), proj, kcn, vc, proj, proj, proj, proj, proj, proj,
      gq.reshape(hd, 1), gks.reshape(1, hd), gkw.reshape(1, hd))


def _swa_kernel(sinks_ref, alibi_ref, q_ref, k_ref, v_ref, z_ref, gq_ref, gk_ref, o_ref, ka_s, vt_s):
    step = pl.program_id(1)
    tq = SWA_WINDOW
    tiles = q_ref.shape[1] // tq
    s = k_ref.shape[1]

    @pl.when(step == 0)
    def _():
        _augment_keys(k_ref, gk_ref, ka_s)
        _store_chunked_t(v_ref[0], vt_s, tq)

    q_t = q_ref[0].T
    diag_cap = _causal_cap(tq, tq, Q_PER_KV)
    edge_cap = _window_edge_cap(tq, tq, Q_PER_KV)
    chains = [(u, g) for u in range(tiles) for g in range(KV_HEADS)]
    qas = [_augment_query(*_group_query(q_t[:, u * tq:(u + 1) * tq], alibi_ref, g, gq_ref[...]), flipped=g == 1)
           for u, g in chains]
    first = [_window_first_chunk(ka_s, vt_s, g, step * tiles + u - 1, qa, tq, edge_cap)
             for (u, g), qa in zip(chains, qas)]
    own = [_chunk_scores(ka_s, vt_s, g, step * tiles + u, qa, tq) for (u, g), qa in zip(chains, qas)]
    carries = _flash_steps([_flash_init(Q_PER_KV * tq) for _ in chains], first)
    carries = _flash_steps(carries, own, diag_cap)
    outs = []
    for (u, g), carry in zip(chains, carries):
        m, acc, l = _flash_split(carry)
        tpos = ((step * tiles + u) * tq + _iota((1, tq), 1)).astype(F32)
        sink = jnp.concatenate([sinks_ref[h] * LOG2E + (SLOPES[h] * LOG2E) * tpos
                                for h in range(Q_PER_KV * g, Q_PER_KV * (g + 1))], axis=1)
        mf = jnp.maximum(m, sink)
        alpha = jnp.exp2(m - mf)
        o = acc * alpha / jnp.maximum(l * alpha + jnp.exp2(sink - mf), TINY)
        outs.append(_regroup_heads(o, tq))
    o_t = jnp.concatenate([jnp.concatenate(outs[u * KV_HEADS:(u + 1) * KV_HEADS], axis=0) for u in range(tiles)],
                          axis=1)
    y = o_t.T * _silu(z_ref[0])
    o_ref[0] = y.astype(o_ref.dtype)


def _swa(proj, sinks, gq, gk, *, tiles_per_step=4):
    bsz, s, _ = proj.shape
    hd = HEAD_DIM
    gw = GROUP_WIDTH
    tq = SWA_WINDOW
    rows = tiles_per_step * tq
    wide = lambda c0: pl.BlockSpec((1, rows, gw), lambda b, i: (b, i, c0 // 4))
    full = lambda c0: pl.BlockSpec((1, s, LANES), lambda b, i: (b, 0, c0))
    return pl.pallas_call(
        _swa_kernel,
        grid=(bsz, s // rows),
        in_specs=[pl.BlockSpec(memory_space=pltpu.SMEM),
                  pl.BlockSpec((N_HEADS, 2 * SLOPE_PIECES, tq), lambda b, i: (0, 0, 0)),
                  wide(ODD_QD), full(ODD_KD), full(ODD_VD), wide(ODD_ZD),
                  pl.BlockSpec((hd, 1), lambda b, i: (0, 0)), pl.BlockSpec((1, hd), lambda b, i: (0, 0))],
        out_specs=pl.BlockSpec((1, rows, gw), lambda b, i: (b, i, 0)),
        out_shape=jax.ShapeDtypeStruct((bsz, s, gw), BF16),
        scratch_shapes=[pltpu.VMEM((KV_HEADS, s, LANES), BF16), pltpu.VMEM((s // tq, LANES, tq), BF16)],
        name="swa",
    )(sinks.astype(F32), jnp.asarray(_alibi_query_features(tq)), proj, proj, proj, proj,
      gq.reshape(hd, 1), gk.reshape(1, hd))


def _pack_odd_weight(w):
    gw, kw, ng = GROUP_WIDTH, KV_HEADS * HEAD_DIM, 3 * N_HEADS
    sizes = [gw, kw, kw, kw, kw, kw, kw, ng, gw, gw, kw, kw, gw]
    starts = np.concatenate([[0], np.cumsum(sizes)]).tolist()
    part = lambda k: w[:, starts[k]:starts[k + 1]]
    pad = jnp.zeros((w.shape[0], LANES - ng), w.dtype)
    order = [0, 8, 9, 12, 1, 2, 3, 4, 5, 6, 10, 11, 7]
    return jnp.concatenate([part(k) for k in order] + [pad], axis=1).astype(BF16)


def kernel(x, norm_g, w_out, e_w_in, a_conv_w, a_conv_b, a_ln_g, a_ln_b, b_qnorm_g, b_knorm_g, o_w_in, c_qnorm_g, c_knorm_cmp_g, c_knorm_slc_g, c_knorm_win_g, c_pos_k, c_pos_v, c_k_w1, c_k_b1, c_k_w2, c_k_b2, c_v_w1, c_v_b1, c_v_w2, c_v_b2, d_qnorm_g, d_knorm_g, d_sinks):
    bsz, s, d = x.shape
    m = bsz * s
    assert s % MOBA_BLOCK == 0 and d == 2 * GROUP_WIDTH
    x2 = x.reshape(m, d)

    proj, q_t, ka, vt, km = _inproj_moba(x2, norm_g[0], e_w_in[0].astype(BF16), b_qnorm_g[0], b_knorm_g[0],
                                         bsz=bsz, chunk=GROUP_WIDTH)
    proj = proj.reshape(bsz, s, -1)
    y_a = _conv_module(proj, a_conv_w[0], a_conv_b[0], a_ln_g[0], a_ln_b[0])
    y_b = _moba(proj, q_t, ka, vt, km)
    x2 = _outproj(x2, y_a.reshape(m, -1), y_b.reshape(m, -1), w_out[0].astype(BF16))

    proj, k_cmp, v_cmp = _inproj(x2, norm_g[1], _pack_odd_weight(o_w_in[0]), chunk=5 * LANES,
                                 copies=((ODD_KC * LANES, LANES), (ODD_VC * LANES, LANES)), group=NSA_CMP_STRIDE)
    proj = proj.reshape(bsz, s, ODD_COLS)
    rows16 = lambda t: t.reshape(bsz, s // NSA_CMP_STRIDE, NSA_CMP_STRIDE * LANES)
    kcn, vc = _compress(rows16(k_cmp), rows16(v_cmp), c_pos_k[0], c_pos_v[0], c_k_w1[0], c_k_b1[0],
                        c_k_w2[0], c_k_b2[0], c_v_w1[0], c_v_b1[0], c_v_w2[0], c_v_b2[0], c_knorm_cmp_g[0])
    y_c = _nsa(proj, kcn, vc, c_qnorm_g[0], c_knorm_slc_g[0], c_knorm_win_g[0])
    y_d = _swa(proj, d_sinks[0], d_qnorm_g[0], d_knorm_g[0])
    x2 = _outproj(x2, y_c.reshape(m, -1), y_d.reshape(m, -1), w_out[1].astype(BF16))
    return x2.reshape(bsz, s, d)
```

```python
import functools

import ml_dtypes
import numpy as np
import jax
import jax.numpy as jnp
from jax import lax
from jax.experimental import pallas as pl
from jax.experimental.pallas import tpu as pltpu

HEAD_DIM = 64
N_HEADS = 8
GROUP_WIDTH = N_HEADS * HEAD_DIM
CONV_WIDTH = 31
MOBA_BLOCK = 256
MOBA_TOPK = 3
KV_HEADS = 2
Q_PER_KV = N_HEADS // KV_HEADS
NSA_CMP_LEN = 32
NSA_CMP_STRIDE = 16
NSA_CMP_HIDDEN = 256
NSA_SEL_BLOCK = 64
NSA_TOPN = 16
NSA_WINDOW = 512
NSA_FORCE = 1e4
SWA_WINDOW = 128
EPS = 1e-6
NEG = -1e30
TINY = 1e-30
LANES = 128
SUBLANES = 8
CONV_HALO = 32

LOG2E = float(np.log2(np.e))
BIG = 2.0 ** 99
M_INIT = -1e38
POS_SPLIT_SHIFT = 8
SLOPE_PIECES = 4
SEL_FEAT0 = 2 * SLOPE_PIECES

F32 = jnp.float32
BF16 = jnp.bfloat16

ODD_QC, ODD_ZC, ODD_QD, ODD_ZD = 0, 4, 8, 12
ODD_KC, ODD_VC, ODD_KS, ODD_VS, ODD_KW, ODD_VW, ODD_KD, ODD_VD, ODD_GC = 16, 17, 18, 19, 20, 21, 22, 23, 24
ODD_COLS = 25 * LANES


def _alibi_slopes(n):
    return [float(2.0 ** (-8.0 * (i + 1) / n)) for i in range(n)]


SLOPES = _alibi_slopes(N_HEADS)


def _alibi_query_features(width):
    table = np.zeros((N_HEADS, 2 * SLOPE_PIECES, width), np.float32)
    for h, slope in enumerate(SLOPES):
        rest = np.float64(slope) * LOG2E
        for k in range(SLOPE_PIECES):
            piece = float(np.float32(rest).astype(ml_dtypes.bfloat16).astype(np.float32))
            table[h, k] = piece
            table[h, SLOPE_PIECES + k] = piece
            rest -= piece
    return table


def _dot(a, b):
    return jnp.dot(a, b, preferred_element_type=F32)


def _split_bf16(x, pieces):
    out = []
    for _ in range(pieces):
        out.append(x.astype(BF16))
        x = x - out[-1].astype(F32)
    return out


def _dot_fine(a, b, a_is_bf16_exact=False):
    if a_is_bf16_exact:
        return sum(_dot(a.astype(BF16), piece) for piece in _split_bf16(b, 3))
    (a_hi, a_lo), (b_hi, b_lo) = _split_bf16(a, 2), _split_bf16(b, 2)
    return _dot(a_hi, b_hi) + (_dot(a_hi, b_lo) + _dot(a_lo, b_hi))


def _sigmoid(x):
    return 1.0 / (1.0 + jnp.exp(-x))


def _silu(x):
    return x * _sigmoid(x)


def _rms(x, g):
    return x * lax.rsqrt(jnp.mean(x * x, axis=-1, keepdims=True) + EPS) * g


def _iota(shape, dim):
    return lax.broadcasted_iota(jnp.int32, shape, dim)


def _inproj_kernel(x_ref, g_ref, w_ref, o_ref, *rest, chunk, copies, group):
    copy_refs, stage_refs = rest[:len(copies)], rest[len(copies):]
    h = _rms(x_ref[...], g_ref[...]).astype(BF16)
    for c in range(o_ref.shape[1] // chunk):
        o_ref[:, c * chunk:(c + 1) * chunk] = _dot(h, w_ref[:, c * chunk:(c + 1) * chunk])
    for ref, stage, (start, width) in zip(copy_refs, stage_refs, copies):
        stage[...] = o_ref[:, start:start + width]
        for t in range(group):
            ref[:, t * width:(t + 1) * width] = stage[pl.ds(t, ref.shape[0], stride=group), :]


def _inproj(x2d, g, w, *, tm=512, chunk, copies=(), group=1):
    m, d = x2d.shape
    e = w.shape[1]
    rows = lambda n, width: pl.BlockSpec((n, width), lambda i: (i, 0))
    const = lambda shape: pl.BlockSpec(shape, lambda i: (0, 0))
    return pl.pallas_call(
        functools.partial(_inproj_kernel, chunk=chunk, copies=tuple(copies), group=group),
        grid=(m // tm,),
        in_specs=[rows(tm, d), const((1, d)), const((d, e))],
        out_specs=[rows(tm, e)] + [rows(tm // group, group * width) for _, width in copies],
        out_shape=[jax.ShapeDtypeStruct((m, e), F32)]
                  + [jax.ShapeDtypeStruct((m // group, group * width), F32) for _, width in copies],
        scratch_shapes=[pltpu.VMEM((tm, width), F32) for _, width in copies],
        name="inproj",
    )(x2d, g.reshape(1, d), w)


def _inproj_moba_kernel(x_ref, g_ref, w_ref, gq_ref, gk_ref, o_ref, qt_ref, ka_ref, vt_ref, km_ref, *, chunk, seq):
    tm = x_ref.shape[0]
    hd, gw = HEAD_DIM, GROUP_WIDTH
    h = _rms(x_ref[...], g_ref[...]).astype(BF16)
    for c in range(o_ref.shape[1] // chunk):
        o_ref[:, c * chunk:(c + 1) * chunk] = _dot(h, w_ref[:, c * chunk:(c + 1) * chunk])
    q0, k0, v0 = 3 * gw, 4 * gw, 5 * gw
    q_t = o_ref[:, q0:q0 + gw].T
    for j in range(N_HEADS):
        qt_ref[0, j * hd:(j + 1) * hd, :] = _norm_query_t(q_t[j * hd:(j + 1) * hd], gq_ref[...])
    pos = (pl.program_id(0) % (seq // tm)) * tm + _iota((tm, LANES), 0)
    for pair in range(N_HEADS // 2):
        kn = _pair_rms(o_ref[:, k0 + pair * LANES:k0 + (pair + 1) * LANES], gk_ref[...])
        ka_ref[0, 2 * pair], ka_ref[0, 2 * pair + 1] = _augment_pair_keys(kn, pos, MOBA_BLOCK.bit_length() - 1)
        km_ref[0, 0, :, pair * LANES:(pair + 1) * LANES] = jnp.mean(
            kn.reshape(tm // MOBA_BLOCK, MOBA_BLOCK, LANES), axis=1)
    v_t = o_ref[:, v0:v0 + gw].T.astype(BF16)
    for c in range(tm // MOBA_BLOCK):
        vt_ref[0, c] = v_t[:, c * MOBA_BLOCK:(c + 1) * MOBA_BLOCK]


def _inproj_moba(x2d, g, w, gq, gk, *, bsz, tm=512, chunk):
    m, d = x2d.shape
    e = w.shape[1]
    s = m // bsz
    per_b = s // tm
    hd, gw, nb = HEAD_DIM, GROUP_WIDTH, tm // MOBA_BLOCK
    assert s % tm == 0 and tm % MOBA_BLOCK == 0
    const = lambda shape: pl.BlockSpec(shape, lambda i: (0, 0))
    return pl.pallas_call(
        functools.partial(_inproj_moba_kernel, chunk=chunk, seq=s),
        grid=(m // tm,),
        in_specs=[pl.BlockSpec((tm, d), lambda i: (i, 0)), const((1, d)), const((d, e)), const((hd, 1)),
                  const((1, hd))],
        out_specs=[pl.BlockSpec((tm, e), lambda i: (i, 0)),
                   pl.BlockSpec((1, gw, tm), lambda i: (i // per_b, 0, i % per_b)),
                   pl.BlockSpec((1, N_HEADS, tm, LANES), lambda i: (i // per_b, 0, i % per_b, 0)),
                   pl.BlockSpec((1, nb, gw, MOBA_BLOCK), lambda i: (i // per_b, i % per_b, 0, 0)),
                   pl.BlockSpec((1, 1, nb, gw), lambda i: (i // per_b, i % per_b, 0, 0))],
        out_shape=[jax.ShapeDtypeStruct((m, e), F32),
                   jax.ShapeDtypeStruct((bsz, gw, s), F32),
                   jax.ShapeDtypeStruct((bsz, N_HEADS, s, LANES), BF16),
                   jax.ShapeDtypeStruct((bsz, s // MOBA_BLOCK, gw, MOBA_BLOCK), BF16),
                   jax.ShapeDtypeStruct((bsz, per_b, nb, gw), F32)],
        name="inproj_moba",
    )(x2d, g.reshape(1, d), w, gq.reshape(hd, 1), gk.reshape(1, hd))


def _outproj_kernel(x_ref, ya_ref, yb_ref, wa_ref, wb_ref, o_ref):
    o_ref[...] = x_ref[...] + _dot(ya_ref[...], wa_ref[...]) + _dot(yb_ref[...], wb_ref[...])


def _outproj(x2d, ya, yb, w, *, tm=1024):
    m, d = x2d.shape
    gw = ya.shape[1]
    return pl.pallas_call(
        _outproj_kernel,
        grid=(m // tm,),
        in_specs=[pl.BlockSpec((tm, d), lambda i: (i, 0)),
                  pl.BlockSpec((tm, gw), lambda i: (i, 0)),
                  pl.BlockSpec((tm, gw), lambda i: (i, 0)),
                  pl.BlockSpec((gw, d), lambda i: (0, 0)),
                  pl.BlockSpec((gw, d), lambda i: (0, 0))],
        out_specs=pl.BlockSpec((tm, d), lambda i: (i, 0)),
        out_shape=jax.ShapeDtypeStruct((m, d), F32),
        name="outproj",
    )(x2d, ya, yb, w[:gw], w[gw:])


def _conv_kernel(uv_ref, ug_ref, uvh_ref, ugh_ref, z_ref, w_ref, b_ref, lg_ref, lb_ref, o_ref, h_s, *, rows):
    i = pl.program_id(1)
    ts = uv_ref.shape[1]
    halo = uvh_ref[0] * _sigmoid(ugh_ref[0])
    h_s[0:CONV_HALO, :] = jnp.where(i > 0, halo, 0.0)
    h_s[CONV_HALO:, :] = uv_ref[0] * _sigmoid(ug_ref[0])
    base = CONV_HALO - (CONV_WIDTH - 1)
    for c in range(ts // rows):
        acc = None
        for b in range(SUBLANES):
            n = rows if b == 0 else rows + SUBLANES
            part = None
            for a in range((base + CONV_WIDTH - 1) // SUBLANES + 1):
                j = SUBLANES * a + b - base
                if 0 <= j < CONV_WIDTH:
                    term = w_ref[j:j + 1, :] * h_s[c * rows + SUBLANES * a:c * rows + SUBLANES * a + n, :]
                    part = term if part is None else part + term
            part = part[b:b + rows]
            acc = part if acc is None else acc + part
        y = acc + b_ref[...]
        mu = jnp.mean(y, axis=-1, keepdims=True)
        yc = y - mu
        y = yc * lax.rsqrt(jnp.mean(yc * yc, axis=-1, keepdims=True) + EPS) * lg_ref[...] + lb_ref[...]
        out = _silu(y) * _silu(z_ref[0, c * rows:(c + 1) * rows, :])
        o_ref[0, c * rows:(c + 1) * rows, :] = out.astype(o_ref.dtype)


def _conv_module(proj, w, b, lg, lb, *, ts=256, rows=128):
    bsz, s, _ = proj.shape
    gw = GROUP_WIDTH
    hb = ts // CONV_HALO
    cur = lambda col: pl.BlockSpec((1, ts, gw), lambda bi, i: (bi, i, col))
    halo = lambda col: pl.BlockSpec((1, CONV_HALO, gw), lambda bi, i: (bi, jnp.maximum(i * hb - 1, 0), col))
    vec = pl.BlockSpec((1, gw), lambda bi, i: (0, 0))
    return pl.pallas_call(
        functools.partial(_conv_kernel, rows=rows),
        grid=(bsz, s // ts),
        in_specs=[cur(0), cur(1), halo(0), halo(1), cur(2),
                  pl.BlockSpec((CONV_WIDTH, gw), lambda bi, i: (0, 0)), vec, vec, vec],
        out_specs=pl.BlockSpec((1, ts, gw), lambda bi, i: (bi, i, 0)),
        out_shape=jax.ShapeDtypeStruct((bsz, s, gw), BF16),
        scratch_shapes=[pltpu.VMEM((CONV_HALO + ts, gw), F32)],
        name="conv_module",
    )(proj, proj, proj, proj, proj, w, b.reshape(1, gw), lg.reshape(1, gw), lb.reshape(1, gw))


def _key_features(pos, flipped, sel_shift=None):
    col = _iota(pos.shape, 1) - (0 if flipped else HEAD_DIM)
    hi = (pos >> POS_SPLIT_SHIFT) << POS_SPLIT_SHIFT
    lo = pos & ((1 << POS_SPLIT_SHIFT) - 1)
    feat = jnp.where(col < SLOPE_PIECES, hi, jnp.where(col < 2 * SLOPE_PIECES, lo, 0))
    if sel_shift is not None:
        feat = jnp.where(col - SEL_FEAT0 == (pos >> sel_shift), 1, feat)
    return jnp.where((col >= 0) & (col < HEAD_DIM), feat, 0).astype(F32)


def _pair_rms(x, g):
    sq = x * x
    hi = sq.astype(BF16)
    lo = (sq - hi.astype(F32)).astype(BF16)
    same_head = (_iota((LANES, LANES), 0) >> 6) == (_iota((LANES, LANES), 1) >> 6)
    ones = jnp.where(same_head, 1.0, 0.0).astype(BF16)
    ss = _dot(hi, ones) + _dot(lo, ones)
    return x * lax.rsqrt(ss * (1.0 / HEAD_DIM) + EPS) * jnp.concatenate([g, g], axis=1)


def _augment_pair_keys(kn, pos, sel_shift=None):
    lane = _iota(kn.shape, 1)
    return [jnp.where(lane >= HEAD_DIM if flipped else lane < HEAD_DIM, kn,
                      _key_features(pos, flipped, sel_shift)).astype(BF16) for flipped in (False, True)]


def _norm_query_t(x, g_col):
    ss = jnp.mean(x * x, axis=0, keepdims=True)
    return x * lax.rsqrt(ss + EPS) * g_col * (HEAD_DIM ** -0.5 * LOG2E)


def _augment_query(q_t, alibi, sel_bias=None, flipped=False):
    n = q_t.shape[1]
    feats = [alibi]
    used = alibi.shape[0]
    if sel_bias is not None:
        feats.append(sel_bias)
        used += sel_bias.shape[0]
    feats.append(jnp.zeros((HEAD_DIM - used, n), F32))
    return jnp.concatenate(feats + [q_t] if flipped else [q_t] + feats, axis=0).astype(BF16)


ONES_ROWS = 16
ACC_ROWS = HEAD_DIM + ONES_ROWS


def _flash_init(n):
    return jnp.full((1, n), M_INIT, F32), jnp.zeros((ACC_ROWS, n), F32)


def _flash_step(carry, sc, v_t):
    m, acc = carry
    m_new = jnp.maximum(m, jnp.max(sc, axis=0, keepdims=True))
    p = jnp.exp2(sc - m_new).astype(BF16)
    v_ones = jnp.concatenate([v_t, jnp.ones((ONES_ROWS, v_t.shape[1]), BF16)], axis=0)
    return m_new, jnp.exp2(m - m_new) * acc + _dot(v_ones, p)


def _flash_split(carry):
    m, acc = carry
    return m, acc[:HEAD_DIM], acc[HEAD_DIM:HEAD_DIM + 1]


def _flash_steps(carries, chunks, cap=None):
    return tuple(_flash_step(carry, sc if cap is None else jnp.minimum(sc, cap), v_t)
                 for carry, (sc, v_t) in zip(carries, chunks))


def _flash_reset(state):
    m_ref, acc_ref = state
    m_ref[...] = jnp.full(m_ref.shape, M_INIT, F32)
    acc_ref[...] = jnp.zeros(acc_ref.shape, F32)


def _flash_update(state, slot, sc, v_t, cap=None):
    m_ref, acc_ref = state
    m_ref[slot], acc_ref[slot] = _flash_step((m_ref[slot], acc_ref[slot]),
                                             sc if cap is None else jnp.minimum(sc, cap), v_t)


def _flash_update_all(state, slots, chunks, cap=None):
    for slot, (sc, v_t) in zip(slots, chunks):
        _flash_update(state, slot, sc, v_t, cap)


def _flash_update_pipelined(state, bufs, slots, c, next_scores, values, cap=None, cap_slots=None):
    def run(src_ref, dst_ref):
        for slot in slots:
            fresh = None if next_scores is None else next_scores(slot)
            masked = cap is not None and (cap_slots is None or slot in cap_slots)
            _flash_update(state, slot, src_ref[slot], values(slot), cap if masked else None)
            if fresh is not None:
                dst_ref[slot] = fresh

    pl.when((c & 1) == 0)(lambda: run(bufs[0], bufs[1]))
    pl.when((c & 1) == 1)(lambda: run(bufs[1], bufs[0]))


def _flash_result(state, slot):
    m_ref, acc_ref = state
    m, out, den = _flash_split((m_ref[slot], acc_ref[slot]))
    return out / jnp.maximum(den, TINY)


def _flash_scratch(chains, n):
    return [pltpu.VMEM((chains, 1, n), F32), pltpu.VMEM((chains, ACC_ROWS, n), F32)]


def _rank_select(score, blk, limit, count):
    ranks = []
    for r0 in range(0, score.shape[0], SUBLANES):
        tile = score[r0:r0 + SUBLANES]
        tile_blk = r0 + _iota(tile.shape, 0)
        rank = jnp.zeros(tile.shape, jnp.int32)
        for mm in range(limit):
            gm = score[mm:mm + 1, :]
            if mm < r0:
                beats = gm >= tile
            elif mm >= r0 + SUBLANES:
                beats = gm > tile
            else:
                beats = (gm > tile) | ((gm == tile) & (mm < tile_blk))
            rank = rank + jnp.where(beats, 1, 0)
        ranks.append(rank)
    return jnp.concatenate(ranks, axis=0) < count


def _tile_lanes(x, reps):
    return jnp.concatenate([x] * reps, axis=1) if reps > 1 else x


def _causal_cap(tk, tq, reps):
    keep = _iota((tk, tq), 0) <= _iota((tk, tq), 1)
    return _tile_lanes(jnp.where(keep, BIG, -BIG), reps)


def _window_edge_cap(tk, tq, reps):
    keep = _iota((tk, tq), 0) > _iota((tk, tq), 1)
    return _tile_lanes(jnp.where(keep, BIG, -BIG), reps)


def _store_chunked_t(x, dst_ref, tk):
    x_t = x.T.astype(BF16)
    for c in range(x.shape[0] // tk):
        dst_ref[c] = x_t[:, c * tk:(c + 1) * tk]


def _moba_kernel(alibi_ref, qt_ref, ka_ref, vt_ref, km_ref, z_ref, o_ref, m_s, acc_s, sc0_s, sc1_s):
    i = pl.program_id(1)
    nblk = vt_ref.shape[1]
    tq = MOBA_BLOCK
    hd = HEAD_DIM
    heads = range(N_HEADS)

    blk = _iota((nblk, tq), 0)
    qas = []
    for j in heads:
        qn = qt_ref[0, j * hd:(j + 1) * hd, :]
        km = jnp.concatenate([km_ref[0, r, :, j * hd:(j + 1) * hd] for r in range(km_ref.shape[1])], axis=0)
        gate = jnp.where(blk < i, _dot_fine(km, qn), NEG)
        chosen = _rank_select(gate, blk, nblk, MOBA_TOPK) & (blk < i)
        sel_bias = jnp.where(chosen | (blk == i), 0.0, -BIG)
        qas.append(_augment_query(qn, alibi_ref[j], sel_bias, flipped=j % 2 == 1))

    def scores(j, n):
        return _dot(ka_ref[0, j, pl.ds(pl.multiple_of(n * tq, tq), tq), :], qas[j])

    def values(j, n):
        return vt_ref[0, n, j * hd:(j + 1) * hd, :]

    state = (m_s, acc_s)
    _flash_reset(state)
    bufs = (sc0_s, sc1_s)
    for j in heads:
        sc0_s[j] = scores(j, 0)

    def body(n, carry):
        _flash_update_pipelined(state, bufs, heads, n, lambda j: scores(j, n + 1), lambda j: values(j, n))
        return carry

    lax.fori_loop(0, i, body, 0)
    diag_cap = _causal_cap(tq, tq, 1)
    _flash_update_pipelined(state, bufs, heads, i, None, lambda j: values(j, i), diag_cap)
    outs = [_flash_result(state, j) for j in heads]
    y = jnp.concatenate(outs, axis=0).T * _silu(z_ref[0])
    o_ref[0] = y.astype(o_ref.dtype)


def _moba(proj, q_t, ka, vt, km):
    bsz, s, _ = proj.shape
    tq = MOBA_BLOCK
    gw = GROUP_WIDTH
    nblk = s // MOBA_BLOCK
    assert SEL_FEAT0 + nblk <= HEAD_DIM
    whole = lambda a: pl.BlockSpec((1,) + a.shape[1:], lambda b, i: (b,) + (0,) * (a.ndim - 1))
    return pl.pallas_call(
        _moba_kernel,
        grid=(bsz, s // tq),
        in_specs=[pl.BlockSpec((N_HEADS, 2 * SLOPE_PIECES, tq), lambda b, i: (0, 0, 0)),
                  pl.BlockSpec((1, gw, tq), lambda b, i: (b, 0, i)), whole(ka), whole(vt), whole(km),
                  pl.BlockSpec((1, tq, gw), lambda b, i: (b, i, 6))],
        out_specs=pl.BlockSpec((1, tq, gw), lambda b, i: (b, i, 0)),
        out_shape=jax.ShapeDtypeStruct((bsz, s, gw), BF16),
        scratch_shapes=_flash_scratch(N_HEADS, tq) + [pltpu.VMEM((N_HEADS, tq, tq), F32)] * 2,
        name="moba",
    )(jnp.asarray(_alibi_query_features(tq)), q_t, ka, vt, km, proj)


def _cmp_kernel(xk_ref, xv_ref, wk1_ref, wv1_ref, pk_ref, pv_ref, bk1_ref, bv1_ref,
                wk2_ref, wv2_ref, bk2_ref, bv2_ref, gn_ref, ko_ref, vo_ref, wke_s, wve_s):
    half = xk_ref.shape[2]
    n = xk_ref.shape[1]
    hd = HEAD_DIM
    nh = NSA_CMP_HIDDEN

    @pl.when(pl.program_id(0) == 0)
    def _():
        for w_ref, we_ref in ((wk1_ref, wke_s), (wv1_ref, wve_s)):
            we_ref[...] = jnp.zeros(we_ref.shape, BF16)
            for tok in range(NSA_CMP_LEN):
                for g in range(KV_HEADS):
                    row = (tok * KV_HEADS + g) * hd
                    we_ref[row:row + hd, g * nh:(g + 1) * nh] = w_ref[tok * hd:(tok + 1) * hd, :]

    def hidden(x_ref, we_ref, pos_ref, b1_ref):
        x = x_ref[0].astype(BF16)
        first = _dot(x, we_ref[:half, :])
        second = _dot(x, we_ref[half:, :])
        pos = jnp.broadcast_to(pos_ref[...], (8, 2 * half)).astype(BF16)
        ph = _dot(pos[:, :half], we_ref[:half, :]) + _dot(pos[:, half:], we_ref[half:, :])
        return _silu(first + pltpu.roll(second, n - 1, 0) + ph[0:1, :] + b1_ref[...])

    hk = hidden(xk_ref, wke_s, pk_ref, bk1_ref).astype(BF16)
    hv = hidden(xv_ref, wve_s, pv_ref, bv1_ref).astype(BF16)
    nh = NSA_CMP_HIDDEN
    ks, vs = [], []
    for g in range(KV_HEADS):
        ks.append(_rms(_dot(hk[:, g * nh:(g + 1) * nh], wk2_ref[...]) + bk2_ref[...], gn_ref[...]))
        vs.append(_dot(hv[:, g * nh:(g + 1) * nh], wv2_ref[...]) + bv2_ref[...])
    ko_ref[0] = jnp.concatenate(ks, axis=1)
    vo_ref[0] = jnp.concatenate(vs, axis=1)


def _compress(xk, xv, pos_k, pos_v, kw1, kb1, kw2, kb2, vw1, vb1, vw2, vb2, gn):
    bsz, n, half = xk.shape
    hd = HEAD_DIM
    tile_pos = lambda p: jnp.tile(p[:, None, :], (1, KV_HEADS, 1)).reshape(1, 2 * half)
    tile_b = lambda b: jnp.tile(b.reshape(1, -1), (1, KV_HEADS))
    const = lambda shape: pl.BlockSpec(shape, lambda b: (0,) * len(shape))
    xs = pl.BlockSpec((1, n, half), lambda b: (b, 0, 0))
    hw = KV_HEADS * NSA_CMP_HIDDEN
    out = pl.BlockSpec((1, n, KV_HEADS * hd), lambda b: (b, 0, 0))
    expanded = pltpu.VMEM((2 * half, hw), BF16)
    return pl.pallas_call(
        _cmp_kernel,
        grid=(bsz,),
        in_specs=[xs, xs, const(kw1.shape), const(vw1.shape),
                  const((1, 2 * half)), const((1, 2 * half)), const((1, hw)), const((1, hw)),
                  const((NSA_CMP_HIDDEN, hd)), const((NSA_CMP_HIDDEN, hd)), const((1, hd)), const((1, hd)),
                  const((1, hd))],
        out_specs=[out, out],
        out_shape=[jax.ShapeDtypeStruct((bsz, n, KV_HEADS * hd), F32)] * 2,
        scratch_shapes=[expanded, expanded],
        name="nsa_compress",
    )(xk, xv, kw1.astype(BF16), vw1.astype(BF16), tile_pos(pos_k), tile_pos(pos_v), tile_b(kb1), tile_b(vb1),
      kw2.astype(BF16), vw2.astype(BF16), kb2.reshape(1, hd), vb2.reshape(1, hd), gn.reshape(1, hd))


def _group_query(q_t, alibi_ref, g, gq_col):
    hd = HEAD_DIM
    heads = range(Q_PER_KV * g, Q_PER_KV * (g + 1))
    qn = jnp.concatenate([_norm_query_t(q_t[h * hd:(h + 1) * hd], gq_col) for h in heads], axis=1)
    alibi = jnp.concatenate([alibi_ref[h] for h in heads], axis=1)
    return qn, alibi


def _augment_keys(k_ref, gk_ref, dst_ref, sel_shift=None):
    assert KV_HEADS == 2
    pos = _iota(k_ref.shape[1:], 0)
    dst_ref[0], dst_ref[1] = _augment_pair_keys(_pair_rms(k_ref[0], gk_ref[...]), pos, sel_shift)


def _chunk_scores(ka_ref, vt_ref, g, c, qa, tq):
    off = pl.multiple_of(c * tq, tq)
    return _dot(ka_ref[g, pl.ds(off, tq), :], qa), vt_ref[c, g * HEAD_DIM:(g + 1) * HEAD_DIM, :]


def _window_first_chunk(ka_ref, vt_ref, g, far, qa, tq, edge_cap):
    sc, v_t = _chunk_scores(ka_ref, vt_ref, g, jnp.maximum(far, 0), qa, tq)
    return jnp.minimum(sc, jnp.minimum(edge_cap, jnp.where(far >= 0, BIG, -BIG))), v_t


def _regroup_heads(o_t, tq):
    return jnp.concatenate([o_t[:, k * tq:(k + 1) * tq] for k in range(Q_PER_KV)], axis=0)


def _nsa_kernel(alibi_ref, q_ref, kc_ref, vc_ref, ks_ref, vs_ref, kw_ref, vw_ref, gate_ref, z_ref,
                gq_ref, gks_ref, gkw_ref, o_ref, kca_s, vct_s, ksa_s, vst_s, kwa_s, vwt_s, m_s, acc_s,
                sc0_s, sc1_s):
    i = pl.program_id(1)
    tq = q_ref.shape[1]
    s = ks_ref.shape[1]
    hd = HEAD_DIM
    ncmp = kc_ref.shape[1]
    nsb = s // NSA_SEL_BLOCK
    n4 = Q_PER_KV * tq

    @pl.when(i == 0)
    def _():
        _augment_keys(ks_ref, gks_ref, ksa_s, NSA_SEL_BLOCK.bit_length() - 1)
        _augment_keys(kw_ref, gkw_ref, kwa_s)
        _store_chunked_t(vs_ref[0], vst_s, tq)
        _store_chunked_t(vw_ref[0], vwt_s, tq)
        cend = _iota((ncmp, LANES), 0) * NSA_CMP_STRIDE + (NSA_CMP_LEN - 1)
        kca_s[0], kca_s[1] = _augment_pair_keys(kc_ref[0], cend)
        vct_s[...] = vc_ref[0].T.astype(BF16)

    t0 = i * tq
    q_t = q_ref[0].T
    gates_t = _sigmoid(gate_ref[0]).T
    diag_cap = _causal_cap(tq, tq, Q_PER_KV)
    edge_cap = _window_edge_cap(tq, tq, Q_PER_KV)
    cend =_iota((ncmp, tq), 0) * NSA_CMP_STRIDE + (NSA_CMP_LEN - 1)
    seen = _tile_lanes(cend <= t0 + _iota((ncmp, tq), 1), Q_PER_KV)
    cstart = _iota((nsb, ncmp), 1) * NSA_CMP_STRIDE
    bstart = _iota((nsb, ncmp), 0) * NSA_SEL_BLOCK
    overlap_t = jnp.where((cstart < bstart + NSA_SEL_BLOCK) & (cstart + NSA_CMP_LEN > bstart), 1.0, 0.0)
    blk = _iota((nsb, tq), 0)
    cur = (t0 + _iota((nsb, tq), 1)) >> (NSA_SEL_BLOCK.bit_length() - 1)
    forced = (blk == 0) | (blk == cur) | (blk == cur - 1)
    groups = range(KV_HEADS)
    far = i - NSA_WINDOW // tq
    lo = jnp.maximum(far + 1, 0)
    first_win = jnp.maximum(far, 0)

    state = (m_s, acc_s)
    sel_slots = tuple(groups)
    win_slots = tuple(KV_HEADS + g for g in groups)
    _flash_reset(state)
    queries = [_group_query(q_t, alibi_ref, g, gq_ref[...]) for g in groups]
    qa_win = [_augment_query(qn, alibi, flipped=g == 1) for g, (qn, alibi) in enumerate(queries)]
    qa_sel = []

    def scores(slot, c):
        g = slot % KV_HEADS
        if slot in sel_slots:
            return _chunk_scores(ksa_s, vst_s, g, c, qa_sel[g], tq)[0]
        return _chunk_scores(kwa_s, vwt_s, g, c, qa_win[g], tq)[0]

    def values(slot, c):
        g = slot % KV_HEADS
        return (vst_s if slot in sel_slots else vwt_s)[c, g * hd:(g + 1) * hd, :]

    cmp_sc = [_dot(kca_s[g], qa_win[g]) for g in groups]
    sc_win_first = [scores(slot, first_win) for slot in win_slots]
    o_cmp = []
    for g in groups:
        qn, alibi = queries[g]
        sc = jnp.where(seen, cmp_sc[g], -BIG)
        e = jnp.where(seen, jnp.exp2(sc - jnp.max(sc, axis=0, keepdims=True)), 0.0)
        p_c = e / jnp.maximum(jnp.sum(e, axis=0, keepdims=True), TINY)
        o_cmp.append(_dot(vct_s[g * hd:(g + 1) * hd, :], p_c.astype(BF16)))

        psum = p_c[:, 0:tq]
        for k in range(1, Q_PER_KV):
            psum = psum + p_c[:, k * tq:(k + 1) * tq]
        imp = _dot_fine(overlap_t, psum, a_is_bf16_exact=True)
        imp = jnp.where(blk <= cur, jnp.where(forced, NSA_FORCE, imp), NEG)
        chosen = _rank_select(imp, blk, nsb, min(NSA_TOPN, nsb))
        sel_bias = _tile_lanes(jnp.where(chosen, 0.0, -BIG), Q_PER_KV)
        qa_sel.append(_augment_query(qn, alibi, sel_bias, flipped=g == 1))

    bufs = (sc0_s, sc1_s)
    for slot in sel_slots:
        sc0_s[slot] = scores(slot, 0)
    for slot, sc in zip(win_slots, sc_win_first):
        sc0_s[slot] = sc
        sc1_s[slot] = sc

    def body(slots, cap, cap_slots, c, carry):
        _flash_update_pipelined(state, bufs, slots, c, lambda slot: scores(slot, c + 1),
                                lambda slot: values(slot, c), cap, cap_slots)
        return carry

    both_slots = sel_slots + win_slots
    lax.fori_loop(0, first_win, functools.partial(body, sel_slots, None, None), 0)
    lax.fori_loop(first_win, lo, functools.partial(body, both_slots, edge_cap, win_slots), 0)
    lax.fori_loop(lo, i, functools.partial(body, both_slots, None, None), 0)
    _flash_update_pipelined(state, bufs, both_slots, i, None, lambda slot: values(slot, i), diag_cap)

    outs = []
    for g in groups:
        def gate_row(branch):
            rows = [gates_t[branch * N_HEADS + h:branch * N_HEADS + h + 1, :]
                    for h in range(Q_PER_KV * g, Q_PER_KV * (g + 1))]
            return jnp.concatenate(rows, axis=1)

        o_slc = _flash_result(state, sel_slots[g])
        o_win = _flash_result(state, win_slots[g])
        o = gate_row(0) * o_cmp[g] + gate_row(1) * o_slc + gate_row(2) * o_win
        outs.append(_regroup_heads(o, tq))
    y = jnp.concatenate(outs, axis=0).T * _silu(z_ref[0])
    o_ref[0] = y.astype(o_ref.dtype)


def _nsa(proj, kcn, vc, gq, gks, gkw, *, tq=256):
    bsz, s, _ = proj.shape
    hd = HEAD_DIM
    gw = GROUP_WIDTH
    ncmp = kcn.shape[1]
    assert SEL_FEAT0 + s // NSA_SEL_BLOCK <= hd and ncmp == LANES
    wide = lambda c0: pl.BlockSpec((1, tq, gw), lambda b, i: (b, i, c0 // 4))
    full = lambda c0: pl.BlockSpec((1, s, LANES), lambda b, i: (b, 0, c0))
    cmp = pl.BlockSpec((1, ncmp, LANES), lambda b, i: (b, 0, 0))
    row = pl.BlockSpec((1, hd), lambda b, i: (0, 0))
    ka = pltpu.VMEM((KV_HEADS, s, LANES), BF16)
    vt = pltpu.VMEM((s // tq, LANES, tq), BF16)
    return pl.pallas_call(
        _nsa_kernel,
        grid=(bsz, s // tq),
        in_specs=[pl.BlockSpec((N_HEADS, 2 * SLOPE_PIECES, tq), lambda b, i: (0, 0, 0)),
                  wide(ODD_QC), cmp, cmp, full(ODD_KS), full(ODD_VS), full(ODD_KW), full(ODD_VW),
                  pl.BlockSpec((1, tq, LANES), lambda b, i: (b, i, ODD_GC)), wide(ODD_ZC),
                  pl.BlockSpec((hd, 1), lambda b, i: (0, 0)), row, row],
        out_specs=pl.BlockSpec((1, tq, gw), lambda b, i: (b, i, 0)),
        out_shape=jax.ShapeDtypeStruct((bsz, s, gw), BF16),
        scratch_shapes=[pltpu.VMEM((KV_HEADS, ncmp, LANES), BF16), pltpu.VMEM((LANES, ncmp), BF16),
                        ka, vt, ka, vt] + _flash_scratch(2 * KV_HEADS, Q_PER_KV * tq)
                       + [pltpu.VMEM((2 * KV_HEADS, tq, Q_PER_KV * tq), F32)] * 2,
        name="nsa",
    )(jnp.asarray(_alibi_query_features(tq)), proj, kcn, vc, proj, proj, proj, proj, proj, proj,
      gq.reshape(hd, 1), gks.reshape(1, hd), gkw.reshape(1, hd))


def _swa_kernel(sinks_ref, alibi_ref, q_ref, k_ref, v_ref, z_ref, gq_ref, gk_ref, o_ref, ka_s, vt_s):
    step = pl.program_id(1)
    tq = SWA_WINDOW
    tiles = q_ref.shape[1] // tq
    s = k_ref.shape[1]

    @pl.when(step == 0)
    def _():
        _augment_keys(k_ref, gk_ref, ka_s)
        _store_chunked_t(v_ref[0], vt_s, tq)

    q_t = q_ref[0].T
    diag_cap = _causal_cap(tq, tq, Q_PER_KV)
    edge_cap = _window_edge_cap(tq, tq, Q_PER_KV)
    chains = [(u, g) for u in range(tiles) for g in range(KV_HEADS)]
    qas = [_augment_query(*_group_query(q_t[:, u * tq:(u + 1) * tq], alibi_ref, g, gq_ref[...]), flipped=g == 1)
           for u, g in chains]
    first = [_window_first_chunk(ka_s, vt_s, g, step * tiles + u - 1, qa, tq, edge_cap)
             for (u, g), qa in zip(chains, qas)]
    own = [_chunk_scores(ka_s, vt_s, g, step * tiles + u, qa, tq) for (u, g), qa in zip(chains, qas)]
    carries = _flash_steps([_flash_init(Q_PER_KV * tq) for _ in chains], first)
    carries = _flash_steps(carries, own, diag_cap)
    outs = []
    for (u, g), carry in zip(chains, carries):
        m, acc, l = _flash_split(carry)
        tpos = ((step * tiles + u) * tq + _iota((1, tq), 1)).astype(F32)
        sink = jnp.concatenate([sinks_ref[h] * LOG2E + (SLOPES[h] * LOG2E) * tpos
                                for h in range(Q_PER_KV * g, Q_PER_KV * (g + 1))], axis=1)
        mf = jnp.maximum(m, sink)
        alpha = jnp.exp2(m - mf)
        o = acc * alpha / jnp.maximum(l * alpha + jnp.exp2(sink - mf), TINY)
        outs.append(_regroup_heads(o, tq))
    o_t = jnp.concatenate([jnp.concatenate(outs[u * KV_HEADS:(u + 1) * KV_HEADS], axis=0) for u in range(tiles)],
                          axis=1)
    y = o_t.T * _silu(z_ref[0])
    o_ref[0] = y.astype(o_ref.dtype)


def _swa(proj, sinks, gq, gk, *, tiles_per_step=4):
    bsz, s, _ = proj.shape
    hd = HEAD_DIM
    gw = GROUP_WIDTH
    tq = SWA_WINDOW
    rows = tiles_per_step * tq
    wide = lambda c0: pl.BlockSpec((1, rows, gw), lambda b, i: (b, i, c0 // 4))
    full = lambda c0: pl.BlockSpec((1, s, LANES), lambda b, i: (b, 0, c0))
    return pl.pallas_call(
        _swa_kernel,
        grid=(bsz, s // rows),
        in_specs=[pl.BlockSpec(memory_space=pltpu.SMEM),
                  pl.BlockSpec((N_HEADS, 2 * SLOPE_PIECES, tq), lambda b, i: (0, 0, 0)),
                  wide(ODD_QD), full(ODD_KD), full(ODD_VD), wide(ODD_ZD),
                  pl.BlockSpec((hd, 1), lambda b, i: (0, 0)), pl.BlockSpec((1, hd), lambda b, i: (0, 0))],
        out_specs=pl.BlockSpec((1, rows, gw), lambda b, i: (b, i, 0)),
        out_shape=jax.ShapeDtypeStruct((bsz, s, gw), BF16),
        scratch_shapes=[pltpu.VMEM((KV_HEADS, s, LANES), BF16), pltpu.VMEM((s // tq, LANES, tq), BF16)],
        name="swa",
    )(sinks.astype(F32), jnp.asarray(_alibi_query_features(tq)), proj, proj, proj, proj,
      gq.reshape(hd, 1), gk.reshape(1, hd))


def _pack_odd_weight(w):
    gw, kw, ng = GROUP_WIDTH, KV_HEADS * HEAD_DIM, 3 * N_HEADS
    sizes = [gw, kw, kw, kw, kw, kw, kw, ng, gw, gw, kw, kw, gw]
    starts = np.concatenate([[0], np.cumsum(sizes)]).tolist()
    part = lambda k: w[:, starts[k]:starts[k + 1]]
    pad = jnp.zeros((w.shape[0], LANES - ng), w.dtype)
    order = [0, 8, 9, 12, 1, 2, 3, 4, 5, 6, 10, 11, 7]
    return jnp.concatenate([part(k) for k in order] + [pad], axis=1).astype(BF16)


def kernel(x, norm_g, w_out, e_w_in, a_conv_w, a_conv_b, a_ln_g, a_ln_b, b_qnorm_g, b_knorm_g, o_w_in, c_qnorm_g, c_knorm_cmp_g, c_knorm_slc_g, c_knorm_win_g, c_pos_k, c_pos_v, c_k_w1, c_k_b1, c_k_w2, c_k_b2, c_v_w1, c_v_b1, c_v_w2, c_v_b2, d_qnorm_g, d_knorm_g, d_sinks):
    bsz, s, d = x.shape
    m = bsz * s
    assert s % MOBA_BLOCK == 0 and d == 2 * GROUP_WIDTH
    x2 = x.reshape(m, d)

    proj, q_t, ka, vt, km = _inproj_moba(x2, norm_g[0], e_w_in[0].astype(BF16), b_qnorm_g[0], b_knorm_g[0],
                                         bsz=bsz, chunk=GROUP_WIDTH)
    proj = proj.reshape(bsz, s, -1)
    y_a = _conv_module(proj, a_conv_w[0], a_conv_b[0], a_ln_g[0], a_ln_b[0])
    y_b = _moba(proj, q_t, ka, vt, km)
    x2 = _outproj(x2, y_a.reshape(m, -1), y_b.reshape(m, -1), w_out[0].astype(BF16))

    proj, k_cmp, v_cmp = _inproj(x2, norm_g[1], _pack_odd_weight(o_w_in[0]), chunk=5 * LANES,
                                 copies=((ODD_KC * LANES, LANES), (ODD_VC * LANES, LANES)), group=NSA_CMP_STRIDE)
    proj = proj.reshape(bsz, s, ODD_COLS)
    rows16 = lambda t: t.reshape(bsz, s // NSA_CMP_STRIDE, NSA_CMP_STRIDE * LANES)
    kcn, vc = _compress(rows16(k_cmp), rows16(v_cmp), c_pos_k[0], c_pos_v[0], c_k_w1[0], c_k_b1[0],
                        c_k_w2[0], c_k_b2[0], c_v_w1[0], c_v_b1[0], c_v_w2[0], c_v_b2[0], c_knorm_cmp_g[0])
    y_c = _nsa(proj, kcn, vc, c_qnorm_g[0], c_knorm_slc_g[0], c_knorm_win_g[0])
    y_d = _swa(proj, d_sinks[0], d_qnorm_g[0], d_knorm_g[0])
    x2 = _outproj(x2, y_c.reshape(m, -1), y_d.reshape(m, -1), w_out[1].astype(BF16))
    return x2.reshape(bsz, s, d)
```

```python
import functools

import ml_dtypes
import numpy as np
import jax
import jax.numpy as jnp
from jax import lax
from jax.experimental import pallas as pl
from jax.experimental.pallas import tpu as pltpu

HEAD_DIM = 64
N_HEADS = 8
GROUP_WIDTH = N_HEADS * HEAD_DIM
CONV_WIDTH = 31
MOBA_BLOCK = 256
MOBA_TOPK = 3
KV_HEADS = 2
Q_PER_KV = N_HEADS // KV_HEADS
NSA_CMP_LEN = 32
NSA_CMP_STRIDE = 16
NSA_CMP_HIDDEN = 256
NSA_SEL_BLOCK = 64
NSA_TOPN = 16
NSA_WINDOW = 512
NSA_FORCE = 1e4
SWA_WINDOW = 128
EPS = 1e-6
NEG = -1e30
TINY = 1e-30
LANES = 128
SUBLANES = 8
CONV_HALO = 32

LOG2E = float(np.log2(np.e))
BIG = 2.0 ** 99
M_INIT = -1e38
POS_SPLIT_SHIFT = 8
SLOPE_PIECES = 4
SEL_FEAT0 = 2 * SLOPE_PIECES

F32 = jnp.float32
BF16 = jnp.bfloat16

ODD_QC, ODD_ZC, ODD_QD, ODD_ZD = 0, 4, 8, 12
ODD_KC, ODD_VC, ODD_KS, ODD_VS, ODD_KW, ODD_VW, ODD_KD, ODD_VD, ODD_GC = 16, 17, 18, 19, 20, 21, 22, 23, 24
ODD_COLS = 25 * LANES


def _alibi_slopes(n):
    return [float(2.0 ** (-8.0 * (i + 1) / n)) for i in range(n)]


SLOPES = _alibi_slopes(N_HEADS)


def _alibi_query_features(width):
    table = np.zeros((N_HEADS, 2 * SLOPE_PIECES, width), np.float32)
    for h, slope in enumerate(SLOPES):
        rest = np.float64(slope) * LOG2E
        for k in range(SLOPE_PIECES):
            piece = float(np.float32(rest).astype(ml_dtypes.bfloat16).astype(np.float32))
            table[h, k] = piece
            table[h, SLOPE_PIECES + k] = piece
            rest -= piece
    return table


def _dot(a, b):
    return jnp.dot(a, b, preferred_element_type=F32)


def _split_bf16(x, pieces):
    out = []
    for _ in range(pieces):
        out.append(x.astype(BF16))
        x = x - out[-1].astype(F32)
    return out


def _dot_fine(a, b, a_is_bf16_exact=False):
    if a_is_bf16_exact:
        return sum(_dot(a.astype(BF16), piece) for piece in _split_bf16(b, 3))
    (a_hi, a_lo), (b_hi, b_lo) = _split_bf16(a, 2), _split_bf16(b, 2)
    return _dot(a_hi, b_hi) + (_dot(a_hi, b_lo) + _dot(a_lo, b_hi))


def _sigmoid(x):
    return 1.0 / (1.0 + jnp.exp(-x))


def _silu(x):
    return x * _sigmoid(x)


def _rms(x, g):
    return x * lax.rsqrt(jnp.mean(x * x, axis=-1, keepdims=True) + EPS) * g


def _iota(shape, dim):
    return lax.broadcasted_iota(jnp.int32, shape, dim)


def _inproj_kernel(x_ref, g_ref, w_ref, o_ref, *rest, chunk, copies, group):
    copy_refs, stage_refs = rest[:len(copies)], rest[len(copies):]
    h = _rms(x_ref[...], g_ref[...]).astype(BF16)
    for c in range(o_ref.shape[1] // chunk):
        o_ref[:, c * chunk:(c + 1) * chunk] = _dot(h, w_ref[:, c * chunk:(c + 1) * chunk])
    for ref, stage, (start, width) in zip(copy_refs, stage_refs, copies):
        stage[...] = o_ref[:, start:start + width]
        for t in range(group):
            ref[:, t * width:(t + 1) * width] = stage[pl.ds(t, ref.shape[0], stride=group), :]


def _inproj(x2d, g, w, *, tm=512, chunk, copies=(), group=1):
    m, d = x2d.shape
    e = w.shape[1]
    rows = lambda n, width: pl.BlockSpec((n, width), lambda i: (i, 0))
    const = lambda shape: pl.BlockSpec(shape, lambda i: (0, 0))
    return pl.pallas_call(
        functools.partial(_inproj_kernel, chunk=chunk, copies=tuple(copies), group=group),
        grid=(m // tm,),
        in_specs=[rows(tm, d), const((1, d)), const((d, e))],
        out_specs=[rows(tm, e)] + [rows(tm // group, group * width) for _, width in copies],
        out_shape=[jax.ShapeDtypeStruct((m, e), F32)]
                  + [jax.ShapeDtypeStruct((m // group, group * width), F32) for _, width in copies],
        scratch_shapes=[pltpu.VMEM((tm, width), F32) for _, width in copies],
        name="inproj",
    )(x2d, g.reshape(1, d), w)


def _inproj_moba_kernel(x_ref, g_ref, w_ref, gq_ref, gk_ref, o_ref, qt_ref, ka_ref, vt_ref, km_ref, *, chunk, seq):
    tm = x_ref.shape[0]
    hd, gw = HEAD_DIM, GROUP_WIDTH
    h = _rms(x_ref[...], g_ref[...]).astype(BF16)
    for c in range(o_ref.shape[1] // chunk):
        o_ref[:, c * chunk:(c + 1) * chunk] = _dot(h, w_ref[:, c * chunk:(c + 1) * chunk])
    q0, k0, v0 = 3 * gw, 4 * gw, 5 * gw
    q_t = o_ref[:, q0:q0 + gw].T
    for j in range(N_HEADS):
        qt_ref[0, j * hd:(j + 1) * hd, :] = _norm_query_t(q_t[j * hd:(j + 1) * hd], gq_ref[...])
    pos = (pl.program_id(0) % (seq // tm)) * tm + _iota((tm, LANES), 0)
    for pair in range(N_HEADS // 2):
        kn = _pair_rms(o_ref[:, k0 + pair * LANES:k0 + (pair + 1) * LANES], gk_ref[...])
        ka_ref[0, 2 * pair], ka_ref[0, 2 * pair + 1] = _augment_pair_keys(kn, pos, MOBA_BLOCK.bit_length() - 1)
        km_ref[0, 0, :, pair * LANES:(pair + 1) * LANES] = jnp.mean(
            kn.reshape(tm // MOBA_BLOCK, MOBA_BLOCK, LANES), axis=1)
    v_t = o_ref[:, v0:v0 + gw].T.astype(BF16)
    for c in range(tm // MOBA_BLOCK):
        vt_ref[0, c] = v_t[:, c * MOBA_BLOCK:(c + 1) * MOBA_BLOCK]


def _inproj_moba(x2d, g, w, gq, gk, *, bsz, tm=512, chunk):
    m, d = x2d.shape
    e = w.shape[1]
    s = m // bsz
    per_b = s // tm
    hd, gw, nb = HEAD_DIM, GROUP_WIDTH, tm // MOBA_BLOCK
    assert s % tm == 0 and tm % MOBA_BLOCK == 0
    const = lambda shape: pl.BlockSpec(shape, lambda i: (0, 0))
    return pl.pallas_call(
        functools.partial(_inproj_moba_kernel, chunk=chunk, seq=s),
        grid=(m // tm,),
        in_specs=[pl.BlockSpec((tm, d), lambda i: (i, 0)), const((1, d)), const((d, e)), const((hd, 1)),
                  const((1, hd))],
        out_specs=[pl.BlockSpec((tm, e), lambda i: (i, 0)),
                   pl.BlockSpec((1, gw, tm), lambda i: (i // per_b, 0, i % per_b)),
                   pl.BlockSpec((1, N_HEADS, tm, LANES), lambda i: (i // per_b, 0, i % per_b, 0)),
                   pl.BlockSpec((1, nb, gw, MOBA_BLOCK), lambda i: (i // per_b, i % per_b, 0, 0)),
                   pl.BlockSpec((1, 1, nb, gw), lambda i: (i // per_b, i % per_b, 0, 0))],
        out_shape=[jax.ShapeDtypeStruct((m, e), F32),
                   jax.ShapeDtypeStruct((bsz, gw, s), F32),
                   jax.ShapeDtypeStruct((bsz, N_HEADS, s, LANES), BF16),
                   jax.ShapeDtypeStruct((bsz, s // MOBA_BLOCK, gw, MOBA_BLOCK), BF16),
                   jax.ShapeDtypeStruct((bsz, per_b, nb, gw), F32)],
        name="inproj_moba",
    )(x2d, g.reshape(1, d), w, gq.reshape(hd, 1), gk.reshape(1, hd))


def _outproj_kernel(x_ref, ya_ref, yb_ref, wa_ref, wb_ref, o_ref):
    o_ref[...] = x_ref[...] + _dot(ya_ref[...], wa_ref[...]) + _dot(yb_ref[...], wb_ref[...])


def _outproj(x2d, ya, yb, w, *, tm=1024):
    m, d = x2d.shape
    gw = ya.shape[1]
    return pl.pallas_call(
        _outproj_kernel,
        grid=(m // tm,),
        in_specs=[pl.BlockSpec((tm, d), lambda i: (i, 0)),
                  pl.BlockSpec((tm, gw), lambda i: (i, 0)),
                  pl.BlockSpec((tm, gw), lambda i: (i, 0)),
                  pl.BlockSpec((gw, d), lambda i: (0, 0)),
                  pl.BlockSpec((gw, d), lambda i: (0, 0))],
        out_specs=pl.BlockSpec((tm, d), lambda i: (i, 0)),
        out_shape=jax.ShapeDtypeStruct((m, d), F32),
        name="outproj",
    )(x2d, ya, yb, w[:gw], w[gw:])


def _conv_kernel(uv_ref, ug_ref, uvh_ref, ugh_ref, z_ref, w_ref, b_ref, lg_ref, lb_ref, o_ref, h_s, *, rows):
    i = pl.program_id(1)
    ts = uv_ref.shape[1]
    halo = uvh_ref[0] * _sigmoid(ugh_ref[0])
    h_s[0:CONV_HALO, :] = jnp.where(i > 0, halo, 0.0)
    h_s[CONV_HALO:, :] = uv_ref[0] * _sigmoid(ug_ref[0])
    base = CONV_HALO - (CONV_WIDTH - 1)
    for c in range(ts // rows):
        acc = None
        for b in range(SUBLANES):
            n = rows if b == 0 else rows + SUBLANES
            part = None
            for a in range((base + CONV_WIDTH - 1) // SUBLANES + 1):
                j = SUBLANES * a + b - base
                if 0 <= j < CONV_WIDTH:
                    term = w_ref[j:j + 1, :] * h_s[c * rows + SUBLANES * a:c * rows + SUBLANES * a + n, :]
                    part = term if part is None else part + term
            part = part[b:b + rows]
            acc = part if acc is None else acc + part
        y = acc + b_ref[...]
        mu = jnp.mean(y, axis=-1, keepdims=True)
        yc = y - mu
        y = yc * lax.rsqrt(jnp.mean(yc * yc, axis=-1, keepdims=True) + EPS) * lg_ref[...] + lb_ref[...]
        out = _silu(y) * _silu(z_ref[0, c * rows:(c + 1) * rows, :])
        o_ref[0, c * rows:(c + 1) * rows, :] = out.astype(o_ref.dtype)


def _conv_module(proj, w, b, lg, lb, *, ts=256, rows=128):
    bsz, s, _ = proj.shape
    gw = GROUP_WIDTH
    hb = ts // CONV_HALO
    cur = lambda col: pl.BlockSpec((1, ts, gw), lambda bi, i: (bi, i, col))
    halo = lambda col: pl.BlockSpec((1, CONV_HALO, gw), lambda bi, i: (bi, jnp.maximum(i * hb - 1, 0), col))
    vec = pl.BlockSpec((1, gw), lambda bi, i: (0, 0))
    return pl.pallas_call(
        functools.partial(_conv_kernel, rows=rows),
        grid=(bsz, s // ts),
        in_specs=[cur(0), cur(1), halo(0), halo(1), cur(2),
                  pl.BlockSpec((CONV_WIDTH, gw), lambda bi, i: (0, 0)), vec, vec, vec],
        out_specs=pl.BlockSpec((1, ts, gw), lambda bi, i: (bi, i, 0)),
        out_shape=jax.ShapeDtypeStruct((bsz, s, gw), BF16),
        scratch_shapes=[pltpu.VMEM((CONV_HALO + ts, gw), F32)],
        name="conv_module",
    )(proj, proj, proj, proj, proj, w, b.reshape(1, gw), lg.reshape(1, gw), lb.reshape(1, gw))


def _key_features(pos, flipped, sel_shift=None):
    col = _iota(pos.shape, 1) - (0 if flipped else HEAD_DIM)
    hi = (pos >> POS_SPLIT_SHIFT) << POS_SPLIT_SHIFT
    lo = pos & ((1 << POS_SPLIT_SHIFT) - 1)
    feat = jnp.where(col < SLOPE_PIECES, hi, jnp.where(col < 2 * SLOPE_PIECES, lo, 0))
    if sel_shift is not None:
        feat = jnp.where(col - SEL_FEAT0 == (pos >> sel_shift), 1, feat)
    return jnp.where((col >= 0) & (col < HEAD_DIM), feat, 0).astype(F32)


def _pair_rms(x, g):
    sq = x * x
    hi = sq.astype(BF16)
    lo = (sq - hi.astype(F32)).astype(BF16)
    same_head = (_iota((LANES, LANES), 0) >> 6) == (_iota((LANES, LANES), 1) >> 6)
    ones = jnp.where(same_head, 1.0, 0.0).astype(BF16)
    ss = _dot(hi, ones) + _dot(lo, ones)
    return x * lax.rsqrt(ss * (1.0 / HEAD_DIM) + EPS) * jnp.concatenate([g, g], axis=1)


def _augment_pair_keys(kn, pos, sel_shift=None):
    lane = _iota(kn.shape, 1)
    return [jnp.where(lane >= HEAD_DIM if flipped else lane < HEAD_DIM, kn,
                      _key_features(pos, flipped, sel_shift)).astype(BF16) for flipped in (False, True)]


def _norm_query_t(x, g_col):
    ss = jnp.mean(x * x, axis=0, keepdims=True)
    return x * lax.rsqrt(ss + EPS) * g_col * (HEAD_DIM ** -0.5 * LOG2E)


def _augment_query(q_t, alibi, sel_bias=None, flipped=False):
    n = q_t.shape[1]
    feats = [alibi]
    used = alibi.shape[0]
    if sel_bias is not None:
        feats.append(sel_bias)
        used += sel_bias.shape[0]
    feats.append(jnp.zeros((HEAD_DIM - used, n), F32))
    return jnp.concatenate(feats + [q_t] if flipped else [q_t] + feats, axis=0).astype(BF16)


ONES_ROWS = 16
ACC_ROWS = HEAD_DIM + ONES_ROWS


def _flash_init(n):
    return jnp.full((1, n), M_INIT, F32), jnp.zeros((ACC_ROWS, n), F32)


def _flash_step(carry, sc, v_t):
    m, acc = carry
    m_new = jnp.maximum(m, jnp.max(sc, axis=0, keepdims=True))
    p = jnp.exp2(sc - m_new).astype(BF16)
    v_ones = jnp.concatenate([v_t, jnp.ones((ONES_ROWS, v_t.shape[1]), BF16)], axis=0)
    return m_new, jnp.exp2(m - m_new) * acc + _dot(v_ones, p)


def _flash_split(carry):
    m, acc = carry
    return m, acc[:HEAD_DIM], acc[HEAD_DIM:HEAD_DIM + 1]


def _flash_steps(carries, chunks, cap=None):
    return tuple(_flash_step(carry, sc if cap is None else jnp.minimum(sc, cap), v_t)
                 for carry, (sc, v_t) in zip(carries, chunks))


def _flash_reset(state):
    m_ref, acc_ref = state
    m_ref[...] = jnp.full(m_ref.shape, M_INIT, F32)
    acc_ref[...] = jnp.zeros(acc_ref.shape, F32)


def _flash_update(state, slot, sc, v_t, cap=None):
    m_ref, acc_ref = state
    m_ref[slot], acc_ref[slot] = _flash_step((m_ref[slot], acc_ref[slot]),
                                             sc if cap is None else jnp.minimum(sc, cap), v_t)


def _flash_update_pipelined(state, bufs, slots, c, next_scores, values, cap=None, cap_slots=None):
    def run(src_ref, dst_ref):
        for slot in slots:
            fresh = None if next_scores is None else next_scores(slot)
            masked = cap is not None and (cap_slots is None or slot in cap_slots)
            _flash_update(state, slot, src_ref[slot], values(slot), cap if masked else None)
            if fresh is not None:
                dst_ref[slot] = fresh

    pl.when((c & 1) == 0)(lambda: run(bufs[0], bufs[1]))
    pl.when((c & 1) == 1)(lambda: run(bufs[1], bufs[0]))


def _flash_result(state, slot):
    m_ref, acc_ref = state
    m, out, den = _flash_split((m_ref[slot], acc_ref[slot]))
    return out / jnp.maximum(den, TINY)


def _flash_scratch(chains, n):
    return [pltpu.VMEM((chains, 1, n), F32), pltpu.VMEM((chains, ACC_ROWS, n), F32)]


def _rank_select(score, blk, limit, count):
    ranks = []
    for r0 in range(0, score.shape[0], SUBLANES):
        tile = score[r0:r0 + SUBLANES]
        tile_blk = r0 + _iota(tile.shape, 0)
        rank = jnp.zeros(tile.shape, jnp.int32)
        for mm in range(limit):
            gm = score[mm:mm + 1, :]
            if mm < r0:
                beats = gm >= tile
            elif mm >= r0 + SUBLANES:
                beats = gm > tile
            else:
                beats = (gm > tile) | ((gm == tile) & (mm < tile_blk))
            rank = rank + jnp.where(beats, 1, 0)
        ranks.append(rank)
    return jnp.concatenate(ranks, axis=0) < count


def _tile_lanes(x, reps):
    return jnp.concatenate([x] * reps, axis=1) if reps > 1 else x


def _causal_cap(tk, tq, reps):
    keep = _iota((tk, tq), 0) <= _iota((tk, tq), 1)
    return _tile_lanes(jnp.where(keep, BIG, -BIG), reps)


def _window_edge_cap(tk, tq, reps):
    keep = _iota((tk, tq), 0) > _iota((tk, tq), 1)
    return _tile_lanes(jnp.where(keep, BIG, -BIG), reps)


def _store_chunked_t(x, dst_ref, tk):
    x_t = x.T.astype(BF16)
    for c in range(x.shape[0] // tk):
        dst_ref[c] = x_t[:, c * tk:(c + 1) * tk]


def _moba_kernel(alibi_ref, qt_ref, ka_ref, vt_ref, km_ref, z_ref, o_ref, m_s, acc_s, sc0_s, sc1_s):
    i = pl.program_id(1)
    nblk = vt_ref.shape[1]
    tq = MOBA_BLOCK
    hd = HEAD_DIM
    chains = [(r, j) for r in range(qt_ref.shape[0]) for j in range(N_HEADS)]
    slots = range(len(chains))

    blk = _iota((nblk, tq), 0)
    qas = []
    for r, j in chains:
        qn = qt_ref[r, j * hd:(j + 1) * hd, :]
        km = jnp.concatenate([km_ref[r, t, :, j * hd:(j + 1) * hd] for t in range(km_ref.shape[1])], axis=0)
        gate = jnp.where(blk < i, _dot_fine(km, qn), NEG)
        chosen = _rank_select(gate, blk, nblk, MOBA_TOPK) & (blk < i)
        sel_bias = jnp.where(chosen | (blk == i), 0.0, -BIG)
        qas.append(_augment_query(qn, alibi_ref[j], sel_bias, flipped=j % 2 == 1))

    def scores(slot, n):
        r, j = chains[slot]
        return _dot(ka_ref[r, j, pl.ds(pl.multiple_of(n * tq, tq), tq), :], qas[slot])

    def values(slot, n):
        r, j = chains[slot]
        return vt_ref[r, n, j * hd:(j + 1) * hd, :]

    state = (m_s, acc_s)
    _flash_reset(state)
    bufs = (sc0_s, sc1_s)
    for slot in slots:
        sc0_s[slot] = scores(slot, 0)

    def body(n, carry):
        _flash_update_pipelined(state, bufs, slots, n, lambda slot: scores(slot, n + 1),
                                lambda slot: values(slot, n))
        return carry

    lax.fori_loop(0, i, body, 0)
    diag_cap = _causal_cap(tq, tq, 1)
    _flash_update_pipelined(state, bufs, slots, i, None, lambda slot: values(slot, i), diag_cap)
    for r in range(qt_ref.shape[0]):
        outs = [_flash_result(state, r * N_HEADS + j) for j in range(N_HEADS)]
        y = jnp.concatenate(outs, axis=0).T * _silu(z_ref[r])
        o_ref[r] = y.astype(o_ref.dtype)


def _moba(proj, q_t, ka, vt, km, *, rows_per_step=2):
    bsz, s, _ = proj.shape
    tq = MOBA_BLOCK
    gw = GROUP_WIDTH
    nblk = s // MOBA_BLOCK
    rps = rows_per_step
    assert SEL_FEAT0 + nblk <= HEAD_DIM and bsz % rps == 0
    whole = lambda a: pl.BlockSpec((rps,) + a.shape[1:], lambda b, i: (b,) + (0,) * (a.ndim - 1))
    return pl.pallas_call(
        _moba_kernel,
        grid=(bsz // rps, s // tq),
        in_specs=[pl.BlockSpec((N_HEADS, 2 * SLOPE_PIECES, tq), lambda b, i: (0, 0, 0)),
                  pl.BlockSpec((rps, gw, tq), lambda b, i: (b, 0, i)), whole(ka), whole(vt), whole(km),
                  pl.BlockSpec((rps, tq, gw), lambda b, i: (b, i, 6))],
        out_specs=pl.BlockSpec((rps, tq, gw), lambda b, i: (b, i, 0)),
        out_shape=jax.ShapeDtypeStruct((bsz, s, gw), BF16),
        scratch_shapes=_flash_scratch(rps * N_HEADS, tq) + [pltpu.VMEM((rps * N_HEADS, tq, tq), F32)] * 2,
        name="moba",
    )(jnp.asarray(_alibi_query_features(tq)), q_t, ka, vt, km, proj)


def _cmp_kernel(xk_ref, xv_ref, wk1_ref, wv1_ref, pk_ref, pv_ref, bk1_ref, bv1_ref,
                wk2_ref, wv2_ref, bk2_ref, bv2_ref, gn_ref, ko_ref, vo_ref, wke_s, wve_s):
    half = xk_ref.shape[2]
    n = xk_ref.shape[1]
    hd = HEAD_DIM
    nh = NSA_CMP_HIDDEN

    @pl.when(pl.program_id(0) == 0)
    def _():
        for w_ref, we_ref in ((wk1_ref, wke_s), (wv1_ref, wve_s)):
            we_ref[...] = jnp.zeros(we_ref.shape, BF16)
            for tok in range(NSA_CMP_LEN):
                for g in range(KV_HEADS):
                    row = (tok * KV_HEADS + g) * hd
                    we_ref[row:row + hd, g * nh:(g + 1) * nh] = w_ref[tok * hd:(tok + 1) * hd, :]

    def hidden(x_ref, we_ref, pos_ref, b1_ref):
        x = x_ref[0].astype(BF16)
        first = _dot(x, we_ref[:half, :])
        second = _dot(x, we_ref[half:, :])
        pos = jnp.broadcast_to(pos_ref[...], (8, 2 * half)).astype(BF16)
        ph = _dot(pos[:, :half], we_ref[:half, :]) + _dot(pos[:, half:], we_ref[half:, :])
        return _silu(first + pltpu.roll(second, n - 1, 0) + ph[0:1, :] + b1_ref[...])

    hk = hidden(xk_ref, wke_s, pk_ref, bk1_ref).astype(BF16)
    hv = hidden(xv_ref, wve_s, pv_ref, bv1_ref).astype(BF16)
    nh = NSA_CMP_HIDDEN
    ks, vs = [], []
    for g in range(KV_HEADS):
        ks.append(_rms(_dot(hk[:, g * nh:(g + 1) * nh], wk2_ref[...]) + bk2_ref[...], gn_ref[...]))
        vs.append(_dot(hv[:, g * nh:(g + 1) * nh], wv2_ref[...]) + bv2_ref[...])
    ko_ref[0] = jnp.concatenate(ks, axis=1)
    vo_ref[0] = jnp.concatenate(vs, axis=1)


def _compress(xk, xv, pos_k, pos_v, kw1, kb1, kw2, kb2, vw1, vb1, vw2, vb2, gn):
    bsz, n, half = xk.shape
    hd = HEAD_DIM
    tile_pos = lambda p: jnp.tile(p[:, None, :], (1, KV_HEADS, 1)).reshape(1, 2 * half)
    tile_b = lambda b: jnp.tile(b.reshape(1, -1), (1, KV_HEADS))
    const = lambda shape: pl.BlockSpec(shape, lambda b: (0,) * len(shape))
    xs = pl.BlockSpec((1, n, half), lambda b: (b, 0, 0))
    hw = KV_HEADS * NSA_CMP_HIDDEN
    out = pl.BlockSpec((1, n, KV_HEADS * hd), lambda b: (b, 0, 0))
    expanded = pltpu.VMEM((2 * half, hw), BF16)
    return pl.pallas_call(
        _cmp_kernel,
        grid=(bsz,),
        in_specs=[xs, xs, const(kw1.shape), const(vw1.shape),
                  const((1, 2 * half)), const((1, 2 * half)), const((1, hw)), const((1, hw)),
                  const((NSA_CMP_HIDDEN, hd)), const((NSA_CMP_HIDDEN, hd)), const((1, hd)), const((1, hd)),
                  const((1, hd))],
        out_specs=[out, out],
        out_shape=[jax.ShapeDtypeStruct((bsz, n, KV_HEADS * hd), F32)] * 2,
        scratch_shapes=[expanded, expanded],
        name="nsa_compress",
    )(xk, xv, kw1.astype(BF16), vw1.astype(BF16), tile_pos(pos_k), tile_pos(pos_v), tile_b(kb1), tile_b(vb1),
      kw2.astype(BF16), vw2.astype(BF16), kb2.reshape(1, hd), vb2.reshape(1, hd), gn.reshape(1, hd))


def _group_query(q_t, alibi_ref, g, gq_col):
    hd = HEAD_DIM
    heads = range(Q_PER_KV * g, Q_PER_KV * (g + 1))
    qn = jnp.concatenate([_norm_query_t(q_t[h * hd:(h + 1) * hd], gq_col) for h in heads], axis=1)
    alibi = jnp.concatenate([alibi_ref[h] for h in heads], axis=1)
    return qn, alibi


def _augment_keys(k, gk_ref, dst_ref, row=0, sel_shift=None):
    assert KV_HEADS == 2
    pos = _iota(k.shape, 0)
    dst_ref[2 * row], dst_ref[2 * row + 1] = _augment_pair_keys(_pair_rms(k, gk_ref[...]), pos, sel_shift)


def _chunk_scores(ka_ref, vt_ref, g, c, qa, tq):
    off = pl.multiple_of(c * tq, tq)
    return _dot(ka_ref[g, pl.ds(off, tq), :], qa), vt_ref[c, g * HEAD_DIM:(g + 1) * HEAD_DIM, :]


def _window_first_chunk(ka_ref, vt_ref, g, far, qa, tq, edge_cap):
    sc, v_t = _chunk_scores(ka_ref, vt_ref, g, jnp.maximum(far, 0), qa, tq)
    return jnp.minimum(sc, jnp.minimum(edge_cap, jnp.where(far >= 0, BIG, -BIG))), v_t


def _regroup_heads(o_t, tq):
    return jnp.concatenate([o_t[:, k * tq:(k + 1) * tq] for k in range(Q_PER_KV)], axis=0)


def _nsa_kernel(alibi_ref, q_ref, kc_ref, vc_ref, ks_ref, vs_ref, kw_ref, vw_ref, gate_ref, z_ref,
                gq_ref, gks_ref, gkw_ref, o_ref, kca_s, vct_s, ksa_s, vst_s, kwa_s, vwt_s, m_s, acc_s,
                sc0_s, sc1_s):
    i = pl.program_id(1)
    rows = range(q_ref.shape[0])
    tq = q_ref.shape[1]
    s = ks_ref.shape[1]
    hd = HEAD_DIM
    ncmp = kc_ref.shape[1]
    nsb = s // NSA_SEL_BLOCK

    @pl.when(i == 0)
    def _():
        cend = _iota((ncmp, LANES), 0) * NSA_CMP_STRIDE + (NSA_CMP_LEN - 1)
        for r in rows:
            _augment_keys(ks_ref[r], gks_ref, ksa_s, r, NSA_SEL_BLOCK.bit_length() - 1)
            _augment_keys(kw_ref[r], gkw_ref, kwa_s, r)
            _store_chunked_t(vs_ref[r], vst_s.at[r], tq)
            _store_chunked_t(vw_ref[r], vwt_s.at[r], tq)
            kca_s[2 * r], kca_s[2 * r + 1] = _augment_pair_keys(kc_ref[r], cend)
            vct_s[r] = vc_ref[r].T.astype(BF16)

    t0 = i * tq
    diag_cap = _causal_cap(tq, tq, Q_PER_KV)
    edge_cap = _window_edge_cap(tq, tq, Q_PER_KV)
    cend =_iota((ncmp, tq), 0) * NSA_CMP_STRIDE + (NSA_CMP_LEN - 1)
    seen = _tile_lanes(cend <= t0 + _iota((ncmp, tq), 1), Q_PER_KV)
    cstart = _iota((nsb, ncmp), 1) * NSA_CMP_STRIDE
    bstart = _iota((nsb, ncmp), 0) * NSA_SEL_BLOCK
    overlap_t = jnp.where((cstart < bstart + NSA_SEL_BLOCK) & (cstart + NSA_CMP_LEN > bstart), 1.0, 0.0)
    blk = _iota((nsb, tq), 0)
    cur = (t0 + _iota((nsb, tq), 1)) >> (NSA_SEL_BLOCK.bit_length() - 1)
    forced = (blk == 0) | (blk == cur) | (blk == cur - 1)
    groups = range(KV_HEADS)
    far = i - NSA_WINDOW // tq
    lo = jnp.maximum(far + 1, 0)
    first_win = jnp.maximum(far, 0)

    state = (m_s, acc_s)
    kv = [(r, g) for r in rows for g in groups]
    sel_slots = tuple(range(len(kv)))
    win_slots = tuple(len(kv) + n for n in range(len(kv)))
    _flash_reset(state)
    q_ts = [q_ref[r].T for r in rows]
    queries = [_group_query(q_ts[r], alibi_ref, g, gq_ref[...]) for r, g in kv]
    qa_win = [_augment_query(qn, alibi, flipped=g == 1) for (r, g), (qn, alibi) in zip(kv, queries)]
    qa_sel = []

    def scores(slot, c):
        n = slot % len(kv)
        ka_ref, qa = (ksa_s, qa_sel) if slot in sel_slots else (kwa_s, qa_win)
        return _dot(ka_ref[n, pl.ds(pl.multiple_of(c * tq, tq), tq), :], qa[n])

    def values(slot, c):
        r, g = kv[slot % len(kv)]
        return (vst_s if slot in sel_slots else vwt_s)[r, c, g * hd:(g + 1) * hd, :]

    cmp_sc = [_dot(kca_s[n], qa_win[n]) for n in range(len(kv))]
    sc_win_first = [scores(slot, first_win) for slot in win_slots]
    o_cmp = []
    for n, (r, g) in enumerate(kv):
        qn, alibi = queries[n]
        sc = jnp.where(seen, cmp_sc[n], -BIG)
        e = jnp.where(seen, jnp.exp2(sc - jnp.max(sc, axis=0, keepdims=True)), 0.0)
        p_c = e / jnp.maximum(jnp.sum(e, axis=0, keepdims=True), TINY)
        o_cmp.append(_dot(vct_s[r, g * hd:(g + 1) * hd, :], p_c.astype(BF16)))

        psum = p_c[:, 0:tq]
        for k in range(1, Q_PER_KV):
            psum = psum + p_c[:, k * tq:(k + 1) * tq]
        imp = _dot_fine(overlap_t, psum, a_is_bf16_exact=True)
        imp = jnp.where(blk <= cur, jnp.where(forced, NSA_FORCE, imp), NEG)
        chosen = _rank_select(imp, blk, nsb, min(NSA_TOPN, nsb))
        sel_bias = _tile_lanes(jnp.where(chosen, 0.0, -BIG), Q_PER_KV)
        qa_sel.append(_augment_query(qn, alibi, sel_bias, flipped=g == 1))

    bufs = (sc0_s, sc1_s)
    for slot in sel_slots:
        sc0_s[slot] = scores(slot, 0)
    for slot, sc in zip(win_slots, sc_win_first):
        sc0_s[slot] = sc
        sc1_s[slot] = sc

    def body(slots, cap, cap_slots, c, carry):
        _flash_update_pipelined(state, bufs, slots, c, lambda slot: scores(slot, c + 1),
                                lambda slot: values(slot, c), cap, cap_slots)
        return carry

    both_slots = sel_slots + win_slots
    lax.fori_loop(0, first_win, functools.partial(body, sel_slots, None, None), 0)
    lax.fori_loop(first_win, lo, functools.partial(body, both_slots, edge_cap, win_slots), 0)
    lax.fori_loop(lo, i, functools.partial(body, both_slots, None, None), 0)
    _flash_update_pipelined(state, bufs, both_slots, i, None, lambda slot: values(slot, i), diag_cap)

    for r in rows:
        gates_t = _sigmoid(gate_ref[r]).T
        outs = []
        for g in groups:
            def gate_row(branch):
                parts = [gates_t[branch * N_HEADS + h:branch * N_HEADS + h + 1, :]
                         for h in range(Q_PER_KV * g, Q_PER_KV * (g + 1))]
                return jnp.concatenate(parts, axis=1)

            n = KV_HEADS * r + g
            o_slc = _flash_result(state, sel_slots[n])
            o_win = _flash_result(state, win_slots[n])
            o = gate_row(0) * o_cmp[n] + gate_row(1) * o_slc + gate_row(2) * o_win
            outs.append(_regroup_heads(o, tq))
        y = jnp.concatenate(outs, axis=0).T * _silu(z_ref[r])
        o_ref[r] = y.astype(o_ref.dtype)


def _nsa(proj, kcn, vc, gq, gks, gkw, *, tq=256, rows_per_step=2):
    bsz, s, _ = proj.shape
    hd = HEAD_DIM
    gw = GROUP_WIDTH
    ncmp = kcn.shape[1]
    rps = rows_per_step
    chains = 2 * KV_HEADS * rps
    assert SEL_FEAT0 + s // NSA_SEL_BLOCK <= hd and ncmp == LANES and bsz % rps == 0
    wide = lambda c0: pl.BlockSpec((rps, tq, gw), lambda b, i: (b, i, c0 // 4))
    full = lambda c0: pl.BlockSpec((rps, s, LANES), lambda b, i: (b, 0, c0))
    cmp = pl.BlockSpec((rps, ncmp, LANES), lambda b, i: (b, 0, 0))
    row = pl.BlockSpec((1, hd), lambda b, i: (0, 0))
    ka = pltpu.VMEM((KV_HEADS * rps, s, LANES), BF16)
    vt = pltpu.VMEM((rps, s // tq, LANES, tq), BF16)
    return pl.pallas_call(
        _nsa_kernel,
        grid=(bsz // rps, s // tq),
        in_specs=[pl.BlockSpec((N_HEADS, 2 * SLOPE_PIECES, tq), lambda b, i: (0, 0, 0)),
                  wide(ODD_QC), cmp, cmp, full(ODD_KS), full(ODD_VS), full(ODD_KW), full(ODD_VW),
                  pl.BlockSpec((rps, tq, LANES), lambda b, i: (b, i, ODD_GC)), wide(ODD_ZC),
                  pl.BlockSpec((hd, 1), lambda b, i: (0, 0)), row, row],
        out_specs=pl.BlockSpec((rps, tq, gw), lambda b, i: (b, i, 0)),
        out_shape=jax.ShapeDtypeStruct((bsz, s, gw), BF16),
        scratch_shapes=[pltpu.VMEM((KV_HEADS * rps, ncmp, LANES), BF16), pltpu.VMEM((rps, LANES, ncmp), BF16),
                        ka, vt, ka, vt] + _flash_scratch(chains, Q_PER_KV * tq)
                       + [pltpu.VMEM((chains, tq, Q_PER_KV * tq), F32)] * 2,
        name="nsa",
    )(jnp.asarray(_alibi_query_features(tq)), proj, kcn, vc, proj, proj, proj, proj, proj, proj,
      gq.reshape(hd, 1), gks.reshape(1, hd), gkw.reshape(1, hd))


def _swa_kernel(sinks_ref, alibi_ref, q_ref, k_ref, v_ref, z_ref, gq_ref, gk_ref, o_ref, ka_s, vt_s):
    step = pl.program_id(1)
    tq = SWA_WINDOW
    tiles = q_ref.shape[1] // tq
    s = k_ref.shape[1]

    @pl.when(step == 0)
    def _():
        _augment_keys(k_ref[0], gk_ref, ka_s)
        _store_chunked_t(v_ref[0], vt_s, tq)

    q_t = q_ref[0].T
    diag_cap = _causal_cap(tq, tq, Q_PER_KV)
    edge_cap = _window_edge_cap(tq, tq, Q_PER_KV)
    chains = [(u, g) for u in range(tiles) for g in range(KV_HEADS)]
    qas = [_augment_query(*_group_query(q_t[:, u * tq:(u + 1) * tq], alibi_ref, g, gq_ref[...]), flipped=g == 1)
           for u, g in chains]
    first = [_window_first_chunk(ka_s, vt_s, g, step * tiles + u - 1, qa, tq, edge_cap)
             for (u, g), qa in zip(chains, qas)]
    own = [_chunk_scores(ka_s, vt_s, g, step * tiles + u, qa, tq) for (u, g), qa in zip(chains, qas)]
    carries = _flash_steps([_flash_init(Q_PER_KV * tq) for _ in chains], first)
    carries = _flash_steps(carries, own, diag_cap)
    outs = []
    for (u, g), carry in zip(chains, carries):
        m, acc, l = _flash_split(carry)
        tpos = ((step * tiles + u) * tq + _iota((1, tq), 1)).astype(F32)
        sink = jnp.concatenate([sinks_ref[h] * LOG2E + (SLOPES[h] * LOG2E) * tpos
                                for h in range(Q_PER_KV * g, Q_PER_KV * (g + 1))], axis=1)
        mf = jnp.maximum(m, sink)
        alpha = jnp.exp2(m - mf)
        o = acc * alpha / jnp.maximum(l * alpha + jnp.exp2(sink - mf), TINY)
        outs.append(_regroup_heads(o, tq))
    o_t = jnp.concatenate([jnp.concatenate(outs[u * KV_HEADS:(u + 1) * KV_HEADS], axis=0) for u in range(tiles)],
                          axis=1)
    y = o_t.T * _silu(z_ref[0])
    o_ref[0] = y.astype(o_ref.dtype)


def _swa(proj, sinks, gq, gk, *, tiles_per_step=4):
    bsz, s, _ = proj.shape
    hd = HEAD_DIM
    gw = GROUP_WIDTH
    tq = SWA_WINDOW
    rows = tiles_per_step * tq
    wide = lambda c0: pl.BlockSpec((1, rows, gw), lambda b, i: (b, i, c0 // 4))
    full = lambda c0: pl.BlockSpec((1, s, LANES), lambda b, i: (b, 0, c0))
    return pl.pallas_call(
        _swa_kernel,
        grid=(bsz, s // rows),
        in_specs=[pl.BlockSpec(memory_space=pltpu.SMEM),
                  pl.BlockSpec((N_HEADS, 2 * SLOPE_PIECES, tq), lambda b, i: (0, 0, 0)),
                  wide(ODD_QD), full(ODD_KD), full(ODD_VD), wide(ODD_ZD),
                  pl.BlockSpec((hd, 1), lambda b, i: (0, 0)), pl.BlockSpec((1, hd), lambda b, i: (0, 0))],
        out_specs=pl.BlockSpec((1, rows, gw), lambda b, i: (b, i, 0)),
        out_shape=jax.ShapeDtypeStruct((bsz, s, gw), BF16),
        scratch_shapes=[pltpu.VMEM((KV_HEADS, s, LANES), BF16), pltpu.VMEM((s // tq, LANES, tq), BF16)],
        name="swa",
    )(sinks.astype(F32), jnp.asarray(_alibi_query_features(tq)), proj, proj, proj, proj,
      gq.reshape(hd, 1), gk.reshape(1, hd))


def _pack_odd_weight(w):
    gw, kw, ng = GROUP_WIDTH, KV_HEADS * HEAD_DIM, 3 * N_HEADS
    sizes = [gw, kw, kw, kw, kw, kw, kw, ng, gw, gw, kw, kw, gw]
    starts = np.concatenate([[0], np.cumsum(sizes)]).tolist()
    part = lambda k: w[:, starts[k]:starts[k + 1]]
    pad = jnp.zeros((w.shape[0], LANES - ng), w.dtype)
    order = [0, 8, 9, 12, 1, 2, 3, 4, 5, 6, 10, 11, 7]
    return jnp.concatenate([part(k) for k in order] + [pad], axis=1).astype(BF16)


def kernel(x, norm_g, w_out, e_w_in, a_conv_w, a_conv_b, a_ln_g, a_ln_b, b_qnorm_g, b_knorm_g, o_w_in, c_qnorm_g, c_knorm_cmp_g, c_knorm_slc_g, c_knorm_win_g, c_pos_k, c_pos_v, c_k_w1, c_k_b1, c_k_w2, c_k_b2, c_v_w1, c_v_b1, c_v_w2, c_v_b2, d_qnorm_g, d_knorm_g, d_sinks):
    bsz, s, d = x.shape
    m = bsz * s
    assert s % MOBA_BLOCK == 0 and d == 2 * GROUP_WIDTH
    x2 = x.reshape(m, d)

    proj, q_t, ka, vt, km = _inproj_moba(x2, norm_g[0], e_w_in[0].astype(BF16), b_qnorm_g[0], b_knorm_g[0],
                                         bsz=bsz, chunk=GROUP_WIDTH)
    proj = proj.reshape(bsz, s, -1)
    y_a = _conv_module(proj, a_conv_w[0], a_conv_b[0], a_ln_g[0], a_ln_b[0])
    y_b = _moba(proj, q_t, ka, vt, km)
    x2 = _outproj(x2, y_a.reshape(m, -1), y_b.reshape(m, -1), w_out[0].astype(BF16))

    proj, k_cmp, v_cmp = _inproj(x2, norm_g[1], _pack_odd_weight(o_w_in[0]), chunk=5 * LANES,
                                 copies=((ODD_KC * LANES, LANES), (ODD_VC * LANES, LANES)), group=NSA_CMP_STRIDE)
    proj = proj.reshape(bsz, s, ODD_COLS)
    rows16 = lambda t: t.reshape(bsz, s // NSA_CMP_STRIDE, NSA_CMP_STRIDE * LANES)
    kcn, vc = _compress(rows16(k_cmp), rows16(v_cmp), c_pos_k[0], c_pos_v[0], c_k_w1[0], c_k_b1[0],
                        c_k_w2[0], c_k_b2[0], c_v_w1[0], c_v_b1[0], c_v_w2[0], c_v_b2[0], c_knorm_cmp_g[0])
    y_c = _nsa(proj, kcn, vc, c_qnorm_g[0], c_knorm_slc_g[0], c_knorm_win_g[0])
    y_d = _swa(proj, d_sinks[0], d_qnorm_g[0], d_knorm_g[0])
    x2 = _outproj(x2, y_c.reshape(m, -1), y_d.reshape(m, -1), w_out[1].astype(BF16))
    return x2.reshape(bsz, s, d)
```

```python
import functools

import ml_dtypes
import numpy as np
import jax
import jax.numpy as jnp
from jax import lax
from jax.experimental import pallas as pl
from jax.experimental.pallas import tpu as pltpu

HEAD_DIM = 64
N_HEADS = 8
GROUP_WIDTH = N_HEADS * HEAD_DIM
CONV_WIDTH = 31
MOBA_BLOCK = 256
MOBA_TOPK = 3
KV_HEADS = 2
Q_PER_KV = N_HEADS // KV_HEADS
NSA_CMP_LEN = 32
NSA_CMP_STRIDE = 16
NSA_CMP_HIDDEN = 256
NSA_SEL_BLOCK = 64
NSA_TOPN = 16
NSA_WINDOW = 512
NSA_FORCE = 1e4
SWA_WINDOW = 128
EPS = 1e-6
NEG = -1e30
TINY = 1e-30
LANES = 128
SUBLANES = 8
CONV_HALO = 32

LOG2E = float(np.log2(np.e))
BIG = 2.0 ** 99
M_INIT = -1e38
POS_SPLIT_SHIFT = 8
SLOPE_PIECES = 4
SEL_FEAT0 = 2 * SLOPE_PIECES

F32 = jnp.float32
BF16 = jnp.bfloat16

ODD_QC, ODD_ZC, ODD_QD, ODD_ZD = 0, 4, 8, 12
ODD_KC, ODD_VC, ODD_KS, ODD_VS, ODD_KW, ODD_VW, ODD_KD, ODD_VD, ODD_GC = 16, 17, 18, 19, 20, 21, 22, 23, 24
ODD_COLS = 25 * LANES


def _alibi_slopes(n):
    return [float(2.0 ** (-8.0 * (i + 1) / n)) for i in range(n)]


SLOPES = _alibi_slopes(N_HEADS)


def _alibi_query_features(width):
    table = np.zeros((N_HEADS, 2 * SLOPE_PIECES, width), np.float32)
    for h, slope in enumerate(SLOPES):
        rest = np.float64(slope) * LOG2E
        for k in range(SLOPE_PIECES):
            piece = float(np.float32(rest).astype(ml_dtypes.bfloat16).astype(np.float32))
            table[h, k] = piece
            table[h, SLOPE_PIECES + k] = piece
            rest -= piece
    return table


def _dot(a, b):
    return jnp.dot(a, b, preferred_element_type=F32)


def _split_bf16(x, pieces):
    out = []
    for _ in range(pieces):
        out.append(x.astype(BF16))
        x = x - out[-1].astype(F32)
    return out


def _dot_fine(a, b, a_is_bf16_exact=False):
    if a_is_bf16_exact:
        return sum(_dot(a.astype(BF16), piece) for piece in _split_bf16(b, 3))
    (a_hi, a_lo), (b_hi, b_lo) = _split_bf16(a, 2), _split_bf16(b, 2)
    return _dot(a_hi, b_hi) + (_dot(a_hi, b_lo) + _dot(a_lo, b_hi))


def _sigmoid(x):
    return 1.0 / (1.0 + jnp.exp(-x))


def _silu(x):
    return x * _sigmoid(x)


def _rms(x, g):
    return x * lax.rsqrt(jnp.mean(x * x, axis=-1, keepdims=True) + EPS) * g


def _iota(shape, dim):
    return lax.broadcasted_iota(jnp.int32, shape, dim)


def _inproj_kernel(x_ref, g_ref, w_ref, o_ref, *rest, chunk, copies, group):
    copy_refs, stage_refs = rest[:len(copies)], rest[len(copies):]
    h = _rms(x_ref[...], g_ref[...]).astype(BF16)
    for c in range(o_ref.shape[1] // chunk):
        o_ref[:, c * chunk:(c + 1) * chunk] = _dot(h, w_ref[:, c * chunk:(c + 1) * chunk])
    for ref, stage, (start, width) in zip(copy_refs, stage_refs, copies):
        stage[...] = o_ref[:, start:start + width]
        for t in range(group):
            ref[:, t * width:(t + 1) * width] = stage[pl.ds(t, ref.shape[0], stride=group), :]


def _inproj(x2d, g, w, *, tm=512, chunk, copies=(), group=1):
    m, d = x2d.shape
    e = w.shape[1]
    rows = lambda n, width: pl.BlockSpec((n, width), lambda i: (i, 0))
    const = lambda shape: pl.BlockSpec(shape, lambda i: (0, 0))
    return pl.pallas_call(
        functools.partial(_inproj_kernel, chunk=chunk, copies=tuple(copies), group=group),
        grid=(m // tm,),
        in_specs=[rows(tm, d), const((1, d)), const((d, e))],
        out_specs=[rows(tm, e)] + [rows(tm // group, group * width) for _, width in copies],
        out_shape=[jax.ShapeDtypeStruct((m, e), F32)]
                  + [jax.ShapeDtypeStruct((m // group, group * width), F32) for _, width in copies],
        scratch_shapes=[pltpu.VMEM((tm, width), F32) for _, width in copies],
        name="inproj",
    )(x2d, g.reshape(1, d), w)


def _conv_rows(h_s, row0, rows, w_ref, b_ref, lg_ref, lb_ref):
    base = CONV_HALO - (CONV_WIDTH - 1)
    acc = None
    for b in range(SUBLANES):
        n = rows if b == 0 else rows + SUBLANES
        part = None
        for a in range((base + CONV_WIDTH - 1) // SUBLANES + 1):
            j = SUBLANES * a + b - base
            if 0 <= j < CONV_WIDTH:
                term = w_ref[j:j + 1, :] * h_s[pl.ds(row0 + SUBLANES * a, n), :]
                part = term if part is None else part + term
        part = part[b:b + rows]
        acc = part if acc is None else acc + part
    y = acc + b_ref[...]
    mu = jnp.mean(y, axis=-1, keepdims=True)
    yc = y - mu
    return _silu(yc * lax.rsqrt(jnp.mean(yc * yc, axis=-1, keepdims=True) + EPS) * lg_ref[...] + lb_ref[...])


def _even_layer_kernel(x_ref, g_ref, w_ref, gq_ref, gk_ref, cw_ref, cb_ref, lg_ref, lb_ref,
                       ya_ref, zb_ref, qt_ref, ka_ref, vt_ref, km_ref, hb_s, conv_s, sec_s, h_s, tail_s,
                       *, seq, rows):
    tm = x_ref.shape[0]
    hd, gw = HEAD_DIM, GROUP_WIDTH
    tile_in_seq = pl.program_id(0) % (seq // tm)
    hb_s[...] = _rms(x_ref[...], g_ref[...]).astype(BF16)
    for c in range(3):
        conv_s[:, c * gw:(c + 1) * gw] = _dot(hb_s[...], w_ref[:, c * gw:(c + 1) * gw])
    h_s[0:CONV_HALO, :] = jnp.where(tile_in_seq > 0, tail_s[...], 0.0)
    h_s[CONV_HALO:, :] = conv_s[:, 0:gw] * _sigmoid(conv_s[:, gw:2 * gw])
    tail_s[...] = h_s[tm:tm + CONV_HALO, :]

    pending = [(3, sec_s.at[0]), (4, sec_s.at[1]), (5, sec_s.at[2]), (6, zb_ref)]
    for c in range(tm // rows):
        y = _conv_rows(h_s, c * rows, rows, cw_ref, cb_ref, lg_ref, lb_ref)
        gate = _silu(conv_s[c * rows:(c + 1) * rows, 2 * gw:3 * gw])
        ya_ref[c * rows:(c + 1) * rows, :] = (y * gate).astype(ya_ref.dtype)
        if pending:
            sec, dst_ref = pending.pop(0)
            dst_ref[...] = _dot(hb_s[...], w_ref[:, sec * gw:(sec + 1) * gw])
    for sec, dst_ref in pending:
        dst_ref[...] = _dot(hb_s[...], w_ref[:, sec * gw:(sec + 1) * gw])
    q_s, k_s, v_s = sec_s.at[0], sec_s.at[1], sec_s.at[2]

    q_t = q_s[...].T
    for j in range(N_HEADS):
        qt_ref[0, j * hd:(j + 1) * hd, :] = _norm_query_t(q_t[j * hd:(j + 1) * hd], gq_ref[...])
    pos = tile_in_seq * tm + _iota((tm, LANES), 0)
    for pair in range(N_HEADS // 2):
        kn = _pair_rms(k_s[:, pair * LANES:(pair + 1) * LANES], gk_ref[...])
        ka_ref[0, 2 * pair], ka_ref[0, 2 * pair + 1] = _augment_pair_keys(kn, pos, MOBA_BLOCK.bit_length() - 1)
        km_ref[0, 0, :, pair * LANES:(pair + 1) * LANES] = jnp.mean(
            kn.reshape(tm // MOBA_BLOCK, MOBA_BLOCK, LANES), axis=1)
    v_t = v_s[...].T.astype(BF16)
    for c in range(tm // MOBA_BLOCK):
        vt_ref[0, c] = v_t[:, c * MOBA_BLOCK:(c + 1) * MOBA_BLOCK]


def _even_layer_front(x2d, g, w, gq, gk, conv_w, conv_b, ln_g, ln_b, *, bsz, tm=512, rows=128):
    m, d = x2d.shape
    e = w.shape[1]
    s = m // bsz
    per_b = s // tm
    hd, gw, nb = HEAD_DIM, GROUP_WIDTH, tm // MOBA_BLOCK
    assert s % tm == 0 and tm % MOBA_BLOCK == 0 and e == 7 * gw and tm % rows == 0
    const = lambda shape: pl.BlockSpec(shape, lambda i: (0, 0))
    tile = pl.BlockSpec((tm, gw), lambda i: (i, 0))
    vec = const((1, gw))
    return pl.pallas_call(
        functools.partial(_even_layer_kernel, seq=s, rows=rows),
        grid=(m // tm,),
        in_specs=[pl.BlockSpec((tm, d), lambda i: (i, 0)), const((1, d)), const((d, e)), const((hd, 1)),
                  const((1, hd)), const((CONV_WIDTH, gw)), vec, vec, vec],
        out_specs=[tile, tile,
                   pl.BlockSpec((1, gw, tm), lambda i: (i // per_b, 0, i % per_b)),
                   pl.BlockSpec((1, N_HEADS, tm, LANES), lambda i: (i // per_b, 0, i % per_b, 0)),
                   pl.BlockSpec((1, nb, gw, MOBA_BLOCK), lambda i: (i // per_b, i % per_b, 0, 0)),
                   pl.BlockSpec((1, 1, nb, gw), lambda i: (i // per_b, i % per_b, 0, 0))],
        out_shape=[jax.ShapeDtypeStruct((m, gw), BF16),
                   jax.ShapeDtypeStruct((m, gw), F32),
                   jax.ShapeDtypeStruct((bsz, gw, s), F32),
                   jax.ShapeDtypeStruct((bsz, N_HEADS, s, LANES), BF16),
                   jax.ShapeDtypeStruct((bsz, s // MOBA_BLOCK, gw, MOBA_BLOCK), BF16),
                   jax.ShapeDtypeStruct((bsz, per_b, nb, gw), F32)],
        scratch_shapes=[pltpu.VMEM((tm, d), BF16), pltpu.VMEM((tm, 3 * gw), F32), pltpu.VMEM((3, tm, gw), F32),
                        pltpu.VMEM((CONV_HALO + tm, gw), F32), pltpu.VMEM((CONV_HALO, gw), F32)],
        name="even_layer_front",
    )(x2d, g.reshape(1, d), w, gq.reshape(hd, 1), gk.reshape(1, hd), conv_w, conv_b.reshape(1, gw),
      ln_g.reshape(1, gw), ln_b.reshape(1, gw))


def _outproj_kernel(x_ref, ya_ref, yb_ref, wa_ref, wb_ref, o_ref):
    o_ref[...] = x_ref[...] + _dot(ya_ref[...], wa_ref[...]) + _dot(yb_ref[...], wb_ref[...])


def _outproj(x2d, ya, yb, w, *, tm=1024):
    m, d = x2d.shape
    gw = ya.shape[1]
    return pl.pallas_call(
        _outproj_kernel,
        grid=(m // tm,),
        in_specs=[pl.BlockSpec((tm, d), lambda i: (i, 0)),
                  pl.BlockSpec((tm, gw), lambda i: (i, 0)),
                  pl.BlockSpec((tm, gw), lambda i: (i, 0)),
                  pl.BlockSpec((gw, d), lambda i: (0, 0)),
                  pl.BlockSpec((gw, d), lambda i: (0, 0))],
        out_specs=pl.BlockSpec((tm, d), lambda i: (i, 0)),
        out_shape=jax.ShapeDtypeStruct((m, d), F32),
        name="outproj",
    )(x2d, ya, yb, w[:gw], w[gw:])


def _key_features(pos, flipped, sel_shift=None):
    col = _iota(pos.shape, 1) - (0 if flipped else HEAD_DIM)
    hi = (pos >> POS_SPLIT_SHIFT) << POS_SPLIT_SHIFT
    lo = pos & ((1 << POS_SPLIT_SHIFT) - 1)
    feat = jnp.where(col < SLOPE_PIECES, hi, jnp.where(col < 2 * SLOPE_PIECES, lo, 0))
    if sel_shift is not None:
        feat = jnp.where(col - SEL_FEAT0 == (pos >> sel_shift), 1, feat)
    return jnp.where((col >= 0) & (col < HEAD_DIM), feat, 0).astype(F32)


def _pair_rms(x, g):
    sq = x * x
    hi = sq.astype(BF16)
    lo = (sq - hi.astype(F32)).astype(BF16)
    same_head = (_iota((LANES, LANES), 0) >> 6) == (_iota((LANES, LANES), 1) >> 6)
    ones = jnp.where(same_head, 1.0, 0.0).astype(BF16)
    ss = _dot(hi, ones) + _dot(lo, ones)
    return x * lax.rsqrt(ss * (1.0 / HEAD_DIM) + EPS) * jnp.concatenate([g, g], axis=1)


def _augment_pair_keys(kn, pos, sel_shift=None):
    lane = _iota(kn.shape, 1)
    return [jnp.where(lane >= HEAD_DIM if flipped else lane < HEAD_DIM, kn,
                      _key_features(pos, flipped, sel_shift)).astype(BF16) for flipped in (False, True)]


def _norm_query_t(x, g_col):
    ss = jnp.mean(x * x, axis=0, keepdims=True)
    return x * lax.rsqrt(ss + EPS) * g_col * (HEAD_DIM ** -0.5 * LOG2E)


def _augment_query(q_t, alibi, sel_bias=None, flipped=False):
    n = q_t.shape[1]
    feats = [alibi]
    used = alibi.shape[0]
    if sel_bias is not None:
        feats.append(sel_bias)
        used += sel_bias.shape[0]
    feats.append(jnp.zeros((HEAD_DIM - used, n), F32))
    return jnp.concatenate(feats + [q_t] if flipped else [q_t] + feats, axis=0).astype(BF16)


ONES_ROWS = 16
ACC_ROWS = HEAD_DIM + ONES_ROWS


def _flash_init(n):
    return jnp.full((1, n), M_INIT, F32), jnp.zeros((ACC_ROWS, n), F32)


def _flash_step(carry, sc, v_t):
    m, acc = carry
    m_new = jnp.maximum(m, jnp.max(sc, axis=0, keepdims=True))
    p = jnp.exp2(sc - m_new).astype(BF16)
    v_ones = jnp.concatenate([v_t, jnp.ones((ONES_ROWS, v_t.shape[1]), BF16)], axis=0)
    return m_new, jnp.exp2(m - m_new) * acc + _dot(v_ones, p)


def _flash_split(carry):
    m, acc = carry
    return m, acc[:HEAD_DIM], acc[HEAD_DIM:HEAD_DIM + 1]


def _flash_steps(carries, chunks, cap=None):
    return tuple(_flash_step(carry, sc if cap is None else jnp.minimum(sc, cap), v_t)
                 for carry, (sc, v_t) in zip(carries, chunks))


def _flash_reset(state):
    m_ref, acc_ref = state
    m_ref[...] = jnp.full(m_ref.shape, M_INIT, F32)
    acc_ref[...] = jnp.zeros(acc_ref.shape, F32)


def _flash_update(state, slot, sc, v_t, cap=None):
    m_ref, acc_ref = state
    m_ref[slot], acc_ref[slot] = _flash_step((m_ref[slot], acc_ref[slot]),
                                             sc if cap is None else jnp.minimum(sc, cap), v_t)


def _flash_update_pipelined(state, bufs, slots, c, next_scores, values, cap=None, cap_slots=None):
    def run(src_ref, dst_ref):
        for slot in slots:
            fresh = None if next_scores is None else next_scores(slot)
            masked = cap is not None and (cap_slots is None or slot in cap_slots)
            _flash_update(state, slot, src_ref[slot], values(slot), cap if masked else None)
            if fresh is not None:
                dst_ref[slot] = fresh

    pl.when((c & 1) == 0)(lambda: run(bufs[0], bufs[1]))
    pl.when((c & 1) == 1)(lambda: run(bufs[1], bufs[0]))


def _flash_result(state, slot):
    m_ref, acc_ref = state
    m, out, den = _flash_split((m_ref[slot], acc_ref[slot]))
    return out / jnp.maximum(den, TINY)


def _flash_scratch(chains, n):
    return [pltpu.VMEM((chains, 1, n), F32), pltpu.VMEM((chains, ACC_ROWS, n), F32)]


def _rank_select(score, blk, limit, count):
    ranks = []
    for r0 in range(0, score.shape[0], SUBLANES):
        tile = score[r0:r0 + SUBLANES]
        tile_blk = r0 + _iota(tile.shape, 0)
        rank = jnp.zeros(tile.shape, jnp.int32)
        for mm in range(limit):
            gm = score[mm:mm + 1, :]
            if mm < r0:
                beats = gm >= tile
            elif mm >= r0 + SUBLANES:
                beats = gm > tile
            else:
                beats = (gm > tile) | ((gm == tile) & (mm < tile_blk))
            rank = rank + jnp.where(beats, 1, 0)
        ranks.append(rank)
    return jnp.concatenate(ranks, axis=0) < count


def _tile_lanes(x, reps):
    return jnp.concatenate([x] * reps, axis=1) if reps > 1 else x


def _causal_cap(tk, tq, reps):
    keep = _iota((tk, tq), 0) <= _iota((tk, tq), 1)
    return _tile_lanes(jnp.where(keep, BIG, -BIG), reps)


def _window_edge_cap(tk, tq, reps):
    keep = _iota((tk, tq), 0) > _iota((tk, tq), 1)
    return _tile_lanes(jnp.where(keep, BIG, -BIG), reps)


def _store_chunked_t(x, dst_ref, tk):
    x_t = x.T.astype(BF16)
    for c in range(x.shape[0] // tk):
        dst_ref[c] = x_t[:, c * tk:(c + 1) * tk]


def _moba_kernel(alibi_ref, qt_ref, ka_ref, vt_ref, km_ref, z_ref, o_ref, m_s, acc_s, sc0_s, sc1_s):
    i = pl.program_id(1)
    nblk = vt_ref.shape[1]
    tq = MOBA_BLOCK
    hd = HEAD_DIM
    chains = [(r, j) for r in range(qt_ref.shape[0]) for j in range(N_HEADS)]
    slots = range(len(chains))

    blk = _iota((nblk, tq), 0)
    qas = []
    for r, j in chains:
        qn = qt_ref[r, j * hd:(j + 1) * hd, :]
        km = jnp.concatenate([km_ref[r, t, :, j * hd:(j + 1) * hd] for t in range(km_ref.shape[1])], axis=0)
        gate = jnp.where(blk < i, _dot_fine(km, qn), NEG)
        chosen = _rank_select(gate, blk, nblk, MOBA_TOPK) & (blk < i)
        sel_bias = jnp.where(chosen | (blk == i), 0.0, -BIG)
        qas.append(_augment_query(qn, alibi_ref[j], sel_bias, flipped=j % 2 == 1))

    def scores(slot, n):
        r, j = chains[slot]
        return _dot(ka_ref[r, j, pl.ds(pl.multiple_of(n * tq, tq), tq), :], qas[slot])

    def values(slot, n):
        r, j = chains[slot]
        return vt_ref[r, n, j * hd:(j + 1) * hd, :]

    state = (m_s, acc_s)
    _flash_reset(state)
    bufs = (sc0_s, sc1_s)
    for slot in slots:
        sc0_s[slot] = scores(slot, 0)

    def body(n, carry):
        _flash_update_pipelined(state, bufs, slots, n, lambda slot: scores(slot, n + 1),
                                lambda slot: values(slot, n))
        return carry

    lax.fori_loop(0, i, body, 0)
    diag_cap = _causal_cap(tq, tq, 1)
    _flash_update_pipelined(state, bufs, slots, i, None, lambda slot: values(slot, i), diag_cap)
    for r in range(qt_ref.shape[0]):
        outs = [_flash_result(state, r * N_HEADS + j) for j in range(N_HEADS)]
        y = jnp.concatenate(outs, axis=0).T * _silu(z_ref[r])
        o_ref[r] = y.astype(o_ref.dtype)


def _moba(z, q_t, ka, vt, km, *, rows_per_step=2):
    bsz, s, _ = z.shape
    tq = MOBA_BLOCK
    gw = GROUP_WIDTH
    nblk = s // MOBA_BLOCK
    rps = rows_per_step
    assert SEL_FEAT0 + nblk <= HEAD_DIM and bsz % rps == 0
    whole = lambda a: pl.BlockSpec((rps,) + a.shape[1:], lambda b, i: (b,) + (0,) * (a.ndim - 1))
    return pl.pallas_call(
        _moba_kernel,
        grid=(bsz // rps, s // tq),
        in_specs=[pl.BlockSpec((N_HEADS, 2 * SLOPE_PIECES, tq), lambda b, i: (0, 0, 0)),
                  pl.BlockSpec((rps, gw, tq), lambda b, i: (b, 0, i)), whole(ka), whole(vt), whole(km),
                  pl.BlockSpec((rps, tq, gw), lambda b, i: (b, i, 0))],
        out_specs=pl.BlockSpec((rps, tq, gw), lambda b, i: (b, i, 0)),
        out_shape=jax.ShapeDtypeStruct((bsz, s, gw), BF16),
        scratch_shapes=_flash_scratch(rps * N_HEADS, tq) + [pltpu.VMEM((rps * N_HEADS, tq, tq), F32)] * 2,
        name="moba",
    )(jnp.asarray(_alibi_query_features(tq)), q_t, ka, vt, km, z)


def _cmp_kernel(xk_ref, xv_ref, wk1_ref, wv1_ref, pk_ref, pv_ref, bk1_ref, bv1_ref,
                wk2_ref, wv2_ref, bk2_ref, bv2_ref, gn_ref, ko_ref, vo_ref, wke_s, wve_s):
    half = xk_ref.shape[2]
    n = xk_ref.shape[1]
    hd = HEAD_DIM
    nh = NSA_CMP_HIDDEN

    @pl.when(pl.program_id(0) == 0)
    def _():
        for w_ref, we_ref in ((wk1_ref, wke_s), (wv1_ref, wve_s)):
            we_ref[...] = jnp.zeros(we_ref.shape, BF16)
            for tok in range(NSA_CMP_LEN):
                for g in range(KV_HEADS):
                    row = (tok * KV_HEADS + g) * hd
                    we_ref[row:row + hd, g * nh:(g + 1) * nh] = w_ref[tok * hd:(tok + 1) * hd, :]

    def hidden(x_ref, we_ref, pos_ref, b1_ref):
        x = x_ref[0].astype(BF16)
        first = _dot(x, we_ref[:half, :])
        second = _dot(x, we_ref[half:, :])
        pos = jnp.broadcast_to(pos_ref[...], (8, 2 * half)).astype(BF16)
        ph = _dot(pos[:, :half], we_ref[:half, :]) + _dot(pos[:, half:], we_ref[half:, :])
        return _silu(first + pltpu.roll(second, n - 1, 0) + ph[0:1, :] + b1_ref[...])

    hk = hidden(xk_ref, wke_s, pk_ref, bk1_ref).astype(BF16)
    hv = hidden(xv_ref, wve_s, pv_ref, bv1_ref).astype(BF16)
    nh = NSA_CMP_HIDDEN
    ks, vs = [], []
    for g in range(KV_HEADS):
        ks.append(_rms(_dot(hk[:, g * nh:(g + 1) * nh], wk2_ref[...]) + bk2_ref[...], gn_ref[...]))
        vs.append(_dot(hv[:, g * nh:(g + 1) * nh], wv2_ref[...]) + bv2_ref[...])
    ko_ref[0] = jnp.concatenate(ks, axis=1)
    vo_ref[0] = jnp.concatenate(vs, axis=1)


def _compress(xk, xv, pos_k, pos_v, kw1, kb1, kw2, kb2, vw1, vb1, vw2, vb2, gn):
    bsz, n, half = xk.shape
    hd = HEAD_DIM
    tile_pos = lambda p: jnp.tile(p[:, None, :], (1, KV_HEADS, 1)).reshape(1, 2 * half)
    tile_b = lambda b: jnp.tile(b.reshape(1, -1), (1, KV_HEADS))
    const = lambda shape: pl.BlockSpec(shape, lambda b: (0,) * len(shape))
    xs = pl.BlockSpec((1, n, half), lambda b: (b, 0, 0))
    hw = KV_HEADS * NSA_CMP_HIDDEN
    out = pl.BlockSpec((1, n, KV_HEADS * hd), lambda b: (b, 0, 0))
    expanded = pltpu.VMEM((2 * half, hw), BF16)
    return pl.pallas_call(
        _cmp_kernel,
        grid=(bsz,),
        in_specs=[xs, xs, const(kw1.shape), const(vw1.shape),
                  const((1, 2 * half)), const((1, 2 * half)), const((1, hw)), const((1, hw)),
                  const((NSA_CMP_HIDDEN, hd)), const((NSA_CMP_HIDDEN, hd)), const((1, hd)), const((1, hd)),
                  const((1, hd))],
        out_specs=[out, out],
        out_shape=[jax.ShapeDtypeStruct((bsz, n, KV_HEADS * hd), F32)] * 2,
        scratch_shapes=[expanded, expanded],
        name="nsa_compress",
    )(xk, xv, kw1.astype(BF16), vw1.astype(BF16), tile_pos(pos_k), tile_pos(pos_v), tile_b(kb1), tile_b(vb1),
      kw2.astype(BF16), vw2.astype(BF16), kb2.reshape(1, hd), vb2.reshape(1, hd), gn.reshape(1, hd))


def _group_query(q_t, alibi_ref, g, gq_col):
    hd = HEAD_DIM
    heads = range(Q_PER_KV * g, Q_PER_KV * (g + 1))
    qn = jnp.concatenate([_norm_query_t(q_t[h * hd:(h + 1) * hd], gq_col) for h in heads], axis=1)
    alibi = jnp.concatenate([alibi_ref[h] for h in heads], axis=1)
    return qn, alibi


def _augment_keys(k, gk_ref, dst_ref, row=0, sel_shift=None):
    assert KV_HEADS == 2
    pos = _iota(k.shape, 0)
    dst_ref[2 * row], dst_ref[2 * row + 1] = _augment_pair_keys(_pair_rms(k, gk_ref[...]), pos, sel_shift)


def _chunk_scores(ka_ref, vt_ref, g, c, qa, tq):
    off = pl.multiple_of(c * tq, tq)
    return _dot(ka_ref[g, pl.ds(off, tq), :], qa), vt_ref[c, g * HEAD_DIM:(g + 1) * HEAD_DIM, :]


def _window_first_chunk(ka_ref, vt_ref, g, far, qa, tq, edge_cap):
    sc, v_t = _chunk_scores(ka_ref, vt_ref, g, jnp.maximum(far, 0), qa, tq)
    return jnp.minimum(sc, jnp.minimum(edge_cap, jnp.where(far >= 0, BIG, -BIG))), v_t


def _regroup_heads(o_t, tq):
    return jnp.concatenate([o_t[:, k * tq:(k + 1) * tq] for k in range(Q_PER_KV)], axis=0)


def _nsa_kernel(alibi_ref, q_ref, kc_ref, vc_ref, ks_ref, vs_ref, kw_ref, vw_ref, gate_ref, z_ref,
                gq_ref, gks_ref, gkw_ref, o_ref, kca_s, vct_s, ksa_s, vst_s, kwa_s, vwt_s, m_s, acc_s,
                sc0_s, sc1_s):
    i = pl.program_id(1)
    rows = range(q_ref.shape[0])
    tq = q_ref.shape[1]
    s = ks_ref.shape[1]
    hd = HEAD_DIM
    ncmp = kc_ref.shape[1]
    nsb = s // NSA_SEL_BLOCK

    @pl.when(i == 0)
    def _():
        cend = _iota((ncmp, LANES), 0) * NSA_CMP_STRIDE + (NSA_CMP_LEN - 1)
        for r in rows:
            _augment_keys(ks_ref[r], gks_ref, ksa_s, r, NSA_SEL_BLOCK.bit_length() - 1)
            _augment_keys(kw_ref[r], gkw_ref, kwa_s, r)
            _store_chunked_t(vs_ref[r], vst_s.at[r], tq)
            _store_chunked_t(vw_ref[r], vwt_s.at[r], tq)
            kca_s[2 * r], kca_s[2 * r + 1] = _augment_pair_keys(kc_ref[r], cend)
            vct_s[r] = vc_ref[r].T.astype(BF16)

    t0 = i * tq
    diag_cap = _causal_cap(tq, tq, Q_PER_KV)
    edge_cap = _window_edge_cap(tq, tq, Q_PER_KV)
    cend =_iota((ncmp, tq), 0) * NSA_CMP_STRIDE + (NSA_CMP_LEN - 1)
    seen = _tile_lanes(cend <= t0 + _iota((ncmp, tq), 1), Q_PER_KV)
    cstart = _iota((nsb, ncmp), 1) * NSA_CMP_STRIDE
    bstart = _iota((nsb, ncmp), 0) * NSA_SEL_BLOCK
    overlap_t = jnp.where((cstart < bstart + NSA_SEL_BLOCK) & (cstart + NSA_CMP_LEN > bstart), 1.0, 0.0)
    blk = _iota((nsb, tq), 0)
    cur = (t0 + _iota((nsb, tq), 1)) >> (NSA_SEL_BLOCK.bit_length() - 1)
    forced = (blk == 0) | (blk == cur) | (blk == cur - 1)
    groups = range(KV_HEADS)
    far = i - NSA_WINDOW // tq
    lo = jnp.maximum(far + 1, 0)
    first_win = jnp.maximum(far, 0)

    state = (m_s, acc_s)
    kv = [(r, g) for r in rows for g in groups]
    sel_slots = tuple(range(len(kv)))
    win_slots = tuple(len(kv) + n for n in range(len(kv)))
    _flash_reset(state)
    q_ts = [q_ref[r].T for r in rows]
    queries = [_group_query(q_ts[r], alibi_ref, g, gq_ref[...]) for r, g in kv]
    qa_win = [_augment_query(qn, alibi, flipped=g == 1) for (r, g), (qn, alibi) in zip(kv, queries)]
    qa_sel = []

    def scores(slot, c):
        n = slot % len(kv)
        ka_ref, qa = (ksa_s, qa_sel) if slot in sel_slots else (kwa_s, qa_win)
        return _dot(ka_ref[n, pl.ds(pl.multiple_of(c * tq, tq), tq), :], qa[n])

    def values(slot, c):
        r, g = kv[slot % len(kv)]
        return (vst_s if slot in sel_slots else vwt_s)[r, c, g * hd:(g + 1) * hd, :]

    cmp_sc = [_dot(kca_s[n], qa_win[n]) for n in range(len(kv))]
    sc_win_first = [scores(slot, first_win) for slot in win_slots]
    o_cmp = []
    for n, (r, g) in enumerate(kv):
        qn, alibi = queries[n]
        sc = jnp.where(seen, cmp_sc[n], -BIG)
        e = jnp.where(seen, jnp.exp2(sc - jnp.max(sc, axis=0, keepdims=True)), 0.0)
        p_c = e / jnp.maximum(jnp.sum(e, axis=0, keepdims=True), TINY)
        o_cmp.append(_dot(vct_s[r, g * hd:(g + 1) * hd, :], p_c.astype(BF16)))

        psum = p_c[:, 0:tq]
        for k in range(1, Q_PER_KV):
            psum = psum + p_c[:, k * tq:(k + 1) * tq]
        imp = _dot_fine(overlap_t, psum, a_is_bf16_exact=True)
        imp = jnp.where(blk <= cur, jnp.where(forced, NSA_FORCE, imp), NEG)
        chosen = _rank_select(imp, blk, nsb, min(NSA_TOPN, nsb))
        sel_bias = _tile_lanes(jnp.where(chosen, 0.0, -BIG), Q_PER_KV)
        qa_sel.append(_augment_query(qn, alibi, sel_bias, flipped=g == 1))

    bufs = (sc0_s, sc1_s)
    for slot in sel_slots:
        sc0_s[slot] = scores(slot, 0)
    for slot, sc in zip(win_slots, sc_win_first):
        sc0_s[slot] = sc
        sc1_s[slot] = sc

    def body(slots, cap, cap_slots, c, carry):
        _flash_update_pipelined(state, bufs, slots, c, lambda slot: scores(slot, c + 1),
                                lambda slot: values(slot, c), cap, cap_slots)
        return carry

    both_slots = sel_slots + win_slots
    lax.fori_loop(0, first_win, functools.partial(body, sel_slots, None, None), 0)
    lax.fori_loop(first_win, lo, functools.partial(body, both_slots, edge_cap, win_slots), 0)
    lax.fori_loop(lo, i, functools.partial(body, both_slots, None, None), 0)
    _flash_update_pipelined(state, bufs, both_slots, i, None, lambda slot: values(slot, i), diag_cap)

    for r in rows:
        gates_t = _sigmoid(gate_ref[r]).T
        outs = []
        for g in groups:
            def gate_row(branch):
                parts = [gates_t[branch * N_HEADS + h:branch * N_HEADS + h + 1, :]
                         for h in range(Q_PER_KV * g, Q_PER_KV * (g + 1))]
                return jnp.concatenate(parts, axis=1)

            n = KV_HEADS * r + g
            o_slc = _flash_result(state, sel_slots[n])
            o_win = _flash_result(state, win_slots[n])
            o = gate_row(0) * o_cmp[n] + gate_row(1) * o_slc + gate_row(2) * o_win
            outs.append(_regroup_heads(o, tq))
        y = jnp.concatenate(outs, axis=0).T * _silu(z_ref[r])
        o_ref[r] = y.astype(o_ref.dtype)


def _nsa(proj, kcn, vc, gq, gks, gkw, *, tq=256, rows_per_step=2):
    bsz, s, _ = proj.shape
    hd = HEAD_DIM
    gw = GROUP_WIDTH
    ncmp = kcn.shape[1]
    rps = rows_per_step
    chains = 2 * KV_HEADS * rps
    assert SEL_FEAT0 + s // NSA_SEL_BLOCK <= hd and ncmp == LANES and bsz % rps == 0
    wide = lambda c0: pl.BlockSpec((rps, tq, gw), lambda b, i: (b, i, c0 // 4))
    full = lambda c0: pl.BlockSpec((rps, s, LANES), lambda b, i: (b, 0, c0))
    cmp = pl.BlockSpec((rps, ncmp, LANES), lambda b, i: (b, 0, 0))
    row = pl.BlockSpec((1, hd), lambda b, i: (0, 0))
    ka = pltpu.VMEM((KV_HEADS * rps, s, LANES), BF16)
    vt = pltpu.VMEM((rps, s // tq, LANES, tq), BF16)
    return pl.pallas_call(
        _nsa_kernel,
        grid=(bsz // rps, s // tq),
        in_specs=[pl.BlockSpec((N_HEADS, 2 * SLOPE_PIECES, tq), lambda b, i: (0, 0, 0)),
                  wide(ODD_QC), cmp, cmp, full(ODD_KS), full(ODD_VS), full(ODD_KW), full(ODD_VW),
                  pl.BlockSpec((rps, tq, LANES), lambda b, i: (b, i, ODD_GC)), wide(ODD_ZC),
                  pl.BlockSpec((hd, 1), lambda b, i: (0, 0)), row, row],
        out_specs=pl.BlockSpec((rps, tq, gw), lambda b, i: (b, i, 0)),
        out_shape=jax.ShapeDtypeStruct((bsz, s, gw), BF16),
        scratch_shapes=[pltpu.VMEM((KV_HEADS * rps, ncmp, LANES), BF16), pltpu.VMEM((rps, LANES, ncmp), BF16),
                        ka, vt, ka, vt] + _flash_scratch(chains, Q_PER_KV * tq)
                       + [pltpu.VMEM((chains, tq, Q_PER_KV * tq), F32)] * 2,
        name="nsa",
    )(jnp.asarray(_alibi_query_features(tq)), proj, kcn, vc, proj, proj, proj, proj, proj, proj,
      gq.reshape(hd, 1), gks.reshape(1, hd), gkw.reshape(1, hd))


def _swa_kernel(sinks_ref, alibi_ref, q_ref, k_ref, v_ref, z_ref, gq_ref, gk_ref, o_ref, ka_s, vt_s):
    step = pl.program_id(1)
    tq = SWA_WINDOW
    tiles = q_ref.shape[1] // tq
    s = k_ref.shape[1]

    @pl.when(step == 0)
    def _():
        _augment_keys(k_ref[0], gk_ref, ka_s)
        _store_chunked_t(v_ref[0], vt_s, tq)

    q_t = q_ref[0].T
    diag_cap = _causal_cap(tq, tq, Q_PER_KV)
    edge_cap = _window_edge_cap(tq, tq, Q_PER_KV)
    chains = [(u, g) for u in range(tiles) for g in range(KV_HEADS)]
    qas = [_augment_query(*_group_query(q_t[:, u * tq:(u + 1) * tq], alibi_ref, g, gq_ref[...]), flipped=g == 1)
           for u, g in chains]
    first = [_window_first_chunk(ka_s, vt_s, g, step * tiles + u - 1, qa, tq, edge_cap)
             for (u, g), qa in zip(chains, qas)]
    own = [_chunk_scores(ka_s, vt_s, g, step * tiles + u, qa, tq) for (u, g), qa in zip(chains, qas)]
    carries = _flash_steps([_flash_init(Q_PER_KV * tq) for _ in chains], first)
    carries = _flash_steps(carries, own, diag_cap)
    outs = []
    for (u, g), carry in zip(chains, carries):
        m, acc, l = _flash_split(carry)
        tpos = ((step * tiles + u) * tq + _iota((1, tq), 1)).astype(F32)
        sink = jnp.concatenate([sinks_ref[h] * LOG2E + (SLOPES[h] * LOG2E) * tpos
                                for h in range(Q_PER_KV * g, Q_PER_KV * (g + 1))], axis=1)
        mf = jnp.maximum(m, sink)
        alpha = jnp.exp2(m - mf)
        o = acc * alpha / jnp.maximum(l * alpha + jnp.exp2(sink - mf), TINY)
        outs.append(_regroup_heads(o, tq))
    o_t = jnp.concatenate([jnp.concatenate(outs[u * KV_HEADS:(u + 1) * KV_HEADS], axis=0) for u in range(tiles)],
                          axis=1)
    y = o_t.T * _silu(z_ref[0])
    o_ref[0] = y.astype(o_ref.dtype)


def _swa(proj, sinks, gq, gk, *, tiles_per_step=4):
    bsz, s, _ = proj.shape
    hd = HEAD_DIM
    gw = GROUP_WIDTH
    tq = SWA_WINDOW
    rows = tiles_per_step * tq
    wide = lambda c0: pl.BlockSpec((1, rows, gw), lambda b, i: (b, i, c0 // 4))
    full = lambda c0: pl.BlockSpec((1, s, LANES), lambda b, i: (b, 0, c0))
    return pl.pallas_call(
        _swa_kernel,
        grid=(bsz, s // rows),
        in_specs=[pl.BlockSpec(memory_space=pltpu.SMEM),
                  pl.BlockSpec((N_HEADS, 2 * SLOPE_PIECES, tq), lambda b, i: (0, 0, 0)),
                  wide(ODD_QD), full(ODD_KD), full(ODD_VD), wide(ODD_ZD),
                  pl.BlockSpec((hd, 1), lambda b, i: (0, 0)), pl.BlockSpec((1, hd), lambda b, i: (0, 0))],
        out_specs=pl.BlockSpec((1, rows, gw), lambda b, i: (b, i, 0)),
        out_shape=jax.ShapeDtypeStruct((bsz, s, gw), BF16),
        scratch_shapes=[pltpu.VMEM((KV_HEADS, s, LANES), BF16), pltpu.VMEM((s // tq, LANES, tq), BF16)],
        name="swa",
    )(sinks.astype(F32), jnp.asarray(_alibi_query_features(tq)), proj, proj, proj, proj,
      gq.reshape(hd, 1), gk.reshape(1, hd))


def _pack_odd_weight(w):
    gw, kw, ng = GROUP_WIDTH, KV_HEADS * HEAD_DIM, 3 * N_HEADS
    sizes = [gw, kw, kw, kw, kw, kw, kw, ng, gw, gw, kw, kw, gw]
    starts = np.concatenate([[0], np.cumsum(sizes)]).tolist()
    part = lambda k: w[:, starts[k]:starts[k + 1]]
    pad = jnp.zeros((w.shape[0], LANES - ng), w.dtype)
    order = [0, 8, 9, 12, 1, 2, 3, 4, 5, 6, 10, 11, 7]
    return jnp.concatenate([part(k) for k in order] + [pad], axis=1).astype(BF16)


def kernel(x, norm_g, w_out, e_w_in, a_conv_w, a_conv_b, a_ln_g, a_ln_b, b_qnorm_g, b_knorm_g, o_w_in, c_qnorm_g, c_knorm_cmp_g, c_knorm_slc_g, c_knorm_win_g, c_pos_k, c_pos_v, c_k_w1, c_k_b1, c_k_w2, c_k_b2, c_v_w1, c_v_b1, c_v_w2, c_v_b2, d_qnorm_g, d_knorm_g, d_sinks):
    bsz, s, d = x.shape
    m = bsz * s
    assert s % MOBA_BLOCK == 0 and d == 2 * GROUP_WIDTH
    x2 = x.reshape(m, d)

    y_a, z_b, q_t, ka, vt, km = _even_layer_front(
        x2, norm_g[0], e_w_in[0].astype(BF16), b_qnorm_g[0], b_knorm_g[0],
        a_conv_w[0], a_conv_b[0], a_ln_g[0], a_ln_b[0], bsz=bsz)
    y_b = _moba(z_b.reshape(bsz, s, -1), q_t, ka, vt, km)
    x2 = _outproj(x2, y_a, y_b.reshape(m, -1), w_out[0].astype(BF16))

    proj, k_cmp, v_cmp = _inproj(x2, norm_g[1], _pack_odd_weight(o_w_in[0]), chunk=5 * LANES,
                                 copies=((ODD_KC * LANES, LANES), (ODD_VC * LANES, LANES)), group=NSA_CMP_STRIDE)
    proj = proj.reshape(bsz, s, ODD_COLS)
    rows16 = lambda t: t.reshape(bsz, s // NSA_CMP_STRIDE, NSA_CMP_STRIDE * LANES)
    kcn, vc = _compress(rows16(k_cmp), rows16(v_cmp), c_pos_k[0], c_pos_v[0], c_k_w1[0], c_k_b1[0],
                        c_k_w2[0], c_k_b2[0], c_v_w1[0], c_v_b1[0], c_v_w2[0], c_v_b2[0], c_knorm_cmp_g[0])
    y_c = _nsa(proj, kcn, vc, c_qnorm_g[0], c_knorm_slc_g[0], c_knorm_win_g[0])
    y_d = _swa(proj, d_sinks[0], d_qnorm_g[0], d_knorm_g[0])
    x2 = _outproj(x2, y_c.reshape(m, -1), y_d.reshape(m, -1), w_out[1].astype(BF16))
    return x2.reshape(bsz, s, d)
```

```python
import functools

import ml_dtypes
import numpy as np
import jax
import jax.numpy as jnp
from jax import lax
from jax.experimental import pallas as pl
from jax.experimental.pallas import tpu as pltpu

HEAD_DIM = 64
N_HEADS = 8
GROUP_WIDTH = N_HEADS * HEAD_DIM
CONV_WIDTH = 31
MOBA_BLOCK = 256
MOBA_TOPK = 3
KV_HEADS = 2
Q_PER_KV = N_HEADS // KV_HEADS
NSA_CMP_LEN = 32
NSA_CMP_STRIDE = 16
NSA_CMP_HIDDEN = 256
NSA_SEL_BLOCK = 64
NSA_TOPN = 16
NSA_WINDOW = 512
NSA_FORCE = 1e4
SWA_WINDOW = 128
EPS = 1e-6
NEG = -1e30
TINY = 1e-30
LANES = 128
SUBLANES = 8
CONV_HALO = 32

LOG2E = float(np.log2(np.e))
BIG = 2.0 ** 99
M_INIT = -1e38
POS_SPLIT_SHIFT = 8
SLOPE_PIECES = 4
SEL_FEAT0 = 2 * SLOPE_PIECES

F32 = jnp.float32
BF16 = jnp.bfloat16

ODD_QC, ODD_ZC, ODD_QD, ODD_ZD = 0, 4, 8, 12
ODD_KC, ODD_VC, ODD_KS, ODD_VS, ODD_KW, ODD_VW, ODD_KD, ODD_VD, ODD_GC = 16, 17, 18, 19, 20, 21, 22, 23, 24
ODD_COLS = 25 * LANES


def _alibi_slopes(n):
    return [float(2.0 ** (-8.0 * (i + 1) / n)) for i in range(n)]


SLOPES = _alibi_slopes(N_HEADS)


def _alibi_query_features(width):
    table = np.zeros((N_HEADS, 2 * SLOPE_PIECES, width), np.float32)
    for h, slope in enumerate(SLOPES):
        rest = np.float64(slope) * LOG2E
        for k in range(SLOPE_PIECES):
            piece = float(np.float32(rest).astype(ml_dtypes.bfloat16).astype(np.float32))
            table[h, k] = piece
            table[h, SLOPE_PIECES + k] = piece
            rest -= piece
    return table


def _dot(a, b):
    return jnp.dot(a, b, preferred_element_type=F32)


def _split_bf16(x, pieces):
    out = []
    for _ in range(pieces):
        out.append(x.astype(BF16))
        x = x - out[-1].astype(F32)
    return out


def _dot_fine(a, b, a_is_bf16_exact=False):
    if a_is_bf16_exact:
        return sum(_dot(a.astype(BF16), piece) for piece in _split_bf16(b, 3))
    (a_hi, a_lo), (b_hi, b_lo) = _split_bf16(a, 2), _split_bf16(b, 2)
    return _dot(a_hi, b_hi) + (_dot(a_hi, b_lo) + _dot(a_lo, b_hi))


def _sigmoid(x):
    return 1.0 / (1.0 + jnp.exp(-x))


def _silu(x):
    return x * _sigmoid(x)


def _rms(x, g):
    return x * lax.rsqrt(jnp.mean(x * x, axis=-1, keepdims=True) + EPS) * g


def _iota(shape, dim):
    return lax.broadcasted_iota(jnp.int32, shape, dim)


def _layer_boundary_kernel(x_ref, ya_ref, yb_ref, wa_ref, wb_ref, g_ref, w_ref, x_out_ref, o_ref, *rest,
                           chunk, copies, group):
    copy_refs, stage_refs = rest[:len(copies)], rest[len(copies):]
    x = x_ref[...] + _dot(ya_ref[...], wa_ref[...]) + _dot(yb_ref[...], wb_ref[...])
    x_out_ref[...] = x
    h = _rms(x, g_ref[...]).astype(BF16)
    for c in range(o_ref.shape[1] // chunk):
        o_ref[:, c * chunk:(c + 1) * chunk] = _dot(h, w_ref[:, c * chunk:(c + 1) * chunk])
    for ref, stage, (start, width) in zip(copy_refs, stage_refs, copies):
        stage[...] = o_ref[:, start:start + width]
        for t in range(group):
            ref[:, t * width:(t + 1) * width] = stage[pl.ds(t, ref.shape[0], stride=group), :]


def _layer_boundary(x2d, ya, yb, w_out, g, w, *, tm=512, chunk, copies=(), group=1):
    m, d = x2d.shape
    e = w.shape[1]
    gw = ya.shape[1]
    rows = lambda n, width: pl.BlockSpec((n, width), lambda i: (i, 0))
    const = lambda shape: pl.BlockSpec(shape, lambda i: (0, 0))
    return pl.pallas_call(
        functools.partial(_layer_boundary_kernel, chunk=chunk, copies=tuple(copies), group=group),
        grid=(m // tm,),
        in_specs=[rows(tm, d), rows(tm, gw), rows(tm, gw), const((gw, d)), const((gw, d)), const((1, d)),
                  const((d, e))],
        out_specs=[rows(tm, d), rows(tm, e)] + [rows(tm // group, group * width) for _, width in copies],
        out_shape=[jax.ShapeDtypeStruct((m, d), F32), jax.ShapeDtypeStruct((m, e), F32)]
                  + [jax.ShapeDtypeStruct((m // group, group * width), F32) for _, width in copies],
        scratch_shapes=[pltpu.VMEM((tm, width), F32) for _, width in copies],
        name="layer_boundary",
    )(x2d, ya, yb, w_out[:gw], w_out[gw:], g.reshape(1, d), w)


def _conv_rows(h_s, row0, rows, w_ref, b_ref, lg_ref, lb_ref):
    base = CONV_HALO - (CONV_WIDTH - 1)
    acc = None
    for b in range(SUBLANES):
        n = rows if b == 0 else rows + SUBLANES
        part = None
        for a in range((base + CONV_WIDTH - 1) // SUBLANES + 1):
            j = SUBLANES * a + b - base
            if 0 <= j < CONV_WIDTH:
                term = w_ref[j:j + 1, :] * h_s[pl.ds(row0 + SUBLANES * a, n), :]
                part = term if part is None else part + term
        part = part[b:b + rows]
        acc = part if acc is None else acc + part
    y = acc + b_ref[...]
    mu = jnp.mean(y, axis=-1, keepdims=True)
    yc = y - mu
    return _silu(yc * lax.rsqrt(jnp.mean(yc * yc, axis=-1, keepdims=True) + EPS) * lg_ref[...] + lb_ref[...])


def _even_layer_kernel(x_ref, g_ref, w_ref, gq_ref, gk_ref, cw_ref, cb_ref, lg_ref, lb_ref,
                       ya_ref, zb_ref, qt_ref, ka_ref, vt_ref, km_ref, hb_s, conv_s, sec_s, h_s, tail_s,
                       *, seq, rows):
    tm = x_ref.shape[0]
    hd, gw = HEAD_DIM, GROUP_WIDTH
    tile_in_seq = pl.program_id(0) % (seq // tm)
    hb_s[...] = _rms(x_ref[...], g_ref[...]).astype(BF16)
    for c in range(3):
        conv_s[:, c * gw:(c + 1) * gw] = _dot(hb_s[...], w_ref[:, c * gw:(c + 1) * gw])
    h_s[0:CONV_HALO, :] = jnp.where(tile_in_seq > 0, tail_s[...], 0.0)
    h_s[CONV_HALO:, :] = conv_s[:, 0:gw] * _sigmoid(conv_s[:, gw:2 * gw])
    tail_s[...] = h_s[tm:tm + CONV_HALO, :]

    def project(sec):
        return _dot(hb_s[...], w_ref[:, sec * gw:(sec + 1) * gw])

    def moba_queries():
        sec_s[...] = project(3)
        q_t = sec_s[...].T
        for j in range(N_HEADS):
            qt_ref[0, j * hd:(j + 1) * hd, :] = _norm_query_t(q_t[j * hd:(j + 1) * hd], gq_ref[...])

    def moba_keys():
        sec_s[...] = project(4)
        pos = tile_in_seq * tm + _iota((tm, LANES), 0)
        for pair in range(N_HEADS // 2):
            kn = _pair_rms(sec_s[:, pair * LANES:(pair + 1) * LANES], gk_ref[...])
            ka_ref[0, 2 * pair], ka_ref[0, 2 * pair + 1] = _augment_pair_keys(kn, pos,
                                                                              MOBA_BLOCK.bit_length() - 1)
            km_ref[0, 0, :, pair * LANES:(pair + 1) * LANES] = jnp.mean(
                kn.reshape(tm // MOBA_BLOCK, MOBA_BLOCK, LANES), axis=1)

    def moba_values():
        sec_s[...] = project(5)
        v_t = sec_s[...].T.astype(BF16)
        for c in range(tm // MOBA_BLOCK):
            vt_ref[0, c] = v_t[:, c * MOBA_BLOCK:(c + 1) * MOBA_BLOCK]

    def moba_gate_path():
        zb_ref[...] = project(6)

    others = [moba_queries, moba_keys, moba_values, moba_gate_path]
    for c in range(tm // rows):
        y = _conv_rows(h_s, c * rows, rows, cw_ref, cb_ref, lg_ref, lb_ref)
        gate = _silu(conv_s[c * rows:(c + 1) * rows, 2 * gw:3 * gw])
        ya_ref[c * rows:(c + 1) * rows, :] = (y * gate).astype(ya_ref.dtype)
        if others:
            others.pop(0)()
    for other in others:
        other()


def _even_layer_front(x2d, g, w, gq, gk, conv_w, conv_b, ln_g, ln_b, *, bsz, tm=512, rows=128):
    m, d = x2d.shape
    e = w.shape[1]
    s = m // bsz
    per_b = s // tm
    hd, gw, nb = HEAD_DIM, GROUP_WIDTH, tm // MOBA_BLOCK
    assert s % tm == 0 and tm % MOBA_BLOCK == 0 and e == 7 * gw and tm % rows == 0
    const = lambda shape: pl.BlockSpec(shape, lambda i: (0, 0))
    tile = pl.BlockSpec((tm, gw), lambda i: (i, 0))
    vec = const((1, gw))
    return pl.pallas_call(
        functools.partial(_even_layer_kernel, seq=s, rows=rows),
        grid=(m // tm,),
        in_specs=[pl.BlockSpec((tm, d), lambda i: (i, 0)), const((1, d)), const((d, e)), const((hd, 1)),
                  const((1, hd)), const((CONV_WIDTH, gw)), vec, vec, vec],
        out_specs=[tile, tile,
                   pl.BlockSpec((1, gw, tm), lambda i: (i // per_b, 0, i % per_b)),
                   pl.BlockSpec((1, N_HEADS, tm, LANES), lambda i: (i // per_b, 0, i % per_b, 0)),
                   pl.BlockSpec((1, nb, gw, MOBA_BLOCK), lambda i: (i // per_b, i % per_b, 0, 0)),
                   pl.BlockSpec((1, 1, nb, gw), lambda i: (i // per_b, i % per_b, 0, 0))],
        out_shape=[jax.ShapeDtypeStruct((m, gw), BF16),
                   jax.ShapeDtypeStruct((m, gw), F32),
                   jax.ShapeDtypeStruct((bsz, gw, s), F32),
                   jax.ShapeDtypeStruct((bsz, N_HEADS, s, LANES), BF16),
                   jax.ShapeDtypeStruct((bsz, s // MOBA_BLOCK, gw, MOBA_BLOCK), BF16),
                   jax.ShapeDtypeStruct((bsz, per_b, nb, gw), F32)],
        scratch_shapes=[pltpu.VMEM((tm, d), BF16), pltpu.VMEM((tm, 3 * gw), F32), pltpu.VMEM((tm, gw), F32),
                        pltpu.VMEM((CONV_HALO + tm, gw), F32), pltpu.VMEM((CONV_HALO, gw), F32)],
        name="even_layer_front",
    )(x2d, g.reshape(1, d), w, gq.reshape(hd, 1), gk.reshape(1, hd), conv_w, conv_b.reshape(1, gw),
      ln_g.reshape(1, gw), ln_b.reshape(1, gw))


def _outproj_kernel(x_ref, ya_ref, yb_ref, wa_ref, wb_ref, o_ref):
    o_ref[...] = x_ref[...] + _dot(ya_ref[...], wa_ref[...]) + _dot(yb_ref[...], wb_ref[...])


def _outproj(x2d, ya, yb, w, *, tm=1024):
    m, d = x2d.shape
    gw = ya.shape[1]
    return pl.pallas_call(
        _outproj_kernel,
        grid=(m // tm,),
        in_specs=[pl.BlockSpec((tm, d), lambda i: (i, 0)),
                  pl.BlockSpec((tm, gw), lambda i: (i, 0)),
                  pl.BlockSpec((tm, gw), lambda i: (i, 0)),
                  pl.BlockSpec((gw, d), lambda i: (0, 0)),
                  pl.BlockSpec((gw, d), lambda i: (0, 0))],
        out_specs=pl.BlockSpec((tm, d), lambda i: (i, 0)),
        out_shape=jax.ShapeDtypeStruct((m, d), F32),
        name="outproj",
    )(x2d, ya, yb, w[:gw], w[gw:])


def _key_features(pos, flipped, sel_shift=None):
    col = _iota(pos.shape, 1) - (0 if flipped else HEAD_DIM)
    hi = (pos >> POS_SPLIT_SHIFT) << POS_SPLIT_SHIFT
    lo = pos & ((1 << POS_SPLIT_SHIFT) - 1)
    feat = jnp.where(col < SLOPE_PIECES, hi, jnp.where(col < 2 * SLOPE_PIECES, lo, 0))
    if sel_shift is not None:
        feat = jnp.where(col - SEL_FEAT0 == (pos >> sel_shift), 1, feat)
    return jnp.where((col >= 0) & (col < HEAD_DIM), feat, 0).astype(F32)


def _pair_rms(x, g):
    sq = x * x
    hi = sq.astype(BF16)
    lo = (sq - hi.astype(F32)).astype(BF16)
    same_head = (_iota((LANES, LANES), 0) >> 6) == (_iota((LANES, LANES), 1) >> 6)
    ones = jnp.where(same_head, 1.0, 0.0).astype(BF16)
    ss = _dot(hi, ones) + _dot(lo, ones)
    return x * lax.rsqrt(ss * (1.0 / HEAD_DIM) + EPS) * jnp.concatenate([g, g], axis=1)


def _augment_pair_keys(kn, pos, sel_shift=None):
    lane = _iota(kn.shape, 1)
    return [jnp.where(lane >= HEAD_DIM if flipped else lane < HEAD_DIM, kn,
                      _key_features(pos, flipped, sel_shift)).astype(BF16) for flipped in (False, True)]


def _norm_query_t(x, g_col):
    ss = jnp.mean(x * x, axis=0, keepdims=True)
    return x * lax.rsqrt(ss + EPS) * g_col * (HEAD_DIM ** -0.5 * LOG2E)


def _augment_query(q_t, alibi, sel_bias=None, flipped=False):
    n = q_t.shape[1]
    feats = [alibi]
    used = alibi.shape[0]
    if sel_bias is not None:
        feats.append(sel_bias)
        used += sel_bias.shape[0]
    feats.append(jnp.zeros((HEAD_DIM - used, n), F32))
    return jnp.concatenate(feats + [q_t] if flipped else [q_t] + feats, axis=0).astype(BF16)


ONES_ROWS = 16
ACC_ROWS = HEAD_DIM + ONES_ROWS


def _flash_init(n):
    return jnp.full((1, n), M_INIT, F32), jnp.zeros((ACC_ROWS, n), F32)


def _flash_step(carry, sc, v_t):
    m, acc = carry
    m_new = jnp.maximum(m, jnp.max(sc, axis=0, keepdims=True))
    p = jnp.exp2(sc - m_new).astype(BF16)
    v_ones = jnp.concatenate([v_t, jnp.ones((ONES_ROWS, v_t.shape[1]), BF16)], axis=0)
    return m_new, jnp.exp2(m - m_new) * acc + _dot(v_ones, p)


def _flash_split(carry):
    m, acc = carry
    return m, acc[:HEAD_DIM], acc[HEAD_DIM:HEAD_DIM + 1]


def _flash_steps(carries, chunks, cap=None):
    return tuple(_flash_step(carry, sc if cap is None else jnp.minimum(sc, cap), v_t)
                 for carry, (sc, v_t) in zip(carries, chunks))


def _flash_reset(state):
    m_ref, acc_ref = state
    m_ref[...] = jnp.full(m_ref.shape, M_INIT, F32)
    acc_ref[...] = jnp.zeros(acc_ref.shape, F32)


def _flash_update(state, slot, sc, v_t, cap=None):
    m_ref, acc_ref = state
    m_ref[slot], acc_ref[slot] = _flash_step((m_ref[slot], acc_ref[slot]),
                                             sc if cap is None else jnp.minimum(sc, cap), v_t)


def _flash_update_pipelined(state, bufs, slots, c, next_scores, values, cap=None, cap_slots=None):
    def run(src_ref, dst_ref):
        for slot in slots:
            fresh = None if next_scores is None else next_scores(slot)
            masked = cap is not None and (cap_slots is None or slot in cap_slots)
            _flash_update(state, slot, src_ref[slot], values(slot), cap if masked else None)
            if fresh is not None:
                dst_ref[slot] = fresh

    pl.when((c & 1) == 0)(lambda: run(bufs[0], bufs[1]))
    pl.when((c & 1) == 1)(lambda: run(bufs[1], bufs[0]))


def _flash_result(state, slot):
    m_ref, acc_ref = state
    m, out, den = _flash_split((m_ref[slot], acc_ref[slot]))
    return out / jnp.maximum(den, TINY)


def _flash_scratch(chains, n):
    return [pltpu.VMEM((chains, 1, n), F32), pltpu.VMEM((chains, ACC_ROWS, n), F32)]


def _rank_select(score, blk, limit, count):
    ranks = []
    for r0 in range(0, score.shape[0], SUBLANES):
        tile = score[r0:r0 + SUBLANES]
        tile_blk = r0 + _iota(tile.shape, 0)
        rank = jnp.zeros(tile.shape, jnp.int32)
        for mm in range(limit):
            gm = score[mm:mm + 1, :]
            if mm < r0:
                beats = gm >= tile
            elif mm >= r0 + SUBLANES:
                beats = gm > tile
            else:
                beats = (gm > tile) | ((gm == tile) & (mm < tile_blk))
            rank = rank + jnp.where(beats, 1, 0)
        ranks.append(rank)
    return jnp.concatenate(ranks, axis=0) < count


def _tile_lanes(x, reps):
    return jnp.concatenate([x] * reps, axis=1) if reps > 1 else x


def _causal_cap(tk, tq, reps):
    keep = _iota((tk, tq), 0) <= _iota((tk, tq), 1)
    return _tile_lanes(jnp.where(keep, BIG, -BIG), reps)


def _window_edge_cap(tk, tq, reps):
    keep = _iota((tk, tq), 0) > _iota((tk, tq), 1)
    return _tile_lanes(jnp.where(keep, BIG, -BIG), reps)


def _store_chunked_t(x, dst_ref, tk):
    x_t = x.T.astype(BF16)
    for c in range(x.shape[0] // tk):
        dst_ref[c] = x_t[:, c * tk:(c + 1) * tk]


def _moba_kernel(alibi_ref, qt_ref, ka_ref, vt_ref, km_ref, z_ref, o_ref, m_s, acc_s, sc0_s, sc1_s):
    i = pl.program_id(1)
    nblk = vt_ref.shape[1]
    tq = MOBA_BLOCK
    hd = HEAD_DIM
    chains = [(r, j) for r in range(qt_ref.shape[0]) for j in range(N_HEADS)]
    slots = range(len(chains))

    blk = _iota((nblk, tq), 0)
    qas = []
    for r, j in chains:
        qn = qt_ref[r, j * hd:(j + 1) * hd, :]
        km = jnp.concatenate([km_ref[r, t, :, j * hd:(j + 1) * hd] for t in range(km_ref.shape[1])], axis=0)
        gate = jnp.where(blk < i, _dot_fine(km, qn), NEG)
        chosen = _rank_select(gate, blk, nblk, MOBA_TOPK) & (blk < i)
        sel_bias = jnp.where(chosen | (blk == i), 0.0, -BIG)
        qas.append(_augment_query(qn, alibi_ref[j], sel_bias, flipped=j % 2 == 1))

    def scores(slot, n):
        r, j = chains[slot]
        return _dot(ka_ref[r, j, pl.ds(pl.multiple_of(n * tq, tq), tq), :], qas[slot])

    def values(slot, n):
        r, j = chains[slot]
        return vt_ref[r, n, j * hd:(j + 1) * hd, :]

    state = (m_s, acc_s)
    _flash_reset(state)
    bufs = (sc0_s, sc1_s)
    for slot in slots:
        sc0_s[slot] = scores(slot, 0)

    def body(n, carry):
        _flash_update_pipelined(state, bufs, slots, n, lambda slot: scores(slot, n + 1),
                                lambda slot: values(slot, n))
        return carry

    lax.fori_loop(0, i, body, 0)
    diag_cap = _causal_cap(tq, tq, 1)
    _flash_update_pipelined(state, bufs, slots, i, None, lambda slot: values(slot, i), diag_cap)
    for r in range(qt_ref.shape[0]):
        outs = [_flash_result(state, r * N_HEADS + j) for j in range(N_HEADS)]
        y = jnp.concatenate(outs, axis=0).T * _silu(z_ref[r])
        o_ref[r] = y.astype(o_ref.dtype)


def _moba(z, q_t, ka, vt, km, *, rows_per_step=2):
    bsz, s, _ = z.shape
    tq = MOBA_BLOCK
    gw = GROUP_WIDTH
    nblk = s // MOBA_BLOCK
    rps = rows_per_step
    assert SEL_FEAT0 + nblk <= HEAD_DIM and bsz % rps == 0
    whole = lambda a: pl.BlockSpec((rps,) + a.shape[1:], lambda b, i: (b,) + (0,) * (a.ndim - 1))
    return pl.pallas_call(
        _moba_kernel,
        grid=(bsz // rps, s // tq),
        in_specs=[pl.BlockSpec((N_HEADS, 2 * SLOPE_PIECES, tq), lambda b, i: (0, 0, 0)),
                  pl.BlockSpec((rps, gw, tq), lambda b, i: (b, 0, i)), whole(ka), whole(vt), whole(km),
                  pl.BlockSpec((rps, tq, gw), lambda b, i: (b, i, 0))],
        out_specs=pl.BlockSpec((rps, tq, gw), lambda b, i: (b, i, 0)),
        out_shape=jax.ShapeDtypeStruct((bsz, s, gw), BF16),
        scratch_shapes=_flash_scratch(rps * N_HEADS, tq) + [pltpu.VMEM((rps * N_HEADS, tq, tq), F32)] * 2,
        name="moba",
    )(jnp.asarray(_alibi_query_features(tq)), q_t, ka, vt, km, z)


def _cmp_kernel(xk_ref, xv_ref, wk1_ref, wv1_ref, pk_ref, pv_ref, bk1_ref, bv1_ref,
                wk2_ref, wv2_ref, bk2_ref, bv2_ref, gn_ref, ko_ref, vo_ref, wke_s, wve_s):
    half = xk_ref.shape[2]
    n = xk_ref.shape[1]
    hd = HEAD_DIM
    nh = NSA_CMP_HIDDEN

    @pl.when(pl.program_id(0) == 0)
    def _():
        for w_ref, we_ref in ((wk1_ref, wke_s), (wv1_ref, wve_s)):
            we_ref[...] = jnp.zeros(we_ref.shape, BF16)
            for tok in range(NSA_CMP_LEN):
                for g in range(KV_HEADS):
                    row = (tok * KV_HEADS + g) * hd
                    we_ref[row:row + hd, g * nh:(g + 1) * nh] = w_ref[tok * hd:(tok + 1) * hd, :]

    def hidden(x_ref, we_ref, pos_ref, b1_ref):
        x = x_ref[0].astype(BF16)
        first = _dot(x, we_ref[:half, :])
        second = _dot(x, we_ref[half:, :])
        pos = jnp.broadcast_to(pos_ref[...], (8, 2 * half)).astype(BF16)
        ph = _dot(pos[:, :half], we_ref[:half, :]) + _dot(pos[:, half:], we_ref[half:, :])
        return _silu(first + pltpu.roll(second, n - 1, 0) + ph[0:1, :] + b1_ref[...])

    hk = hidden(xk_ref, wke_s, pk_ref, bk1_ref).astype(BF16)
    hv = hidden(xv_ref, wve_s, pv_ref, bv1_ref).astype(BF16)
    nh = NSA_CMP_HIDDEN
    ks, vs = [], []
    for g in range(KV_HEADS):
        ks.append(_rms(_dot(hk[:, g * nh:(g + 1) * nh], wk2_ref[...]) + bk2_ref[...], gn_ref[...]))
        vs.append(_dot(hv[:, g * nh:(g + 1) * nh], wv2_ref[...]) + bv2_ref[...])
    ko_ref[0] = jnp.concatenate(ks, axis=1)
    vo_ref[0] = jnp.concatenate(vs, axis=1)


def _compress(xk, xv, pos_k, pos_v, kw1, kb1, kw2, kb2, vw1, vb1, vw2, vb2, gn):
    bsz, n, half = xk.shape
    hd = HEAD_DIM
    tile_pos = lambda p: jnp.tile(p[:, None, :], (1, KV_HEADS, 1)).reshape(1, 2 * half)
    tile_b = lambda b: jnp.tile(b.reshape(1, -1), (1, KV_HEADS))
    const = lambda shape: pl.BlockSpec(shape, lambda b: (0,) * len(shape))
    xs = pl.BlockSpec((1, n, half), lambda b: (b, 0, 0))
    hw = KV_HEADS * NSA_CMP_HIDDEN
    out = pl.BlockSpec((1, n, KV_HEADS * hd), lambda b: (b, 0, 0))
    expanded = pltpu.VMEM((2 * half, hw), BF16)
    return pl.pallas_call(
        _cmp_kernel,
        grid=(bsz,),
        in_specs=[xs, xs, const(kw1.shape), const(vw1.shape),
                  const((1, 2 * half)), const((1, 2 * half)), const((1, hw)), const((1, hw)),
                  const((NSA_CMP_HIDDEN, hd)), const((NSA_CMP_HIDDEN, hd)), const((1, hd)), const((1, hd)),
                  const((1, hd))],
        out_specs=[out, out],
        out_shape=[jax.ShapeDtypeStruct((bsz, n, KV_HEADS * hd), F32)] * 2,
        scratch_shapes=[expanded, expanded],
        name="nsa_compress",
    )(xk, xv, kw1.astype(BF16), vw1.astype(BF16), tile_pos(pos_k), tile_pos(pos_v), tile_b(kb1), tile_b(vb1),
      kw2.astype(BF16), vw2.astype(BF16), kb2.reshape(1, hd), vb2.reshape(1, hd), gn.reshape(1, hd))


def _group_query(q_t, alibi_ref, g, gq_col):
    hd = HEAD_DIM
    heads = range(Q_PER_KV * g, Q_PER_KV * (g + 1))
    qn = jnp.concatenate([_norm_query_t(q_t[h * hd:(h + 1) * hd], gq_col) for h in heads], axis=1)
    alibi = jnp.concatenate([alibi_ref[h] for h in heads], axis=1)
    return qn, alibi


def _augment_keys(k, gk_ref, dst_ref, row=0, sel_shift=None):
    assert KV_HEADS == 2
    pos = _iota(k.shape, 0)
    dst_ref[2 * row], dst_ref[2 * row + 1] = _augment_pair_keys(_pair_rms(k, gk_ref[...]), pos, sel_shift)


def _chunk_scores(ka_ref, vt_ref, g, c, qa, tq):
    off = pl.multiple_of(c * tq, tq)
    return _dot(ka_ref[g, pl.ds(off, tq), :], qa), vt_ref[c, g * HEAD_DIM:(g + 1) * HEAD_DIM, :]


def _window_first_chunk(ka_ref, vt_ref, g, far, qa, tq, edge_cap):
    sc, v_t = _chunk_scores(ka_ref, vt_ref, g, jnp.maximum(far, 0), qa, tq)
    return jnp.minimum(sc, jnp.minimum(edge_cap, jnp.where(far >= 0, BIG, -BIG))), v_t


def _regroup_heads(o_t, tq):
    return jnp.concatenate([o_t[:, k * tq:(k + 1) * tq] for k in range(Q_PER_KV)], axis=0)


def _nsa_kernel(alibi_ref, q_ref, kc_ref, vc_ref, ks_ref, vs_ref, kw_ref, vw_ref, gate_ref, z_ref,
                gq_ref, gks_ref, gkw_ref, o_ref, kca_s, vct_s, ksa_s, vst_s, kwa_s, vwt_s, m_s, acc_s,
                sc0_s, sc1_s):
    i = pl.program_id(1)
    rows = range(q_ref.shape[0])
    tq = q_ref.shape[1]
    s = ks_ref.shape[1]
    hd = HEAD_DIM
    ncmp = kc_ref.shape[1]
    nsb = s // NSA_SEL_BLOCK

    @pl.when(i == 0)
    def _():
        cend = _iota((ncmp, LANES), 0) * NSA_CMP_STRIDE + (NSA_CMP_LEN - 1)
        for r in rows:
            _augment_keys(ks_ref[r], gks_ref, ksa_s, r, NSA_SEL_BLOCK.bit_length() - 1)
            _augment_keys(kw_ref[r], gkw_ref, kwa_s, r)
            _store_chunked_t(vs_ref[r], vst_s.at[r], tq)
            _store_chunked_t(vw_ref[r], vwt_s.at[r], tq)
            kca_s[2 * r], kca_s[2 * r + 1] = _augment_pair_keys(kc_ref[r], cend)
            vct_s[r] = vc_ref[r].T.astype(BF16)

    t0 = i * tq
    diag_cap = _causal_cap(tq, tq, Q_PER_KV)
    edge_cap = _window_edge_cap(tq, tq, Q_PER_KV)
    cend =_iota((ncmp, tq), 0) * NSA_CMP_STRIDE + (NSA_CMP_LEN - 1)
    seen = _tile_lanes(cend <= t0 + _iota((ncmp, tq), 1), Q_PER_KV)
    cstart = _iota((nsb, ncmp), 1) * NSA_CMP_STRIDE
    bstart = _iota((nsb, ncmp), 0) * NSA_SEL_BLOCK
    overlap_t = jnp.where((cstart < bstart + NSA_SEL_BLOCK) & (cstart + NSA_CMP_LEN > bstart), 1.0, 0.0)
    blk = _iota((nsb, tq), 0)
    cur = (t0 + _iota((nsb, tq), 1)) >> (NSA_SEL_BLOCK.bit_length() - 1)
    forced = (blk == 0) | (blk == cur) | (blk == cur - 1)
    groups = range(KV_HEADS)
    far = i - NSA_WINDOW // tq
    lo = jnp.maximum(far + 1, 0)
    first_win = jnp.maximum(far, 0)

    state = (m_s, acc_s)
    kv = [(r, g) for r in rows for g in groups]
    sel_slots = tuple(range(len(kv)))
    win_slots = tuple(len(kv) + n for n in range(len(kv)))
    _flash_reset(state)
    q_ts = [q_ref[r].T for r in rows]
    queries = [_group_query(q_ts[r], alibi_ref, g, gq_ref[...]) for r, g in kv]
    qa_win = [_augment_query(qn, alibi, flipped=g == 1) for (r, g), (qn, alibi) in zip(kv, queries)]
    qa_sel = []

    def scores(slot, c):
        n = slot % len(kv)
        ka_ref, qa = (ksa_s, qa_sel) if slot in sel_slots else (kwa_s, qa_win)
        return _dot(ka_ref[n, pl.ds(pl.multiple_of(c * tq, tq), tq), :], qa[n])

    def values(slot, c):
        r, g = kv[slot % len(kv)]
        return (vst_s if slot in sel_slots else vwt_s)[r, c, g * hd:(g + 1) * hd, :]

    cmp_sc = [_dot(kca_s[n], qa_win[n]) for n in range(len(kv))]
    sc_win_first = [scores(slot, first_win) for slot in win_slots]
    o_cmp = []
    for n, (r, g) in enumerate(kv):
        qn, alibi = queries[n]
        sc = jnp.where(seen, cmp_sc[n], -BIG)
        e = jnp.where(seen, jnp.exp2(sc - jnp.max(sc, axis=0, keepdims=True)), 0.0)
        p_c = e / jnp.maximum(jnp.sum(e, axis=0, keepdims=True), TINY)
        o_cmp.append(_dot(vct_s[r, g * hd:(g + 1) * hd, :], p_c.astype(BF16)))

        psum = p_c[:, 0:tq]
        for k in range(1, Q_PER_KV):
            psum = psum + p_c[:, k * tq:(k + 1) * tq]
        imp = _dot_fine(overlap_t, psum, a_is_bf16_exact=True)
        imp = jnp.where(blk <= cur, jnp.where(forced, NSA_FORCE, imp), NEG)
        chosen = _rank_select(imp, blk, nsb, min(NSA_TOPN, nsb))
        sel_bias = _tile_lanes(jnp.where(chosen, 0.0, -BIG), Q_PER_KV)
        qa_sel.append(_augment_query(qn, alibi, sel_bias, flipped=g == 1))

    bufs = (sc0_s, sc1_s)
    for slot in sel_slots:
        sc0_s[slot] = scores(slot, 0)
    for slot, sc in zip(win_slots, sc_win_first):
        sc0_s[slot] = sc
        sc1_s[slot] = sc

    def body(slots, cap, cap_slots, c, carry):
        _flash_update_pipelined(state, bufs, slots, c, lambda slot: scores(slot, c + 1),
                                lambda slot: values(slot, c), cap, cap_slots)
        return carry

    both_slots = sel_slots + win_slots
    lax.fori_loop(0, first_win, functools.partial(body, sel_slots, None, None), 0)
    lax.fori_loop(first_win, lo, functools.partial(body, both_slots, edge_cap, win_slots), 0)
    lax.fori_loop(lo, i, functools.partial(body, both_slots, None, None), 0)
    _flash_update_pipelined(state, bufs, both_slots, i, None, lambda slot: values(slot, i), diag_cap)

    for r in rows:
        gates_t = _sigmoid(gate_ref[r]).T
        outs = []
        for g in groups:
            def gate_row(branch):
                parts = [gates_t[branch * N_HEADS + h:branch * N_HEADS + h + 1, :]
                         for h in range(Q_PER_KV * g, Q_PER_KV * (g + 1))]
                return jnp.concatenate(parts, axis=1)

            n = KV_HEADS * r + g
            o_slc = _flash_result(state, sel_slots[n])
            o_win = _flash_result(state, win_slots[n])
            o = gate_row(0) * o_cmp[n] + gate_row(1) * o_slc + gate_row(2) * o_win
            outs.append(_regroup_heads(o, tq))
        y = jnp.concatenate(outs, axis=0).T * _silu(z_ref[r])
        o_ref[r] = y.astype(o_ref.dtype)


def _nsa(proj, kcn, vc, gq, gks, gkw, *, tq=256, rows_per_step=2):
    bsz, s, _ = proj.shape
    hd = HEAD_DIM
    gw = GROUP_WIDTH
    ncmp = kcn.shape[1]
    rps = rows_per_step
    chains = 2 * KV_HEADS * rps
    assert SEL_FEAT0 + s // NSA_SEL_BLOCK <= hd and ncmp == LANES and bsz % rps == 0
    wide = lambda c0: pl.BlockSpec((rps, tq, gw), lambda b, i: (b, i, c0 // 4))
    full = lambda c0: pl.BlockSpec((rps, s, LANES), lambda b, i: (b, 0, c0))
    cmp = pl.BlockSpec((rps, ncmp, LANES), lambda b, i: (b, 0, 0))
    row = pl.BlockSpec((1, hd), lambda b, i: (0, 0))
    ka = pltpu.VMEM((KV_HEADS * rps, s, LANES), BF16)
    vt = pltpu.VMEM((rps, s // tq, LANES, tq), BF16)
    return pl.pallas_call(
        _nsa_kernel,
        grid=(bsz // rps, s // tq),
        in_specs=[pl.BlockSpec((N_HEADS, 2 * SLOPE_PIECES, tq), lambda b, i: (0, 0, 0)),
                  wide(ODD_QC), cmp, cmp, full(ODD_KS), full(ODD_VS), full(ODD_KW), full(ODD_VW),
                  pl.BlockSpec((rps, tq, LANES), lambda b, i: (b, i, ODD_GC)), wide(ODD_ZC),
                  pl.BlockSpec((hd, 1), lambda b, i: (0, 0)), row, row],
        out_specs=pl.BlockSpec((rps, tq, gw), lambda b, i: (b, i, 0)),
        out_shape=jax.ShapeDtypeStruct((bsz, s, gw), BF16),
        scratch_shapes=[pltpu.VMEM((KV_HEADS * rps, ncmp, LANES), BF16), pltpu.VMEM((rps, LANES, ncmp), BF16),
                        ka, vt, ka, vt] + _flash_scratch(chains, Q_PER_KV * tq)
                       + [pltpu.VMEM((chains, tq, Q_PER_KV * tq), F32)] * 2,
        name="nsa",
    )(jnp.asarray(_alibi_query_features(tq)), proj, kcn, vc, proj, proj, proj, proj, proj, proj,
      gq.reshape(hd, 1), gks.reshape(1, hd), gkw.reshape(1, hd))


def _swa_kernel(sinks_ref, alibi_ref, q_ref, k_ref, v_ref, z_ref, gq_ref, gk_ref, o_ref, ka_s, vt_s):
    step = pl.program_id(1)
    tq = SWA_WINDOW
    tiles = q_ref.shape[1] // tq
    s = k_ref.shape[1]

    @pl.when(step == 0)
    def _():
        _augment_keys(k_ref[0], gk_ref, ka_s)
        _store_chunked_t(v_ref[0], vt_s, tq)

    q_t = q_ref[0].T
    diag_cap = _causal_cap(tq, tq, Q_PER_KV)
    edge_cap = _window_edge_cap(tq, tq, Q_PER_KV)
    chains = [(u, g) for u in range(tiles) for g in range(KV_HEADS)]
    qas = [_augment_query(*_group_query(q_t[:, u * tq:(u + 1) * tq], alibi_ref, g, gq_ref[...]), flipped=g == 1)
           for u, g in chains]
    first = [_window_first_chunk(ka_s, vt_s, g, step * tiles + u - 1, qa, tq, edge_cap)
             for (u, g), qa in zip(chains, qas)]
    own = [_chunk_scores(ka_s, vt_s, g, step * tiles + u, qa, tq) for (u, g), qa in zip(chains, qas)]
    carries = _flash_steps([_flash_init(Q_PER_KV * tq) for _ in chains], first)
    carries = _flash_steps(carries, own, diag_cap)
    outs = []
    for (u, g), carry in zip(chains, carries):
        m, acc, l = _flash_split(carry)
        tpos = ((step * tiles + u) * tq + _iota((1, tq), 1)).astype(F32)
        sink = jnp.concatenate([sinks_ref[h] * LOG2E + (SLOPES[h] * LOG2E) * tpos
                                for h in range(Q_PER_KV * g, Q_PER_KV * (g + 1))], axis=1)
        mf = jnp.maximum(m, sink)
        alpha = jnp.exp2(m - mf)
        o = acc * alpha / jnp.maximum(l * alpha + jnp.exp2(sink - mf), TINY)
        outs.append(_regroup_heads(o, tq))
    o_t = jnp.concatenate([jnp.concatenate(outs[u * KV_HEADS:(u + 1) * KV_HEADS], axis=0) for u in range(tiles)],
                          axis=1)
    y = o_t.T * _silu(z_ref[0])
    o_ref[0] = y.astype(o_ref.dtype)


def _swa(proj, sinks, gq, gk, *, tiles_per_step=4):
    bsz, s, _ = proj.shape
    hd = HEAD_DIM
    gw = GROUP_WIDTH
    tq = SWA_WINDOW
    rows = tiles_per_step * tq
    wide = lambda c0: pl.BlockSpec((1, rows, gw), lambda b, i: (b, i, c0 // 4))
    full = lambda c0: pl.BlockSpec((1, s, LANES), lambda b, i: (b, 0, c0))
    return pl.pallas_call(
        _swa_kernel,
        grid=(bsz, s // rows),
        in_specs=[pl.BlockSpec(memory_space=pltpu.SMEM),
                  pl.BlockSpec((N_HEADS, 2 * SLOPE_PIECES, tq), lambda b, i: (0, 0, 0)),
                  wide(ODD_QD), full(ODD_KD), full(ODD_VD), wide(ODD_ZD),
                  pl.BlockSpec((hd, 1), lambda b, i: (0, 0)), pl.BlockSpec((1, hd), lambda b, i: (0, 0))],
        out_specs=pl.BlockSpec((1, rows, gw), lambda b, i: (b, i, 0)),
        out_shape=jax.ShapeDtypeStruct((bsz, s, gw), BF16),
        scratch_shapes=[pltpu.VMEM((KV_HEADS, s, LANES), BF16), pltpu.VMEM((s // tq, LANES, tq), BF16)],
        name="swa",
    )(sinks.astype(F32), jnp.asarray(_alibi_query_features(tq)), proj, proj, proj, proj,
      gq.reshape(hd, 1), gk.reshape(1, hd))


def _pack_odd_weight(w):
    gw, kw, ng = GROUP_WIDTH, KV_HEADS * HEAD_DIM, 3 * N_HEADS
    sizes = [gw, kw, kw, kw, kw, kw, kw, ng, gw, gw, kw, kw, gw]
    starts = np.concatenate([[0], np.cumsum(sizes)]).tolist()
    part = lambda k: w[:, starts[k]:starts[k + 1]]
    pad = jnp.zeros((w.shape[0], LANES - ng), w.dtype)
    order = [0, 8, 9, 12, 1, 2, 3, 4, 5, 6, 10, 11, 7]
    return jnp.concatenate([part(k) for k in order] + [pad], axis=1).astype(BF16)


def kernel(x, norm_g, w_out, e_w_in, a_conv_w, a_conv_b, a_ln_g, a_ln_b, b_qnorm_g, b_knorm_g, o_w_in, c_qnorm_g, c_knorm_cmp_g, c_knorm_slc_g, c_knorm_win_g, c_pos_k, c_pos_v, c_k_w1, c_k_b1, c_k_w2, c_k_b2, c_v_w1, c_v_b1, c_v_w2, c_v_b2, d_qnorm_g, d_knorm_g, d_sinks):
    bsz, s, d = x.shape
    m = bsz * s
    assert s % MOBA_BLOCK == 0 and d == 2 * GROUP_WIDTH
    x2 = x.reshape(m, d)

    y_a, z_b, q_t, ka, vt, km = _even_layer_front(
        x2, norm_g[0], e_w_in[0].astype(BF16), b_qnorm_g[0], b_knorm_g[0],
        a_conv_w[0], a_conv_b[0], a_ln_g[0], a_ln_b[0], bsz=bsz)
    y_b = _moba(z_b.reshape(bsz, s, -1), q_t, ka, vt, km)

    x2, proj, k_cmp, v_cmp = _layer_boundary(
        x2, y_a, y_b.reshape(m, -1), w_out[0].astype(BF16), norm_g[1], _pack_odd_weight(o_w_in[0]),
        chunk=5 * LANES, copies=((ODD_KC * LANES, LANES), (ODD_VC * LANES, LANES)), group=NSA_CMP_STRIDE)
    proj = proj.reshape(bsz, s, ODD_COLS)
    rows16 = lambda t: t.reshape(bsz, s // NSA_CMP_STRIDE, NSA_CMP_STRIDE * LANES)
    kcn, vc = _compress(rows16(k_cmp), rows16(v_cmp), c_pos_k[0], c_pos_v[0], c_k_w1[0], c_k_b1[0],
                        c_k_w2[0], c_k_b2[0], c_v_w1[0], c_v_b1[0], c_v_w2[0], c_v_b2[0], c_knorm_cmp_g[0])
    y_c = _nsa(proj, kcn, vc, c_qnorm_g[0], c_knorm_slc_g[0], c_knorm_win_g[0])
    y_d = _swa(proj, d_sinks[0], d_qnorm_g[0], d_knorm_g[0])
    x2 = _outproj(x2, y_c.reshape(m, -1), y_d.reshape(m, -1), w_out[1].astype(BF16))
    return x2.reshape(bsz, s, d)
```

```python
import functools

import ml_dtypes
import numpy as np
import jax
import jax.numpy as jnp
from jax import lax
from jax.experimental import pallas as pl
from jax.experimental.pallas import tpu as pltpu

HEAD_DIM = 64
N_HEADS = 8
GROUP_WIDTH = N_HEADS * HEAD_DIM
CONV_WIDTH = 31
MOBA_BLOCK = 256
MOBA_TOPK = 3
KV_HEADS = 2
Q_PER_KV = N_HEADS // KV_HEADS
NSA_CMP_LEN = 32
NSA_CMP_STRIDE = 16
NSA_CMP_HIDDEN = 256
NSA_SEL_BLOCK = 64
NSA_TOPN = 16
NSA_WINDOW = 512
NSA_FORCE = 1e4
SWA_WINDOW = 128
EPS = 1e-6
NEG = -1e30
TINY = 1e-30
LANES = 128
SUBLANES = 8
CONV_HALO = 32

LOG2E = float(np.log2(np.e))
BIG = 2.0 ** 99
M_INIT = -1e38
POS_SPLIT_SHIFT = 8
SLOPE_PIECES = 4
SEL_FEAT0 = 2 * SLOPE_PIECES

F32 = jnp.float32
BF16 = jnp.bfloat16

ODD_QC, ODD_ZC, ODD_QD, ODD_ZD = 0, 4, 8, 12
ODD_KC, ODD_VC, ODD_KS, ODD_VS, ODD_KW, ODD_VW, ODD_KD, ODD_VD, ODD_GC = 16, 17, 18, 19, 20, 21, 22, 23, 24
ODD_COLS = 25 * LANES


def _alibi_slopes(n):
    return [float(2.0 ** (-8.0 * (i + 1) / n)) for i in range(n)]


SLOPES = _alibi_slopes(N_HEADS)


def _alibi_query_features(width):
    table = np.zeros((N_HEADS, 2 * SLOPE_PIECES, width), np.float32)
    for h, slope in enumerate(SLOPES):
        rest = np.float64(slope) * LOG2E
        for k in range(SLOPE_PIECES):
            piece = float(np.float32(rest).astype(ml_dtypes.bfloat16).astype(np.float32))
            table[h, k] = piece
            table[h, SLOPE_PIECES + k] = piece
            rest -= piece
    return table


def _dot(a, b):
    return jnp.dot(a, b, preferred_element_type=F32)


def _split_bf16(x, pieces):
    out = []
    for _ in range(pieces):
        out.append(x.astype(BF16))
        x = x - out[-1].astype(F32)
    return out


def _dot_fine(a, b, a_is_bf16_exact=False):
    if a_is_bf16_exact:
        return sum(_dot(a.astype(BF16), piece) for piece in _split_bf16(b, 3))
    (a_hi, a_lo), (b_hi, b_lo) = _split_bf16(a, 2), _split_bf16(b, 2)
    return _dot(a_hi, b_hi) + (_dot(a_hi, b_lo) + _dot(a_lo, b_hi))


def _sigmoid(x):
    return 1.0 / (1.0 + jnp.exp(-x))


def _silu(x):
    return x * _sigmoid(x)


def _rms(x, g):
    return x * lax.rsqrt(jnp.mean(x * x, axis=-1, keepdims=True) + EPS) * g


def _iota(shape, dim):
    return lax.broadcasted_iota(jnp.int32, shape, dim)


def _layer_boundary_kernel(x_ref, ya_ref, yb_ref, wa_ref, wb_ref, g_ref, w_ref, x_out_ref, o_ref, *rest,
                           chunk, copies, group):
    copy_refs, stage_refs = rest[:len(copies)], rest[len(copies):]
    x = x_ref[...] + _dot(ya_ref[...], wa_ref[...]) + _dot(yb_ref[...], wb_ref[...])
    x_out_ref[...] = x
    h = _rms(x, g_ref[...]).astype(BF16)
    for c in range(o_ref.shape[1] // chunk):
        o_ref[:, c * chunk:(c + 1) * chunk] = _dot(h, w_ref[:, c * chunk:(c + 1) * chunk])
    for ref, stage, (start, width) in zip(copy_refs, stage_refs, copies):
        stage[...] = o_ref[:, start:start + width]
        for t in range(group):
            ref[:, t * width:(t + 1) * width] = stage[pl.ds(t, ref.shape[0], stride=group), :]


def _layer_boundary(x2d, ya, yb, w_out, g, w, *, tm=512, chunk, copies=(), group=1):
    m, d = x2d.shape
    e = w.shape[1]
    gw = ya.shape[1]
    rows = lambda n, width: pl.BlockSpec((n, width), lambda i: (i, 0))
    const = lambda shape: pl.BlockSpec(shape, lambda i: (0, 0))
    return pl.pallas_call(
        functools.partial(_layer_boundary_kernel, chunk=chunk, copies=tuple(copies), group=group),
        grid=(m // tm,),
        in_specs=[rows(tm, d), rows(tm, gw), rows(tm, gw), const((gw, d)), const((gw, d)), const((1, d)),
                  const((d, e))],
        out_specs=[rows(tm, d), rows(tm, e)] + [rows(tm // group, group * width) for _, width in copies],
        out_shape=[jax.ShapeDtypeStruct((m, d), F32), jax.ShapeDtypeStruct((m, e), F32)]
                  + [jax.ShapeDtypeStruct((m // group, group * width), F32) for _, width in copies],
        scratch_shapes=[pltpu.VMEM((tm, width), F32) for _, width in copies],
        name="layer_boundary",
    )(x2d, ya, yb, w_out[:gw], w_out[gw:], g.reshape(1, d), w)


def _conv_rows(h_s, row0, rows, w_ref, b_ref, lg_ref, lb_ref):
    base = CONV_HALO - (CONV_WIDTH - 1)
    acc = None
    for b in range(SUBLANES):
        n = rows if b == 0 else rows + SUBLANES
        part = None
        for a in range((base + CONV_WIDTH - 1) // SUBLANES + 1):
            j = SUBLANES * a + b - base
            if 0 <= j < CONV_WIDTH:
                term = w_ref[j:j + 1, :] * h_s[pl.ds(row0 + SUBLANES * a, n), :]
                part = term if part is None else part + term
        part = part[b:b + rows]
        acc = part if acc is None else acc + part
    y = acc + b_ref[...]
    mu = jnp.mean(y, axis=-1, keepdims=True)
    yc = y - mu
    return _silu(yc * lax.rsqrt(jnp.mean(yc * yc, axis=-1, keepdims=True) + EPS) * lg_ref[...] + lb_ref[...])


def _even_layer_kernel(x_ref, g_ref, w_ref, gq_ref, gk_ref, cw_ref, cb_ref, lg_ref, lb_ref,
                       ya_ref, zb_ref, qt_ref, ka_ref, vt_ref, km_ref, hb_s, conv_s, sec_s, h_s, tail_s,
                       *, seq, rows):
    tm = x_ref.shape[0]
    hd, gw = HEAD_DIM, GROUP_WIDTH
    tile_in_seq = pl.program_id(0) % (seq // tm)
    hb_s[...] = _rms(x_ref[...], g_ref[...]).astype(BF16)
    for c in range(3):
        conv_s[:, c * gw:(c + 1) * gw] = _dot(hb_s[...], w_ref[:, c * gw:(c + 1) * gw])
    h_s[0:CONV_HALO, :] = jnp.where(tile_in_seq > 0, tail_s[...], 0.0)
    h_s[CONV_HALO:, :] = conv_s[:, 0:gw] * _sigmoid(conv_s[:, gw:2 * gw])
    tail_s[...] = h_s[tm:tm + CONV_HALO, :]

    def project(sec, dst_ref):
        dst_ref[...] = _dot(hb_s[...], w_ref[:, sec * gw:(sec + 1) * gw])

    def moba_queries():
        project(3, sec_s)
        q_t = sec_s[...].T
        for j in range(N_HEADS):
            qt_ref[0, j * hd:(j + 1) * hd, :] = _norm_query_t(q_t[j * hd:(j + 1) * hd], gq_ref[...])

    def moba_keys():
        project(4, sec_s)
        pos = tile_in_seq * tm + _iota((tm, LANES), 0)
        for pair in range(N_HEADS // 2):
            kn = _pair_rms(sec_s[:, pair * LANES:(pair + 1) * LANES], gk_ref[...])
            ka_ref[0, 2 * pair], ka_ref[0, 2 * pair + 1] = _augment_pair_keys(kn, pos,
                                                                              MOBA_BLOCK.bit_length() - 1)
            km_ref[0, 0, :, pair * LANES:(pair + 1) * LANES] = jnp.mean(
                kn.reshape(tm // MOBA_BLOCK, MOBA_BLOCK, LANES), axis=1)

    def moba_values():
        project(5, sec_s)
        v_t = sec_s[...].T.astype(BF16)
        for c in range(tm // MOBA_BLOCK):
            vt_ref[0, c] = v_t[:, c * MOBA_BLOCK:(c + 1) * MOBA_BLOCK]

    def moba_gate_path():
        project(6, zb_ref)

    for c in range(tm // rows):
        y = _conv_rows(h_s, c * rows, rows, cw_ref, cb_ref, lg_ref, lb_ref)
        gate = _silu(conv_s[c * rows:(c + 1) * rows, 2 * gw:3 * gw])
        ya_ref[c * rows:(c + 1) * rows, :] = (y * gate).astype(ya_ref.dtype)
    moba_queries()
    moba_keys()
    moba_values()
    moba_gate_path()


def _even_layer_front(x2d, g, w, gq, gk, conv_w, conv_b, ln_g, ln_b, *, bsz, tm=512, rows=128):
    m, d = x2d.shape
    e = w.shape[1]
    s = m // bsz
    per_b = s // tm
    hd, gw, nb = HEAD_DIM, GROUP_WIDTH, tm // MOBA_BLOCK
    assert s % tm == 0 and tm % MOBA_BLOCK == 0 and e == 7 * gw and tm % rows == 0
    const = lambda shape: pl.BlockSpec(shape, lambda i: (0, 0))
    tile = pl.BlockSpec((tm, gw), lambda i: (i, 0))
    vec = const((1, gw))
    return pl.pallas_call(
        functools.partial(_even_layer_kernel, seq=s, rows=rows),
        grid=(m // tm,),
        in_specs=[pl.BlockSpec((tm, d), lambda i: (i, 0)), const((1, d)), const((d, e)), const((hd, 1)),
                  const((1, hd)), const((CONV_WIDTH, gw)), vec, vec, vec],
        out_specs=[tile, tile,
                   pl.BlockSpec((1, gw, tm), lambda i: (i // per_b, 0, i % per_b)),
                   pl.BlockSpec((1, N_HEADS, tm, LANES), lambda i: (i // per_b, 0, i % per_b, 0)),
                   pl.BlockSpec((1, nb, gw, MOBA_BLOCK), lambda i: (i // per_b, i % per_b, 0, 0)),
                   pl.BlockSpec((1, 1, nb, gw), lambda i: (i // per_b, i % per_b, 0, 0))],
        out_shape=[jax.ShapeDtypeStruct((m, gw), BF16),
                   jax.ShapeDtypeStruct((m, gw), F32),
                   jax.ShapeDtypeStruct((bsz, gw, s), F32),
                   jax.ShapeDtypeStruct((bsz, N_HEADS, s, LANES), BF16),
                   jax.ShapeDtypeStruct((bsz, s // MOBA_BLOCK, gw, MOBA_BLOCK), BF16),
                   jax.ShapeDtypeStruct((bsz, per_b, nb, gw), F32)],
        scratch_shapes=[pltpu.VMEM((tm, d), BF16), pltpu.VMEM((tm, 3 * gw), F32), pltpu.VMEM((tm, gw), F32),
                        pltpu.VMEM((CONV_HALO + tm, gw), F32), pltpu.VMEM((CONV_HALO, gw), F32)],
        name="even_layer_front",
    )(x2d, g.reshape(1, d), w, gq.reshape(hd, 1), gk.reshape(1, hd), conv_w, conv_b.reshape(1, gw),
      ln_g.reshape(1, gw), ln_b.reshape(1, gw))


def _outproj_kernel(x_ref, ya_ref, yb_ref, wa_ref, wb_ref, o_ref):
    o_ref[...] = x_ref[...] + _dot(ya_ref[...], wa_ref[...]) + _dot(yb_ref[...], wb_ref[...])


def _outproj(x2d, ya, yb, w, *, tm=1024):
    m, d = x2d.shape
    gw = ya.shape[1]
    return pl.pallas_call(
        _outproj_kernel,
        grid=(m // tm,),
        in_specs=[pl.BlockSpec((tm, d), lambda i: (i, 0)),
                  pl.BlockSpec((tm, gw), lambda i: (i, 0)),
                  pl.BlockSpec((tm, gw), lambda i: (i, 0)),
                  pl.BlockSpec((gw, d), lambda i: (0, 0)),
                  pl.BlockSpec((gw, d), lambda i: (0, 0))],
        out_specs=pl.BlockSpec((tm, d), lambda i: (i, 0)),
        out_shape=jax.ShapeDtypeStruct((m, d), F32),
        name="outproj",
    )(x2d, ya, yb, w[:gw], w[gw:])


def _key_features(pos, flipped, sel_shift=None):
    col = _iota(pos.shape, 1) - (0 if flipped else HEAD_DIM)
    hi = (pos >> POS_SPLIT_SHIFT) << POS_SPLIT_SHIFT
    lo = pos & ((1 << POS_SPLIT_SHIFT) - 1)
    feat = jnp.where(col < SLOPE_PIECES, hi, jnp.where(col < 2 * SLOPE_PIECES, lo, 0))
    if sel_shift is not None:
        feat = jnp.where(col - SEL_FEAT0 == (pos >> sel_shift), 1, feat)
    return jnp.where((col >= 0) & (col < HEAD_DIM), feat, 0).astype(F32)


def _pair_rms(x, g):
    sq = x * x
    hi = sq.astype(BF16)
    lo = (sq - hi.astype(F32)).astype(BF16)
    head_shift = HEAD_DIM.bit_length() - 1
    same_head = (_iota((LANES, LANES), 0) >> head_shift) == (_iota((LANES, LANES), 1) >> head_shift)
    ones = jnp.where(same_head, 1.0, 0.0).astype(BF16)
    ss = _dot(hi, ones) + _dot(lo, ones)
    return x * lax.rsqrt(ss * (1.0 / HEAD_DIM) + EPS) * jnp.concatenate([g, g], axis=1)


def _augment_pair_keys(kn, pos, sel_shift=None):
    lane = _iota(kn.shape, 1)
    return [jnp.where(lane >= HEAD_DIM if flipped else lane < HEAD_DIM, kn,
                      _key_features(pos, flipped, sel_shift)).astype(BF16) for flipped in (False, True)]


def _norm_query_t(x, g_col):
    ss = jnp.mean(x * x, axis=0, keepdims=True)
    return x * lax.rsqrt(ss + EPS) * g_col * (HEAD_DIM ** -0.5 * LOG2E)


def _augment_query(q_t, alibi, sel_bias=None, flipped=False):
    n = q_t.shape[1]
    feats = [alibi]
    used = alibi.shape[0]
    if sel_bias is not None:
        feats.append(sel_bias)
        used += sel_bias.shape[0]
    feats.append(jnp.zeros((HEAD_DIM - used, n), F32))
    return jnp.concatenate(feats + [q_t] if flipped else [q_t] + feats, axis=0).astype(BF16)


ONES_ROWS = 16
ACC_ROWS = HEAD_DIM + ONES_ROWS


def _flash_init(n):
    return jnp.full((1, n), M_INIT, F32), jnp.zeros((ACC_ROWS, n), F32)


def _flash_step(carry, sc, v_t):
    m, acc = carry
    m_new = jnp.maximum(m, jnp.max(sc, axis=0, keepdims=True))
    p = jnp.exp2(sc - m_new).astype(BF16)
    v_ones = jnp.concatenate([v_t, jnp.ones((ONES_ROWS, v_t.shape[1]), BF16)], axis=0)
    return m_new, jnp.exp2(m - m_new) * acc + _dot(v_ones, p)


def _flash_split(carry):
    m, acc = carry
    return m, acc[:HEAD_DIM], acc[HEAD_DIM:HEAD_DIM + 1]


def _flash_steps(carries, chunks, cap=None):
    return tuple(_flash_step(carry, sc if cap is None else jnp.minimum(sc, cap), v_t)
                 for carry, (sc, v_t) in zip(carries, chunks))


def _flash_reset(state):
    m_ref, acc_ref = state
    m_ref[...] = jnp.full(m_ref.shape, M_INIT, F32)
    acc_ref[...] = jnp.zeros(acc_ref.shape, F32)


def _flash_update(state, slot, sc, v_t, cap=None):
    m_ref, acc_ref = state
    m_ref[slot], acc_ref[slot] = _flash_step((m_ref[slot], acc_ref[slot]),
                                             sc if cap is None else jnp.minimum(sc, cap), v_t)


def _flash_update_pipelined(state, bufs, slots, c, next_scores, values, cap=None, cap_slots=None):
    def run(src_ref, dst_ref):
        for slot in slots:
            fresh = None if next_scores is None else next_scores(slot)
            masked = cap is not None and (cap_slots is None or slot in cap_slots)
            _flash_update(state, slot, src_ref[slot], values(slot), cap if masked else None)
            if fresh is not None:
                dst_ref[slot] = fresh

    pl.when((c & 1) == 0)(lambda: run(bufs[0], bufs[1]))
    pl.when((c & 1) == 1)(lambda: run(bufs[1], bufs[0]))


def _flash_result(state, slot):
    m_ref, acc_ref = state
    m, out, den = _flash_split((m_ref[slot], acc_ref[slot]))
    return out / jnp.maximum(den, TINY)


def _flash_scratch(chains, n):
    return [pltpu.VMEM((chains, 1, n), F32), pltpu.VMEM((chains, ACC_ROWS, n), F32)]


def _rank_select(score, blk, limit, count):
    ranks = []
    for r0 in range(0, score.shape[0], SUBLANES):
        tile = score[r0:r0 + SUBLANES]
        tile_blk = r0 + _iota(tile.shape, 0)
        rank = jnp.zeros(tile.shape, jnp.int32)
        for mm in range(limit):
            gm = score[mm:mm + 1, :]
            if mm < r0:
                beats = gm >= tile
            elif mm >= r0 + SUBLANES:
                beats = gm > tile
            else:
                beats = (gm > tile) | ((gm == tile) & (mm < tile_blk))
            rank = rank + jnp.where(beats, 1, 0)
        ranks.append(rank)
    return jnp.concatenate(ranks, axis=0) < count


def _tile_lanes(x, reps):
    return jnp.concatenate([x] * reps, axis=1) if reps > 1 else x


def _causal_cap(tk, tq, reps):
    keep = _iota((tk, tq), 0) <= _iota((tk, tq), 1)
    return _tile_lanes(jnp.where(keep, BIG, -BIG), reps)


def _window_edge_cap(tk, tq, reps):
    keep = _iota((tk, tq), 0) > _iota((tk, tq), 1)
    return _tile_lanes(jnp.where(keep, BIG, -BIG), reps)


def _store_chunked_t(x, dst_ref, tk):
    x_t = x.T.astype(BF16)
    for c in range(x.shape[0] // tk):
        dst_ref[c] = x_t[:, c * tk:(c + 1) * tk]


def _moba_kernel(alibi_ref, qt_ref, ka_ref, vt_ref, km_ref, z_ref, o_ref, m_s, acc_s, sc0_s, sc1_s):
    i = pl.program_id(1)
    nblk = vt_ref.shape[1]
    tq = MOBA_BLOCK
    hd = HEAD_DIM
    chains = [(r, j) for r in range(qt_ref.shape[0]) for j in range(N_HEADS)]
    slots = range(len(chains))

    blk = _iota((nblk, tq), 0)
    qas = []
    for r, j in chains:
        qn = qt_ref[r, j * hd:(j + 1) * hd, :]
        km = jnp.concatenate([km_ref[r, t, :, j * hd:(j + 1) * hd] for t in range(km_ref.shape[1])], axis=0)
        gate = jnp.where(blk < i, _dot_fine(km, qn), NEG)
        chosen = _rank_select(gate, blk, nblk, MOBA_TOPK) & (blk < i)
        sel_bias = jnp.where(chosen | (blk == i), 0.0, -BIG)
        qas.append(_augment_query(qn, alibi_ref[j], sel_bias, flipped=j % 2 == 1))

    def scores(slot, n):
        r, j = chains[slot]
        return _dot(ka_ref[r, j, pl.ds(pl.multiple_of(n * tq, tq), tq), :], qas[slot])

    def values(slot, n):
        r, j = chains[slot]
        return vt_ref[r, n, j * hd:(j + 1) * hd, :]

    state = (m_s, acc_s)
    _flash_reset(state)
    bufs = (sc0_s, sc1_s)
    for slot in slots:
        sc0_s[slot] = scores(slot, 0)

    def body(n, carry):
        _flash_update_pipelined(state, bufs, slots, n, lambda slot: scores(slot, n + 1),
                                lambda slot: values(slot, n))
        return carry

    lax.fori_loop(0, i, body, 0)
    diag_cap = _causal_cap(tq, tq, 1)
    _flash_update_pipelined(state, bufs, slots, i, None, lambda slot: values(slot, i), diag_cap)
    for r in range(qt_ref.shape[0]):
        outs = [_flash_result(state, r * N_HEADS + j) for j in range(N_HEADS)]
        y = jnp.concatenate(outs, axis=0).T * _silu(z_ref[r])
        o_ref[r] = y.astype(o_ref.dtype)


def _moba(z, q_t, ka, vt, km, *, rows_per_step=2):
    bsz, s, _ = z.shape
    tq = MOBA_BLOCK
    gw = GROUP_WIDTH
    nblk = s // MOBA_BLOCK
    rps = rows_per_step
    assert SEL_FEAT0 + nblk <= HEAD_DIM and bsz % rps == 0
    whole = lambda a: pl.BlockSpec((rps,) + a.shape[1:], lambda b, i: (b,) + (0,) * (a.ndim - 1))
    return pl.pallas_call(
        _moba_kernel,
        grid=(bsz // rps, s // tq),
        in_specs=[pl.BlockSpec((N_HEADS, 2 * SLOPE_PIECES, tq), lambda b, i: (0, 0, 0)),
                  pl.BlockSpec((rps, gw, tq), lambda b, i: (b, 0, i)), whole(ka), whole(vt), whole(km),
                  pl.BlockSpec((rps, tq, gw), lambda b, i: (b, i, 0))],
        out_specs=pl.BlockSpec((rps, tq, gw), lambda b, i: (b, i, 0)),
        out_shape=jax.ShapeDtypeStruct((bsz, s, gw), BF16),
        scratch_shapes=_flash_scratch(rps * N_HEADS, tq) + [pltpu.VMEM((rps * N_HEADS, tq, tq), F32)] * 2,
        name="moba",
    )(jnp.asarray(_alibi_query_features(tq)), q_t, ka, vt, km, z)


def _cmp_kernel(xk_ref, xv_ref, wk1_ref, wv1_ref, pk_ref, pv_ref, bk1_ref, bv1_ref,
                wk2_ref, wv2_ref, bk2_ref, bv2_ref, gn_ref, ko_ref, vo_ref, wke_s, wve_s):
    half = xk_ref.shape[2]
    n = xk_ref.shape[1]
    hd = HEAD_DIM
    nh = NSA_CMP_HIDDEN

    @pl.when(pl.program_id(0) == 0)
    def _():
        for w_ref, we_ref in ((wk1_ref, wke_s), (wv1_ref, wve_s)):
            we_ref[...] = jnp.zeros(we_ref.shape, BF16)
            for tok in range(NSA_CMP_LEN):
                for g in range(KV_HEADS):
                    row = (tok * KV_HEADS + g) * hd
                    we_ref[row:row + hd, g * nh:(g + 1) * nh] = w_ref[tok * hd:(tok + 1) * hd, :]

    rows = xk_ref.shape[0]

    def hidden(x_ref, we_ref, pos_ref, b1_ref):
        x = x_ref[...].reshape(rows * n, half).astype(BF16)
        first = _dot(x, we_ref[:half, :])
        second = _dot(x, we_ref[half:, :])
        second = jnp.concatenate([pltpu.roll(second[r * n:(r + 1) * n], n - 1, 0) for r in range(rows)], axis=0)
        pos = jnp.broadcast_to(pos_ref[...], (8, 2 * half)).astype(BF16)
        ph = _dot(pos[:, :half], we_ref[:half, :]) + _dot(pos[:, half:], we_ref[half:, :])
        return _silu(first + second + ph[0:1, :] + b1_ref[...])

    hk = hidden(xk_ref, wke_s, pk_ref, bk1_ref).astype(BF16)
    hv = hidden(xv_ref, wve_s, pv_ref, bv1_ref).astype(BF16)
    ks, vs = [], []
    for g in range(KV_HEADS):
        ks.append(_rms(_dot(hk[:, g * nh:(g + 1) * nh], wk2_ref[...]) + bk2_ref[...], gn_ref[...]))
        vs.append(_dot(hv[:, g * nh:(g + 1) * nh], wv2_ref[...]) + bv2_ref[...])
    ko_ref[...] = jnp.concatenate(ks, axis=1).reshape(ko_ref.shape)
    vo_ref[...] = jnp.concatenate(vs, axis=1).reshape(vo_ref.shape)


def _compress(xk, xv, pos_k, pos_v, kw1, kb1, kw2, kb2, vw1, vb1, vw2, vb2, gn, *, rows_per_step=4):
    bsz, n, half = xk.shape
    hd = HEAD_DIM
    rps = rows_per_step
    assert bsz % rps == 0
    tile_pos = lambda p: jnp.tile(p[:, None, :], (1, KV_HEADS, 1)).reshape(1, 2 * half)
    tile_b = lambda b: jnp.tile(b.reshape(1, -1), (1, KV_HEADS))
    const = lambda shape: pl.BlockSpec(shape, lambda b: (0,) * len(shape))
    xs = pl.BlockSpec((rps, n, half), lambda b: (b, 0, 0))
    hw = KV_HEADS * NSA_CMP_HIDDEN
    out = pl.BlockSpec((rps, n, KV_HEADS * hd), lambda b: (b, 0, 0))
    expanded = pltpu.VMEM((2 * half, hw), BF16)
    return pl.pallas_call(
        _cmp_kernel,
        grid=(bsz // rps,),
        in_specs=[xs, xs, const(kw1.shape), const(vw1.shape),
                  const((1, 2 * half)), const((1, 2 * half)), const((1, hw)), const((1, hw)),
                  const((NSA_CMP_HIDDEN, hd)), const((NSA_CMP_HIDDEN, hd)), const((1, hd)), const((1, hd)),
                  const((1, hd))],
        out_specs=[out, out],
        out_shape=[jax.ShapeDtypeStruct((bsz, n, KV_HEADS * hd), F32)] * 2,
        scratch_shapes=[expanded, expanded],
        name="nsa_compress",
    )(xk, xv, kw1.astype(BF16), vw1.astype(BF16), tile_pos(pos_k), tile_pos(pos_v), tile_b(kb1), tile_b(vb1),
      kw2.astype(BF16), vw2.astype(BF16), kb2.reshape(1, hd), vb2.reshape(1, hd), gn.reshape(1, hd))


def _group_query(q_t, alibi_ref, g, gq_col):
    hd = HEAD_DIM
    heads = range(Q_PER_KV * g, Q_PER_KV * (g + 1))
    qn = jnp.concatenate([_norm_query_t(q_t[h * hd:(h + 1) * hd], gq_col) for h in heads], axis=1)
    alibi = jnp.concatenate([alibi_ref[h] for h in heads], axis=1)
    return qn, alibi


def _augment_keys(k, gk_ref, dst_ref, row=0, sel_shift=None):
    assert KV_HEADS == 2
    pos = _iota(k.shape, 0)
    dst_ref[2 * row], dst_ref[2 * row + 1] = _augment_pair_keys(_pair_rms(k, gk_ref[...]), pos, sel_shift)


def _chunk_scores(ka_ref, vt_ref, g, c, qa, tq):
    off = pl.multiple_of(c * tq, tq)
    return _dot(ka_ref[g, pl.ds(off, tq), :], qa), vt_ref[c, g * HEAD_DIM:(g + 1) * HEAD_DIM, :]


def _window_first_chunk(ka_ref, vt_ref, g, far, qa, tq, edge_cap):
    sc, v_t = _chunk_scores(ka_ref, vt_ref, g, jnp.maximum(far, 0), qa, tq)
    return jnp.minimum(sc, jnp.minimum(edge_cap, jnp.where(far >= 0, BIG, -BIG))), v_t


def _regroup_heads(o_t, tq):
    return jnp.concatenate([o_t[:, k * tq:(k + 1) * tq] for k in range(Q_PER_KV)], axis=0)


def _nsa_kernel(alibi_ref, q_ref, kc_ref, vc_ref, ks_ref, vs_ref, kw_ref, vw_ref, gate_ref, z_ref,
                gq_ref, gks_ref, gkw_ref, o_ref, kca_s, vct_s, ksa_s, vst_s, kwa_s, vwt_s, m_s, acc_s,
                sc0_s, sc1_s):
    i = pl.program_id(1)
    rows = range(q_ref.shape[0])
    tq = q_ref.shape[1]
    s = ks_ref.shape[1]
    hd = HEAD_DIM
    ncmp = kc_ref.shape[1]
    nsb = s // NSA_SEL_BLOCK

    @pl.when(i == 0)
    def _():
        cend = _iota((ncmp, LANES), 0) * NSA_CMP_STRIDE + (NSA_CMP_LEN - 1)
        for r in rows:
            _augment_keys(ks_ref[r], gks_ref, ksa_s, r, NSA_SEL_BLOCK.bit_length() - 1)
            _augment_keys(kw_ref[r], gkw_ref, kwa_s, r)
            _store_chunked_t(vs_ref[r], vst_s.at[r], tq)
            _store_chunked_t(vw_ref[r], vwt_s.at[r], tq)
            kca_s[2 * r], kca_s[2 * r + 1] = _augment_pair_keys(kc_ref[r], cend)
            vct_s[r] = vc_ref[r].T.astype(BF16)

    t0 = i * tq
    diag_cap = _causal_cap(tq, tq, Q_PER_KV)
    edge_cap = _window_edge_cap(tq, tq, Q_PER_KV)
    cend =_iota((ncmp, tq), 0) * NSA_CMP_STRIDE + (NSA_CMP_LEN - 1)
    seen = _tile_lanes(cend <= t0 + _iota((ncmp, tq), 1), Q_PER_KV)
    cstart = _iota((nsb, ncmp), 1) * NSA_CMP_STRIDE
    bstart = _iota((nsb, ncmp), 0) * NSA_SEL_BLOCK
    overlap_t = jnp.where((cstart < bstart + NSA_SEL_BLOCK) & (cstart + NSA_CMP_LEN > bstart), 1.0, 0.0)
    blk = _iota((nsb, tq), 0)
    cur = (t0 + _iota((nsb, tq), 1)) >> (NSA_SEL_BLOCK.bit_length() - 1)
    forced = (blk == 0) | (blk == cur) | (blk == cur - 1)
    groups = range(KV_HEADS)
    far = i - NSA_WINDOW // tq
    lo = jnp.maximum(far + 1, 0)
    first_win = jnp.maximum(far, 0)

    state = (m_s, acc_s)
    kv = [(r, g) for r in rows for g in groups]
    sel_slots = tuple(range(len(kv)))
    win_slots = tuple(len(kv) + n for n in range(len(kv)))
    _flash_reset(state)
    q_ts = [q_ref[r].T for r in rows]
    queries = [_group_query(q_ts[r], alibi_ref, g, gq_ref[...]) for r, g in kv]
    qa_win = [_augment_query(qn, alibi, flipped=g == 1) for (r, g), (qn, alibi) in zip(kv, queries)]
    qa_sel = []

    def scores(slot, c):
        n = slot % len(kv)
        ka_ref, qa = (ksa_s, qa_sel) if slot in sel_slots else (kwa_s, qa_win)
        return _dot(ka_ref[n, pl.ds(pl.multiple_of(c * tq, tq), tq), :], qa[n])

    def values(slot, c):
        r, g = kv[slot % len(kv)]
        return (vst_s if slot in sel_slots else vwt_s)[r, c, g * hd:(g + 1) * hd, :]

    cmp_sc = [_dot(kca_s[n], qa_win[n]) for n in range(len(kv))]
    sc_win_first = [scores(slot, first_win) for slot in win_slots]
    o_cmp = []
    for n, (r, g) in enumerate(kv):
        qn, alibi = queries[n]
        sc = jnp.where(seen, cmp_sc[n], -BIG)
        e = jnp.where(seen, jnp.exp2(sc - jnp.max(sc, axis=0, keepdims=True)), 0.0)
        p_c = e / jnp.maximum(jnp.sum(e, axis=0, keepdims=True), TINY)
        o_cmp.append(_dot(vct_s[r, g * hd:(g + 1) * hd, :], p_c.astype(BF16)))

        psum = p_c[:, 0:tq]
        for k in range(1, Q_PER_KV):
            psum = psum + p_c[:, k * tq:(k + 1) * tq]
        imp = _dot_fine(overlap_t, psum, a_is_bf16_exact=True)
        imp = jnp.where(blk <= cur, jnp.where(forced, NSA_FORCE, imp), NEG)
        chosen = _rank_select(imp, blk, nsb, min(NSA_TOPN, nsb))
        sel_bias = _tile_lanes(jnp.where(chosen, 0.0, -BIG), Q_PER_KV)
        qa_sel.append(_augment_query(qn, alibi, sel_bias, flipped=g == 1))

    bufs = (sc0_s, sc1_s)
    for slot in sel_slots:
        sc0_s[slot] = scores(slot, 0)
    for slot, sc in zip(win_slots, sc_win_first):
        sc0_s[slot] = sc
        sc1_s[slot] = sc

    def body(slots, cap, cap_slots, c, carry):
        _flash_update_pipelined(state, bufs, slots, c, lambda slot: scores(slot, c + 1),
                                lambda slot: values(slot, c), cap, cap_slots)
        return carry

    both_slots = sel_slots + win_slots
    lax.fori_loop(0, first_win, functools.partial(body, sel_slots, None, None), 0)
    lax.fori_loop(first_win, lo, functools.partial(body, both_slots, edge_cap, win_slots), 0)
    lax.fori_loop(lo, i, functools.partial(body, both_slots, None, None), 0)
    _flash_update_pipelined(state, bufs, both_slots, i, None, lambda slot: values(slot, i), diag_cap)

    for r in rows:
        gates_t = _sigmoid(gate_ref[r]).T
        outs = []
        for g in groups:
            def gate_row(branch):
                parts = [gates_t[branch * N_HEADS + h:branch * N_HEADS + h + 1, :]
                         for h in range(Q_PER_KV * g, Q_PER_KV * (g + 1))]
                return jnp.concatenate(parts, axis=1)

            n = KV_HEADS * r + g
            o_slc = _flash_result(state, sel_slots[n])
            o_win = _flash_result(state, win_slots[n])
            o = gate_row(0) * o_cmp[n] + gate_row(1) * o_slc + gate_row(2) * o_win
            outs.append(_regroup_heads(o, tq))
        y = jnp.concatenate(outs, axis=0).T * _silu(z_ref[r])
        o_ref[r] = y.astype(o_ref.dtype)


def _nsa(proj, kcn, vc, gq, gks, gkw, *, tq=256, rows_per_step=2):
    bsz, s, _ = proj.shape
    hd = HEAD_DIM
    gw = GROUP_WIDTH
    ncmp = kcn.shape[1]
    rps = rows_per_step
    chains = 2 * KV_HEADS * rps
    assert SEL_FEAT0 + s // NSA_SEL_BLOCK <= hd and ncmp == LANES and bsz % rps == 0
    wide = lambda c0: pl.BlockSpec((rps, tq, gw), lambda b, i: (b, i, c0 // 4))
    full = lambda c0: pl.BlockSpec((rps, s, LANES), lambda b, i: (b, 0, c0))
    cmp = pl.BlockSpec((rps, ncmp, LANES), lambda b, i: (b, 0, 0))
    row = pl.BlockSpec((1, hd), lambda b, i: (0, 0))
    ka = pltpu.VMEM((KV_HEADS * rps, s, LANES), BF16)
    vt = pltpu.VMEM((rps, s // tq, LANES, tq), BF16)
    return pl.pallas_call(
        _nsa_kernel,
        grid=(bsz // rps, s // tq),
        in_specs=[pl.BlockSpec((N_HEADS, 2 * SLOPE_PIECES, tq), lambda b, i: (0, 0, 0)),
                  wide(ODD_QC), cmp, cmp, full(ODD_KS), full(ODD_VS), full(ODD_KW), full(ODD_VW),
                  pl.BlockSpec((rps, tq, LANES), lambda b, i: (b, i, ODD_GC)), wide(ODD_ZC),
                  pl.BlockSpec((hd, 1), lambda b, i: (0, 0)), row, row],
        out_specs=pl.BlockSpec((rps, tq, gw), lambda b, i: (b, i, 0)),
        out_shape=jax.ShapeDtypeStruct((bsz, s, gw), BF16),
        scratch_shapes=[pltpu.VMEM((KV_HEADS * rps, ncmp, LANES), BF16), pltpu.VMEM((rps, LANES, ncmp), BF16),
                        ka, vt, ka, vt] + _flash_scratch(chains, Q_PER_KV * tq)
                       + [pltpu.VMEM((chains, tq, Q_PER_KV * tq), F32)] * 2,
        name="nsa",
    )(jnp.asarray(_alibi_query_features(tq)), proj, kcn, vc, proj, proj, proj, proj, proj, proj,
      gq.reshape(hd, 1), gks.reshape(1, hd), gkw.reshape(1, hd))


def _swa_kernel(sinks_ref, alibi_ref, q_ref, k_ref, v_ref, z_ref, gq_ref, gk_ref, o_ref, ka_s, vt_s):
    step = pl.program_id(1)
    tq = SWA_WINDOW
    tiles = q_ref.shape[1] // tq
    s = k_ref.shape[1]

    @pl.when(step == 0)
    def _():
        _augment_keys(k_ref[0], gk_ref, ka_s)
        _store_chunked_t(v_ref[0], vt_s, tq)

    q_t = q_ref[0].T
    diag_cap = _causal_cap(tq, tq, Q_PER_KV)
    edge_cap = _window_edge_cap(tq, tq, Q_PER_KV)
    chains = [(u, g) for u in range(tiles) for g in range(KV_HEADS)]
    qas = [_augment_query(*_group_query(q_t[:, u * tq:(u + 1) * tq], alibi_ref, g, gq_ref[...]), flipped=g == 1)
           for u, g in chains]
    first = [_window_first_chunk(ka_s, vt_s, g, step * tiles + u - 1, qa, tq, edge_cap)
             for (u, g), qa in zip(chains, qas)]
    own = [_chunk_scores(ka_s, vt_s, g, step * tiles + u, qa, tq) for (u, g), qa in zip(chains, qas)]
    carries = _flash_steps([_flash_init(Q_PER_KV * tq) for _ in chains], first)
    carries = _flash_steps(carries, own, diag_cap)
    outs = []
    for (u, g), carry in zip(chains, carries):
        m, acc, l = _flash_split(carry)
        tpos = ((step * tiles + u) * tq + _iota((1, tq), 1)).astype(F32)
        sink = jnp.concatenate([sinks_ref[h] * LOG2E + (SLOPES[h] * LOG2E) * tpos
                                for h in range(Q_PER_KV * g, Q_PER_KV * (g + 1))], axis=1)
        mf = jnp.maximum(m, sink)
        alpha = jnp.exp2(m - mf)
        o = acc * alpha / jnp.maximum(l * alpha + jnp.exp2(sink - mf), TINY)
        outs.append(_regroup_heads(o, tq))
    o_t = jnp.concatenate([jnp.concatenate(outs[u * KV_HEADS:(u + 1) * KV_HEADS], axis=0) for u in range(tiles)],
                          axis=1)
    y = o_t.T * _silu(z_ref[0])
    o_ref[0] = y.astype(o_ref.dtype)


def _swa(proj, sinks, gq, gk, *, tiles_per_step=8):
    bsz, s, _ = proj.shape
    hd = HEAD_DIM
    gw = GROUP_WIDTH
    tq = SWA_WINDOW
    rows = tiles_per_step * tq
    wide = lambda c0: pl.BlockSpec((1, rows, gw), lambda b, i: (b, i, c0 // 4))
    full = lambda c0: pl.BlockSpec((1, s, LANES), lambda b, i: (b, 0, c0))
    return pl.pallas_call(
        _swa_kernel,
        grid=(bsz, s // rows),
        in_specs=[pl.BlockSpec(memory_space=pltpu.SMEM),
                  pl.BlockSpec((N_HEADS, 2 * SLOPE_PIECES, tq), lambda b, i: (0, 0, 0)),
                  wide(ODD_QD), full(ODD_KD), full(ODD_VD), wide(ODD_ZD),
                  pl.BlockSpec((hd, 1), lambda b, i: (0, 0)), pl.BlockSpec((1, hd), lambda b, i: (0, 0))],
        out_specs=pl.BlockSpec((1, rows, gw), lambda b, i: (b, i, 0)),
        out_shape=jax.ShapeDtypeStruct((bsz, s, gw), BF16),
        scratch_shapes=[pltpu.VMEM((KV_HEADS, s, LANES), BF16), pltpu.VMEM((s // tq, LANES, tq), BF16)],
        name="swa",
    )(sinks.astype(F32), jnp.asarray(_alibi_query_features(tq)), proj, proj, proj, proj,
      gq.reshape(hd, 1), gk.reshape(1, hd))


def _pack_odd_weight(w):
    gw, kw, ng = GROUP_WIDTH, KV_HEADS * HEAD_DIM, 3 * N_HEADS
    sizes = [gw, kw, kw, kw, kw, kw, kw, ng, gw, gw, kw, kw, gw]
    starts = np.concatenate([[0], np.cumsum(sizes)]).tolist()
    part = lambda k: w[:, starts[k]:starts[k + 1]]
    pad = jnp.zeros((w.shape[0], LANES - ng), w.dtype)
    order = [0, 8, 9, 12, 1, 2, 3, 4, 5, 6, 10, 11, 7]
    return jnp.concatenate([part(k) for k in order] + [pad], axis=1).astype(BF16)


def kernel(x, norm_g, w_out, e_w_in, a_conv_w, a_conv_b, a_ln_g, a_ln_b, b_qnorm_g, b_knorm_g, o_w_in, c_qnorm_g, c_knorm_cmp_g, c_knorm_slc_g, c_knorm_win_g, c_pos_k, c_pos_v, c_k_w1, c_k_b1, c_k_w2, c_k_b2, c_v_w1, c_v_b1, c_v_w2, c_v_b2, d_qnorm_g, d_knorm_g, d_sinks):
    bsz, s, d = x.shape
    m = bsz * s
    assert s % MOBA_BLOCK == 0 and d == 2 * GROUP_WIDTH
    x2 = x.reshape(m, d)

    y_a, z_b, q_t, ka, vt, km = _even_layer_front(
        x2, norm_g[0], e_w_in[0].astype(BF16), b_qnorm_g[0], b_knorm_g[0],
        a_conv_w[0], a_conv_b[0], a_ln_g[0], a_ln_b[0], bsz=bsz)
    y_b = _moba(z_b.reshape(bsz, s, -1), q_t, ka, vt, km)

    x2, proj, k_cmp, v_cmp = _layer_boundary(
        x2, y_a, y_b.reshape(m, -1), w_out[0].astype(BF16), norm_g[1], _pack_odd_weight(o_w_in[0]),
        chunk=5 * LANES, copies=((ODD_KC * LANES, LANES), (ODD_VC * LANES, LANES)), group=NSA_CMP_STRIDE)
    proj = proj.reshape(bsz, s, ODD_COLS)
    rows16 = lambda t: t.reshape(bsz, s // NSA_CMP_STRIDE, NSA_CMP_STRIDE * LANES)
    kcn, vc = _compress(rows16(k_cmp), rows16(v_cmp), c_pos_k[0], c_pos_v[0], c_k_w1[0], c_k_b1[0],
                        c_k_w2[0], c_k_b2[0], c_v_w1[0], c_v_b1[0], c_v_w2[0], c_v_b2[0], c_knorm_cmp_g[0])
    y_c = _nsa(proj, kcn, vc, c_qnorm_g[0], c_knorm_slc_g[0], c_knorm_win_g[0])
    y_d = _swa(proj, d_sinks[0], d_qnorm_g[0], d_knorm_g[0])
    x2 = _outproj(x2, y_c.reshape(m, -1), y_d.reshape(m, -1), w_out[1].astype(BF16))
    return x2.reshape(bsz, s, d)
```

```python
import functools

import ml_dtypes
import numpy as np
import jax
import jax.numpy as jnp
from jax import lax
from jax.experimental import pallas as pl
from jax.experimental.pallas import tpu as pltpu

HEAD_DIM = 64
N_HEADS = 8
GROUP_WIDTH = N_HEADS * HEAD_DIM
CONV_WIDTH = 31
MOBA_BLOCK = 256
MOBA_TOPK = 3
KV_HEADS = 2
Q_PER_KV = N_HEADS // KV_HEADS
NSA_CMP_LEN = 32
NSA_CMP_STRIDE = 16
NSA_CMP_HIDDEN = 256
NSA_SEL_BLOCK = 64
NSA_TOPN = 16
NSA_WINDOW = 512
NSA_FORCE = 1e4
SWA_WINDOW = 128
EPS = 1e-6
NEG = -1e30
TINY = 1e-30
LANES = 128
SUBLANES = 8
CONV_HALO = 32

LOG2E = float(np.log2(np.e))
BIG = 2.0 ** 99
M_INIT = -1e38
POS_SPLIT_SHIFT = 8
SLOPE_PIECES = 4
SEL_FEAT0 = 2 * SLOPE_PIECES

F32 = jnp.float32
BF16 = jnp.bfloat16

ODD_QC, ODD_ZC, ODD_QD, ODD_ZD = 0, 4, 8, 12
ODD_KC, ODD_VC, ODD_KS, ODD_VS, ODD_KW, ODD_VW, ODD_KD, ODD_VD, ODD_GC = 16, 17, 18, 19, 20, 21, 22, 23, 24
ODD_COLS = 25 * LANES


def _alibi_slopes(n):
    return [float(2.0 ** (-8.0 * (i + 1) / n)) for i in range(n)]


SLOPES = _alibi_slopes(N_HEADS)


def _alibi_query_features(width):
    table = np.zeros((N_HEADS, 2 * SLOPE_PIECES, width), np.float32)
    for h, slope in enumerate(SLOPES):
        rest = np.float64(slope) * LOG2E
        for k in range(SLOPE_PIECES):
            piece = float(np.float32(rest).astype(ml_dtypes.bfloat16).astype(np.float32))
            table[h, k] = piece
            table[h, SLOPE_PIECES + k] = piece
            rest -= piece
    return table


def _dot(a, b):
    return jnp.dot(a, b, preferred_element_type=F32)


def _split_bf16(x, pieces):
    out = []
    for _ in range(pieces):
        out.append(x.astype(BF16))
        x = x - out[-1].astype(F32)
    return out


def _dot_fine(a, b, a_is_bf16_exact=False):
    if a_is_bf16_exact:
        return sum(_dot(a.astype(BF16), piece) for piece in _split_bf16(b, 3))
    (a_hi, a_lo), (b_hi, b_lo) = _split_bf16(a, 2), _split_bf16(b, 2)
    return _dot(a_hi, b_hi) + (_dot(a_hi, b_lo) + _dot(a_lo, b_hi))


def _sigmoid(x):
    return 1.0 / (1.0 + jnp.exp(-x))


def _silu(x):
    return x * _sigmoid(x)


def _rms(x, g):
    return x * lax.rsqrt(jnp.mean(x * x, axis=-1, keepdims=True) + EPS) * g


def _iota(shape, dim):
    return lax.broadcasted_iota(jnp.int32, shape, dim)


def _layer_boundary_kernel(x_ref, ya_ref, yb_ref, wa_ref, wb_ref, g_ref, w_ref, x_out_ref, o_ref, *rest,
                           chunk, copies, group):
    copy_refs, stage_refs = rest[:len(copies)], rest[len(copies):]
    x = x_ref[...] + _dot(ya_ref[...], wa_ref[...]) + _dot(yb_ref[...], wb_ref[...])
    x_out_ref[...] = x
    h = _rms(x, g_ref[...]).astype(BF16)
    for c in range(o_ref.shape[1] // chunk):
        o_ref[:, c * chunk:(c + 1) * chunk] = _dot(h, w_ref[:, c * chunk:(c + 1) * chunk])
    for ref, stage, (start, width) in zip(copy_refs, stage_refs, copies):
        stage[...] = o_ref[:, start:start + width]
        for t in range(group):
            ref[:, t * width:(t + 1) * width] = stage[pl.ds(t, ref.shape[0], stride=group), :]


def _layer_boundary(x2d, ya, yb, w_out, g, w, *, tm=512, chunk, copies=(), group=1):
    m, d = x2d.shape
    e = w.shape[1]
    gw = ya.shape[1]
    rows = lambda n, width: pl.BlockSpec((n, width), lambda i: (i, 0))
    const = lambda shape: pl.BlockSpec(shape, lambda i: (0, 0))
    return pl.pallas_call(
        functools.partial(_layer_boundary_kernel, chunk=chunk, copies=tuple(copies), group=group),
        grid=(m // tm,),
        in_specs=[rows(tm, d), rows(tm, gw), rows(tm, gw), const((gw, d)), const((gw, d)), const((1, d)),
                  const((d, e))],
        out_specs=[rows(tm, d), rows(tm, e)] + [rows(tm // group, group * width) for _, width in copies],
        out_shape=[jax.ShapeDtypeStruct((m, d), F32), jax.ShapeDtypeStruct((m, e), F32)]
                  + [jax.ShapeDtypeStruct((m // group, group * width), F32) for _, width in copies],
        scratch_shapes=[pltpu.VMEM((tm, width), F32) for _, width in copies],
        name="layer_boundary",
    )(x2d, ya, yb, w_out[:gw], w_out[gw:], g.reshape(1, d), w)


def _conv_rows(h_s, row0, rows, w_ref, b_ref, lg_ref, lb_ref):
    base = CONV_HALO - (CONV_WIDTH - 1)
    acc = None
    for b in range(SUBLANES):
        n = rows if b == 0 else rows + SUBLANES
        part = None
        for a in range((base + CONV_WIDTH - 1) // SUBLANES + 1):
            j = SUBLANES * a + b - base
            if 0 <= j < CONV_WIDTH:
                term = w_ref[j:j + 1, :] * h_s[pl.ds(row0 + SUBLANES * a, n), :]
                part = term if part is None else part + term
        part = part[b:b + rows]
        acc = part if acc is None else acc + part
    y = acc + b_ref[...]
    mu = jnp.mean(y, axis=-1, keepdims=True)
    yc = y - mu
    return _silu(yc * lax.rsqrt(jnp.mean(yc * yc, axis=-1, keepdims=True) + EPS) * lg_ref[...] + lb_ref[...])


def _even_layer_kernel(x_ref, g_ref, w_ref, gq_ref, gk_ref, cw_ref, cb_ref, lg_ref, lb_ref,
                       ya_ref, zb_ref, qt_ref, ka_ref, vt_ref, km_ref, hb_s, conv_s, sec_s, h_s, tail_s,
                       *, seq, rows):
    tm = x_ref.shape[0]
    hd, gw = HEAD_DIM, GROUP_WIDTH
    tile_in_seq = pl.program_id(0) % (seq // tm)
    hb_s[...] = _rms(x_ref[...], g_ref[...]).astype(BF16)
    for c in range(3):
        conv_s[:, c * gw:(c + 1) * gw] = _dot(hb_s[...], w_ref[:, c * gw:(c + 1) * gw])
    h_s[0:CONV_HALO, :] = jnp.where(tile_in_seq > 0, tail_s[...], 0.0)
    h_s[CONV_HALO:, :] = conv_s[:, 0:gw] * _sigmoid(conv_s[:, gw:2 * gw])
    tail_s[...] = h_s[tm:tm + CONV_HALO, :]

    def project(sec, dst_ref):
        dst_ref[...] = _dot(hb_s[...], w_ref[:, sec * gw:(sec + 1) * gw])

    def moba_queries():
        project(3, sec_s)
        q_t = sec_s[...].T
        for j in range(N_HEADS):
            qt_ref[0, j * hd:(j + 1) * hd, :] = _norm_query_t(q_t[j * hd:(j + 1) * hd], gq_ref[...])

    def moba_keys():
        project(4, sec_s)
        pos = tile_in_seq * tm + _iota((tm, LANES), 0)
        for pair in range(N_HEADS // 2):
            kn = _pair_rms(sec_s[:, pair * LANES:(pair + 1) * LANES], gk_ref[...])
            ka_ref[0, 2 * pair], ka_ref[0, 2 * pair + 1] = _augment_pair_keys(kn, pos,
                                                                              MOBA_BLOCK.bit_length() - 1)
            km_ref[0, 0, :, pair * LANES:(pair + 1) * LANES] = jnp.mean(
                kn.reshape(tm // MOBA_BLOCK, MOBA_BLOCK, LANES), axis=1)

    def moba_values():
        project(5, sec_s)
        v_t = sec_s[...].T.astype(BF16)
        for c in range(tm // MOBA_BLOCK):
            vt_ref[0, c] = v_t[:, c * MOBA_BLOCK:(c + 1) * MOBA_BLOCK]

    def moba_gate_path():
        project(6, zb_ref)

    for c in range(tm // rows):
        y = _conv_rows(h_s, c * rows, rows, cw_ref, cb_ref, lg_ref, lb_ref)
        gate = _silu(conv_s[c * rows:(c + 1) * rows, 2 * gw:3 * gw])
        ya_ref[c * rows:(c + 1) * rows, :] = (y * gate).astype(ya_ref.dtype)
    moba_queries()
    moba_keys()
    moba_values()
    moba_gate_path()


def _even_layer_front(x2d, g, w, gq, gk, conv_w, conv_b, ln_g, ln_b, *, bsz, tm=512, rows=128):
    m, d = x2d.shape
    e = w.shape[1]
    s = m // bsz
    per_b = s // tm
    hd, gw, nb = HEAD_DIM, GROUP_WIDTH, tm // MOBA_BLOCK
    assert s % tm == 0 and tm % MOBA_BLOCK == 0 and e == 7 * gw and tm % rows == 0
    const = lambda shape: pl.BlockSpec(shape, lambda i: (0, 0))
    tile = pl.BlockSpec((tm, gw), lambda i: (i, 0))
    vec = const((1, gw))
    return pl.pallas_call(
        functools.partial(_even_layer_kernel, seq=s, rows=rows),
        grid=(m // tm,),
        in_specs=[pl.BlockSpec((tm, d), lambda i: (i, 0)), const((1, d)), const((d, e)), const((hd, 1)),
                  const((1, hd)), const((CONV_WIDTH, gw)), vec, vec, vec],
        out_specs=[tile, tile,
                   pl.BlockSpec((1, gw, tm), lambda i: (i // per_b, 0, i % per_b)),
                   pl.BlockSpec((1, N_HEADS, tm, LANES), lambda i: (i // per_b, 0, i % per_b, 0)),
                   pl.BlockSpec((1, nb, gw, MOBA_BLOCK), lambda i: (i // per_b, i % per_b, 0, 0)),
                   pl.BlockSpec((1, 1, nb, gw), lambda i: (i // per_b, i % per_b, 0, 0))],
        out_shape=[jax.ShapeDtypeStruct((m, gw), BF16),
                   jax.ShapeDtypeStruct((m, gw), F32),
                   jax.ShapeDtypeStruct((bsz, gw, s), F32),
                   jax.ShapeDtypeStruct((bsz, N_HEADS, s, LANES), BF16),
                   jax.ShapeDtypeStruct((bsz, s // MOBA_BLOCK, gw, MOBA_BLOCK), BF16),
                   jax.ShapeDtypeStruct((bsz, per_b, nb, gw), F32)],
        scratch_shapes=[pltpu.VMEM((tm, d), BF16), pltpu.VMEM((tm, 3 * gw), F32), pltpu.VMEM((tm, gw), F32),
                        pltpu.VMEM((CONV_HALO + tm, gw), F32), pltpu.VMEM((CONV_HALO, gw), F32)],
        name="even_layer_front",
    )(x2d, g.reshape(1, d), w, gq.reshape(hd, 1), gk.reshape(1, hd), conv_w, conv_b.reshape(1, gw),
      ln_g.reshape(1, gw), ln_b.reshape(1, gw))


def _outproj_kernel(x_ref, ya_ref, yb_ref, wa_ref, wb_ref, o_ref):
    o_ref[...] = x_ref[...] + _dot(ya_ref[...], wa_ref[...]) + _dot(yb_ref[...], wb_ref[...])


def _outproj(x2d, ya, yb, w, *, tm=1024):
    m, d = x2d.shape
    gw = ya.shape[1]
    return pl.pallas_call(
        _outproj_kernel,
        grid=(m // tm,),
        in_specs=[pl.BlockSpec((tm, d), lambda i: (i, 0)),
                  pl.BlockSpec((tm, gw), lambda i: (i, 0)),
                  pl.BlockSpec((tm, gw), lambda i: (i, 0)),
                  pl.BlockSpec((gw, d), lambda i: (0, 0)),
                  pl.BlockSpec((gw, d), lambda i: (0, 0))],
        out_specs=pl.BlockSpec((tm, d), lambda i: (i, 0)),
        out_shape=jax.ShapeDtypeStruct((m, d), F32),
        name="outproj",
    )(x2d, ya, yb, w[:gw], w[gw:])


def _key_features(pos, flipped, sel_shift=None):
    col = _iota(pos.shape, 1) - (0 if flipped else HEAD_DIM)
    hi = (pos >> POS_SPLIT_SHIFT) << POS_SPLIT_SHIFT
    lo = pos & ((1 << POS_SPLIT_SHIFT) - 1)
    feat = jnp.where(col < SLOPE_PIECES, hi, jnp.where(col < 2 * SLOPE_PIECES, lo, 0))
    if sel_shift is not None:
        feat = jnp.where(col - SEL_FEAT0 == (pos >> sel_shift), 1, feat)
    return jnp.where((col >= 0) & (col < HEAD_DIM), feat, 0).astype(F32)


def _pair_rms(x, g):
    sq = x * x
    hi = sq.astype(BF16)
    lo = (sq - hi.astype(F32)).astype(BF16)
    head_shift = HEAD_DIM.bit_length() - 1
    same_head = (_iota((LANES, LANES), 0) >> head_shift) == (_iota((LANES, LANES), 1) >> head_shift)
    ones = jnp.where(same_head, 1.0, 0.0).astype(BF16)
    ss = _dot(hi, ones) + _dot(lo, ones)
    return x * lax.rsqrt(ss * (1.0 / HEAD_DIM) + EPS) * jnp.concatenate([g, g], axis=1)


def _augment_pair_keys(kn, pos, sel_shift=None):
    lane = _iota(kn.shape, 1)
    return [jnp.where(lane >= HEAD_DIM if flipped else lane < HEAD_DIM, kn,
                      _key_features(pos, flipped, sel_shift)).astype(BF16) for flipped in (False, True)]


def _norm_query_t(x, g_col):
    ss = jnp.mean(x * x, axis=0, keepdims=True)
    return x * lax.rsqrt(ss + EPS) * g_col * (HEAD_DIM ** -0.5 * LOG2E)


def _augment_query(q_t, alibi, sel_bias=None, flipped=False):
    n = q_t.shape[1]
    feats = [alibi]
    used = alibi.shape[0]
    if sel_bias is not None:
        feats.append(sel_bias)
        used += sel_bias.shape[0]
    feats.append(jnp.zeros((HEAD_DIM - used, n), F32))
    return jnp.concatenate(feats + [q_t] if flipped else [q_t] + feats, axis=0).astype(BF16)


ONES_ROWS = 16
ACC_ROWS = HEAD_DIM + ONES_ROWS


def _flash_init(n):
    return jnp.full((1, n), M_INIT, F32), jnp.zeros((ACC_ROWS, n), F32)


def _flash_step(carry, sc, v_t):
    m, acc = carry
    m_new = jnp.maximum(m, jnp.max(sc, axis=0, keepdims=True))
    p = jnp.exp2(sc - m_new).astype(BF16)
    v_ones = jnp.concatenate([v_t, jnp.ones((ONES_ROWS, v_t.shape[1]), BF16)], axis=0)
    return m_new, jnp.exp2(m - m_new) * acc + _dot(v_ones, p)


def _flash_split(carry):
    m, acc = carry
    return m, acc[:HEAD_DIM], acc[HEAD_DIM:HEAD_DIM + 1]


def _flash_steps(carries, chunks, cap=None):
    return tuple(_flash_step(carry, sc if cap is None else jnp.minimum(sc, cap), v_t)
                 for carry, (sc, v_t) in zip(carries, chunks))


def _flash_reset(state):
    m_ref, acc_ref = state
    m_ref[...] = jnp.full(m_ref.shape, M_INIT, F32)
    acc_ref[...] = jnp.zeros(acc_ref.shape, F32)


def _flash_update(state, slot, sc, v_t, cap=None):
    m_ref, acc_ref = state
    m_ref[slot], acc_ref[slot] = _flash_step((m_ref[slot], acc_ref[slot]),
                                             sc if cap is None else jnp.minimum(sc, cap), v_t)


def _flash_update_triangular(state, slot, sc_ref, v_t, cap, full_half, partial_lanes):
    m_ref, acc_ref = state
    half = sc_ref.shape[0] // 2
    full = slice(full_half * half, (full_half + 1) * half)
    part = slice((1 - full_half) * half, (2 - full_half) * half)
    _flash_update(state, slot, sc_ref[full, :], v_t[:, full], cap[full, :])

    def gather(x):
        return jnp.concatenate([x[..., a:b] for a, b in partial_lanes], axis=-1)

    sc = jnp.minimum(jnp.concatenate([sc_ref[part, a:b] for a, b in partial_lanes], axis=1), gather(cap[part, :]))
    m, acc = _flash_step((gather(m_ref[slot]), gather(acc_ref[slot])), sc, v_t[:, part])
    width = partial_lanes[0][1] - partial_lanes[0][0]
    for k, (a, b) in enumerate(partial_lanes):
        m_ref[slot, :, a:b] = m[:, k * width:(k + 1) * width]
        acc_ref[slot, :, a:b] = acc[:, k * width:(k + 1) * width]


def _half_lanes(tq, reps, upper):
    off = tq // 2 if upper else 0
    return [(k * tq + off, k * tq + off + tq // 2) for k in range(reps)]


def _flash_update_pipelined(state, bufs, slots, c, next_scores, values, cap=None, cap_slots=None,
                            triangular=None):
    def run(src_ref, dst_ref):
        for slot in slots:
            fresh = None if next_scores is None else next_scores(slot)
            masked = cap is not None and (cap_slots is None or slot in cap_slots)
            if masked and triangular is not None:
                _flash_update_triangular(state, slot, src_ref.at[slot], values(slot), cap, *triangular)
            else:
                _flash_update(state, slot, src_ref[slot], values(slot), cap if masked else None)
            if fresh is not None:
                dst_ref[slot] = fresh

    pl.when((c & 1) == 0)(lambda: run(bufs[0], bufs[1]))
    pl.when((c & 1) == 1)(lambda: run(bufs[1], bufs[0]))


def _flash_result(state, slot):
    m_ref, acc_ref = state
    m, out, den = _flash_split((m_ref[slot], acc_ref[slot]))
    return out / jnp.maximum(den, TINY)


def _flash_scratch(chains, n):
    return [pltpu.VMEM((chains, 1, n), F32), pltpu.VMEM((chains, ACC_ROWS, n), F32)]


def _rank_select(score, blk, limit, count):
    ranks = []
    for r0 in range(0, score.shape[0], SUBLANES):
        tile = score[r0:r0 + SUBLANES]
        tile_blk = r0 + _iota(tile.shape, 0)
        rank = jnp.zeros(tile.shape, jnp.int32)
        for mm in range(limit):
            gm = score[mm:mm + 1, :]
            if mm < r0:
                beats = gm >= tile
            elif mm >= r0 + SUBLANES:
                beats = gm > tile
            else:
                beats = (gm > tile) | ((gm == tile) & (mm < tile_blk))
            rank = rank + jnp.where(beats, 1, 0)
        ranks.append(rank)
    return jnp.concatenate(ranks, axis=0) < count


def _tile_lanes(x, reps):
    return jnp.concatenate([x] * reps, axis=1) if reps > 1 else x


def _causal_cap(tk, tq, reps):
    keep = _iota((tk, tq), 0) <= _iota((tk, tq), 1)
    return _tile_lanes(jnp.where(keep, BIG, -BIG), reps)


def _window_edge_cap(tk, tq, reps):
    keep = _iota((tk, tq), 0) > _iota((tk, tq), 1)
    return _tile_lanes(jnp.where(keep, BIG, -BIG), reps)


def _store_chunked_t(x, dst_ref, tk):
    x_t = x.T.astype(BF16)
    for c in range(x.shape[0] // tk):
        dst_ref[c] = x_t[:, c * tk:(c + 1) * tk]


def _moba_kernel(alibi_ref, qt_ref, ka_ref, vt_ref, km_ref, z_ref, o_ref, m_s, acc_s, sc0_s, sc1_s):
    i = pl.program_id(1)
    nblk = vt_ref.shape[1]
    tq = MOBA_BLOCK
    hd = HEAD_DIM
    chains = [(r, j) for r in range(qt_ref.shape[0]) for j in range(N_HEADS)]
    slots = range(len(chains))

    blk = _iota((nblk, tq), 0)
    qas = []
    for r, j in chains:
        qn = qt_ref[r, j * hd:(j + 1) * hd, :]
        km = jnp.concatenate([km_ref[r, t, :, j * hd:(j + 1) * hd] for t in range(km_ref.shape[1])], axis=0)
        gate = jnp.where(blk < i, _dot_fine(km, qn), NEG)
        chosen = _rank_select(gate, blk, nblk, MOBA_TOPK) & (blk < i)
        sel_bias = jnp.where(chosen | (blk == i), 0.0, -BIG)
        qas.append(_augment_query(qn, alibi_ref[j], sel_bias, flipped=j % 2 == 1))

    def scores(slot, n):
        r, j = chains[slot]
        return _dot(ka_ref[r, j, pl.ds(pl.multiple_of(n * tq, tq), tq), :], qas[slot])

    def values(slot, n):
        r, j = chains[slot]
        return vt_ref[r, n, j * hd:(j + 1) * hd, :]

    state = (m_s, acc_s)
    _flash_reset(state)
    bufs = (sc0_s, sc1_s)
    for slot in slots:
        sc0_s[slot] = scores(slot, 0)

    def body(n, carry):
        _flash_update_pipelined(state, bufs, slots, n, lambda slot: scores(slot, n + 1),
                                lambda slot: values(slot, n))
        return carry

    lax.fori_loop(0, i, body, 0)
    diag_cap = _causal_cap(tq, tq, 1)
    _flash_update_pipelined(state, bufs, slots, i, None, lambda slot: values(slot, i), diag_cap,
                            triangular=(0, _half_lanes(tq, 1, upper=True)))
    for r in range(qt_ref.shape[0]):
        outs = [_flash_result(state, r * N_HEADS + j) for j in range(N_HEADS)]
        y = jnp.concatenate(outs, axis=0).T * _silu(z_ref[r])
        o_ref[r] = y.astype(o_ref.dtype)


def _moba(z, q_t, ka, vt, km, *, rows_per_step=2):
    bsz, s, _ = z.shape
    tq = MOBA_BLOCK
    gw = GROUP_WIDTH
    nblk = s // MOBA_BLOCK
    rps = rows_per_step
    assert SEL_FEAT0 + nblk <= HEAD_DIM and bsz % rps == 0
    whole = lambda a: pl.BlockSpec((rps,) + a.shape[1:], lambda b, i: (b,) + (0,) * (a.ndim - 1))
    return pl.pallas_call(
        _moba_kernel,
        grid=(bsz // rps, s // tq),
        in_specs=[pl.BlockSpec((N_HEADS, 2 * SLOPE_PIECES, tq), lambda b, i: (0, 0, 0)),
                  pl.BlockSpec((rps, gw, tq), lambda b, i: (b, 0, i)), whole(ka), whole(vt), whole(km),
                  pl.BlockSpec((rps, tq, gw), lambda b, i: (b, i, 0))],
        out_specs=pl.BlockSpec((rps, tq, gw), lambda b, i: (b, i, 0)),
        out_shape=jax.ShapeDtypeStruct((bsz, s, gw), BF16),
        scratch_shapes=_flash_scratch(rps * N_HEADS, tq) + [pltpu.VMEM((rps * N_HEADS, tq, tq), F32)] * 2,
        name="moba",
    )(jnp.asarray(_alibi_query_features(tq)), q_t, ka, vt, km, z)


def _cmp_kernel(xk_ref, xv_ref, wk1_ref, wv1_ref, pk_ref, pv_ref, bk1_ref, bv1_ref,
                wk2_ref, wv2_ref, bk2_ref, bv2_ref, gn_ref, ko_ref, vo_ref, wke_s, wve_s):
    half = xk_ref.shape[2]
    n = xk_ref.shape[1]
    hd = HEAD_DIM
    nh = NSA_CMP_HIDDEN

    @pl.when(pl.program_id(0) == 0)
    def _():
        for w_ref, we_ref in ((wk1_ref, wke_s), (wv1_ref, wve_s)):
            we_ref[...] = jnp.zeros(we_ref.shape, BF16)
            for tok in range(NSA_CMP_LEN):
                for g in range(KV_HEADS):
                    row = (tok * KV_HEADS + g) * hd
                    we_ref[row:row + hd, g * nh:(g + 1) * nh] = w_ref[tok * hd:(tok + 1) * hd, :]

    rows = xk_ref.shape[0]

    def hidden(x_ref, we_ref, pos_ref, b1_ref):
        x = x_ref[...].reshape(rows * n, half).astype(BF16)
        first = _dot(x, we_ref[:half, :])
        second = _dot(x, we_ref[half:, :])
        second = jnp.concatenate([pltpu.roll(second[r * n:(r + 1) * n], n - 1, 0) for r in range(rows)], axis=0)
        pos = jnp.broadcast_to(pos_ref[...], (8, 2 * half)).astype(BF16)
        ph = _dot(pos[:, :half], we_ref[:half, :]) + _dot(pos[:, half:], we_ref[half:, :])
        return _silu(first + second + ph[0:1, :] + b1_ref[...])

    hk = hidden(xk_ref, wke_s, pk_ref, bk1_ref).astype(BF16)
    hv = hidden(xv_ref, wve_s, pv_ref, bv1_ref).astype(BF16)
    ks, vs = [], []
    for g in range(KV_HEADS):
        ks.append(_rms(_dot(hk[:, g * nh:(g + 1) * nh], wk2_ref[...]) + bk2_ref[...], gn_ref[...]))
        vs.append(_dot(hv[:, g * nh:(g + 1) * nh], wv2_ref[...]) + bv2_ref[...])
    ko_ref[...] = jnp.concatenate(ks, axis=1).reshape(ko_ref.shape)
    vo_ref[...] = jnp.concatenate(vs, axis=1).reshape(vo_ref.shape)


def _compress(xk, xv, pos_k, pos_v, kw1, kb1, kw2, kb2, vw1, vb1, vw2, vb2, gn, *, rows_per_step=4):
    bsz, n, half = xk.shape
    hd = HEAD_DIM
    rps = rows_per_step
    assert bsz % rps == 0
    tile_pos = lambda p: jnp.tile(p[:, None, :], (1, KV_HEADS, 1)).reshape(1, 2 * half)
    tile_b = lambda b: jnp.tile(b.reshape(1, -1), (1, KV_HEADS))
    const = lambda shape: pl.BlockSpec(shape, lambda b: (0,) * len(shape))
    xs = pl.BlockSpec((rps, n, half), lambda b: (b, 0, 0))
    hw = KV_HEADS * NSA_CMP_HIDDEN
    out = pl.BlockSpec((rps, n, KV_HEADS * hd), lambda b: (b, 0, 0))
    expanded = pltpu.VMEM((2 * half, hw), BF16)
    return pl.pallas_call(
        _cmp_kernel,
        grid=(bsz // rps,),
        in_specs=[xs, xs, const(kw1.shape), const(vw1.shape),
                  const((1, 2 * half)), const((1, 2 * half)), const((1, hw)), const((1, hw)),
                  const((NSA_CMP_HIDDEN, hd)), const((NSA_CMP_HIDDEN, hd)), const((1, hd)), const((1, hd)),
                  const((1, hd))],
        out_specs=[out, out],
        out_shape=[jax.ShapeDtypeStruct((bsz, n, KV_HEADS * hd), F32)] * 2,
        scratch_shapes=[expanded, expanded],
        name="nsa_compress",
    )(xk, xv, kw1.astype(BF16), vw1.astype(BF16), tile_pos(pos_k), tile_pos(pos_v), tile_b(kb1), tile_b(vb1),
      kw2.astype(BF16), vw2.astype(BF16), kb2.reshape(1, hd), vb2.reshape(1, hd), gn.reshape(1, hd))


def _group_query(q_t, alibi_ref, g, gq_col):
    hd = HEAD_DIM
    heads = range(Q_PER_KV * g, Q_PER_KV * (g + 1))
    qn = jnp.concatenate([_norm_query_t(q_t[h * hd:(h + 1) * hd], gq_col) for h in heads], axis=1)
    alibi = jnp.concatenate([alibi_ref[h] for h in heads], axis=1)
    return qn, alibi


def _augment_keys(k, gk_ref, dst_ref, row=0, sel_shift=None):
    assert KV_HEADS == 2
    pos = _iota(k.shape, 0)
    dst_ref[2 * row], dst_ref[2 * row + 1] = _augment_pair_keys(_pair_rms(k, gk_ref[...]), pos, sel_shift)


def _chunk_scores(ka_ref, vt_ref, g, c, qa, tq):
    off = pl.multiple_of(c * tq, tq)
    return _dot(ka_ref[g, pl.ds(off, tq), :], qa), vt_ref[c, g * HEAD_DIM:(g + 1) * HEAD_DIM, :]


def _window_first_chunk(ka_ref, vt_ref, g, far, qa, tq, edge_cap):
    sc, v_t = _chunk_scores(ka_ref, vt_ref, g, jnp.maximum(far, 0), qa, tq)
    return jnp.minimum(sc, jnp.minimum(edge_cap, jnp.where(far >= 0, BIG, -BIG))), v_t


def _regroup_heads(o_t, tq):
    return jnp.concatenate([o_t[:, k * tq:(k + 1) * tq] for k in range(Q_PER_KV)], axis=0)


def _nsa_kernel(alibi_ref, q_ref, kc_ref, vc_ref, ks_ref, vs_ref, kw_ref, vw_ref, gate_ref, z_ref,
                gq_ref, gks_ref, gkw_ref, o_ref, kca_s, vct_s, ksa_s, vst_s, kwa_s, vwt_s, m_s, acc_s,
                sc0_s, sc1_s):
    i = pl.program_id(1)
    rows = range(q_ref.shape[0])
    tq = q_ref.shape[1]
    s = ks_ref.shape[1]
    hd = HEAD_DIM
    ncmp = kc_ref.shape[1]
    nsb = s // NSA_SEL_BLOCK

    @pl.when(i == 0)
    def _():
        cend = _iota((ncmp, LANES), 0) * NSA_CMP_STRIDE + (NSA_CMP_LEN - 1)
        for r in rows:
            _augment_keys(ks_ref[r], gks_ref, ksa_s, r, NSA_SEL_BLOCK.bit_length() - 1)
            _augment_keys(kw_ref[r], gkw_ref, kwa_s, r)
            _store_chunked_t(vs_ref[r], vst_s.at[r], tq)
            _store_chunked_t(vw_ref[r], vwt_s.at[r], tq)
            kca_s[2 * r], kca_s[2 * r + 1] = _augment_pair_keys(kc_ref[r], cend)
            vct_s[r] = vc_ref[r].T.astype(BF16)

    t0 = i * tq
    diag_cap = _causal_cap(tq, tq, Q_PER_KV)
    edge_cap = _window_edge_cap(tq, tq, Q_PER_KV)
    cend =_iota((ncmp, tq), 0) * NSA_CMP_STRIDE + (NSA_CMP_LEN - 1)
    seen = _tile_lanes(cend <= t0 + _iota((ncmp, tq), 1), Q_PER_KV)
    cstart = _iota((nsb, ncmp), 1) * NSA_CMP_STRIDE
    bstart = _iota((nsb, ncmp), 0) * NSA_SEL_BLOCK
    overlap_t = jnp.where((cstart < bstart + NSA_SEL_BLOCK) & (cstart + NSA_CMP_LEN > bstart), 1.0, 0.0)
    blk = _iota((nsb, tq), 0)
    cur = (t0 + _iota((nsb, tq), 1)) >> (NSA_SEL_BLOCK.bit_length() - 1)
    forced = (blk == 0) | (blk == cur) | (blk == cur - 1)
    groups = range(KV_HEADS)
    far = i - NSA_WINDOW // tq
    lo = jnp.maximum(far + 1, 0)
    first_win = jnp.maximum(far, 0)

    state = (m_s, acc_s)
    kv = [(r, g) for r in rows for g in groups]
    sel_slots = tuple(range(len(kv)))
    win_slots = tuple(len(kv) + n for n in range(len(kv)))
    _flash_reset(state)
    q_ts = [q_ref[r].T for r in rows]
    queries = [_group_query(q_ts[r], alibi_ref, g, gq_ref[...]) for r, g in kv]
    qa_win = [_augment_query(qn, alibi, flipped=g == 1) for (r, g), (qn, alibi) in zip(kv, queries)]
    qa_sel = []

    def scores(slot, c):
        n = slot % len(kv)
        ka_ref, qa = (ksa_s, qa_sel) if slot in sel_slots else (kwa_s, qa_win)
        return _dot(ka_ref[n, pl.ds(pl.multiple_of(c * tq, tq), tq), :], qa[n])

    def values(slot, c):
        r, g = kv[slot % len(kv)]
        return (vst_s if slot in sel_slots else vwt_s)[r, c, g * hd:(g + 1) * hd, :]

    cmp_sc = [_dot(kca_s[n], qa_win[n]) for n in range(len(kv))]
    sc_win_first = [scores(slot, first_win) for slot in win_slots]
    o_cmp = []
    for n, (r, g) in enumerate(kv):
        qn, alibi = queries[n]
        sc = jnp.where(seen, cmp_sc[n], -BIG)
        e = jnp.where(seen, jnp.exp2(sc - jnp.max(sc, axis=0, keepdims=True)), 0.0)
        p_c = e / jnp.maximum(jnp.sum(e, axis=0, keepdims=True), TINY)
        o_cmp.append(_dot(vct_s[r, g * hd:(g + 1) * hd, :], p_c.astype(BF16)))

        psum = p_c[:, 0:tq]
        for k in range(1, Q_PER_KV):
            psum = psum + p_c[:, k * tq:(k + 1) * tq]
        imp = _dot_fine(overlap_t, psum, a_is_bf16_exact=True)
        imp = jnp.where(blk <= cur, jnp.where(forced, NSA_FORCE, imp), NEG)
        chosen = _rank_select(imp, blk, nsb, min(NSA_TOPN, nsb))
        sel_bias = _tile_lanes(jnp.where(chosen, 0.0, -BIG), Q_PER_KV)
        qa_sel.append(_augment_query(qn, alibi, sel_bias, flipped=g == 1))

    bufs = (sc0_s, sc1_s)
    for slot in sel_slots:
        sc0_s[slot] = scores(slot, 0)
    for slot, sc in zip(win_slots, sc_win_first):
        sc0_s[slot] = sc
        sc1_s[slot] = sc

    def body(slots, cap, cap_slots, triangular, c, carry):
        _flash_update_pipelined(state, bufs, slots, c, lambda slot: scores(slot, c + 1),
                                lambda slot: values(slot, c), cap, cap_slots, triangular)
        return carry

    both_slots = sel_slots + win_slots
    edge_shape = (1, _half_lanes(tq, Q_PER_KV, upper=False))
    diag_shape = (0, _half_lanes(tq, Q_PER_KV, upper=True))
    lax.fori_loop(0, first_win, functools.partial(body, sel_slots, None, None, None), 0)
    lax.fori_loop(first_win, lo, functools.partial(body, both_slots, edge_cap, win_slots, edge_shape), 0)
    lax.fori_loop(lo, i, functools.partial(body, both_slots, None, None, None), 0)
    _flash_update_pipelined(state, bufs, both_slots, i, None, lambda slot: values(slot, i), diag_cap,
                            triangular=diag_shape)

    for r in rows:
        gates_t = _sigmoid(gate_ref[r]).T
        outs = []
        for g in groups:
            def gate_row(branch):
                parts = [gates_t[branch * N_HEADS + h:branch * N_HEADS + h + 1, :]
                         for h in range(Q_PER_KV * g, Q_PER_KV * (g + 1))]
                return jnp.concatenate(parts, axis=1)

            n = KV_HEADS * r + g
            o_slc = _flash_result(state, sel_slots[n])
            o_win = _flash_result(state, win_slots[n])
            o = gate_row(0) * o_cmp[n] + gate_row(1) * o_slc + gate_row(2) * o_win
            outs.append(_regroup_heads(o, tq))
        y = jnp.concatenate(outs, axis=0).T * _silu(z_ref[r])
        o_ref[r] = y.astype(o_ref.dtype)


def _nsa(proj, kcn, vc, gq, gks, gkw, *, tq=256, rows_per_step=2):
    bsz, s, _ = proj.shape
    hd = HEAD_DIM
    gw = GROUP_WIDTH
    ncmp = kcn.shape[1]
    rps = rows_per_step
    chains = 2 * KV_HEADS * rps
    assert SEL_FEAT0 + s // NSA_SEL_BLOCK <= hd and ncmp == LANES and bsz % rps == 0
    wide = lambda c0: pl.BlockSpec((rps, tq, gw), lambda b, i: (b, i, c0 // 4))
    full = lambda c0: pl.BlockSpec((rps, s, LANES), lambda b, i: (b, 0, c0))
    cmp = pl.BlockSpec((rps, ncmp, LANES), lambda b, i: (b, 0, 0))
    row = pl.BlockSpec((1, hd), lambda b, i: (0, 0))
    ka = pltpu.VMEM((KV_HEADS * rps, s, LANES), BF16)
    vt = pltpu.VMEM((rps, s // tq, LANES, tq), BF16)
    return pl.pallas_call(
        _nsa_kernel,
        grid=(bsz // rps, s // tq),
        in_specs=[pl.BlockSpec((N_HEADS, 2 * SLOPE_PIECES, tq), lambda b, i: (0, 0, 0)),
                  wide(ODD_QC), cmp, cmp, full(ODD_KS), full(ODD_VS), full(ODD_KW), full(ODD_VW),
                  pl.BlockSpec((rps, tq, LANES), lambda b, i: (b, i, ODD_GC)), wide(ODD_ZC),
                  pl.BlockSpec((hd, 1), lambda b, i: (0, 0)), row, row],
        out_specs=pl.BlockSpec((rps, tq, gw), lambda b, i: (b, i, 0)),
        out_shape=jax.ShapeDtypeStruct((bsz, s, gw), BF16),
        scratch_shapes=[pltpu.VMEM((KV_HEADS * rps, ncmp, LANES), BF16), pltpu.VMEM((rps, LANES, ncmp), BF16),
                        ka, vt, ka, vt] + _flash_scratch(chains, Q_PER_KV * tq)
                       + [pltpu.VMEM((chains, tq, Q_PER_KV * tq), F32)] * 2,
        name="nsa",
    )(jnp.asarray(_alibi_query_features(tq)), proj, kcn, vc, proj, proj, proj, proj, proj, proj,
      gq.reshape(hd, 1), gks.reshape(1, hd), gkw.reshape(1, hd))


def _swa_kernel(sinks_ref, alibi_ref, q_ref, k_ref, v_ref, z_ref, gq_ref, gk_ref, o_ref, ka_s, vt_s):
    step = pl.program_id(1)
    tq = SWA_WINDOW
    tiles = q_ref.shape[1] // tq
    s = k_ref.shape[1]

    @pl.when(step == 0)
    def _():
        _augment_keys(k_ref[0], gk_ref, ka_s)
        _store_chunked_t(v_ref[0], vt_s, tq)

    q_t = q_ref[0].T
    diag_cap = _causal_cap(tq, tq, Q_PER_KV)
    edge_cap = _window_edge_cap(tq, tq, Q_PER_KV)
    chains = [(u, g) for u in range(tiles) for g in range(KV_HEADS)]
    qas = [_augment_query(*_group_query(q_t[:, u * tq:(u + 1) * tq], alibi_ref, g, gq_ref[...]), flipped=g == 1)
           for u, g in chains]
    first = [_window_first_chunk(ka_s, vt_s, g, step * tiles + u - 1, qa, tq, edge_cap)
             for (u, g), qa in zip(chains, qas)]
    own = [_chunk_scores(ka_s, vt_s, g, step * tiles + u, qa, tq) for (u, g), qa in zip(chains, qas)]
    carries = _flash_steps([_flash_init(Q_PER_KV * tq) for _ in chains], first)
    carries = _flash_steps(carries, own, diag_cap)
    outs = []
    for (u, g), carry in zip(chains, carries):
        m, acc, l = _flash_split(carry)
        tpos = ((step * tiles + u) * tq + _iota((1, tq), 1)).astype(F32)
        sink = jnp.concatenate([sinks_ref[h] * LOG2E + (SLOPES[h] * LOG2E) * tpos
                                for h in range(Q_PER_KV * g, Q_PER_KV * (g + 1))], axis=1)
        mf = jnp.maximum(m, sink)
        alpha = jnp.exp2(m - mf)
        o = acc * alpha / jnp.maximum(l * alpha + jnp.exp2(sink - mf), TINY)
        outs.append(_regroup_heads(o, tq))
    o_t = jnp.concatenate([jnp.concatenate(outs[u * KV_HEADS:(u + 1) * KV_HEADS], axis=0) for u in range(tiles)],
                          axis=1)
    y = o_t.T * _silu(z_ref[0])
    o_ref[0] = y.astype(o_ref.dtype)


def _swa(proj, sinks, gq, gk, *, tiles_per_step=8):
    bsz, s, _ = proj.shape
    hd = HEAD_DIM
    gw = GROUP_WIDTH
    tq = SWA_WINDOW
    rows = tiles_per_step * tq
    wide = lambda c0: pl.BlockSpec((1, rows, gw), lambda b, i: (b, i, c0 // 4))
    full = lambda c0: pl.BlockSpec((1, s, LANES), lambda b, i: (b, 0, c0))
    return pl.pallas_call(
        _swa_kernel,
        grid=(bsz, s // rows),
        in_specs=[pl.BlockSpec(memory_space=pltpu.SMEM),
                  pl.BlockSpec((N_HEADS, 2 * SLOPE_PIECES, tq), lambda b, i: (0, 0, 0)),
                  wide(ODD_QD), full(ODD_KD), full(ODD_VD), wide(ODD_ZD),
                  pl.BlockSpec((hd, 1), lambda b, i: (0, 0)), pl.BlockSpec((1, hd), lambda b, i: (0, 0))],
        out_specs=pl.BlockSpec((1, rows, gw), lambda b, i: (b, i, 0)),
        out_shape=jax.ShapeDtypeStruct((bsz, s, gw), BF16),
        scratch_shapes=[pltpu.VMEM((KV_HEADS, s, LANES), BF16), pltpu.VMEM((s // tq, LANES, tq), BF16)],
        name="swa",
    )(sinks.astype(F32), jnp.asarray(_alibi_query_features(tq)), proj, proj, proj, proj,
      gq.reshape(hd, 1), gk.reshape(1, hd))


def _pack_odd_weight(w):
    gw, kw, ng = GROUP_WIDTH, KV_HEADS * HEAD_DIM, 3 * N_HEADS
    sizes = [gw, kw, kw, kw, kw, kw, kw, ng, gw, gw, kw, kw, gw]
    starts = np.concatenate([[0], np.cumsum(sizes)]).tolist()
    part = lambda k: w[:, starts[k]:starts[k + 1]]
    pad = jnp.zeros((w.shape[0], LANES - ng), w.dtype)
    order = [0, 8, 9, 12, 1, 2, 3, 4, 5, 6, 10, 11, 7]
    return jnp.concatenate([part(k) for k in order] + [pad], axis=1).astype(BF16)


def kernel(x, norm_g, w_out, e_w_in, a_conv_w, a_conv_b, a_ln_g, a_ln_b, b_qnorm_g, b_knorm_g, o_w_in, c_qnorm_g, c_knorm_cmp_g, c_knorm_slc_g, c_knorm_win_g, c_pos_k, c_pos_v, c_k_w1, c_k_b1, c_k_w2, c_k_b2, c_v_w1, c_v_b1, c_v_w2, c_v_b2, d_qnorm_g, d_knorm_g, d_sinks):
    bsz, s, d = x.shape
    m = bsz * s
    assert s % MOBA_BLOCK == 0 and d == 2 * GROUP_WIDTH
    x2 = x.reshape(m, d)

    y_a, z_b, q_t, ka, vt, km = _even_layer_front(
        x2, norm_g[0], e_w_in[0].astype(BF16), b_qnorm_g[0], b_knorm_g[0],
        a_conv_w[0], a_conv_b[0], a_ln_g[0], a_ln_b[0], bsz=bsz)
    y_b = _moba(z_b.reshape(bsz, s, -1), q_t, ka, vt, km)

    x2, proj, k_cmp, v_cmp = _layer_boundary(
        x2, y_a, y_b.reshape(m, -1), w_out[0].astype(BF16), norm_g[1], _pack_odd_weight(o_w_in[0]),
        chunk=5 * LANES, copies=((ODD_KC * LANES, LANES), (ODD_VC * LANES, LANES)), group=NSA_CMP_STRIDE)
    proj = proj.reshape(bsz, s, ODD_COLS)
    rows16 = lambda t: t.reshape(bsz, s // NSA_CMP_STRIDE, NSA_CMP_STRIDE * LANES)
    kcn, vc = _compress(rows16(k_cmp), rows16(v_cmp), c_pos_k[0], c_pos_v[0], c_k_w1[0], c_k_b1[0],
                        c_k_w2[0], c_k_b2[0], c_v_w1[0], c_v_b1[0], c_v_w2[0], c_v_b2[0], c_knorm_cmp_g[0])
    y_c = _nsa(proj, kcn, vc, c_qnorm_g[0], c_knorm_slc_g[0], c_knorm_win_g[0])
    y_d = _swa(proj, d_sinks[0], d_qnorm_g[0], d_knorm_g[0])
    x2 = _outproj(x2, y_c.reshape(m, -1), y_d.reshape(m, -1), w_out[1].astype(BF16))
    return x2.reshape(bsz, s, d)
```

```python
import functools

import ml_dtypes
import numpy as np
import jax
import jax.numpy as jnp
from jax import lax
from jax.experimental import pallas as pl
from jax.experimental.pallas import tpu as pltpu

HEAD_DIM = 64
N_HEADS = 8
GROUP_WIDTH = N_HEADS * HEAD_DIM
CONV_WIDTH = 31
MOBA_BLOCK = 256
MOBA_TOPK = 3
KV_HEADS = 2
Q_PER_KV = N_HEADS // KV_HEADS
NSA_CMP_LEN = 32
NSA_CMP_STRIDE = 16
NSA_CMP_HIDDEN = 256
NSA_SEL_BLOCK = 64
NSA_TOPN = 16
NSA_WINDOW = 512
NSA_FORCE = 1e4
SWA_WINDOW = 128
EPS = 1e-6
NEG = -1e30
TINY = 1e-30
LANES = 128
SUBLANES = 8
CONV_HALO = 32

LOG2E = float(np.log2(np.e))
BIG = 2.0 ** 99
M_INIT = -1e38
POS_SPLIT_SHIFT = 8
SLOPE_PIECES = 4
SEL_FEAT0 = 2 * SLOPE_PIECES

F32 = jnp.float32
BF16 = jnp.bfloat16

ODD_QC, ODD_ZC, ODD_QD, ODD_ZD = 0, 4, 8, 12
ODD_KC, ODD_VC, ODD_KS, ODD_VS, ODD_KW, ODD_VW, ODD_KD, ODD_VD, ODD_GC = 16, 17, 18, 19, 20, 21, 22, 23, 24
ODD_COLS = 25 * LANES


def _alibi_slopes(n):
    return [float(2.0 ** (-8.0 * (i + 1) / n)) for i in range(n)]


SLOPES = _alibi_slopes(N_HEADS)


def _alibi_query_features(width):
    table = np.zeros((N_HEADS, 2 * SLOPE_PIECES, width), np.float32)
    for h, slope in enumerate(SLOPES):
        rest = np.float64(slope) * LOG2E
        for k in range(SLOPE_PIECES):
            piece = float(np.float32(rest).astype(ml_dtypes.bfloat16).astype(np.float32))
            table[h, k] = piece
            table[h, SLOPE_PIECES + k] = piece
            rest -= piece
    return table


def _dot(a, b):
    return jnp.dot(a, b, preferred_element_type=F32)


def _split_bf16(x, pieces):
    out = []
    for _ in range(pieces):
        out.append(x.astype(BF16))
        x = x - out[-1].astype(F32)
    return out


def _dot_fine(a, b, a_is_bf16_exact=False):
    if a_is_bf16_exact:
        return sum(_dot(a.astype(BF16), piece) for piece in _split_bf16(b, 3))
    (a_hi, a_lo), (b_hi, b_lo) = _split_bf16(a, 2), _split_bf16(b, 2)
    return _dot(a_hi, b_hi) + (_dot(a_hi, b_lo) + _dot(a_lo, b_hi))


def _sigmoid(x):
    return 1.0 / (1.0 + jnp.exp(-x))


def _silu(x):
    return x * _sigmoid(x)


def _rms(x, g):
    return x * lax.rsqrt(jnp.mean(x * x, axis=-1, keepdims=True) + EPS) * g


def _iota(shape, dim):
    return lax.broadcasted_iota(jnp.int32, shape, dim)


def _layer_boundary_kernel(x_ref, ya_ref, yb_ref, wa_ref, wb_ref, g_ref, w_ref, x_out_ref, o_ref, *rest,
                           chunk, copies, group, moves):
    copy_refs, stage_refs, wp_s = rest[:len(copies)], rest[len(copies):-1], rest[-1]

    @pl.when(pl.program_id(0) == 0)
    def _():
        wp_s[...] = jnp.zeros(wp_s.shape, wp_s.dtype)
        for src, width, dst in moves:
            wp_s[:, dst:dst + width] = w_ref[:, src:src + width]

    x = x_ref[...] + _dot(ya_ref[...], wa_ref[...]) + _dot(yb_ref[...], wb_ref[...])
    x_out_ref[...] = x
    h = _rms(x, g_ref[...]).astype(BF16)
    for c in range(o_ref.shape[1] // chunk):
        o_ref[:, c * chunk:(c + 1) * chunk] = _dot(h, wp_s[:, c * chunk:(c + 1) * chunk])
    for ref, stage, (start, width) in zip(copy_refs, stage_refs, copies):
        stage[...] = o_ref[:, start:start + width]
        for t in range(group):
            ref[:, t * width:(t + 1) * width] = stage[pl.ds(t, ref.shape[0], stride=group), :]


def _layer_boundary(x2d, ya, yb, w_out, g, w, *, e, moves, tm=512, chunk, copies=(), group=1):
    m, d = x2d.shape
    gw = ya.shape[1]
    rows = lambda n, width: pl.BlockSpec((n, width), lambda i: (i, 0))
    const = lambda shape: pl.BlockSpec(shape, lambda i: (0, 0))
    return pl.pallas_call(
        functools.partial(_layer_boundary_kernel, chunk=chunk, copies=tuple(copies), group=group,
                          moves=tuple(moves)),
        grid=(m // tm,),
        in_specs=[rows(tm, d), rows(tm, gw), rows(tm, gw), const((gw, d)), const((gw, d)), const((1, d)),
                  const(w.shape)],
        out_specs=[rows(tm, d), rows(tm, e)] + [rows(tm // group, group * width) for _, width in copies],
        out_shape=[jax.ShapeDtypeStruct((m, d), F32), jax.ShapeDtypeStruct((m, e), F32)]
                  + [jax.ShapeDtypeStruct((m // group, group * width), F32) for _, width in copies],
        scratch_shapes=[pltpu.VMEM((tm, width), F32) for _, width in copies] + [pltpu.VMEM((d, e), BF16)],
        name="layer_boundary",
    )(x2d, ya, yb, w_out[:gw], w_out[gw:], g.reshape(1, d), w)


def _conv_rows(h_s, row0, rows, w_ref, b_ref, lg_ref, lb_ref):
    base = CONV_HALO - (CONV_WIDTH - 1)
    acc = None
    for b in range(SUBLANES):
        n = rows if b == 0 else rows + SUBLANES
        part = None
        for a in range((base + CONV_WIDTH - 1) // SUBLANES + 1):
            j = SUBLANES * a + b - base
            if 0 <= j < CONV_WIDTH:
                term = w_ref[j:j + 1, :] * h_s[pl.ds(row0 + SUBLANES * a, n), :]
                part = term if part is None else part + term
        part = part[b:b + rows]
        acc = part if acc is None else acc + part
    y = acc + b_ref[...]
    mu = jnp.mean(y, axis=-1, keepdims=True)
    yc = y - mu
    return _silu(yc * lax.rsqrt(jnp.mean(yc * yc, axis=-1, keepdims=True) + EPS) * lg_ref[...] + lb_ref[...])


def _even_layer_kernel(x_ref, g_ref, w_ref, gq_ref, gk_ref, cw_ref, cb_ref, lg_ref, lb_ref,
                       ya_ref, zb_ref, qt_ref, ka_ref, vt_ref, km_ref, hb_s, conv_s, sec_s, h_s, tail_s,
                       *, seq, rows):
    tm = x_ref.shape[0]
    hd, gw = HEAD_DIM, GROUP_WIDTH
    tile_in_seq = pl.program_id(0) % (seq // tm)
    hb_s[...] = _rms(x_ref[...], g_ref[...]).astype(BF16)
    for c in range(3):
        conv_s[:, c * gw:(c + 1) * gw] = _dot(hb_s[...], w_ref[:, c * gw:(c + 1) * gw])
    h_s[0:CONV_HALO, :] = jnp.where(tile_in_seq > 0, tail_s[...], 0.0)
    h_s[CONV_HALO:, :] = conv_s[:, 0:gw] * _sigmoid(conv_s[:, gw:2 * gw])
    tail_s[...] = h_s[tm:tm + CONV_HALO, :]

    def project(sec, dst_ref):
        dst_ref[...] = _dot(hb_s[...], w_ref[:, sec * gw:(sec + 1) * gw])

    def moba_queries():
        project(3, sec_s)
        q_t = sec_s[...].T
        for j in range(N_HEADS):
            qt_ref[0, j * hd:(j + 1) * hd, :] = _norm_query_t(q_t[j * hd:(j + 1) * hd], gq_ref[...])

    def moba_keys():
        project(4, sec_s)
        pos = tile_in_seq * tm + _iota((tm, LANES), 0)
        for pair in range(N_HEADS // 2):
            kn = _pair_rms(sec_s[:, pair * LANES:(pair + 1) * LANES], gk_ref[...])
            ka_ref[0, 2 * pair], ka_ref[0, 2 * pair + 1] = _augment_pair_keys(kn, pos,
                                                                              MOBA_BLOCK.bit_length() - 1)
            km_ref[0, 0, :, pair * LANES:(pair + 1) * LANES] = jnp.mean(
                kn.reshape(tm // MOBA_BLOCK, MOBA_BLOCK, LANES), axis=1)

    def moba_values():
        project(5, sec_s)
        v_t = sec_s[...].T.astype(BF16)
        for c in range(tm // MOBA_BLOCK):
            vt_ref[0, c] = v_t[:, c * MOBA_BLOCK:(c + 1) * MOBA_BLOCK]

    def moba_gate_path():
        project(6, zb_ref)

    for c in range(tm // rows):
        y = _conv_rows(h_s, c * rows, rows, cw_ref, cb_ref, lg_ref, lb_ref)
        gate = _silu(conv_s[c * rows:(c + 1) * rows, 2 * gw:3 * gw])
        ya_ref[c * rows:(c + 1) * rows, :] = (y * gate).astype(ya_ref.dtype)
    moba_queries()
    moba_keys()
    moba_values()
    moba_gate_path()


def _even_layer_front(x2d, g, w, gq, gk, conv_w, conv_b, ln_g, ln_b, *, bsz, tm=512, rows=128):
    m, d = x2d.shape
    e = w.shape[1]
    s = m // bsz
    per_b = s // tm
    hd, gw, nb = HEAD_DIM, GROUP_WIDTH, tm // MOBA_BLOCK
    assert s % tm == 0 and tm % MOBA_BLOCK == 0 and e == 7 * gw and tm % rows == 0
    const = lambda shape: pl.BlockSpec(shape, lambda i: (0, 0))
    tile = pl.BlockSpec((tm, gw), lambda i: (i, 0))
    vec = const((1, gw))
    return pl.pallas_call(
        functools.partial(_even_layer_kernel, seq=s, rows=rows),
        grid=(m // tm,),
        in_specs=[pl.BlockSpec((tm, d), lambda i: (i, 0)), const((1, d)), const((d, e)), const((hd, 1)),
                  const((1, hd)), const((CONV_WIDTH, gw)), vec, vec, vec],
        out_specs=[tile, tile,
                   pl.BlockSpec((1, gw, tm), lambda i: (i // per_b, 0, i % per_b)),
                   pl.BlockSpec((1, N_HEADS, tm, LANES), lambda i: (i // per_b, 0, i % per_b, 0)),
                   pl.BlockSpec((1, nb, gw, MOBA_BLOCK), lambda i: (i // per_b, i % per_b, 0, 0)),
                   pl.BlockSpec((1, 1, nb, gw), lambda i: (i // per_b, i % per_b, 0, 0))],
        out_shape=[jax.ShapeDtypeStruct((m, gw), BF16),
                   jax.ShapeDtypeStruct((m, gw), F32),
                   jax.ShapeDtypeStruct((bsz, gw, s), F32),
                   jax.ShapeDtypeStruct((bsz, N_HEADS, s, LANES), BF16),
                   jax.ShapeDtypeStruct((bsz, s // MOBA_BLOCK, gw, MOBA_BLOCK), BF16),
                   jax.ShapeDtypeStruct((bsz, per_b, nb, gw), F32)],
        scratch_shapes=[pltpu.VMEM((tm, d), BF16), pltpu.VMEM((tm, 3 * gw), F32), pltpu.VMEM((tm, gw), F32),
                        pltpu.VMEM((CONV_HALO + tm, gw), F32), pltpu.VMEM((CONV_HALO, gw), F32)],
        name="even_layer_front",
    )(x2d, g.reshape(1, d), w, gq.reshape(hd, 1), gk.reshape(1, hd), conv_w, conv_b.reshape(1, gw),
      ln_g.reshape(1, gw), ln_b.reshape(1, gw))


def _outproj_kernel(x_ref, ya_ref, yb_ref, wa_ref, wb_ref, o_ref):
    o_ref[...] = x_ref[...] + _dot(ya_ref[...], wa_ref[...]) + _dot(yb_ref[...], wb_ref[...])


def _outproj(x2d, ya, yb, w, *, tm=1024):
    m, d = x2d.shape
    gw = ya.shape[1]
    return pl.pallas_call(
        _outproj_kernel,
        grid=(m // tm,),
        in_specs=[pl.BlockSpec((tm, d), lambda i: (i, 0)),
                  pl.BlockSpec((tm, gw), lambda i: (i, 0)),
                  pl.BlockSpec((tm, gw), lambda i: (i, 0)),
                  pl.BlockSpec((gw, d), lambda i: (0, 0)),
                  pl.BlockSpec((gw, d), lambda i: (0, 0))],
        out_specs=pl.BlockSpec((tm, d), lambda i: (i, 0)),
        out_shape=jax.ShapeDtypeStruct((m, d), F32),
        name="outproj",
    )(x2d, ya, yb, w[:gw], w[gw:])


def _key_features(pos, flipped, sel_shift=None):
    col = _iota(pos.shape, 1) - (0 if flipped else HEAD_DIM)
    hi = (pos >> POS_SPLIT_SHIFT) << POS_SPLIT_SHIFT
    lo = pos & ((1 << POS_SPLIT_SHIFT) - 1)
    feat = jnp.where(col < SLOPE_PIECES, hi, jnp.where(col < 2 * SLOPE_PIECES, lo, 0))
    if sel_shift is not None:
        feat = jnp.where(col - SEL_FEAT0 == (pos >> sel_shift), 1, feat)
    return jnp.where((col >= 0) & (col < HEAD_DIM), feat, 0).astype(F32)


def _pair_rms(x, g):
    sq = x * x
    hi = sq.astype(BF16)
    lo = (sq - hi.astype(F32)).astype(BF16)
    head_shift = HEAD_DIM.bit_length() - 1
    same_head = (_iota((LANES, LANES), 0) >> head_shift) == (_iota((LANES, LANES), 1) >> head_shift)
    ones = jnp.where(same_head, 1.0, 0.0).astype(BF16)
    ss = _dot(hi, ones) + _dot(lo, ones)
    return x * lax.rsqrt(ss * (1.0 / HEAD_DIM) + EPS) * jnp.concatenate([g, g], axis=1)


def _augment_pair_keys(kn, pos, sel_shift=None):
    lane = _iota(kn.shape, 1)
    return [jnp.where(lane >= HEAD_DIM if flipped else lane < HEAD_DIM, kn,
                      _key_features(pos, flipped, sel_shift)).astype(BF16) for flipped in (False, True)]


def _norm_query_t(x, g_col):
    ss = jnp.mean(x * x, axis=0, keepdims=True)
    return x * lax.rsqrt(ss + EPS) * g_col * (HEAD_DIM ** -0.5 * LOG2E)


def _augment_query(q_t, alibi, sel_bias=None, flipped=False):
    n = q_t.shape[1]
    feats = [alibi]
    used = alibi.shape[0]
    if sel_bias is not None:
        feats.append(sel_bias)
        used += sel_bias.shape[0]
    feats.append(jnp.zeros((HEAD_DIM - used, n), F32))
    return jnp.concatenate(feats + [q_t] if flipped else [q_t] + feats, axis=0).astype(BF16)


ONES_ROWS = 16
ACC_ROWS = HEAD_DIM + ONES_ROWS


def _flash_init(n):
    return jnp.full((1, n), M_INIT, F32), jnp.zeros((ACC_ROWS, n), F32)


def _flash_step(carry, sc, v_t):
    m, acc = carry
    m_new = jnp.maximum(m, jnp.max(sc, axis=0, keepdims=True))
    p = jnp.exp2(sc - m_new).astype(BF16)
    v_ones = jnp.concatenate([v_t, jnp.ones((ONES_ROWS, v_t.shape[1]), BF16)], axis=0)
    return m_new, jnp.exp2(m - m_new) * acc + _dot(v_ones, p)


def _flash_split(carry):
    m, acc = carry
    return m, acc[:HEAD_DIM], acc[HEAD_DIM:HEAD_DIM + 1]


def _flash_steps(carries, chunks, cap=None):
    return tuple(_flash_step(carry, sc if cap is None else jnp.minimum(sc, cap), v_t)
                 for carry, (sc, v_t) in zip(carries, chunks))


def _flash_reset(state):
    m_ref, acc_ref = state
    m_ref[...] = jnp.full(m_ref.shape, M_INIT, F32)
    acc_ref[...] = jnp.zeros(acc_ref.shape, F32)


def _flash_update(state, slot, sc, v_t, cap=None):
    m_ref, acc_ref = state
    m_ref[slot], acc_ref[slot] = _flash_step((m_ref[slot], acc_ref[slot]),
                                             sc if cap is None else jnp.minimum(sc, cap), v_t)


def _flash_update_triangular(state, slot, sc_ref, v_t, cap, full_half, partial_lanes):
    m_ref, acc_ref = state
    half = sc_ref.shape[0] // 2
    full = slice(full_half * half, (full_half + 1) * half)
    part = slice((1 - full_half) * half, (2 - full_half) * half)
    _flash_update(state, slot, sc_ref[full, :], v_t[:, full], cap[full, :])

    def gather(x):
        return jnp.concatenate([x[..., a:b] for a, b in partial_lanes], axis=-1)

    sc = jnp.minimum(jnp.concatenate([sc_ref[part, a:b] for a, b in partial_lanes], axis=1), gather(cap[part, :]))
    m, acc = _flash_step((gather(m_ref[slot]), gather(acc_ref[slot])), sc, v_t[:, part])
    width = partial_lanes[0][1] - partial_lanes[0][0]
    for k, (a, b) in enumerate(partial_lanes):
        m_ref[slot, :, a:b] = m[:, k * width:(k + 1) * width]
        acc_ref[slot, :, a:b] = acc[:, k * width:(k + 1) * width]


def _half_lanes(tq, reps, upper):
    off = tq // 2 if upper else 0
    return [(k * tq + off, k * tq + off + tq // 2) for k in range(reps)]


def _flash_update_pipelined(state, bufs, slots, c, next_scores, values, cap=None, cap_slots=None,
                            triangular=None):
    def run(src_ref, dst_ref):
        for slot in slots:
            fresh = None if next_scores is None else next_scores(slot)
            masked = cap is not None and (cap_slots is None or slot in cap_slots)
            if masked and triangular is not None:
                _flash_update_triangular(state, slot, src_ref.at[slot], values(slot), cap, *triangular)
            else:
                _flash_update(state, slot, src_ref[slot], values(slot), cap if masked else None)
            if fresh is not None:
                dst_ref[slot] = fresh

    pl.when((c & 1) == 0)(lambda: run(bufs[0], bufs[1]))
    pl.when((c & 1) == 1)(lambda: run(bufs[1], bufs[0]))


def _flash_result(state, slot):
    m_ref, acc_ref = state
    m, out, den = _flash_split((m_ref[slot], acc_ref[slot]))
    return out / jnp.maximum(den, TINY)


def _flash_scratch(chains, n):
    return [pltpu.VMEM((chains, 1, n), F32), pltpu.VMEM((chains, ACC_ROWS, n), F32)]


def _rank_select(score, blk, limit, count):
    ranks = []
    for r0 in range(0, score.shape[0], SUBLANES):
        tile = score[r0:r0 + SUBLANES]
        tile_blk = r0 + _iota(tile.shape, 0)
        rank = jnp.zeros(tile.shape, jnp.int32)
        for mm in range(limit):
            gm = score[mm:mm + 1, :]
            if mm < r0:
                beats = gm >= tile
            elif mm >= r0 + SUBLANES:
                beats = gm > tile
            else:
                beats = (gm > tile) | ((gm == tile) & (mm < tile_blk))
            rank = rank + jnp.where(beats, 1, 0)
        ranks.append(rank)
    return jnp.concatenate(ranks, axis=0) < count


def _tile_lanes(x, reps):
    return jnp.concatenate([x] * reps, axis=1) if reps > 1 else x


def _causal_cap(tk, tq, reps):
    keep = _iota((tk, tq), 0) <= _iota((tk, tq), 1)
    return _tile_lanes(jnp.where(keep, BIG, -BIG), reps)


def _window_edge_cap(tk, tq, reps):
    keep = _iota((tk, tq), 0) > _iota((tk, tq), 1)
    return _tile_lanes(jnp.where(keep, BIG, -BIG), reps)


def _store_chunked_t(x, dst_ref, tk):
    x_t = x.T.astype(BF16)
    for c in range(x.shape[0] // tk):
        dst_ref[c] = x_t[:, c * tk:(c + 1) * tk]


def _moba_kernel(alibi_ref, qt_ref, ka_ref, vt_ref, km_ref, z_ref, o_ref, m_s, acc_s, sc0_s, sc1_s):
    i = pl.program_id(1)
    nblk = vt_ref.shape[1]
    tq = MOBA_BLOCK
    hd = HEAD_DIM
    chains = [(r, j) for r in range(qt_ref.shape[0]) for j in range(N_HEADS)]
    slots = range(len(chains))

    blk = _iota((nblk, tq), 0)
    qas = []
    for r, j in chains:
        qn = qt_ref[r, j * hd:(j + 1) * hd, :]
        km = jnp.concatenate([km_ref[r, t, :, j * hd:(j + 1) * hd] for t in range(km_ref.shape[1])], axis=0)
        gate = jnp.where(blk < i, _dot_fine(km, qn), NEG)
        chosen = _rank_select(gate, blk, nblk, MOBA_TOPK) & (blk < i)
        sel_bias = jnp.where(chosen | (blk == i), 0.0, -BIG)
        qas.append(_augment_query(qn, alibi_ref[j], sel_bias, flipped=j % 2 == 1))

    def scores(slot, n):
        r, j = chains[slot]
        return _dot(ka_ref[r, j, pl.ds(pl.multiple_of(n * tq, tq), tq), :], qas[slot])

    def values(slot, n):
        r, j = chains[slot]
        return vt_ref[r, n, j * hd:(j + 1) * hd, :]

    state = (m_s, acc_s)
    _flash_reset(state)
    bufs = (sc0_s, sc1_s)
    for slot in slots:
        sc0_s[slot] = scores(slot, 0)

    def body(n, carry):
        _flash_update_pipelined(state, bufs, slots, n, lambda slot: scores(slot, n + 1),
                                lambda slot: values(slot, n))
        return carry

    lax.fori_loop(0, i, body, 0)
    diag_cap = _causal_cap(tq, tq, 1)
    _flash_update_pipelined(state, bufs, slots, i, None, lambda slot: values(slot, i), diag_cap,
                            triangular=(0, _half_lanes(tq, 1, upper=True)))
    for r in range(qt_ref.shape[0]):
        outs = [_flash_result(state, r * N_HEADS + j) for j in range(N_HEADS)]
        y = jnp.concatenate(outs, axis=0).T * _silu(z_ref[r])
        o_ref[r] = y.astype(o_ref.dtype)


def _moba(z, q_t, ka, vt, km, *, rows_per_step=2):
    bsz, s, _ = z.shape
    tq = MOBA_BLOCK
    gw = GROUP_WIDTH
    nblk = s // MOBA_BLOCK
    rps = rows_per_step
    assert SEL_FEAT0 + nblk <= HEAD_DIM and bsz % rps == 0
    whole = lambda a: pl.BlockSpec((rps,) + a.shape[1:], lambda b, i: (b,) + (0,) * (a.ndim - 1))
    return pl.pallas_call(
        _moba_kernel,
        grid=(bsz // rps, s // tq),
        in_specs=[pl.BlockSpec((N_HEADS, 2 * SLOPE_PIECES, tq), lambda b, i: (0, 0, 0)),
                  pl.BlockSpec((rps, gw, tq), lambda b, i: (b, 0, i)), whole(ka), whole(vt), whole(km),
                  pl.BlockSpec((rps, tq, gw), lambda b, i: (b, i, 0))],
        out_specs=pl.BlockSpec((rps, tq, gw), lambda b, i: (b, i, 0)),
        out_shape=jax.ShapeDtypeStruct((bsz, s, gw), BF16),
        scratch_shapes=_flash_scratch(rps * N_HEADS, tq) + [pltpu.VMEM((rps * N_HEADS, tq, tq), F32)] * 2,
        name="moba",
    )(jnp.asarray(_alibi_query_features(tq)), q_t, ka, vt, km, z)


def _cmp_kernel(xk_ref, xv_ref, wk1_ref, wv1_ref, pk_ref, pv_ref, bk1_ref, bv1_ref,
                wk2_ref, wv2_ref, bk2_ref, bv2_ref, gn_ref, ko_ref, vo_ref, wke_s, wve_s):
    half = xk_ref.shape[2]
    n = xk_ref.shape[1]
    hd = HEAD_DIM
    nh = NSA_CMP_HIDDEN

    @pl.when(pl.program_id(0) == 0)
    def _():
        for w_ref, we_ref in ((wk1_ref, wke_s), (wv1_ref, wve_s)):
            we_ref[...] = jnp.zeros(we_ref.shape, BF16)
            for tok in range(NSA_CMP_LEN):
                for g in range(KV_HEADS):
                    row = (tok * KV_HEADS + g) * hd
                    we_ref[row:row + hd, g * nh:(g + 1) * nh] = w_ref[tok * hd:(tok + 1) * hd, :]

    rows = xk_ref.shape[0]

    def hidden(x_ref, we_ref, pos_ref, b1_ref):
        x = x_ref[...].reshape(rows * n, half).astype(BF16)
        first = _dot(x, we_ref[:half, :])
        second = _dot(x, we_ref[half:, :])
        second = jnp.concatenate([pltpu.roll(second[r * n:(r + 1) * n], n - 1, 0) for r in range(rows)], axis=0)
        pos = jnp.broadcast_to(pos_ref[...], (8, 2 * half)).astype(BF16)
        ph = _dot(pos[:, :half], we_ref[:half, :]) + _dot(pos[:, half:], we_ref[half:, :])
        return _silu(first + second + ph[0:1, :] + b1_ref[...])

    hk = hidden(xk_ref, wke_s, pk_ref, bk1_ref).astype(BF16)
    hv = hidden(xv_ref, wve_s, pv_ref, bv1_ref).astype(BF16)
    ks, vs = [], []
    for g in range(KV_HEADS):
        ks.append(_rms(_dot(hk[:, g * nh:(g + 1) * nh], wk2_ref[...]) + bk2_ref[...], gn_ref[...]))
        vs.append(_dot(hv[:, g * nh:(g + 1) * nh], wv2_ref[...]) + bv2_ref[...])
    ko_ref[...] = jnp.concatenate(ks, axis=1).reshape(ko_ref.shape)
    vo_ref[...] = jnp.concatenate(vs, axis=1).reshape(vo_ref.shape)


def _compress(xk, xv, pos_k, pos_v, kw1, kb1, kw2, kb2, vw1, vb1, vw2, vb2, gn, *, rows_per_step=4):
    bsz, n, half = xk.shape
    hd = HEAD_DIM
    rps = rows_per_step
    assert bsz % rps == 0
    tile_pos = lambda p: jnp.tile(p[:, None, :], (1, KV_HEADS, 1)).reshape(1, 2 * half)
    tile_b = lambda b: jnp.tile(b.reshape(1, -1), (1, KV_HEADS))
    const = lambda shape: pl.BlockSpec(shape, lambda b: (0,) * len(shape))
    xs = pl.BlockSpec((rps, n, half), lambda b: (b, 0, 0))
    hw = KV_HEADS * NSA_CMP_HIDDEN
    out = pl.BlockSpec((rps, n, KV_HEADS * hd), lambda b: (b, 0, 0))
    expanded = pltpu.VMEM((2 * half, hw), BF16)
    return pl.pallas_call(
        _cmp_kernel,
        grid=(bsz // rps,),
        in_specs=[xs, xs, const(kw1.shape), const(vw1.shape),
                  const((1, 2 * half)), const((1, 2 * half)), const((1, hw)), const((1, hw)),
                  const((NSA_CMP_HIDDEN, hd)), const((NSA_CMP_HIDDEN, hd)), const((1, hd)), const((1, hd)),
                  const((1, hd))],
        out_specs=[out, out],
        out_shape=[jax.ShapeDtypeStruct((bsz, n, KV_HEADS * hd), F32)] * 2,
        scratch_shapes=[expanded, expanded],
        name="nsa_compress",
    )(xk, xv, kw1.astype(BF16), vw1.astype(BF16), tile_pos(pos_k), tile_pos(pos_v), tile_b(kb1), tile_b(vb1),
      kw2.astype(BF16), vw2.astype(BF16), kb2.reshape(1, hd), vb2.reshape(1, hd), gn.reshape(1, hd))


def _group_query(q_t, alibi_ref, g, gq_col):
    hd = HEAD_DIM
    heads = range(Q_PER_KV * g, Q_PER_KV * (g + 1))
    qn = jnp.concatenate([_norm_query_t(q_t[h * hd:(h + 1) * hd], gq_col) for h in heads], axis=1)
    alibi = jnp.concatenate([alibi_ref[h] for h in heads], axis=1)
    return qn, alibi


def _augment_keys(k, gk_ref, dst_ref, row=0, sel_shift=None):
    assert KV_HEADS == 2
    pos = _iota(k.shape, 0)
    dst_ref[2 * row], dst_ref[2 * row + 1] = _augment_pair_keys(_pair_rms(k, gk_ref[...]), pos, sel_shift)


def _chunk_scores(ka_ref, vt_ref, g, c, qa, tq):
    off = pl.multiple_of(c * tq, tq)
    return _dot(ka_ref[g, pl.ds(off, tq), :], qa), vt_ref[c, g * HEAD_DIM:(g + 1) * HEAD_DIM, :]


def _window_first_chunk(ka_ref, vt_ref, g, far, qa, tq, edge_cap):
    sc, v_t = _chunk_scores(ka_ref, vt_ref, g, jnp.maximum(far, 0), qa, tq)
    return jnp.minimum(sc, jnp.minimum(edge_cap, jnp.where(far >= 0, BIG, -BIG))), v_t


def _regroup_heads(o_t, tq):
    return jnp.concatenate([o_t[:, k * tq:(k + 1) * tq] for k in range(Q_PER_KV)], axis=0)


def _nsa_kernel(alibi_ref, q_ref, kc_ref, vc_ref, ks_ref, vs_ref, kw_ref, vw_ref, gate_ref, z_ref,
                gq_ref, gks_ref, gkw_ref, o_ref, kca_s, vct_s, ksa_s, vst_s, kwa_s, vwt_s, m_s, acc_s,
                sc0_s, sc1_s):
    i = pl.program_id(1)
    rows = range(q_ref.shape[0])
    tq = q_ref.shape[1]
    s = ks_ref.shape[1]
    hd = HEAD_DIM
    ncmp = kc_ref.shape[1]
    nsb = s // NSA_SEL_BLOCK

    @pl.when(i == 0)
    def _():
        cend = _iota((ncmp, LANES), 0) * NSA_CMP_STRIDE + (NSA_CMP_LEN - 1)
        for r in rows:
            _augment_keys(ks_ref[r], gks_ref, ksa_s, r, NSA_SEL_BLOCK.bit_length() - 1)
            _augment_keys(kw_ref[r], gkw_ref, kwa_s, r)
            _store_chunked_t(vs_ref[r], vst_s.at[r], tq)
            _store_chunked_t(vw_ref[r], vwt_s.at[r], tq)
            kca_s[2 * r], kca_s[2 * r + 1] = _augment_pair_keys(kc_ref[r], cend)
            vct_s[r] = vc_ref[r].T.astype(BF16)

    t0 = i * tq
    diag_cap = _causal_cap(tq, tq, Q_PER_KV)
    edge_cap = _window_edge_cap(tq, tq, Q_PER_KV)
    cend =_iota((ncmp, tq), 0) * NSA_CMP_STRIDE + (NSA_CMP_LEN - 1)
    seen = _tile_lanes(cend <= t0 + _iota((ncmp, tq), 1), Q_PER_KV)
    cstart = _iota((nsb, ncmp), 1) * NSA_CMP_STRIDE
    bstart = _iota((nsb, ncmp), 0) * NSA_SEL_BLOCK
    overlap_t = jnp.where((cstart < bstart + NSA_SEL_BLOCK) & (cstart + NSA_CMP_LEN > bstart), 1.0, 0.0)
    blk = _iota((nsb, tq), 0)
    cur = (t0 + _iota((nsb, tq), 1)) >> (NSA_SEL_BLOCK.bit_length() - 1)
    forced = (blk == 0) | (blk == cur) | (blk == cur - 1)
    groups = range(KV_HEADS)
    far = i - NSA_WINDOW // tq
    lo = jnp.maximum(far + 1, 0)
    first_win = jnp.maximum(far, 0)

    state = (m_s, acc_s)
    kv = [(r, g) for r in rows for g in groups]
    sel_slots = tuple(range(len(kv)))
    win_slots = tuple(len(kv) + n for n in range(len(kv)))
    _flash_reset(state)
    q_ts = [q_ref[r].T for r in rows]
    queries = [_group_query(q_ts[r], alibi_ref, g, gq_ref[...]) for r, g in kv]
    qa_win = [_augment_query(qn, alibi, flipped=g == 1) for (r, g), (qn, alibi) in zip(kv, queries)]
    qa_sel = []

    def scores(slot, c):
        n = slot % len(kv)
        ka_ref, qa = (ksa_s, qa_sel) if slot in sel_slots else (kwa_s, qa_win)
        return _dot(ka_ref[n, pl.ds(pl.multiple_of(c * tq, tq), tq), :], qa[n])

    def values(slot, c):
        r, g = kv[slot % len(kv)]
        return (vst_s if slot in sel_slots else vwt_s)[r, c, g * hd:(g + 1) * hd, :]

    cmp_sc = [_dot(kca_s[n], qa_win[n]) for n in range(len(kv))]
    sc_win_first = [scores(slot, first_win) for slot in win_slots]
    o_cmp = []
    for n, (r, g) in enumerate(kv):
        qn, alibi = queries[n]
        sc = jnp.where(seen, cmp_sc[n], -BIG)
        e = jnp.where(seen, jnp.exp2(sc - jnp.max(sc, axis=0, keepdims=True)), 0.0)
        p_c = e / jnp.maximum(jnp.sum(e, axis=0, keepdims=True), TINY)
        o_cmp.append(_dot(vct_s[r, g * hd:(g + 1) * hd, :], p_c.astype(BF16)))

        psum = p_c[:, 0:tq]
        for k in range(1, Q_PER_KV):
            psum = psum + p_c[:, k * tq:(k + 1) * tq]
        imp = _dot_fine(overlap_t, psum, a_is_bf16_exact=True)
        imp = jnp.where(blk <= cur, jnp.where(forced, NSA_FORCE, imp), NEG)
        chosen = _rank_select(imp, blk, nsb, min(NSA_TOPN, nsb))
        sel_bias = _tile_lanes(jnp.where(chosen, 0.0, -BIG), Q_PER_KV)
        qa_sel.append(_augment_query(qn, alibi, sel_bias, flipped=g == 1))

    bufs = (sc0_s, sc1_s)
    for slot in sel_slots:
        sc0_s[slot] = scores(slot, 0)
    for slot, sc in zip(win_slots, sc_win_first):
        sc0_s[slot] = sc
        sc1_s[slot] = sc

    def body(slots, cap, cap_slots, triangular, c, carry):
        _flash_update_pipelined(state, bufs, slots, c, lambda slot: scores(slot, c + 1),
                                lambda slot: values(slot, c), cap, cap_slots, triangular)
        return carry

    both_slots = sel_slots + win_slots
    edge_shape = (1, _half_lanes(tq, Q_PER_KV, upper=False))
    diag_shape = (0, _half_lanes(tq, Q_PER_KV, upper=True))
    lax.fori_loop(0, first_win, functools.partial(body, sel_slots, None, None, None), 0)
    lax.fori_loop(first_win, lo, functools.partial(body, both_slots, edge_cap, win_slots, edge_shape), 0)
    lax.fori_loop(lo, i, functools.partial(body, both_slots, None, None, None), 0)
    _flash_update_pipelined(state, bufs, both_slots, i, None, lambda slot: values(slot, i), diag_cap,
                            triangular=diag_shape)

    for r in rows:
        gates_t = _sigmoid(gate_ref[r]).T
        outs = []
        for g in groups:
            def gate_row(branch):
                parts = [gates_t[branch * N_HEADS + h:branch * N_HEADS + h + 1, :]
                         for h in range(Q_PER_KV * g, Q_PER_KV * (g + 1))]
                return jnp.concatenate(parts, axis=1)

            n = KV_HEADS * r + g
            o_slc = _flash_result(state, sel_slots[n])
            o_win = _flash_result(state, win_slots[n])
            o = gate_row(0) * o_cmp[n] + gate_row(1) * o_slc + gate_row(2) * o_win
            outs.append(_regroup_heads(o, tq))
        y = jnp.concatenate(outs, axis=0).T * _silu(z_ref[r])
        o_ref[r] = y.astype(o_ref.dtype)


def _nsa(proj, kcn, vc, gq, gks, gkw, *, tq=256, rows_per_step=2):
    bsz, s, _ = proj.shape
    hd = HEAD_DIM
    gw = GROUP_WIDTH
    ncmp = kcn.shape[1]
    rps = rows_per_step
    chains = 2 * KV_HEADS * rps
    assert SEL_FEAT0 + s // NSA_SEL_BLOCK <= hd and ncmp == LANES and bsz % rps == 0
    wide = lambda c0: pl.BlockSpec((rps, tq, gw), lambda b, i: (b, i, c0 // 4))
    full = lambda c0: pl.BlockSpec((rps, s, LANES), lambda b, i: (b, 0, c0))
    cmp = pl.BlockSpec((rps, ncmp, LANES), lambda b, i: (b, 0, 0))
    row = pl.BlockSpec((1, hd), lambda b, i: (0, 0))
    ka = pltpu.VMEM((KV_HEADS * rps, s, LANES), BF16)
    vt = pltpu.VMEM((rps, s // tq, LANES, tq), BF16)
    return pl.pallas_call(
        _nsa_kernel,
        grid=(bsz // rps, s // tq),
        in_specs=[pl.BlockSpec((N_HEADS, 2 * SLOPE_PIECES, tq), lambda b, i: (0, 0, 0)),
                  wide(ODD_QC), cmp, cmp, full(ODD_KS), full(ODD_VS), full(ODD_KW), full(ODD_VW),
                  pl.BlockSpec((rps, tq, LANES), lambda b, i: (b, i, ODD_GC)), wide(ODD_ZC),
                  pl.BlockSpec((hd, 1), lambda b, i: (0, 0)), row, row],
        out_specs=pl.BlockSpec((rps, tq, gw), lambda b, i: (b, i, 0)),
        out_shape=jax.ShapeDtypeStruct((bsz, s, gw), BF16),
        scratch_shapes=[pltpu.VMEM((KV_HEADS * rps, ncmp, LANES), BF16), pltpu.VMEM((rps, LANES, ncmp), BF16),
                        ka, vt, ka, vt] + _flash_scratch(chains, Q_PER_KV * tq)
                       + [pltpu.VMEM((chains, tq, Q_PER_KV * tq), F32)] * 2,
        name="nsa",
    )(jnp.asarray(_alibi_query_features(tq)), proj, kcn, vc, proj, proj, proj, proj, proj, proj,
      gq.reshape(hd, 1), gks.reshape(1, hd), gkw.reshape(1, hd))


def _swa_kernel(sinks_ref, alibi_ref, q_ref, k_ref, v_ref, z_ref, gq_ref, gk_ref, o_ref, ka_s, vt_s):
    step = pl.program_id(1)
    tq = SWA_WINDOW
    tiles = q_ref.shape[1] // tq
    s = k_ref.shape[1]

    @pl.when(step == 0)
    def _():
        _augment_keys(k_ref[0], gk_ref, ka_s)
        _store_chunked_t(v_ref[0], vt_s, tq)

    q_t = q_ref[0].T
    diag_cap = _causal_cap(tq, tq, Q_PER_KV)
    edge_cap = _window_edge_cap(tq, tq, Q_PER_KV)
    chains = [(u, g) for u in range(tiles) for g in range(KV_HEADS)]
    qas = [_augment_query(*_group_query(q_t[:, u * tq:(u + 1) * tq], alibi_ref, g, gq_ref[...]), flipped=g == 1)
           for u, g in chains]
    first = [_window_first_chunk(ka_s, vt_s, g, step * tiles + u - 1, qa, tq, edge_cap)
             for (u, g), qa in zip(chains, qas)]
    own = [_chunk_scores(ka_s, vt_s, g, step * tiles + u, qa, tq) for (u, g), qa in zip(chains, qas)]
    carries = _flash_steps([_flash_init(Q_PER_KV * tq) for _ in chains], first)
    carries = _flash_steps(carries, own, diag_cap)
    outs = []
    for (u, g), carry in zip(chains, carries):
        m, acc, l = _flash_split(carry)
        tpos = ((step * tiles + u) * tq + _iota((1, tq), 1)).astype(F32)
        sink = jnp.concatenate([sinks_ref[h] * LOG2E + (SLOPES[h] * LOG2E) * tpos
                                for h in range(Q_PER_KV * g, Q_PER_KV * (g + 1))], axis=1)
        mf = jnp.maximum(m, sink)
        alpha = jnp.exp2(m - mf)
        o = acc * alpha / jnp.maximum(l * alpha + jnp.exp2(sink - mf), TINY)
        outs.append(_regroup_heads(o, tq))
    o_t = jnp.concatenate([jnp.concatenate(outs[u * KV_HEADS:(u + 1) * KV_HEADS], axis=0) for u in range(tiles)],
                          axis=1)
    y = o_t.T * _silu(z_ref[0])
    o_ref[0] = y.astype(o_ref.dtype)


def _swa(proj, sinks, gq, gk, *, tiles_per_step=8):
    bsz, s, _ = proj.shape
    hd = HEAD_DIM
    gw = GROUP_WIDTH
    tq = SWA_WINDOW
    rows = tiles_per_step * tq
    wide = lambda c0: pl.BlockSpec((1, rows, gw), lambda b, i: (b, i, c0 // 4))
    full = lambda c0: pl.BlockSpec((1, s, LANES), lambda b, i: (b, 0, c0))
    return pl.pallas_call(
        _swa_kernel,
        grid=(bsz, s // rows),
        in_specs=[pl.BlockSpec(memory_space=pltpu.SMEM),
                  pl.BlockSpec((N_HEADS, 2 * SLOPE_PIECES, tq), lambda b, i: (0, 0, 0)),
                  wide(ODD_QD), full(ODD_KD), full(ODD_VD), wide(ODD_ZD),
                  pl.BlockSpec((hd, 1), lambda b, i: (0, 0)), pl.BlockSpec((1, hd), lambda b, i: (0, 0))],
        out_specs=pl.BlockSpec((1, rows, gw), lambda b, i: (b, i, 0)),
        out_shape=jax.ShapeDtypeStruct((bsz, s, gw), BF16),
        scratch_shapes=[pltpu.VMEM((KV_HEADS, s, LANES), BF16), pltpu.VMEM((s // tq, LANES, tq), BF16)],
        name="swa",
    )(sinks.astype(F32), jnp.asarray(_alibi_query_features(tq)), proj, proj, proj, proj,
      gq.reshape(hd, 1), gk.reshape(1, hd))


def _odd_weight_moves():
    gw, kw, ng = GROUP_WIDTH, KV_HEADS * HEAD_DIM, 3 * N_HEADS
    sizes = [gw, kw, kw, kw, kw, kw, kw, ng, gw, gw, kw, kw, gw]
    starts = np.concatenate([[0], np.cumsum(sizes)]).tolist()
    order = [0, 8, 9, 12, 1, 2, 3, 4, 5, 6, 10, 11, 7]
    moves, dst = [], 0
    for k in order:
        moves.append((starts[k], sizes[k], dst))
        dst += sizes[k]
    assert dst + LANES - ng == ODD_COLS
    return moves


def kernel(x, norm_g, w_out, e_w_in, a_conv_w, a_conv_b, a_ln_g, a_ln_b, b_qnorm_g, b_knorm_g, o_w_in, c_qnorm_g, c_knorm_cmp_g, c_knorm_slc_g, c_knorm_win_g, c_pos_k, c_pos_v, c_k_w1, c_k_b1, c_k_w2, c_k_b2, c_v_w1, c_v_b1, c_v_w2, c_v_b2, d_qnorm_g, d_knorm_g, d_sinks):
    bsz, s, d = x.shape
    m = bsz * s
    assert s % MOBA_BLOCK == 0 and d == 2 * GROUP_WIDTH
    x2 = x.reshape(m, d)

    y_a, z_b, q_t, ka, vt, km = _even_layer_front(
        x2, norm_g[0], e_w_in[0].astype(BF16), b_qnorm_g[0], b_knorm_g[0],
        a_conv_w[0], a_conv_b[0], a_ln_g[0], a_ln_b[0], bsz=bsz)
    y_b = _moba(z_b.reshape(bsz, s, -1), q_t, ka, vt, km)

    x2, proj, k_cmp, v_cmp = _layer_boundary(
        x2, y_a, y_b.reshape(m, -1), w_out[0].astype(BF16), norm_g[1], o_w_in[0].astype(BF16),
        e=ODD_COLS, moves=_odd_weight_moves(), chunk=5 * LANES,
        copies=((ODD_KC * LANES, LANES), (ODD_VC * LANES, LANES)), group=NSA_CMP_STRIDE)
    proj = proj.reshape(bsz, s, ODD_COLS)
    rows16 = lambda t: t.reshape(bsz, s // NSA_CMP_STRIDE, NSA_CMP_STRIDE * LANES)
    kcn, vc = _compress(rows16(k_cmp), rows16(v_cmp), c_pos_k[0], c_pos_v[0], c_k_w1[0], c_k_b1[0],
                        c_k_w2[0], c_k_b2[0], c_v_w1[0], c_v_b1[0], c_v_w2[0], c_v_b2[0], c_knorm_cmp_g[0])
    y_c = _nsa(proj, kcn, vc, c_qnorm_g[0], c_knorm_slc_g[0], c_knorm_win_g[0])
    y_d = _swa(proj, d_sinks[0], d_qnorm_g[0], d_knorm_g[0])
    x2 = _outproj(x2, y_c.reshape(m, -1), y_d.reshape(m, -1), w_out[1].astype(BF16))
    return x2.reshape(bsz, s, d)
```

```python
import functools

import ml_dtypes
import numpy as np
import jax
import jax.numpy as jnp
from jax import lax
from jax.experimental import pallas as pl
from jax.experimental.pallas import tpu as pltpu

HEAD_DIM = 64
N_HEADS = 8
GROUP_WIDTH = N_HEADS * HEAD_DIM
CONV_WIDTH = 31
MOBA_BLOCK = 256
MOBA_TOPK = 3
KV_HEADS = 2
Q_PER_KV = N_HEADS // KV_HEADS
NSA_CMP_LEN = 32
NSA_CMP_STRIDE = 16
NSA_CMP_HIDDEN = 256
NSA_SEL_BLOCK = 64
NSA_TOPN = 16
NSA_WINDOW = 512
NSA_FORCE = 1e4
SWA_WINDOW = 128
EPS = 1e-6
NEG = -1e30
TINY = 1e-30
LANES = 128
SUBLANES = 8
CONV_HALO = 32

LOG2E = float(np.log2(np.e))
BIG = 2.0 ** 99
M_INIT = -1e38
POS_SPLIT_SHIFT = 8
SLOPE_PIECES = 4
SEL_FEAT0 = 2 * SLOPE_PIECES

F32 = jnp.float32
BF16 = jnp.bfloat16

ODD_QC, ODD_ZC, ODD_QD, ODD_ZD = 0, 4, 8, 12
ODD_KC, ODD_VC, ODD_KS, ODD_VS, ODD_KW, ODD_VW, ODD_KD, ODD_VD, ODD_GC = 16, 17, 18, 19, 20, 21, 22, 23, 24
ODD_COLS = 25 * LANES


def _alibi_slopes(n):
    return [float(2.0 ** (-8.0 * (i + 1) / n)) for i in range(n)]


SLOPES = _alibi_slopes(N_HEADS)


def _alibi_query_features(width):
    table = np.zeros((N_HEADS, 2 * SLOPE_PIECES, width), np.float32)
    for h, slope in enumerate(SLOPES):
        rest = np.float64(slope) * LOG2E
        for k in range(SLOPE_PIECES):
            piece = float(np.float32(rest).astype(ml_dtypes.bfloat16).astype(np.float32))
            table[h, k] = piece
            table[h, SLOPE_PIECES + k] = piece
            rest -= piece
    return table


def _dot(a, b):
    return jnp.dot(a, b, preferred_element_type=F32)


def _split_bf16(x, pieces):
    out = []
    for _ in range(pieces):
        out.append(x.astype(BF16))
        x = x - out[-1].astype(F32)
    return out


def _dot_fine(a, b, a_is_bf16_exact=False):
    if a_is_bf16_exact:
        return sum(_dot(a.astype(BF16), piece) for piece in _split_bf16(b, 3))
    (a_hi, a_lo), (b_hi, b_lo) = _split_bf16(a, 2), _split_bf16(b, 2)
    return _dot(a_hi, b_hi) + (_dot(a_hi, b_lo) + _dot(a_lo, b_hi))


def _sigmoid(x):
    return 1.0 / (1.0 + jnp.exp(-x))


def _silu(x):
    return x * _sigmoid(x)


def _rms(x, g):
    return x * lax.rsqrt(jnp.mean(x * x, axis=-1, keepdims=True) + EPS) * g


def _iota(shape, dim):
    return lax.broadcasted_iota(jnp.int32, shape, dim)


def _layer_boundary_kernel(x_ref, ya_ref, yb_ref, wa_ref, wb_ref, g_ref, w_ref, x_out_ref, o_ref, *rest,
                           chunk, copies, group, moves):
    copy_refs, stage_refs, wp_s = rest[:len(copies)], rest[len(copies):-1], rest[-1]

    @pl.when(pl.program_id(0) == 0)
    def _():
        wp_s[...] = jnp.zeros(wp_s.shape, wp_s.dtype)
        for src, width, dst in moves:
            wp_s[:, dst:dst + width] = w_ref[:, src:src + width]

    x = x_ref[...] + _dot(ya_ref[...], wa_ref[...]) + _dot(yb_ref[...], wb_ref[...])
    x_out_ref[...] = x
    h = _rms(x, g_ref[...]).astype(BF16)
    for c in range(o_ref.shape[1] // chunk):
        o_ref[:, c * chunk:(c + 1) * chunk] = _dot(h, wp_s[:, c * chunk:(c + 1) * chunk])
    for ref, stage, (start, width) in zip(copy_refs, stage_refs, copies):
        stage[...] = o_ref[:, start:start + width]
        for t in range(group):
            ref[:, t * width:(t + 1) * width] = stage[pl.ds(t, ref.shape[0], stride=group), :]


def _layer_boundary(x2d, ya, yb, w_out, g, w, *, e, moves, tm=512, chunk, copies=(), group=1):
    m, d = x2d.shape
    gw = ya.shape[1]
    rows = lambda n, width: pl.BlockSpec((n, width), lambda i: (i, 0))
    const = lambda shape: pl.BlockSpec(shape, lambda i: (0, 0))
    return pl.pallas_call(
        functools.partial(_layer_boundary_kernel, chunk=chunk, copies=tuple(copies), group=group,
                          moves=tuple(moves)),
        grid=(m // tm,),
        in_specs=[rows(tm, d), rows(tm, gw), rows(tm, gw), const((gw, d)), const((gw, d)), const((1, d)),
                  const(w.shape)],
        out_specs=[rows(tm, d), rows(tm, e)] + [rows(tm // group, group * width) for _, width in copies],
        out_shape=[jax.ShapeDtypeStruct((m, d), F32), jax.ShapeDtypeStruct((m, e), F32)]
                  + [jax.ShapeDtypeStruct((m // group, group * width), F32) for _, width in copies],
        scratch_shapes=[pltpu.VMEM((tm, width), F32) for _, width in copies] + [pltpu.VMEM((d, e), BF16)],
        name="layer_boundary",
    )(x2d, ya, yb, w_out[:gw], w_out[gw:], g.reshape(1, d), w)


def _conv_rows(h_s, row0, rows, w_ref, b_ref, lg_ref, lb_ref):
    base = CONV_HALO - (CONV_WIDTH - 1)
    acc = None
    for b in range(SUBLANES):
        n = rows if b == 0 else rows + SUBLANES
        part = None
        for a in range((base + CONV_WIDTH - 1) // SUBLANES + 1):
            j = SUBLANES * a + b - base
            if 0 <= j < CONV_WIDTH:
                term = w_ref[j:j + 1, :] * h_s[pl.ds(row0 + SUBLANES * a, n), :]
                part = term if part is None else part + term
        part = part[b:b + rows]
        acc = part if acc is None else acc + part
    y = acc + b_ref[...]
    mu = jnp.mean(y, axis=-1, keepdims=True)
    yc = y - mu
    return _silu(yc * lax.rsqrt(jnp.mean(yc * yc, axis=-1, keepdims=True) + EPS) * lg_ref[...] + lb_ref[...])


def _even_layer_kernel(x_ref, g_ref, w_ref, gq_ref, gk_ref, cw_ref, cb_ref, lg_ref, lb_ref,
                       ya_ref, zb_ref, qt_ref, ka_ref, vt_ref, km_ref, hb_s, conv_s, moba_s, h_s, tail_s,
                       *, seq, rows):
    tm = x_ref.shape[0]
    hd, gw = HEAD_DIM, GROUP_WIDTH
    tile_in_seq = pl.program_id(0) % (seq // tm)
    hb_s[...] = _rms(x_ref[...], g_ref[...]).astype(BF16)
    conv_s[...] = _dot(hb_s[...], w_ref[:, 0:3 * gw])
    h_s[0:CONV_HALO, :] = jnp.where(tile_in_seq > 0, tail_s[...], 0.0)
    h_s[CONV_HALO:, :] = conv_s[:, 0:gw] * _sigmoid(conv_s[:, gw:2 * gw])
    tail_s[...] = h_s[tm:tm + CONV_HALO, :]
    for c in range(tm // rows):
        y = _conv_rows(h_s, c * rows, rows, cw_ref, cb_ref, lg_ref, lb_ref)
        gate = _silu(conv_s[c * rows:(c + 1) * rows, 2 * gw:3 * gw])
        ya_ref[c * rows:(c + 1) * rows, :] = (y * gate).astype(ya_ref.dtype)

    moba_s[...] = _dot(hb_s[...], w_ref[:, 3 * gw:7 * gw])
    q_t = moba_s[:, 0:gw].T
    for j in range(N_HEADS):
        qt_ref[0, j * hd:(j + 1) * hd, :] = _norm_query_t(q_t[j * hd:(j + 1) * hd], gq_ref[...])
    pos = tile_in_seq * tm + _iota((tm, LANES), 0)
    for pair in range(N_HEADS // 2):
        kn = _pair_rms(moba_s[:, gw + pair * LANES:gw + (pair + 1) * LANES], gk_ref[...])
        ka_ref[0, 2 * pair], ka_ref[0, 2 * pair + 1] = _augment_pair_keys(kn, pos, MOBA_BLOCK.bit_length() - 1)
        km_ref[0, 0, :, pair * LANES:(pair + 1) * LANES] = jnp.mean(
            kn.reshape(tm // MOBA_BLOCK, MOBA_BLOCK, LANES), axis=1)
    v_t = moba_s[:, 2 * gw:3 * gw].T.astype(BF16)
    for c in range(tm // MOBA_BLOCK):
        vt_ref[0, c] = v_t[:, c * MOBA_BLOCK:(c + 1) * MOBA_BLOCK]
    zb_ref[...] = moba_s[:, 3 * gw:4 * gw]


def _even_layer_front(x2d, g, w, gq, gk, conv_w, conv_b, ln_g, ln_b, *, bsz, tm=512, rows=128):
    m, d = x2d.shape
    e = w.shape[1]
    s = m // bsz
    per_b = s // tm
    hd, gw, nb = HEAD_DIM, GROUP_WIDTH, tm // MOBA_BLOCK
    assert s % tm == 0 and tm % MOBA_BLOCK == 0 and e == 7 * gw and tm % rows == 0
    const = lambda shape: pl.BlockSpec(shape, lambda i: (0, 0))
    tile = pl.BlockSpec((tm, gw), lambda i: (i, 0))
    vec = const((1, gw))
    return pl.pallas_call(
        functools.partial(_even_layer_kernel, seq=s, rows=rows),
        grid=(m // tm,),
        in_specs=[pl.BlockSpec((tm, d), lambda i: (i, 0)), const((1, d)), const((d, e)), const((hd, 1)),
                  const((1, hd)), const((CONV_WIDTH, gw)), vec, vec, vec],
        out_specs=[tile, tile,
                   pl.BlockSpec((1, gw, tm), lambda i: (i // per_b, 0, i % per_b)),
                   pl.BlockSpec((1, N_HEADS, tm, LANES), lambda i: (i // per_b, 0, i % per_b, 0)),
                   pl.BlockSpec((1, nb, gw, MOBA_BLOCK), lambda i: (i // per_b, i % per_b, 0, 0)),
                   pl.BlockSpec((1, 1, nb, gw), lambda i: (i // per_b, i % per_b, 0, 0))],
        out_shape=[jax.ShapeDtypeStruct((m, gw), BF16),
                   jax.ShapeDtypeStruct((m, gw), F32),
                   jax.ShapeDtypeStruct((bsz, gw, s), F32),
                   jax.ShapeDtypeStruct((bsz, N_HEADS, s, LANES), BF16),
                   jax.ShapeDtypeStruct((bsz, s // MOBA_BLOCK, gw, MOBA_BLOCK), BF16),
                   jax.ShapeDtypeStruct((bsz, per_b, nb, gw), F32)],
        scratch_shapes=[pltpu.VMEM((tm, d), BF16), pltpu.VMEM((tm, 3 * gw), F32), pltpu.VMEM((tm, 4 * gw), F32),
                        pltpu.VMEM((CONV_HALO + tm, gw), F32), pltpu.VMEM((CONV_HALO, gw), F32)],
        name="even_layer_front",
    )(x2d, g.reshape(1, d), w, gq.reshape(hd, 1), gk.reshape(1, hd), conv_w, conv_b.reshape(1, gw),
      ln_g.reshape(1, gw), ln_b.reshape(1, gw))


def _outproj_kernel(x_ref, ya_ref, yb_ref, wa_ref, wb_ref, o_ref):
    o_ref[...] = x_ref[...] + _dot(ya_ref[...], wa_ref[...]) + _dot(yb_ref[...], wb_ref[...])


def _outproj(x2d, ya, yb, w, *, tm=1024):
    m, d = x2d.shape
    gw = ya.shape[1]
    return pl.pallas_call(
        _outproj_kernel,
        grid=(m // tm,),
        in_specs=[pl.BlockSpec((tm, d), lambda i: (i, 0)),
                  pl.BlockSpec((tm, gw), lambda i: (i, 0)),
                  pl.BlockSpec((tm, gw), lambda i: (i, 0)),
                  pl.BlockSpec((gw, d), lambda i: (0, 0)),
                  pl.BlockSpec((gw, d), lambda i: (0, 0))],
        out_specs=pl.BlockSpec((tm, d), lambda i: (i, 0)),
        out_shape=jax.ShapeDtypeStruct((m, d), F32),
        name="outproj",
    )(x2d, ya, yb, w[:gw], w[gw:])


def _key_features(pos, flipped, sel_shift=None):
    col = _iota(pos.shape, 1) - (0 if flipped else HEAD_DIM)
    hi = (pos >> POS_SPLIT_SHIFT) << POS_SPLIT_SHIFT
    lo = pos & ((1 << POS_SPLIT_SHIFT) - 1)
    feat = jnp.where(col < SLOPE_PIECES, hi, jnp.where(col < 2 * SLOPE_PIECES, lo, 0))
    if sel_shift is not None:
        feat = jnp.where(col - SEL_FEAT0 == (pos >> sel_shift), 1, feat)
    return jnp.where((col >= 0) & (col < HEAD_DIM), feat, 0).astype(F32)


def _pair_rms(x, g):
    sq = x * x
    hi = sq.astype(BF16)
    lo = (sq - hi.astype(F32)).astype(BF16)
    head_shift = HEAD_DIM.bit_length() - 1
    same_head = (_iota((LANES, LANES), 0) >> head_shift) == (_iota((LANES, LANES), 1) >> head_shift)
    ones = jnp.where(same_head, 1.0, 0.0).astype(BF16)
    ss = _dot(hi, ones) + _dot(lo, ones)
    return x * lax.rsqrt(ss * (1.0 / HEAD_DIM) + EPS) * jnp.concatenate([g, g], axis=1)


def _augment_pair_keys(kn, pos, sel_shift=None):
    lane = _iota(kn.shape, 1)
    return [jnp.where(lane >= HEAD_DIM if flipped else lane < HEAD_DIM, kn,
                      _key_features(pos, flipped, sel_shift)).astype(BF16) for flipped in (False, True)]


def _norm_query_t(x, g_col):
    ss = jnp.mean(x * x, axis=0, keepdims=True)
    return x * lax.rsqrt(ss + EPS) * g_col * (HEAD_DIM ** -0.5 * LOG2E)


def _augment_query(q_t, alibi, sel_bias=None, flipped=False):
    n = q_t.shape[1]
    feats = [alibi]
    used = alibi.shape[0]
    if sel_bias is not None:
        feats.append(sel_bias)
        used += sel_bias.shape[0]
    feats.append(jnp.zeros((HEAD_DIM - used, n), F32))
    return jnp.concatenate(feats + [q_t] if flipped else [q_t] + feats, axis=0).astype(BF16)


ONES_ROWS = 16
ACC_ROWS = HEAD_DIM + ONES_ROWS


def _flash_init(n):
    return jnp.full((1, n), M_INIT, F32), jnp.zeros((ACC_ROWS, n), F32)


def _flash_step(carry, sc, v_t):
    m, acc = carry
    m_new = jnp.maximum(m, jnp.max(sc, axis=0, keepdims=True))
    p = jnp.exp2(sc - m_new).astype(BF16)
    v_ones = jnp.concatenate([v_t, jnp.ones((ONES_ROWS, v_t.shape[1]), BF16)], axis=0)
    return m_new, jnp.exp2(m - m_new) * acc + _dot(v_ones, p)


def _flash_split(carry):
    m, acc = carry
    return m, acc[:HEAD_DIM], acc[HEAD_DIM:HEAD_DIM + 1]


def _flash_steps(carries, chunks, cap=None):
    return tuple(_flash_step(carry, sc if cap is None else jnp.minimum(sc, cap), v_t)
                 for carry, (sc, v_t) in zip(carries, chunks))


def _flash_reset(state):
    m_ref, acc_ref = state
    m_ref[...] = jnp.full(m_ref.shape, M_INIT, F32)
    acc_ref[...] = jnp.zeros(acc_ref.shape, F32)


def _flash_update(state, slot, sc, v_t, cap=None):
    m_ref, acc_ref = state
    m_ref[slot], acc_ref[slot] = _flash_step((m_ref[slot], acc_ref[slot]),
                                             sc if cap is None else jnp.minimum(sc, cap), v_t)


def _flash_update_triangular(state, slot, sc_ref, v_t, cap, full_half, partial_lanes):
    m_ref, acc_ref = state
    half = sc_ref.shape[0] // 2
    full = slice(full_half * half, (full_half + 1) * half)
    part = slice((1 - full_half) * half, (2 - full_half) * half)
    _flash_update(state, slot, sc_ref[full, :], v_t[:, full], cap[full, :])

    def gather(x):
        return jnp.concatenate([x[..., a:b] for a, b in partial_lanes], axis=-1)

    sc = jnp.minimum(jnp.concatenate([sc_ref[part, a:b] for a, b in partial_lanes], axis=1), gather(cap[part, :]))
    m, acc = _flash_step((gather(m_ref[slot]), gather(acc_ref[slot])), sc, v_t[:, part])
    width = partial_lanes[0][1] - partial_lanes[0][0]
    for k, (a, b) in enumerate(partial_lanes):
        m_ref[slot, :, a:b] = m[:, k * width:(k + 1) * width]
        acc_ref[slot, :, a:b] = acc[:, k * width:(k + 1) * width]


def _half_lanes(tq, reps, upper):
    off = tq // 2 if upper else 0
    return [(k * tq + off, k * tq + off + tq // 2) for k in range(reps)]


def _flash_update_pipelined(state, bufs, slots, c, next_scores, values, cap=None, cap_slots=None,
                            triangular=None):
    def run(src_ref, dst_ref):
        for slot in slots:
            fresh = None if next_scores is None else next_scores(slot)
            masked = cap is not None and (cap_slots is None or slot in cap_slots)
            if masked and triangular is not None:
                _flash_update_triangular(state, slot, src_ref.at[slot], values(slot), cap, *triangular)
            else:
                _flash_update(state, slot, src_ref[slot], values(slot), cap if masked else None)
            if fresh is not None:
                dst_ref[slot] = fresh

    pl.when((c & 1) == 0)(lambda: run(bufs[0], bufs[1]))
    pl.when((c & 1) == 1)(lambda: run(bufs[1], bufs[0]))


def _flash_result(state, slot):
    m_ref, acc_ref = state
    m, out, den = _flash_split((m_ref[slot], acc_ref[slot]))
    return out / jnp.maximum(den, TINY)


def _flash_scratch(chains, n):
    return [pltpu.VMEM((chains, 1, n), F32), pltpu.VMEM((chains, ACC_ROWS, n), F32)]


def _rank_select(score, blk, limit, count):
    ranks = []
    for r0 in range(0, score.shape[0], SUBLANES):
        tile = score[r0:r0 + SUBLANES]
        tile_blk = r0 + _iota(tile.shape, 0)
        rank = jnp.zeros(tile.shape, jnp.int32)
        for mm in range(limit):
            gm = score[mm:mm + 1, :]
            if mm < r0:
                beats = gm >= tile
            elif mm >= r0 + SUBLANES:
                beats = gm > tile
            else:
                beats = (gm > tile) | ((gm == tile) & (mm < tile_blk))
            rank = rank + jnp.where(beats, 1, 0)
        ranks.append(rank)
    return jnp.concatenate(ranks, axis=0) < count


def _tile_lanes(x, reps):
    return jnp.concatenate([x] * reps, axis=1) if reps > 1 else x


def _causal_cap(tk, tq, reps):
    keep = _iota((tk, tq), 0) <= _iota((tk, tq), 1)
    return _tile_lanes(jnp.where(keep, BIG, -BIG), reps)


def _window_edge_cap(tk, tq, reps):
    keep = _iota((tk, tq), 0) > _iota((tk, tq), 1)
    return _tile_lanes(jnp.where(keep, BIG, -BIG), reps)


def _store_chunked_t(x, dst_ref, tk):
    x_t = x.T.astype(BF16)
    for c in range(x.shape[0] // tk):
        dst_ref[c] = x_t[:, c * tk:(c + 1) * tk]


def _moba_kernel(alibi_ref, qt_ref, ka_ref, vt_ref, km_ref, z_ref, o_ref, m_s, acc_s, sc0_s, sc1_s):
    i = pl.program_id(1)
    nblk = vt_ref.shape[1]
    tq = MOBA_BLOCK
    hd = HEAD_DIM
    chains = [(r, j) for r in range(qt_ref.shape[0]) for j in range(N_HEADS)]
    slots = range(len(chains))

    blk = _iota((nblk, tq), 0)
    qas = []
    for r, j in chains:
        qn = qt_ref[r, j * hd:(j + 1) * hd, :]
        km = jnp.concatenate([km_ref[r, t, :, j * hd:(j + 1) * hd] for t in range(km_ref.shape[1])], axis=0)
        gate = jnp.where(blk < i, _dot_fine(km, qn), NEG)
        chosen = _rank_select(gate, blk, nblk, MOBA_TOPK) & (blk < i)
        sel_bias = jnp.where(chosen | (blk == i), 0.0, -BIG)
        qas.append(_augment_query(qn, alibi_ref[j], sel_bias, flipped=j % 2 == 1))

    def scores(slot, n):
        r, j = chains[slot]
        return _dot(ka_ref[r, j, pl.ds(pl.multiple_of(n * tq, tq), tq), :], qas[slot])

    def values(slot, n):
        r, j = chains[slot]
        return vt_ref[r, n, j * hd:(j + 1) * hd, :]

    state = (m_s, acc_s)
    _flash_reset(state)
    bufs = (sc0_s, sc1_s)
    for slot in slots:
        sc0_s[slot] = scores(slot, 0)

    def body(n, carry):
        _flash_update_pipelined(state, bufs, slots, n, lambda slot: scores(slot, n + 1),
                                lambda slot: values(slot, n))
        return carry

    lax.fori_loop(0, i, body, 0)
    diag_cap = _causal_cap(tq, tq, 1)
    _flash_update_pipelined(state, bufs, slots, i, None, lambda slot: values(slot, i), diag_cap,
                            triangular=(0, _half_lanes(tq, 1, upper=True)))
    for r in range(qt_ref.shape[0]):
        outs = [_flash_result(state, r * N_HEADS + j) for j in range(N_HEADS)]
        y = jnp.concatenate(outs, axis=0).T * _silu(z_ref[r])
        o_ref[r] = y.astype(o_ref.dtype)


def _moba(z, q_t, ka, vt, km, *, rows_per_step=2):
    bsz, s, _ = z.shape
    tq = MOBA_BLOCK
    gw = GROUP_WIDTH
    nblk = s // MOBA_BLOCK
    rps = rows_per_step
    assert SEL_FEAT0 + nblk <= HEAD_DIM and bsz % rps == 0
    whole = lambda a: pl.BlockSpec((rps,) + a.shape[1:], lambda b, i: (b,) + (0,) * (a.ndim - 1))
    return pl.pallas_call(
        _moba_kernel,
        grid=(bsz // rps, s // tq),
        in_specs=[pl.BlockSpec((N_HEADS, 2 * SLOPE_PIECES, tq), lambda b, i: (0, 0, 0)),
                  pl.BlockSpec((rps, gw, tq), lambda b, i: (b, 0, i)), whole(ka), whole(vt), whole(km),
                  pl.BlockSpec((rps, tq, gw), lambda b, i: (b, i, 0))],
        out_specs=pl.BlockSpec((rps, tq, gw), lambda b, i: (b, i, 0)),
        out_shape=jax.ShapeDtypeStruct((bsz, s, gw), BF16),
        scratch_shapes=_flash_scratch(rps * N_HEADS, tq) + [pltpu.VMEM((rps * N_HEADS, tq, tq), F32)] * 2,
        name="moba",
    )(jnp.asarray(_alibi_query_features(tq)), q_t, ka, vt, km, z)


def _cmp_kernel(xk_ref, xv_ref, wk1_ref, wv1_ref, pk_ref, pv_ref, bk1_ref, bv1_ref,
                wk2_ref, wv2_ref, bk2_ref, bv2_ref, gn_ref, ko_ref, vo_ref, wke_s, wve_s):
    half = xk_ref.shape[2]
    n = xk_ref.shape[1]
    hd = HEAD_DIM
    nh = NSA_CMP_HIDDEN

    @pl.when(pl.program_id(0) == 0)
    def _():
        for w_ref, we_ref in ((wk1_ref, wke_s), (wv1_ref, wve_s)):
            we_ref[...] = jnp.zeros(we_ref.shape, BF16)
            for tok in range(NSA_CMP_LEN):
                for g in range(KV_HEADS):
                    row = (tok * KV_HEADS + g) * hd
                    we_ref[row:row + hd, g * nh:(g + 1) * nh] = w_ref[tok * hd:(tok + 1) * hd, :]

    rows = xk_ref.shape[0]

    def hidden(x_ref, we_ref, pos_ref, b1_ref):
        x = x_ref[...].reshape(rows * n, half).astype(BF16)
        first = _dot(x, we_ref[:half, :])
        second = _dot(x, we_ref[half:, :])
        second = jnp.concatenate([pltpu.roll(second[r * n:(r + 1) * n], n - 1, 0) for r in range(rows)], axis=0)
        pos = jnp.broadcast_to(pos_ref[...], (8, 2 * half)).astype(BF16)
        ph = _dot(pos[:, :half], we_ref[:half, :]) + _dot(pos[:, half:], we_ref[half:, :])
        return _silu(first + second + ph[0:1, :] + b1_ref[...])

    hk = hidden(xk_ref, wke_s, pk_ref, bk1_ref).astype(BF16)
    hv = hidden(xv_ref, wve_s, pv_ref, bv1_ref).astype(BF16)
    ks, vs = [], []
    for g in range(KV_HEADS):
        ks.append(_rms(_dot(hk[:, g * nh:(g + 1) * nh], wk2_ref[...]) + bk2_ref[...], gn_ref[...]))
        vs.append(_dot(hv[:, g * nh:(g + 1) * nh], wv2_ref[...]) + bv2_ref[...])
    ko_ref[...] = jnp.concatenate(ks, axis=1).reshape(ko_ref.shape)
    vo_ref[...] = jnp.concatenate(vs, axis=1).reshape(vo_ref.shape)


def _compress(xk, xv, pos_k, pos_v, kw1, kb1, kw2, kb2, vw1, vb1, vw2, vb2, gn, *, rows_per_step=4):
    bsz, n, half = xk.shape
    hd = HEAD_DIM
    rps = rows_per_step
    assert bsz % rps == 0
    tile_pos = lambda p: jnp.tile(p[:, None, :], (1, KV_HEADS, 1)).reshape(1, 2 * half)
    tile_b = lambda b: jnp.tile(b.reshape(1, -1), (1, KV_HEADS))
    const = lambda shape: pl.BlockSpec(shape, lambda b: (0,) * len(shape))
    xs = pl.BlockSpec((rps, n, half), lambda b: (b, 0, 0))
    hw = KV_HEADS * NSA_CMP_HIDDEN
    out = pl.BlockSpec((rps, n, KV_HEADS * hd), lambda b: (b, 0, 0))
    expanded = pltpu.VMEM((2 * half, hw), BF16)
    return pl.pallas_call(
        _cmp_kernel,
        grid=(bsz // rps,),
        in_specs=[xs, xs, const(kw1.shape), const(vw1.shape),
                  const((1, 2 * half)), const((1, 2 * half)), const((1, hw)), const((1, hw)),
                  const((NSA_CMP_HIDDEN, hd)), const((NSA_CMP_HIDDEN, hd)), const((1, hd)), const((1, hd)),
                  const((1, hd))],
        out_specs=[out, out],
        out_shape=[jax.ShapeDtypeStruct((bsz, n, KV_HEADS * hd), F32)] * 2,
        scratch_shapes=[expanded, expanded],
        name="nsa_compress",
    )(xk, xv, kw1.astype(BF16), vw1.astype(BF16), tile_pos(pos_k), tile_pos(pos_v), tile_b(kb1), tile_b(vb1),
      kw2.astype(BF16), vw2.astype(BF16), kb2.reshape(1, hd), vb2.reshape(1, hd), gn.reshape(1, hd))


def _group_query(q_t, alibi_ref, g, gq_col):
    hd = HEAD_DIM
    heads = range(Q_PER_KV * g, Q_PER_KV * (g + 1))
    qn = jnp.concatenate([_norm_query_t(q_t[h * hd:(h + 1) * hd], gq_col) for h in heads], axis=1)
    alibi = jnp.concatenate([alibi_ref[h] for h in heads], axis=1)
    return qn, alibi


def _augment_keys(k, gk_ref, dst_ref, row=0, sel_shift=None):
    assert KV_HEADS == 2
    pos = _iota(k.shape, 0)
    dst_ref[2 * row], dst_ref[2 * row + 1] = _augment_pair_keys(_pair_rms(k, gk_ref[...]), pos, sel_shift)


def _chunk_scores(ka_ref, vt_ref, g, c, qa, tq):
    off = pl.multiple_of(c * tq, tq)
    return _dot(ka_ref[g, pl.ds(off, tq), :], qa), vt_ref[c, g * HEAD_DIM:(g + 1) * HEAD_DIM, :]


def _window_first_chunk(ka_ref, vt_ref, g, far, qa, tq, edge_cap):
    sc, v_t = _chunk_scores(ka_ref, vt_ref, g, jnp.maximum(far, 0), qa, tq)
    return jnp.minimum(sc, jnp.minimum(edge_cap, jnp.where(far >= 0, BIG, -BIG))), v_t


def _regroup_heads(o_t, tq):
    return jnp.concatenate([o_t[:, k * tq:(k + 1) * tq] for k in range(Q_PER_KV)], axis=0)


def _nsa_kernel(alibi_ref, q_ref, kc_ref, vc_ref, ks_ref, vs_ref, kw_ref, vw_ref, gate_ref, z_ref,
                gq_ref, gks_ref, gkw_ref, o_ref, kca_s, vct_s, ksa_s, vst_s, kwa_s, vwt_s, m_s, acc_s,
                sc0_s, sc1_s):
    i = pl.program_id(1)
    rows = range(q_ref.shape[0])
    tq = q_ref.shape[1]
    s = ks_ref.shape[1]
    hd = HEAD_DIM
    ncmp = kc_ref.shape[1]
    nsb = s // NSA_SEL_BLOCK

    @pl.when(i == 0)
    def _():
        cend = _iota((ncmp, LANES), 0) * NSA_CMP_STRIDE + (NSA_CMP_LEN - 1)
        for r in rows:
            _augment_keys(ks_ref[r], gks_ref, ksa_s, r, NSA_SEL_BLOCK.bit_length() - 1)
            _augment_keys(kw_ref[r], gkw_ref, kwa_s, r)
            _store_chunked_t(vs_ref[r], vst_s.at[r], tq)
            _store_chunked_t(vw_ref[r], vwt_s.at[r], tq)
            kca_s[2 * r], kca_s[2 * r + 1] = _augment_pair_keys(kc_ref[r], cend)
            vct_s[r] = vc_ref[r].T.astype(BF16)

    t0 = i * tq
    diag_cap = _causal_cap(tq, tq, Q_PER_KV)
    edge_cap = _window_edge_cap(tq, tq, Q_PER_KV)
    cend =_iota((ncmp, tq), 0) * NSA_CMP_STRIDE + (NSA_CMP_LEN - 1)
    seen = _tile_lanes(cend <= t0 + _iota((ncmp, tq), 1), Q_PER_KV)
    cstart = _iota((nsb, ncmp), 1) * NSA_CMP_STRIDE
    bstart = _iota((nsb, ncmp), 0) * NSA_SEL_BLOCK
    overlap_t = jnp.where((cstart < bstart + NSA_SEL_BLOCK) & (cstart + NSA_CMP_LEN > bstart), 1.0, 0.0)
    blk = _iota((nsb, tq), 0)
    cur = (t0 + _iota((nsb, tq), 1)) >> (NSA_SEL_BLOCK.bit_length() - 1)
    forced = (blk == 0) | (blk == cur) | (blk == cur - 1)
    groups = range(KV_HEADS)
    far = i - NSA_WINDOW // tq
    lo = jnp.maximum(far + 1, 0)
    first_win = jnp.maximum(far, 0)

    state = (m_s, acc_s)
    kv = [(r, g) for r in rows for g in groups]
    sel_slots = tuple(range(len(kv)))
    win_slots = tuple(len(kv) + n for n in range(len(kv)))
    _flash_reset(state)
    q_ts = [q_ref[r].T for r in rows]
    queries = [_group_query(q_ts[r], alibi_ref, g, gq_ref[...]) for r, g in kv]
    qa_win = [_augment_query(qn, alibi, flipped=g == 1) for (r, g), (qn, alibi) in zip(kv, queries)]
    qa_sel = []

    def scores(slot, c):
        n = slot % len(kv)
        ka_ref, qa = (ksa_s, qa_sel) if slot in sel_slots else (kwa_s, qa_win)
        return _dot(ka_ref[n, pl.ds(pl.multiple_of(c * tq, tq), tq), :], qa[n])

    def values(slot, c):
        r, g = kv[slot % len(kv)]
        return (vst_s if slot in sel_slots else vwt_s)[r, c, g * hd:(g + 1) * hd, :]

    cmp_sc = [_dot(kca_s[n], qa_win[n]) for n in range(len(kv))]
    sc_win_first = [scores(slot, first_win) for slot in win_slots]
    o_cmp = []
    for n, (r, g) in enumerate(kv):
        qn, alibi = queries[n]
        sc = jnp.where(seen, cmp_sc[n], -BIG)
        e = jnp.where(seen, jnp.exp2(sc - jnp.max(sc, axis=0, keepdims=True)), 0.0)
        p_c = e / jnp.maximum(jnp.sum(e, axis=0, keepdims=True), TINY)
        o_cmp.append(_dot(vct_s[r, g * hd:(g + 1) * hd, :], p_c.astype(BF16)))

        psum = p_c[:, 0:tq]
        for k in range(1, Q_PER_KV):
            psum = psum + p_c[:, k * tq:(k + 1) * tq]
        imp = _dot_fine(overlap_t, psum, a_is_bf16_exact=True)
        imp = jnp.where(blk <= cur, jnp.where(forced, NSA_FORCE, imp), NEG)
        chosen = _rank_select(imp, blk, nsb, min(NSA_TOPN, nsb))
        sel_bias = _tile_lanes(jnp.where(chosen, 0.0, -BIG), Q_PER_KV)
        qa_sel.append(_augment_query(qn, alibi, sel_bias, flipped=g == 1))

    bufs = (sc0_s, sc1_s)
    for slot in sel_slots:
        sc0_s[slot] = scores(slot, 0)
    for slot, sc in zip(win_slots, sc_win_first):
        sc0_s[slot] = sc
        sc1_s[slot] = sc

    def body(slots, cap, cap_slots, triangular, c, carry):
        _flash_update_pipelined(state, bufs, slots, c, lambda slot: scores(slot, c + 1),
                                lambda slot: values(slot, c), cap, cap_slots, triangular)
        return carry

    both_slots = sel_slots + win_slots
    edge_shape = (1, _half_lanes(tq, Q_PER_KV, upper=False))
    diag_shape = (0, _half_lanes(tq, Q_PER_KV, upper=True))
    lax.fori_loop(0, first_win, functools.partial(body, sel_slots, None, None, None), 0)
    lax.fori_loop(first_win, lo, functools.partial(body, both_slots, edge_cap, win_slots, edge_shape), 0)
    lax.fori_loop(lo, i, functools.partial(body, both_slots, None, None, None), 0)
    _flash_update_pipelined(state, bufs, both_slots, i, None, lambda slot: values(slot, i), diag_cap,
                            triangular=diag_shape)

    for r in rows:
        gates_t = _sigmoid(gate_ref[r]).T
        outs = []
        for g in groups:
            def gate_row(branch):
                parts = [gates_t[branch * N_HEADS + h:branch * N_HEADS + h + 1, :]
                         for h in range(Q_PER_KV * g, Q_PER_KV * (g + 1))]
                return jnp.concatenate(parts, axis=1)

            n = KV_HEADS * r + g
            o_slc = _flash_result(state, sel_slots[n])
            o_win = _flash_result(state, win_slots[n])
            o = gate_row(0) * o_cmp[n] + gate_row(1) * o_slc + gate_row(2) * o_win
            outs.append(_regroup_heads(o, tq))
        y = jnp.concatenate(outs, axis=0).T * _silu(z_ref[r])
        o_ref[r] = y.astype(o_ref.dtype)


def _nsa(proj, kcn, vc, gq, gks, gkw, *, tq=256, rows_per_step=2):
    bsz, s, _ = proj.shape
    hd = HEAD_DIM
    gw = GROUP_WIDTH
    ncmp = kcn.shape[1]
    rps = rows_per_step
    chains = 2 * KV_HEADS * rps
    assert SEL_FEAT0 + s // NSA_SEL_BLOCK <= hd and ncmp == LANES and bsz % rps == 0
    wide = lambda c0: pl.BlockSpec((rps, tq, gw), lambda b, i: (b, i, c0 // 4))
    full = lambda c0: pl.BlockSpec((rps, s, LANES), lambda b, i: (b, 0, c0))
    cmp = pl.BlockSpec((rps, ncmp, LANES), lambda b, i: (b, 0, 0))
    row = pl.BlockSpec((1, hd), lambda b, i: (0, 0))
    ka = pltpu.VMEM((KV_HEADS * rps, s, LANES), BF16)
    vt = pltpu.VMEM((rps, s // tq, LANES, tq), BF16)
    return pl.pallas_call(
        _nsa_kernel,
        grid=(bsz // rps, s // tq),
        in_specs=[pl.BlockSpec((N_HEADS, 2 * SLOPE_PIECES, tq), lambda b, i: (0, 0, 0)),
                  wide(ODD_QC), cmp, cmp, full(ODD_KS), full(ODD_VS), full(ODD_KW), full(ODD_VW),
                  pl.BlockSpec((rps, tq, LANES), lambda b, i: (b, i, ODD_GC)), wide(ODD_ZC),
                  pl.BlockSpec((hd, 1), lambda b, i: (0, 0)), row, row],
        out_specs=pl.BlockSpec((rps, tq, gw), lambda b, i: (b, i, 0)),
        out_shape=jax.ShapeDtypeStruct((bsz, s, gw), BF16),
        scratch_shapes=[pltpu.VMEM((KV_HEADS * rps, ncmp, LANES), BF16), pltpu.VMEM((rps, LANES, ncmp), BF16),
                        ka, vt, ka, vt] + _flash_scratch(chains, Q_PER_KV * tq)
                       + [pltpu.VMEM((chains, tq, Q_PER_KV * tq), F32)] * 2,
        name="nsa",
    )(jnp.asarray(_alibi_query_features(tq)), proj, kcn, vc, proj, proj, proj, proj, proj, proj,
      gq.reshape(hd, 1), gks.reshape(1, hd), gkw.reshape(1, hd))


def _swa_kernel(sinks_ref, alibi_ref, q_ref, k_ref, v_ref, z_ref, gq_ref, gk_ref, o_ref, ka_s, vt_s):
    step = pl.program_id(1)
    tq = SWA_WINDOW
    tiles = q_ref.shape[1] // tq
    s = k_ref.shape[1]

    @pl.when(step == 0)
    def _():
        _augment_keys(k_ref[0], gk_ref, ka_s)
        _store_chunked_t(v_ref[0], vt_s, tq)

    q_t = q_ref[0].T
    diag_cap = _causal_cap(tq, tq, Q_PER_KV)
    edge_cap = _window_edge_cap(tq, tq, Q_PER_KV)
    chains = [(u, g) for u in range(tiles) for g in range(KV_HEADS)]
    qas = [_augment_query(*_group_query(q_t[:, u * tq:(u + 1) * tq], alibi_ref, g, gq_ref[...]), flipped=g == 1)
           for u, g in chains]
    first = [_window_first_chunk(ka_s, vt_s, g, step * tiles + u - 1, qa, tq, edge_cap)
             for (u, g), qa in zip(chains, qas)]
    own = [_chunk_scores(ka_s, vt_s, g, step * tiles + u, qa, tq) for (u, g), qa in zip(chains, qas)]
    carries = _flash_steps([_flash_init(Q_PER_KV * tq) for _ in chains], first)
    carries = _flash_steps(carries, own, diag_cap)
    outs = []
    for (u, g), carry in zip(chains, carries):
        m, acc, l = _flash_split(carry)
        tpos = ((step * tiles + u) * tq + _iota((1, tq), 1)).astype(F32)
        sink = jnp.concatenate([sinks_ref[h] * LOG2E + (SLOPES[h] * LOG2E) * tpos
                                for h in range(Q_PER_KV * g, Q_PER_KV * (g + 1))], axis=1)
        mf = jnp.maximum(m, sink)
        alpha = jnp.exp2(m - mf)
        o = acc * alpha / jnp.maximum(l * alpha + jnp.exp2(sink - mf), TINY)
        outs.append(_regroup_heads(o, tq))
    o_t = jnp.concatenate([jnp.concatenate(outs[u * KV_HEADS:(u + 1) * KV_HEADS], axis=0) for u in range(tiles)],
                          axis=1)
    y = o_t.T * _silu(z_ref[0])
    o_ref[0] = y.astype(o_ref.dtype)


def _swa(proj, sinks, gq, gk, *, tiles_per_step=8):
    bsz, s, _ = proj.shape
    hd = HEAD_DIM
    gw = GROUP_WIDTH
    tq = SWA_WINDOW
    rows = tiles_per_step * tq
    wide = lambda c0: pl.BlockSpec((1, rows, gw), lambda b, i: (b, i, c0 // 4))
    full = lambda c0: pl.BlockSpec((1, s, LANES), lambda b, i: (b, 0, c0))
    return pl.pallas_call(
        _swa_kernel,
        grid=(bsz, s // rows),
        in_specs=[pl.BlockSpec(memory_space=pltpu.SMEM),
                  pl.BlockSpec((N_HEADS, 2 * SLOPE_PIECES, tq), lambda b, i: (0, 0, 0)),
                  wide(ODD_QD), full(ODD_KD), full(ODD_VD), wide(ODD_ZD),
                  pl.BlockSpec((hd, 1), lambda b, i: (0, 0)), pl.BlockSpec((1, hd), lambda b, i: (0, 0))],
        out_specs=pl.BlockSpec((1, rows, gw), lambda b, i: (b, i, 0)),
        out_shape=jax.ShapeDtypeStruct((bsz, s, gw), BF16),
        scratch_shapes=[pltpu.VMEM((KV_HEADS, s, LANES), BF16), pltpu.VMEM((s // tq, LANES, tq), BF16)],
        name="swa",
    )(sinks.astype(F32), jnp.asarray(_alibi_query_features(tq)), proj, proj, proj, proj,
      gq.reshape(hd, 1), gk.reshape(1, hd))


def _odd_weight_moves():
    gw, kw, ng = GROUP_WIDTH, KV_HEADS * HEAD_DIM, 3 * N_HEADS
    sizes = [gw, kw, kw, kw, kw, kw, kw, ng, gw, gw, kw, kw, gw]
    starts = np.concatenate([[0], np.cumsum(sizes)]).tolist()
    order = [0, 8, 9, 12, 1, 2, 3, 4, 5, 6, 10, 11, 7]
    moves, dst = [], 0
    for k in order:
        moves.append((starts[k], sizes[k], dst))
        dst += sizes[k]
    assert dst + LANES - ng == ODD_COLS
    return moves


def kernel(x, norm_g, w_out, e_w_in, a_conv_w, a_conv_b, a_ln_g, a_ln_b, b_qnorm_g, b_knorm_g, o_w_in, c_qnorm_g, c_knorm_cmp_g, c_knorm_slc_g, c_knorm_win_g, c_pos_k, c_pos_v, c_k_w1, c_k_b1, c_k_w2, c_k_b2, c_v_w1, c_v_b1, c_v_w2, c_v_b2, d_qnorm_g, d_knorm_g, d_sinks):
    bsz, s, d = x.shape
    m = bsz * s
    assert s % MOBA_BLOCK == 0 and d == 2 * GROUP_WIDTH
    x2 = x.reshape(m, d)

    y_a, z_b, q_t, ka, vt, km = _even_layer_front(
        x2, norm_g[0], e_w_in[0].astype(BF16), b_qnorm_g[0], b_knorm_g[0],
        a_conv_w[0], a_conv_b[0], a_ln_g[0], a_ln_b[0], bsz=bsz)
    y_b = _moba(z_b.reshape(bsz, s, -1), q_t, ka, vt, km)

    x2, proj, k_cmp, v_cmp = _layer_boundary(
        x2, y_a, y_b.reshape(m, -1), w_out[0].astype(BF16), norm_g[1], o_w_in[0].astype(BF16),
        e=ODD_COLS, moves=_odd_weight_moves(), chunk=ODD_COLS,
        copies=((ODD_KC * LANES, LANES), (ODD_VC * LANES, LANES)), group=NSA_CMP_STRIDE)
    proj = proj.reshape(bsz, s, ODD_COLS)
    rows16 = lambda t: t.reshape(bsz, s // NSA_CMP_STRIDE, NSA_CMP_STRIDE * LANES)
    kcn, vc = _compress(rows16(k_cmp), rows16(v_cmp), c_pos_k[0], c_pos_v[0], c_k_w1[0], c_k_b1[0],
                        c_k_w2[0], c_k_b2[0], c_v_w1[0], c_v_b1[0], c_v_w2[0], c_v_b2[0], c_knorm_cmp_g[0])
    y_c = _nsa(proj, kcn, vc, c_qnorm_g[0], c_knorm_slc_g[0], c_knorm_win_g[0])
    y_d = _swa(proj, d_sinks[0], d_qnorm_g[0], d_knorm_g[0])
    x2 = _outproj(x2, y_c.reshape(m, -1), y_d.reshape(m, -1), w_out[1].astype(BF16))
    return x2.reshape(bsz, s, d)
```

```python
import functools

import ml_dtypes
import numpy as np
import jax
import jax.numpy as jnp
from jax import lax
from jax.experimental import pallas as pl
from jax.experimental.pallas import tpu as pltpu

HEAD_DIM = 64
N_HEADS = 8
GROUP_WIDTH = N_HEADS * HEAD_DIM
CONV_WIDTH = 31
MOBA_BLOCK = 256
MOBA_TOPK = 3
KV_HEADS = 2
Q_PER_KV = N_HEADS // KV_HEADS
NSA_CMP_LEN = 32
NSA_CMP_STRIDE = 16
NSA_CMP_HIDDEN = 256
NSA_SEL_BLOCK = 64
NSA_TOPN = 16
NSA_WINDOW = 512
NSA_FORCE = 1e4
SWA_WINDOW = 128
EPS = 1e-6
NEG = -1e30
TINY = 1e-30
LANES = 128
SUBLANES = 8
CONV_HALO = 32

LOG2E = float(np.log2(np.e))
BIG = 2.0 ** 99
M_INIT = -1e38
POS_SPLIT_SHIFT = 8
SLOPE_PIECES = 4
SEL_FEAT0 = 2 * SLOPE_PIECES

F32 = jnp.float32
BF16 = jnp.bfloat16

ODD_QC, ODD_ZC, ODD_QD, ODD_ZD = 0, 4, 8, 12
ODD_KC, ODD_VC, ODD_KS, ODD_VS, ODD_KW, ODD_VW, ODD_KD, ODD_VD, ODD_GC = 16, 17, 18, 19, 20, 21, 22, 23, 24
ODD_COLS = 25 * LANES


def _alibi_slopes(n):
    return [float(2.0 ** (-8.0 * (i + 1) / n)) for i in range(n)]


SLOPES = _alibi_slopes(N_HEADS)


def _alibi_query_features(width):
    table = np.zeros((N_HEADS, 2 * SLOPE_PIECES, width), np.float32)
    for h, slope in enumerate(SLOPES):
        rest = np.float64(slope) * LOG2E
        for k in range(SLOPE_PIECES):
            piece = float(np.float32(rest).astype(ml_dtypes.bfloat16).astype(np.float32))
            table[h, k] = piece
            table[h, SLOPE_PIECES + k] = piece
            rest -= piece
    return table


def _dot(a, b):
    return jnp.dot(a, b, preferred_element_type=F32)


def _split_bf16(x, pieces):
    out = []
    for _ in range(pieces):
        out.append(x.astype(BF16))
        x = x - out[-1].astype(F32)
    return out


def _dot_fine(a, b, a_is_bf16_exact=False):
    if a_is_bf16_exact:
        return sum(_dot(a.astype(BF16), piece) for piece in _split_bf16(b, 3))
    (a_hi, a_lo), (b_hi, b_lo) = _split_bf16(a, 2), _split_bf16(b, 2)
    return _dot(a_hi, b_hi) + (_dot(a_hi, b_lo) + _dot(a_lo, b_hi))


def _sigmoid(x):
    return 1.0 / (1.0 + jnp.exp(-x))


def _silu(x):
    return x * _sigmoid(x)


def _rms(x, g):
    return x * lax.rsqrt(jnp.mean(x * x, axis=-1, keepdims=True) + EPS) * g


def _iota(shape, dim):
    return lax.broadcasted_iota(jnp.int32, shape, dim)


def _layer_boundary_kernel(x_ref, ya_ref, yb_ref, wo_ref, g_ref, w_ref, x_out_ref, o_ref, *rest,
                           chunk, copies, group, moves):
    copy_refs, stage_refs, wp_s = rest[:len(copies)], rest[len(copies):-1], rest[-1]

    @pl.when(pl.program_id(0) == 0)
    def _():
        wp_s[...] = jnp.zeros(wp_s.shape, wp_s.dtype)
        for src, width, dst in moves:
            wp_s[:, dst:dst + width] = w_ref[:, src:src + width]

    x = _residual_update(x_ref, ya_ref, yb_ref, wo_ref)
    x_out_ref[...] = x
    h = _rms(x, g_ref[...]).astype(BF16)
    for c in range(o_ref.shape[1] // chunk):
        o_ref[:, c * chunk:(c + 1) * chunk] = _dot(h, wp_s[:, c * chunk:(c + 1) * chunk])
    for ref, stage, (start, width) in zip(copy_refs, stage_refs, copies):
        stage[...] = o_ref[:, start:start + width]
        for t in range(group):
            ref[:, t * width:(t + 1) * width] = stage[pl.ds(t, ref.shape[0], stride=group), :]


def _layer_boundary(x2d, ya, yb, w_out, g, w, *, e, moves, tm=512, chunk, copies=(), group=1):
    m, d = x2d.shape
    gw = ya.shape[1]
    rows = lambda n, width: pl.BlockSpec((n, width), lambda i: (i, 0))
    const = lambda shape: pl.BlockSpec(shape, lambda i: (0, 0))
    return pl.pallas_call(
        functools.partial(_layer_boundary_kernel, chunk=chunk, copies=tuple(copies), group=group,
                          moves=tuple(moves)),
        grid=(m // tm,),
        in_specs=[rows(tm, d), rows(tm, gw), rows(tm, gw), const(w_out.shape), const((1, d)), const(w.shape)],
        out_specs=[rows(tm, d), rows(tm, e)] + [rows(tm // group, group * width) for _, width in copies],
        out_shape=[jax.ShapeDtypeStruct((m, d), F32), jax.ShapeDtypeStruct((m, e), F32)]
                  + [jax.ShapeDtypeStruct((m // group, group * width), F32) for _, width in copies],
        scratch_shapes=[pltpu.VMEM((tm, width), F32) for _, width in copies] + [pltpu.VMEM((d, e), BF16)],
        name="layer_boundary",
    )(x2d, ya, yb, w_out, g.reshape(1, d), w)


def _conv_rows(h_s, row0, rows, w_ref, b_ref, lg_ref, lb_ref):
    base = CONV_HALO - (CONV_WIDTH - 1)
    acc = None
    for b in range(SUBLANES):
        n = rows if b == 0 else rows + SUBLANES
        part = None
        for a in range((base + CONV_WIDTH - 1) // SUBLANES + 1):
            j = SUBLANES * a + b - base
            if 0 <= j < CONV_WIDTH:
                term = w_ref[j:j + 1, :] * h_s[pl.ds(row0 + SUBLANES * a, n), :]
                part = term if part is None else part + term
        part = part[b:b + rows]
        acc = part if acc is None else acc + part
    y = acc + b_ref[...]
    mu = jnp.mean(y, axis=-1, keepdims=True)
    yc = y - mu
    return _silu(yc * lax.rsqrt(jnp.mean(yc * yc, axis=-1, keepdims=True) + EPS) * lg_ref[...] + lb_ref[...])


def _even_layer_kernel(x_ref, g_ref, w_ref, gq_ref, gk_ref, cw_ref, cb_ref, lg_ref, lb_ref,
                       ya_ref, zb_ref, qt_ref, ka_ref, vt_ref, km_ref, hb_s, conv_s, moba_s, h_s, tail_s,
                       *, seq, rows):
    tm = x_ref.shape[0]
    hd, gw = HEAD_DIM, GROUP_WIDTH
    tile_in_seq = pl.program_id(0) % (seq // tm)
    hb_s[...] = _rms(x_ref[...], g_ref[...]).astype(BF16)
    conv_s[...] = _dot(hb_s[...], w_ref[:, 0:3 * gw])
    h_s[0:CONV_HALO, :] = jnp.where(tile_in_seq > 0, tail_s[...], 0.0)
    h_s[CONV_HALO:, :] = conv_s[:, 0:gw] * _sigmoid(conv_s[:, gw:2 * gw])
    tail_s[...] = h_s[tm:tm + CONV_HALO, :]
    for c in range(tm // rows):
        y = _conv_rows(h_s, c * rows, rows, cw_ref, cb_ref, lg_ref, lb_ref)
        gate = _silu(conv_s[c * rows:(c + 1) * rows, 2 * gw:3 * gw])
        ya_ref[c * rows:(c + 1) * rows, :] = (y * gate).astype(ya_ref.dtype)

    moba_s[...] = _dot(hb_s[...], w_ref[:, 3 * gw:7 * gw])
    q_t = moba_s[:, 0:gw].T
    for j in range(N_HEADS):
        qt_ref[0, j * hd:(j + 1) * hd, :] = _norm_query_t(q_t[j * hd:(j + 1) * hd], gq_ref[...])
    pos = tile_in_seq * tm + _iota((tm, LANES), 0)
    for pair in range(N_HEADS // 2):
        kn = _pair_rms(moba_s[:, gw + pair * LANES:gw + (pair + 1) * LANES], gk_ref[...])
        ka_ref[0, 2 * pair], ka_ref[0, 2 * pair + 1] = _augment_pair_keys(kn, pos, MOBA_BLOCK.bit_length() - 1)
        km_ref[0, 0, :, pair * LANES:(pair + 1) * LANES] = jnp.mean(
            kn.reshape(tm // MOBA_BLOCK, MOBA_BLOCK, LANES), axis=1)
    v_t = moba_s[:, 2 * gw:3 * gw].T.astype(BF16)
    for c in range(tm // MOBA_BLOCK):
        vt_ref[0, c] = v_t[:, c * MOBA_BLOCK:(c + 1) * MOBA_BLOCK]
    zb_ref[...] = moba_s[:, 3 * gw:4 * gw]


def _even_layer_front(x2d, g, w, gq, gk, conv_w, conv_b, ln_g, ln_b, *, bsz, tm=512, rows=512):
    m, d = x2d.shape
    e = w.shape[1]
    s = m // bsz
    per_b = s // tm
    hd, gw, nb = HEAD_DIM, GROUP_WIDTH, tm // MOBA_BLOCK
    assert s % tm == 0 and tm % MOBA_BLOCK == 0 and e == 7 * gw and tm % rows == 0
    const = lambda shape: pl.BlockSpec(shape, lambda i: (0, 0))
    tile = pl.BlockSpec((tm, gw), lambda i: (i, 0))
    vec = const((1, gw))
    return pl.pallas_call(
        functools.partial(_even_layer_kernel, seq=s, rows=rows),
        grid=(m // tm,),
        in_specs=[pl.BlockSpec((tm, d), lambda i: (i, 0)), const((1, d)), const((d, e)), const((hd, 1)),
                  const((1, hd)), const((CONV_WIDTH, gw)), vec, vec, vec],
        out_specs=[tile, tile,
                   pl.BlockSpec((1, gw, tm), lambda i: (i // per_b, 0, i % per_b)),
                   pl.BlockSpec((1, N_HEADS, tm, LANES), lambda i: (i // per_b, 0, i % per_b, 0)),
                   pl.BlockSpec((1, nb, gw, MOBA_BLOCK), lambda i: (i // per_b, i % per_b, 0, 0)),
                   pl.BlockSpec((1, 1, nb, gw), lambda i: (i // per_b, i % per_b, 0, 0))],
        out_shape=[jax.ShapeDtypeStruct((m, gw), BF16),
                   jax.ShapeDtypeStruct((m, gw), F32),
                   jax.ShapeDtypeStruct((bsz, gw, s), F32),
                   jax.ShapeDtypeStruct((bsz, N_HEADS, s, LANES), BF16),
                   jax.ShapeDtypeStruct((bsz, s // MOBA_BLOCK, gw, MOBA_BLOCK), BF16),
                   jax.ShapeDtypeStruct((bsz, per_b, nb, gw), F32)],
        scratch_shapes=[pltpu.VMEM((tm, d), BF16), pltpu.VMEM((tm, 3 * gw), F32), pltpu.VMEM((tm, 4 * gw), F32),
                        pltpu.VMEM((CONV_HALO + tm, gw), F32), pltpu.VMEM((CONV_HALO, gw), F32)],
        name="even_layer_front",
    )(x2d, g.reshape(1, d), w, gq.reshape(hd, 1), gk.reshape(1, hd), conv_w, conv_b.reshape(1, gw),
      ln_g.reshape(1, gw), ln_b.reshape(1, gw))


def _residual_update(x_ref, ya_ref, yb_ref, w_ref):
    return x_ref[...] + _dot(jnp.concatenate([ya_ref[...], yb_ref[...]], axis=1), w_ref[...])


def _outproj_kernel(x_ref, ya_ref, yb_ref, w_ref, o_ref):
    o_ref[...] = _residual_update(x_ref, ya_ref, yb_ref, w_ref)


def _outproj(x2d, ya, yb, w, *, tm=1024):
    m, d = x2d.shape
    gw = ya.shape[1]
    return pl.pallas_call(
        _outproj_kernel,
        grid=(m // tm,),
        in_specs=[pl.BlockSpec((tm, d), lambda i: (i, 0)),
                  pl.BlockSpec((tm, gw), lambda i: (i, 0)),
                  pl.BlockSpec((tm, gw), lambda i: (i, 0)),
                  pl.BlockSpec(w.shape, lambda i: (0, 0))],
        out_specs=pl.BlockSpec((tm, d), lambda i: (i, 0)),
        out_shape=jax.ShapeDtypeStruct((m, d), F32),
        name="outproj",
    )(x2d, ya, yb, w)


def _key_features(pos, flipped, sel_shift=None):
    col = _iota(pos.shape, 1) - (0 if flipped else HEAD_DIM)
    hi = (pos >> POS_SPLIT_SHIFT) << POS_SPLIT_SHIFT
    lo = pos & ((1 << POS_SPLIT_SHIFT) - 1)
    feat = jnp.where(col < SLOPE_PIECES, hi, jnp.where(col < 2 * SLOPE_PIECES, lo, 0))
    if sel_shift is not None:
        feat = jnp.where(col - SEL_FEAT0 == (pos >> sel_shift), 1, feat)
    return jnp.where((col >= 0) & (col < HEAD_DIM), feat, 0).astype(F32)


def _pair_rms(x, g):
    sq = x * x
    hi = sq.astype(BF16)
    lo = (sq - hi.astype(F32)).astype(BF16)
    head_shift = HEAD_DIM.bit_length() - 1
    same_head = (_iota((LANES, LANES), 0) >> head_shift) == (_iota((LANES, LANES), 1) >> head_shift)
    ones = jnp.where(same_head, 1.0, 0.0).astype(BF16)
    ss = _dot(hi, ones) + _dot(lo, ones)
    return x * lax.rsqrt(ss * (1.0 / HEAD_DIM) + EPS) * jnp.concatenate([g, g], axis=1)


def _augment_pair_keys(kn, pos, sel_shift=None):
    lane = _iota(kn.shape, 1)
    return [jnp.where(lane >= HEAD_DIM if flipped else lane < HEAD_DIM, kn,
                      _key_features(pos, flipped, sel_shift)).astype(BF16) for flipped in (False, True)]


def _norm_query_t(x, g_col):
    ss = jnp.mean(x * x, axis=0, keepdims=True)
    return x * lax.rsqrt(ss + EPS) * g_col * (HEAD_DIM ** -0.5 * LOG2E)


def _augment_query(q_t, alibi, sel_bias=None, flipped=False):
    n = q_t.shape[1]
    feats = [alibi]
    used = alibi.shape[0]
    if sel_bias is not None:
        feats.append(sel_bias)
        used += sel_bias.shape[0]
    feats.append(jnp.zeros((HEAD_DIM - used, n), F32))
    return jnp.concatenate(feats + [q_t] if flipped else [q_t] + feats, axis=0).astype(BF16)


ONES_ROWS = 16
ACC_ROWS = HEAD_DIM + ONES_ROWS


def _flash_init(n):
    return jnp.full((1, n), M_INIT, F32), jnp.zeros((ACC_ROWS, n), F32)


def _flash_step(carry, sc, v_t):
    m, acc = carry
    m_new = jnp.maximum(m, jnp.max(sc, axis=0, keepdims=True))
    p = jnp.exp2(sc - m_new).astype(BF16)
    v_ones = jnp.concatenate([v_t, jnp.ones((ONES_ROWS, v_t.shape[1]), BF16)], axis=0)
    return m_new, jnp.exp2(m - m_new) * acc + _dot(v_ones, p)


def _flash_split(carry):
    m, acc = carry
    return m, acc[:HEAD_DIM], acc[HEAD_DIM:HEAD_DIM + 1]


def _flash_steps(carries, chunks, cap=None):
    return tuple(_flash_step(carry, sc if cap is None else jnp.minimum(sc, cap), v_t)
                 for carry, (sc, v_t) in zip(carries, chunks))


def _flash_reset(state):
    m_ref, acc_ref = state
    m_ref[...] = jnp.full(m_ref.shape, M_INIT, F32)
    acc_ref[...] = jnp.zeros(acc_ref.shape, F32)


def _flash_update(state, slot, sc, v_t, cap=None):
    m_ref, acc_ref = state
    m_ref[slot], acc_ref[slot] = _flash_step((m_ref[slot], acc_ref[slot]),
                                             sc if cap is None else jnp.minimum(sc, cap), v_t)


def _flash_update_triangular(state, slot, sc_ref, v_t, cap, full_half, partial_lanes):
    m_ref, acc_ref = state
    half = sc_ref.shape[0] // 2
    full = slice(full_half * half, (full_half + 1) * half)
    part = slice((1 - full_half) * half, (2 - full_half) * half)
    _flash_update(state, slot, sc_ref[full, :], v_t[:, full], cap[full, :])

    def gather(x):
        return jnp.concatenate([x[..., a:b] for a, b in partial_lanes], axis=-1)

    sc = jnp.minimum(jnp.concatenate([sc_ref[part, a:b] for a, b in partial_lanes], axis=1), gather(cap[part, :]))
    m, acc = _flash_step((gather(m_ref[slot]), gather(acc_ref[slot])), sc, v_t[:, part])
    width = partial_lanes[0][1] - partial_lanes[0][0]
    for k, (a, b) in enumerate(partial_lanes):
        m_ref[slot, :, a:b] = m[:, k * width:(k + 1) * width]
        acc_ref[slot, :, a:b] = acc[:, k * width:(k + 1) * width]


def _half_lanes(tq, reps, upper):
    off = tq // 2 if upper else 0
    return [(k * tq + off, k * tq + off + tq // 2) for k in range(reps)]


def _flash_update_pipelined(state, bufs, slots, c, next_scores, values, cap=None, cap_slots=None,
                            triangular=None):
    def run(src_ref, dst_ref):
        for slot in slots:
            fresh = None if next_scores is None else next_scores(slot)
            masked = cap is not None and (cap_slots is None or slot in cap_slots)
            if masked and triangular is not None:
                _flash_update_triangular(state, slot, src_ref.at[slot], values(slot), cap, *triangular)
            else:
                _flash_update(state, slot, src_ref[slot], values(slot), cap if masked else None)
            if fresh is not None:
                dst_ref[slot] = fresh

    pl.when((c & 1) == 0)(lambda: run(bufs[0], bufs[1]))
    pl.when((c & 1) == 1)(lambda: run(bufs[1], bufs[0]))


def _flash_result(state, slot):
    m_ref, acc_ref = state
    m, out, den = _flash_split((m_ref[slot], acc_ref[slot]))
    return out / jnp.maximum(den, TINY)


def _flash_scratch(chains, n):
    return [pltpu.VMEM((chains, 1, n), F32), pltpu.VMEM((chains, ACC_ROWS, n), F32)]


def _rank_select(score, blk, limit, count):
    ranks = []
    for r0 in range(0, score.shape[0], SUBLANES):
        tile = score[r0:r0 + SUBLANES]
        tile_blk = r0 + _iota(tile.shape, 0)
        rank = jnp.zeros(tile.shape, jnp.int32)
        for mm in range(limit):
            gm = score[mm:mm + 1, :]
            if mm < r0:
                beats = gm >= tile
            elif mm >= r0 + SUBLANES:
                beats = gm > tile
            else:
                beats = (gm > tile) | ((gm == tile) & (mm < tile_blk))
            rank = rank + jnp.where(beats, 1, 0)
        ranks.append(rank)
    return jnp.concatenate(ranks, axis=0) < count


def _tile_lanes(x, reps):
    return jnp.concatenate([x] * reps, axis=1) if reps > 1 else x


def _causal_cap(tk, tq, reps):
    keep = _iota((tk, tq), 0) <= _iota((tk, tq), 1)
    return _tile_lanes(jnp.where(keep, BIG, -BIG), reps)


def _window_edge_cap(tk, tq, reps):
    keep = _iota((tk, tq), 0) > _iota((tk, tq), 1)
    return _tile_lanes(jnp.where(keep, BIG, -BIG), reps)


def _store_chunked_t(x, dst_ref, tk):
    x_t = x.T.astype(BF16)
    for c in range(x.shape[0] // tk):
        dst_ref[c] = x_t[:, c * tk:(c + 1) * tk]


def _moba_kernel(alibi_ref, qt_ref, ka_ref, vt_ref, km_ref, z_ref, o_ref, m_s, acc_s, sc0_s, sc1_s):
    i = pl.program_id(1)
    nblk = vt_ref.shape[1]
    tq = MOBA_BLOCK
    hd = HEAD_DIM
    chains = [(r, j) for r in range(qt_ref.shape[0]) for j in range(N_HEADS)]
    slots = range(len(chains))

    blk = _iota((nblk, tq), 0)
    qas = []
    for r, j in chains:
        qn = qt_ref[r, j * hd:(j + 1) * hd, :]
        km = jnp.concatenate([km_ref[r, t, :, j * hd:(j + 1) * hd] for t in range(km_ref.shape[1])], axis=0)
        gate = jnp.where(blk < i, _dot_fine(km, qn), NEG)
        chosen = _rank_select(gate, blk, nblk, MOBA_TOPK) & (blk < i)
        sel_bias = jnp.where(chosen | (blk == i), 0.0, -BIG)
        qas.append(_augment_query(qn, alibi_ref[j], sel_bias, flipped=j % 2 == 1))

    def scores(slot, n):
        r, j = chains[slot]
        return _dot(ka_ref[r, j, pl.ds(pl.multiple_of(n * tq, tq), tq), :], qas[slot])

    def values(slot, n):
        r, j = chains[slot]
        return vt_ref[r, n, j * hd:(j + 1) * hd, :]

    state = (m_s, acc_s)
    _flash_reset(state)
    bufs = (sc0_s, sc1_s)
    for slot in slots:
        sc0_s[slot] = scores(slot, 0)

    def body(n, carry):
        _flash_update_pipelined(state, bufs, slots, n, lambda slot: scores(slot, n + 1),
                                lambda slot: values(slot, n))
        return carry

    lax.fori_loop(0, i, body, 0)
    diag_cap = _causal_cap(tq, tq, 1)
    _flash_update_pipelined(state, bufs, slots, i, None, lambda slot: values(slot, i), diag_cap,
                            triangular=(0, _half_lanes(tq, 1, upper=True)))
    for r in range(qt_ref.shape[0]):
        outs = [_flash_result(state, r * N_HEADS + j) for j in range(N_HEADS)]
        y = jnp.concatenate(outs, axis=0).T * _silu(z_ref[r])
        o_ref[r] = y.astype(o_ref.dtype)


def _moba(z, q_t, ka, vt, km, *, rows_per_step=2):
    bsz, s, _ = z.shape
    tq = MOBA_BLOCK
    gw = GROUP_WIDTH
    nblk = s // MOBA_BLOCK
    rps = rows_per_step
    assert SEL_FEAT0 + nblk <= HEAD_DIM and bsz % rps == 0
    whole = lambda a: pl.BlockSpec((rps,) + a.shape[1:], lambda b, i: (b,) + (0,) * (a.ndim - 1))
    return pl.pallas_call(
        _moba_kernel,
        grid=(bsz // rps, s // tq),
        in_specs=[pl.BlockSpec((N_HEADS, 2 * SLOPE_PIECES, tq), lambda b, i: (0, 0, 0)),
                  pl.BlockSpec((rps, gw, tq), lambda b, i: (b, 0, i)), whole(ka), whole(vt), whole(km),
                  pl.BlockSpec((rps, tq, gw), lambda b, i: (b, i, 0))],
        out_specs=pl.BlockSpec((rps, tq, gw), lambda b, i: (b, i, 0)),
        out_shape=jax.ShapeDtypeStruct((bsz, s, gw), BF16),
        scratch_shapes=_flash_scratch(rps * N_HEADS, tq) + [pltpu.VMEM((rps * N_HEADS, tq, tq), F32)] * 2,
        name="moba",
    )(jnp.asarray(_alibi_query_features(tq)), q_t, ka, vt, km, z)


def _cmp_kernel(xk_ref, xv_ref, wk1_ref, wv1_ref, pk_ref, pv_ref, bk1_ref, bv1_ref,
                wk2_ref, wv2_ref, bk2_ref, bv2_ref, gn_ref, ko_ref, vo_ref, wke_s, wve_s):
    half = xk_ref.shape[2]
    n = xk_ref.shape[1]
    hd = HEAD_DIM
    nh = NSA_CMP_HIDDEN

    @pl.when(pl.program_id(0) == 0)
    def _():
        for w_ref, we_ref in ((wk1_ref, wke_s), (wv1_ref, wve_s)):
            we_ref[...] = jnp.zeros(we_ref.shape, BF16)
            for tok in range(NSA_CMP_LEN):
                for g in range(KV_HEADS):
                    row = (tok * KV_HEADS + g) * hd
                    we_ref[row:row + hd, g * nh:(g + 1) * nh] = w_ref[tok * hd:(tok + 1) * hd, :]

    rows = xk_ref.shape[0]

    def hidden(x_ref, we_ref, pos_ref, b1_ref):
        x = x_ref[...].reshape(rows * n, half).astype(BF16)
        first = _dot(x, we_ref[:half, :])
        second = _dot(x, we_ref[half:, :])
        second = jnp.concatenate([pltpu.roll(second[r * n:(r + 1) * n], n - 1, 0) for r in range(rows)], axis=0)
        pos = jnp.broadcast_to(pos_ref[...], (8, 2 * half)).astype(BF16)
        ph = _dot(pos[:, :half], we_ref[:half, :]) + _dot(pos[:, half:], we_ref[half:, :])
        return _silu(first + second + ph[0:1, :] + b1_ref[...])

    hk = hidden(xk_ref, wke_s, pk_ref, bk1_ref).astype(BF16)
    hv = hidden(xv_ref, wve_s, pv_ref, bv1_ref).astype(BF16)
    ks, vs = [], []
    for g in range(KV_HEADS):
        ks.append(_rms(_dot(hk[:, g * nh:(g + 1) * nh], wk2_ref[...]) + bk2_ref[...], gn_ref[...]))
        vs.append(_dot(hv[:, g * nh:(g + 1) * nh], wv2_ref[...]) + bv2_ref[...])
    ko_ref[...] = jnp.concatenate(ks, axis=1).reshape(ko_ref.shape)
    vo_ref[...] = jnp.concatenate(vs, axis=1).reshape(vo_ref.shape)


def _compress(xk, xv, pos_k, pos_v, kw1, kb1, kw2, kb2, vw1, vb1, vw2, vb2, gn, *, rows_per_step=4):
    bsz, n, half = xk.shape
    hd = HEAD_DIM
    rps = rows_per_step
    assert bsz % rps == 0
    tile_pos = lambda p: jnp.tile(p[:, None, :], (1, KV_HEADS, 1)).reshape(1, 2 * half)
    tile_b = lambda b: jnp.tile(b.reshape(1, -1), (1, KV_HEADS))
    const = lambda shape: pl.BlockSpec(shape, lambda b: (0,) * len(shape))
    xs = pl.BlockSpec((rps, n, half), lambda b: (b, 0, 0))
    hw = KV_HEADS * NSA_CMP_HIDDEN
    out = pl.BlockSpec((rps, n, KV_HEADS * hd), lambda b: (b, 0, 0))
    expanded = pltpu.VMEM((2 * half, hw), BF16)
    return pl.pallas_call(
        _cmp_kernel,
        grid=(bsz // rps,),
        in_specs=[xs, xs, const(kw1.shape), const(vw1.shape),
                  const((1, 2 * half)), const((1, 2 * half)), const((1, hw)), const((1, hw)),
                  const((NSA_CMP_HIDDEN, hd)), const((NSA_CMP_HIDDEN, hd)), const((1, hd)), const((1, hd)),
                  const((1, hd))],
        out_specs=[out, out],
        out_shape=[jax.ShapeDtypeStruct((bsz, n, KV_HEADS * hd), F32)] * 2,
        scratch_shapes=[expanded, expanded],
        name="nsa_compress",
    )(xk, xv, kw1.astype(BF16), vw1.astype(BF16), tile_pos(pos_k), tile_pos(pos_v), tile_b(kb1), tile_b(vb1),
      kw2.astype(BF16), vw2.astype(BF16), kb2.reshape(1, hd), vb2.reshape(1, hd), gn.reshape(1, hd))


def _group_query(q_t, alibi_ref, g, gq_col):
    hd = HEAD_DIM
    heads = range(Q_PER_KV * g, Q_PER_KV * (g + 1))
    qn = jnp.concatenate([_norm_query_t(q_t[h * hd:(h + 1) * hd], gq_col) for h in heads], axis=1)
    alibi = jnp.concatenate([alibi_ref[h] for h in heads], axis=1)
    return qn, alibi


def _augment_keys(k, gk_ref, dst_ref, row=0, sel_shift=None):
    assert KV_HEADS == 2
    pos = _iota(k.shape, 0)
    dst_ref[2 * row], dst_ref[2 * row + 1] = _augment_pair_keys(_pair_rms(k, gk_ref[...]), pos, sel_shift)


def _chunk_scores(ka_ref, vt_ref, g, c, qa, tq):
    off = pl.multiple_of(c * tq, tq)
    return _dot(ka_ref[g, pl.ds(off, tq), :], qa), vt_ref[c, g * HEAD_DIM:(g + 1) * HEAD_DIM, :]


def _window_first_chunk(ka_ref, vt_ref, g, far, qa, tq, edge_cap):
    sc, v_t = _chunk_scores(ka_ref, vt_ref, g, jnp.maximum(far, 0), qa, tq)
    return jnp.minimum(sc, jnp.minimum(edge_cap, jnp.where(far >= 0, BIG, -BIG))), v_t


def _regroup_heads(o_t, tq):
    return jnp.concatenate([o_t[:, k * tq:(k + 1) * tq] for k in range(Q_PER_KV)], axis=0)


def _nsa_kernel(alibi_ref, q_ref, kc_ref, vc_ref, ks_ref, vs_ref, kw_ref, vw_ref, gate_ref, z_ref,
                gq_ref, gks_ref, gkw_ref, o_ref, kca_s, vct_s, ksa_s, vst_s, kwa_s, vwt_s, m_s, acc_s,
                sc0_s, sc1_s):
    i = pl.program_id(1)
    rows = range(q_ref.shape[0])
    tq = q_ref.shape[1]
    s = ks_ref.shape[1]
    hd = HEAD_DIM
    ncmp = kc_ref.shape[1]
    nsb = s // NSA_SEL_BLOCK

    @pl.when(i == 0)
    def _():
        cend = _iota((ncmp, LANES), 0) * NSA_CMP_STRIDE + (NSA_CMP_LEN - 1)
        for r in rows:
            _augment_keys(ks_ref[r], gks_ref, ksa_s, r, NSA_SEL_BLOCK.bit_length() - 1)
            _augment_keys(kw_ref[r], gkw_ref, kwa_s, r)
            _store_chunked_t(vs_ref[r], vst_s.at[r], tq)
            _store_chunked_t(vw_ref[r], vwt_s.at[r], tq)
            kca_s[2 * r], kca_s[2 * r + 1] = _augment_pair_keys(kc_ref[r], cend)
            vct_s[r] = vc_ref[r].T.astype(BF16)

    t0 = i * tq
    diag_cap = _causal_cap(tq, tq, Q_PER_KV)
    edge_cap = _window_edge_cap(tq, tq, Q_PER_KV)
    cend =_iota((ncmp, tq), 0) * NSA_CMP_STRIDE + (NSA_CMP_LEN - 1)
    seen = _tile_lanes(cend <= t0 + _iota((ncmp, tq), 1), Q_PER_KV)
    cstart = _iota((nsb, ncmp), 1) * NSA_CMP_STRIDE
    bstart = _iota((nsb, ncmp), 0) * NSA_SEL_BLOCK
    overlap_t = jnp.where((cstart < bstart + NSA_SEL_BLOCK) & (cstart + NSA_CMP_LEN > bstart), 1.0, 0.0)
    blk = _iota((nsb, tq), 0)
    cur = (t0 + _iota((nsb, tq), 1)) >> (NSA_SEL_BLOCK.bit_length() - 1)
    forced = (blk == 0) | (blk == cur) | (blk == cur - 1)
    groups = range(KV_HEADS)
    far = i - NSA_WINDOW // tq
    lo = jnp.maximum(far + 1, 0)
    first_win = jnp.maximum(far, 0)

    state = (m_s, acc_s)
    kv = [(r, g) for r in rows for g in groups]
    sel_slots = tuple(range(len(kv)))
    win_slots = tuple(len(kv) + n for n in range(len(kv)))
    _flash_reset(state)
    q_ts = [q_ref[r].T for r in rows]
    queries = [_group_query(q_ts[r], alibi_ref, g, gq_ref[...]) for r, g in kv]
    qa_win = [_augment_query(qn, alibi, flipped=g == 1) for (r, g), (qn, alibi) in zip(kv, queries)]
    qa_sel = []

    def scores(slot, c):
        n = slot % len(kv)
        ka_ref, qa = (ksa_s, qa_sel) if slot in sel_slots else (kwa_s, qa_win)
        return _dot(ka_ref[n, pl.ds(pl.multiple_of(c * tq, tq), tq), :], qa[n])

    def values(slot, c):
        r, g = kv[slot % len(kv)]
        return (vst_s if slot in sel_slots else vwt_s)[r, c, g * hd:(g + 1) * hd, :]

    cmp_sc = [_dot(kca_s[n], qa_win[n]) for n in range(len(kv))]
    sc_win_first = [scores(slot, first_win) for slot in win_slots]
    o_cmp = []
    for n, (r, g) in enumerate(kv):
        qn, alibi = queries[n]
        sc = jnp.where(seen, cmp_sc[n], -BIG)
        e = jnp.where(seen, jnp.exp2(sc - jnp.max(sc, axis=0, keepdims=True)), 0.0)
        p_c = e / jnp.maximum(jnp.sum(e, axis=0, keepdims=True), TINY)
        o_cmp.append(_dot(vct_s[r, g * hd:(g + 1) * hd, :], p_c.astype(BF16)))

        psum = p_c[:, 0:tq]
        for k in range(1, Q_PER_KV):
            psum = psum + p_c[:, k * tq:(k + 1) * tq]
        imp = _dot_fine(overlap_t, psum, a_is_bf16_exact=True)
        imp = jnp.where(blk <= cur, jnp.where(forced, NSA_FORCE, imp), NEG)
        chosen = _rank_select(imp, blk, nsb, min(NSA_TOPN, nsb))
        sel_bias = _tile_lanes(jnp.where(chosen, 0.0, -BIG), Q_PER_KV)
        qa_sel.append(_augment_query(qn, alibi, sel_bias, flipped=g == 1))

    bufs = (sc0_s, sc1_s)
    for slot in sel_slots:
        sc0_s[slot] = scores(slot, 0)
    for slot, sc in zip(win_slots, sc_win_first):
        sc0_s[slot] = sc
        sc1_s[slot] = sc

    def body(slots, cap, cap_slots, triangular, c, carry):
        _flash_update_pipelined(state, bufs, slots, c, lambda slot: scores(slot, c + 1),
                                lambda slot: values(slot, c), cap, cap_slots, triangular)
        return carry

    both_slots = sel_slots + win_slots
    edge_shape = (1, _half_lanes(tq, Q_PER_KV, upper=False))
    diag_shape = (0, _half_lanes(tq, Q_PER_KV, upper=True))
    lax.fori_loop(0, first_win, functools.partial(body, sel_slots, None, None, None), 0)
    lax.fori_loop(first_win, lo, functools.partial(body, both_slots, edge_cap, win_slots, edge_shape), 0)
    lax.fori_loop(lo, i, functools.partial(body, both_slots, None, None, None), 0)
    _flash_update_pipelined(state, bufs, both_slots, i, None, lambda slot: values(slot, i), diag_cap,
                            triangular=diag_shape)

    for r in rows:
        gates_t = _sigmoid(gate_ref[r]).T
        outs = []
        for g in groups:
            def gate_row(branch):
                parts = [gates_t[branch * N_HEADS + h:branch * N_HEADS + h + 1, :]
                         for h in range(Q_PER_KV * g, Q_PER_KV * (g + 1))]
                return jnp.concatenate(parts, axis=1)

            n = KV_HEADS * r + g
            o_slc = _flash_result(state, sel_slots[n])
            o_win = _flash_result(state, win_slots[n])
            o = gate_row(0) * o_cmp[n] + gate_row(1) * o_slc + gate_row(2) * o_win
            outs.append(_regroup_heads(o, tq))
        y = jnp.concatenate(outs, axis=0).T * _silu(z_ref[r])
        o_ref[r] = y.astype(o_ref.dtype)


def _nsa(proj, kcn, vc, gq, gks, gkw, *, tq=256, rows_per_step=2):
    bsz, s, _ = proj.shape
    hd = HEAD_DIM
    gw = GROUP_WIDTH
    ncmp = kcn.shape[1]
    rps = rows_per_step
    chains = 2 * KV_HEADS * rps
    assert SEL_FEAT0 + s // NSA_SEL_BLOCK <= hd and ncmp == LANES and bsz % rps == 0
    wide = lambda c0: pl.BlockSpec((rps, tq, gw), lambda b, i: (b, i, c0 // 4))
    full = lambda c0: pl.BlockSpec((rps, s, LANES), lambda b, i: (b, 0, c0))
    cmp = pl.BlockSpec((rps, ncmp, LANES), lambda b, i: (b, 0, 0))
    row = pl.BlockSpec((1, hd), lambda b, i: (0, 0))
    ka = pltpu.VMEM((KV_HEADS * rps, s, LANES), BF16)
    vt = pltpu.VMEM((rps, s // tq, LANES, tq), BF16)
    return pl.pallas_call(
        _nsa_kernel,
        grid=(bsz // rps, s // tq),
        in_specs=[pl.BlockSpec((N_HEADS, 2 * SLOPE_PIECES, tq), lambda b, i: (0, 0, 0)),
                  wide(ODD_QC), cmp, cmp, full(ODD_KS), full(ODD_VS), full(ODD_KW), full(ODD_VW),
                  pl.BlockSpec((rps, tq, LANES), lambda b, i: (b, i, ODD_GC)), wide(ODD_ZC),
                  pl.BlockSpec((hd, 1), lambda b, i: (0, 0)), row, row],
        out_specs=pl.BlockSpec((rps, tq, gw), lambda b, i: (b, i, 0)),
        out_shape=jax.ShapeDtypeStruct((bsz, s, gw), BF16),
        scratch_shapes=[pltpu.VMEM((KV_HEADS * rps, ncmp, LANES), BF16), pltpu.VMEM((rps, LANES, ncmp), BF16),
                        ka, vt, ka, vt] + _flash_scratch(chains, Q_PER_KV * tq)
                       + [pltpu.VMEM((chains, tq, Q_PER_KV * tq), F32)] * 2,
        name="nsa",
    )(jnp.asarray(_alibi_query_features(tq)), proj, kcn, vc, proj, proj, proj, proj, proj, proj,
      gq.reshape(hd, 1), gks.reshape(1, hd), gkw.reshape(1, hd))


def _swa_kernel(sinks_ref, alibi_ref, q_ref, k_ref, v_ref, z_ref, gq_ref, gk_ref, o_ref, ka_s, vt_s):
    step = pl.program_id(1)
    tq = SWA_WINDOW
    tiles = q_ref.shape[1] // tq
    s = k_ref.shape[1]

    @pl.when(step == 0)
    def _():
        _augment_keys(k_ref[0], gk_ref, ka_s)
        _store_chunked_t(v_ref[0], vt_s, tq)

    q_t = q_ref[0].T
    diag_cap = _causal_cap(tq, tq, Q_PER_KV)
    edge_cap = _window_edge_cap(tq, tq, Q_PER_KV)
    chains = [(u, g) for u in range(tiles) for g in range(KV_HEADS)]
    qas = [_augment_query(*_group_query(q_t[:, u * tq:(u + 1) * tq], alibi_ref, g, gq_ref[...]), flipped=g == 1)
           for u, g in chains]
    first = [_window_first_chunk(ka_s, vt_s, g, step * tiles + u - 1, qa, tq, edge_cap)
             for (u, g), qa in zip(chains, qas)]
    own = [_chunk_scores(ka_s, vt_s, g, step * tiles + u, qa, tq) for (u, g), qa in zip(chains, qas)]
    carries = _flash_steps([_flash_init(Q_PER_KV * tq) for _ in chains], first)
    carries = _flash_steps(carries, own, diag_cap)
    outs = []
    for (u, g), carry in zip(chains, carries):
        m, acc, l = _flash_split(carry)
        tpos = ((step * tiles + u) * tq + _iota((1, tq), 1)).astype(F32)
        sink = jnp.concatenate([sinks_ref[h] * LOG2E + (SLOPES[h] * LOG2E) * tpos
                                for h in range(Q_PER_KV * g, Q_PER_KV * (g + 1))], axis=1)
        mf = jnp.maximum(m, sink)
        alpha = jnp.exp2(m - mf)
        o = acc * alpha / jnp.maximum(l * alpha + jnp.exp2(sink - mf), TINY)
        outs.append(_regroup_heads(o, tq))
    o_t = jnp.concatenate([jnp.concatenate(outs[u * KV_HEADS:(u + 1) * KV_HEADS], axis=0) for u in range(tiles)],
                          axis=1)
    y = o_t.T * _silu(z_ref[0])
    o_ref[0] = y.astype(o_ref.dtype)


def _swa(proj, sinks, gq, gk, *, tiles_per_step=8):
    bsz, s, _ = proj.shape
    hd = HEAD_DIM
    gw = GROUP_WIDTH
    tq = SWA_WINDOW
    rows = tiles_per_step * tq
    wide = lambda c0: pl.BlockSpec((1, rows, gw), lambda b, i: (b, i, c0 // 4))
    full = lambda c0: pl.BlockSpec((1, s, LANES), lambda b, i: (b, 0, c0))
    return pl.pallas_call(
        _swa_kernel,
        grid=(bsz, s // rows),
        in_specs=[pl.BlockSpec(memory_space=pltpu.SMEM),
                  pl.BlockSpec((N_HEADS, 2 * SLOPE_PIECES, tq), lambda b, i: (0, 0, 0)),
                  wide(ODD_QD), full(ODD_KD), full(ODD_VD), wide(ODD_ZD),
                  pl.BlockSpec((hd, 1), lambda b, i: (0, 0)), pl.BlockSpec((1, hd), lambda b, i: (0, 0))],
        out_specs=pl.BlockSpec((1, rows, gw), lambda b, i: (b, i, 0)),
        out_shape=jax.ShapeDtypeStruct((bsz, s, gw), BF16),
        scratch_shapes=[pltpu.VMEM((KV_HEADS, s, LANES), BF16), pltpu.VMEM((s // tq, LANES, tq), BF16)],
        name="swa",
    )(sinks.astype(F32), jnp.asarray(_alibi_query_features(tq)), proj, proj, proj, proj,
      gq.reshape(hd, 1), gk.reshape(1, hd))


def _odd_weight_moves():
    gw, kw, ng = GROUP_WIDTH, KV_HEADS * HEAD_DIM, 3 * N_HEADS
    sizes = [gw, kw, kw, kw, kw, kw, kw, ng, gw, gw, kw, kw, gw]
    starts = np.concatenate([[0], np.cumsum(sizes)]).tolist()
    order = [0, 8, 9, 12, 1, 2, 3, 4, 5, 6, 10, 11, 7]
    moves, dst = [], 0
    for k in order:
        moves.append((starts[k], sizes[k], dst))
        dst += sizes[k]
    assert dst + LANES - ng == ODD_COLS
    return moves


def kernel(x, norm_g, w_out, e_w_in, a_conv_w, a_conv_b, a_ln_g, a_ln_b, b_qnorm_g, b_knorm_g, o_w_in, c_qnorm_g, c_knorm_cmp_g, c_knorm_slc_g, c_knorm_win_g, c_pos_k, c_pos_v, c_k_w1, c_k_b1, c_k_w2, c_k_b2, c_v_w1, c_v_b1, c_v_w2, c_v_b2, d_qnorm_g, d_knorm_g, d_sinks):
    bsz, s, d = x.shape
    m = bsz * s
    assert s % MOBA_BLOCK == 0 and d == 2 * GROUP_WIDTH
    x2 = x.reshape(m, d)

    y_a, z_b, q_t, ka, vt, km = _even_layer_front(
        x2, norm_g[0], e_w_in[0].astype(BF16), b_qnorm_g[0], b_knorm_g[0],
        a_conv_w[0], a_conv_b[0], a_ln_g[0], a_ln_b[0], bsz=bsz)
    y_b = _moba(z_b.reshape(bsz, s, -1), q_t, ka, vt, km)

    x2, proj, k_cmp, v_cmp = _layer_boundary(
        x2, y_a, y_b.reshape(m, -1), w_out[0].astype(BF16), norm_g[1], o_w_in[0].astype(BF16),
        e=ODD_COLS, moves=_odd_weight_moves(), chunk=ODD_COLS,
        copies=((ODD_KC * LANES, LANES), (ODD_VC * LANES, LANES)), group=NSA_CMP_STRIDE)
    proj = proj.reshape(bsz, s, ODD_COLS)
    rows16 = lambda t: t.reshape(bsz, s // NSA_CMP_STRIDE, NSA_CMP_STRIDE * LANES)
    kcn, vc = _compress(rows16(k_cmp), rows16(v_cmp), c_pos_k[0], c_pos_v[0], c_k_w1[0], c_k_b1[0],
                        c_k_w2[0], c_k_b2[0], c_v_w1[0], c_v_b1[0], c_v_w2[0], c_v_b2[0], c_knorm_cmp_g[0])
    y_c = _nsa(proj, kcn, vc, c_qnorm_g[0], c_knorm_slc_g[0], c_knorm_win_g[0])
    y_d = _swa(proj, d_sinks[0], d_qnorm_g[0], d_knorm_g[0])
    x2 = _outproj(x2, y_c.reshape(m, -1), y_d.reshape(m, -1), w_out[1].astype(BF16))
    return x2.reshape(bsz, s, d)
```

```python
import functools

import ml_dtypes
import numpy as np
import jax
import jax.numpy as jnp
from jax import lax
from jax.experimental import pallas as pl
from jax.experimental.pallas import tpu as pltpu

HEAD_DIM = 64
N_HEADS = 8
GROUP_WIDTH = N_HEADS * HEAD_DIM
CONV_WIDTH = 31
MOBA_BLOCK = 256
MOBA_TOPK = 3
KV_HEADS = 2
Q_PER_KV = N_HEADS // KV_HEADS
NSA_CMP_LEN = 32
NSA_CMP_STRIDE = 16
NSA_CMP_HIDDEN = 256
NSA_SEL_BLOCK = 64
NSA_TOPN = 16
NSA_WINDOW = 512
NSA_FORCE = 1e4
SWA_WINDOW = 128
EPS = 1e-6
NEG = -1e30
TINY = 1e-30
LANES = 128
SUBLANES = 8
CONV_HALO = 32

LOG2E = float(np.log2(np.e))
BIG = 2.0 ** 99
M_INIT = -1e38
POS_SPLIT_SHIFT = 8
SLOPE_PIECES = 4
SEL_FEAT0 = 2 * SLOPE_PIECES

F32 = jnp.float32
BF16 = jnp.bfloat16

ODD_QC, ODD_ZC, ODD_QD, ODD_ZD = 0, 4, 8, 12
ODD_KC, ODD_VC, ODD_KS, ODD_VS, ODD_KW, ODD_VW, ODD_KD, ODD_VD, ODD_GC = 16, 17, 18, 19, 20, 21, 22, 23, 24
ODD_COLS = 25 * LANES


def _alibi_slopes(n):
    return [float(2.0 ** (-8.0 * (i + 1) / n)) for i in range(n)]


SLOPES = _alibi_slopes(N_HEADS)


def _alibi_query_features(width):
    table = np.zeros((N_HEADS, 2 * SLOPE_PIECES, width), np.float32)
    for h, slope in enumerate(SLOPES):
        rest = np.float64(slope) * LOG2E
        for k in range(SLOPE_PIECES):
            piece = float(np.float32(rest).astype(ml_dtypes.bfloat16).astype(np.float32))
            table[h, k] = piece
            table[h, SLOPE_PIECES + k] = piece
            rest -= piece
    return table


def _dot(a, b):
    return jnp.dot(a, b, preferred_element_type=F32)


def _split_bf16(x, pieces):
    out = []
    for _ in range(pieces):
        out.append(x.astype(BF16))
        x = x - out[-1].astype(F32)
    return out


def _dot_fine(a, b, a_is_bf16_exact=False):
    if a_is_bf16_exact:
        return sum(_dot(a.astype(BF16), piece) for piece in _split_bf16(b, 3))
    (a_hi, a_lo), (b_hi, b_lo) = _split_bf16(a, 2), _split_bf16(b, 2)
    return _dot(a_hi, b_hi) + (_dot(a_hi, b_lo) + _dot(a_lo, b_hi))


def _sigmoid(x):
    return 1.0 / (1.0 + jnp.exp(-x))


def _silu(x):
    return x * _sigmoid(x)


def _rms(x, g):
    return x * lax.rsqrt(jnp.mean(x * x, axis=-1, keepdims=True) + EPS) * g


def _iota(shape, dim):
    return lax.broadcasted_iota(jnp.int32, shape, dim)


def _layer_boundary_kernel(x_ref, ya_ref, yb_ref, wo_ref, g_ref, w_ref, x_out_ref, o_ref, *rest,
                           chunk, copies, group, moves):
    copy_refs, stage_refs, wp_s = rest[:len(copies)], rest[len(copies):-1], rest[-1]

    @pl.when(pl.program_id(0) == 0)
    def _():
        wp_s[...] = jnp.zeros(wp_s.shape, wp_s.dtype)
        for src, width, dst in moves:
            wp_s[:, dst:dst + width] = w_ref[:, src:src + width]

    x = _residual_update(x_ref, ya_ref, yb_ref, wo_ref)
    x_out_ref[...] = x
    h = _rms(x, g_ref[...]).astype(BF16)
    for c in range(o_ref.shape[1] // chunk):
        o_ref[:, c * chunk:(c + 1) * chunk] = _dot(h, wp_s[:, c * chunk:(c + 1) * chunk])
    for ref, stage, (start, width) in zip(copy_refs, stage_refs, copies):
        stage[...] = o_ref[:, start:start + width]
        for t in range(group):
            ref[:, t * width:(t + 1) * width] = stage[pl.ds(t, ref.shape[0], stride=group), :]


def _layer_boundary(x2d, ya, yb, w_out, g, w, *, e, moves, tm=512, chunk, copies=(), group=1):
    m, d = x2d.shape
    gw = ya.shape[1]
    rows = lambda n, width: pl.BlockSpec((n, width), lambda i: (i, 0))
    const = lambda shape: pl.BlockSpec(shape, lambda i: (0, 0))
    return pl.pallas_call(
        functools.partial(_layer_boundary_kernel, chunk=chunk, copies=tuple(copies), group=group,
                          moves=tuple(moves)),
        grid=(m // tm,),
        in_specs=[rows(tm, d), rows(tm, gw), rows(tm, gw), const(w_out.shape), const((1, d)), const(w.shape)],
        out_specs=[rows(tm, d), rows(tm, e)] + [rows(tm // group, group * width) for _, width in copies],
        out_shape=[jax.ShapeDtypeStruct((m, d), F32), jax.ShapeDtypeStruct((m, e), F32)]
                  + [jax.ShapeDtypeStruct((m // group, group * width), F32) for _, width in copies],
        scratch_shapes=[pltpu.VMEM((tm, width), F32) for _, width in copies] + [pltpu.VMEM((d, e), BF16)],
        name="layer_boundary",
    )(x2d, ya, yb, w_out, g.reshape(1, d), w)


def _conv_rows(h_s, row0, rows, w_ref, b_ref, lg_ref, lb_ref):
    base = CONV_HALO - (CONV_WIDTH - 1)
    acc = None
    for b in range(SUBLANES):
        n = rows if b == 0 else rows + SUBLANES
        part = None
        for a in range((base + CONV_WIDTH - 1) // SUBLANES + 1):
            j = SUBLANES * a + b - base
            if 0 <= j < CONV_WIDTH:
                term = w_ref[j:j + 1, :] * h_s[pl.ds(row0 + SUBLANES * a, n), :]
                part = term if part is None else part + term
        part = part[b:b + rows]
        acc = part if acc is None else acc + part
    y = acc + b_ref[...]
    mu = jnp.mean(y, axis=-1, keepdims=True)
    yc = y - mu
    return _silu(yc * lax.rsqrt(jnp.mean(yc * yc, axis=-1, keepdims=True) + EPS) * lg_ref[...] + lb_ref[...])


def _even_layer_kernel(x_ref, g_ref, w_ref, gq_ref, gk_ref, cw_ref, cb_ref, lg_ref, lb_ref,
                       ya_ref, zb_ref, qt_ref, ka_ref, vt_ref, km_ref, hb_s, conv_s, moba_s, h_s, tail_s,
                       *, seq, rows):
    tm = x_ref.shape[0]
    hd, gw = HEAD_DIM, GROUP_WIDTH
    tile_in_seq = pl.program_id(0) % (seq // tm)
    hb_s[...] = _rms(x_ref[...], g_ref[...]).astype(BF16)
    conv_s[...] = _dot(hb_s[...], w_ref[:, 0:3 * gw])
    h_s[0:CONV_HALO, :] = jnp.where(tile_in_seq > 0, tail_s[...], 0.0)
    h_s[CONV_HALO:, :] = conv_s[:, 0:gw] * _sigmoid(conv_s[:, gw:2 * gw])
    tail_s[...] = h_s[tm:tm + CONV_HALO, :]
    for c in range(tm // rows):
        y = _conv_rows(h_s, c * rows, rows, cw_ref, cb_ref, lg_ref, lb_ref)
        gate = _silu(conv_s[c * rows:(c + 1) * rows, 2 * gw:3 * gw])
        ya_ref[c * rows:(c + 1) * rows, :] = (y * gate).astype(ya_ref.dtype)

    moba_s[...] = _dot(hb_s[...], w_ref[:, 3 * gw:7 * gw])
    q_t = moba_s[:, 0:gw].T
    for j in range(N_HEADS):
        qt_ref[0, j * hd:(j + 1) * hd, :] = _norm_query_t(q_t[j * hd:(j + 1) * hd], gq_ref[...])
    pos = tile_in_seq * tm + _iota((tm, LANES), 0)
    for pair in range(N_HEADS // 2):
        kn = _pair_rms(moba_s[:, gw + pair * LANES:gw + (pair + 1) * LANES], gk_ref[...])
        ka_ref[0, 2 * pair], ka_ref[0, 2 * pair + 1] = _augment_pair_keys(kn, pos, MOBA_BLOCK.bit_length() - 1)
        km_ref[0, 0, :, pair * LANES:(pair + 1) * LANES] = jnp.mean(
            kn.reshape(tm // MOBA_BLOCK, MOBA_BLOCK, LANES), axis=1)
    v_t = moba_s[:, 2 * gw:3 * gw].T.astype(BF16)
    for c in range(tm // MOBA_BLOCK):
        vt_ref[0, c] = v_t[:, c * MOBA_BLOCK:(c + 1) * MOBA_BLOCK]
    zb_ref[...] = moba_s[:, 3 * gw:4 * gw]


def _even_layer_front(x2d, g, w, gq, gk, conv_w, conv_b, ln_g, ln_b, *, bsz, tm=512, rows=512):
    m, d = x2d.shape
    e = w.shape[1]
    s = m // bsz
    per_b = s // tm
    hd, gw, nb = HEAD_DIM, GROUP_WIDTH, tm // MOBA_BLOCK
    assert s % tm == 0 and tm % MOBA_BLOCK == 0 and e == 7 * gw and tm % rows == 0
    const = lambda shape: pl.BlockSpec(shape, lambda i: (0, 0))
    tile = pl.BlockSpec((tm, gw), lambda i: (i, 0))
    vec = const((1, gw))
    return pl.pallas_call(
        functools.partial(_even_layer_kernel, seq=s, rows=rows),
        grid=(m // tm,),
        in_specs=[pl.BlockSpec((tm, d), lambda i: (i, 0)), const((1, d)), const((d, e)), const((hd, 1)),
                  const((1, hd)), const((CONV_WIDTH, gw)), vec, vec, vec],
        out_specs=[tile, tile,
                   pl.BlockSpec((1, gw, tm), lambda i: (i // per_b, 0, i % per_b)),
                   pl.BlockSpec((1, N_HEADS, tm, LANES), lambda i: (i // per_b, 0, i % per_b, 0)),
                   pl.BlockSpec((1, nb, gw, MOBA_BLOCK), lambda i: (i // per_b, i % per_b, 0, 0)),
                   pl.BlockSpec((1, 1, nb, gw), lambda i: (i // per_b, i % per_b, 0, 0))],
        out_shape=[jax.ShapeDtypeStruct((m, gw), BF16),
                   jax.ShapeDtypeStruct((m, gw), F32),
                   jax.ShapeDtypeStruct((bsz, gw, s), F32),
                   jax.ShapeDtypeStruct((bsz, N_HEADS, s, LANES), BF16),
                   jax.ShapeDtypeStruct((bsz, s // MOBA_BLOCK, gw, MOBA_BLOCK), BF16),
                   jax.ShapeDtypeStruct((bsz, per_b, nb, gw), F32)],
        scratch_shapes=[pltpu.VMEM((tm, d), BF16), pltpu.VMEM((tm, 3 * gw), F32), pltpu.VMEM((tm, 4 * gw), F32),
                        pltpu.VMEM((CONV_HALO + tm, gw), F32), pltpu.VMEM((CONV_HALO, gw), F32)],
        name="even_layer_front",
    )(x2d, g.reshape(1, d), w, gq.reshape(hd, 1), gk.reshape(1, hd), conv_w, conv_b.reshape(1, gw),
      ln_g.reshape(1, gw), ln_b.reshape(1, gw))


def _residual_update(x_ref, ya_ref, yb_ref, w_ref):
    return x_ref[...] + _dot(jnp.concatenate([ya_ref[...], yb_ref[...]], axis=1), w_ref[...])


def _outproj_kernel(x_ref, ya_ref, yb_ref, w_ref, o_ref):
    o_ref[...] = _residual_update(x_ref, ya_ref, yb_ref, w_ref)


def _outproj(x2d, ya, yb, w, *, tm=1024):
    m, d = x2d.shape
    gw = ya.shape[1]
    return pl.pallas_call(
        _outproj_kernel,
        grid=(m // tm,),
        in_specs=[pl.BlockSpec((tm, d), lambda i: (i, 0)),
                  pl.BlockSpec((tm, gw), lambda i: (i, 0)),
                  pl.BlockSpec((tm, gw), lambda i: (i, 0)),
                  pl.BlockSpec(w.shape, lambda i: (0, 0))],
        out_specs=pl.BlockSpec((tm, d), lambda i: (i, 0)),
        out_shape=jax.ShapeDtypeStruct((m, d), F32),
        name="outproj",
    )(x2d, ya, yb, w)


def _key_features(pos, flipped, sel_shift=None):
    col = _iota(pos.shape, 1) - (0 if flipped else HEAD_DIM)
    hi = (pos >> POS_SPLIT_SHIFT) << POS_SPLIT_SHIFT
    lo = pos & ((1 << POS_SPLIT_SHIFT) - 1)
    feat = jnp.where(col < SLOPE_PIECES, hi, jnp.where(col < 2 * SLOPE_PIECES, lo, 0))
    if sel_shift is not None:
        feat = jnp.where(col - SEL_FEAT0 == (pos >> sel_shift), 1, feat)
    return jnp.where((col >= 0) & (col < HEAD_DIM), feat, 0).astype(F32)


def _pair_rms(x, g):
    sq = x * x
    hi = sq.astype(BF16)
    lo = (sq - hi.astype(F32)).astype(BF16)
    head_shift = HEAD_DIM.bit_length() - 1
    same_head = (_iota((LANES, LANES), 0) >> head_shift) == (_iota((LANES, LANES), 1) >> head_shift)
    ones = jnp.where(same_head, 1.0, 0.0).astype(BF16)
    ss = _dot(hi, ones) + _dot(lo, ones)
    return x * lax.rsqrt(ss * (1.0 / HEAD_DIM) + EPS) * jnp.concatenate([g, g], axis=1)


def _augment_pair_keys(kn, pos, sel_shift=None):
    lane = _iota(kn.shape, 1)
    return [jnp.where(lane >= HEAD_DIM if flipped else lane < HEAD_DIM, kn,
                      _key_features(pos, flipped, sel_shift)).astype(BF16) for flipped in (False, True)]


def _norm_query_t(x, g_col):
    ss = jnp.mean(x * x, axis=0, keepdims=True)
    return x * lax.rsqrt(ss + EPS) * g_col * (HEAD_DIM ** -0.5 * LOG2E)


def _augment_query(q_t, alibi, sel_bias=None, flipped=False):
    n = q_t.shape[1]
    feats = [alibi]
    used = alibi.shape[0]
    if sel_bias is not None:
        feats.append(sel_bias)
        used += sel_bias.shape[0]
    feats.append(jnp.zeros((HEAD_DIM - used, n), F32))
    return jnp.concatenate(feats + [q_t] if flipped else [q_t] + feats, axis=0).astype(BF16)


ONES_ROWS = 16
ACC_ROWS = HEAD_DIM + ONES_ROWS


def _flash_init(n):
    return jnp.full((1, n), M_INIT, F32), jnp.zeros((ACC_ROWS, n), F32)


def _flash_step(carry, sc, v_t):
    m, acc = carry
    m_new = jnp.maximum(m, jnp.max(sc, axis=0, keepdims=True))
    p = jnp.exp2(sc - m_new).astype(BF16)
    v_ones = jnp.concatenate([v_t, jnp.ones((ONES_ROWS, v_t.shape[1]), BF16)], axis=0)
    return m_new, jnp.exp2(m - m_new) * acc + _dot(v_ones, p)


def _flash_split(carry):
    m, acc = carry
    return m, acc[:HEAD_DIM], acc[HEAD_DIM:HEAD_DIM + 1]


def _flash_steps(carries, chunks, cap=None):
    return tuple(_flash_step(carry, sc if cap is None else jnp.minimum(sc, cap), v_t)
                 for carry, (sc, v_t) in zip(carries, chunks))


def _flash_reset(state):
    m_ref, acc_ref = state
    m_ref[...] = jnp.full(m_ref.shape, M_INIT, F32)
    acc_ref[...] = jnp.zeros(acc_ref.shape, F32)


def _flash_update(state, slot, sc, v_t, cap=None):
    m_ref, acc_ref = state
    m_ref[slot], acc_ref[slot] = _flash_step((m_ref[slot], acc_ref[slot]),
                                             sc if cap is None else jnp.minimum(sc, cap), v_t)


def _flash_update_triangular(state, slot, sc_ref, v_t, cap, full_half, partial_lanes):
    m_ref, acc_ref = state
    half = sc_ref.shape[0] // 2
    full = slice(full_half * half, (full_half + 1) * half)
    part = slice((1 - full_half) * half, (2 - full_half) * half)
    _flash_update(state, slot, sc_ref[full, :], v_t[:, full], cap[full, :])

    def gather(x):
        return jnp.concatenate([x[..., a:b] for a, b in partial_lanes], axis=-1)

    sc = jnp.minimum(jnp.concatenate([sc_ref[part, a:b] for a, b in partial_lanes], axis=1), gather(cap[part, :]))
    m, acc = _flash_step((gather(m_ref[slot]), gather(acc_ref[slot])), sc, v_t[:, part])
    width = partial_lanes[0][1] - partial_lanes[0][0]
    for k, (a, b) in enumerate(partial_lanes):
        m_ref[slot, :, a:b] = m[:, k * width:(k + 1) * width]
        acc_ref[slot, :, a:b] = acc[:, k * width:(k + 1) * width]


def _half_lanes(tq, reps, upper):
    off = tq // 2 if upper else 0
    return [(k * tq + off, k * tq + off + tq // 2) for k in range(reps)]


def _flash_update_pipelined(state, bufs, slots, c, next_scores, values, cap=None, cap_slots=None,
                            triangular=None):
    def run(src_ref, dst_ref):
        for slot in slots:
            fresh = None if next_scores is None else next_scores(slot)
            masked = cap is not None and (cap_slots is None or slot in cap_slots)
            if masked and triangular is not None:
                _flash_update_triangular(state, slot, src_ref.at[slot], values(slot), cap, *triangular)
            else:
                _flash_update(state, slot, src_ref[slot], values(slot), cap if masked else None)
            if fresh is not None:
                dst_ref[slot] = fresh

    pl.when((c & 1) == 0)(lambda: run(bufs[0], bufs[1]))
    pl.when((c & 1) == 1)(lambda: run(bufs[1], bufs[0]))


def _flash_result(state, slot):
    m_ref, acc_ref = state
    m, out, den = _flash_split((m_ref[slot], acc_ref[slot]))
    return out / jnp.maximum(den, TINY)


def _flash_scratch(chains, n):
    return [pltpu.VMEM((chains, 1, n), F32), pltpu.VMEM((chains, ACC_ROWS, n), F32)]


def _rank_select(score, blk, limit, count):
    ranks = []
    for r0 in range(0, score.shape[0], SUBLANES):
        tile = score[r0:r0 + SUBLANES]
        tile_blk = r0 + _iota(tile.shape, 0)
        rank = jnp.zeros(tile.shape, jnp.int32)
        for mm in range(limit):
            gm = score[mm:mm + 1, :]
            if mm < r0:
                beats = gm >= tile
            elif mm >= r0 + SUBLANES:
                beats = gm > tile
            else:
                beats = (gm > tile) | ((gm == tile) & (mm < tile_blk))
            rank = rank + jnp.where(beats, 1, 0)
        ranks.append(rank)
    return jnp.concatenate(ranks, axis=0) < count


def _tile_lanes(x, reps):
    return jnp.concatenate([x] * reps, axis=1) if reps > 1 else x


def _causal_cap(tk, tq, reps):
    keep = _iota((tk, tq), 0) <= _iota((tk, tq), 1)
    return _tile_lanes(jnp.where(keep, BIG, -BIG), reps)


def _window_edge_cap(tk, tq, reps):
    keep = _iota((tk, tq), 0) > _iota((tk, tq), 1)
    return _tile_lanes(jnp.where(keep, BIG, -BIG), reps)


def _store_chunked_t(x, dst_ref, tk):
    x_t = x.T.astype(BF16)
    for c in range(x.shape[0] // tk):
        dst_ref[c] = x_t[:, c * tk:(c + 1) * tk]


def _moba_kernel(alibi_ref, qt_ref, ka_ref, vt_ref, km_ref, z_ref, o_ref, m_s, acc_s, sc0_s, sc1_s):
    i = pl.program_id(1)
    nblk = vt_ref.shape[1]
    tq = MOBA_BLOCK
    hd = HEAD_DIM
    chains = [(r, j) for r in range(qt_ref.shape[0]) for j in range(N_HEADS)]
    slots = range(len(chains))

    blk = _iota((nblk, tq), 0)
    qas = []
    for r, j in chains:
        qn = qt_ref[r, j * hd:(j + 1) * hd, :]
        km = jnp.concatenate([km_ref[r, t, :, j * hd:(j + 1) * hd] for t in range(km_ref.shape[1])], axis=0)
        gate = jnp.where(blk < i, _dot_fine(km, qn), NEG)
        chosen = _rank_select(gate, blk, nblk, MOBA_TOPK) & (blk < i)
        sel_bias = jnp.where(chosen | (blk == i), 0.0, -BIG)
        qas.append(_augment_query(qn, alibi_ref[j], sel_bias, flipped=j % 2 == 1))

    def scores(slot, n):
        r, j = chains[slot]
        return _dot(ka_ref[r, j, pl.ds(pl.multiple_of(n * tq, tq), tq), :], qas[slot])

    def values(slot, n):
        r, j = chains[slot]
        return vt_ref[r, n, j * hd:(j + 1) * hd, :]

    state = (m_s, acc_s)
    _flash_reset(state)
    bufs = (sc0_s, sc1_s)
    for slot in slots:
        sc0_s[slot] = scores(slot, 0)

    def body(n, carry):
        _flash_update_pipelined(state, bufs, slots, n, lambda slot: scores(slot, n + 1),
                                lambda slot: values(slot, n))
        return carry

    lax.fori_loop(0, i, body, 0)
    diag_cap = _causal_cap(tq, tq, 1)
    _flash_update_pipelined(state, bufs, slots, i, None, lambda slot: values(slot, i), diag_cap,
                            triangular=(0, _half_lanes(tq, 1, upper=True)))
    for r in range(qt_ref.shape[0]):
        outs = [_flash_result(state, r * N_HEADS + j) for j in range(N_HEADS)]
        y = jnp.concatenate(outs, axis=0).T * _silu(z_ref[r])
        o_ref[r] = y.astype(o_ref.dtype)


def _moba(z, q_t, ka, vt, km, *, rows_per_step=2):
    bsz, s, _ = z.shape
    tq = MOBA_BLOCK
    gw = GROUP_WIDTH
    nblk = s // MOBA_BLOCK
    rps = rows_per_step
    assert SEL_FEAT0 + nblk <= HEAD_DIM and bsz % rps == 0
    whole = lambda a: pl.BlockSpec((rps,) + a.shape[1:], lambda b, i: (b,) + (0,) * (a.ndim - 1))
    return pl.pallas_call(
        _moba_kernel,
        grid=(bsz // rps, s // tq),
        in_specs=[pl.BlockSpec((N_HEADS, 2 * SLOPE_PIECES, tq), lambda b, i: (0, 0, 0)),
                  pl.BlockSpec((rps, gw, tq), lambda b, i: (b, 0, i)), whole(ka), whole(vt), whole(km),
                  pl.BlockSpec((rps, tq, gw), lambda b, i: (b, i, 0))],
        out_specs=pl.BlockSpec((rps, tq, gw), lambda b, i: (b, i, 0)),
        out_shape=jax.ShapeDtypeStruct((bsz, s, gw), BF16),
        scratch_shapes=_flash_scratch(rps * N_HEADS, tq) + [pltpu.VMEM((rps * N_HEADS, tq, tq), F32)] * 2,
        name="moba",
    )(jnp.asarray(_alibi_query_features(tq)), q_t, ka, vt, km, z)


def _cmp_kernel(xk_ref, xv_ref, wk1_ref, wv1_ref, pk_ref, pv_ref, bk1_ref, bv1_ref,
                wk2_ref, wv2_ref, bk2_ref, bv2_ref, gn_ref, ko_ref, vo_ref, wke_s, wve_s):
    half = xk_ref.shape[2]
    n = xk_ref.shape[1]
    hd = HEAD_DIM
    nh = NSA_CMP_HIDDEN

    @pl.when(pl.program_id(0) == 0)
    def _():
        for w_ref, we_ref in ((wk1_ref, wke_s), (wv1_ref, wve_s)):
            we_ref[...] = jnp.zeros(we_ref.shape, BF16)
            for tok in range(NSA_CMP_LEN):
                for g in range(KV_HEADS):
                    row = (tok * KV_HEADS + g) * hd
                    we_ref[row:row + hd, g * nh:(g + 1) * nh] = w_ref[tok * hd:(tok + 1) * hd, :]

    rows = xk_ref.shape[0]

    def hidden(x_ref, we_ref, pos_ref, b1_ref):
        x = x_ref[...].reshape(rows * n, half).astype(BF16)
        first = _dot(x, we_ref[:half, :])
        second = _dot(x, we_ref[half:, :])
        second = jnp.concatenate([pltpu.roll(second[r * n:(r + 1) * n], n - 1, 0) for r in range(rows)], axis=0)
        pos = jnp.broadcast_to(pos_ref[...], (8, 2 * half)).astype(BF16)
        ph = _dot(pos[:, :half], we_ref[:half, :]) + _dot(pos[:, half:], we_ref[half:, :])
        return _silu(first + second + ph[0:1, :] + b1_ref[...])

    hk = hidden(xk_ref, wke_s, pk_ref, bk1_ref).astype(BF16)
    hv = hidden(xv_ref, wve_s, pv_ref, bv1_ref).astype(BF16)
    ks, vs = [], []
    for g in range(KV_HEADS):
        ks.append(_rms(_dot(hk[:, g * nh:(g + 1) * nh], wk2_ref[...]) + bk2_ref[...], gn_ref[...]))
        vs.append(_dot(hv[:, g * nh:(g + 1) * nh], wv2_ref[...]) + bv2_ref[...])
    ko_ref[...] = jnp.concatenate(ks, axis=1).reshape(ko_ref.shape)
    vo_ref[...] = jnp.concatenate(vs, axis=1).reshape(vo_ref.shape)


def _compress(xk, xv, pos_k, pos_v, kw1, kb1, kw2, kb2, vw1, vb1, vw2, vb2, gn, *, rows_per_step=4):
    bsz, n, half = xk.shape
    hd = HEAD_DIM
    rps = rows_per_step
    assert bsz % rps == 0
    tile_pos = lambda p: jnp.tile(p[:, None, :], (1, KV_HEADS, 1)).reshape(1, 2 * half)
    tile_b = lambda b: jnp.tile(b.reshape(1, -1), (1, KV_HEADS))
    const = lambda shape: pl.BlockSpec(shape, lambda b: (0,) * len(shape))
    xs = pl.BlockSpec((rps, n, half), lambda b: (b, 0, 0))
    hw = KV_HEADS * NSA_CMP_HIDDEN
    out = pl.BlockSpec((rps, n, KV_HEADS * hd), lambda b: (b, 0, 0))
    expanded = pltpu.VMEM((2 * half, hw), BF16)
    return pl.pallas_call(
        _cmp_kernel,
        grid=(bsz // rps,),
        in_specs=[xs, xs, const(kw1.shape), const(vw1.shape),
                  const((1, 2 * half)), const((1, 2 * half)), const((1, hw)), const((1, hw)),
                  const((NSA_CMP_HIDDEN, hd)), const((NSA_CMP_HIDDEN, hd)), const((1, hd)), const((1, hd)),
                  const((1, hd))],
        out_specs=[out, out],
        out_shape=[jax.ShapeDtypeStruct((bsz, n, KV_HEADS * hd), F32)] * 2,
        scratch_shapes=[expanded, expanded],
        name="nsa_compress",
    )(xk, xv, kw1.astype(BF16), vw1.astype(BF16), tile_pos(pos_k), tile_pos(pos_v), tile_b(kb1), tile_b(vb1),
      kw2.astype(BF16), vw2.astype(BF16), kb2.reshape(1, hd), vb2.reshape(1, hd), gn.reshape(1, hd))


def _group_query(q_t, alibi_ref, g, gq_col):
    hd = HEAD_DIM
    heads = range(Q_PER_KV * g, Q_PER_KV * (g + 1))
    qn = jnp.concatenate([_norm_query_t(q_t[h * hd:(h + 1) * hd], gq_col) for h in heads], axis=1)
    alibi = jnp.concatenate([alibi_ref[h] for h in heads], axis=1)
    return qn, alibi


def _augment_keys(k, gk_ref, dst_ref, row=0, sel_shift=None):
    assert KV_HEADS == 2
    pos = _iota(k.shape, 0)
    dst_ref[2 * row], dst_ref[2 * row + 1] = _augment_pair_keys(_pair_rms(k, gk_ref[...]), pos, sel_shift)


def _chunk_scores(ka_ref, vt_ref, g, c, qa, tq):
    off = pl.multiple_of(c * tq, tq)
    return _dot(ka_ref[g, pl.ds(off, tq), :], qa), vt_ref[c, g * HEAD_DIM:(g + 1) * HEAD_DIM, :]


def _window_first_chunk(ka_ref, vt_ref, g, far, qa, tq, edge_cap):
    sc, v_t = _chunk_scores(ka_ref, vt_ref, g, jnp.maximum(far, 0), qa, tq)
    return jnp.minimum(sc, jnp.minimum(edge_cap, jnp.where(far >= 0, BIG, -BIG))), v_t


def _regroup_heads(o_t, tq):
    return jnp.concatenate([o_t[:, k * tq:(k + 1) * tq] for k in range(Q_PER_KV)], axis=0)


def _nsa_kernel(alibi_ref, q_ref, kc_ref, vc_ref, ks_ref, vs_ref, kw_ref, vw_ref, gate_ref, z_ref,
                gq_ref, gks_ref, gkw_ref, o_ref, kca_s, vct_s, ksa_s, vst_s, kwa_s, vwt_s, m_s, acc_s,
                sc0_s, sc1_s):
    i = pl.program_id(1)
    rows = range(q_ref.shape[0])
    tq = q_ref.shape[1]
    s = ks_ref.shape[1]
    hd = HEAD_DIM
    ncmp = kc_ref.shape[1]
    nsb = s // NSA_SEL_BLOCK

    @pl.when(i == 0)
    def _():
        cend = _iota((ncmp, LANES), 0) * NSA_CMP_STRIDE + (NSA_CMP_LEN - 1)
        for r in rows:
            _augment_keys(ks_ref[r], gks_ref, ksa_s, r, NSA_SEL_BLOCK.bit_length() - 1)
            _augment_keys(kw_ref[r], gkw_ref, kwa_s, r)
            _store_chunked_t(vs_ref[r], vst_s.at[r], tq)
            _store_chunked_t(vw_ref[r], vwt_s.at[r], tq)
            kca_s[2 * r], kca_s[2 * r + 1] = _augment_pair_keys(kc_ref[r], cend)
            vct_s[r] = vc_ref[r].T.astype(BF16)

    t0 = i * tq
    diag_cap = _causal_cap(tq, tq, Q_PER_KV)
    edge_cap = _window_edge_cap(tq, tq, Q_PER_KV)
    cend =_iota((ncmp, tq), 0) * NSA_CMP_STRIDE + (NSA_CMP_LEN - 1)
    seen_cap = _tile_lanes(jnp.where(cend <= t0 + _iota((ncmp, tq), 1), BIG, -BIG), Q_PER_KV)
    cstart = _iota((nsb, ncmp), 1) * NSA_CMP_STRIDE
    bstart = _iota((nsb, ncmp), 0) * NSA_SEL_BLOCK
    overlap_t = jnp.where((cstart < bstart + NSA_SEL_BLOCK) & (cstart + NSA_CMP_LEN > bstart), 1.0, 0.0)
    blk = _iota((nsb, tq), 0)
    cur = (t0 + _iota((nsb, tq), 1)) >> (NSA_SEL_BLOCK.bit_length() - 1)
    forced = (blk == 0) | (blk == cur) | (blk == cur - 1)
    groups = range(KV_HEADS)
    far = i - NSA_WINDOW // tq
    lo = jnp.maximum(far + 1, 0)
    first_win = jnp.maximum(far, 0)

    state = (m_s, acc_s)
    kv = [(r, g) for r in rows for g in groups]
    sel_slots = tuple(range(len(kv)))
    win_slots = tuple(len(kv) + n for n in range(len(kv)))
    _flash_reset(state)
    q_ts = [q_ref[r].T for r in rows]
    queries = [_group_query(q_ts[r], alibi_ref, g, gq_ref[...]) for r, g in kv]
    qa_win = [_augment_query(qn, alibi, flipped=g == 1) for (r, g), (qn, alibi) in zip(kv, queries)]
    qa_sel = []

    def scores(slot, c):
        n = slot % len(kv)
        ka_ref, qa = (ksa_s, qa_sel) if slot in sel_slots else (kwa_s, qa_win)
        return _dot(ka_ref[n, pl.ds(pl.multiple_of(c * tq, tq), tq), :], qa[n])

    def values(slot, c):
        r, g = kv[slot % len(kv)]
        return (vst_s if slot in sel_slots else vwt_s)[r, c, g * hd:(g + 1) * hd, :]

    cmp_sc = [_dot(kca_s[n], qa_win[n]) for n in range(len(kv))]
    sc_win_first = [scores(slot, first_win) for slot in win_slots]
    o_cmp = []
    for n, (r, g) in enumerate(kv):
        qn, alibi = queries[n]
        sc = jnp.minimum(cmp_sc[n], seen_cap)
        top = jnp.max(sc, axis=0, keepdims=True)
        e = jnp.exp2(sc - top)
        inv = jnp.where(top > -0.5 * BIG, 1.0 / jnp.maximum(jnp.sum(e, axis=0, keepdims=True), TINY), 0.0)
        o_cmp.append(_dot(vct_s[r, g * hd:(g + 1) * hd, :], e.astype(BF16)) * inv)

        psum = e[:, 0:tq] * inv[:, 0:tq]
        for k in range(1, Q_PER_KV):
            psum = psum + e[:, k * tq:(k + 1) * tq] * inv[:, k * tq:(k + 1) * tq]
        imp = _dot_fine(overlap_t, psum, a_is_bf16_exact=True)
        imp = jnp.where(blk <= cur, jnp.where(forced, NSA_FORCE, imp), NEG)
        chosen = _rank_select(imp, blk, nsb, min(NSA_TOPN, nsb))
        sel_bias = _tile_lanes(jnp.where(chosen, 0.0, -BIG), Q_PER_KV)
        qa_sel.append(_augment_query(qn, alibi, sel_bias, flipped=g == 1))

    bufs = (sc0_s, sc1_s)
    for slot in sel_slots:
        sc0_s[slot] = scores(slot, 0)
    for slot, sc in zip(win_slots, sc_win_first):
        sc0_s[slot] = sc
        sc1_s[slot] = sc

    def body(slots, cap, cap_slots, triangular, c, carry):
        _flash_update_pipelined(state, bufs, slots, c, lambda slot: scores(slot, c + 1),
                                lambda slot: values(slot, c), cap, cap_slots, triangular)
        return carry

    both_slots = sel_slots + win_slots
    edge_shape = (1, _half_lanes(tq, Q_PER_KV, upper=False))
    diag_shape = (0, _half_lanes(tq, Q_PER_KV, upper=True))
    lax.fori_loop(0, first_win, functools.partial(body, sel_slots, None, None, None), 0)
    lax.fori_loop(first_win, lo, functools.partial(body, both_slots, edge_cap, win_slots, edge_shape), 0)
    lax.fori_loop(lo, i, functools.partial(body, both_slots, None, None, None), 0)
    _flash_update_pipelined(state, bufs, both_slots, i, None, lambda slot: values(slot, i), diag_cap,
                            triangular=diag_shape)

    for r in rows:
        gates_t = _sigmoid(gate_ref[r]).T
        outs = []
        for g in groups:
            def gate_row(branch):
                parts = [gates_t[branch * N_HEADS + h:branch * N_HEADS + h + 1, :]
                         for h in range(Q_PER_KV * g, Q_PER_KV * (g + 1))]
                return jnp.concatenate(parts, axis=1)

            n = KV_HEADS * r + g
            o_slc = _flash_result(state, sel_slots[n])
            o_win = _flash_result(state, win_slots[n])
            o = gate_row(0) * o_cmp[n] + gate_row(1) * o_slc + gate_row(2) * o_win
            outs.append(_regroup_heads(o, tq))
        y = jnp.concatenate(outs, axis=0).T * _silu(z_ref[r])
        o_ref[r] = y.astype(o_ref.dtype)


def _nsa(proj, kcn, vc, gq, gks, gkw, *, tq=256, rows_per_step=2):
    bsz, s, _ = proj.shape
    hd = HEAD_DIM
    gw = GROUP_WIDTH
    ncmp = kcn.shape[1]
    rps = rows_per_step
    chains = 2 * KV_HEADS * rps
    assert SEL_FEAT0 + s // NSA_SEL_BLOCK <= hd and ncmp == LANES and bsz % rps == 0
    wide = lambda c0: pl.BlockSpec((rps, tq, gw), lambda b, i: (b, i, c0 // 4))
    full = lambda c0: pl.BlockSpec((rps, s, LANES), lambda b, i: (b, 0, c0))
    cmp = pl.BlockSpec((rps, ncmp, LANES), lambda b, i: (b, 0, 0))
    row = pl.BlockSpec((1, hd), lambda b, i: (0, 0))
    ka = pltpu.VMEM((KV_HEADS * rps, s, LANES), BF16)
    vt = pltpu.VMEM((rps, s // tq, LANES, tq), BF16)
    return pl.pallas_call(
        _nsa_kernel,
        grid=(bsz // rps, s // tq),
        in_specs=[pl.BlockSpec((N_HEADS, 2 * SLOPE_PIECES, tq), lambda b, i: (0, 0, 0)),
                  wide(ODD_QC), cmp, cmp, full(ODD_KS), full(ODD_VS), full(ODD_KW), full(ODD_VW),
                  pl.BlockSpec((rps, tq, LANES), lambda b, i: (b, i, ODD_GC)), wide(ODD_ZC),
                  pl.BlockSpec((hd, 1), lambda b, i: (0, 0)), row, row],
        out_specs=pl.BlockSpec((rps, tq, gw), lambda b, i: (b, i, 0)),
        out_shape=jax.ShapeDtypeStruct((bsz, s, gw), BF16),
        scratch_shapes=[pltpu.VMEM((KV_HEADS * rps, ncmp, LANES), BF16), pltpu.VMEM((rps, LANES, ncmp), BF16),
                        ka, vt, ka, vt] + _flash_scratch(chains, Q_PER_KV * tq)
                       + [pltpu.VMEM((chains, tq, Q_PER_KV * tq), F32)] * 2,
        name="nsa",
    )(jnp.asarray(_alibi_query_features(tq)), proj, kcn, vc, proj, proj, proj, proj, proj, proj,
      gq.reshape(hd, 1), gks.reshape(1, hd), gkw.reshape(1, hd))


def _swa_kernel(sinks_ref, alibi_ref, q_ref, k_ref, v_ref, z_ref, gq_ref, gk_ref, o_ref, ka_s, vt_s):
    step = pl.program_id(1)
    tq = SWA_WINDOW
    tiles = q_ref.shape[1] // tq
    s = k_ref.shape[1]

    @pl.when(step == 0)
    def _():
        _augment_keys(k_ref[0], gk_ref, ka_s)
        _store_chunked_t(v_ref[0], vt_s, tq)

    q_t = q_ref[0].T
    diag_cap = _causal_cap(tq, tq, Q_PER_KV)
    edge_cap = _window_edge_cap(tq, tq, Q_PER_KV)
    chains = [(u, g) for u in range(tiles) for g in range(KV_HEADS)]
    qas = [_augment_query(*_group_query(q_t[:, u * tq:(u + 1) * tq], alibi_ref, g, gq_ref[...]), flipped=g == 1)
           for u, g in chains]
    first = [_window_first_chunk(ka_s, vt_s, g, step * tiles + u - 1, qa, tq, edge_cap)
             for (u, g), qa in zip(chains, qas)]
    own = [_chunk_scores(ka_s, vt_s, g, step * tiles + u, qa, tq) for (u, g), qa in zip(chains, qas)]
    carries = _flash_steps([_flash_init(Q_PER_KV * tq) for _ in chains], first)
    carries = _flash_steps(carries, own, diag_cap)
    outs = []
    for (u, g), carry in zip(chains, carries):
        m, acc, l = _flash_split(carry)
        tpos = ((step * tiles + u) * tq + _iota((1, tq), 1)).astype(F32)
        sink = jnp.concatenate([sinks_ref[h] * LOG2E + (SLOPES[h] * LOG2E) * tpos
                                for h in range(Q_PER_KV * g, Q_PER_KV * (g + 1))], axis=1)
        mf = jnp.maximum(m, sink)
        alpha = jnp.exp2(m - mf)
        o = acc * alpha / jnp.maximum(l * alpha + jnp.exp2(sink - mf), TINY)
        outs.append(_regroup_heads(o, tq))
    o_t = jnp.concatenate([jnp.concatenate(outs[u * KV_HEADS:(u + 1) * KV_HEADS], axis=0) for u in range(tiles)],
                          axis=1)
    y = o_t.T * _silu(z_ref[0])
    o_ref[0] = y.astype(o_ref.dtype)


def _swa(proj, sinks, gq, gk, *, tiles_per_step=16):
    bsz, s, _ = proj.shape
    hd = HEAD_DIM
    gw = GROUP_WIDTH
    tq = SWA_WINDOW
    rows = tiles_per_step * tq
    wide = lambda c0: pl.BlockSpec((1, rows, gw), lambda b, i: (b, i, c0 // 4))
    full = lambda c0: pl.BlockSpec((1, s, LANES), lambda b, i: (b, 0, c0))
    return pl.pallas_call(
        _swa_kernel,
        grid=(bsz, s // rows),
        in_specs=[pl.BlockSpec(memory_space=pltpu.SMEM),
                  pl.BlockSpec((N_HEADS, 2 * SLOPE_PIECES, tq), lambda b, i: (0, 0, 0)),
                  wide(ODD_QD), full(ODD_KD), full(ODD_VD), wide(ODD_ZD),
                  pl.BlockSpec((hd, 1), lambda b, i: (0, 0)), pl.BlockSpec((1, hd), lambda b, i: (0, 0))],
        out_specs=pl.BlockSpec((1, rows, gw), lambda b, i: (b, i, 0)),
        out_shape=jax.ShapeDtypeStruct((bsz, s, gw), BF16),
        scratch_shapes=[pltpu.VMEM((KV_HEADS, s, LANES), BF16), pltpu.VMEM((s // tq, LANES, tq), BF16)],
        name="swa",
    )(sinks.astype(F32), jnp.asarray(_alibi_query_features(tq)), proj, proj, proj, proj,
      gq.reshape(hd, 1), gk.reshape(1, hd))


def _odd_weight_moves():
    gw, kw, ng = GROUP_WIDTH, KV_HEADS * HEAD_DIM, 3 * N_HEADS
    sizes = [gw, kw, kw, kw, kw, kw, kw, ng, gw, gw, kw, kw, gw]
    starts = np.concatenate([[0], np.cumsum(sizes)]).tolist()
    order = [0, 8, 9, 12, 1, 2, 3, 4, 5, 6, 10, 11, 7]
    moves, dst = [], 0
    for k in order:
        moves.append((starts[k], sizes[k], dst))
        dst += sizes[k]
    assert dst + LANES - ng == ODD_COLS
    return moves


def kernel(x, norm_g, w_out, e_w_in, a_conv_w, a_conv_b, a_ln_g, a_ln_b, b_qnorm_g, b_knorm_g, o_w_in, c_qnorm_g, c_knorm_cmp_g, c_knorm_slc_g, c_knorm_win_g, c_pos_k, c_pos_v, c_k_w1, c_k_b1, c_k_w2, c_k_b2, c_v_w1, c_v_b1, c_v_w2, c_v_b2, d_qnorm_g, d_knorm_g, d_sinks):
    bsz, s, d = x.shape
    m = bsz * s
    assert s % MOBA_BLOCK == 0 and d == 2 * GROUP_WIDTH
    x2 = x.reshape(m, d)

    y_a, z_b, q_t, ka, vt, km = _even_layer_front(
        x2, norm_g[0], e_w_in[0].astype(BF16), b_qnorm_g[0], b_knorm_g[0],
        a_conv_w[0], a_conv_b[0], a_ln_g[0], a_ln_b[0], bsz=bsz)
    y_b = _moba(z_b.reshape(bsz, s, -1), q_t, ka, vt, km)

    x2, proj, k_cmp, v_cmp = _layer_boundary(
        x2, y_a, y_b.reshape(m, -1), w_out[0].astype(BF16), norm_g[1], o_w_in[0].astype(BF16),
        e=ODD_COLS, moves=_odd_weight_moves(), chunk=ODD_COLS,
        copies=((ODD_KC * LANES, LANES), (ODD_VC * LANES, LANES)), group=NSA_CMP_STRIDE)
    proj = proj.reshape(bsz, s, ODD_COLS)
    rows16 = lambda t: t.reshape(bsz, s // NSA_CMP_STRIDE, NSA_CMP_STRIDE * LANES)
    kcn, vc = _compress(rows16(k_cmp), rows16(v_cmp), c_pos_k[0], c_pos_v[0], c_k_w1[0], c_k_b1[0],
                        c_k_w2[0], c_k_b2[0], c_v_w1[0], c_v_b1[0], c_v_w2[0], c_v_b2[0], c_knorm_cmp_g[0])
    y_c = _nsa(proj, kcn, vc, c_qnorm_g[0], c_knorm_slc_g[0], c_knorm_win_g[0])
    y_d = _swa(proj, d_sinks[0], d_qnorm_g[0], d_knorm_g[0])
    x2 = _outproj(x2, y_c.reshape(m, -1), y_d.reshape(m, -1), w_out[1].astype(BF16))
    return x2.reshape(bsz, s, d)
```

```python
import functools

import ml_dtypes
import numpy as np
import jax
import jax.numpy as jnp
from jax import lax
from jax.experimental import pallas as pl
from jax.experimental.pallas import tpu as pltpu

HEAD_DIM = 64
N_HEADS = 8
GROUP_WIDTH = N_HEADS * HEAD_DIM
CONV_WIDTH = 31
MOBA_BLOCK = 256
MOBA_TOPK = 3
KV_HEADS = 2
Q_PER_KV = N_HEADS // KV_HEADS
NSA_CMP_LEN = 32
NSA_CMP_STRIDE = 16
NSA_CMP_HIDDEN = 256
NSA_SEL_BLOCK = 64
NSA_TOPN = 16
NSA_WINDOW = 512
NSA_FORCE = 1e4
SWA_WINDOW = 128
EPS = 1e-6
NEG = -1e30
TINY = 1e-30
LANES = 128
SUBLANES = 8
CONV_HALO = 32

LOG2E = float(np.log2(np.e))
BIG = 2.0 ** 99
M_INIT = -1e38
POS_SPLIT_SHIFT = 8
SLOPE_PIECES = 4
SEL_FEAT0 = 2 * SLOPE_PIECES

F32 = jnp.float32
BF16 = jnp.bfloat16

ODD_QC, ODD_ZC, ODD_QD, ODD_ZD = 0, 4, 8, 12
ODD_KC, ODD_VC, ODD_KS, ODD_VS, ODD_KW, ODD_VW, ODD_KD, ODD_VD, ODD_GC = 16, 17, 18, 19, 20, 21, 22, 23, 24
ODD_COLS = 25 * LANES


def _alibi_slopes(n):
    return [float(2.0 ** (-8.0 * (i + 1) / n)) for i in range(n)]


SLOPES = _alibi_slopes(N_HEADS)


def _alibi_query_features(width):
    table = np.zeros((N_HEADS, 2 * SLOPE_PIECES, width), np.float32)
    for h, slope in enumerate(SLOPES):
        rest = np.float64(slope) * LOG2E
        for k in range(SLOPE_PIECES):
            piece = float(np.float32(rest).astype(ml_dtypes.bfloat16).astype(np.float32))
            table[h, k] = piece
            table[h, SLOPE_PIECES + k] = piece
            rest -= piece
    return table


def _dot(a, b):
    return jnp.dot(a, b, preferred_element_type=F32)


def _split_bf16(x, pieces):
    out = []
    for _ in range(pieces):
        out.append(x.astype(BF16))
        x = x - out[-1].astype(F32)
    return out


def _dot_fine(a, b, a_is_bf16_exact=False):
    if a_is_bf16_exact:
        return sum(_dot(a.astype(BF16), piece) for piece in _split_bf16(b, 3))
    (a_hi, a_lo), (b_hi, b_lo) = _split_bf16(a, 2), _split_bf16(b, 2)
    return _dot(a_hi, b_hi) + (_dot(a_hi, b_lo) + _dot(a_lo, b_hi))


def _sigmoid(x):
    return 1.0 / (1.0 + jnp.exp(-x))


def _silu(x):
    return x * _sigmoid(x)


def _rms(x, g):
    return x * lax.rsqrt(jnp.mean(x * x, axis=-1, keepdims=True) + EPS) * g


def _iota(shape, dim):
    return lax.broadcasted_iota(jnp.int32, shape, dim)


def _layer_boundary_kernel(x_ref, ya_ref, yb_ref, wo_ref, g_ref, w_ref, x_out_ref, o_ref, *rest,
                           chunk, copies, group, moves):
    copy_refs, stage_refs, wp_s = rest[:len(copies)], rest[len(copies):-1], rest[-1]

    @pl.when(pl.program_id(0) == 0)
    def _():
        wp_s[...] = jnp.zeros(wp_s.shape, wp_s.dtype)
        for src, width, dst in moves:
            wp_s[:, dst:dst + width] = w_ref[:, src:src + width]

    x = _residual_update(x_ref, ya_ref, yb_ref, wo_ref)
    x_out_ref[...] = x
    h = _rms(x, g_ref[...]).astype(BF16)
    for c in range(o_ref.shape[1] // chunk):
        o_ref[:, c * chunk:(c + 1) * chunk] = _dot(h, wp_s[:, c * chunk:(c + 1) * chunk])
    for ref, stage, (start, width) in zip(copy_refs, stage_refs, copies):
        stage[...] = o_ref[:, start:start + width]
        for t in range(group):
            ref[:, t * width:(t + 1) * width] = stage[pl.ds(t, ref.shape[0], stride=group), :]


def _layer_boundary(x2d, ya, yb, w_out, g, w, *, e, moves, tm=512, chunk, copies=(), group=1):
    m, d = x2d.shape
    gw = ya.shape[1]
    rows = lambda n, width: pl.BlockSpec((n, width), lambda i: (i, 0))
    const = lambda shape: pl.BlockSpec(shape, lambda i: (0, 0))
    return pl.pallas_call(
        functools.partial(_layer_boundary_kernel, chunk=chunk, copies=tuple(copies), group=group,
                          moves=tuple(moves)),
        grid=(m // tm,),
        in_specs=[rows(tm, d), rows(tm, gw), rows(tm, gw), const(w_out.shape), const((1, d)), const(w.shape)],
        out_specs=[rows(tm, d), rows(tm, e)] + [rows(tm // group, group * width) for _, width in copies],
        out_shape=[jax.ShapeDtypeStruct((m, d), F32), jax.ShapeDtypeStruct((m, e), F32)]
                  + [jax.ShapeDtypeStruct((m // group, group * width), F32) for _, width in copies],
        scratch_shapes=[pltpu.VMEM((tm, width), F32) for _, width in copies] + [pltpu.VMEM((d, e), BF16)],
        name="layer_boundary",
    )(x2d, ya, yb, w_out, g.reshape(1, d), w)


def _conv_rows(h_s, row0, rows, w_ref, b_ref, lg_ref, lb_ref):
    base = CONV_HALO - (CONV_WIDTH - 1)
    acc = None
    for b in range(SUBLANES):
        n = rows if b == 0 else rows + SUBLANES
        part = None
        for a in range((base + CONV_WIDTH - 1) // SUBLANES + 1):
            j = SUBLANES * a + b - base
            if 0 <= j < CONV_WIDTH:
                term = w_ref[j:j + 1, :] * h_s[pl.ds(row0 + SUBLANES * a, n), :]
                part = term if part is None else part + term
        part = part[b:b + rows]
        acc = part if acc is None else acc + part
    y = acc + b_ref[...]
    mu = jnp.mean(y, axis=-1, keepdims=True)
    yc = y - mu
    return _silu(yc * lax.rsqrt(jnp.mean(yc * yc, axis=-1, keepdims=True) + EPS) * lg_ref[...] + lb_ref[...])


def _even_layer_kernel(x_ref, g_ref, w_ref, gq_ref, gk_ref, cw_ref, cb_ref, lg_ref, lb_ref,
                       ya_ref, zb_ref, qt_ref, ka_ref, vt_ref, km_ref, wb_s, hb_s, conv_s, moba_s, h_s, tail_s,
                       *, seq, rows):
    tm = x_ref.shape[0]
    hd, gw = HEAD_DIM, GROUP_WIDTH
    tile_in_seq = pl.program_id(0) % (seq // tm)

    @pl.when(pl.program_id(0) == 0)
    def _():
        wb_s[...] = w_ref[...].astype(BF16)

    hb_s[...] = _rms(x_ref[...], g_ref[...]).astype(BF16)
    conv_s[...] = _dot(hb_s[...], wb_s[:, 0:3 * gw])
    h_s[0:CONV_HALO, :] = jnp.where(tile_in_seq > 0, tail_s[...], 0.0)
    h_s[CONV_HALO:, :] = conv_s[:, 0:gw] * _sigmoid(conv_s[:, gw:2 * gw])
    tail_s[...] = h_s[tm:tm + CONV_HALO, :]
    for c in range(tm // rows):
        y = _conv_rows(h_s, c * rows, rows, cw_ref, cb_ref, lg_ref, lb_ref)
        gate = _silu(conv_s[c * rows:(c + 1) * rows, 2 * gw:3 * gw])
        ya_ref[c * rows:(c + 1) * rows, :] = (y * gate).astype(ya_ref.dtype)

    moba_s[...] = _dot(hb_s[...], wb_s[:, 3 * gw:7 * gw])
    q_t = moba_s[:, 0:gw].T
    for j in range(N_HEADS):
        qt_ref[0, j * hd:(j + 1) * hd, :] = _norm_query_t(q_t[j * hd:(j + 1) * hd], gq_ref[...])
    pos = tile_in_seq * tm + _iota((tm, LANES), 0)
    for pair in range(N_HEADS // 2):
        kn = _pair_rms(moba_s[:, gw + pair * LANES:gw + (pair + 1) * LANES], gk_ref[...])
        ka_ref[0, 2 * pair], ka_ref[0, 2 * pair + 1] = _augment_pair_keys(kn, pos, MOBA_BLOCK.bit_length() - 1)
        km_ref[0, 0, :, pair * LANES:(pair + 1) * LANES] = jnp.mean(
            kn.reshape(tm // MOBA_BLOCK, MOBA_BLOCK, LANES), axis=1)
    v_t = moba_s[:, 2 * gw:3 * gw].T.astype(BF16)
    for c in range(tm // MOBA_BLOCK):
        vt_ref[0, c] = v_t[:, c * MOBA_BLOCK:(c + 1) * MOBA_BLOCK]
    zb_ref[...] = moba_s[:, 3 * gw:4 * gw]


def _even_layer_front(x2d, g, w, gq, gk, conv_w, conv_b, ln_g, ln_b, *, bsz, tm=512, rows=512):
    m, d = x2d.shape
    e = w.shape[1]
    s = m // bsz
    per_b = s // tm
    hd, gw, nb = HEAD_DIM, GROUP_WIDTH, tm // MOBA_BLOCK
    assert s % tm == 0 and tm % MOBA_BLOCK == 0 and e == 7 * gw and tm % rows == 0
    const = lambda shape: pl.BlockSpec(shape, lambda i: (0, 0))
    tile = pl.BlockSpec((tm, gw), lambda i: (i, 0))
    vec = const((1, gw))
    return pl.pallas_call(
        functools.partial(_even_layer_kernel, seq=s, rows=rows),
        grid=(m // tm,),
        in_specs=[pl.BlockSpec((tm, d), lambda i: (i, 0)), const((1, d)),
                  pl.BlockSpec((d, e), lambda i: (0, 0), pipeline_mode=pl.Buffered(1)),
                  const((hd, 1)), const((1, hd)), const((CONV_WIDTH, gw)), vec, vec, vec],
        out_specs=[tile, tile,
                   pl.BlockSpec((1, gw, tm), lambda i: (i // per_b, 0, i % per_b)),
                   pl.BlockSpec((1, N_HEADS, tm, LANES), lambda i: (i // per_b, 0, i % per_b, 0)),
                   pl.BlockSpec((1, nb, gw, MOBA_BLOCK), lambda i: (i // per_b, i % per_b, 0, 0)),
                   pl.BlockSpec((1, 1, nb, gw), lambda i: (i // per_b, i % per_b, 0, 0))],
        out_shape=[jax.ShapeDtypeStruct((m, gw), BF16),
                   jax.ShapeDtypeStruct((m, gw), F32),
                   jax.ShapeDtypeStruct((bsz, gw, s), F32),
                   jax.ShapeDtypeStruct((bsz, N_HEADS, s, LANES), BF16),
                   jax.ShapeDtypeStruct((bsz, s // MOBA_BLOCK, gw, MOBA_BLOCK), BF16),
                   jax.ShapeDtypeStruct((bsz, per_b, nb, gw), F32)],
        scratch_shapes=[pltpu.VMEM((d, e), BF16),
                        pltpu.VMEM((tm, d), BF16), pltpu.VMEM((tm, 3 * gw), F32), pltpu.VMEM((tm, 4 * gw), F32),
                        pltpu.VMEM((CONV_HALO + tm, gw), F32), pltpu.VMEM((CONV_HALO, gw), F32)],
        name="even_layer_front",
    )(x2d, g.reshape(1, d), w, gq.reshape(hd, 1), gk.reshape(1, hd), conv_w, conv_b.reshape(1, gw),
      ln_g.reshape(1, gw), ln_b.reshape(1, gw))


def _residual_update(x_ref, ya_ref, yb_ref, w_ref):
    return x_ref[...] + _dot(jnp.concatenate([ya_ref[...], yb_ref[...]], axis=1), w_ref[...])


def _outproj_kernel(x_ref, ya_ref, yb_ref, w_ref, o_ref):
    o_ref[...] = _residual_update(x_ref, ya_ref, yb_ref, w_ref)


def _outproj(x2d, ya, yb, w, *, tm=1024):
    m, d = x2d.shape
    gw = ya.shape[1]
    return pl.pallas_call(
        _outproj_kernel,
        grid=(m // tm,),
        in_specs=[pl.BlockSpec((tm, d), lambda i: (i, 0)),
                  pl.BlockSpec((tm, gw), lambda i: (i, 0)),
                  pl.BlockSpec((tm, gw), lambda i: (i, 0)),
                  pl.BlockSpec(w.shape, lambda i: (0, 0))],
        out_specs=pl.BlockSpec((tm, d), lambda i: (i, 0)),
        out_shape=jax.ShapeDtypeStruct((m, d), F32),
        name="outproj",
    )(x2d, ya, yb, w)


def _key_features(pos, flipped, sel_shift=None):
    col = _iota(pos.shape, 1) - (0 if flipped else HEAD_DIM)
    hi = (pos >> POS_SPLIT_SHIFT) << POS_SPLIT_SHIFT
    lo = pos & ((1 << POS_SPLIT_SHIFT) - 1)
    feat = jnp.where(col < SLOPE_PIECES, hi, jnp.where(col < 2 * SLOPE_PIECES, lo, 0))
    if sel_shift is not None:
        feat = jnp.where(col - SEL_FEAT0 == (pos >> sel_shift), 1, feat)
    return jnp.where((col >= 0) & (col < HEAD_DIM), feat, 0).astype(F32)


def _pair_rms(x, g):
    sq = x * x
    hi = sq.astype(BF16)
    lo = (sq - hi.astype(F32)).astype(BF16)
    head_shift = HEAD_DIM.bit_length() - 1
    same_head = (_iota((LANES, LANES), 0) >> head_shift) == (_iota((LANES, LANES), 1) >> head_shift)
    ones = jnp.where(same_head, 1.0, 0.0).astype(BF16)
    ss = _dot(hi, ones) + _dot(lo, ones)
    return x * lax.rsqrt(ss * (1.0 / HEAD_DIM) + EPS) * jnp.concatenate([g, g], axis=1)


def _augment_pair_keys(kn, pos, sel_shift=None):
    lane = _iota(kn.shape, 1)
    return [jnp.where(lane >= HEAD_DIM if flipped else lane < HEAD_DIM, kn,
                      _key_features(pos, flipped, sel_shift)).astype(BF16) for flipped in (False, True)]


def _norm_query_t(x, g_col):
    ss = jnp.mean(x * x, axis=0, keepdims=True)
    return x * lax.rsqrt(ss + EPS) * g_col * (HEAD_DIM ** -0.5 * LOG2E)


def _augment_query(q_t, alibi, sel_bias=None, flipped=False):
    n = q_t.shape[1]
    feats = [alibi]
    used = alibi.shape[0]
    if sel_bias is not None:
        feats.append(sel_bias)
        used += sel_bias.shape[0]
    feats.append(jnp.zeros((HEAD_DIM - used, n), F32))
    return jnp.concatenate(feats + [q_t] if flipped else [q_t] + feats, axis=0).astype(BF16)


ONES_ROWS = 16
ACC_ROWS = HEAD_DIM + ONES_ROWS


def _flash_init(n):
    return jnp.full((1, n), M_INIT, F32), jnp.zeros((ACC_ROWS, n), F32)


def _flash_step(carry, sc, v_t):
    m, acc = carry
    m_new = jnp.maximum(m, jnp.max(sc, axis=0, keepdims=True))
    p = jnp.exp2(sc - m_new).astype(BF16)
    v_ones = jnp.concatenate([v_t, jnp.ones((ONES_ROWS, v_t.shape[1]), BF16)], axis=0)
    return m_new, jnp.exp2(m - m_new) * acc + _dot(v_ones, p)


def _flash_split(carry):
    m, acc = carry
    return m, acc[:HEAD_DIM], acc[HEAD_DIM:HEAD_DIM + 1]


def _flash_steps(carries, chunks, cap=None):
    return tuple(_flash_step(carry, sc if cap is None else jnp.minimum(sc, cap), v_t)
                 for carry, (sc, v_t) in zip(carries, chunks))


def _flash_reset(state):
    m_ref, acc_ref = state
    m_ref[...] = jnp.full(m_ref.shape, M_INIT, F32)
    acc_ref[...] = jnp.zeros(acc_ref.shape, F32)


def _flash_update(state, slot, sc, v_t, cap=None):
    m_ref, acc_ref = state
    m_ref[slot], acc_ref[slot] = _flash_step((m_ref[slot], acc_ref[slot]),
                                             sc if cap is None else jnp.minimum(sc, cap), v_t)


def _flash_update_triangular(state, slot, sc_ref, v_t, cap, full_half, partial_lanes):
    m_ref, acc_ref = state
    half = sc_ref.shape[0] // 2
    full = slice(full_half * half, (full_half + 1) * half)
    part = slice((1 - full_half) * half, (2 - full_half) * half)
    _flash_update(state, slot, sc_ref[full, :], v_t[:, full], cap[full, :])

    def gather(x):
        return jnp.concatenate([x[..., a:b] for a, b in partial_lanes], axis=-1)

    sc = jnp.minimum(jnp.concatenate([sc_ref[part, a:b] for a, b in partial_lanes], axis=1), gather(cap[part, :]))
    m, acc = _flash_step((gather(m_ref[slot]), gather(acc_ref[slot])), sc, v_t[:, part])
    width = partial_lanes[0][1] - partial_lanes[0][0]
    for k, (a, b) in enumerate(partial_lanes):
        m_ref[slot, :, a:b] = m[:, k * width:(k + 1) * width]
        acc_ref[slot, :, a:b] = acc[:, k * width:(k + 1) * width]


def _half_lanes(tq, reps, upper):
    off = tq // 2 if upper else 0
    return [(k * tq + off, k * tq + off + tq // 2) for k in range(reps)]


def _flash_update_pipelined(state, bufs, slots, c, next_scores, values, cap=None, cap_slots=None,
                            triangular=None):
    def run(src_ref, dst_ref):
        for slot in slots:
            fresh = None if next_scores is None else next_scores(slot)
            masked = cap is not None and (cap_slots is None or slot in cap_slots)
            if masked and triangular is not None:
                _flash_update_triangular(state, slot, src_ref.at[slot], values(slot), cap, *triangular)
            else:
                _flash_update(state, slot, src_ref[slot], values(slot), cap if masked else None)
            if fresh is not None:
                dst_ref[slot] = fresh

    pl.when((c & 1) == 0)(lambda: run(bufs[0], bufs[1]))
    pl.when((c & 1) == 1)(lambda: run(bufs[1], bufs[0]))


def _flash_result(state, slot):
    m_ref, acc_ref = state
    m, out, den = _flash_split((m_ref[slot], acc_ref[slot]))
    return out / jnp.maximum(den, TINY)


def _flash_scratch(chains, n):
    return [pltpu.VMEM((chains, 1, n), F32), pltpu.VMEM((chains, ACC_ROWS, n), F32)]


def _rank_select(score, blk, limit, count):
    ranks = []
    for r0 in range(0, score.shape[0], SUBLANES):
        tile = score[r0:r0 + SUBLANES]
        tile_blk = r0 + _iota(tile.shape, 0)
        rank = jnp.zeros(tile.shape, jnp.int32)
        for mm in range(limit):
            gm = score[mm:mm + 1, :]
            if mm < r0:
                beats = gm >= tile
            elif mm >= r0 + SUBLANES:
                beats = gm > tile
            else:
                beats = (gm > tile) | ((gm == tile) & (mm < tile_blk))
            rank = rank + jnp.where(beats, 1, 0)
        ranks.append(rank)
    return jnp.concatenate(ranks, axis=0) < count


def _tile_lanes(x, reps):
    return jnp.concatenate([x] * reps, axis=1) if reps > 1 else x


def _causal_cap(tk, tq, reps):
    keep = _iota((tk, tq), 0) <= _iota((tk, tq), 1)
    return _tile_lanes(jnp.where(keep, BIG, -BIG), reps)


def _window_edge_cap(tk, tq, reps):
    keep = _iota((tk, tq), 0) > _iota((tk, tq), 1)
    return _tile_lanes(jnp.where(keep, BIG, -BIG), reps)


def _store_chunked_t(x, dst_ref, tk):
    x_t = x.T.astype(BF16)
    for c in range(x.shape[0] // tk):
        dst_ref[c] = x_t[:, c * tk:(c + 1) * tk]


def _moba_kernel(alibi_ref, qt_ref, ka_ref, vt_ref, km_ref, z_ref, o_ref, m_s, acc_s, sc0_s, sc1_s):
    i = pl.program_id(1)
    nblk = vt_ref.shape[1]
    tq = MOBA_BLOCK
    hd = HEAD_DIM
    chains = [(r, j) for r in range(qt_ref.shape[0]) for j in range(N_HEADS)]
    slots = range(len(chains))

    blk = _iota((nblk, tq), 0)
    qas = []
    for r, j in chains:
        qn = qt_ref[r, j * hd:(j + 1) * hd, :]
        km = jnp.concatenate([km_ref[r, t, :, j * hd:(j + 1) * hd] for t in range(km_ref.shape[1])], axis=0)
        gate = jnp.where(blk < i, _dot_fine(km, qn), NEG)
        chosen = _rank_select(gate, blk, nblk, MOBA_TOPK) & (blk < i)
        sel_bias = jnp.where(chosen | (blk == i), 0.0, -BIG)
        qas.append(_augment_query(qn, alibi_ref[j], sel_bias, flipped=j % 2 == 1))

    def scores(slot, n):
        r, j = chains[slot]
        return _dot(ka_ref[r, j, pl.ds(pl.multiple_of(n * tq, tq), tq), :], qas[slot])

    def values(slot, n):
        r, j = chains[slot]
        return vt_ref[r, n, j * hd:(j + 1) * hd, :]

    state = (m_s, acc_s)
    _flash_reset(state)
    bufs = (sc0_s, sc1_s)
    for slot in slots:
        sc0_s[slot] = scores(slot, 0)

    def body(n, carry):
        _flash_update_pipelined(state, bufs, slots, n, lambda slot: scores(slot, n + 1),
                                lambda slot: values(slot, n))
        return carry

    lax.fori_loop(0, i, body, 0)
    diag_cap = _causal_cap(tq, tq, 1)
    _flash_update_pipelined(state, bufs, slots, i, None, lambda slot: values(slot, i), diag_cap,
                            triangular=(0, _half_lanes(tq, 1, upper=True)))
    for r in range(qt_ref.shape[0]):
        outs = [_flash_result(state, r * N_HEADS + j) for j in range(N_HEADS)]
        y = jnp.concatenate(outs, axis=0).T * _silu(z_ref[r])
        o_ref[r] = y.astype(o_ref.dtype)


def _moba(z, q_t, ka, vt, km, *, rows_per_step=2):
    bsz, s, _ = z.shape
    tq = MOBA_BLOCK
    gw = GROUP_WIDTH
    nblk = s // MOBA_BLOCK
    rps = rows_per_step
    assert SEL_FEAT0 + nblk <= HEAD_DIM and bsz % rps == 0
    whole = lambda a: pl.BlockSpec((rps,) + a.shape[1:], lambda b, i: (b,) + (0,) * (a.ndim - 1))
    return pl.pallas_call(
        _moba_kernel,
        grid=(bsz // rps, s // tq),
        in_specs=[pl.BlockSpec((N_HEADS, 2 * SLOPE_PIECES, tq), lambda b, i: (0, 0, 0)),
                  pl.BlockSpec((rps, gw, tq), lambda b, i: (b, 0, i)), whole(ka), whole(vt), whole(km),
                  pl.BlockSpec((rps, tq, gw), lambda b, i: (b, i, 0))],
        out_specs=pl.BlockSpec((rps, tq, gw), lambda b, i: (b, i, 0)),
        out_shape=jax.ShapeDtypeStruct((bsz, s, gw), BF16),
        scratch_shapes=_flash_scratch(rps * N_HEADS, tq) + [pltpu.VMEM((rps * N_HEADS, tq, tq), F32)] * 2,
        name="moba",
    )(jnp.asarray(_alibi_query_features(tq)), q_t, ka, vt, km, z)


def _cmp_kernel(xk_ref, xv_ref, wk1_ref, wv1_ref, pk_ref, pv_ref, bk1_ref, bv1_ref,
                wk2_ref, wv2_ref, bk2_ref, bv2_ref, gn_ref, ko_ref, vo_ref, wke_s, wve_s):
    half = xk_ref.shape[2]
    n = xk_ref.shape[1]
    hd = HEAD_DIM
    nh = NSA_CMP_HIDDEN

    @pl.when(pl.program_id(0) == 0)
    def _():
        for w_ref, we_ref in ((wk1_ref, wke_s), (wv1_ref, wve_s)):
            we_ref[...] = jnp.zeros(we_ref.shape, BF16)
            for tok in range(NSA_CMP_LEN):
                for g in range(KV_HEADS):
                    row = (tok * KV_HEADS + g) * hd
                    we_ref[row:row + hd, g * nh:(g + 1) * nh] = w_ref[tok * hd:(tok + 1) * hd, :]

    rows = xk_ref.shape[0]

    def hidden(x_ref, we_ref, pos_ref, b1_ref):
        x = x_ref[...].reshape(rows * n, half).astype(BF16)
        first = _dot(x, we_ref[:half, :])
        second = _dot(x, we_ref[half:, :])
        second = jnp.concatenate([pltpu.roll(second[r * n:(r + 1) * n], n - 1, 0) for r in range(rows)], axis=0)
        pos = jnp.broadcast_to(pos_ref[...], (8, 2 * half)).astype(BF16)
        ph = _dot(pos[:, :half], we_ref[:half, :]) + _dot(pos[:, half:], we_ref[half:, :])
        return _silu(first + second + ph[0:1, :] + b1_ref[...])

    hk = hidden(xk_ref, wke_s, pk_ref, bk1_ref).astype(BF16)
    hv = hidden(xv_ref, wve_s, pv_ref, bv1_ref).astype(BF16)
    ks, vs = [], []
    for g in range(KV_HEADS):
        ks.append(_rms(_dot(hk[:, g * nh:(g + 1) * nh], wk2_ref[...]) + bk2_ref[...], gn_ref[...]))
        vs.append(_dot(hv[:, g * nh:(g + 1) * nh], wv2_ref[...]) + bv2_ref[...])
    ko_ref[...] = jnp.concatenate(ks, axis=1).reshape(ko_ref.shape)
    vo_ref[...] = jnp.concatenate(vs, axis=1).reshape(vo_ref.shape)


def _compress(xk, xv, pos_k, pos_v, kw1, kb1, kw2, kb2, vw1, vb1, vw2, vb2, gn, *, rows_per_step=4):
    bsz, n, half = xk.shape
    hd = HEAD_DIM
    rps = rows_per_step
    assert bsz % rps == 0
    tile_pos = lambda p: jnp.tile(p[:, None, :], (1, KV_HEADS, 1)).reshape(1, 2 * half)
    tile_b = lambda b: jnp.tile(b.reshape(1, -1), (1, KV_HEADS))
    const = lambda shape: pl.BlockSpec(shape, lambda b: (0,) * len(shape))
    xs = pl.BlockSpec((rps, n, half), lambda b: (b, 0, 0))
    hw = KV_HEADS * NSA_CMP_HIDDEN
    out = pl.BlockSpec((rps, n, KV_HEADS * hd), lambda b: (b, 0, 0))
    expanded = pltpu.VMEM((2 * half, hw), BF16)
    return pl.pallas_call(
        _cmp_kernel,
        grid=(bsz // rps,),
        in_specs=[xs, xs, const(kw1.shape), const(vw1.shape),
                  const((1, 2 * half)), const((1, 2 * half)), const((1, hw)), const((1, hw)),
                  const((NSA_CMP_HIDDEN, hd)), const((NSA_CMP_HIDDEN, hd)), const((1, hd)), const((1, hd)),
                  const((1, hd))],
        out_specs=[out, out],
        out_shape=[jax.ShapeDtypeStruct((bsz, n, KV_HEADS * hd), F32)] * 2,
        scratch_shapes=[expanded, expanded],
        name="nsa_compress",
    )(xk, xv, kw1.astype(BF16), vw1.astype(BF16), tile_pos(pos_k), tile_pos(pos_v), tile_b(kb1), tile_b(vb1),
      kw2.astype(BF16), vw2.astype(BF16), kb2.reshape(1, hd), vb2.reshape(1, hd), gn.reshape(1, hd))


def _group_query(q_t, alibi_ref, g, gq_col):
    hd = HEAD_DIM
    heads = range(Q_PER_KV * g, Q_PER_KV * (g + 1))
    qn = jnp.concatenate([_norm_query_t(q_t[h * hd:(h + 1) * hd], gq_col) for h in heads], axis=1)
    alibi = jnp.concatenate([alibi_ref[h] for h in heads], axis=1)
    return qn, alibi


def _augment_keys(k, gk_ref, dst_ref, row=0, sel_shift=None):
    assert KV_HEADS == 2
    pos = _iota(k.shape, 0)
    dst_ref[2 * row], dst_ref[2 * row + 1] = _augment_pair_keys(_pair_rms(k, gk_ref[...]), pos, sel_shift)


def _chunk_scores(ka_ref, vt_ref, g, c, qa, tq):
    off = pl.multiple_of(c * tq, tq)
    return _dot(ka_ref[g, pl.ds(off, tq), :], qa), vt_ref[c, g * HEAD_DIM:(g + 1) * HEAD_DIM, :]


def _window_first_chunk(ka_ref, vt_ref, g, far, qa, tq, edge_cap):
    sc, v_t = _chunk_scores(ka_ref, vt_ref, g, jnp.maximum(far, 0), qa, tq)
    return jnp.minimum(sc, jnp.minimum(edge_cap, jnp.where(far >= 0, BIG, -BIG))), v_t


def _regroup_heads(o_t, tq):
    return jnp.concatenate([o_t[:, k * tq:(k + 1) * tq] for k in range(Q_PER_KV)], axis=0)


def _nsa_kernel(alibi_ref, q_ref, kc_ref, vc_ref, ks_ref, vs_ref, kw_ref, vw_ref, gate_ref, z_ref,
                gq_ref, gks_ref, gkw_ref, o_ref, kca_s, vct_s, ksa_s, vst_s, kwa_s, vwt_s, m_s, acc_s,
                sc0_s, sc1_s):
    i = pl.program_id(1)
    rows = range(q_ref.shape[0])
    tq = q_ref.shape[1]
    s = ks_ref.shape[1]
    hd = HEAD_DIM
    ncmp = kc_ref.shape[1]
    nsb = s // NSA_SEL_BLOCK

    @pl.when(i == 0)
    def _():
        cend = _iota((ncmp, LANES), 0) * NSA_CMP_STRIDE + (NSA_CMP_LEN - 1)
        for r in rows:
            _augment_keys(ks_ref[r], gks_ref, ksa_s, r, NSA_SEL_BLOCK.bit_length() - 1)
            _augment_keys(kw_ref[r], gkw_ref, kwa_s, r)
            _store_chunked_t(vs_ref[r], vst_s.at[r], tq)
            _store_chunked_t(vw_ref[r], vwt_s.at[r], tq)
            kca_s[2 * r], kca_s[2 * r + 1] = _augment_pair_keys(kc_ref[r], cend)
            vct_s[r] = vc_ref[r].T.astype(BF16)

    t0 = i * tq
    diag_cap = _causal_cap(tq, tq, Q_PER_KV)
    edge_cap = _window_edge_cap(tq, tq, Q_PER_KV)
    cend =_iota((ncmp, tq), 0) * NSA_CMP_STRIDE + (NSA_CMP_LEN - 1)
    seen_cap = _tile_lanes(jnp.where(cend <= t0 + _iota((ncmp, tq), 1), BIG, -BIG), Q_PER_KV)
    cstart = _iota((nsb, ncmp), 1) * NSA_CMP_STRIDE
    bstart = _iota((nsb, ncmp), 0) * NSA_SEL_BLOCK
    overlap_t = jnp.where((cstart < bstart + NSA_SEL_BLOCK) & (cstart + NSA_CMP_LEN > bstart), 1.0, 0.0)
    blk = _iota((nsb, tq), 0)
    cur = (t0 + _iota((nsb, tq), 1)) >> (NSA_SEL_BLOCK.bit_length() - 1)
    forced = (blk == 0) | (blk == cur) | (blk == cur - 1)
    groups = range(KV_HEADS)
    far = i - NSA_WINDOW // tq
    lo = jnp.maximum(far + 1, 0)
    first_win = jnp.maximum(far, 0)

    state = (m_s, acc_s)
    kv = [(r, g) for r in rows for g in groups]
    sel_slots = tuple(range(len(kv)))
    win_slots = tuple(len(kv) + n for n in range(len(kv)))
    _flash_reset(state)
    q_ts = [q_ref[r].T for r in rows]
    queries = [_group_query(q_ts[r], alibi_ref, g, gq_ref[...]) for r, g in kv]
    qa_win = [_augment_query(qn, alibi, flipped=g == 1) for (r, g), (qn, alibi) in zip(kv, queries)]
    qa_sel = []

    def scores(slot, c):
        n = slot % len(kv)
        ka_ref, qa = (ksa_s, qa_sel) if slot in sel_slots else (kwa_s, qa_win)
        return _dot(ka_ref[n, pl.ds(pl.multiple_of(c * tq, tq), tq), :], qa[n])

    def values(slot, c):
        r, g = kv[slot % len(kv)]
        return (vst_s if slot in sel_slots else vwt_s)[r, c, g * hd:(g + 1) * hd, :]

    cmp_sc = [_dot(kca_s[n], qa_win[n]) for n in range(len(kv))]
    sc_win_first = [scores(slot, first_win) for slot in win_slots]
    o_cmp = []
    for n, (r, g) in enumerate(kv):
        qn, alibi = queries[n]
        sc = jnp.minimum(cmp_sc[n], seen_cap)
        top = jnp.max(sc, axis=0, keepdims=True)
        e = jnp.exp2(sc - top)
        inv = jnp.where(top > -0.5 * BIG, 1.0 / jnp.maximum(jnp.sum(e, axis=0, keepdims=True), TINY), 0.0)
        o_cmp.append(_dot(vct_s[r, g * hd:(g + 1) * hd, :], e.astype(BF16)) * inv)

        psum = e[:, 0:tq] * inv[:, 0:tq]
        for k in range(1, Q_PER_KV):
            psum = psum + e[:, k * tq:(k + 1) * tq] * inv[:, k * tq:(k + 1) * tq]
        imp = _dot_fine(overlap_t, psum, a_is_bf16_exact=True)
        imp = jnp.where(blk <= cur, jnp.where(forced, NSA_FORCE, imp), NEG)
        chosen = _rank_select(imp, blk, nsb, min(NSA_TOPN, nsb))
        sel_bias = _tile_lanes(jnp.where(chosen, 0.0, -BIG), Q_PER_KV)
        qa_sel.append(_augment_query(qn, alibi, sel_bias, flipped=g == 1))

    bufs = (sc0_s, sc1_s)
    for slot in sel_slots:
        sc0_s[slot] = scores(slot, 0)
    for slot, sc in zip(win_slots, sc_win_first):
        sc0_s[slot] = sc
        sc1_s[slot] = sc

    def body(slots, cap, cap_slots, triangular, c, carry):
        _flash_update_pipelined(state, bufs, slots, c, lambda slot: scores(slot, c + 1),
                                lambda slot: values(slot, c), cap, cap_slots, triangular)
        return carry

    both_slots = sel_slots + win_slots
    edge_shape = (1, _half_lanes(tq, Q_PER_KV, upper=False))
    diag_shape = (0, _half_lanes(tq, Q_PER_KV, upper=True))
    lax.fori_loop(0, first_win, functools.partial(body, sel_slots, None, None, None), 0)
    lax.fori_loop(first_win, lo, functools.partial(body, both_slots, edge_cap, win_slots, edge_shape), 0)
    lax.fori_loop(lo, i, functools.partial(body, both_slots, None, None, None), 0)
    _flash_update_pipelined(state, bufs, both_slots, i, None, lambda slot: values(slot, i), diag_cap,
                            triangular=diag_shape)

    for r in rows:
        gates_t = _sigmoid(gate_ref[r]).T
        outs = []
        for g in groups:
            def gate_row(branch):
                parts = [gates_t[branch * N_HEADS + h:branch * N_HEADS + h + 1, :]
                         for h in range(Q_PER_KV * g, Q_PER_KV * (g + 1))]
                return jnp.concatenate(parts, axis=1)

            n = KV_HEADS * r + g
            o_slc = _flash_result(state, sel_slots[n])
            o_win = _flash_result(state, win_slots[n])
            o = gate_row(0) * o_cmp[n] + gate_row(1) * o_slc + gate_row(2) * o_win
            outs.append(_regroup_heads(o, tq))
        y = jnp.concatenate(outs, axis=0).T * _silu(z_ref[r])
        o_ref[r] = y.astype(o_ref.dtype)


def _nsa(proj, kcn, vc, gq, gks, gkw, *, tq=256, rows_per_step=2):
    bsz, s, _ = proj.shape
    hd = HEAD_DIM
    gw = GROUP_WIDTH
    ncmp = kcn.shape[1]
    rps = rows_per_step
    chains = 2 * KV_HEADS * rps
    assert SEL_FEAT0 + s // NSA_SEL_BLOCK <= hd and ncmp == LANES and bsz % rps == 0
    wide = lambda c0: pl.BlockSpec((rps, tq, gw), lambda b, i: (b, i, c0 // 4))
    full = lambda c0: pl.BlockSpec((rps, s, LANES), lambda b, i: (b, 0, c0))
    cmp = pl.BlockSpec((rps, ncmp, LANES), lambda b, i: (b, 0, 0))
    row = pl.BlockSpec((1, hd), lambda b, i: (0, 0))
    ka = pltpu.VMEM((KV_HEADS * rps, s, LANES), BF16)
    vt = pltpu.VMEM((rps, s // tq, LANES, tq), BF16)
    return pl.pallas_call(
        _nsa_kernel,
        grid=(bsz // rps, s // tq),
        in_specs=[pl.BlockSpec((N_HEADS, 2 * SLOPE_PIECES, tq), lambda b, i: (0, 0, 0)),
                  wide(ODD_QC), cmp, cmp, full(ODD_KS), full(ODD_VS), full(ODD_KW), full(ODD_VW),
                  pl.BlockSpec((rps, tq, LANES), lambda b, i: (b, i, ODD_GC)), wide(ODD_ZC),
                  pl.BlockSpec((hd, 1), lambda b, i: (0, 0)), row, row],
        out_specs=pl.BlockSpec((rps, tq, gw), lambda b, i: (b, i, 0)),
        out_shape=jax.ShapeDtypeStruct((bsz, s, gw), BF16),
        scratch_shapes=[pltpu.VMEM((KV_HEADS * rps, ncmp, LANES), BF16), pltpu.VMEM((rps, LANES, ncmp), BF16),
                        ka, vt, ka, vt] + _flash_scratch(chains, Q_PER_KV * tq)
                       + [pltpu.VMEM((chains, tq, Q_PER_KV * tq), F32)] * 2,
        name="nsa",
    )(jnp.asarray(_alibi_query_features(tq)), proj, kcn, vc, proj, proj, proj, proj, proj, proj,
      gq.reshape(hd, 1), gks.reshape(1, hd), gkw.reshape(1, hd))


def _swa_kernel(sinks_ref, alibi_ref, q_ref, k_ref, v_ref, z_ref, gq_ref, gk_ref, o_ref, ka_s, vt_s):
    step = pl.program_id(1)
    tq = SWA_WINDOW
    tiles = q_ref.shape[1] // tq
    s = k_ref.shape[1]

    @pl.when(step == 0)
    def _():
        _augment_keys(k_ref[0], gk_ref, ka_s)
        _store_chunked_t(v_ref[0], vt_s, tq)

    q_t = q_ref[0].T
    diag_cap = _causal_cap(tq, tq, Q_PER_KV)
    edge_cap = _window_edge_cap(tq, tq, Q_PER_KV)
    chains = [(u, g) for u in range(tiles) for g in range(KV_HEADS)]
    qas = [_augment_query(*_group_query(q_t[:, u * tq:(u + 1) * tq], alibi_ref, g, gq_ref[...]), flipped=g == 1)
           for u, g in chains]
    first = [_window_first_chunk(ka_s, vt_s, g, step * tiles + u - 1, qa, tq, edge_cap)
             for (u, g), qa in zip(chains, qas)]
    own = [_chunk_scores(ka_s, vt_s, g, step * tiles + u, qa, tq) for (u, g), qa in zip(chains, qas)]
    carries = _flash_steps([_flash_init(Q_PER_KV * tq) for _ in chains], first)
    carries = _flash_steps(carries, own, diag_cap)
    outs = []
    for (u, g), carry in zip(chains, carries):
        m, acc, l = _flash_split(carry)
        tpos = ((step * tiles + u) * tq + _iota((1, tq), 1)).astype(F32)
        sink = jnp.concatenate([sinks_ref[h] * LOG2E + (SLOPES[h] * LOG2E) * tpos
                                for h in range(Q_PER_KV * g, Q_PER_KV * (g + 1))], axis=1)
        mf = jnp.maximum(m, sink)
        alpha = jnp.exp2(m - mf)
        o = acc * alpha / jnp.maximum(l * alpha + jnp.exp2(sink - mf), TINY)
        outs.append(_regroup_heads(o, tq))
    o_t = jnp.concatenate([jnp.concatenate(outs[u * KV_HEADS:(u + 1) * KV_HEADS], axis=0) for u in range(tiles)],
                          axis=1)
    y = o_t.T * _silu(z_ref[0])
    o_ref[0] = y.astype(o_ref.dtype)


def _swa(proj, sinks, gq, gk, *, tiles_per_step=16):
    bsz, s, _ = proj.shape
    hd = HEAD_DIM
    gw = GROUP_WIDTH
    tq = SWA_WINDOW
    rows = tiles_per_step * tq
    wide = lambda c0: pl.BlockSpec((1, rows, gw), lambda b, i: (b, i, c0 // 4))
    full = lambda c0: pl.BlockSpec((1, s, LANES), lambda b, i: (b, 0, c0))
    return pl.pallas_call(
        _swa_kernel,
        grid=(bsz, s // rows),
        in_specs=[pl.BlockSpec(memory_space=pltpu.SMEM),
                  pl.BlockSpec((N_HEADS, 2 * SLOPE_PIECES, tq), lambda b, i: (0, 0, 0)),
                  wide(ODD_QD), full(ODD_KD), full(ODD_VD), wide(ODD_ZD),
                  pl.BlockSpec((hd, 1), lambda b, i: (0, 0)), pl.BlockSpec((1, hd), lambda b, i: (0, 0))],
        out_specs=pl.BlockSpec((1, rows, gw), lambda b, i: (b, i, 0)),
        out_shape=jax.ShapeDtypeStruct((bsz, s, gw), BF16),
        scratch_shapes=[pltpu.VMEM((KV_HEADS, s, LANES), BF16), pltpu.VMEM((s // tq, LANES, tq), BF16)],
        name="swa",
    )(sinks.astype(F32), jnp.asarray(_alibi_query_features(tq)), proj, proj, proj, proj,
      gq.reshape(hd, 1), gk.reshape(1, hd))


def _odd_weight_moves():
    gw, kw, ng = GROUP_WIDTH, KV_HEADS * HEAD_DIM, 3 * N_HEADS
    sizes = [gw, kw, kw, kw, kw, kw, kw, ng, gw, gw, kw, kw, gw]
    starts = np.concatenate([[0], np.cumsum(sizes)]).tolist()
    order = [0, 8, 9, 12, 1, 2, 3, 4, 5, 6, 10, 11, 7]
    moves, dst = [], 0
    for k in order:
        moves.append((starts[k], sizes[k], dst))
        dst += sizes[k]
    assert dst + LANES - ng == ODD_COLS
    return moves


def kernel(x, norm_g, w_out, e_w_in, a_conv_w, a_conv_b, a_ln_g, a_ln_b, b_qnorm_g, b_knorm_g, o_w_in, c_qnorm_g, c_knorm_cmp_g, c_knorm_slc_g, c_knorm_win_g, c_pos_k, c_pos_v, c_k_w1, c_k_b1, c_k_w2, c_k_b2, c_v_w1, c_v_b1, c_v_w2, c_v_b2, d_qnorm_g, d_knorm_g, d_sinks):
    bsz, s, d = x.shape
    m = bsz * s
    assert s % MOBA_BLOCK == 0 and d == 2 * GROUP_WIDTH
    x2 = x.reshape(m, d)

    y_a, z_b, q_t, ka, vt, km = _even_layer_front(
        x2, norm_g[0], e_w_in[0], b_qnorm_g[0], b_knorm_g[0],
        a_conv_w[0], a_conv_b[0], a_ln_g[0], a_ln_b[0], bsz=bsz)
    y_b = _moba(z_b.reshape(bsz, s, -1), q_t, ka, vt, km)

    x2, proj, k_cmp, v_cmp = _layer_boundary(
        x2, y_a, y_b.reshape(m, -1), w_out[0].astype(BF16), norm_g[1], o_w_in[0].astype(BF16),
        e=ODD_COLS, moves=_odd_weight_moves(), chunk=ODD_COLS,
        copies=((ODD_KC * LANES, LANES), (ODD_VC * LANES, LANES)), group=NSA_CMP_STRIDE)
    proj = proj.reshape(bsz, s, ODD_COLS)
    rows16 = lambda t: t.reshape(bsz, s // NSA_CMP_STRIDE, NSA_CMP_STRIDE * LANES)
    kcn, vc = _compress(rows16(k_cmp), rows16(v_cmp), c_pos_k[0], c_pos_v[0], c_k_w1[0], c_k_b1[0],
                        c_k_w2[0], c_k_b2[0], c_v_w1[0], c_v_b1[0], c_v_w2[0], c_v_b2[0], c_knorm_cmp_g[0])
    y_c = _nsa(proj, kcn, vc, c_qnorm_g[0], c_knorm_slc_g[0], c_knorm_win_g[0])
    y_d = _swa(proj, d_sinks[0], d_qnorm_g[0], d_knorm_g[0])
    x2 = _outproj(x2, y_c.reshape(m, -1), y_d.reshape(m, -1), w_out[1].astype(BF16))
    return x2.reshape(bsz, s, d)
```

```python
import functools

import ml_dtypes
import numpy as np
import jax
import jax.numpy as jnp
from jax import lax
from jax.experimental import pallas as pl
from jax.experimental.pallas import tpu as pltpu

HEAD_DIM = 64
N_HEADS = 8
GROUP_WIDTH = N_HEADS * HEAD_DIM
CONV_WIDTH = 31
MOBA_BLOCK = 256
MOBA_TOPK = 3
KV_HEADS = 2
Q_PER_KV = N_HEADS // KV_HEADS
NSA_CMP_LEN = 32
NSA_CMP_STRIDE = 16
NSA_CMP_HIDDEN = 256
NSA_SEL_BLOCK = 64
NSA_TOPN = 16
NSA_WINDOW = 512
NSA_FORCE = 1e4
SWA_WINDOW = 128
EPS = 1e-6
NEG = -1e30
TINY = 1e-30
LANES = 128
SUBLANES = 8
CONV_HALO = 32

LOG2E = float(np.log2(np.e))
BIG = 2.0 ** 99
M_INIT = -1e38
POS_SPLIT_SHIFT = 8
SLOPE_PIECES = 4
SEL_FEAT0 = 2 * SLOPE_PIECES

F32 = jnp.float32
BF16 = jnp.bfloat16

ODD_QC, ODD_ZC, ODD_QD, ODD_ZD = 0, 4, 8, 12
ODD_KC, ODD_VC, ODD_KS, ODD_VS, ODD_KW, ODD_VW, ODD_KD, ODD_VD, ODD_GC = 16, 17, 18, 19, 20, 21, 22, 23, 24
ODD_COLS = 25 * LANES


def _alibi_slopes(n):
    return [float(2.0 ** (-8.0 * (i + 1) / n)) for i in range(n)]


SLOPES = _alibi_slopes(N_HEADS)


def _alibi_query_features(width):
    table = np.zeros((N_HEADS, 2 * SLOPE_PIECES, width), np.float32)
    for h, slope in enumerate(SLOPES):
        rest = np.float64(slope) * LOG2E
        for k in range(SLOPE_PIECES):
            piece = float(np.float32(rest).astype(ml_dtypes.bfloat16).astype(np.float32))
            table[h, k] = piece
            table[h, SLOPE_PIECES + k] = piece
            rest -= piece
    return table


def _dot(a, b):
    return jnp.dot(a, b, preferred_element_type=F32)


def _split_bf16(x, pieces):
    out = []
    for _ in range(pieces):
        out.append(x.astype(BF16))
        x = x - out[-1].astype(F32)
    return out


def _dot_fine(a, b, a_is_bf16_exact=False):
    if a_is_bf16_exact:
        return sum(_dot(a.astype(BF16), piece) for piece in _split_bf16(b, 3))
    (a_hi, a_lo), (b_hi, b_lo) = _split_bf16(a, 2), _split_bf16(b, 2)
    return _dot(a_hi, b_hi) + (_dot(a_hi, b_lo) + _dot(a_lo, b_hi))


def _sigmoid(x):
    return 1.0 / (1.0 + jnp.exp(-x))


def _silu(x):
    return x * _sigmoid(x)


def _rms(x, g):
    return x * lax.rsqrt(jnp.mean(x * x, axis=-1, keepdims=True) + EPS) * g


def _iota(shape, dim):
    return lax.broadcasted_iota(jnp.int32, shape, dim)


def _layer_boundary_kernel(x_ref, ya_ref, yb_ref, wo_ref, g_ref, w_ref, x_out_ref, o_ref, *rest,
                           chunk, copies, group, moves):
    copy_refs, stage_refs, wp_s = rest[:len(copies)], rest[len(copies):-1], rest[-1]

    @pl.when(pl.program_id(0) == 0)
    def _():
        wp_s[...] = jnp.zeros(wp_s.shape, wp_s.dtype)
        for src, width, dst in moves:
            wp_s[:, dst:dst + width] = w_ref[:, src:src + width]

    x = _residual_update(x_ref, ya_ref, yb_ref, wo_ref)
    x_out_ref[...] = x
    h = _rms(x, g_ref[...]).astype(BF16)
    for c in range(o_ref.shape[1] // chunk):
        o_ref[:, c * chunk:(c + 1) * chunk] = _dot(h, wp_s[:, c * chunk:(c + 1) * chunk])
    for ref, stage, (start, width) in zip(copy_refs, stage_refs, copies):
        stage[...] = o_ref[:, start:start + width]
        for t in range(group):
            ref[:, t * width:(t + 1) * width] = stage[pl.ds(t, ref.shape[0], stride=group), :]


def _layer_boundary(x2d, ya, yb, w_out, g, w, *, e, moves, tm=512, chunk, copies=(), group=1):
    m, d = x2d.shape
    gw = ya.shape[1]
    rows = lambda n, width: pl.BlockSpec((n, width), lambda i: (i, 0))
    const = lambda shape: pl.BlockSpec(shape, lambda i: (0, 0))
    return pl.pallas_call(
        functools.partial(_layer_boundary_kernel, chunk=chunk, copies=tuple(copies), group=group,
                          moves=tuple(moves)),
        grid=(m // tm,),
        in_specs=[rows(tm, d), rows(tm, gw), rows(tm, gw), const(w_out.shape), const((1, d)), const(w.shape)],
        out_specs=[rows(tm, d), rows(tm, e)] + [rows(tm // group, group * width) for _, width in copies],
        out_shape=[jax.ShapeDtypeStruct((m, d), F32), jax.ShapeDtypeStruct((m, e), F32)]
                  + [jax.ShapeDtypeStruct((m // group, group * width), F32) for _, width in copies],
        scratch_shapes=[pltpu.VMEM((tm, width), F32) for _, width in copies] + [pltpu.VMEM((d, e), BF16)],
        name="layer_boundary",
    )(x2d, ya, yb, w_out, g.reshape(1, d), w)


def _conv_rows(h_s, row0, rows, w_ref, b_ref, lg_ref, lb_ref):
    base = CONV_HALO - (CONV_WIDTH - 1)
    acc = None
    for b in range(SUBLANES):
        n = rows if b == 0 else rows + SUBLANES
        part = None
        for a in range((base + CONV_WIDTH - 1) // SUBLANES + 1):
            j = SUBLANES * a + b - base
            if 0 <= j < CONV_WIDTH:
                term = w_ref[j:j + 1, :] * h_s[pl.ds(row0 + SUBLANES * a, n), :]
                part = term if part is None else part + term
        part = part[b:b + rows]
        acc = part if acc is None else acc + part
    y = acc + b_ref[...]
    mu = jnp.mean(y, axis=-1, keepdims=True)
    yc = y - mu
    return _silu(yc * lax.rsqrt(jnp.mean(yc * yc, axis=-1, keepdims=True) + EPS) * lg_ref[...] + lb_ref[...])


def _even_layer_kernel(x_ref, g_ref, w_ref, gq_ref, gk_ref, cw_ref, cb_ref, lg_ref, lb_ref,
                       ya_ref, zb_ref, qt_ref, ka_ref, vt_ref, km_ref, wb_s, hb_s, conv_s, moba_s, h_s, tail_s,
                       *, seq, rows):
    tm = x_ref.shape[0]
    hd, gw = HEAD_DIM, GROUP_WIDTH
    tile_in_seq = pl.program_id(0) % (seq // tm)

    @pl.when(pl.program_id(0) == 0)
    def _():
        wb_s[...] = w_ref[...].astype(BF16)

    hb_s[...] = _rms(x_ref[...], g_ref[...]).astype(BF16)
    conv_s[...] = _dot(hb_s[...], wb_s[:, 0:3 * gw])
    h_s[0:CONV_HALO, :] = jnp.where(tile_in_seq > 0, tail_s[...], 0.0)
    h_s[CONV_HALO:, :] = conv_s[:, 0:gw] * _sigmoid(conv_s[:, gw:2 * gw])
    tail_s[...] = h_s[tm:tm + CONV_HALO, :]
    for c in range(tm // rows):
        y = _conv_rows(h_s, c * rows, rows, cw_ref, cb_ref, lg_ref, lb_ref)
        gate = _silu(conv_s[c * rows:(c + 1) * rows, 2 * gw:3 * gw])
        ya_ref[c * rows:(c + 1) * rows, :] = (y * gate).astype(ya_ref.dtype)

    moba_s[...] = _dot(hb_s[...], wb_s[:, 3 * gw:7 * gw])
    q_t = moba_s[:, 0:gw].T
    for j in range(N_HEADS):
        qt_ref[0, j * hd:(j + 1) * hd, :] = _norm_query_t(q_t[j * hd:(j + 1) * hd], gq_ref[...])
    pos = tile_in_seq * tm + _iota((tm, LANES), 0)
    for pair in range(N_HEADS // 2):
        kn = _pair_rms(moba_s[:, gw + pair * LANES:gw + (pair + 1) * LANES], gk_ref[...])
        ka_ref[0, 2 * pair], ka_ref[0, 2 * pair + 1] = _augment_pair_keys(kn, pos, MOBA_BLOCK.bit_length() - 1)
        km_ref[0, 0, :, pair * LANES:(pair + 1) * LANES] = jnp.mean(
            kn.reshape(tm // MOBA_BLOCK, MOBA_BLOCK, LANES), axis=1)
    v_t = moba_s[:, 2 * gw:3 * gw].T.astype(BF16)
    for c in range(tm // MOBA_BLOCK):
        vt_ref[0, c] = v_t[:, c * MOBA_BLOCK:(c + 1) * MOBA_BLOCK]
    zb_ref[...] = moba_s[:, 3 * gw:4 * gw]


def _even_layer_front(x2d, g, w, gq, gk, conv_w, conv_b, ln_g, ln_b, *, bsz, tm=512, rows=512):
    m, d = x2d.shape
    e = w.shape[1]
    s = m // bsz
    per_b = s // tm
    hd, gw, nb = HEAD_DIM, GROUP_WIDTH, tm // MOBA_BLOCK
    assert s % tm == 0 and tm % MOBA_BLOCK == 0 and e == 7 * gw and tm % rows == 0
    const = lambda shape: pl.BlockSpec(shape, lambda i: (0, 0))
    tile = pl.BlockSpec((tm, gw), lambda i: (i, 0))
    vec = const((1, gw))
    return pl.pallas_call(
        functools.partial(_even_layer_kernel, seq=s, rows=rows),
        grid=(m // tm,),
        in_specs=[pl.BlockSpec((tm, d), lambda i: (i, 0)), const((1, d)),
                  pl.BlockSpec((d, e), lambda i: (0, 0), pipeline_mode=pl.Buffered(1)),
                  const((hd, 1)), const((1, hd)), const((CONV_WIDTH, gw)), vec, vec, vec],
        out_specs=[tile, tile,
                   pl.BlockSpec((1, gw, tm), lambda i: (i // per_b, 0, i % per_b)),
                   pl.BlockSpec((1, N_HEADS, tm, LANES), lambda i: (i // per_b, 0, i % per_b, 0)),
                   pl.BlockSpec((1, nb, gw, MOBA_BLOCK), lambda i: (i // per_b, i % per_b, 0, 0)),
                   pl.BlockSpec((1, 1, nb, gw), lambda i: (i // per_b, i % per_b, 0, 0))],
        out_shape=[jax.ShapeDtypeStruct((m, gw), BF16),
                   jax.ShapeDtypeStruct((m, gw), F32),
                   jax.ShapeDtypeStruct((bsz, gw, s), F32),
                   jax.ShapeDtypeStruct((bsz, N_HEADS, s, LANES), BF16),
                   jax.ShapeDtypeStruct((bsz, s // MOBA_BLOCK, gw, MOBA_BLOCK), BF16),
                   jax.ShapeDtypeStruct((bsz, per_b, nb, gw), F32)],
        scratch_shapes=[pltpu.VMEM((d, e), BF16),
                        pltpu.VMEM((tm, d), BF16), pltpu.VMEM((tm, 3 * gw), F32), pltpu.VMEM((tm, 4 * gw), F32),
                        pltpu.VMEM((CONV_HALO + tm, gw), F32), pltpu.VMEM((CONV_HALO, gw), F32)],
        name="even_layer_front",
    )(x2d, g.reshape(1, d), w, gq.reshape(hd, 1), gk.reshape(1, hd), conv_w, conv_b.reshape(1, gw),
      ln_g.reshape(1, gw), ln_b.reshape(1, gw))


def _residual_update(x_ref, ya_ref, yb_ref, w_ref):
    return x_ref[...] + _dot(jnp.concatenate([ya_ref[...], yb_ref[...]], axis=1), w_ref[...])


def _key_features(pos, flipped, sel_shift=None):
    col = _iota(pos.shape, 1) - (0 if flipped else HEAD_DIM)
    hi = (pos >> POS_SPLIT_SHIFT) << POS_SPLIT_SHIFT
    lo = pos & ((1 << POS_SPLIT_SHIFT) - 1)
    feat = jnp.where(col < SLOPE_PIECES, hi, jnp.where(col < 2 * SLOPE_PIECES, lo, 0))
    if sel_shift is not None:
        feat = jnp.where(col - SEL_FEAT0 == (pos >> sel_shift), 1, feat)
    return jnp.where((col >= 0) & (col < HEAD_DIM), feat, 0).astype(F32)


def _pair_rms(x, g):
    sq = x * x
    hi = sq.astype(BF16)
    lo = (sq - hi.astype(F32)).astype(BF16)
    head_shift = HEAD_DIM.bit_length() - 1
    same_head = (_iota((LANES, LANES), 0) >> head_shift) == (_iota((LANES, LANES), 1) >> head_shift)
    ones = jnp.where(same_head, 1.0, 0.0).astype(BF16)
    ss = _dot(hi, ones) + _dot(lo, ones)
    return x * lax.rsqrt(ss * (1.0 / HEAD_DIM) + EPS) * jnp.concatenate([g, g], axis=1)


def _augment_pair_keys(kn, pos, sel_shift=None):
    lane = _iota(kn.shape, 1)
    return [jnp.where(lane >= HEAD_DIM if flipped else lane < HEAD_DIM, kn,
                      _key_features(pos, flipped, sel_shift)).astype(BF16) for flipped in (False, True)]


def _norm_query_t(x, g_col):
    ss = jnp.mean(x * x, axis=0, keepdims=True)
    return x * lax.rsqrt(ss + EPS) * g_col * (HEAD_DIM ** -0.5 * LOG2E)


def _augment_query(q_t, alibi, sel_bias=None, flipped=False):
    n = q_t.shape[1]
    feats = [alibi]
    used = alibi.shape[0]
    if sel_bias is not None:
        feats.append(sel_bias)
        used += sel_bias.shape[0]
    feats.append(jnp.zeros((HEAD_DIM - used, n), F32))
    return jnp.concatenate(feats + [q_t] if flipped else [q_t] + feats, axis=0).astype(BF16)


ONES_ROWS = 16
ACC_ROWS = HEAD_DIM + ONES_ROWS


def _flash_init(n):
    return jnp.full((1, n), M_INIT, F32), jnp.zeros((ACC_ROWS, n), F32)


def _flash_step(carry, sc, v_t):
    m, acc = carry
    m_new = jnp.maximum(m, jnp.max(sc, axis=0, keepdims=True))
    p = jnp.exp2(sc - m_new).astype(BF16)
    v_ones = jnp.concatenate([v_t, jnp.ones((ONES_ROWS, v_t.shape[1]), BF16)], axis=0)
    return m_new, jnp.exp2(m - m_new) * acc + _dot(v_ones, p)


def _flash_split(carry):
    m, acc = carry
    return m, acc[:HEAD_DIM], acc[HEAD_DIM:HEAD_DIM + 1]


def _flash_steps(carries, chunks, cap=None):
    return tuple(_flash_step(carry, sc if cap is None else jnp.minimum(sc, cap), v_t)
                 for carry, (sc, v_t) in zip(carries, chunks))


def _flash_reset(state):
    m_ref, acc_ref = state
    m_ref[...] = jnp.full(m_ref.shape, M_INIT, F32)
    acc_ref[...] = jnp.zeros(acc_ref.shape, F32)


def _flash_update(state, slot, sc, v_t, cap=None):
    m_ref, acc_ref = state
    m_ref[slot], acc_ref[slot] = _flash_step((m_ref[slot], acc_ref[slot]),
                                             sc if cap is None else jnp.minimum(sc, cap), v_t)


def _flash_update_triangular(state, slot, sc_ref, v_t, cap, full_half, partial_lanes):
    m_ref, acc_ref = state
    half = sc_ref.shape[0] // 2
    full = slice(full_half * half, (full_half + 1) * half)
    part = slice((1 - full_half) * half, (2 - full_half) * half)
    _flash_update(state, slot, sc_ref[full, :], v_t[:, full], cap[full, :])

    def gather(x):
        return jnp.concatenate([x[..., a:b] for a, b in partial_lanes], axis=-1)

    sc = jnp.minimum(jnp.concatenate([sc_ref[part, a:b] for a, b in partial_lanes], axis=1), gather(cap[part, :]))
    m, acc = _flash_step((gather(m_ref[slot]), gather(acc_ref[slot])), sc, v_t[:, part])
    width = partial_lanes[0][1] - partial_lanes[0][0]
    for k, (a, b) in enumerate(partial_lanes):
        m_ref[slot, :, a:b] = m[:, k * width:(k + 1) * width]
        acc_ref[slot, :, a:b] = acc[:, k * width:(k + 1) * width]


def _half_lanes(tq, reps, upper):
    off = tq // 2 if upper else 0
    return [(k * tq + off, k * tq + off + tq // 2) for k in range(reps)]


def _flash_update_pipelined(state, bufs, slots, c, next_scores, values, cap=None, cap_slots=None,
                            triangular=None):
    def run(src_ref, dst_ref):
        for slot in slots:
            fresh = None if next_scores is None else next_scores(slot)
            masked = cap is not None and (cap_slots is None or slot in cap_slots)
            if masked and triangular is not None:
                _flash_update_triangular(state, slot, src_ref.at[slot], values(slot), cap, *triangular)
            else:
                _flash_update(state, slot, src_ref[slot], values(slot), cap if masked else None)
            if fresh is not None:
                dst_ref[slot] = fresh

    pl.when((c & 1) == 0)(lambda: run(bufs[0], bufs[1]))
    pl.when((c & 1) == 1)(lambda: run(bufs[1], bufs[0]))


def _flash_result(state, slot):
    m_ref, acc_ref = state
    m, out, den = _flash_split((m_ref[slot], acc_ref[slot]))
    return out / jnp.maximum(den, TINY)


def _flash_scratch(chains, n):
    return [pltpu.VMEM((chains, 1, n), F32), pltpu.VMEM((chains, ACC_ROWS, n), F32)]


def _rank_select(score, blk, limit, count):
    ranks = []
    for r0 in range(0, score.shape[0], SUBLANES):
        tile = score[r0:r0 + SUBLANES]
        tile_blk = r0 + _iota(tile.shape, 0)
        rank = jnp.zeros(tile.shape, jnp.int32)
        for mm in range(limit):
            gm = score[mm:mm + 1, :]
            if mm < r0:
                beats = gm >= tile
            elif mm >= r0 + SUBLANES:
                beats = gm > tile
            else:
                beats = (gm > tile) | ((gm == tile) & (mm < tile_blk))
            rank = rank + jnp.where(beats, 1, 0)
        ranks.append(rank)
    return jnp.concatenate(ranks, axis=0) < count


def _tile_lanes(x, reps):
    return jnp.concatenate([x] * reps, axis=1) if reps > 1 else x


def _causal_cap(tk, tq, reps):
    keep = _iota((tk, tq), 0) <= _iota((tk, tq), 1)
    return _tile_lanes(jnp.where(keep, BIG, -BIG), reps)


def _window_edge_cap(tk, tq, reps):
    keep = _iota((tk, tq), 0) > _iota((tk, tq), 1)
    return _tile_lanes(jnp.where(keep, BIG, -BIG), reps)


def _store_chunked_t(x, dst_ref, tk):
    x_t = x.T.astype(BF16)
    for c in range(x.shape[0] // tk):
        dst_ref[c] = x_t[:, c * tk:(c + 1) * tk]


def _moba_kernel(alibi_ref, qt_ref, ka_ref, vt_ref, km_ref, z_ref, o_ref, m_s, acc_s, sc0_s, sc1_s):
    i = pl.program_id(1)
    nblk = vt_ref.shape[1]
    tq = MOBA_BLOCK
    hd = HEAD_DIM
    chains = [(r, j) for r in range(qt_ref.shape[0]) for j in range(N_HEADS)]
    slots = range(len(chains))

    blk = _iota((nblk, tq), 0)
    qas = []
    for r, j in chains:
        qn = qt_ref[r, j * hd:(j + 1) * hd, :]
        km = jnp.concatenate([km_ref[r, t, :, j * hd:(j + 1) * hd] for t in range(km_ref.shape[1])], axis=0)
        gate = jnp.where(blk < i, _dot_fine(km, qn), NEG)
        chosen = _rank_select(gate, blk, nblk, MOBA_TOPK) & (blk < i)
        sel_bias = jnp.where(chosen | (blk == i), 0.0, -BIG)
        qas.append(_augment_query(qn, alibi_ref[j], sel_bias, flipped=j % 2 == 1))

    def scores(slot, n):
        r, j = chains[slot]
        return _dot(ka_ref[r, j, pl.ds(pl.multiple_of(n * tq, tq), tq), :], qas[slot])

    def values(slot, n):
        r, j = chains[slot]
        return vt_ref[r, n, j * hd:(j + 1) * hd, :]

    state = (m_s, acc_s)
    _flash_reset(state)
    bufs = (sc0_s, sc1_s)
    for slot in slots:
        sc0_s[slot] = scores(slot, 0)

    def body(n, carry):
        _flash_update_pipelined(state, bufs, slots, n, lambda slot: scores(slot, n + 1),
                                lambda slot: values(slot, n))
        return carry

    lax.fori_loop(0, i, body, 0)
    diag_cap = _causal_cap(tq, tq, 1)
    _flash_update_pipelined(state, bufs, slots, i, None, lambda slot: values(slot, i), diag_cap,
                            triangular=(0, _half_lanes(tq, 1, upper=True)))
    for r in range(qt_ref.shape[0]):
        outs = [_flash_result(state, r * N_HEADS + j) for j in range(N_HEADS)]
        y = jnp.concatenate(outs, axis=0).T * _silu(z_ref[r])
        o_ref[r] = y.astype(o_ref.dtype)


def _moba(z, q_t, ka, vt, km, *, rows_per_step=2):
    bsz, s, _ = z.shape
    tq = MOBA_BLOCK
    gw = GROUP_WIDTH
    nblk = s // MOBA_BLOCK
    rps = rows_per_step
    assert SEL_FEAT0 + nblk <= HEAD_DIM and bsz % rps == 0
    whole = lambda a: pl.BlockSpec((rps,) + a.shape[1:], lambda b, i: (b,) + (0,) * (a.ndim - 1))
    return pl.pallas_call(
        _moba_kernel,
        grid=(bsz // rps, s // tq),
        in_specs=[pl.BlockSpec((N_HEADS, 2 * SLOPE_PIECES, tq), lambda b, i: (0, 0, 0)),
                  pl.BlockSpec((rps, gw, tq), lambda b, i: (b, 0, i)), whole(ka), whole(vt), whole(km),
                  pl.BlockSpec((rps, tq, gw), lambda b, i: (b, i, 0))],
        out_specs=pl.BlockSpec((rps, tq, gw), lambda b, i: (b, i, 0)),
        out_shape=jax.ShapeDtypeStruct((bsz, s, gw), BF16),
        scratch_shapes=_flash_scratch(rps * N_HEADS, tq) + [pltpu.VMEM((rps * N_HEADS, tq, tq), F32)] * 2,
        name="moba",
    )(jnp.asarray(_alibi_query_features(tq)), q_t, ka, vt, km, z)


def _cmp_kernel(xk_ref, xv_ref, wk1_ref, wv1_ref, pk_ref, pv_ref, bk1_ref, bv1_ref,
                wk2_ref, wv2_ref, bk2_ref, bv2_ref, gn_ref, ko_ref, vo_ref, wke_s, wve_s):
    half = xk_ref.shape[2]
    n = xk_ref.shape[1]
    hd = HEAD_DIM
    nh = NSA_CMP_HIDDEN

    @pl.when(pl.program_id(0) == 0)
    def _():
        for w_ref, we_ref in ((wk1_ref, wke_s), (wv1_ref, wve_s)):
            we_ref[...] = jnp.zeros(we_ref.shape, BF16)
            for tok in range(NSA_CMP_LEN):
                for g in range(KV_HEADS):
                    row = (tok * KV_HEADS + g) * hd
                    we_ref[row:row + hd, g * nh:(g + 1) * nh] = w_ref[tok * hd:(tok + 1) * hd, :]

    rows = xk_ref.shape[0]

    def hidden(x_ref, we_ref, pos_ref, b1_ref):
        x = x_ref[...].reshape(rows * n, half).astype(BF16)
        first = _dot(x, we_ref[:half, :])
        second = _dot(x, we_ref[half:, :])
        second = jnp.concatenate([pltpu.roll(second[r * n:(r + 1) * n], n - 1, 0) for r in range(rows)], axis=0)
        pos = jnp.broadcast_to(pos_ref[...], (8, 2 * half)).astype(BF16)
        ph = _dot(pos[:, :half], we_ref[:half, :]) + _dot(pos[:, half:], we_ref[half:, :])
        return _silu(first + second + ph[0:1, :] + b1_ref[...])

    hk = hidden(xk_ref, wke_s, pk_ref, bk1_ref).astype(BF16)
    hv = hidden(xv_ref, wve_s, pv_ref, bv1_ref).astype(BF16)
    ks, vs = [], []
    for g in range(KV_HEADS):
        ks.append(_rms(_dot(hk[:, g * nh:(g + 1) * nh], wk2_ref[...]) + bk2_ref[...], gn_ref[...]))
        vs.append(_dot(hv[:, g * nh:(g + 1) * nh], wv2_ref[...]) + bv2_ref[...])
    ko_ref[...] = jnp.concatenate(ks, axis=1).reshape(ko_ref.shape)
    vo_ref[...] = jnp.concatenate(vs, axis=1).reshape(vo_ref.shape)


def _compress(xk, xv, pos_k, pos_v, kw1, kb1, kw2, kb2, vw1, vb1, vw2, vb2, gn, *, rows_per_step=4):
    bsz, n, half = xk.shape
    hd = HEAD_DIM
    rps = rows_per_step
    assert bsz % rps == 0
    tile_pos = lambda p: jnp.tile(p[:, None, :], (1, KV_HEADS, 1)).reshape(1, 2 * half)
    tile_b = lambda b: jnp.tile(b.reshape(1, -1), (1, KV_HEADS))
    const = lambda shape: pl.BlockSpec(shape, lambda b: (0,) * len(shape))
    xs = pl.BlockSpec((rps, n, half), lambda b: (b, 0, 0))
    hw = KV_HEADS * NSA_CMP_HIDDEN
    out = pl.BlockSpec((rps, n, KV_HEADS * hd), lambda b: (b, 0, 0))
    expanded = pltpu.VMEM((2 * half, hw), BF16)
    return pl.pallas_call(
        _cmp_kernel,
        grid=(bsz // rps,),
        in_specs=[xs, xs, const(kw1.shape), const(vw1.shape),
                  const((1, 2 * half)), const((1, 2 * half)), const((1, hw)), const((1, hw)),
                  const((NSA_CMP_HIDDEN, hd)), const((NSA_CMP_HIDDEN, hd)), const((1, hd)), const((1, hd)),
                  const((1, hd))],
        out_specs=[out, out],
        out_shape=[jax.ShapeDtypeStruct((bsz, n, KV_HEADS * hd), F32)] * 2,
        scratch_shapes=[expanded, expanded],
        name="nsa_compress",
    )(xk, xv, kw1.astype(BF16), vw1.astype(BF16), tile_pos(pos_k), tile_pos(pos_v), tile_b(kb1), tile_b(vb1),
      kw2.astype(BF16), vw2.astype(BF16), kb2.reshape(1, hd), vb2.reshape(1, hd), gn.reshape(1, hd))


def _group_query(q_t, alibi_ref, g, gq_col):
    hd = HEAD_DIM
    heads = range(Q_PER_KV * g, Q_PER_KV * (g + 1))
    qn = jnp.concatenate([_norm_query_t(q_t[h * hd:(h + 1) * hd], gq_col) for h in heads], axis=1)
    alibi = jnp.concatenate([alibi_ref[h] for h in heads], axis=1)
    return qn, alibi


def _augment_keys(k, gk_ref, dst_ref, row=0, sel_shift=None):
    assert KV_HEADS == 2
    pos = _iota(k.shape, 0)
    dst_ref[2 * row], dst_ref[2 * row + 1] = _augment_pair_keys(_pair_rms(k, gk_ref[...]), pos, sel_shift)


def _chunk_scores(ka_ref, vt_ref, g, c, qa, tq):
    off = pl.multiple_of(c * tq, tq)
    return _dot(ka_ref[g, pl.ds(off, tq), :], qa), vt_ref[c, g * HEAD_DIM:(g + 1) * HEAD_DIM, :]


def _window_first_chunk(ka_ref, vt_ref, g, far, qa, tq, edge_cap):
    sc, v_t = _chunk_scores(ka_ref, vt_ref, g, jnp.maximum(far, 0), qa, tq)
    return jnp.minimum(sc, jnp.minimum(edge_cap, jnp.where(far >= 0, BIG, -BIG))), v_t


def _regroup_heads(o_t, tq):
    return jnp.concatenate([o_t[:, k * tq:(k + 1) * tq] for k in range(Q_PER_KV)], axis=0)


def _nsa_kernel(alibi_ref, q_ref, kc_ref, vc_ref, ks_ref, vs_ref, kw_ref, vw_ref, gate_ref, z_ref,
                gq_ref, gks_ref, gkw_ref, o_ref, kca_s, vct_s, ksa_s, vst_s, kwa_s, vwt_s, m_s, acc_s,
                sc0_s, sc1_s):
    i = pl.program_id(1)
    rows = range(q_ref.shape[0])
    tq = q_ref.shape[1]
    s = ks_ref.shape[1]
    hd = HEAD_DIM
    ncmp = kc_ref.shape[1]
    nsb = s // NSA_SEL_BLOCK

    @pl.when(i == 0)
    def _():
        cend = _iota((ncmp, LANES), 0) * NSA_CMP_STRIDE + (NSA_CMP_LEN - 1)
        for r in rows:
            _augment_keys(ks_ref[r], gks_ref, ksa_s, r, NSA_SEL_BLOCK.bit_length() - 1)
            _augment_keys(kw_ref[r], gkw_ref, kwa_s, r)
            _store_chunked_t(vs_ref[r], vst_s.at[r], tq)
            _store_chunked_t(vw_ref[r], vwt_s.at[r], tq)
            kca_s[2 * r], kca_s[2 * r + 1] = _augment_pair_keys(kc_ref[r], cend)
            vct_s[r] = vc_ref[r].T.astype(BF16)

    t0 = i * tq
    diag_cap = _causal_cap(tq, tq, Q_PER_KV)
    edge_cap = _window_edge_cap(tq, tq, Q_PER_KV)
    cend =_iota((ncmp, tq), 0) * NSA_CMP_STRIDE + (NSA_CMP_LEN - 1)
    seen_cap = _tile_lanes(jnp.where(cend <= t0 + _iota((ncmp, tq), 1), BIG, -BIG), Q_PER_KV)
    cstart = _iota((nsb, ncmp), 1) * NSA_CMP_STRIDE
    bstart = _iota((nsb, ncmp), 0) * NSA_SEL_BLOCK
    overlap_t = jnp.where((cstart < bstart + NSA_SEL_BLOCK) & (cstart + NSA_CMP_LEN > bstart), 1.0, 0.0)
    blk = _iota((nsb, tq), 0)
    cur = (t0 + _iota((nsb, tq), 1)) >> (NSA_SEL_BLOCK.bit_length() - 1)
    forced = (blk == 0) | (blk == cur) | (blk == cur - 1)
    groups = range(KV_HEADS)
    far = i - NSA_WINDOW // tq
    lo = jnp.maximum(far + 1, 0)
    first_win = jnp.maximum(far, 0)

    state = (m_s, acc_s)
    kv = [(r, g) for r in rows for g in groups]
    sel_slots = tuple(range(len(kv)))
    win_slots = tuple(len(kv) + n for n in range(len(kv)))
    _flash_reset(state)
    q_ts = [q_ref[r].T for r in rows]
    queries = [_group_query(q_ts[r], alibi_ref, g, gq_ref[...]) for r, g in kv]
    qa_win = [_augment_query(qn, alibi, flipped=g == 1) for (r, g), (qn, alibi) in zip(kv, queries)]
    qa_sel = []

    def scores(slot, c):
        n = slot % len(kv)
        ka_ref, qa = (ksa_s, qa_sel) if slot in sel_slots else (kwa_s, qa_win)
        return _dot(ka_ref[n, pl.ds(pl.multiple_of(c * tq, tq), tq), :], qa[n])

    def values(slot, c):
        r, g = kv[slot % len(kv)]
        return (vst_s if slot in sel_slots else vwt_s)[r, c, g * hd:(g + 1) * hd, :]

    cmp_sc = [_dot(kca_s[n], qa_win[n]) for n in range(len(kv))]
    sc_win_first = [scores(slot, first_win) for slot in win_slots]
    o_cmp = []
    for n, (r, g) in enumerate(kv):
        qn, alibi = queries[n]
        sc = jnp.minimum(cmp_sc[n], seen_cap)
        top = jnp.max(sc, axis=0, keepdims=True)
        e = jnp.exp2(sc - top)
        inv = jnp.where(top > -0.5 * BIG, 1.0 / jnp.maximum(jnp.sum(e, axis=0, keepdims=True), TINY), 0.0)
        o_cmp.append(_dot(vct_s[r, g * hd:(g + 1) * hd, :], e.astype(BF16)) * inv)

        psum = e[:, 0:tq] * inv[:, 0:tq]
        for k in range(1, Q_PER_KV):
            psum = psum + e[:, k * tq:(k + 1) * tq] * inv[:, k * tq:(k + 1) * tq]
        imp = _dot_fine(overlap_t, psum, a_is_bf16_exact=True)
        imp = jnp.where(blk <= cur, jnp.where(forced, NSA_FORCE, imp), NEG)
        chosen = _rank_select(imp, blk, nsb, min(NSA_TOPN, nsb))
        sel_bias = _tile_lanes(jnp.where(chosen, 0.0, -BIG), Q_PER_KV)
        qa_sel.append(_augment_query(qn, alibi, sel_bias, flipped=g == 1))

    bufs = (sc0_s, sc1_s)
    for slot in sel_slots:
        sc0_s[slot] = scores(slot, 0)
    for slot, sc in zip(win_slots, sc_win_first):
        sc0_s[slot] = sc
        sc1_s[slot] = sc

    def body(slots, cap, cap_slots, triangular, c, carry):
        _flash_update_pipelined(state, bufs, slots, c, lambda slot: scores(slot, c + 1),
                                lambda slot: values(slot, c), cap, cap_slots, triangular)
        return carry

    both_slots = sel_slots + win_slots
    edge_shape = (1, _half_lanes(tq, Q_PER_KV, upper=False))
    diag_shape = (0, _half_lanes(tq, Q_PER_KV, upper=True))
    lax.fori_loop(0, first_win, functools.partial(body, sel_slots, None, None, None), 0)
    lax.fori_loop(first_win, lo, functools.partial(body, both_slots, edge_cap, win_slots, edge_shape), 0)
    lax.fori_loop(lo, i, functools.partial(body, both_slots, None, None, None), 0)
    _flash_update_pipelined(state, bufs, both_slots, i, None, lambda slot: values(slot, i), diag_cap,
                            triangular=diag_shape)

    for r in rows:
        gates_t = _sigmoid(gate_ref[r]).T
        outs = []
        for g in groups:
            def gate_row(branch):
                parts = [gates_t[branch * N_HEADS + h:branch * N_HEADS + h + 1, :]
                         for h in range(Q_PER_KV * g, Q_PER_KV * (g + 1))]
                return jnp.concatenate(parts, axis=1)

            n = KV_HEADS * r + g
            o_slc = _flash_result(state, sel_slots[n])
            o_win = _flash_result(state, win_slots[n])
            o = gate_row(0) * o_cmp[n] + gate_row(1) * o_slc + gate_row(2) * o_win
            outs.append(_regroup_heads(o, tq))
        y = jnp.concatenate(outs, axis=0).T * _silu(z_ref[r])
        o_ref[r] = y.astype(o_ref.dtype)


def _nsa(proj, kcn, vc, gq, gks, gkw, *, tq=256, rows_per_step=2):
    bsz, s, _ = proj.shape
    hd = HEAD_DIM
    gw = GROUP_WIDTH
    ncmp = kcn.shape[1]
    rps = rows_per_step
    chains = 2 * KV_HEADS * rps
    assert SEL_FEAT0 + s // NSA_SEL_BLOCK <= hd and ncmp == LANES and bsz % rps == 0
    wide = lambda c0: pl.BlockSpec((rps, tq, gw), lambda b, i: (b, i, c0 // 4))
    full = lambda c0: pl.BlockSpec((rps, s, LANES), lambda b, i: (b, 0, c0))
    cmp = pl.BlockSpec((rps, ncmp, LANES), lambda b, i: (b, 0, 0))
    row = pl.BlockSpec((1, hd), lambda b, i: (0, 0))
    ka = pltpu.VMEM((KV_HEADS * rps, s, LANES), BF16)
    vt = pltpu.VMEM((rps, s // tq, LANES, tq), BF16)
    return pl.pallas_call(
        _nsa_kernel,
        grid=(bsz // rps, s // tq),
        in_specs=[pl.BlockSpec((N_HEADS, 2 * SLOPE_PIECES, tq), lambda b, i: (0, 0, 0)),
                  wide(ODD_QC), cmp, cmp, full(ODD_KS), full(ODD_VS), full(ODD_KW), full(ODD_VW),
                  pl.BlockSpec((rps, tq, LANES), lambda b, i: (b, i, ODD_GC)), wide(ODD_ZC),
                  pl.BlockSpec((hd, 1), lambda b, i: (0, 0)), row, row],
        out_specs=pl.BlockSpec((rps, tq, gw), lambda b, i: (b, i, 0)),
        out_shape=jax.ShapeDtypeStruct((bsz, s, gw), BF16),
        scratch_shapes=[pltpu.VMEM((KV_HEADS * rps, ncmp, LANES), BF16), pltpu.VMEM((rps, LANES, ncmp), BF16),
                        ka, vt, ka, vt] + _flash_scratch(chains, Q_PER_KV * tq)
                       + [pltpu.VMEM((chains, tq, Q_PER_KV * tq), F32)] * 2,
        name="nsa",
    )(jnp.asarray(_alibi_query_features(tq)), proj, kcn, vc, proj, proj, proj, proj, proj, proj,
      gq.reshape(hd, 1), gks.reshape(1, hd), gkw.reshape(1, hd))


def _swa_kernel(sinks_ref, alibi_ref, q_ref, k_ref, v_ref, z_ref, gq_ref, gk_ref, x_ref, yc_ref, w_ref, o_ref,
                ka_s, vt_s):
    step = pl.program_id(1)
    tq = SWA_WINDOW
    tiles = q_ref.shape[1] // tq
    s = k_ref.shape[1]

    @pl.when(step == 0)
    def _():
        _augment_keys(k_ref[0], gk_ref, ka_s)
        _store_chunked_t(v_ref[0], vt_s, tq)

    q_t = q_ref[0].T
    diag_cap = _causal_cap(tq, tq, Q_PER_KV)
    edge_cap = _window_edge_cap(tq, tq, Q_PER_KV)
    chains = [(u, g) for u in range(tiles) for g in range(KV_HEADS)]
    qas = [_augment_query(*_group_query(q_t[:, u * tq:(u + 1) * tq], alibi_ref, g, gq_ref[...]), flipped=g == 1)
           for u, g in chains]
    first = [_window_first_chunk(ka_s, vt_s, g, step * tiles + u - 1, qa, tq, edge_cap)
             for (u, g), qa in zip(chains, qas)]
    own = [_chunk_scores(ka_s, vt_s, g, step * tiles + u, qa, tq) for (u, g), qa in zip(chains, qas)]
    carries = _flash_steps([_flash_init(Q_PER_KV * tq) for _ in chains], first)
    carries = _flash_steps(carries, own, diag_cap)
    outs = []
    for (u, g), carry in zip(chains, carries):
        m, acc, l = _flash_split(carry)
        tpos = ((step * tiles + u) * tq + _iota((1, tq), 1)).astype(F32)
        sink = jnp.concatenate([sinks_ref[h] * LOG2E + (SLOPES[h] * LOG2E) * tpos
                                for h in range(Q_PER_KV * g, Q_PER_KV * (g + 1))], axis=1)
        mf = jnp.maximum(m, sink)
        alpha = jnp.exp2(m - mf)
        o = acc * alpha / jnp.maximum(l * alpha + jnp.exp2(sink - mf), TINY)
        outs.append(_regroup_heads(o, tq))
    o_t = jnp.concatenate([jnp.concatenate(outs[u * KV_HEADS:(u + 1) * KV_HEADS], axis=0) for u in range(tiles)],
                          axis=1)
    y = (o_t.T * _silu(z_ref[0])).astype(BF16)
    o_ref[0] = x_ref[0] + _dot(jnp.concatenate([yc_ref[0], y], axis=1), w_ref[...])


def _swa(proj, sinks, gq, gk, x, y_c, w_out, *, tiles_per_step=8):
    bsz, s, _ = proj.shape
    d = x.shape[-1]
    hd = HEAD_DIM
    gw = GROUP_WIDTH
    tq = SWA_WINDOW
    rows = tiles_per_step * tq
    wide = lambda c0: pl.BlockSpec((1, rows, gw), lambda b, i: (b, i, c0 // 4))
    full = lambda c0: pl.BlockSpec((1, s, LANES), lambda b, i: (b, 0, c0))
    return pl.pallas_call(
        _swa_kernel,
        grid=(bsz, s // rows),
        in_specs=[pl.BlockSpec(memory_space=pltpu.SMEM),
                  pl.BlockSpec((N_HEADS, 2 * SLOPE_PIECES, tq), lambda b, i: (0, 0, 0)),
                  wide(ODD_QD), full(ODD_KD), full(ODD_VD), wide(ODD_ZD),
                  pl.BlockSpec((hd, 1), lambda b, i: (0, 0)), pl.BlockSpec((1, hd), lambda b, i: (0, 0)),
                  pl.BlockSpec((1, rows, d), lambda b, i: (b, i, 0)),
                  pl.BlockSpec((1, rows, gw), lambda b, i: (b, i, 0)),
                  pl.BlockSpec(w_out.shape, lambda b, i: (0, 0))],
        out_specs=pl.BlockSpec((1, rows, d), lambda b, i: (b, i, 0)),
        out_shape=jax.ShapeDtypeStruct((bsz, s, d), F32),
        scratch_shapes=[pltpu.VMEM((KV_HEADS, s, LANES), BF16), pltpu.VMEM((s // tq, LANES, tq), BF16)],
        name="swa_outproj",
    )(sinks.astype(F32), jnp.asarray(_alibi_query_features(tq)), proj, proj, proj, proj,
      gq.reshape(hd, 1), gk.reshape(1, hd), x, y_c, w_out)


def _odd_weight_moves():
    gw, kw, ng = GROUP_WIDTH, KV_HEADS * HEAD_DIM, 3 * N_HEADS
    sizes = [gw, kw, kw, kw, kw, kw, kw, ng, gw, gw, kw, kw, gw]
    starts = np.concatenate([[0], np.cumsum(sizes)]).tolist()
    order = [0, 8, 9, 12, 1, 2, 3, 4, 5, 6, 10, 11, 7]
    moves, dst = [], 0
    for k in order:
        moves.append((starts[k], sizes[k], dst))
        dst += sizes[k]
    assert dst + LANES - ng == ODD_COLS
    return moves


def kernel(x, norm_g, w_out, e_w_in, a_conv_w, a_conv_b, a_ln_g, a_ln_b, b_qnorm_g, b_knorm_g, o_w_in, c_qnorm_g, c_knorm_cmp_g, c_knorm_slc_g, c_knorm_win_g, c_pos_k, c_pos_v, c_k_w1, c_k_b1, c_k_w2, c_k_b2, c_v_w1, c_v_b1, c_v_w2, c_v_b2, d_qnorm_g, d_knorm_g, d_sinks):
    bsz, s, d = x.shape
    m = bsz * s
    assert s % MOBA_BLOCK == 0 and d == 2 * GROUP_WIDTH
    x2 = x.reshape(m, d)

    y_a, z_b, q_t, ka, vt, km = _even_layer_front(
        x2, norm_g[0], e_w_in[0], b_qnorm_g[0], b_knorm_g[0],
        a_conv_w[0], a_conv_b[0], a_ln_g[0], a_ln_b[0], bsz=bsz)
    y_b = _moba(z_b.reshape(bsz, s, -1), q_t, ka, vt, km)

    x2, proj, k_cmp, v_cmp = _layer_boundary(
        x2, y_a, y_b.reshape(m, -1), w_out[0].astype(BF16), norm_g[1], o_w_in[0].astype(BF16),
        e=ODD_COLS, moves=_odd_weight_moves(), chunk=ODD_COLS,
        copies=((ODD_KC * LANES, LANES), (ODD_VC * LANES, LANES)), group=NSA_CMP_STRIDE)
    proj = proj.reshape(bsz, s, ODD_COLS)
    rows16 = lambda t: t.reshape(bsz, s // NSA_CMP_STRIDE, NSA_CMP_STRIDE * LANES)
    kcn, vc = _compress(rows16(k_cmp), rows16(v_cmp), c_pos_k[0], c_pos_v[0], c_k_w1[0], c_k_b1[0],
                        c_k_w2[0], c_k_b2[0], c_v_w1[0], c_v_b1[0], c_v_w2[0], c_v_b2[0], c_knorm_cmp_g[0])
    y_c = _nsa(proj, kcn, vc, c_qnorm_g[0], c_knorm_slc_g[0], c_knorm_win_g[0])
    return _swa(proj, d_sinks[0], d_qnorm_g[0], d_knorm_g[0], x2.reshape(bsz, s, d), y_c, w_out[1].astype(BF16))
```

```python
import functools

import ml_dtypes
import numpy as np
import jax
import jax.numpy as jnp
from jax import lax
from jax.experimental import pallas as pl
from jax.experimental.pallas import tpu as pltpu

HEAD_DIM = 64
N_HEADS = 8
GROUP_WIDTH = N_HEADS * HEAD_DIM
CONV_WIDTH = 31
MOBA_BLOCK = 256
MOBA_TOPK = 3
KV_HEADS = 2
Q_PER_KV = N_HEADS // KV_HEADS
NSA_CMP_LEN = 32
NSA_CMP_STRIDE = 16
NSA_CMP_HIDDEN = 256
NSA_SEL_BLOCK = 64
NSA_TOPN = 16
NSA_WINDOW = 512
NSA_FORCE = 1e4
SWA_WINDOW = 128
EPS = 1e-6
NEG = -1e30
TINY = 1e-30
LANES = 128
SUBLANES = 8
CONV_HALO = 32

LOG2E = float(np.log2(np.e))
BIG = 2.0 ** 99
M_INIT = -1e38
POS_SPLIT_SHIFT = 8
SLOPE_PIECES = 4
SEL_FEAT0 = 2 * SLOPE_PIECES

F32 = jnp.float32
BF16 = jnp.bfloat16

ODD_QC, ODD_ZC, ODD_QD, ODD_ZD = 0, 4, 8, 12
ODD_KC, ODD_VC, ODD_KS, ODD_VS, ODD_KW, ODD_VW, ODD_KD, ODD_VD, ODD_GC = 16, 17, 18, 19, 20, 21, 22, 23, 24
ODD_COLS = 25 * LANES


def _alibi_slopes(n):
    return [float(2.0 ** (-8.0 * (i + 1) / n)) for i in range(n)]


SLOPES = _alibi_slopes(N_HEADS)


def _alibi_query_features(width):
    table = np.zeros((N_HEADS, 2 * SLOPE_PIECES, width), np.float32)
    for h, slope in enumerate(SLOPES):
        rest = np.float64(slope) * LOG2E
        for k in range(SLOPE_PIECES):
            piece = float(np.float32(rest).astype(ml_dtypes.bfloat16).astype(np.float32))
            table[h, k] = piece
            table[h, SLOPE_PIECES + k] = piece
            rest -= piece
    return table


def _dot(a, b):
    return jnp.dot(a, b, preferred_element_type=F32)


def _split_bf16(x, pieces):
    out = []
    for _ in range(pieces):
        out.append(x.astype(BF16))
        x = x - out[-1].astype(F32)
    return out


def _dot_fine(a, b, a_is_bf16_exact=False):
    if a_is_bf16_exact:
        return sum(_dot(a.astype(BF16), piece) for piece in _split_bf16(b, 3))
    (a_hi, a_lo), (b_hi, b_lo) = _split_bf16(a, 2), _split_bf16(b, 2)
    return _dot(a_hi, b_hi) + (_dot(a_hi, b_lo) + _dot(a_lo, b_hi))


def _sigmoid(x):
    return 1.0 / (1.0 + jnp.exp(-x))


def _silu(x):
    return x * _sigmoid(x)


def _rms(x, g):
    return x * lax.rsqrt(jnp.mean(x * x, axis=-1, keepdims=True) + EPS) * g


def _iota(shape, dim):
    return lax.broadcasted_iota(jnp.int32, shape, dim)


def _layer_boundary_kernel(x_ref, ya_ref, yb_ref, wo_ref, g_ref, w_ref, x_out_ref, o_ref, *rest,
                           chunk, copies, group, moves):
    copy_refs, stage_refs, wp_s = rest[:len(copies)], rest[len(copies):-1], rest[-1]

    @pl.when(pl.program_id(0) == 0)
    def _():
        wp_s[...] = jnp.zeros(wp_s.shape, wp_s.dtype)
        for src, width, dst in moves:
            wp_s[:, dst:dst + width] = w_ref[:, src:src + width]

    x = _residual_update(x_ref, ya_ref, yb_ref, wo_ref)
    x_out_ref[...] = x
    h = _rms(x, g_ref[...]).astype(BF16)
    for c in range(o_ref.shape[1] // chunk):
        o_ref[:, c * chunk:(c + 1) * chunk] = _dot(h, wp_s[:, c * chunk:(c + 1) * chunk])
    for ref, stage, (start, width) in zip(copy_refs, stage_refs, copies):
        stage[...] = o_ref[:, start:start + width]
        for t in range(group):
            ref[:, t * width:(t + 1) * width] = stage[pl.ds(t, ref.shape[0], stride=group), :]


def _layer_boundary(x2d, ya, yb, w_out, g, w, *, e, moves, tm=512, chunk, copies=(), group=1):
    m, d = x2d.shape
    gw = ya.shape[1]
    rows = lambda n, width: pl.BlockSpec((n, width), lambda i: (i, 0))
    const = lambda shape: pl.BlockSpec(shape, lambda i: (0, 0))
    return pl.pallas_call(
        functools.partial(_layer_boundary_kernel, chunk=chunk, copies=tuple(copies), group=group,
                          moves=tuple(moves)),
        grid=(m // tm,),
        in_specs=[rows(tm, d), rows(tm, gw), rows(tm, gw), const(w_out.shape), const((1, d)), const(w.shape)],
        out_specs=[rows(tm, d), rows(tm, e)] + [rows(tm // group, group * width) for _, width in copies],
        out_shape=[jax.ShapeDtypeStruct((m, d), F32), jax.ShapeDtypeStruct((m, e), F32)]
                  + [jax.ShapeDtypeStruct((m // group, group * width), F32) for _, width in copies],
        scratch_shapes=[pltpu.VMEM((tm, width), F32) for _, width in copies] + [pltpu.VMEM((d, e), BF16)],
        name="layer_boundary",
    )(x2d, ya, yb, w_out, g.reshape(1, d), w)


def _conv_rows(h_s, row0, rows, w_ref, b_ref, lg_ref, lb_ref):
    base = CONV_HALO - (CONV_WIDTH - 1)
    acc = None
    for b in range(SUBLANES):
        n = rows if b == 0 else rows + SUBLANES
        part = None
        for a in range((base + CONV_WIDTH - 1) // SUBLANES + 1):
            j = SUBLANES * a + b - base
            if 0 <= j < CONV_WIDTH:
                term = w_ref[j:j + 1, :] * h_s[pl.ds(row0 + SUBLANES * a, n), :]
                part = term if part is None else part + term
        part = part[b:b + rows]
        acc = part if acc is None else acc + part
    y = acc + b_ref[...]
    mu = jnp.mean(y, axis=-1, keepdims=True)
    yc = y - mu
    return _silu(yc * lax.rsqrt(jnp.mean(yc * yc, axis=-1, keepdims=True) + EPS) * lg_ref[...] + lb_ref[...])


def _even_layer_kernel(x_ref, g_ref, w_ref, gq_ref, gk_ref, cw_ref, cb_ref, lg_ref, lb_ref,
                       ya_ref, zb_ref, qt_ref, ka_ref, vt_ref, km_ref, wb_s, hb_s, conv_s, moba_s, h_s, tail_s,
                       *, seq, rows):
    tm = x_ref.shape[0]
    hd, gw = HEAD_DIM, GROUP_WIDTH
    tile_in_seq = pl.program_id(0) % (seq // tm)

    @pl.when(pl.program_id(0) == 0)
    def _():
        wb_s[...] = w_ref[...].astype(BF16)

    hb_s[...] = _rms(x_ref[...], g_ref[...]).astype(BF16)
    conv_s[...] = _dot(hb_s[...], wb_s[:, 0:3 * gw])
    h_s[0:CONV_HALO, :] = jnp.where(tile_in_seq > 0, tail_s[...], 0.0)
    h_s[CONV_HALO:, :] = conv_s[:, 0:gw] * _sigmoid(conv_s[:, gw:2 * gw])
    tail_s[...] = h_s[tm:tm + CONV_HALO, :]
    for c in range(tm // rows):
        y = _conv_rows(h_s, c * rows, rows, cw_ref, cb_ref, lg_ref, lb_ref)
        gate = _silu(conv_s[c * rows:(c + 1) * rows, 2 * gw:3 * gw])
        ya_ref[c * rows:(c + 1) * rows, :] = (y * gate).astype(ya_ref.dtype)

    moba_s[...] = _dot(hb_s[...], wb_s[:, 3 * gw:7 * gw])
    q_t = moba_s[:, 0:gw].T
    for j in range(N_HEADS):
        qt_ref[0, j * hd:(j + 1) * hd, :] = _norm_query_t(q_t[j * hd:(j + 1) * hd], gq_ref[...])
    pos = tile_in_seq * tm + _iota((tm, LANES), 0)
    for pair in range(N_HEADS // 2):
        kn = _pair_rms(moba_s[:, gw + pair * LANES:gw + (pair + 1) * LANES], gk_ref[...])
        ka_ref[0, 2 * pair], ka_ref[0, 2 * pair + 1] = _augment_pair_keys(kn, pos, MOBA_BLOCK.bit_length() - 1)
        km_ref[0, 0, :, pair * LANES:(pair + 1) * LANES] = jnp.mean(
            kn.reshape(tm // MOBA_BLOCK, MOBA_BLOCK, LANES), axis=1)
    v_t = moba_s[:, 2 * gw:3 * gw].T.astype(BF16)
    for c in range(tm // MOBA_BLOCK):
        vt_ref[0, c] = v_t[:, c * MOBA_BLOCK:(c + 1) * MOBA_BLOCK]
    zb_ref[...] = moba_s[:, 3 * gw:4 * gw]


def _even_layer_front(x2d, g, w, gq, gk, conv_w, conv_b, ln_g, ln_b, *, bsz, tm=512, rows=512):
    m, d = x2d.shape
    e = w.shape[1]
    s = m // bsz
    per_b = s // tm
    hd, gw, nb = HEAD_DIM, GROUP_WIDTH, tm // MOBA_BLOCK
    assert s % tm == 0 and tm % MOBA_BLOCK == 0 and e == 7 * gw and tm % rows == 0
    const = lambda shape: pl.BlockSpec(shape, lambda i: (0, 0))
    tile = pl.BlockSpec((tm, gw), lambda i: (i, 0))
    vec = const((1, gw))
    return pl.pallas_call(
        functools.partial(_even_layer_kernel, seq=s, rows=rows),
        grid=(m // tm,),
        in_specs=[pl.BlockSpec((tm, d), lambda i: (i, 0)), const((1, d)),
                  pl.BlockSpec((d, e), lambda i: (0, 0), pipeline_mode=pl.Buffered(1)),
                  const((hd, 1)), const((1, hd)), const((CONV_WIDTH, gw)), vec, vec, vec],
        out_specs=[tile, tile,
                   pl.BlockSpec((1, gw, tm), lambda i: (i // per_b, 0, i % per_b)),
                   pl.BlockSpec((1, N_HEADS, tm, LANES), lambda i: (i // per_b, 0, i % per_b, 0)),
                   pl.BlockSpec((1, nb, gw, MOBA_BLOCK), lambda i: (i // per_b, i % per_b, 0, 0)),
                   pl.BlockSpec((1, 1, nb, gw), lambda i: (i // per_b, i % per_b, 0, 0))],
        out_shape=[jax.ShapeDtypeStruct((m, gw), BF16),
                   jax.ShapeDtypeStruct((m, gw), F32),
                   jax.ShapeDtypeStruct((bsz, gw, s), F32),
                   jax.ShapeDtypeStruct((bsz, N_HEADS, s, LANES), BF16),
                   jax.ShapeDtypeStruct((bsz, s // MOBA_BLOCK, gw, MOBA_BLOCK), BF16),
                   jax.ShapeDtypeStruct((bsz, per_b, nb, gw), F32)],
        scratch_shapes=[pltpu.VMEM((d, e), BF16),
                        pltpu.VMEM((tm, d), BF16), pltpu.VMEM((tm, 3 * gw), F32), pltpu.VMEM((tm, 4 * gw), F32),
                        pltpu.VMEM((CONV_HALO + tm, gw), F32), pltpu.VMEM((CONV_HALO, gw), F32)],
        name="even_layer_front",
    )(x2d, g.reshape(1, d), w, gq.reshape(hd, 1), gk.reshape(1, hd), conv_w, conv_b.reshape(1, gw),
      ln_g.reshape(1, gw), ln_b.reshape(1, gw))


def _residual_update(x_ref, ya_ref, yb_ref, w_ref):
    return x_ref[...] + _dot(jnp.concatenate([ya_ref[...], yb_ref[...]], axis=1), w_ref[...])


def _key_features(pos, flipped, sel_shift=None):
    col = _iota(pos.shape, 1) - (0 if flipped else HEAD_DIM)
    hi = (pos >> POS_SPLIT_SHIFT) << POS_SPLIT_SHIFT
    lo = pos & ((1 << POS_SPLIT_SHIFT) - 1)
    feat = jnp.where(col < SLOPE_PIECES, hi, jnp.where(col < 2 * SLOPE_PIECES, lo, 0))
    if sel_shift is not None:
        feat = jnp.where(col - SEL_FEAT0 == (pos >> sel_shift), 1, feat)
    return jnp.where((col >= 0) & (col < HEAD_DIM), feat, 0).astype(F32)


def _pair_rms(x, g):
    sq = x * x
    hi = sq.astype(BF16)
    lo = (sq - hi.astype(F32)).astype(BF16)
    head_shift = HEAD_DIM.bit_length() - 1
    same_head = (_iota((LANES, LANES), 0) >> head_shift) == (_iota((LANES, LANES), 1) >> head_shift)
    ones = jnp.where(same_head, 1.0, 0.0).astype(BF16)
    ss = _dot(hi, ones) + _dot(lo, ones)
    return x * lax.rsqrt(ss * (1.0 / HEAD_DIM) + EPS) * jnp.concatenate([g, g], axis=1)


def _augment_pair_keys(kn, pos, sel_shift=None):
    lane = _iota(kn.shape, 1)
    return [jnp.where(lane >= HEAD_DIM if flipped else lane < HEAD_DIM, kn,
                      _key_features(pos, flipped, sel_shift)).astype(BF16) for flipped in (False, True)]


def _norm_query_t(x, g_col):
    ss = jnp.mean(x * x, axis=0, keepdims=True)
    return x * lax.rsqrt(ss + EPS) * g_col * (HEAD_DIM ** -0.5 * LOG2E)


def _augment_query(q_t, alibi, sel_bias=None, flipped=False):
    n = q_t.shape[1]
    feats = [alibi]
    used = alibi.shape[0]
    if sel_bias is not None:
        feats.append(sel_bias)
        used += sel_bias.shape[0]
    feats.append(jnp.zeros((HEAD_DIM - used, n), F32))
    return jnp.concatenate(feats + [q_t] if flipped else [q_t] + feats, axis=0).astype(BF16)


ONES_ROWS = 16
ACC_ROWS = HEAD_DIM + ONES_ROWS


def _flash_init(n):
    return jnp.full((1, n), M_INIT, F32), jnp.zeros((ACC_ROWS, n), F32)


def _flash_step(carry, sc, v_t):
    m, acc = carry
    m_new = jnp.maximum(m, jnp.max(sc, axis=0, keepdims=True))
    p = jnp.exp2(sc - m_new).astype(BF16)
    v_ones = jnp.concatenate([v_t, jnp.ones((ONES_ROWS, v_t.shape[1]), BF16)], axis=0)
    return m_new, jnp.exp2(m - m_new) * acc + _dot(v_ones, p)


def _flash_split(carry):
    m, acc = carry
    return m, acc[:HEAD_DIM], acc[HEAD_DIM:HEAD_DIM + 1]


def _flash_steps(carries, chunks, cap=None):
    return tuple(_flash_step(carry, sc if cap is None else jnp.minimum(sc, cap), v_t)
                 for carry, (sc, v_t) in zip(carries, chunks))


def _flash_reset(state):
    m_ref, acc_ref = state
    m_ref[...] = jnp.full(m_ref.shape, M_INIT, F32)
    acc_ref[...] = jnp.zeros(acc_ref.shape, F32)


def _flash_update(state, slot, sc, v_t, cap=None):
    m_ref, acc_ref = state
    m_ref[slot], acc_ref[slot] = _flash_step((m_ref[slot], acc_ref[slot]),
                                             sc if cap is None else jnp.minimum(sc, cap), v_t)


def _flash_update_triangular(state, slot, sc_ref, v_t, cap, full_half, partial_lanes):
    m_ref, acc_ref = state
    half = sc_ref.shape[0] // 2
    full = slice(full_half * half, (full_half + 1) * half)
    part = slice((1 - full_half) * half, (2 - full_half) * half)
    _flash_update(state, slot, sc_ref[full, :], v_t[:, full], cap[full, :])

    def gather(x):
        return jnp.concatenate([x[..., a:b] for a, b in partial_lanes], axis=-1)

    sc = jnp.minimum(jnp.concatenate([sc_ref[part, a:b] for a, b in partial_lanes], axis=1), gather(cap[part, :]))
    m, acc = _flash_step((gather(m_ref[slot]), gather(acc_ref[slot])), sc, v_t[:, part])
    width = partial_lanes[0][1] - partial_lanes[0][0]
    for k, (a, b) in enumerate(partial_lanes):
        m_ref[slot, :, a:b] = m[:, k * width:(k + 1) * width]
        acc_ref[slot, :, a:b] = acc[:, k * width:(k + 1) * width]


def _half_lanes(tq, reps, upper):
    off = tq // 2 if upper else 0
    return [(k * tq + off, k * tq + off + tq // 2) for k in range(reps)]


def _flash_update_pipelined(state, bufs, slots, c, next_scores, values, cap=None, cap_slots=None,
                            triangular=None):
    def run(src_ref, dst_ref):
        for slot in slots:
            fresh = None if next_scores is None else next_scores(slot)
            masked = cap is not None and (cap_slots is None or slot in cap_slots)
            if masked and triangular is not None:
                _flash_update_triangular(state, slot, src_ref.at[slot], values(slot), cap, *triangular)
            else:
                _flash_update(state, slot, src_ref[slot], values(slot), cap if masked else None)
            if fresh is not None:
                dst_ref[slot] = fresh

    pl.when((c & 1) == 0)(lambda: run(bufs[0], bufs[1]))
    pl.when((c & 1) == 1)(lambda: run(bufs[1], bufs[0]))


def _flash_result(state, slot):
    m_ref, acc_ref = state
    m, out, den = _flash_split((m_ref[slot], acc_ref[slot]))
    return out / jnp.maximum(den, TINY)


def _flash_scratch(chains, n):
    return [pltpu.VMEM((chains, 1, n), F32), pltpu.VMEM((chains, ACC_ROWS, n), F32)]


def _rank_select(score, blk, limit, count):
    ranks = []
    for r0 in range(0, score.shape[0], SUBLANES):
        tile = score[r0:r0 + SUBLANES]
        tile_blk = r0 + _iota(tile.shape, 0)
        rank = jnp.zeros(tile.shape, jnp.int32)
        for mm in range(limit):
            gm = score[mm:mm + 1, :]
            if mm < r0:
                beats = gm >= tile
            elif mm >= r0 + SUBLANES:
                beats = gm > tile
            else:
                beats = (gm > tile) | ((gm == tile) & (mm < tile_blk))
            rank = rank + jnp.where(beats, 1, 0)
        ranks.append(rank)
    return jnp.concatenate(ranks, axis=0) < count


def _tile_lanes(x, reps):
    return jnp.concatenate([x] * reps, axis=1) if reps > 1 else x


def _causal_cap(tk, tq, reps):
    keep = _iota((tk, tq), 0) <= _iota((tk, tq), 1)
    return _tile_lanes(jnp.where(keep, BIG, -BIG), reps)


def _window_edge_cap(tk, tq, reps):
    keep = _iota((tk, tq), 0) > _iota((tk, tq), 1)
    return _tile_lanes(jnp.where(keep, BIG, -BIG), reps)


def _store_chunked_t(x, dst_ref, tk):
    x_t = x.T.astype(BF16)
    for c in range(x.shape[0] // tk):
        dst_ref[c] = x_t[:, c * tk:(c + 1) * tk]


def _moba_kernel(alibi_ref, qt_ref, ka_ref, vt_ref, km_ref, z_ref, o_ref, m_s, acc_s, sc0_s, sc1_s):
    i = pl.program_id(1)
    nblk = vt_ref.shape[1]
    tq = MOBA_BLOCK
    hd = HEAD_DIM
    chains = [(r, j) for r in range(qt_ref.shape[0]) for j in range(N_HEADS)]
    slots = range(len(chains))

    blk = _iota((nblk, tq), 0)
    qas = []
    for r, j in chains:
        qn = qt_ref[r, j * hd:(j + 1) * hd, :]
        km = jnp.concatenate([km_ref[r, t, :, j * hd:(j + 1) * hd] for t in range(km_ref.shape[1])], axis=0)
        gate = jnp.where(blk < i, _dot_fine(km, qn), NEG)
        chosen = _rank_select(gate, blk, nblk, MOBA_TOPK) & (blk < i)
        sel_bias = jnp.where(chosen | (blk == i), 0.0, -BIG)
        qas.append(_augment_query(qn, alibi_ref[j], sel_bias, flipped=j % 2 == 1))

    def scores(slot, n):
        r, j = chains[slot]
        return _dot(ka_ref[r, j, pl.ds(pl.multiple_of(n * tq, tq), tq), :], qas[slot])

    def values(slot, n):
        r, j = chains[slot]
        return vt_ref[r, n, j * hd:(j + 1) * hd, :]

    state = (m_s, acc_s)
    _flash_reset(state)
    bufs = (sc0_s, sc1_s)
    for slot in slots:
        sc0_s[slot] = scores(slot, 0)

    def body(n, carry):
        _flash_update_pipelined(state, bufs, slots, n, lambda slot: scores(slot, n + 1),
                                lambda slot: values(slot, n))
        return carry

    lax.fori_loop(0, i, body, 0)
    diag_cap = _causal_cap(tq, tq, 1)
    _flash_update_pipelined(state, bufs, slots, i, None, lambda slot: values(slot, i), diag_cap,
                            triangular=(0, _half_lanes(tq, 1, upper=True)))
    for r in range(qt_ref.shape[0]):
        outs = [_flash_result(state, r * N_HEADS + j) for j in range(N_HEADS)]
        y = jnp.concatenate(outs, axis=0).T * _silu(z_ref[r])
        o_ref[r] = y.astype(o_ref.dtype)


def _moba(z, q_t, ka, vt, km, *, rows_per_step=2):
    bsz, s, _ = z.shape
    tq = MOBA_BLOCK
    gw = GROUP_WIDTH
    nblk = s // MOBA_BLOCK
    rps = rows_per_step
    assert SEL_FEAT0 + nblk <= HEAD_DIM and bsz % rps == 0
    whole = lambda a: pl.BlockSpec((rps,) + a.shape[1:], lambda b, i: (b,) + (0,) * (a.ndim - 1))
    return pl.pallas_call(
        _moba_kernel,
        grid=(bsz // rps, s // tq),
        in_specs=[pl.BlockSpec((N_HEADS, 2 * SLOPE_PIECES, tq), lambda b, i: (0, 0, 0)),
                  pl.BlockSpec((rps, gw, tq), lambda b, i: (b, 0, i)), whole(ka), whole(vt), whole(km),
                  pl.BlockSpec((rps, tq, gw), lambda b, i: (b, i, 0))],
        out_specs=pl.BlockSpec((rps, tq, gw), lambda b, i: (b, i, 0)),
        out_shape=jax.ShapeDtypeStruct((bsz, s, gw), BF16),
        scratch_shapes=_flash_scratch(rps * N_HEADS, tq) + [pltpu.VMEM((rps * N_HEADS, tq, tq), F32)] * 2,
        name="moba",
    )(jnp.asarray(_alibi_query_features(tq)), q_t, ka, vt, km, z)


def _cmp_kernel(xk_ref, xv_ref, wk1_ref, wv1_ref, pk_ref, pv_ref, bk1_ref, bv1_ref,
                wk2_ref, wv2_ref, bk2_ref, bv2_ref, gn_ref, ko_ref, vo_ref, wke_s, wve_s):
    half = xk_ref.shape[2]
    n = xk_ref.shape[1]
    hd = HEAD_DIM
    nh = NSA_CMP_HIDDEN

    @pl.when(pl.program_id(0) == 0)
    def _():
        for w_ref, we_ref in ((wk1_ref, wke_s), (wv1_ref, wve_s)):
            we_ref[...] = jnp.zeros(we_ref.shape, BF16)
            for tok in range(NSA_CMP_LEN):
                for g in range(KV_HEADS):
                    row = (tok * KV_HEADS + g) * hd
                    we_ref[row:row + hd, g * nh:(g + 1) * nh] = w_ref[tok * hd:(tok + 1) * hd, :]

    rows = xk_ref.shape[0]

    def hidden(x_ref, we_ref, pos_ref, b1_ref):
        x = x_ref[...].reshape(rows * n, half).astype(BF16)
        first = _dot(x, we_ref[:half, :])
        second = _dot(x, we_ref[half:, :])
        second = jnp.concatenate([pltpu.roll(second[r * n:(r + 1) * n], n - 1, 0) for r in range(rows)], axis=0)
        pos = jnp.broadcast_to(pos_ref[...], (8, 2 * half)).astype(BF16)
        ph = _dot(pos[:, :half], we_ref[:half, :]) + _dot(pos[:, half:], we_ref[half:, :])
        return _silu(first + second + ph[0:1, :] + b1_ref[...])

    hk = hidden(xk_ref, wke_s, pk_ref, bk1_ref).astype(BF16)
    hv = hidden(xv_ref, wve_s, pv_ref, bv1_ref).astype(BF16)
    ks, vs = [], []
    for g in range(KV_HEADS):
        ks.append(_rms(_dot(hk[:, g * nh:(g + 1) * nh], wk2_ref[...]) + bk2_ref[...], gn_ref[...]))
        vs.append(_dot(hv[:, g * nh:(g + 1) * nh], wv2_ref[...]) + bv2_ref[...])
    ko_ref[...] = jnp.concatenate(ks, axis=1).reshape(ko_ref.shape)
    vo_ref[...] = jnp.concatenate(vs, axis=1).reshape(vo_ref.shape)


def _compress(xk, xv, pos_k, pos_v, kw1, kb1, kw2, kb2, vw1, vb1, vw2, vb2, gn, *, rows_per_step=4):
    bsz, n, half = xk.shape
    hd = HEAD_DIM
    rps = rows_per_step
    assert bsz % rps == 0
    tile_pos = lambda p: jnp.tile(p[:, None, :], (1, KV_HEADS, 1)).reshape(1, 2 * half)
    tile_b = lambda b: jnp.tile(b.reshape(1, -1), (1, KV_HEADS))
    const = lambda shape: pl.BlockSpec(shape, lambda b: (0,) * len(shape))
    xs = pl.BlockSpec((rps, n, half), lambda b: (b, 0, 0))
    hw = KV_HEADS * NSA_CMP_HIDDEN
    out = pl.BlockSpec((rps, n, KV_HEADS * hd), lambda b: (b, 0, 0))
    expanded = pltpu.VMEM((2 * half, hw), BF16)
    return pl.pallas_call(
        _cmp_kernel,
        grid=(bsz // rps,),
        in_specs=[xs, xs, const(kw1.shape), const(vw1.shape),
                  const((1, 2 * half)), const((1, 2 * half)), const((1, hw)), const((1, hw)),
                  const((NSA_CMP_HIDDEN, hd)), const((NSA_CMP_HIDDEN, hd)), const((1, hd)), const((1, hd)),
                  const((1, hd))],
        out_specs=[out, out],
        out_shape=[jax.ShapeDtypeStruct((bsz, n, KV_HEADS * hd), F32)] * 2,
        scratch_shapes=[expanded, expanded],
        name="nsa_compress",
    )(xk, xv, kw1.astype(BF16), vw1.astype(BF16), tile_pos(pos_k), tile_pos(pos_v), tile_b(kb1), tile_b(vb1),
      kw2.astype(BF16), vw2.astype(BF16), kb2.reshape(1, hd), vb2.reshape(1, hd), gn.reshape(1, hd))


def _group_query(q_t, alibi_ref, g, gq_col):
    hd = HEAD_DIM
    heads = range(Q_PER_KV * g, Q_PER_KV * (g + 1))
    qn = jnp.concatenate([_norm_query_t(q_t[h * hd:(h + 1) * hd], gq_col) for h in heads], axis=1)
    alibi = jnp.concatenate([alibi_ref[h] for h in heads], axis=1)
    return qn, alibi


def _augment_keys(k, gk_ref, dst_ref, row=0, sel_shift=None):
    assert KV_HEADS == 2
    pos = _iota(k.shape, 0)
    dst_ref[2 * row], dst_ref[2 * row + 1] = _augment_pair_keys(_pair_rms(k, gk_ref[...]), pos, sel_shift)


def _chunk_scores(ka_ref, vt_ref, g, c, qa, tq):
    off = pl.multiple_of(c * tq, tq)
    return _dot(ka_ref[g, pl.ds(off, tq), :], qa), vt_ref[c, g * HEAD_DIM:(g + 1) * HEAD_DIM, :]


def _window_first_chunk(ka_ref, vt_ref, g, far, qa, tq, edge_cap):
    sc, v_t = _chunk_scores(ka_ref, vt_ref, g, jnp.maximum(far, 0), qa, tq)
    return jnp.minimum(sc, jnp.minimum(edge_cap, jnp.where(far >= 0, BIG, -BIG))), v_t


def _regroup_heads(o_t, tq):
    return jnp.concatenate([o_t[:, k * tq:(k + 1) * tq] for k in range(Q_PER_KV)], axis=0)


def _nsa_kernel(alibi_ref, q_ref, kc_ref, vc_ref, ks_ref, vs_ref, kw_ref, vw_ref, gate_ref, z_ref,
                gq_ref, gks_ref, gkw_ref, o_ref, kca_s, vct_s, ksa_s, vst_s, kwa_s, vwt_s, m_s, acc_s,
                sc0_s, sc1_s):
    i = pl.program_id(1)
    rows = range(q_ref.shape[0])
    tq = q_ref.shape[1]
    s = ks_ref.shape[1]
    hd = HEAD_DIM
    ncmp = kc_ref.shape[1]
    nsb = s // NSA_SEL_BLOCK

    @pl.when(i == 0)
    def _():
        cend = _iota((ncmp, LANES), 0) * NSA_CMP_STRIDE + (NSA_CMP_LEN - 1)
        for r in rows:
            _augment_keys(ks_ref[r], gks_ref, ksa_s, r, NSA_SEL_BLOCK.bit_length() - 1)
            _augment_keys(kw_ref[r], gkw_ref, kwa_s, r)
            _store_chunked_t(vs_ref[r], vst_s.at[r], tq)
            _store_chunked_t(vw_ref[r], vwt_s.at[r], tq)
            kca_s[2 * r], kca_s[2 * r + 1] = _augment_pair_keys(kc_ref[r], cend)
            vct_s[r] = vc_ref[r].T.astype(BF16)

    t0 = i * tq
    diag_cap = _causal_cap(tq, tq, Q_PER_KV)
    edge_cap = _window_edge_cap(tq, tq, Q_PER_KV)
    cend =_iota((ncmp, tq), 0) * NSA_CMP_STRIDE + (NSA_CMP_LEN - 1)
    seen_cap = _tile_lanes(jnp.where(cend <= t0 + _iota((ncmp, tq), 1), BIG, -BIG), Q_PER_KV)
    cstart = _iota((nsb, ncmp), 1) * NSA_CMP_STRIDE
    bstart = _iota((nsb, ncmp), 0) * NSA_SEL_BLOCK
    overlap_t = jnp.where((cstart < bstart + NSA_SEL_BLOCK) & (cstart + NSA_CMP_LEN > bstart), 1.0, 0.0)
    blk = _iota((nsb, tq), 0)
    cur = (t0 + _iota((nsb, tq), 1)) >> (NSA_SEL_BLOCK.bit_length() - 1)
    forced = (blk == 0) | (blk == cur) | (blk == cur - 1)
    groups = range(KV_HEADS)
    far = i - NSA_WINDOW // tq
    lo = jnp.maximum(far + 1, 0)
    first_win = jnp.maximum(far, 0)

    state = (m_s, acc_s)
    kv = [(r, g) for r in rows for g in groups]
    sel_slots = tuple(range(len(kv)))
    win_slots = tuple(len(kv) + n for n in range(len(kv)))
    _flash_reset(state)
    q_ts = [q_ref[r].T for r in rows]
    queries = [_group_query(q_ts[r], alibi_ref, g, gq_ref[...]) for r, g in kv]
    qa_win = [_augment_query(qn, alibi, flipped=g == 1) for (r, g), (qn, alibi) in zip(kv, queries)]
    qa_sel = []

    def scores(slot, c):
        n = slot % len(kv)
        ka_ref, qa = (ksa_s, qa_sel) if slot in sel_slots else (kwa_s, qa_win)
        return _dot(ka_ref[n, pl.ds(pl.multiple_of(c * tq, tq), tq), :], qa[n])

    def values(slot, c):
        r, g = kv[slot % len(kv)]
        return (vst_s if slot in sel_slots else vwt_s)[r, c, g * hd:(g + 1) * hd, :]

    cmp_sc = [_dot(kca_s[n], qa_win[n]) for n in range(len(kv))]
    sc_win_first = [scores(slot, first_win) for slot in win_slots]
    o_cmp = []
    for n, (r, g) in enumerate(kv):
        qn, alibi = queries[n]
        sc = jnp.minimum(cmp_sc[n], seen_cap)
        top = jnp.max(sc, axis=0, keepdims=True)
        e = jnp.exp2(sc - top)
        inv = jnp.where(top > -0.5 * BIG, 1.0 / jnp.maximum(jnp.sum(e, axis=0, keepdims=True), TINY), 0.0)
        o_cmp.append(_dot(vct_s[r, g * hd:(g + 1) * hd, :], e.astype(BF16)) * inv)

        psum = e[:, 0:tq] * inv[:, 0:tq]
        for k in range(1, Q_PER_KV):
            psum = psum + e[:, k * tq:(k + 1) * tq] * inv[:, k * tq:(k + 1) * tq]
        imp = _dot_fine(overlap_t, psum, a_is_bf16_exact=True)
        imp = jnp.where(blk <= cur, jnp.where(forced, NSA_FORCE, imp), NEG)
        chosen = _rank_select(imp, blk, nsb, min(NSA_TOPN, nsb))
        sel_bias = _tile_lanes(jnp.where(chosen, 0.0, -BIG), Q_PER_KV)
        qa_sel.append(_augment_query(qn, alibi, sel_bias, flipped=g == 1))

    bufs = (sc0_s, sc1_s)
    for slot in sel_slots:
        sc0_s[slot] = scores(slot, 0)
    for slot, sc in zip(win_slots, sc_win_first):
        sc0_s[slot] = sc
        sc1_s[slot] = sc

    def body(slots, cap, cap_slots, triangular, c, carry):
        _flash_update_pipelined(state, bufs, slots, c, lambda slot: scores(slot, c + 1),
                                lambda slot: values(slot, c), cap, cap_slots, triangular)
        return carry

    both_slots = sel_slots + win_slots
    edge_shape = (1, _half_lanes(tq, Q_PER_KV, upper=False))
    diag_shape = (0, _half_lanes(tq, Q_PER_KV, upper=True))
    lax.fori_loop(0, first_win, functools.partial(body, sel_slots, None, None, None), 0)
    lax.fori_loop(first_win, lo, functools.partial(body, both_slots, edge_cap, win_slots, edge_shape), 0)
    lax.fori_loop(lo, i, functools.partial(body, both_slots, None, None, None), 0)
    _flash_update_pipelined(state, bufs, both_slots, i, None, lambda slot: values(slot, i), diag_cap,
                            triangular=diag_shape)

    for r in rows:
        gates_t = _sigmoid(gate_ref[r]).T
        outs = []
        for g in groups:
            def gate_row(branch):
                parts = [gates_t[branch * N_HEADS + h:branch * N_HEADS + h + 1, :]
                         for h in range(Q_PER_KV * g, Q_PER_KV * (g + 1))]
                return jnp.concatenate(parts, axis=1)

            n = KV_HEADS * r + g
            o_slc = _flash_result(state, sel_slots[n])
            o_win = _flash_result(state, win_slots[n])
            o = gate_row(0) * o_cmp[n] + gate_row(1) * o_slc + gate_row(2) * o_win
            outs.append(_regroup_heads(o, tq))
        y = jnp.concatenate(outs, axis=0).T * _silu(z_ref[r])
        o_ref[r] = y.astype(o_ref.dtype)


def _nsa(proj, kcn, vc, gq, gks, gkw, *, tq=256, rows_per_step=2):
    bsz, s, _ = proj.shape
    hd = HEAD_DIM
    gw = GROUP_WIDTH
    ncmp = kcn.shape[1]
    rps = rows_per_step
    chains = 2 * KV_HEADS * rps
    assert SEL_FEAT0 + s // NSA_SEL_BLOCK <= hd and ncmp == LANES and bsz % rps == 0
    wide = lambda c0: pl.BlockSpec((rps, tq, gw), lambda b, i: (b, i, c0 // 4))
    full = lambda c0: pl.BlockSpec((rps, s, LANES), lambda b, i: (b, 0, c0))
    cmp = pl.BlockSpec((rps, ncmp, LANES), lambda b, i: (b, 0, 0))
    row = pl.BlockSpec((1, hd), lambda b, i: (0, 0))
    ka = pltpu.VMEM((KV_HEADS * rps, s, LANES), BF16)
    vt = pltpu.VMEM((rps, s // tq, LANES, tq), BF16)
    return pl.pallas_call(
        _nsa_kernel,
        grid=(bsz // rps, s // tq),
        in_specs=[pl.BlockSpec((N_HEADS, 2 * SLOPE_PIECES, tq), lambda b, i: (0, 0, 0)),
                  wide(ODD_QC), cmp, cmp, full(ODD_KS), full(ODD_VS), full(ODD_KW), full(ODD_VW),
                  pl.BlockSpec((rps, tq, LANES), lambda b, i: (b, i, ODD_GC)), wide(ODD_ZC),
                  pl.BlockSpec((hd, 1), lambda b, i: (0, 0)), row, row],
        out_specs=pl.BlockSpec((rps, tq, gw), lambda b, i: (b, i, 0)),
        out_shape=jax.ShapeDtypeStruct((bsz, s, gw), BF16),
        scratch_shapes=[pltpu.VMEM((KV_HEADS * rps, ncmp, LANES), BF16), pltpu.VMEM((rps, LANES, ncmp), BF16),
                        ka, vt, ka, vt] + _flash_scratch(chains, Q_PER_KV * tq)
                       + [pltpu.VMEM((chains, tq, Q_PER_KV * tq), F32)] * 2,
        name="nsa",
    )(jnp.asarray(_alibi_query_features(tq)), proj, kcn, vc, proj, proj, proj, proj, proj, proj,
      gq.reshape(hd, 1), gks.reshape(1, hd), gkw.reshape(1, hd))


def _swa_kernel(sinks_ref, alibi_ref, q_ref, k_ref, v_ref, z_ref, gq_ref, gk_ref, x_ref, yc_ref, w_ref, o_ref,
                ka_s, vt_s, wb_s):
    step = pl.program_id(1)
    tq = SWA_WINDOW
    tiles = q_ref.shape[1] // tq
    s = k_ref.shape[1]

    @pl.when((pl.program_id(0) == 0) & (step == 0))
    def _():
        wb_s[...] = w_ref[...].astype(BF16)

    @pl.when(step == 0)
    def _():
        _augment_keys(k_ref[0], gk_ref, ka_s)
        _store_chunked_t(v_ref[0], vt_s, tq)

    q_t = q_ref[0].T
    diag_cap = _causal_cap(tq, tq, Q_PER_KV)
    edge_cap = _window_edge_cap(tq, tq, Q_PER_KV)
    chains = [(u, g) for u in range(tiles) for g in range(KV_HEADS)]
    qas = [_augment_query(*_group_query(q_t[:, u * tq:(u + 1) * tq], alibi_ref, g, gq_ref[...]), flipped=g == 1)
           for u, g in chains]
    first = [_window_first_chunk(ka_s, vt_s, g, step * tiles + u - 1, qa, tq, edge_cap)
             for (u, g), qa in zip(chains, qas)]
    own = [_chunk_scores(ka_s, vt_s, g, step * tiles + u, qa, tq) for (u, g), qa in zip(chains, qas)]
    carries = _flash_steps([_flash_init(Q_PER_KV * tq) for _ in chains], first)
    carries = _flash_steps(carries, own, diag_cap)
    outs = []
    for (u, g), carry in zip(chains, carries):
        m, acc, l = _flash_split(carry)
        tpos = ((step * tiles + u) * tq + _iota((1, tq), 1)).astype(F32)
        sink = jnp.concatenate([sinks_ref[h] * LOG2E + (SLOPES[h] * LOG2E) * tpos
                                for h in range(Q_PER_KV * g, Q_PER_KV * (g + 1))], axis=1)
        mf = jnp.maximum(m, sink)
        alpha = jnp.exp2(m - mf)
        o = acc * alpha / jnp.maximum(l * alpha + jnp.exp2(sink - mf), TINY)
        outs.append(_regroup_heads(o, tq))
    o_t = jnp.concatenate([jnp.concatenate(outs[u * KV_HEADS:(u + 1) * KV_HEADS], axis=0) for u in range(tiles)],
                          axis=1)
    y = (o_t.T * _silu(z_ref[0])).astype(BF16)
    o_ref[0] = x_ref[0] + _dot(jnp.concatenate([yc_ref[0], y], axis=1), wb_s[...])


def _swa(proj, sinks, gq, gk, x, y_c, w_out, *, tiles_per_step=8):
    bsz, s, _ = proj.shape
    d = x.shape[-1]
    hd = HEAD_DIM
    gw = GROUP_WIDTH
    tq = SWA_WINDOW
    rows = tiles_per_step * tq
    wide = lambda c0: pl.BlockSpec((1, rows, gw), lambda b, i: (b, i, c0 // 4))
    full = lambda c0: pl.BlockSpec((1, s, LANES), lambda b, i: (b, 0, c0))
    return pl.pallas_call(
        _swa_kernel,
        grid=(bsz, s // rows),
        in_specs=[pl.BlockSpec(memory_space=pltpu.SMEM),
                  pl.BlockSpec((N_HEADS, 2 * SLOPE_PIECES, tq), lambda b, i: (0, 0, 0)),
                  wide(ODD_QD), full(ODD_KD), full(ODD_VD), wide(ODD_ZD),
                  pl.BlockSpec((hd, 1), lambda b, i: (0, 0)), pl.BlockSpec((1, hd), lambda b, i: (0, 0)),
                  pl.BlockSpec((1, rows, d), lambda b, i: (b, i, 0)),
                  pl.BlockSpec((1, rows, gw), lambda b, i: (b, i, 0)),
                  pl.BlockSpec(w_out.shape, lambda b, i: (0, 0), pipeline_mode=pl.Buffered(1))],
        out_specs=pl.BlockSpec((1, rows, d), lambda b, i: (b, i, 0)),
        out_shape=jax.ShapeDtypeStruct((bsz, s, d), F32),
        scratch_shapes=[pltpu.VMEM((KV_HEADS, s, LANES), BF16), pltpu.VMEM((s // tq, LANES, tq), BF16),
                        pltpu.VMEM(w_out.shape, BF16)],
        name="swa_outproj",
    )(sinks.astype(F32), jnp.asarray(_alibi_query_features(tq)), proj, proj, proj, proj,
      gq.reshape(hd, 1), gk.reshape(1, hd), x, y_c, w_out)


def _odd_weight_moves():
    gw, kw, ng = GROUP_WIDTH, KV_HEADS * HEAD_DIM, 3 * N_HEADS
    sizes = [gw, kw, kw, kw, kw, kw, kw, ng, gw, gw, kw, kw, gw]
    starts = np.concatenate([[0], np.cumsum(sizes)]).tolist()
    order = [0, 8, 9, 12, 1, 2, 3, 4, 5, 6, 10, 11, 7]
    moves, dst = [], 0
    for k in order:
        moves.append((starts[k], sizes[k], dst))
        dst += sizes[k]
    assert dst + LANES - ng == ODD_COLS
    return moves


def kernel(x, norm_g, w_out, e_w_in, a_conv_w, a_conv_b, a_ln_g, a_ln_b, b_qnorm_g, b_knorm_g, o_w_in, c_qnorm_g, c_knorm_cmp_g, c_knorm_slc_g, c_knorm_win_g, c_pos_k, c_pos_v, c_k_w1, c_k_b1, c_k_w2, c_k_b2, c_v_w1, c_v_b1, c_v_w2, c_v_b2, d_qnorm_g, d_knorm_g, d_sinks):
    bsz, s, d = x.shape
    m = bsz * s
    assert s % MOBA_BLOCK == 0 and d == 2 * GROUP_WIDTH
    x2 = x.reshape(m, d)

    y_a, z_b, q_t, ka, vt, km = _even_layer_front(
        x2, norm_g[0], e_w_in[0], b_qnorm_g[0], b_knorm_g[0],
        a_conv_w[0], a_conv_b[0], a_ln_g[0], a_ln_b[0], bsz=bsz)
    y_b = _moba(z_b.reshape(bsz, s, -1), q_t, ka, vt, km)

    x2, proj, k_cmp, v_cmp = _layer_boundary(
        x2, y_a, y_b.reshape(m, -1), w_out[0].astype(BF16), norm_g[1], o_w_in[0].astype(BF16),
        e=ODD_COLS, moves=_odd_weight_moves(), chunk=ODD_COLS,
        copies=((ODD_KC * LANES, LANES), (ODD_VC * LANES, LANES)), group=NSA_CMP_STRIDE)
    proj = proj.reshape(bsz, s, ODD_COLS)
    rows16 = lambda t: t.reshape(bsz, s // NSA_CMP_STRIDE, NSA_CMP_STRIDE * LANES)
    kcn, vc = _compress(rows16(k_cmp), rows16(v_cmp), c_pos_k[0], c_pos_v[0], c_k_w1[0], c_k_b1[0],
                        c_k_w2[0], c_k_b2[0], c_v_w1[0], c_v_b1[0], c_v_w2[0], c_v_b2[0], c_knorm_cmp_g[0])
    y_c = _nsa(proj, kcn, vc, c_qnorm_g[0], c_knorm_slc_g[0], c_knorm_win_g[0])
    return _swa(proj, d_sinks[0], d_qnorm_g[0], d_knorm_g[0], x2.reshape(bsz, s, d), y_c, w_out[1])
```
